```python
import math
import jax, jax.numpy as jnp
from jax import lax
import numpy as np

D_MODEL = 1024
BATCH = 8
SEQ = 4096
DEPTH = 4

N_MIXERS = 2
N_GLA_LAYERS = (DEPTH + 1) // 2
N_ATTN_LAYERS = DEPTH // 2

GRID_W = 64

GLA_HEADS = 4
GLA_KEY_DIM = D_MODEL // 2
GLA_VAL_DIM = D_MODEL
GLA_DK = GLA_KEY_DIM // GLA_HEADS
GLA_DV = GLA_VAL_DIM // GLA_HEADS
GLA_GATE_RANK = 16
GLA_GATE_NORMALIZER = 16.0
GLA_CHUNK = 64
GLA_IN_DIM = 2 * GLA_KEY_DIM + 2 * GLA_VAL_DIM + 2 * GLA_GATE_RANK

ATTN_HEAD_DIM = 128
ATTN_Q_HEADS = D_MODEL // ATTN_HEAD_DIM
ATTN_KV_HEADS = 2
ATTN_GROUP = ATTN_Q_HEADS // ATTN_KV_HEADS
ATTN_QKV_DIM = (ATTN_Q_HEADS + 2 * ATTN_KV_HEADS) * ATTN_HEAD_DIM
QUERY_BLOCK = 128
ROPE_THETA = 10000.0
ROPE_PAIRS_PER_AXIS = ATTN_HEAD_DIM // 4

D_FF = 2816
CONV_WIDTH = 3

NORM_EPS = 1e-6

kernel_name = 'hybrid_gla_gqa2drope_convffn_encoder'


def rmsnorm(x, w):
    xf = x.astype(jnp.float32)
    y = xf * lax.rsqrt(jnp.mean(xf * xf, axis=-1, keepdims=True) + NORM_EPS)
    return (y * w.astype(jnp.float32)).astype(x.dtype)


def gla_chunked(q, k, v, log_a, strict):
    B, H, S, dk = q.shape
    dv = v.shape[-1]
    n = S // GLA_CHUNK
    c = lambda t: t.reshape(B, H, n, GLA_CHUNK, t.shape[-1])
    q, k, v, log_a = c(q), c(k), c(v), c(log_a)
    b = jnp.cumsum(log_a, axis=3)
    b_end = b[:, :, :, -1:, :]
    q_dec = q * jnp.exp(b)
    k_inv = k * jnp.exp(-b)
    mask = jnp.tril(jnp.ones((GLA_CHUNK, GLA_CHUNK), dtype=bool), k=-1 if strict else 0)
    att = jnp.where(mask, jnp.einsum('bhncd,bhnsd->bhncs', q_dec, k_inv), 0.0)
    o_intra = jnp.einsum('bhncs,bhnsv->bhncv', att, v)
    kv_chunk = jnp.einsum('bhncd,bhncv->bhndv', k * jnp.exp(b_end - b), v)
    decay_chunk = jnp.exp(b_end[:, :, :, 0, :])

    def step(state, inp):
        d, kv_c = inp
        return d[..., None] * state + kv_c, state

    _, s_prev = lax.scan(step, jnp.zeros((B, H, dk, dv), jnp.float32),
                         (jnp.moveaxis(decay_chunk, 2, 0), jnp.moveaxis(kv_chunk, 2, 0)))
    s_prev = jnp.moveaxis(s_prev, 0, 2)
    o_inter = jnp.einsum('bhncd,bhndv->bhncv', q_dec, s_prev)
    return (o_intra + o_inter).reshape(B, H, S, dv)


def gla_mixer(h, w_in, w_gate_up_f, b_gate_f, w_gate_up_b, b_gate_b, norm_w, w_out):
    B, S, _ = h.shape
    f32 = jnp.float32
    proj = h @ w_in
    q, k, v, g, r = jnp.split(
        proj, [GLA_KEY_DIM, 2 * GLA_KEY_DIM, 2 * GLA_KEY_DIM + GLA_VAL_DIM,
               2 * GLA_KEY_DIM + 2 * GLA_VAL_DIM], axis=-1)
    r_f, r_b = jnp.split(r, 2, axis=-1)

    def heads(t, d):
        return t.reshape(B, S, GLA_HEADS, d).transpose(0, 2, 1, 3).astype(f32)

    def log_decay(r_dir, w_up, b_up):
        logits = (r_dir @ w_up + b_up).astype(f32)
        return heads(jax.nn.log_sigmoid(logits) / GLA_GATE_NORMALIZER, GLA_DK)

    q = heads(q, GLA_DK) * (GLA_DK ** -0.5)
    k = heads(k, GLA_DK)
    v = heads(v, GLA_DV)
    la_f = log_decay(r_f, w_gate_up_f, b_gate_f)
    la_b = log_decay(r_b, w_gate_up_b, b_gate_b)
    flip = lambda t: jnp.flip(t, axis=2)
    o_f = gla_chunked(q, k, v, la_f, strict=False)
    o_b = flip(gla_chunked(flip(q), flip(k), flip(v), flip(la_b), strict=True))
    o = (o_f + o_b).transpose(0, 2, 1, 3)
    o = rmsnorm(o, norm_w) * jax.nn.silu(g.astype(f32).reshape(B, S, GLA_HEADS, GLA_DV))
    return o.reshape(B, S, GLA_VAL_DIM).astype(h.dtype) @ w_out


def apply_rope(x, cos, sin):
    half = x.shape[-1] // 2
    x1, x2 = x[..., :half], x[..., half:]
    return jnp.concatenate([x1 * cos - x2 * sin, x1 * sin + x2 * cos], axis=-1)


def attn_mixer(h, w_qkv, q_norm, k_norm, w_out, cos, sin):
    B, S, _ = h.shape
    proj = h @ w_qkv
    q, k, v = jnp.split(proj, [ATTN_Q_HEADS * ATTN_HEAD_DIM,
                               (ATTN_Q_HEADS + ATTN_KV_HEADS) * ATTN_HEAD_DIM], axis=-1)
    q = q.reshape(B, S, ATTN_Q_HEADS, ATTN_HEAD_DIM)
    k = k.reshape(B, S, ATTN_KV_HEADS, ATTN_HEAD_DIM)
    v = v.reshape(B, S, ATTN_KV_HEADS, ATTN_HEAD_DIM)
    q = apply_rope(rmsnorm(q, q_norm).astype(jnp.float32), cos, sin).astype(h.dtype)
    k = apply_rope(rmsnorm(k, k_norm).astype(jnp.float32), cos, sin).astype(h.dtype)
    n_blk = S // QUERY_BLOCK
    qb = q.reshape(B, S, ATTN_KV_HEADS, ATTN_GROUP, ATTN_HEAD_DIM).transpose(0, 2, 3, 1, 4)
    qb = qb.reshape(B, ATTN_KV_HEADS, ATTN_GROUP, n_blk, QUERY_BLOCK, ATTN_HEAD_DIM)
    qb = qb.transpose(3, 0, 1, 2, 4, 5)
    k = k.transpose(0, 2, 1, 3)
    v = v.transpose(0, 2, 1, 3)
    scale = ATTN_HEAD_DIM ** -0.5

    def block(q_blk):
        s = jnp.einsum('bkgqd,bksd->bkgqs', q_blk, k).astype(jnp.float32) * scale
        p = jax.nn.softmax(s, axis=-1).astype(v.dtype)
        return jnp.einsum('bkgqs,bksd->bkgqd', p, v)

    o = lax.map(block, qb)
    o = o.transpose(1, 0, 4, 2, 3, 5).reshape(B, S, ATTN_Q_HEADS * ATTN_HEAD_DIM)
    return o @ w_out


def conv_ffn(h, w_up, w_conv, b_conv, w_down):
    u = h @ w_up
    u = lax.conv_general_dilated(
        u, w_conv[:, None, :], window_strides=(1,), padding=[(1, 1)],
        dimension_numbers=('NWC', 'WIO', 'NWC'), feature_group_count=u.shape[-1]) + b_conv
    val, gate = jnp.split(u, 2, axis=-1)
    return (jax.nn.silu(gate) * val) @ w_down


def _fwd_setup_inputs(seed: int = 0) -> dict:
    key = jax.random.key(seed)
    ks = jax.random.split(key, 20)
    nrm = lambda k, shape, s: jax.random.normal(k, shape, jnp.float32) * s
    NG, NA = N_GLA_LAYERS, N_ATTN_LAYERS
    return {
        'x': nrm(ks[0], (BATCH, SEQ, D_MODEL), 1.0),
        'norm_mix': 1.0 + nrm(ks[1], (DEPTH, D_MODEL), 0.01),
        'norm_ffn': 1.0 + nrm(ks[2], (DEPTH, D_MODEL), 0.01),
        'gla_w_in': nrm(ks[3], (NG, D_MODEL, GLA_IN_DIM), D_MODEL ** -0.5),
        'gla_w_gate_up_f': nrm(ks[4], (NG, GLA_GATE_RANK, GLA_KEY_DIM), GLA_GATE_RANK ** -0.5),
        'gla_b_gate_f': nrm(ks[5], (NG, GLA_KEY_DIM), 0.1),
        'gla_w_gate_up_b': nrm(ks[6], (NG, GLA_GATE_RANK, GLA_KEY_DIM), GLA_GATE_RANK ** -0.5),
        'gla_b_gate_b': nrm(ks[7], (NG, GLA_KEY_DIM), 0.1),
        'gla_norm': 1.0 + nrm(ks[8], (NG, GLA_DV), 0.01),
        'gla_w_out': nrm(ks[9], (NG, GLA_VAL_DIM, D_MODEL), GLA_VAL_DIM ** -0.5),
        'attn_w_qkv': nrm(ks[10], (NA, D_MODEL, ATTN_QKV_DIM), D_MODEL ** -0.5),
        'attn_q_norm': 1.0 + nrm(ks[11], (NA, ATTN_HEAD_DIM), 0.01),
        'attn_k_norm': 1.0 + nrm(ks[12], (NA, ATTN_HEAD_DIM), 0.01),
        'attn_w_out': nrm(ks[13], (NA, ATTN_Q_HEADS * ATTN_HEAD_DIM, D_MODEL),
                          (ATTN_Q_HEADS * ATTN_HEAD_DIM) ** -0.5),
        'ffn_w_up': nrm(ks[14], (DEPTH, D_MODEL, 2 * D_FF), D_MODEL ** -0.5),
        'ffn_w_conv': nrm(ks[15], (DEPTH, CONV_WIDTH, 2 * D_FF), CONV_WIDTH ** -0.5),
        'ffn_b_conv': nrm(ks[16], (DEPTH, 2 * D_FF), 0.01),
        'ffn_w_down': nrm(ks[17], (DEPTH, D_FF, D_MODEL), D_FF ** -0.5),
    }


def _fwd_reference(x, norm_mix, norm_ffn, gla_w_in, gla_w_gate_up_f, gla_b_gate_f,
              gla_w_gate_up_b, gla_b_gate_b, gla_norm, gla_w_out,
              attn_w_qkv, attn_q_norm, attn_k_norm, attn_w_out,
              ffn_w_up, ffn_w_conv, ffn_b_conv, ffn_w_down):
    S = x.shape[1]
    rows = S // GRID_W
    f32 = jnp.float32
    row_idx = jnp.repeat(jnp.arange(rows, dtype=f32), GRID_W)
    col_idx = jnp.tile(jnp.arange(GRID_W, dtype=f32), rows)
    inv_freq = ROPE_THETA ** (-jnp.arange(ROPE_PAIRS_PER_AXIS, dtype=f32) / ROPE_PAIRS_PER_AXIS)
    ang = jnp.concatenate([row_idx[:, None] * inv_freq, col_idx[:, None] * inv_freq], axis=-1)
    cos = jnp.cos(ang)[None, :, None, :]
    sin = jnp.sin(ang)[None, :, None, :]

    for i in range(DEPTH):
        h = rmsnorm(x, norm_mix[i])
        j = i // N_MIXERS
        if i % N_MIXERS == 0:
            x = x + gla_mixer(h, gla_w_in[j], gla_w_gate_up_f[j], gla_b_gate_f[j],
                              gla_w_gate_up_b[j], gla_b_gate_b[j], gla_norm[j], gla_w_out[j])
        else:
            x = x + attn_mixer(h, attn_w_qkv[j], attn_q_norm[j], attn_k_norm[j],
                               attn_w_out[j], cos, sin)
        x = x + conv_ffn(rmsnorm(x, norm_ffn[i]), ffn_w_up[i], ffn_w_conv[i],
                         ffn_b_conv[i], ffn_w_down[i])
    return x


import jax as _jax
import jax.numpy as _jnp

TWIN_FORMAT = 'train_step'
FWD_PARAMS = ['x', 'norm_mix', 'norm_ffn', 'gla_w_in', 'gla_w_gate_up_f', 'gla_b_gate_f', 'gla_w_gate_up_b', 'gla_b_gate_b', 'gla_norm', 'gla_w_out', 'attn_w_qkv', 'attn_q_norm', 'attn_k_norm', 'attn_w_out', 'ffn_w_up', 'ffn_w_conv', 'ffn_b_conv', 'ffn_w_down']
TWIN_WEIGHTS = ['norm_mix', 'norm_ffn', 'gla_w_in', 'gla_w_gate_up_f', 'gla_b_gate_f', 'gla_w_gate_up_b', 'gla_b_gate_b', 'gla_norm', 'gla_w_out', 'attn_w_qkv', 'attn_q_norm', 'attn_k_norm', 'attn_w_out', 'ffn_w_up', 'ffn_w_conv', 'ffn_b_conv', 'ffn_w_down']
TWIN_DIFF_INPUT = 'x'
TWIN_INPUTS = ['x', 'norm_mix', 'norm_ffn', 'gla_w_in', 'gla_w_gate_up_f', 'gla_b_gate_f', 'gla_w_gate_up_b', 'gla_b_gate_b', 'gla_norm', 'gla_w_out', 'attn_w_qkv', 'attn_q_norm', 'attn_k_norm', 'attn_w_out', 'ffn_w_up', 'ffn_w_conv', 'ffn_b_conv', 'ffn_w_down', 'loss_target', 'm_norm_mix', 'm_norm_ffn', 'm_gla_w_in', 'm_gla_w_gate_up_f', 'm_gla_b_gate_f', 'm_gla_w_gate_up_b', 'm_gla_b_gate_b', 'm_gla_norm', 'm_gla_w_out', 'm_attn_w_qkv', 'm_attn_q_norm', 'm_attn_k_norm', 'm_attn_w_out', 'm_ffn_w_up', 'm_ffn_w_conv', 'm_ffn_b_conv', 'm_ffn_w_down', 'v_norm_mix', 'v_norm_ffn', 'v_gla_w_in', 'v_gla_w_gate_up_f', 'v_gla_b_gate_f', 'v_gla_w_gate_up_b', 'v_gla_b_gate_b', 'v_gla_norm', 'v_gla_w_out', 'v_attn_w_qkv', 'v_attn_q_norm', 'v_attn_k_norm', 'v_attn_w_out', 'v_ffn_w_up', 'v_ffn_w_conv', 'v_ffn_b_conv', 'v_ffn_w_down']
TWIN_OUTPUTS = ['loss', 'grad_x', 'grad_norm_mix', 'grad_norm_ffn', 'grad_gla_w_in', 'grad_gla_w_gate_up_f', 'grad_gla_b_gate_f', 'grad_gla_w_gate_up_b', 'grad_gla_b_gate_b', 'grad_gla_norm', 'grad_gla_w_out', 'grad_attn_w_qkv', 'grad_attn_q_norm', 'grad_attn_k_norm', 'grad_attn_w_out', 'grad_ffn_w_up', 'grad_ffn_w_conv', 'grad_ffn_b_conv', 'grad_ffn_w_down', 'delta_norm_mix', 'delta_norm_ffn', 'delta_gla_w_in', 'delta_gla_w_gate_up_f', 'delta_gla_b_gate_f', 'delta_gla_w_gate_up_b', 'delta_gla_b_gate_b', 'delta_gla_norm', 'delta_gla_w_out', 'delta_attn_w_qkv', 'delta_attn_q_norm', 'delta_attn_k_norm', 'delta_attn_w_out', 'delta_ffn_w_up', 'delta_ffn_w_conv', 'delta_ffn_b_conv', 'delta_ffn_w_down', 'new_m_norm_mix', 'new_m_norm_ffn', 'new_m_gla_w_in', 'new_m_gla_w_gate_up_f', 'new_m_gla_b_gate_f', 'new_m_gla_w_gate_up_b', 'new_m_gla_b_gate_b', 'new_m_gla_norm', 'new_m_gla_w_out', 'new_m_attn_w_qkv', 'new_m_attn_q_norm', 'new_m_attn_k_norm', 'new_m_attn_w_out', 'new_m_ffn_w_up', 'new_m_ffn_w_conv', 'new_m_ffn_b_conv', 'new_m_ffn_w_down', 'new_v_norm_mix', 'new_v_norm_ffn', 'new_v_gla_w_in', 'new_v_gla_w_gate_up_f', 'new_v_gla_b_gate_f', 'new_v_gla_w_gate_up_b', 'new_v_gla_b_gate_b', 'new_v_gla_norm', 'new_v_gla_w_out', 'new_v_attn_w_qkv', 'new_v_attn_q_norm', 'new_v_attn_k_norm', 'new_v_attn_w_out', 'new_v_ffn_w_up', 'new_v_ffn_w_conv', 'new_v_ffn_b_conv', 'new_v_ffn_w_down']
TWIN_LEAF_KINDS = {'loss': 'loss', 'grad_x': 'grad_x', 'grad_norm_mix': 'grad_w', 'grad_norm_ffn': 'grad_w', 'grad_gla_w_in': 'grad_w', 'grad_gla_w_gate_up_f': 'grad_w', 'grad_gla_b_gate_f': 'grad_w', 'grad_gla_w_gate_up_b': 'grad_w', 'grad_gla_b_gate_b': 'grad_w', 'grad_gla_norm': 'grad_w', 'grad_gla_w_out': 'grad_w', 'grad_attn_w_qkv': 'grad_w', 'grad_attn_q_norm': 'grad_w', 'grad_attn_k_norm': 'grad_w', 'grad_attn_w_out': 'grad_w', 'grad_ffn_w_up': 'grad_w', 'grad_ffn_w_conv': 'grad_w', 'grad_ffn_b_conv': 'grad_w', 'grad_ffn_w_down': 'grad_w', 'delta_norm_mix': 'delta_w', 'delta_norm_ffn': 'delta_w', 'delta_gla_w_in': 'delta_w', 'delta_gla_w_gate_up_f': 'delta_w', 'delta_gla_b_gate_f': 'delta_w', 'delta_gla_w_gate_up_b': 'delta_w', 'delta_gla_b_gate_b': 'delta_w', 'delta_gla_norm': 'delta_w', 'delta_gla_w_out': 'delta_w', 'delta_attn_w_qkv': 'delta_w', 'delta_attn_q_norm': 'delta_w', 'delta_attn_k_norm': 'delta_w', 'delta_attn_w_out': 'delta_w', 'delta_ffn_w_up': 'delta_w', 'delta_ffn_w_conv': 'delta_w', 'delta_ffn_b_conv': 'delta_w', 'delta_ffn_w_down': 'delta_w', 'new_m_norm_mix': 'new_m', 'new_m_norm_ffn': 'new_m', 'new_m_gla_w_in': 'new_m', 'new_m_gla_w_gate_up_f': 'new_m', 'new_m_gla_b_gate_f': 'new_m', 'new_m_gla_w_gate_up_b': 'new_m', 'new_m_gla_b_gate_b': 'new_m', 'new_m_gla_norm': 'new_m', 'new_m_gla_w_out': 'new_m', 'new_m_attn_w_qkv': 'new_m', 'new_m_attn_q_norm': 'new_m', 'new_m_attn_k_norm': 'new_m', 'new_m_attn_w_out': 'new_m', 'new_m_ffn_w_up': 'new_m', 'new_m_ffn_w_conv': 'new_m', 'new_m_ffn_b_conv': 'new_m', 'new_m_ffn_w_down': 'new_m', 'new_v_norm_mix': 'new_v', 'new_v_norm_ffn': 'new_v', 'new_v_gla_w_in': 'new_v', 'new_v_gla_w_gate_up_f': 'new_v', 'new_v_gla_b_gate_f': 'new_v', 'new_v_gla_w_gate_up_b': 'new_v', 'new_v_gla_b_gate_b': 'new_v', 'new_v_gla_norm': 'new_v', 'new_v_gla_w_out': 'new_v', 'new_v_attn_w_qkv': 'new_v', 'new_v_attn_q_norm': 'new_v', 'new_v_attn_k_norm': 'new_v', 'new_v_attn_w_out': 'new_v', 'new_v_ffn_w_up': 'new_v', 'new_v_ffn_w_conv': 'new_v', 'new_v_ffn_b_conv': 'new_v', 'new_v_ffn_w_down': 'new_v'}


def _forward(args):
    return _fwd_reference(*[args[k] for k in FWD_PARAMS])


def _output_shape():
    def fwd():
        inp = _fwd_setup_inputs(0)
        return _fwd_reference(*[inp[k] for k in FWD_PARAMS])
    out = _jax.eval_shape(fwd)
    return out.shape, out.dtype

N_MICROBATCH = 1
ADAM_LR = 0.001
ADAM_B1 = 0.9
ADAM_B2 = 0.999
ADAM_EPS = 1e-08
ADAM_WD = 0.01
ADAM_STEP = 10
PER_EXAMPLE_BATCH_AXIS = {'x': 0, 'loss_target': 0}
SHARED_INPUTS = []
_WEIGHT_DTYPES = {'norm_mix': _jnp.float32, 'norm_ffn': _jnp.float32, 'gla_w_in': _jnp.float32, 'gla_w_gate_up_f': _jnp.float32, 'gla_b_gate_f': _jnp.float32, 'gla_w_gate_up_b': _jnp.float32, 'gla_b_gate_b': _jnp.float32, 'gla_norm': _jnp.float32, 'gla_w_out': _jnp.float32, 'attn_w_qkv': _jnp.float32, 'attn_q_norm': _jnp.float32, 'attn_k_norm': _jnp.float32, 'attn_w_out': _jnp.float32, 'ffn_w_up': _jnp.float32, 'ffn_w_conv': _jnp.float32, 'ffn_b_conv': _jnp.float32, 'ffn_w_down': _jnp.float32}
MOMENT_SCALE = {'norm_mix': 9.194676e+00, 'norm_ffn': 2.551972e+01, 'gla_w_in': 6.783066e-01, 'gla_w_gate_up_f': 6.309284e-02, 'gla_b_gate_f': 2.678959e-01, 'gla_w_gate_up_b': 6.227974e-02, 'gla_b_gate_b': 2.516166e-01, 'gla_norm': 4.537315e+01, 'gla_w_out': 6.084096e-01, 'attn_w_qkv': 1.239709e-01, 'attn_q_norm': 6.714702e-01, 'attn_k_norm': 6.708039e-01, 'attn_w_out': 1.053434e-01, 'ffn_w_up': 3.364363e-01, 'ffn_w_conv': 3.524427e+00, 'ffn_b_conv': 3.276250e+00, 'ffn_w_down': 5.180741e-01}


def _to_microbatches(a, axis):
    t = _jnp.moveaxis(a, axis, 0)
    t = t.reshape((N_MICROBATCH, t.shape[0] // N_MICROBATCH) + t.shape[1:])
    return _jnp.moveaxis(t, 1, axis + 1)


def setup_inputs(seed: int = 0) -> dict:
    inp = _fwd_setup_inputs(seed)
    key = _jax.random.fold_in(_jax.random.key(seed), 7919)
    shape, _ = _output_shape()
    out = dict(inp)
    out["loss_target"] = _jax.random.normal(_jax.random.fold_in(key, 0), shape, _jnp.float32)
    for i, name in enumerate(TWIN_WEIGHTS):
        w = inp[name].astype(_jnp.float32)
        if MOMENT_SCALE is None:
            s = _jnp.sqrt(_jnp.mean(_jnp.square(w)) + 1e-30)
        else:
            s = MOMENT_SCALE[name]
        km, kv = _jax.random.split(_jax.random.fold_in(key, i + 1))
        out[name] = w
        out["m_" + name] = s * _jax.random.normal(km, w.shape, _jnp.float32)
        out["v_" + name] = (s * s) * _jax.random.uniform(kv, w.shape, _jnp.float32, 0.5, 1.5)
    if N_MICROBATCH > 1:
        for name, axis in PER_EXAMPLE_BATCH_AXIS.items():
            out[name] = _to_microbatches(out[name], axis)
    return {'x': out['x'], 'norm_mix': out['norm_mix'], 'norm_ffn': out['norm_ffn'], 'gla_w_in': out['gla_w_in'], 'gla_w_gate_up_f': out['gla_w_gate_up_f'], 'gla_b_gate_f': out['gla_b_gate_f'], 'gla_w_gate_up_b': out['gla_w_gate_up_b'], 'gla_b_gate_b': out['gla_b_gate_b'], 'gla_norm': out['gla_norm'], 'gla_w_out': out['gla_w_out'], 'attn_w_qkv': out['attn_w_qkv'], 'attn_q_norm': out['attn_q_norm'], 'attn_k_norm': out['attn_k_norm'], 'attn_w_out': out['attn_w_out'], 'ffn_w_up': out['ffn_w_up'], 'ffn_w_conv': out['ffn_w_conv'], 'ffn_b_conv': out['ffn_b_conv'], 'ffn_w_down': out['ffn_w_down'], 'loss_target': out['loss_target'], 'm_norm_mix': out['m_norm_mix'], 'm_norm_ffn': out['m_norm_ffn'], 'm_gla_w_in': out['m_gla_w_in'], 'm_gla_w_gate_up_f': out['m_gla_w_gate_up_f'], 'm_gla_b_gate_f': out['m_gla_b_gate_f'], 'm_gla_w_gate_up_b': out['m_gla_w_gate_up_b'], 'm_gla_b_gate_b': out['m_gla_b_gate_b'], 'm_gla_norm': out['m_gla_norm'], 'm_gla_w_out': out['m_gla_w_out'], 'm_attn_w_qkv': out['m_attn_w_qkv'], 'm_attn_q_norm': out['m_attn_q_norm'], 'm_attn_k_norm': out['m_attn_k_norm'], 'm_attn_w_out': out['m_attn_w_out'], 'm_ffn_w_up': out['m_ffn_w_up'], 'm_ffn_w_conv': out['m_ffn_w_conv'], 'm_ffn_b_conv': out['m_ffn_b_conv'], 'm_ffn_w_down': out['m_ffn_w_down'], 'v_norm_mix': out['v_norm_mix'], 'v_norm_ffn': out['v_norm_ffn'], 'v_gla_w_in': out['v_gla_w_in'], 'v_gla_w_gate_up_f': out['v_gla_w_gate_up_f'], 'v_gla_b_gate_f': out['v_gla_b_gate_f'], 'v_gla_w_gate_up_b': out['v_gla_w_gate_up_b'], 'v_gla_b_gate_b': out['v_gla_b_gate_b'], 'v_gla_norm': out['v_gla_norm'], 'v_gla_w_out': out['v_gla_w_out'], 'v_attn_w_qkv': out['v_attn_w_qkv'], 'v_attn_q_norm': out['v_attn_q_norm'], 'v_attn_k_norm': out['v_attn_k_norm'], 'v_attn_w_out': out['v_attn_w_out'], 'v_ffn_w_up': out['v_ffn_w_up'], 'v_ffn_w_conv': out['v_ffn_w_conv'], 'v_ffn_b_conv': out['v_ffn_b_conv'], 'v_ffn_w_down': out['v_ffn_w_down']}


def _loss(weights, diff, rest, loss_target):
    with _jax.named_scope("forward"):
        args = {**rest, TWIN_DIFF_INPUT: diff, **{k: w.astype(_WEIGHT_DTYPES[k]) for k, w in weights.items()}}
        y = _forward(args)
    with _jax.named_scope("loss_head"):
        err = _jnp.square(y.astype(_jnp.float32) - loss_target)
        return 0.5 * _jnp.sum(_jnp.mean(err, axis=-1)) if err.ndim else 0.5 * err


def _adamw(w, g, m, v):
    m = ADAM_B1 * m + (1.0 - ADAM_B1) * g
    v = ADAM_B2 * v + (1.0 - ADAM_B2) * _jnp.square(g)
    m_hat = m / (1.0 - ADAM_B1 ** ADAM_STEP)
    v_hat = v / (1.0 - ADAM_B2 ** ADAM_STEP)
    delta = -ADAM_LR * (m_hat / (_jnp.sqrt(v_hat) + ADAM_EPS) + ADAM_WD * w)
    return delta, m, v


def reference(x, norm_mix, norm_ffn, gla_w_in, gla_w_gate_up_f, gla_b_gate_f, gla_w_gate_up_b, gla_b_gate_b, gla_norm, gla_w_out, attn_w_qkv, attn_q_norm, attn_k_norm, attn_w_out, ffn_w_up, ffn_w_conv, ffn_b_conv, ffn_w_down, loss_target, m_norm_mix, m_norm_ffn, m_gla_w_in, m_gla_w_gate_up_f, m_gla_b_gate_f, m_gla_w_gate_up_b, m_gla_b_gate_b, m_gla_norm, m_gla_w_out, m_attn_w_qkv, m_attn_q_norm, m_attn_k_norm, m_attn_w_out, m_ffn_w_up, m_ffn_w_conv, m_ffn_b_conv, m_ffn_w_down, v_norm_mix, v_norm_ffn, v_gla_w_in, v_gla_w_gate_up_f, v_gla_b_gate_f, v_gla_w_gate_up_b, v_gla_b_gate_b, v_gla_norm, v_gla_w_out, v_attn_w_qkv, v_attn_q_norm, v_attn_k_norm, v_attn_w_out, v_ffn_w_up, v_ffn_w_conv, v_ffn_b_conv, v_ffn_w_down):
    given = dict(x=x, norm_mix=norm_mix, norm_ffn=norm_ffn, gla_w_in=gla_w_in, gla_w_gate_up_f=gla_w_gate_up_f, gla_b_gate_f=gla_b_gate_f, gla_w_gate_up_b=gla_w_gate_up_b, gla_b_gate_b=gla_b_gate_b, gla_norm=gla_norm, gla_w_out=gla_w_out, attn_w_qkv=attn_w_qkv, attn_q_norm=attn_q_norm, attn_k_norm=attn_k_norm, attn_w_out=attn_w_out, ffn_w_up=ffn_w_up, ffn_w_conv=ffn_w_conv, ffn_b_conv=ffn_b_conv, ffn_w_down=ffn_w_down, loss_target=loss_target, m_norm_mix=m_norm_mix, m_norm_ffn=m_norm_ffn, m_gla_w_in=m_gla_w_in, m_gla_w_gate_up_f=m_gla_w_gate_up_f, m_gla_b_gate_f=m_gla_b_gate_f, m_gla_w_gate_up_b=m_gla_w_gate_up_b, m_gla_b_gate_b=m_gla_b_gate_b, m_gla_norm=m_gla_norm, m_gla_w_out=m_gla_w_out, m_attn_w_qkv=m_attn_w_qkv, m_attn_q_norm=m_attn_q_norm, m_attn_k_norm=m_attn_k_norm, m_attn_w_out=m_attn_w_out, m_ffn_w_up=m_ffn_w_up, m_ffn_w_conv=m_ffn_w_conv, m_ffn_b_conv=m_ffn_b_conv, m_ffn_w_down=m_ffn_w_down, v_norm_mix=v_norm_mix, v_norm_ffn=v_norm_ffn, v_gla_w_in=v_gla_w_in, v_gla_w_gate_up_f=v_gla_w_gate_up_f, v_gla_b_gate_f=v_gla_b_gate_f, v_gla_w_gate_up_b=v_gla_w_gate_up_b, v_gla_b_gate_b=v_gla_b_gate_b, v_gla_norm=v_gla_norm, v_gla_w_out=v_gla_w_out, v_attn_w_qkv=v_attn_w_qkv, v_attn_q_norm=v_attn_q_norm, v_attn_k_norm=v_attn_k_norm, v_attn_w_out=v_attn_w_out, v_ffn_w_up=v_ffn_w_up, v_ffn_w_conv=v_ffn_w_conv, v_ffn_b_conv=v_ffn_b_conv, v_ffn_w_down=v_ffn_w_down)
    weights = {n: given[n] for n in TWIN_WEIGHTS}
    shared = {n: given[n] for n in SHARED_INPUTS}
    per_example = {n: given[n] for n in ['x']}
    grad_fn = _jax.value_and_grad(_loss, argnums=(0, 1))

    def one_microbatch(ex, loss_target):
        ex = dict(ex)
        diff = ex.pop(TWIN_DIFF_INPUT)
        return grad_fn(weights, diff, {**shared, **ex}, loss_target)

    if N_MICROBATCH == 1:
        loss, (grad_w, grad_x) = one_microbatch(per_example, given["loss_target"])
    else:
        def body(carry, xs):
            loss_sum, grad_sum = carry
            l_k, (gw_k, gx_k) = one_microbatch(xs[0], xs[1])
            with _jax.named_scope("update"):
                return (loss_sum + l_k, _jax.tree.map(_jnp.add, grad_sum, gw_k)), gx_k

        init = (_jnp.zeros((), _jnp.float32), _jax.tree.map(_jnp.zeros_like, weights))
        (loss, grad_w), grad_x = _jax.lax.scan(body, init, (per_example, given["loss_target"]))
    with _jax.named_scope("update"):
        delta_w, new_m, new_v = {}, {}, {}
        for n in TWIN_WEIGHTS:
            delta_w[n], new_m[n], new_v[n] = _adamw(weights[n], grad_w[n], given["m_" + n], given["v_" + n])
    return (loss, grad_x, *[grad_w[n] for n in TWIN_WEIGHTS], *[delta_w[n] for n in TWIN_WEIGHTS],
            *[new_m[n] for n in TWIN_WEIGHTS], *[new_v[n] for n in TWIN_WEIGHTS])
```

```python
import functools

import jax
import jax.numpy as jnp
from jax import lax
from jax.experimental import pallas as pl
from jax.experimental.pallas import tpu as pltpu

F32 = jnp.float32
BF16 = jnp.bfloat16
MESH = pl.DeviceIdType.MESH
HIGHEST = lax.Precision.HIGHEST

D_MODEL = 1024
DEPTH = 4
GRID_W = 64
NORM_EPS = 1e-6
GLA_HEADS = 4
GLA_DK = 128
GLA_DV = 256
GLA_KEY = GLA_HEADS * GLA_DK
GLA_VAL = GLA_HEADS * GLA_DV
GLA_RANK = 16
GLA_CHUNK = 64
GLA_GATE_NORMALIZER = 16.0
GLA_IN = 2 * GLA_KEY + 2 * GLA_VAL + 2 * GLA_RANK
GLA_IN_PAD = 3200
GLA_R_BLOCK = (2 * GLA_KEY + 2 * GLA_VAL) // 128
ATT_HD = 128
ATT_QH = 8
ATT_KVH = 2
ATT_GROUP = ATT_QH // ATT_KVH
ATT_QKV = (ATT_QH + 2 * ATT_KVH) * ATT_HD
ROPE_THETA = 10000.0
D_FF = 2816
ADAM_LR = 0.001
ADAM_B1 = 0.9
ADAM_B2 = 0.999
ADAM_EPS = 1e-08
ADAM_WD = 0.01
ADAM_STEP = 10

N_CHIPS = 4
LANES = 128
VMEM_LIMIT = 48 * 1024 * 1024


def _cp(sem):
    return pltpu.CompilerParams(dimension_semantics=sem, vmem_limit_bytes=VMEM_LIMIT)


def _pick(n, cands):
    for c in cands:
        if n % c == 0:
            return c
    return n


def _dg(a, b, ca, cb):
    return lax.dot_general(a, b, (((ca,), (cb,)), ((), ())), preferred_element_type=F32)


def _sigmoid(x):
    return 1.0 / (1.0 + jnp.exp(-x))


def _rmsnorm_fwd(x, w, name):
    S, D = x.shape
    tm = _pick(S, (512, 256))

    def body(x_ref, w_ref, h_ref):
        xv = x_ref[...]
        r = lax.rsqrt(jnp.mean(xv * xv, axis=-1, keepdims=True) + NORM_EPS)
        h_ref[...] = (xv * r * w_ref[...]).astype(BF16)

    return pl.pallas_call(
        body, name=name, grid=(S // tm,),
        in_specs=[pl.BlockSpec((tm, D), lambda i: (i, 0)), pl.BlockSpec((1, D), lambda i: (0, 0))],
        out_specs=pl.BlockSpec((tm, D), lambda i: (i, 0)),
        out_shape=jax.ShapeDtypeStruct((S, D), BF16),
        compiler_params=_cp(("parallel",)),
    )(x, w)


def _loss_grad(y, t, name):
    S, D = y.shape
    tm = _pick(S, (512, 256))

    def body(y_ref, t_ref, dy_ref, dyb_ref, loss_ref):
        i = pl.program_id(0)
        d = y_ref[...] - t_ref[...]
        dy = d * (1.0 / D)
        dy_ref[...] = dy
        dyb_ref[...] = dy.astype(BF16)
        sq = jnp.sum(jnp.sum(d * d, axis=1, keepdims=True), axis=0, keepdims=True)
        part = jnp.broadcast_to(sq * (0.5 / D), (1, LANES))

        @pl.when(i == 0)
        def _():
            loss_ref[...] = part

        @pl.when(i > 0)
        def _():
            loss_ref[...] += part

    return pl.pallas_call(
        body, name=name, grid=(S // tm,),
        in_specs=[pl.BlockSpec((tm, D), lambda i: (i, 0)), pl.BlockSpec((tm, D), lambda i: (i, 0))],
        out_specs=[pl.BlockSpec((tm, D), lambda i: (i, 0)), pl.BlockSpec((tm, D), lambda i: (i, 0)),
                   pl.BlockSpec((1, LANES), lambda i: (0, 0))],
        out_shape=[jax.ShapeDtypeStruct((S, D), F32), jax.ShapeDtypeStruct((S, D), BF16),
                   jax.ShapeDtypeStruct((1, LANES), F32)],
        compiler_params=_cp(("arbitrary",)),
    )(y, t)


def _matmul(a, b, ca, cb, name, res=None, out_dtype=F32):
    M, K = a.shape[1 - ca], a.shape[ca]
    N = b.shape[1 - cb]
    assert b.shape[cb] == K
    tm = _pick(M, (512, 256, 128))
    tn = _pick(N, (512, 640, 256, 128))
    tk = _pick(K, (1024, 1408, 512, 640, 256, 128))
    nk = K // tk
    if ca == 1:
        a_spec = pl.BlockSpec((tm, tk), lambda i, j, k: (i, k))
    else:
        a_spec = pl.BlockSpec((tk, tm), lambda i, j, k: (k, i))
    if cb == 0:
        b_spec = pl.BlockSpec((tk, tn), lambda i, j, k: (k, j))
    else:
        b_spec = pl.BlockSpec((tn, tk), lambda i, j, k: (j, k))
    o_spec = pl.BlockSpec((tm, tn), lambda i, j, k: (i, j))
    has_res = res is not None

    def body(*refs):
        if has_res:
            a_ref, b_ref, r_ref, o_ref, acc = refs
        else:
            a_ref, b_ref, o_ref, acc = refs
        k = pl.program_id(2)

        @pl.when(k == 0)
        def _():
            acc[...] = jnp.zeros_like(acc)

        acc[...] += _dg(a_ref[...].astype(BF16), b_ref[...].astype(BF16), ca, cb)

        @pl.when(k == nk - 1)
        def _():
            v = acc[...]
            if has_res:
                v = v + r_ref[...]
            o_ref[...] = v.astype(out_dtype)

    in_specs = [a_spec, b_spec] + ([o_spec] if has_res else [])
    args = (a, b) + ((res,) if has_res else ())
    return pl.pallas_call(
        body, name=name, grid=(M // tm, N // tn, nk),
        in_specs=in_specs, out_specs=o_spec,
        out_shape=jax.ShapeDtypeStruct((M, N), out_dtype),
        scratch_shapes=[pltpu.VMEM((tm, tn), F32)],
        compiler_params=_cp(("parallel", "parallel", "arbitrary")),
    )(*args)


def _dgrad_norm(dy, w, x, wn, dres, name):
    M, Kp = dy.shape
    D = w.shape[0]
    tm = _pick(M, (256, 128))
    tk = _pick(Kp, (512, 640, 256, 128))
    nk = Kp // tk

    def body(dy_ref, w_ref, x_ref, wn_ref, dres_ref, dx_ref, dxb_ref, dwn_ref, acc):
        i = pl.program_id(0)
        k = pl.program_id(1)

        @pl.when(k == 0)
        def _():
            acc[...] = jnp.zeros_like(acc)

        acc[...] += _dg(dy_ref[...], w_ref[...], 1, 1)

        @pl.when(k == nk - 1)
        def _():
            dh = acc[...]
            xv = x_ref[...]
            r = lax.rsqrt(jnp.mean(xv * xv, axis=-1, keepdims=True) + NORM_EPS)
            yv = xv * r
            dyv = dh * wn_ref[...]
            dxv = r * (dyv - yv * jnp.mean(dyv * yv, axis=-1, keepdims=True)) + dres_ref[...]
            dx_ref[...] = dxv
            dxb_ref[...] = dxv.astype(BF16)
            part = jnp.sum(dh * yv, axis=0, keepdims=True)

            @pl.when(i == 0)
            def _():
                dwn_ref[...] = part

            @pl.when(i > 0)
            def _():
                dwn_ref[...] += part

    row = pl.BlockSpec((tm, D), lambda i, k: (i, 0))
    return pl.pallas_call(
        body, name=name, grid=(M // tm, nk),
        in_specs=[pl.BlockSpec((tm, tk), lambda i, k: (i, k)), pl.BlockSpec((D, tk), lambda i, k: (0, k)),
                  row, pl.BlockSpec((1, D), lambda i, k: (0, 0)), row],
        out_specs=[row, row, pl.BlockSpec((1, D), lambda i, k: (0, 0))],
        out_shape=[jax.ShapeDtypeStruct((M, D), F32), jax.ShapeDtypeStruct((M, D), BF16),
                   jax.ShapeDtypeStruct((1, D), F32)],
        scratch_shapes=[pltpu.VMEM((tm, D), F32)],
        compiler_params=_cp(("arbitrary", "arbitrary")),
    )(dy, w, x, wn, dres)


FFN_TN = 128
FFN_ROWS = 256
PAD = 8


def _conv3(pad_ref, w, r0, tr):
    um = pad_ref[pl.ds(PAD - 1 + r0, tr), :]
    uc = pad_ref[pl.ds(PAD + r0, tr), :]
    up = pad_ref[pl.ds(PAD + 1 + r0, tr), :]
    return w[0:1, :] * um + w[1:2, :] * uc + w[2:3, :] * up, (um, uc, up)


def _zero_pads(pad_ref, S, tn):
    pad_ref[pl.ds(0, PAD), :] = jnp.zeros((PAD, tn), F32)
    pad_ref[pl.ds(PAD + S, PAD), :] = jnp.zeros((PAD, tn), F32)


def _ffn_mid_fwd(h, wup, wconv, bconv, name):
    S, D = h.shape
    F = wup.shape[1] // 2
    tn = FFN_TN
    nb = F // tn
    tr = min(FFN_ROWS, S)

    def body(h_ref, wv_ref, wg_ref, cv_ref, cg_ref, bv_ref, bg_ref, a_ref, uv_ref, ug_ref, padv, padg):
        _zero_pads(padv, S, tn)
        _zero_pads(padg, S, tn)
        hv = h_ref[...]
        padv[pl.ds(PAD, S), :] = jnp.dot(hv, wv_ref[...], preferred_element_type=F32)
        padg[pl.ds(PAD, S), :] = jnp.dot(hv, wg_ref[...], preferred_element_type=F32)
        uv_ref[...] = padv[pl.ds(PAD, S), :]
        ug_ref[...] = padg[pl.ds(PAD, S), :]
        cwv, cwg, bv, bg = cv_ref[...], cg_ref[...], bv_ref[...], bg_ref[...]
        for r0 in range(0, S, tr):
            cv = _conv3(padv, cwv, r0, tr)[0] + bv
            cg = _conv3(padg, cwg, r0, tr)[0] + bg
            a_ref[pl.ds(r0, tr), :] = (cg * _sigmoid(cg) * cv).astype(BF16)

    col = lambda off: (lambda j: (0, j + off))
    return pl.pallas_call(
        body, name=name, grid=(nb,),
        in_specs=[pl.BlockSpec((S, D), lambda j: (0, 0)),
                  pl.BlockSpec((D, tn), col(0)), pl.BlockSpec((D, tn), col(nb)),
                  pl.BlockSpec((3, tn), col(0)), pl.BlockSpec((3, tn), col(nb)),
                  pl.BlockSpec((1, tn), col(0)), pl.BlockSpec((1, tn), col(nb))],
        out_specs=[pl.BlockSpec((S, tn), col(0))] * 3,
        out_shape=[jax.ShapeDtypeStruct((S, F), BF16), jax.ShapeDtypeStruct((S, F), F32),
                   jax.ShapeDtypeStruct((S, F), F32)],
        scratch_shapes=[pltpu.VMEM((S + 2 * PAD, tn), F32)] * 2,
        compiler_params=_cp(("parallel",)),
    )(h, wup, wup, wconv, wconv, bconv, bconv)


def _rows8(rows):
    n = rows[0].shape[1]
    idx = lax.broadcasted_iota(jnp.int32, (8, n), 0)
    out = jnp.zeros((8, n), F32)
    for k, r in enumerate(rows):
        out = jnp.where(idx == k, r, out)
    return out


def _ffn_mid_bwd(dyb, wdown, uv, ug, wconv, bconv, name):
    S, D = dyb.shape
    F = wdown.shape[0]
    tn = FFN_TN
    nb = F // tn
    tr = min(FFN_ROWS, S)

    def body(dy_ref, wd_ref, uv_ref, ug_ref, cv_ref, cg_ref, bv_ref, bg_ref,
             duv_ref, dug_ref, a_ref, gwv_ref, gwg_ref, da_s, padv, padg, pdv, pdg):
        for p in (padv, padg, pdv, pdg):
            _zero_pads(p, S, tn)
        da_s[...] = _dg(dy_ref[...], wd_ref[...], 1, 1)
        padv[pl.ds(PAD, S), :] = uv_ref[...]
        padg[pl.ds(PAD, S), :] = ug_ref[...]
        cwv, cwg, bv, bg = cv_ref[...], cg_ref[...], bv_ref[...], bg_ref[...]
        zero = jnp.zeros((1, tn), F32)
        gv = [zero, zero, zero, zero]
        gg = [zero, zero, zero, zero]
        for r0 in range(0, S, tr):
            cv, shv = _conv3(padv, cwv, r0, tr)
            cg, shg = _conv3(padg, cwg, r0, tr)
            cv = cv + bv
            cg = cg + bg
            sg = _sigmoid(cg)
            sl = cg * sg
            a_ref[pl.ds(r0, tr), :] = (sl * cv).astype(BF16)
            da = da_s[pl.ds(r0, tr), :]
            dcv = da * sl
            dcg = da * cv * (sg * (1.0 + cg * (1.0 - sg)))
            pdv[pl.ds(PAD + r0, tr), :] = dcv
            pdg[pl.ds(PAD + r0, tr), :] = dcg
            for k in range(3):
                gv[k] = gv[k] + jnp.sum(dcv * shv[k], axis=0, keepdims=True)
                gg[k] = gg[k] + jnp.sum(dcg * shg[k], axis=0, keepdims=True)
            gv[3] = gv[3] + jnp.sum(dcv, axis=0, keepdims=True)
            gg[3] = gg[3] + jnp.sum(dcg, axis=0, keepdims=True)
        gwv_ref[...] = _rows8(gv)
        gwg_ref[...] = _rows8(gg)
        for r0 in range(0, S, tr):
            for pd, cw, out in ((pdv, cwv, duv_ref), (pdg, cwg, dug_ref)):
                dm = pd[pl.ds(PAD - 1 + r0, tr), :]
                dc = pd[pl.ds(PAD + r0, tr), :]
                dp = pd[pl.ds(PAD + 1 + r0, tr), :]
                out[pl.ds(r0, tr), :] = (cw[0:1, :] * dp + cw[1:2, :] * dc + cw[2:3, :] * dm).astype(BF16)

    col = lambda off: (lambda j: (0, j + off))
    blk = pl.BlockSpec((S, tn), col(0))
    g8 = pl.BlockSpec((8, tn), col(0))
    return pl.pallas_call(
        body, name=name, grid=(nb,),
        in_specs=[pl.BlockSpec((S, D), lambda j: (0, 0)), pl.BlockSpec((tn, D), lambda j: (j, 0)), blk, blk,
                  pl.BlockSpec((3, tn), col(0)), pl.BlockSpec((3, tn), col(nb)),
                  pl.BlockSpec((1, tn), col(0)), pl.BlockSpec((1, tn), col(nb))],
        out_specs=[blk, blk, blk, g8, g8],
        out_shape=[jax.ShapeDtypeStruct((S, F), BF16), jax.ShapeDtypeStruct((S, F), BF16),
                   jax.ShapeDtypeStruct((S, F), BF16), jax.ShapeDtypeStruct((8, F), F32),
                   jax.ShapeDtypeStruct((8, F), F32)],
        scratch_shapes=[pltpu.VMEM((S, tn), F32)] + [pltpu.VMEM((S + 2 * PAD, tn), F32)] * 4,
        compiler_params=_cp(("parallel",)),
    )(dyb, wdown, uv, ug, wconv, wconv, bconv, bconv)


def _log_sigmoid(x):
    return jnp.minimum(x, 0.0) - jnp.log(1.0 + jnp.exp(-jnp.abs(x)))


def _gla_gate_fwd(proj, wgf, bgf, wgb, bgb, name):
    S = proj.shape[0]
    tm = _pick(S, (512, 256))

    def body(r_ref, wf_ref, bf_ref, wb_ref, bb_ref, laf_ref, lab_ref):
        r = r_ref[...].astype(BF16)
        lf = jnp.dot(r, wf_ref[...].astype(BF16), preferred_element_type=F32) + bf_ref[...]
        lb = jnp.dot(r, wb_ref[...].astype(BF16), preferred_element_type=F32) + bb_ref[...]
        laf_ref[...] = _log_sigmoid(lf) * (1.0 / GLA_GATE_NORMALIZER)
        lab_ref[...] = _log_sigmoid(lb) * (1.0 / GLA_GATE_NORMALIZER)

    full = lambda shp: pl.BlockSpec(shp, lambda i: (0, 0))
    row = pl.BlockSpec((tm, GLA_KEY), lambda i: (i, 0))
    return pl.pallas_call(
        body, name=name, grid=(S // tm,),
        in_specs=[pl.BlockSpec((tm, LANES), lambda i: (i, GLA_R_BLOCK)),
                  full((LANES, GLA_KEY)), full((1, GLA_KEY)), full((LANES, GLA_KEY)), full((1, GLA_KEY))],
        out_specs=[row, row],
        out_shape=[jax.ShapeDtypeStruct((S, GLA_KEY), F32)] * 2,
        compiler_params=_cp(("parallel",)),
    )(proj, wgf, bgf, wgb, bgb)


def _gla_gate_bwd(dlaf, dlab, proj, wgf, bgf, wgb, bgb, name):
    S = proj.shape[0]
    tm = _pick(S, (512, 256))

    def body(dlf_ref, dlb_ref, r_ref, wf_ref, bf_ref, wb_ref, bb_ref, dr_ref, dwf_ref, dbf_ref, dwb_ref, dbb_ref):
        i = pl.program_id(0)
        r = r_ref[...].astype(BF16)
        wf = wf_ref[...].astype(BF16)
        wb = wb_ref[...].astype(BF16)
        lf = jnp.dot(r, wf, preferred_element_type=F32) + bf_ref[...]
        lb = jnp.dot(r, wb, preferred_element_type=F32) + bb_ref[...]
        glf = dlf_ref[...] * (1.0 / GLA_GATE_NORMALIZER) * (1.0 / (1.0 + jnp.exp(lf)))
        glb = dlb_ref[...] * (1.0 / GLA_GATE_NORMALIZER) * (1.0 / (1.0 + jnp.exp(lb)))
        gfb = glf.astype(BF16)
        gbb = glb.astype(BF16)
        dr_ref[...] = _dg(gfb, wf, 1, 1) + _dg(gbb, wb, 1, 1)
        parts = (_dg(r, gfb, 0, 0), jnp.sum(glf, axis=0, keepdims=True),
                 _dg(r, gbb, 0, 0), jnp.sum(glb, axis=0, keepdims=True))
        outs = (dwf_ref, dbf_ref, dwb_ref, dbb_ref)

        @pl.when(i == 0)
        def _():
            for o, p in zip(outs, parts):
                o[...] = p

        @pl.when(i > 0)
        def _():
            for o, p in zip(outs, parts):
                o[...] += p

    full = lambda shp: pl.BlockSpec(shp, lambda i: (0, 0))
    row = pl.BlockSpec((tm, GLA_KEY), lambda i: (i, 0))
    return pl.pallas_call(
        body, name=name, grid=(S // tm,),
        in_specs=[row, row, pl.BlockSpec((tm, LANES), lambda i: (i, GLA_R_BLOCK)),
                  full((LANES, GLA_KEY)), full((1, GLA_KEY)), full((LANES, GLA_KEY)), full((1, GLA_KEY))],
        out_specs=[pl.BlockSpec((tm, LANES), lambda i: (i, 0)),
                   full((LANES, GLA_KEY)), full((1, GLA_KEY)), full((LANES, GLA_KEY)), full((1, GLA_KEY))],
        out_shape=[jax.ShapeDtypeStruct((S, LANES), F32),
                   jax.ShapeDtypeStruct((LANES, GLA_KEY), F32), jax.ShapeDtypeStruct((1, GLA_KEY), F32),
                   jax.ShapeDtypeStruct((LANES, GLA_KEY), F32), jax.ShapeDtypeStruct((1, GLA_KEY), F32)],
        compiler_params=_cp(("arbitrary",)),
    )(dlaf, dlab, proj, wgf, bgf, wgb, bgb)


def _gla_masks(rev):
    C = GLA_CHUNK
    t = lax.broadcasted_iota(jnp.int32, (C, C), 0)
    s = lax.broadcasted_iota(jnp.int32, (C, C), 1)
    if rev:
        return (s >= t), (s > t), (t >= s), (t > s)
    return (s <= t), (s <= t), (t <= s), (t <= s)


def _gla_chunk_common(q, k, la, cum, end_row):
    b = jnp.dot(cum.astype(F32), la, precision=HIGHEST, preferred_element_type=F32)
    bend = b[end_row:end_row + 1, :]
    e = jnp.exp(b)
    qd = q * (GLA_DK ** -0.5) * e
    ei = jnp.exp(-b)
    ee = jnp.exp(bend - b)
    d = jnp.exp(bend)
    return e, ei, ee, d, qd, k * ei, k * ee


GLA_CB = 8


def _gla_specs(S, rev_order):
    n = S // GLA_CHUNK
    cb = min(GLA_CB, n)
    nblk = n // cb
    rows = cb * GLA_CHUNK
    ci = (lambda i: nblk - 1 - i) if rev_order else (lambda i: i)
    q_spec = pl.BlockSpec((rows, GLA_DK), lambda h, i: (ci(i), h))
    k_spec = pl.BlockSpec((rows, GLA_DK), lambda h, i: (ci(i), GLA_HEADS + h))
    v_spec = pl.BlockSpec((rows, GLA_DV), lambda h, i: (ci(i), GLA_KEY * 2 // GLA_DV + h))
    la_spec = pl.BlockSpec((rows, GLA_DK), lambda h, i: (ci(i), h))
    o_spec = pl.BlockSpec((rows, GLA_DV), lambda h, i: (ci(i), h))
    st_spec = pl.BlockSpec((1, cb, GLA_DV, GLA_DK), lambda h, i: (h, ci(i), 0, 0))
    return n, cb, nblk, q_spec, k_spec, v_spec, la_spec, o_spec, st_spec


def _gla_scan_fwd(proj, la, rev, name):
    S = proj.shape[0]
    C = GLA_CHUNK
    n, cb, nblk, q_spec, k_spec, v_spec, la_spec, o_spec, st_spec = _gla_specs(S, rev)
    end_row = 0 if rev else C - 1
    order = list(range(cb))[::-1] if rev else list(range(cb))

    def body(q_ref, k_ref, v_ref, la_ref, o_ref, st_ref, state):
        i = pl.program_id(1)

        @pl.when(i == 0)
        def _():
            state[...] = jnp.zeros_like(state)

        cum, mask, _, _ = _gla_masks(rev)
        st = state[...]
        for cc in order:
            rows = pl.ds(cc * C, C)
            q, k, v, lav = q_ref[rows, :], k_ref[rows, :], v_ref[rows, :], la_ref[rows, :]
            _, _, _, d, qd, ki, ke = _gla_chunk_common(q, k, lav, cum, end_row)
            qdb, kib, keb, vb = qd.astype(BF16), ki.astype(BF16), ke.astype(BF16), v.astype(BF16)
            att = jnp.where(mask, _dg(qdb, kib, 1, 1), 0.0)
            o_ref[rows, :] = (jnp.dot(att.astype(BF16), vb, preferred_element_type=F32)
                              + _dg(qdb, st.astype(BF16), 1, 1))
            st_ref[0, cc] = st
            st = st * d + _dg(vb, keb, 0, 0)
        state[...] = st

    return pl.pallas_call(
        body, name=name, grid=(GLA_HEADS, nblk),
        in_specs=[q_spec, k_spec, v_spec, la_spec],
        out_specs=[o_spec, st_spec],
        out_shape=[jax.ShapeDtypeStruct((S, GLA_VAL), F32),
                   jax.ShapeDtypeStruct((GLA_HEADS, n, GLA_DV, GLA_DK), F32)],
        scratch_shapes=[pltpu.VMEM((GLA_DV, GLA_DK), F32)],
        compiler_params=_cp(("parallel", "arbitrary")),
    )(proj, proj, proj, la)


def _gla_scan_bwd(do, proj, la, states, rev, name):
    S = proj.shape[0]
    C = GLA_CHUNK
    n, cb, nblk, q_spec, k_spec, v_spec, la_spec, o_spec, st_spec = _gla_specs(S, not rev)
    end_row = 0 if rev else C - 1
    order = list(range(cb)) if rev else list(range(cb))[::-1]

    def body(do_ref, q_ref, k_ref, v_ref, la_ref, st_ref, dq_ref, dk_ref, dv_ref, dla_ref, gstate):
        i = pl.program_id(1)

        @pl.when(i == 0)
        def _():
            gstate[...] = jnp.zeros_like(gstate)

        cum, mask, cum_t, mask_t = _gla_masks(rev)
        g = gstate[...]
        for cc in order:
            rows = pl.ds(cc * C, C)
            q, k, v, lav = q_ref[rows, :], k_ref[rows, :], v_ref[rows, :], la_ref[rows, :]
            dov = do_ref[rows, :]
            st = st_ref[0, cc]
            e, ei, ee, d, qd, ki, ke = _gla_chunk_common(q, k, lav, cum, end_row)
            qdb, kib, keb, vb = qd.astype(BF16), ki.astype(BF16), ke.astype(BF16), v.astype(BF16)
            dob, gb, stb = dov.astype(BF16), g.astype(BF16), st.astype(BF16)
            att_t = jnp.where(mask_t, _dg(kib, qdb, 1, 1), 0.0)
            da = jnp.where(mask, _dg(dob, vb, 1, 1), 0.0)
            da_t = jnp.where(mask_t, _dg(vb, dob, 1, 1), 0.0)
            dv_ref[rows, :] = jnp.dot(att_t.astype(BF16), dob, preferred_element_type=F32) + _dg(keb, gb, 1, 1)
            dqd = (jnp.dot(da.astype(BF16), kib, preferred_element_type=F32)
                   + jnp.dot(dob, stb, preferred_element_type=F32))
            dki = jnp.dot(da_t.astype(BF16), qdb, preferred_element_type=F32)
            dke = jnp.dot(vb, gb, preferred_element_type=F32)
            dd = jnp.sum(st * g, axis=0, keepdims=True)
            g = g * d + _dg(dob, qdb, 0, 0)
            dq_ref[rows, :] = dqd * e * (GLA_DK ** -0.5)
            dk_ref[rows, :] = dki * ei + dke * ee
            dkeke = dke * ke
            db = dqd * qd - dki * ki - dkeke
            dbend = jnp.sum(dkeke, axis=0, keepdims=True) + dd * d
            dla_ref[rows, :] = jnp.dot(cum_t.astype(F32), db, precision=HIGHEST, preferred_element_type=F32) + dbend
        gstate[...] = g

    key_out = la_spec
    return pl.pallas_call(
        body, name=name, grid=(GLA_HEADS, nblk),
        in_specs=[o_spec, q_spec, k_spec, v_spec, la_spec, st_spec],
        out_specs=[key_out, key_out, o_spec, key_out],
        out_shape=[jax.ShapeDtypeStruct((S, GLA_KEY), F32), jax.ShapeDtypeStruct((S, GLA_KEY), F32),
                   jax.ShapeDtypeStruct((S, GLA_VAL), F32), jax.ShapeDtypeStruct((S, GLA_KEY), F32)],
        scratch_shapes=[pltpu.VMEM((GLA_DV, GLA_DK), F32)],
        compiler_params=_cp(("parallel", "arbitrary")),
    )(do, proj, proj, proj, la, states)


def _gla_out_fwd(of, ob, proj, gn, name):
    S = of.shape[0]
    tm = _pick(S, (256, 128))
    gblk = (2 * GLA_KEY + GLA_VAL) // GLA_VAL

    def body(of_ref, ob_ref, g_ref, gn_ref, z_ref):
        gnv = gn_ref[...]
        for h in range(GLA_HEADS):
            cols = pl.ds(h * GLA_DV, GLA_DV)
            o = of_ref[:, cols] + ob_ref[:, cols]
            r = lax.rsqrt(jnp.mean(o * o, axis=-1, keepdims=True) + NORM_EPS)
            gv = g_ref[:, cols]
            z_ref[:, cols] = (o * r * gnv * (gv * _sigmoid(gv))).astype(BF16)

    row = pl.BlockSpec((tm, GLA_VAL), lambda i: (i, 0))
    return pl.pallas_call(
        body, name=name, grid=(S // tm,),
        in_specs=[row, row, pl.BlockSpec((tm, GLA_VAL), lambda i: (i, gblk)),
                  pl.BlockSpec((1, GLA_DV), lambda i: (0, 0))],
        out_specs=row,
        out_shape=jax.ShapeDtypeStruct((S, GLA_VAL), BF16),
        compiler_params=_cp(("parallel",)),
    )(of, ob, proj, gn)


def _gla_out_bwd(dz, of, ob, proj, gn, name):
    S = of.shape[0]
    tm = _pick(S, (256, 128))
    gblk = (2 * GLA_KEY + GLA_VAL) // GLA_VAL

    def body(dz_ref, of_ref, ob_ref, g_ref, gn_ref, do_ref, dg_ref, dgn_ref):
        i = pl.program_id(0)
        gnv = gn_ref[...]
        part = jnp.zeros((1, GLA_DV), F32)
        for h in range(GLA_HEADS):
            cols = pl.ds(h * GLA_DV, GLA_DV)
            o = of_ref[:, cols] + ob_ref[:, cols]
            r = lax.rsqrt(jnp.mean(o * o, axis=-1, keepdims=True) + NORM_EPS)
            y = o * r
            gv = g_ref[:, cols]
            sg = _sigmoid(gv)
            dzv = dz_ref[:, cols]
            dg_ref[:, cols] = dzv * (y * gnv) * (sg * (1.0 + gv * (1.0 - sg)))
            don = dzv * (gv * sg)
            part = part + jnp.sum(don * y, axis=0, keepdims=True)
            dy = don * gnv
            do_ref[:, cols] = r * (dy - y * jnp.mean(dy * y, axis=-1, keepdims=True))

        @pl.when(i == 0)
        def _():
            dgn_ref[...] = part

        @pl.when(i > 0)
        def _():
            dgn_ref[...] += part

    row = pl.BlockSpec((tm, GLA_VAL), lambda i: (i, 0))
    one = pl.BlockSpec((1, GLA_DV), lambda i: (0, 0))
    return pl.pallas_call(
        body, name=name, grid=(S // tm,),
        in_specs=[row, row, row, pl.BlockSpec((tm, GLA_VAL), lambda i: (i, gblk)), one],
        out_specs=[row, row, one],
        out_shape=[jax.ShapeDtypeStruct((S, GLA_VAL), F32), jax.ShapeDtypeStruct((S, GLA_VAL), F32),
                   jax.ShapeDtypeStruct((1, GLA_DV), F32)],
        compiler_params=_cp(("arbitrary",)),
    )(dz, of, ob, proj, gn)


N_QK_HEADS = ATT_QH + ATT_KVH


def _qk_prep_fwd(proj, qn, kn, rc, rs, name):
    S = proj.shape[0]
    tm = _pick(S, (256, 128))
    W = N_QK_HEADS * ATT_HD

    def body(p_ref, qn_ref, kn_ref, rc_ref, rs_ref, v_in_ref, qk_ref, v_ref):
        c, s = rc_ref[...], rs_ref[...]
        for h in range(N_QK_HEADS):
            cols = pl.ds(h * ATT_HD, ATT_HD)
            w = qn_ref[...] if h < ATT_QH else kn_ref[...]
            xv = p_ref[:, cols]
            r = lax.rsqrt(jnp.mean(xv * xv, axis=-1, keepdims=True) + NORM_EPS)
            y = xv * r * w
            qk_ref[:, cols] = (y * c + pltpu.roll(y, ATT_HD // 2, 1) * s).astype(BF16)
        v_ref[...] = v_in_ref[...].astype(BF16)

    one = pl.BlockSpec((1, ATT_HD), lambda i: (0, 0))
    tab = pl.BlockSpec((tm, ATT_HD), lambda i: (i, 0))
    vw = ATT_KVH * ATT_HD
    return pl.pallas_call(
        body, name=name, grid=(S // tm,),
        in_specs=[pl.BlockSpec((tm, W), lambda i: (i, 0)), one, one, tab, tab,
                  pl.BlockSpec((tm, vw), lambda i: (i, W // vw))],
        out_specs=[pl.BlockSpec((tm, W), lambda i: (i, 0)), pl.BlockSpec((tm, vw), lambda i: (i, 0))],
        out_shape=[jax.ShapeDtypeStruct((S, W), BF16), jax.ShapeDtypeStruct((S, vw), BF16)],
        compiler_params=_cp(("parallel",)),
    )(proj, qn, kn, rc, rs, proj)


def _qk_prep_bwd(dqk, proj, qn, kn, rc, rs, name):
    S = proj.shape[0]
    tm = _pick(S, (256, 128))
    W = N_QK_HEADS * ATT_HD

    def body(d_ref, p_ref, qn_ref, kn_ref, rc_ref, rs_ref, dp_ref, dqn_ref, dkn_ref):
        i = pl.program_id(0)
        c, s = rc_ref[...], rs_ref[...]
        parts = [jnp.zeros((1, ATT_HD), F32), jnp.zeros((1, ATT_HD), F32)]
        for h in range(N_QK_HEADS):
            cols = pl.ds(h * ATT_HD, ATT_HD)
            w = qn_ref[...] if h < ATT_QH else kn_ref[...]
            dout = d_ref[:, cols]
            dy = dout * c + pltpu.roll(dout * s, ATT_HD // 2, 1)
            xv = p_ref[:, cols]
            r = lax.rsqrt(jnp.mean(xv * xv, axis=-1, keepdims=True) + NORM_EPS)
            xr = xv * r
            which = 0 if h < ATT_QH else 1
            parts[which] = parts[which] + jnp.sum(dy * xr, axis=0, keepdims=True)
            dxr = dy * w
            dp_ref[:, cols] = r * (dxr - xr * jnp.mean(dxr * xr, axis=-1, keepdims=True))

        @pl.when(i == 0)
        def _():
            dqn_ref[...] = parts[0]
            dkn_ref[...] = parts[1]

        @pl.when(i > 0)
        def _():
            dqn_ref[...] += parts[0]
            dkn_ref[...] += parts[1]

    one = pl.BlockSpec((1, ATT_HD), lambda i: (0, 0))
    tab = pl.BlockSpec((tm, ATT_HD), lambda i: (i, 0))
    row = pl.BlockSpec((tm, W), lambda i: (i, 0))
    return pl.pallas_call(
        body, name=name, grid=(S // tm,),
        in_specs=[row, row, one, one, tab, tab],
        out_specs=[row, one, one],
        out_shape=[jax.ShapeDtypeStruct((S, W), F32), jax.ShapeDtypeStruct((1, ATT_HD), F32),
                   jax.ShapeDtypeStruct((1, ATT_HD), F32)],
        compiler_params=_cp(("arbitrary",)),
    )(dqk, proj, qn, kn, rc, rs)


ATT_TQ = 256
ATT_TK = 512


def _attn_fwd(qk, v, name):
    S = qk.shape[0]
    tq = min(ATT_TQ, S)
    tk = min(ATT_TK, S)
    scale = ATT_HD ** -0.5

    def body(q_ref, k_ref, v_ref, o_ref, lse_ref):
        q = q_ref[...]

        def step(j, carry):
            m, l, acc = carry
            rows = pl.ds(pl.multiple_of(j * tk, tk), tk)
            kc, vc = k_ref[rows, :], v_ref[rows, :]
            s = _dg(q, kc, 1, 1) * scale
            m_new = jnp.maximum(m, jnp.max(s, axis=1, keepdims=True))
            alpha = jnp.exp(m - m_new)
            p = jnp.exp(s - m_new)
            l = alpha * l + jnp.sum(p, axis=1, keepdims=True)
            acc = alpha * acc + jnp.dot(p.astype(BF16), vc, preferred_element_type=F32)
            return m_new, l, acc

        init = (jnp.full((tq, 1), -1e30, F32), jnp.zeros((tq, 1), F32), jnp.zeros((tq, ATT_HD), F32))
        m, l, acc = lax.fori_loop(0, S // tk, step, init)
        o_ref[...] = acc / l
        lse_ref[...] = jnp.broadcast_to(m + jnp.log(l), (tq, ATT_HD))

    qo = pl.BlockSpec((tq, ATT_HD), lambda h, i: (i, h))
    return pl.pallas_call(
        body, name=name, grid=(ATT_QH, S // tq),
        in_specs=[qo, pl.BlockSpec((S, ATT_HD), lambda h, i: (0, ATT_QH + h // ATT_GROUP)),
                  pl.BlockSpec((S, ATT_HD), lambda h, i: (0, h // ATT_GROUP))],
        out_specs=[qo, qo],
        out_shape=[jax.ShapeDtypeStruct((S, ATT_QH * ATT_HD), F32)] * 2,
        compiler_params=_cp(("parallel", "parallel")),
    )(qk, qk, v)


def _attn_bwd(do, o, lse, qk, v, name):
    S = qk.shape[0]
    tq = min(ATT_TQ, S)
    tk = min(ATT_TK, S)
    scale = ATT_HD ** -0.5

    def body(do_ref, o_ref, lse_ref, q_ref, k_ref, v_ref, dq_ref, dk_ref, dv_ref):
        g = pl.program_id(1)
        i = pl.program_id(2)

        @pl.when((g == 0) & (i == 0))
        def _():
            dk_ref[...] = jnp.zeros_like(dk_ref)
            dv_ref[...] = jnp.zeros_like(dv_ref)

        q = q_ref[...]
        dov = do_ref[...]
        dob = dov.astype(BF16)
        delta = jnp.sum(dov * o_ref[...], axis=1, keepdims=True)
        lse_col = lse_ref[:, 0:1]

        def step(j, dq):
            rows = pl.ds(pl.multiple_of(j * tk, tk), tk)
            kc, vc = k_ref[rows, :], v_ref[rows, :]
            s = _dg(q, kc, 1, 1) * scale
            p = jnp.exp(s - lse_col)
            dp = _dg(dob, vc, 1, 1)
            ds = (p * (dp - delta) * scale).astype(BF16)
            dv_ref[rows, :] += _dg(p.astype(BF16), dob, 0, 0)
            dk_ref[rows, :] += _dg(ds, q, 0, 0)
            return dq + jnp.dot(ds, kc, preferred_element_type=F32)

        dq_ref[...] = lax.fori_loop(0, S // tk, step, jnp.zeros((tq, ATT_HD), F32))

    qo = pl.BlockSpec((tq, ATT_HD), lambda kv, g, i: (i, kv * ATT_GROUP + g))
    kvo = pl.BlockSpec((S, ATT_HD), lambda kv, g, i: (0, kv))
    return pl.pallas_call(
        body, name=name, grid=(ATT_KVH, ATT_GROUP, S // tq),
        in_specs=[qo, qo, qo, qo, pl.BlockSpec((S, ATT_HD), lambda kv, g, i: (0, ATT_QH + kv)), kvo],
        out_specs=[qo, kvo, kvo],
        out_shape=[jax.ShapeDtypeStruct((S, ATT_QH * ATT_HD), F32),
                   jax.ShapeDtypeStruct((S, ATT_KVH * ATT_HD), F32),
                   jax.ShapeDtypeStruct((S, ATT_KVH * ATT_HD), F32)],
        compiler_params=_cp(("parallel", "arbitrary", "arbitrary")),
    )(do, o, lse, qk, qk, v)


def _adamw(w, g, m, v, name):
    rows, cols = w.shape
    tr = rows
    for cand in (512, 256, 128, 64, 32, 16, 8):
        if rows % cand == 0 and cand * cols * 4 <= 2 * 1024 * 1024:
            tr = cand
            break

    def body(w_ref, g_ref, m_ref, v_ref, d_ref, nm_ref, nv_ref):
        gv = g_ref[...]
        nm = ADAM_B1 * m_ref[...] + (1.0 - ADAM_B1) * gv
        nv = ADAM_B2 * v_ref[...] + (1.0 - ADAM_B2) * (gv * gv)
        m_hat = nm / (1.0 - ADAM_B1 ** ADAM_STEP)
        v_hat = nv / (1.0 - ADAM_B2 ** ADAM_STEP)
        d_ref[...] = -ADAM_LR * (m_hat / (jnp.sqrt(v_hat) + ADAM_EPS) + ADAM_WD * w_ref[...])
        nm_ref[...] = nm
        nv_ref[...] = nv

    blk = pl.BlockSpec((tr, cols), lambda i: (i, 0))
    return pl.pallas_call(
        body, name=name, grid=(rows // tr,),
        in_specs=[blk] * 4, out_specs=[blk] * 3,
        out_shape=[jax.ShapeDtypeStruct((rows, cols), F32)] * 3,
        compiler_params=_cp(("parallel",)),
    )(w, g, m, v)


ANY = pl.BlockSpec(memory_space=pl.ANY)


def _place():
    return lax.axis_index("x"), lax.axis_index("y"), lax.axis_index("c")


def _other_chips(x, y):
    return [(1 - x, y), (x, 1 - y), (1 - x, 1 - y)]


def _allgather_chips(xp, name):
    R = xp.shape[0]
    H = R // 2

    def body(x_ref, out_ref, send_sems, recv_sems, local_sem):
        x, y, c = _place()
        me = 2 * x + y
        sibling = (x, y, 1 - c)
        chips = _other_chips(x, y)
        mine = pl.ds(pl.multiple_of(c * H, 16), H)
        theirs = pl.ds(pl.multiple_of((1 - c) * H, 16), H)

        def copy(k, src, dst, to):
            return pltpu.make_async_remote_copy(src_ref=src, dst_ref=dst, send_sem=send_sems.at[k],
                                                recv_sem=recv_sems.at[k], device_id=to, device_id_type=MESH)

        own = pltpu.make_async_copy(x_ref, out_ref.at[me], local_sem)
        own.start()
        first = [copy(j, x_ref.at[mine], out_ref.at[me, mine], (px, py, c)) for j, (px, py) in enumerate(chips)]
        for cp in first:
            cp.start()
        passed = []
        for j, (px, py) in enumerate(chips):
            p = 2 * px + py
            copy(j, x_ref.at[mine], out_ref.at[p, mine], (px, py, c)).wait_recv()
            fwd = copy(3 + j, out_ref.at[p, mine], out_ref.at[p, mine], sibling)
            fwd.start()
            passed.append(fwd)
        for j, (px, py) in enumerate(chips):
            p = 2 * px + py
            copy(3 + j, out_ref.at[p, theirs], out_ref.at[p, theirs], sibling).wait_recv()
        for cp in first + passed:
            cp.wait_send()
        own.wait()

    return pl.pallas_call(
        body, name=name,
        in_specs=[ANY], out_specs=ANY,
        out_shape=jax.ShapeDtypeStruct((N_CHIPS, R, LANES), xp.dtype),
        scratch_shapes=[pltpu.SemaphoreType.DMA((6,)), pltpu.SemaphoreType.DMA((6,)), pltpu.SemaphoreType.DMA],
    )(xp)


def _allreduce_small(v, name):
    R = v.shape[0]
    n_dev = 8

    def body(v_ref, sum_ref, all_ref, send_sems, recv_sems, local_sem):
        x, y, c = _place()
        me, sibling = (x, y, c), (x, y, 1 - c)
        chips = _other_chips(x, y)

        def rows(px, py, pc):
            return all_ref.at[pl.ds(pl.multiple_of((4 * px + 2 * py + pc) * R, 8), R), :]

        def copy(k, block, to, src=None):
            return pltpu.make_async_remote_copy(
                src_ref=rows(*block) if src is None else src, dst_ref=rows(*block),
                send_sem=send_sems.at[k], recv_sem=recv_sems.at[k], device_id=to, device_id_type=MESH)

        own = pltpu.make_async_copy(v_ref, rows(*me), local_sem)
        own.start()
        first = [copy(0, me, sibling, src=v_ref)]
        first += [copy(1 + j, me, (*chip, c), src=v_ref) for j, chip in enumerate(chips)]
        for cp in first:
            cp.start()
        passed = [copy(4 + j, (*chip, c), sibling) for j, chip in enumerate(chips)]
        for j, chip in enumerate(chips):
            copy(1 + j, (*chip, c), me).wait_recv()
            passed[j].start()
        copy(0, sibling, me).wait_recv()
        for j, chip in enumerate(chips):
            copy(4 + j, (*chip, 1 - c), me).wait_recv()
        for cp in first + passed:
            cp.wait_send()
        own.wait()
        acc = all_ref[pl.ds(0, R), :]
        for d in range(1, n_dev):
            acc = acc + all_ref[pl.ds(d * R, R), :]
        sum_ref[...] = acc

    vm = pl.BlockSpec(memory_space=pltpu.VMEM)
    return pl.pallas_call(
        body, name=name,
        in_specs=[vm], out_specs=[vm, vm],
        out_shape=[jax.ShapeDtypeStruct((R, LANES), F32), jax.ShapeDtypeStruct((n_dev * R, LANES), F32)],
        scratch_shapes=[pltpu.SemaphoreType.DMA((7,)), pltpu.SemaphoreType.DMA((7,)), pltpu.SemaphoreType.DMA],
    )(v)[0]


def _swap_other_half(g, name):
    R = g.shape[1]
    H = R // 2

    def body(g_ref, got_ref, send_sems, recv_sems):
        x, y, c = _place()
        theirs = pl.ds(pl.multiple_of((1 - c) * H, 16), H)
        copies = [pltpu.make_async_remote_copy(
            src_ref=g_ref.at[p, theirs], dst_ref=got_ref.at[p], send_sem=send_sems.at[p], recv_sem=recv_sems.at[p],
            device_id=(x, y, 1 - c), device_id_type=MESH) for p in range(N_CHIPS)]
        for cp in copies:
            cp.start()
        for cp in copies:
            cp.wait_recv()
        for cp in copies:
            cp.wait_send()

    return pl.pallas_call(
        body, name=name, in_specs=[ANY], out_specs=ANY,
        out_shape=jax.ShapeDtypeStruct((N_CHIPS, H, LANES), g.dtype),
        scratch_shapes=[pltpu.SemaphoreType.DMA((N_CHIPS,)), pltpu.SemaphoreType.DMA((N_CHIPS,))],
    )(g)


def _send_chip_partials(sb, name):
    H = sb.shape[1]

    def body(s_ref, got_ref, send_sems, recv_sems):
        x, y, c = _place()
        me = 2 * x + y
        chips = _other_chips(x, y)
        copies = [pltpu.make_async_remote_copy(
            src_ref=s_ref.at[2 * px + py], dst_ref=got_ref.at[me], send_sem=send_sems.at[j], recv_sem=recv_sems.at[j],
            device_id=(px, py, c), device_id_type=MESH) for j, (px, py) in enumerate(chips)]
        for cp in copies:
            cp.start()
        for j, (px, py) in enumerate(chips):
            pltpu.make_async_remote_copy(
                src_ref=s_ref.at[me], dst_ref=got_ref.at[2 * px + py], send_sem=send_sems.at[j],
                recv_sem=recv_sems.at[j], device_id=(px, py, c), device_id_type=MESH).wait_recv()
        for cp in copies:
            cp.wait_send()

    return pl.pallas_call(
        body, name=name, in_specs=[ANY], out_specs=ANY,
        out_shape=jax.ShapeDtypeStruct(sb.shape, sb.dtype),
        scratch_shapes=[pltpu.SemaphoreType.DMA((3,)), pltpu.SemaphoreType.DMA((3,))],
    )(sb)


def _join_halves(fh, name):
    H = fh.shape[0]

    def body(f_ref, out_ref, send_sem, recv_sem, local_sem):
        x, y, c = _place()
        mine = pl.ds(pl.multiple_of(c * H, 16), H)
        theirs = pl.ds(pl.multiple_of((1 - c) * H, 16), H)
        own = pltpu.make_async_copy(f_ref, out_ref.at[mine], local_sem)
        own.start()
        cp = pltpu.make_async_remote_copy(src_ref=f_ref, dst_ref=out_ref.at[mine], send_sem=send_sem,
                                          recv_sem=recv_sem, device_id=(x, y, 1 - c), device_id_type=MESH)
        cp.start()
        pltpu.make_async_remote_copy(src_ref=f_ref, dst_ref=out_ref.at[theirs], send_sem=send_sem,
                                     recv_sem=recv_sem, device_id=(x, y, 1 - c), device_id_type=MESH).wait_recv()
        cp.wait_send()
        own.wait()

    return pl.pallas_call(
        body, name=name, in_specs=[ANY], out_specs=ANY,
        out_shape=jax.ShapeDtypeStruct((2 * H, LANES), fh.dtype),
        scratch_shapes=[pltpu.SemaphoreType.DMA, pltpu.SemaphoreType.DMA, pltpu.SemaphoreType.DMA],
    )(fh)


RS_ROWS = 1024


def _add_sibling(g, got, c, me, name):
    H = got.shape[1]
    nb = H // RS_ROWS

    def body(sp_ref, g_ref, got_ref, sb_ref, sf_ref):
        p = pl.program_id(1)
        s = g_ref[0] + got_ref[0]
        sb_ref[0] = s.astype(BF16)

        @pl.when(p == sp_ref[1])
        def _():
            sf_ref[...] = s

    grid_spec = pltpu.PrefetchScalarGridSpec(
        num_scalar_prefetch=1, grid=(nb, N_CHIPS),
        in_specs=[pl.BlockSpec((1, RS_ROWS, LANES), lambda i, p, sp: (p, sp[0] * nb + i, 0)),
                  pl.BlockSpec((1, RS_ROWS, LANES), lambda i, p, sp: (p, i, 0))],
        out_specs=[pl.BlockSpec((1, RS_ROWS, LANES), lambda i, p, sp: (p, i, 0)),
                   pl.BlockSpec((RS_ROWS, LANES), lambda i, p, sp: (i, 0))])
    return pl.pallas_call(
        body, name=name, grid_spec=grid_spec,
        out_shape=[jax.ShapeDtypeStruct((N_CHIPS, H, LANES), BF16), jax.ShapeDtypeStruct((H, LANES), F32)],
        compiler_params=_cp(("arbitrary", "arbitrary")),
    )(jnp.stack([c, me]).astype(jnp.int32), g, got)


def _add_chips(sf, got, others, name):
    H = sf.shape[0]
    nb = H // RS_ROWS

    def body(sp_ref, sf_ref, r1_ref, r2_ref, r3_ref, out_ref):
        out_ref[...] = ((sf_ref[...] + r1_ref[0].astype(F32)) + r2_ref[0].astype(F32)) + r3_ref[0].astype(F32)

    def slot(k):
        return pl.BlockSpec((1, RS_ROWS, LANES), lambda i, sp: (sp[k], i, 0))

    blk = pl.BlockSpec((RS_ROWS, LANES), lambda i, sp: (i, 0))
    grid_spec = pltpu.PrefetchScalarGridSpec(
        num_scalar_prefetch=1, grid=(nb,), in_specs=[blk, slot(0), slot(1), slot(2)], out_specs=blk)
    return pl.pallas_call(
        body, name=name, grid_spec=grid_spec,
        out_shape=jax.ShapeDtypeStruct((H, LANES), F32),
        compiler_params=_cp(("arbitrary",)),
    )(others.astype(jnp.int32), sf, got, got, got)


SHARDED = (("gla_w_in", 2), ("gla_w_out", 1), ("attn_w_qkv", 2), ("attn_w_out", 1), ("ffn_w_up", 2),
           ("ffn_w_down", 1), ("gla_w_gate_up_f", 2), ("gla_w_gate_up_b", 2), ("ffn_w_conv", 2))
SHARDED_BF16 = SHARDED[:6]
SHARDED_F32 = SHARDED[6:]
REPLICATED = ("norm_mix", "norm_ffn", "gla_b_gate_f", "gla_b_gate_b", "gla_norm", "attn_q_norm", "attn_k_norm",
              "ffn_b_conv")


PIECE_ROWS = 16


def _piece_rows(shape):
    n = 1
    for s in shape:
        n *= s
    rows = n // LANES
    return rows, -(-rows // PIECE_ROWS) * PIECE_ROWS


def _pack(pieces, dtype, row_multiple):
    flat = []
    for p in pieces:
        rows, padded = _piece_rows(p.shape)
        flat.append(jnp.pad(p.astype(dtype).reshape(rows, LANES), ((0, padded - rows), (0, 0))))
    rows = sum(f.shape[0] for f in flat)
    padded = -(-rows // row_multiple) * row_multiple
    if padded > rows:
        flat.append(jnp.zeros((padded - rows, LANES), dtype))
    return jnp.concatenate(flat, axis=0)


def _unpack(buf, shapes):
    out, r = [], 0
    for shp in shapes:
        rows, padded = _piece_rows(shp)
        out.append(buf[r:r + rows].reshape(shp))
        r += padded
    return out


def _gather_weights(shards, layout, dtype, row_multiple, name):
    packed = _pack([shards[n] for n, _ in layout], dtype, row_multiple)
    allp = _allgather_chips(packed, name)
    shapes = [shards[n].shape for n, _ in layout]
    per_chip = [_unpack(allp[p], shapes) for p in range(N_CHIPS)]
    return {n: jnp.concatenate([per_chip[p][k] for p in range(N_CHIPS)], axis=ax)
            for k, (n, ax) in enumerate(layout)}


def _rope_tables(S):
    rows = S // GRID_W
    row_idx = jnp.repeat(jnp.arange(rows, dtype=F32), GRID_W)
    col_idx = jnp.tile(jnp.arange(GRID_W, dtype=F32), rows)
    pairs = ATT_HD // 4
    inv_freq = ROPE_THETA ** (-jnp.arange(pairs, dtype=F32) / pairs)
    ang = jnp.concatenate([row_idx[:, None] * inv_freq, col_idx[:, None] * inv_freq], axis=-1)
    cos, sin = jnp.cos(ang), jnp.sin(ang)
    return jnp.concatenate([cos, cos], axis=-1), jnp.concatenate([-sin, sin], axis=-1)


def _gate_rows(w, first_row):
    return jnp.zeros((LANES, GLA_KEY), F32).at[first_row:first_row + GLA_RANK].set(w.astype(F32))


def _local_step(x, target, W, P):
    S = x.shape[0]
    rc, rs = _rope_tables(S)
    row = lambda a: a.reshape(1, -1)
    saved = []
    for i in range(DEPTH):
        j = i // 2
        nm = row(P["norm_mix"][i])
        h1 = _rmsnorm_fwd(x, nm, f"norm_mix_fwd{i}")
        if i % 2 == 0:
            win = W["gla_w_in"][j]
            wgf = _gate_rows(W["gla_w_gate_up_f"][j], 0)
            wgb = _gate_rows(W["gla_w_gate_up_b"][j], GLA_RANK)
            bgf, bgb = row(P["gla_b_gate_f"][j]), row(P["gla_b_gate_b"][j])
            gn = row(P["gla_norm"][j])
            proj = _matmul(h1, win, 1, 0, f"gla_in{i}")
            laf, lab = _gla_gate_fwd(proj, wgf, bgf, wgb, bgb, f"gla_gate_fwd{i}")
            of, stf = _gla_scan_fwd(proj, laf, False, f"gla_scan_f_fwd{i}")
            ob, stb = _gla_scan_fwd(proj, lab, True, f"gla_scan_b_fwd{i}")
            z = _gla_out_fwd(of, ob, proj, gn, f"gla_out_fwd{i}")
            xm = _matmul(z, W["gla_w_out"][j], 1, 0, f"gla_outproj{i}", res=x)
            mix = dict(proj=proj, laf=laf, lab=lab, of=of, ob=ob, stf=stf, stb=stb, z=z, wgf=wgf, wgb=wgb)
        else:
            proj = _matmul(h1, W["attn_w_qkv"][j], 1, 0, f"attn_qkv{i}")
            qn, kn = row(P["attn_q_norm"][j]), row(P["attn_k_norm"][j])
            qk, vb = _qk_prep_fwd(proj, qn, kn, rc, rs, f"qk_prep_fwd{i}")
            o, lse = _attn_fwd(qk, vb, f"attn_fwd{i}")
            xm = _matmul(o, W["attn_w_out"][j], 1, 0, f"attn_outproj{i}", res=x)
            mix = dict(proj=proj, qk=qk, vb=vb, o=o, lse=lse)
        h2 = _rmsnorm_fwd(xm, row(P["norm_ffn"][i]), f"norm_ffn_fwd{i}")
        a, uv, ug = _ffn_mid_fwd(h2, W["ffn_w_up"][i], W["ffn_w_conv"][i], row(P["ffn_b_conv"][i]), f"ffn_mid_fwd{i}")
        xo = _matmul(a, W["ffn_w_down"][i], 1, 0, f"ffn_down{i}", res=xm)
        saved.append(dict(x=x, h1=h1, xm=xm, h2=h2, uv=uv, ug=ug, mix=mix))
        x = xo

    dx, dxb, loss = _loss_grad(x, target, "loss")

    G = {n: [None] * (DEPTH if n.startswith(("norm", "ffn")) else DEPTH // 2)
         for n in [n for n, _ in SHARDED] + list(REPLICATED)}
    for i in reversed(range(DEPTH)):
        j = i // 2
        sv = saved[i]
        mix = sv["mix"]
        duv, dug, a, gwv, gwg = _ffn_mid_bwd(dxb, W["ffn_w_down"][i], sv["uv"], sv["ug"], W["ffn_w_conv"][i],
                                             row(P["ffn_b_conv"][i]), f"ffn_mid_bwd{i}")
        G["ffn_w_down"][i] = _matmul(a, dxb, 0, 0, f"ffn_down_wgrad{i}")
        du = jnp.concatenate([duv, dug], axis=1)
        G["ffn_w_up"][i] = _matmul(sv["h2"], du, 0, 0, f"ffn_up_wgrad{i}")
        G["ffn_w_conv"][i] = jnp.concatenate([gwv[:3], gwg[:3]], axis=1)
        G["ffn_b_conv"][i] = jnp.concatenate([gwv[3], gwg[3]], axis=0)
        dxm, dxmb, dn = _dgrad_norm(du, W["ffn_w_up"][i], sv["xm"], row(P["norm_ffn"][i]), dx, f"ffn_up_dgrad{i}")
        G["norm_ffn"][i] = dn[0]
        if i % 2 == 0:
            proj = mix["proj"]
            bgf, bgb = row(P["gla_b_gate_f"][j]), row(P["gla_b_gate_b"][j])
            gn = row(P["gla_norm"][j])
            dz = _matmul(dxmb, W["gla_w_out"][j], 1, 1, f"gla_outproj_dgrad{i}")
            G["gla_w_out"][j] = _matmul(mix["z"], dxmb, 0, 0, f"gla_outproj_wgrad{i}")
            do, dg, dgn = _gla_out_bwd(dz, mix["of"], mix["ob"], proj, gn, f"gla_out_bwd{i}")
            G["gla_norm"][j] = dgn[0]
            dqf, dkf, dvf, dlaf = _gla_scan_bwd(do, proj, mix["laf"], mix["stf"], False, f"gla_scan_f_bwd{i}")
            dqb, dkb, dvb, dlab = _gla_scan_bwd(do, proj, mix["lab"], mix["stb"], True, f"gla_scan_b_bwd{i}")
            dr, dwf, dbf, dwb, dbb = _gla_gate_bwd(dlaf, dlab, proj, mix["wgf"], bgf, mix["wgb"], bgb,
                                                   f"gla_gate_bwd{i}")
            G["gla_w_gate_up_f"][j] = dwf[:GLA_RANK]
            G["gla_w_gate_up_b"][j] = dwb[GLA_RANK:2 * GLA_RANK]
            G["gla_b_gate_f"][j] = dbf[0]
            G["gla_b_gate_b"][j] = dbb[0]
            dproj = jnp.concatenate([dqf + dqb, dkf + dkb, dvf + dvb, dg, dr], axis=1).astype(BF16)
            G["gla_w_in"][j] = _matmul(sv["h1"], dproj, 0, 0, f"gla_in_wgrad{i}")[:, :GLA_IN]
            wmix = W["gla_w_in"][j]
        else:
            proj = mix["proj"]
            qn, kn = row(P["attn_q_norm"][j]), row(P["attn_k_norm"][j])
            do = _matmul(dxmb, W["attn_w_out"][j], 1, 1, f"attn_outproj_dgrad{i}")
            G["attn_w_out"][j] = _matmul(mix["o"], dxmb, 0, 0, f"attn_outproj_wgrad{i}")
            dq, dk, dv = _attn_bwd(do, mix["o"], mix["lse"], mix["qk"], mix["vb"], f"attn_bwd{i}")
            dqk = jnp.concatenate([dq, dk], axis=1)
            dpqk, dqn, dkn = _qk_prep_bwd(dqk, proj, qn, kn, rc, rs, f"qk_prep_bwd{i}")
            G["attn_q_norm"][j] = dqn[0]
            G["attn_k_norm"][j] = dkn[0]
            dproj = jnp.concatenate([dpqk, dv], axis=1).astype(BF16)
            G["attn_w_qkv"][j] = _matmul(sv["h1"], dproj, 0, 0, f"attn_qkv_wgrad{i}")
            wmix = W["attn_w_qkv"][j]
        dx, dxb, dn = _dgrad_norm(dproj, wmix, sv["x"], row(P["norm_mix"][i]), dxm, f"mix_in_dgrad{i}")
        G["norm_mix"][i] = dn[0]
    grads = {n: jnp.stack(v) for n, v in G.items()}
    return loss, dx, grads


def _pad_gla_in(w):
    return jnp.pad(w, ((0, 0), (0, 0), (0, GLA_IN_PAD - GLA_IN)))


def kernel(x, norm_mix, norm_ffn, gla_w_in, gla_w_gate_up_f, gla_b_gate_f, gla_w_gate_up_b, gla_b_gate_b, gla_norm, gla_w_out, attn_w_qkv, attn_q_norm, attn_k_norm, attn_w_out, ffn_w_up, ffn_w_conv, ffn_b_conv, ffn_w_down, loss_target, m_norm_mix, m_norm_ffn, m_gla_w_in, m_gla_w_gate_up_f, m_gla_b_gate_f, m_gla_w_gate_up_b, m_gla_b_gate_b, m_gla_norm, m_gla_w_out, m_attn_w_qkv, m_attn_q_norm, m_attn_k_norm, m_attn_w_out, m_ffn_w_up, m_ffn_w_conv, m_ffn_b_conv, m_ffn_w_down, v_norm_mix, v_norm_ffn, v_gla_w_in, v_gla_w_gate_up_f, v_gla_b_gate_f, v_gla_w_gate_up_b, v_gla_b_gate_b, v_gla_norm, v_gla_w_out, v_attn_w_qkv, v_attn_q_norm, v_attn_k_norm, v_attn_w_out, v_ffn_w_up, v_ffn_w_conv, v_ffn_b_conv, v_ffn_w_down):
    names = ("norm_mix", "norm_ffn", "gla_w_in", "gla_w_gate_up_f", "gla_b_gate_f", "gla_w_gate_up_b",
             "gla_b_gate_b", "gla_norm", "gla_w_out", "attn_w_qkv", "attn_q_norm", "attn_k_norm", "attn_w_out",
             "ffn_w_up", "ffn_w_conv", "ffn_b_conv", "ffn_w_down")
    w = dict(zip(names, (norm_mix, norm_ffn, gla_w_in, gla_w_gate_up_f, gla_b_gate_f, gla_w_gate_up_b,
                         gla_b_gate_b, gla_norm, gla_w_out, attn_w_qkv, attn_q_norm, attn_k_norm, attn_w_out,
                         ffn_w_up, ffn_w_conv, ffn_b_conv, ffn_w_down)))
    m = dict(zip(names, (m_norm_mix, m_norm_ffn, m_gla_w_in, m_gla_w_gate_up_f, m_gla_b_gate_f,
                         m_gla_w_gate_up_b, m_gla_b_gate_b, m_gla_norm, m_gla_w_out, m_attn_w_qkv, m_attn_q_norm,
                         m_attn_k_norm, m_attn_w_out, m_ffn_w_up, m_ffn_w_conv, m_ffn_b_conv, m_ffn_w_down)))
    v = dict(zip(names, (v_norm_mix, v_norm_ffn, v_gla_w_in, v_gla_w_gate_up_f, v_gla_b_gate_f,
                         v_gla_w_gate_up_b, v_gla_b_gate_b, v_gla_norm, v_gla_w_out, v_attn_w_qkv, v_attn_q_norm,
                         v_attn_k_norm, v_attn_w_out, v_ffn_w_up, v_ffn_w_conv, v_ffn_b_conv, v_ffn_w_down)))
    px, py, pc = _place()
    me = 2 * px + py

    W = _gather_weights(w, SHARDED_BF16, BF16, 32, "gather_weights_bf16")
    W.update(_gather_weights(w, SHARDED_F32, F32, 32, "gather_weights_f32"))
    W["gla_w_in"] = _pad_gla_in(W["gla_w_in"])
    P = {n: w[n] for n in REPLICATED}

    loss_part, dx, grads = _local_step(x[0], loss_target[0], W, P)

    pieces = []
    for p in range(N_CHIPS):
        shard_p = []
        for n, ax in SHARDED:
            width = w[n].shape[ax]
            shard_p.append(lax.slice_in_dim(grads[n], p * width, (p + 1) * width, axis=ax))
        pieces.append(_pack(shard_p, F32, 2 * RS_ROWS))
    g_all = jnp.stack(pieces)
    got = _swap_other_half(g_all, "grads_to_sibling")
    sb, sf = _add_sibling(g_all, got, pc, me, "grads_add_sibling")
    got_b = _send_chip_partials(sb, "grads_to_chips")
    others = jnp.stack([jnp.where(me <= k, k + 1, k) for k in range(N_CHIPS - 1)])
    fh = _add_chips(sf, got_b, others, "grads_add_chips")
    g_mine = _join_halves(fh, "grads_join_halves")
    gsh = dict(zip([n for n, _ in SHARDED], _unpack(g_mine, [w[n].shape for n, _ in SHARDED])))

    small = _pack([grads[n] for n in REPLICATED] + [loss_part], F32, 16)
    small_sum = _allreduce_small(small, "small_allreduce")
    parts = _unpack(small_sum, [w[n].shape for n in REPLICATED] + [(1, LANES)])
    gsh.update(dict(zip(REPLICATED, parts[:-1])))
    loss = parts[-1][0, 0]

    delta, new_m, new_v = {}, {}, {}
    for n in names:
        shp = w[n].shape
        two_d = (-1, shp[-1])
        d, nm, nv = _adamw(w[n].reshape(two_d), gsh[n].reshape(two_d), m[n].reshape(two_d), v[n].reshape(two_d),
                           f"adamw_{n}")
        delta[n], new_m[n], new_v[n] = d.reshape(shp), nm.reshape(shp), nv.reshape(shp)

    return (loss, dx[None], *[gsh[n] for n in names], *[delta[n] for n in names],
            *[new_m[n] for n in names], *[new_v[n] for n in names])
```

```python
import functools

import jax
import jax.numpy as jnp
from jax import lax
from jax.experimental import pallas as pl
from jax.experimental.pallas import tpu as pltpu

F32 = jnp.float32
BF16 = jnp.bfloat16
MESH = pl.DeviceIdType.MESH
HIGHEST = lax.Precision.HIGHEST

D_MODEL = 1024
DEPTH = 4
GRID_W = 64
NORM_EPS = 1e-6
GLA_HEADS = 4
GLA_DK = 128
GLA_DV = 256
GLA_KEY = GLA_HEADS * GLA_DK
GLA_VAL = GLA_HEADS * GLA_DV
GLA_RANK = 16
GLA_CHUNK = 64
GLA_GATE_NORMALIZER = 16.0
GLA_IN = 2 * GLA_KEY + 2 * GLA_VAL + 2 * GLA_RANK
GLA_IN_PAD = 3200
GLA_R_BLOCK = (2 * GLA_KEY + 2 * GLA_VAL) // 128
ATT_HD = 128
ATT_QH = 8
ATT_KVH = 2
ATT_GROUP = ATT_QH // ATT_KVH
ATT_QKV = (ATT_QH + 2 * ATT_KVH) * ATT_HD
ROPE_THETA = 10000.0
D_FF = 2816
ADAM_LR = 0.001
ADAM_B1 = 0.9
ADAM_B2 = 0.999
ADAM_EPS = 1e-08
ADAM_WD = 0.01
ADAM_STEP = 10

N_CHIPS = 4
LANES = 128
VMEM_LIMIT = 48 * 1024 * 1024


def _cp(sem):
    return pltpu.CompilerParams(dimension_semantics=sem, vmem_limit_bytes=VMEM_LIMIT)


def _pick(n, cands):
    for c in cands:
        if n % c == 0:
            return c
    return n


def _dg(a, b, ca, cb):
    return lax.dot_general(a, b, (((ca,), (cb,)), ((), ())), preferred_element_type=F32)


def _sigmoid(x):
    return 1.0 / (1.0 + jnp.exp(-x))


def _rmsnorm_fwd(x, w, name):
    S, D = x.shape
    tm = _pick(S, (512, 256))

    def body(x_ref, w_ref, h_ref):
        xv = x_ref[...]
        r = lax.rsqrt(jnp.mean(xv * xv, axis=-1, keepdims=True) + NORM_EPS)
        h_ref[...] = (xv * r * w_ref[...]).astype(BF16)

    return pl.pallas_call(
        body, name=name, grid=(S // tm,),
        in_specs=[pl.BlockSpec((tm, D), lambda i: (i, 0)), pl.BlockSpec((1, D), lambda i: (0, 0))],
        out_specs=pl.BlockSpec((tm, D), lambda i: (i, 0)),
        out_shape=jax.ShapeDtypeStruct((S, D), BF16),
        compiler_params=_cp(("parallel",)),
    )(x, w)


def _loss_grad(y, t, name):
    S, D = y.shape
    tm = _pick(S, (512, 256))

    def body(y_ref, t_ref, dy_ref, dyb_ref, loss_ref):
        i = pl.program_id(0)
        d = y_ref[...] - t_ref[...]
        dy = d * (1.0 / D)
        dy_ref[...] = dy
        dyb_ref[...] = dy.astype(BF16)
        sq = jnp.sum(jnp.sum(d * d, axis=1, keepdims=True), axis=0, keepdims=True)
        part = jnp.broadcast_to(sq * (0.5 / D), (1, LANES))

        @pl.when(i == 0)
        def _():
            loss_ref[...] = part

        @pl.when(i > 0)
        def _():
            loss_ref[...] += part

    return pl.pallas_call(
        body, name=name, grid=(S // tm,),
        in_specs=[pl.BlockSpec((tm, D), lambda i: (i, 0)), pl.BlockSpec((tm, D), lambda i: (i, 0))],
        out_specs=[pl.BlockSpec((tm, D), lambda i: (i, 0)), pl.BlockSpec((tm, D), lambda i: (i, 0)),
                   pl.BlockSpec((1, LANES), lambda i: (0, 0))],
        out_shape=[jax.ShapeDtypeStruct((S, D), F32), jax.ShapeDtypeStruct((S, D), BF16),
                   jax.ShapeDtypeStruct((1, LANES), F32)],
        compiler_params=_cp(("arbitrary",)),
    )(y, t)


def _matmul(a, b, ca, cb, name, res=None, out_dtype=F32):
    M, K = a.shape[1 - ca], a.shape[ca]
    N = b.shape[1 - cb]
    assert b.shape[cb] == K
    tm = _pick(M, (1024, 1408, 512, 256, 128))
    tn = _pick(N, (1024, 1408, 768, 640, 512, 256, 128))
    tk = _pick(K, (512, 1408, 256, 128))
    nk = K // tk
    if ca == 1:
        a_spec = pl.BlockSpec((tm, tk), lambda i, j, k: (i, k))
    else:
        a_spec = pl.BlockSpec((tk, tm), lambda i, j, k: (k, i))
    if cb == 0:
        b_spec = pl.BlockSpec((tk, tn), lambda i, j, k: (k, j))
    else:
        b_spec = pl.BlockSpec((tn, tk), lambda i, j, k: (j, k))
    o_spec = pl.BlockSpec((tm, tn), lambda i, j, k: (i, j))
    has_res = res is not None

    def body(*refs):
        if has_res:
            a_ref, b_ref, r_ref, o_ref, acc = refs
        else:
            a_ref, b_ref, o_ref, acc = refs
        k = pl.program_id(2)

        @pl.when(k == 0)
        def _():
            acc[...] = jnp.zeros_like(acc)

        acc[...] += _dg(a_ref[...].astype(BF16), b_ref[...].astype(BF16), ca, cb)

        @pl.when(k == nk - 1)
        def _():
            v = acc[...]
            if has_res:
                v = v + r_ref[...]
            o_ref[...] = v.astype(out_dtype)

    in_specs = [a_spec, b_spec] + ([o_spec] if has_res else [])
    args = (a, b) + ((res,) if has_res else ())
    return pl.pallas_call(
        body, name=name, grid=(M // tm, N // tn, nk),
        in_specs=in_specs, out_specs=o_spec,
        out_shape=jax.ShapeDtypeStruct((M, N), out_dtype),
        scratch_shapes=[pltpu.VMEM((tm, tn), F32)],
        compiler_params=_cp(("parallel", "parallel", "arbitrary")),
    )(*args)


def _dgrad_norm(dy, w, x, wn, dres, name):
    M, Kp = dy.shape
    D = w.shape[0]
    tm = _pick(M, (512, 256, 128))
    tk = _pick(Kp, (1408, 768, 640, 512, 256, 128))
    nk = Kp // tk

    def body(dy_ref, w_ref, x_ref, wn_ref, dres_ref, dx_ref, dxb_ref, dwn_ref, acc):
        i = pl.program_id(0)
        k = pl.program_id(1)

        @pl.when(k == 0)
        def _():
            acc[...] = jnp.zeros_like(acc)

        acc[...] += _dg(dy_ref[...], w_ref[...], 1, 1)

        @pl.when(k == nk - 1)
        def _():
            dh = acc[...]
            xv = x_ref[...]
            r = lax.rsqrt(jnp.mean(xv * xv, axis=-1, keepdims=True) + NORM_EPS)
            yv = xv * r
            dyv = dh * wn_ref[...]
            dxv = r * (dyv - yv * jnp.mean(dyv * yv, axis=-1, keepdims=True)) + dres_ref[...]
            dx_ref[...] = dxv
            dxb_ref[...] = dxv.astype(BF16)
            part = jnp.sum(dh * yv, axis=0, keepdims=True)

            @pl.when(i == 0)
            def _():
                dwn_ref[...] = part

            @pl.when(i > 0)
            def _():
                dwn_ref[...] += part

    row = pl.BlockSpec((tm, D), lambda i, k: (i, 0))
    return pl.pallas_call(
        body, name=name, grid=(M // tm, nk),
        in_specs=[pl.BlockSpec((tm, tk), lambda i, k: (i, k)), pl.BlockSpec((D, tk), lambda i, k: (0, k)),
                  row, pl.BlockSpec((1, D), lambda i, k: (0, 0)), row],
        out_specs=[row, row, pl.BlockSpec((1, D), lambda i, k: (0, 0))],
        out_shape=[jax.ShapeDtypeStruct((M, D), F32), jax.ShapeDtypeStruct((M, D), BF16),
                   jax.ShapeDtypeStruct((1, D), F32)],
        scratch_shapes=[pltpu.VMEM((tm, D), F32)],
        compiler_params=_cp(("arbitrary", "arbitrary")),
    )(dy, w, x, wn, dres)


FFN_TN = 128
FFN_ROWS = 256
PAD = 8


def _conv3(pad_ref, w, r0, tr):
    um = pad_ref[pl.ds(PAD - 1 + r0, tr), :]
    uc = pad_ref[pl.ds(PAD + r0, tr), :]
    up = pad_ref[pl.ds(PAD + 1 + r0, tr), :]
    return w[0:1, :] * um + w[1:2, :] * uc + w[2:3, :] * up, (um, uc, up)


def _zero_pads(pad_ref, S, tn):
    pad_ref[pl.ds(0, PAD), :] = jnp.zeros((PAD, tn), F32)
    pad_ref[pl.ds(PAD + S, PAD), :] = jnp.zeros((PAD, tn), F32)


def _ffn_mid_fwd(h, wup, wconv, bconv, name):
    S, D = h.shape
    F = wup.shape[1] // 2
    tn = FFN_TN
    nb = F // tn
    tr = min(FFN_ROWS, S)

    def body(h_ref, wv_ref, wg_ref, cv_ref, cg_ref, bv_ref, bg_ref, a_ref, uv_ref, ug_ref, padv, padg):
        _zero_pads(padv, S, tn)
        _zero_pads(padg, S, tn)
        hv = h_ref[...]
        padv[pl.ds(PAD, S), :] = jnp.dot(hv, wv_ref[...], preferred_element_type=F32)
        padg[pl.ds(PAD, S), :] = jnp.dot(hv, wg_ref[...], preferred_element_type=F32)
        uv_ref[...] = padv[pl.ds(PAD, S), :]
        ug_ref[...] = padg[pl.ds(PAD, S), :]
        cwv, cwg, bv, bg = cv_ref[...], cg_ref[...], bv_ref[...], bg_ref[...]
        for r0 in range(0, S, tr):
            cv = _conv3(padv, cwv, r0, tr)[0] + bv
            cg = _conv3(padg, cwg, r0, tr)[0] + bg
            a_ref[pl.ds(r0, tr), :] = (cg * _sigmoid(cg) * cv).astype(BF16)

    col = lambda off: (lambda j: (0, j + off))
    return pl.pallas_call(
        body, name=name, grid=(nb,),
        in_specs=[pl.BlockSpec((S, D), lambda j: (0, 0)),
                  pl.BlockSpec((D, tn), col(0)), pl.BlockSpec((D, tn), col(nb)),
                  pl.BlockSpec((3, tn), col(0)), pl.BlockSpec((3, tn), col(nb)),
                  pl.BlockSpec((1, tn), col(0)), pl.BlockSpec((1, tn), col(nb))],
        out_specs=[pl.BlockSpec((S, tn), col(0))] * 3,
        out_shape=[jax.ShapeDtypeStruct((S, F), BF16), jax.ShapeDtypeStruct((S, F), F32),
                   jax.ShapeDtypeStruct((S, F), F32)],
        scratch_shapes=[pltpu.VMEM((S + 2 * PAD, tn), F32)] * 2,
        compiler_params=_cp(("parallel",)),
    )(h, wup, wup, wconv, wconv, bconv, bconv)


def _rows8(rows):
    n = rows[0].shape[1]
    idx = lax.broadcasted_iota(jnp.int32, (8, n), 0)
    out = jnp.zeros((8, n), F32)
    for k, r in enumerate(rows):
        out = jnp.where(idx == k, r, out)
    return out


def _ffn_mid_bwd(dyb, wdown, uv, ug, wconv, bconv, name):
    S, D = dyb.shape
    F = wdown.shape[0]
    tn = FFN_TN
    nb = F // tn
    tr = min(FFN_ROWS, S)

    def body(dy_ref, wd_ref, uv_ref, ug_ref, cv_ref, cg_ref, bv_ref, bg_ref,
             duv_ref, dug_ref, a_ref, gwv_ref, gwg_ref, da_s, padv, padg, pdv, pdg):
        for p in (padv, padg, pdv, pdg):
            _zero_pads(p, S, tn)
        da_s[...] = _dg(dy_ref[...], wd_ref[...], 1, 1)
        padv[pl.ds(PAD, S), :] = uv_ref[...]
        padg[pl.ds(PAD, S), :] = ug_ref[...]
        cwv, cwg, bv, bg = cv_ref[...], cg_ref[...], bv_ref[...], bg_ref[...]
        zero = jnp.zeros((1, tn), F32)
        gv = [zero, zero, zero, zero]
        gg = [zero, zero, zero, zero]
        for r0 in range(0, S, tr):
            cv, shv = _conv3(padv, cwv, r0, tr)
            cg, shg = _conv3(padg, cwg, r0, tr)
            cv = cv + bv
            cg = cg + bg
            sg = _sigmoid(cg)
            sl = cg * sg
            a_ref[pl.ds(r0, tr), :] = (sl * cv).astype(BF16)
            da = da_s[pl.ds(r0, tr), :]
            dcv = da * sl
            dcg = da * cv * (sg * (1.0 + cg * (1.0 - sg)))
            pdv[pl.ds(PAD + r0, tr), :] = dcv
            pdg[pl.ds(PAD + r0, tr), :] = dcg
            for k in range(3):
                gv[k] = gv[k] + jnp.sum(dcv * shv[k], axis=0, keepdims=True)
                gg[k] = gg[k] + jnp.sum(dcg * shg[k], axis=0, keepdims=True)
            gv[3] = gv[3] + jnp.sum(dcv, axis=0, keepdims=True)
            gg[3] = gg[3] + jnp.sum(dcg, axis=0, keepdims=True)
        gwv_ref[...] = _rows8(gv)
        gwg_ref[...] = _rows8(gg)
        for r0 in range(0, S, tr):
            for pd, cw, out in ((pdv, cwv, duv_ref), (pdg, cwg, dug_ref)):
                dm = pd[pl.ds(PAD - 1 + r0, tr), :]
                dc = pd[pl.ds(PAD + r0, tr), :]
                dp = pd[pl.ds(PAD + 1 + r0, tr), :]
                out[pl.ds(r0, tr), :] = (cw[0:1, :] * dp + cw[1:2, :] * dc + cw[2:3, :] * dm).astype(BF16)

    col = lambda off: (lambda j: (0, j + off))
    blk = pl.BlockSpec((S, tn), col(0))
    g8 = pl.BlockSpec((8, tn), col(0))
    return pl.pallas_call(
        body, name=name, grid=(nb,),
        in_specs=[pl.BlockSpec((S, D), lambda j: (0, 0)), pl.BlockSpec((tn, D), lambda j: (j, 0)), blk, blk,
                  pl.BlockSpec((3, tn), col(0)), pl.BlockSpec((3, tn), col(nb)),
                  pl.BlockSpec((1, tn), col(0)), pl.BlockSpec((1, tn), col(nb))],
        out_specs=[blk, blk, blk, g8, g8],
        out_shape=[jax.ShapeDtypeStruct((S, F), BF16), jax.ShapeDtypeStruct((S, F), BF16),
                   jax.ShapeDtypeStruct((S, F), BF16), jax.ShapeDtypeStruct((8, F), F32),
                   jax.ShapeDtypeStruct((8, F), F32)],
        scratch_shapes=[pltpu.VMEM((S, tn), F32)] + [pltpu.VMEM((S + 2 * PAD, tn), F32)] * 4,
        compiler_params=_cp(("parallel",)),
    )(dyb, wdown, uv, ug, wconv, wconv, bconv, bconv)


def _log_sigmoid(x):
    return jnp.minimum(x, 0.0) - jnp.log(1.0 + jnp.exp(-jnp.abs(x)))


def _gla_gate_fwd(proj, wgf, bgf, wgb, bgb, name):
    S = proj.shape[0]
    tm = _pick(S, (512, 256))

    def body(r_ref, wf_ref, bf_ref, wb_ref, bb_ref, laf_ref, lab_ref):
        r = r_ref[...].astype(BF16)
        lf = jnp.dot(r, wf_ref[...].astype(BF16), preferred_element_type=F32) + bf_ref[...]
        lb = jnp.dot(r, wb_ref[...].astype(BF16), preferred_element_type=F32) + bb_ref[...]
        laf_ref[...] = _log_sigmoid(lf) * (1.0 / GLA_GATE_NORMALIZER)
        lab_ref[...] = _log_sigmoid(lb) * (1.0 / GLA_GATE_NORMALIZER)

    full = lambda shp: pl.BlockSpec(shp, lambda i: (0, 0))
    row = pl.BlockSpec((tm, GLA_KEY), lambda i: (i, 0))
    return pl.pallas_call(
        body, name=name, grid=(S // tm,),
        in_specs=[pl.BlockSpec((tm, LANES), lambda i: (i, GLA_R_BLOCK)),
                  full((LANES, GLA_KEY)), full((1, GLA_KEY)), full((LANES, GLA_KEY)), full((1, GLA_KEY))],
        out_specs=[row, row],
        out_shape=[jax.ShapeDtypeStruct((S, GLA_KEY), F32)] * 2,
        compiler_params=_cp(("parallel",)),
    )(proj, wgf, bgf, wgb, bgb)


def _gla_gate_bwd(dlaf, dlab, proj, wgf, bgf, wgb, bgb, name):
    S = proj.shape[0]
    tm = _pick(S, (512, 256))

    def body(dlf_ref, dlb_ref, r_ref, wf_ref, bf_ref, wb_ref, bb_ref, dr_ref, dwf_ref, dbf_ref, dwb_ref, dbb_ref):
        i = pl.program_id(0)
        r = r_ref[...].astype(BF16)
        wf = wf_ref[...].astype(BF16)
        wb = wb_ref[...].astype(BF16)
        lf = jnp.dot(r, wf, preferred_element_type=F32) + bf_ref[...]
        lb = jnp.dot(r, wb, preferred_element_type=F32) + bb_ref[...]
        glf = dlf_ref[...] * (1.0 / GLA_GATE_NORMALIZER) * (1.0 / (1.0 + jnp.exp(lf)))
        glb = dlb_ref[...] * (1.0 / GLA_GATE_NORMALIZER) * (1.0 / (1.0 + jnp.exp(lb)))
        gfb = glf.astype(BF16)
        gbb = glb.astype(BF16)
        dr_ref[...] = _dg(gfb, wf, 1, 1) + _dg(gbb, wb, 1, 1)
        parts = (_dg(r, gfb, 0, 0), jnp.sum(glf, axis=0, keepdims=True),
                 _dg(r, gbb, 0, 0), jnp.sum(glb, axis=0, keepdims=True))
        outs = (dwf_ref, dbf_ref, dwb_ref, dbb_ref)

        @pl.when(i == 0)
        def _():
            for o, p in zip(outs, parts):
                o[...] = p

        @pl.when(i > 0)
        def _():
            for o, p in zip(outs, parts):
                o[...] += p

    full = lambda shp: pl.BlockSpec(shp, lambda i: (0, 0))
    row = pl.BlockSpec((tm, GLA_KEY), lambda i: (i, 0))
    return pl.pallas_call(
        body, name=name, grid=(S // tm,),
        in_specs=[row, row, pl.BlockSpec((tm, LANES), lambda i: (i, GLA_R_BLOCK)),
                  full((LANES, GLA_KEY)), full((1, GLA_KEY)), full((LANES, GLA_KEY)), full((1, GLA_KEY))],
        out_specs=[pl.BlockSpec((tm, LANES), lambda i: (i, 0)),
                   full((LANES, GLA_KEY)), full((1, GLA_KEY)), full((LANES, GLA_KEY)), full((1, GLA_KEY))],
        out_shape=[jax.ShapeDtypeStruct((S, LANES), F32),
                   jax.ShapeDtypeStruct((LANES, GLA_KEY), F32), jax.ShapeDtypeStruct((1, GLA_KEY), F32),
                   jax.ShapeDtypeStruct((LANES, GLA_KEY), F32), jax.ShapeDtypeStruct((1, GLA_KEY), F32)],
        compiler_params=_cp(("arbitrary",)),
    )(dlaf, dlab, proj, wgf, bgf, wgb, bgb)


def _gla_masks(rev):
    C = GLA_CHUNK
    t = lax.broadcasted_iota(jnp.int32, (C, C), 0)
    s = lax.broadcasted_iota(jnp.int32, (C, C), 1)
    if rev:
        return (s >= t), (s > t), (t >= s), (t > s)
    return (s <= t), (s <= t), (t <= s), (t <= s)


def _gla_chunk_common(q, k, la, cum, end_row):
    b = jnp.dot(cum.astype(F32), la, precision=HIGHEST, preferred_element_type=F32)
    bend = b[end_row:end_row + 1, :]
    e = jnp.exp(b)
    qd = q * (GLA_DK ** -0.5) * e
    ei = jnp.exp(-b)
    ee = jnp.exp(bend - b)
    d = jnp.exp(bend)
    return e, ei, ee, d, qd, k * ei, k * ee


GLA_CB = 8


def _gla_specs(S, rev_order):
    n = S // GLA_CHUNK
    cb = min(GLA_CB, n)
    nblk = n // cb
    rows = cb * GLA_CHUNK
    ci = (lambda i: nblk - 1 - i) if rev_order else (lambda i: i)
    q_spec = pl.BlockSpec((rows, GLA_DK), lambda h, i: (ci(i), h))
    k_spec = pl.BlockSpec((rows, GLA_DK), lambda h, i: (ci(i), GLA_HEADS + h))
    v_spec = pl.BlockSpec((rows, GLA_DV), lambda h, i: (ci(i), GLA_KEY * 2 // GLA_DV + h))
    la_spec = pl.BlockSpec((rows, GLA_DK), lambda h, i: (ci(i), h))
    o_spec = pl.BlockSpec((rows, GLA_DV), lambda h, i: (ci(i), h))
    st_spec = pl.BlockSpec((1, cb, GLA_DV, GLA_DK), lambda h, i: (h, ci(i), 0, 0))
    return n, cb, nblk, q_spec, k_spec, v_spec, la_spec, o_spec, st_spec


def _gla_scan_fwd(proj, la, rev, name):
    S = proj.shape[0]
    C = GLA_CHUNK
    n, cb, nblk, q_spec, k_spec, v_spec, la_spec, o_spec, st_spec = _gla_specs(S, rev)
    end_row = 0 if rev else C - 1
    order = list(range(cb))[::-1] if rev else list(range(cb))

    def body(q_ref, k_ref, v_ref, la_ref, o_ref, st_ref, state):
        i = pl.program_id(1)

        @pl.when(i == 0)
        def _():
            state[...] = jnp.zeros_like(state)

        cum, mask, _, _ = _gla_masks(rev)
        st = state[...]
        for cc in order:
            rows = pl.ds(cc * C, C)
            q, k, v, lav = q_ref[rows, :], k_ref[rows, :], v_ref[rows, :], la_ref[rows, :]
            _, _, _, d, qd, ki, ke = _gla_chunk_common(q, k, lav, cum, end_row)
            qdb, kib, keb, vb = qd.astype(BF16), ki.astype(BF16), ke.astype(BF16), v.astype(BF16)
            att = jnp.where(mask, _dg(qdb, kib, 1, 1), 0.0)
            o_ref[rows, :] = (jnp.dot(att.astype(BF16), vb, preferred_element_type=F32)
                              + _dg(qdb, st.astype(BF16), 1, 1))
            st_ref[0, cc] = st
            st = st * d + _dg(vb, keb, 0, 0)
        state[...] = st

    return pl.pallas_call(
        body, name=name, grid=(GLA_HEADS, nblk),
        in_specs=[q_spec, k_spec, v_spec, la_spec],
        out_specs=[o_spec, st_spec],
        out_shape=[jax.ShapeDtypeStruct((S, GLA_VAL), F32),
                   jax.ShapeDtypeStruct((GLA_HEADS, n, GLA_DV, GLA_DK), F32)],
        scratch_shapes=[pltpu.VMEM((GLA_DV, GLA_DK), F32)],
        compiler_params=_cp(("parallel", "arbitrary")),
    )(proj, proj, proj, la)


def _gla_scan_bwd(do, proj, la, states, rev, name):
    S = proj.shape[0]
    C = GLA_CHUNK
    n, cb, nblk, q_spec, k_spec, v_spec, la_spec, o_spec, st_spec = _gla_specs(S, not rev)
    end_row = 0 if rev else C - 1
    order = list(range(cb)) if rev else list(range(cb))[::-1]

    def body(do_ref, q_ref, k_ref, v_ref, la_ref, st_ref, dq_ref, dk_ref, dv_ref, dla_ref, gstate):
        i = pl.program_id(1)

        @pl.when(i == 0)
        def _():
            gstate[...] = jnp.zeros_like(gstate)

        cum, mask, cum_t, mask_t = _gla_masks(rev)
        g = gstate[...]
        for cc in order:
            rows = pl.ds(cc * C, C)
            q, k, v, lav = q_ref[rows, :], k_ref[rows, :], v_ref[rows, :], la_ref[rows, :]
            dov = do_ref[rows, :]
            st = st_ref[0, cc]
            e, ei, ee, d, qd, ki, ke = _gla_chunk_common(q, k, lav, cum, end_row)
            qdb, kib, keb, vb = qd.astype(BF16), ki.astype(BF16), ke.astype(BF16), v.astype(BF16)
            dob, gb, stb = dov.astype(BF16), g.astype(BF16), st.astype(BF16)
            att_t = jnp.where(mask_t, _dg(kib, qdb, 1, 1), 0.0)
            da = jnp.where(mask, _dg(dob, vb, 1, 1), 0.0)
            da_t = jnp.where(mask_t, _dg(vb, dob, 1, 1), 0.0)
            dv_ref[rows, :] = jnp.dot(att_t.astype(BF16), dob, preferred_element_type=F32) + _dg(keb, gb, 1, 1)
            dqd = (jnp.dot(da.astype(BF16), kib, preferred_element_type=F32)
                   + jnp.dot(dob, stb, preferred_element_type=F32))
            dki = jnp.dot(da_t.astype(BF16), qdb, preferred_element_type=F32)
            dke = jnp.dot(vb, gb, preferred_element_type=F32)
            dd = jnp.sum(st * g, axis=0, keepdims=True)
            g = g * d + _dg(dob, qdb, 0, 0)
            dq_ref[rows, :] = dqd * e * (GLA_DK ** -0.5)
            dk_ref[rows, :] = dki * ei + dke * ee
            dkeke = dke * ke
            db = dqd * qd - dki * ki - dkeke
            dbend = jnp.sum(dkeke, axis=0, keepdims=True) + dd * d
            dla_ref[rows, :] = jnp.dot(cum_t.astype(F32), db, precision=HIGHEST, preferred_element_type=F32) + dbend
        gstate[...] = g

    key_out = la_spec
    return pl.pallas_call(
        body, name=name, grid=(GLA_HEADS, nblk),
        in_specs=[o_spec, q_spec, k_spec, v_spec, la_spec, st_spec],
        out_specs=[key_out, key_out, o_spec, key_out],
        out_shape=[jax.ShapeDtypeStruct((S, GLA_KEY), F32), jax.ShapeDtypeStruct((S, GLA_KEY), F32),
                   jax.ShapeDtypeStruct((S, GLA_VAL), F32), jax.ShapeDtypeStruct((S, GLA_KEY), F32)],
        scratch_shapes=[pltpu.VMEM((GLA_DV, GLA_DK), F32)],
        compiler_params=_cp(("parallel", "arbitrary")),
    )(do, proj, proj, proj, la, states)


def _gla_out_fwd(of, ob, proj, gn, name):
    S = of.shape[0]
    tm = _pick(S, (256, 128))
    gblk = (2 * GLA_KEY + GLA_VAL) // GLA_VAL

    def body(of_ref, ob_ref, g_ref, gn_ref, z_ref):
        gnv = gn_ref[...]
        for h in range(GLA_HEADS):
            cols = pl.ds(h * GLA_DV, GLA_DV)
            o = of_ref[:, cols] + ob_ref[:, cols]
            r = lax.rsqrt(jnp.mean(o * o, axis=-1, keepdims=True) + NORM_EPS)
            gv = g_ref[:, cols]
            z_ref[:, cols] = (o * r * gnv * (gv * _sigmoid(gv))).astype(BF16)

    row = pl.BlockSpec((tm, GLA_VAL), lambda i: (i, 0))
    return pl.pallas_call(
        body, name=name, grid=(S // tm,),
        in_specs=[row, row, pl.BlockSpec((tm, GLA_VAL), lambda i: (i, gblk)),
                  pl.BlockSpec((1, GLA_DV), lambda i: (0, 0))],
        out_specs=row,
        out_shape=jax.ShapeDtypeStruct((S, GLA_VAL), BF16),
        compiler_params=_cp(("parallel",)),
    )(of, ob, proj, gn)


def _gla_out_bwd(dz, of, ob, proj, gn, name):
    S = of.shape[0]
    tm = _pick(S, (256, 128))
    gblk = (2 * GLA_KEY + GLA_VAL) // GLA_VAL

    def body(dz_ref, of_ref, ob_ref, g_ref, gn_ref, do_ref, dg_ref, dgn_ref):
        i = pl.program_id(0)
        gnv = gn_ref[...]
        part = jnp.zeros((1, GLA_DV), F32)
        for h in range(GLA_HEADS):
            cols = pl.ds(h * GLA_DV, GLA_DV)
            o = of_ref[:, cols] + ob_ref[:, cols]
            r = lax.rsqrt(jnp.mean(o * o, axis=-1, keepdims=True) + NORM_EPS)
            y = o * r
            gv = g_ref[:, cols]
            sg = _sigmoid(gv)
            dzv = dz_ref[:, cols]
            dg_ref[:, cols] = dzv * (y * gnv) * (sg * (1.0 + gv * (1.0 - sg)))
            don = dzv * (gv * sg)
            part = part + jnp.sum(don * y, axis=0, keepdims=True)
            dy = don * gnv
            do_ref[:, cols] = r * (dy - y * jnp.mean(dy * y, axis=-1, keepdims=True))

        @pl.when(i == 0)
        def _():
            dgn_ref[...] = part

        @pl.when(i > 0)
        def _():
            dgn_ref[...] += part

    row = pl.BlockSpec((tm, GLA_VAL), lambda i: (i, 0))
    one = pl.BlockSpec((1, GLA_DV), lambda i: (0, 0))
    return pl.pallas_call(
        body, name=name, grid=(S // tm,),
        in_specs=[row, row, row, pl.BlockSpec((tm, GLA_VAL), lambda i: (i, gblk)), one],
        out_specs=[row, row, one],
        out_shape=[jax.ShapeDtypeStruct((S, GLA_VAL), F32), jax.ShapeDtypeStruct((S, GLA_VAL), F32),
                   jax.ShapeDtypeStruct((1, GLA_DV), F32)],
        compiler_params=_cp(("arbitrary",)),
    )(dz, of, ob, proj, gn)


N_QK_HEADS = ATT_QH + ATT_KVH


def _qk_prep_fwd(proj, qn, kn, rc, rs, name):
    S = proj.shape[0]
    tm = _pick(S, (256, 128))
    W = N_QK_HEADS * ATT_HD
    scale = ATT_HD ** -0.5

    def body(p_ref, qn_ref, kn_ref, rc_ref, rs_ref, v_in_ref, qk_ref, v_ref, kt_ref, vt_ref):
        c, s = rc_ref[...], rs_ref[...]
        for h in range(N_QK_HEADS):
            cols = pl.ds(h * ATT_HD, ATT_HD)
            w = qn_ref[...] if h < ATT_QH else kn_ref[...]
            xv = p_ref[:, cols]
            r = lax.rsqrt(jnp.mean(xv * xv, axis=-1, keepdims=True) + NORM_EPS)
            y = xv * r * w
            out = y * c + pltpu.roll(y, ATT_HD // 2, 1) * s
            if h < ATT_QH:
                qk_ref[:, cols] = (out * scale).astype(BF16)
            else:
                qk_ref[:, cols] = out.astype(BF16)
                kt_ref[pl.ds((h - ATT_QH) * ATT_HD, ATT_HD), :] = out.T.astype(BF16)
        v_ref[...] = v_in_ref[...].astype(BF16)
        for h in range(ATT_KVH):
            vt_ref[pl.ds(h * ATT_HD, ATT_HD), :] = v_in_ref[:, pl.ds(h * ATT_HD, ATT_HD)].T.astype(BF16)

    one = pl.BlockSpec((1, ATT_HD), lambda i: (0, 0))
    tab = pl.BlockSpec((tm, ATT_HD), lambda i: (i, 0))
    vw = ATT_KVH * ATT_HD
    tr = pl.BlockSpec((vw, tm), lambda i: (0, i))
    return pl.pallas_call(
        body, name=name, grid=(S // tm,),
        in_specs=[pl.BlockSpec((tm, W), lambda i: (i, 0)), one, one, tab, tab,
                  pl.BlockSpec((tm, vw), lambda i: (i, W // vw))],
        out_specs=[pl.BlockSpec((tm, W), lambda i: (i, 0)), pl.BlockSpec((tm, vw), lambda i: (i, 0)), tr, tr],
        out_shape=[jax.ShapeDtypeStruct((S, W), BF16), jax.ShapeDtypeStruct((S, vw), BF16),
                   jax.ShapeDtypeStruct((vw, S), BF16), jax.ShapeDtypeStruct((vw, S), BF16)],
        compiler_params=_cp(("parallel",)),
    )(proj, qn, kn, rc, rs, proj)


def _qk_prep_bwd(dqk, proj, qn, kn, rc, rs, name):
    S = proj.shape[0]
    tm = _pick(S, (256, 128))
    W = N_QK_HEADS * ATT_HD

    def body(d_ref, p_ref, qn_ref, kn_ref, rc_ref, rs_ref, dp_ref, dqn_ref, dkn_ref):
        i = pl.program_id(0)
        c, s = rc_ref[...], rs_ref[...]
        parts = [jnp.zeros((1, ATT_HD), F32), jnp.zeros((1, ATT_HD), F32)]
        for h in range(N_QK_HEADS):
            cols = pl.ds(h * ATT_HD, ATT_HD)
            w = qn_ref[...] if h < ATT_QH else kn_ref[...]
            dout = d_ref[:, cols]
            dy = dout * c + pltpu.roll(dout * s, ATT_HD // 2, 1)
            xv = p_ref[:, cols]
            r = lax.rsqrt(jnp.mean(xv * xv, axis=-1, keepdims=True) + NORM_EPS)
            xr = xv * r
            which = 0 if h < ATT_QH else 1
            parts[which] = parts[which] + jnp.sum(dy * xr, axis=0, keepdims=True)
            dxr = dy * w
            dp_ref[:, cols] = r * (dxr - xr * jnp.mean(dxr * xr, axis=-1, keepdims=True))

        @pl.when(i == 0)
        def _():
            dqn_ref[...] = parts[0]
            dkn_ref[...] = parts[1]

        @pl.when(i > 0)
        def _():
            dqn_ref[...] += parts[0]
            dkn_ref[...] += parts[1]

    one = pl.BlockSpec((1, ATT_HD), lambda i: (0, 0))
    tab = pl.BlockSpec((tm, ATT_HD), lambda i: (i, 0))
    row = pl.BlockSpec((tm, W), lambda i: (i, 0))
    return pl.pallas_call(
        body, name=name, grid=(S // tm,),
        in_specs=[row, row, one, one, tab, tab],
        out_specs=[row, one, one],
        out_shape=[jax.ShapeDtypeStruct((S, W), F32), jax.ShapeDtypeStruct((1, ATT_HD), F32),
                   jax.ShapeDtypeStruct((1, ATT_HD), F32)],
        compiler_params=_cp(("arbitrary",)),
    )(dqk, proj, qn, kn, rc, rs)


ATT_TQ = 256
LSE_ROWS = 8


def _attn_fwd(qk, vt, name):
    S = qk.shape[0]
    tq = min(ATT_TQ, S)

    def body(q_ref, k_ref, vt_ref, o_ref, lse_ref):
        st = _dg(k_ref[...], q_ref[...], 1, 1)
        m = jnp.max(st, axis=0, keepdims=True)
        pt = jnp.exp(st - m)
        l = jnp.sum(pt, axis=0, keepdims=True)
        ot = jnp.dot(vt_ref[...], pt.astype(BF16), preferred_element_type=F32)
        o_ref[...] = (ot * (1.0 / l)).T
        lse_ref[...] = jnp.broadcast_to(m + jnp.log(l), (LSE_ROWS, tq))

    qo = pl.BlockSpec((tq, ATT_HD), lambda h, i: (i, h))
    return pl.pallas_call(
        body, name=name, grid=(ATT_QH, S // tq),
        in_specs=[qo, pl.BlockSpec((S, ATT_HD), lambda h, i: (0, ATT_QH + h // ATT_GROUP)),
                  pl.BlockSpec((ATT_HD, S), lambda h, i: (h // ATT_GROUP, 0))],
        out_specs=[qo, pl.BlockSpec((LSE_ROWS, tq), lambda h, i: (h, i))],
        out_shape=[jax.ShapeDtypeStruct((S, ATT_QH * ATT_HD), F32),
                   jax.ShapeDtypeStruct((ATT_QH * LSE_ROWS, S), F32)],
        compiler_params=_cp(("parallel", "parallel")),
    )(qk, qk, vt)


def _attn_bwd(do, o, lse, qk, v, kt, name):
    S = qk.shape[0]
    tq = min(ATT_TQ, S)
    scale = ATT_HD ** -0.5

    def body(do_ref, o_ref, lse_ref, q_ref, k_ref, v_ref, kt_ref, dq_ref, dk_ref, dv_ref):
        g = pl.program_id(1)
        i = pl.program_id(2)

        @pl.when((g == 0) & (i == 0))
        def _():
            dk_ref[...] = jnp.zeros_like(dk_ref)
            dv_ref[...] = jnp.zeros_like(dv_ref)

        q = q_ref[...]
        dov = do_ref[...]
        dob = dov.astype(BF16)
        delta = jnp.sum((dov * o_ref[...]).T, axis=0, keepdims=True)
        st = _dg(k_ref[...], q, 1, 1)
        pt = jnp.exp(st - lse_ref[0:1, :])
        dpt = _dg(v_ref[...], dob, 1, 1)
        dst = (pt * (dpt - delta)).astype(BF16)
        dv_ref[...] += jnp.dot(pt.astype(BF16), dob, preferred_element_type=F32)
        dk_ref[...] += jnp.dot(dst, q, preferred_element_type=F32)
        dq_ref[...] = jnp.dot(kt_ref[...], dst, preferred_element_type=F32).T * scale

    qo = pl.BlockSpec((tq, ATT_HD), lambda kv, g, i: (i, kv * ATT_GROUP + g))
    kvo = pl.BlockSpec((S, ATT_HD), lambda kv, g, i: (0, kv))
    return pl.pallas_call(
        body, name=name, grid=(ATT_KVH, ATT_GROUP, S // tq),
        in_specs=[qo, qo, pl.BlockSpec((LSE_ROWS, tq), lambda kv, g, i: (kv * ATT_GROUP + g, i)), qo,
                  pl.BlockSpec((S, ATT_HD), lambda kv, g, i: (0, ATT_QH + kv)), kvo,
                  pl.BlockSpec((ATT_HD, S), lambda kv, g, i: (kv, 0))],
        out_specs=[qo, kvo, kvo],
        out_shape=[jax.ShapeDtypeStruct((S, ATT_QH * ATT_HD), F32),
                   jax.ShapeDtypeStruct((S, ATT_KVH * ATT_HD), F32),
                   jax.ShapeDtypeStruct((S, ATT_KVH * ATT_HD), F32)],
        compiler_params=_cp(("parallel", "arbitrary", "arbitrary")),
    )(do, o, lse, qk, qk, v, kt)


def _adamw(w, g, m, v, name):
    rows, cols = w.shape
    tr = rows
    for cand in (512, 256, 128, 64, 32, 16, 8):
        if rows % cand == 0 and cand * cols * 4 <= 2 * 1024 * 1024:
            tr = cand
            break

    def body(w_ref, g_ref, m_ref, v_ref, d_ref, nm_ref, nv_ref):
        gv = g_ref[...]
        nm = ADAM_B1 * m_ref[...] + (1.0 - ADAM_B1) * gv
        nv = ADAM_B2 * v_ref[...] + (1.0 - ADAM_B2) * (gv * gv)
        m_hat = nm / (1.0 - ADAM_B1 ** ADAM_STEP)
        v_hat = nv / (1.0 - ADAM_B2 ** ADAM_STEP)
        d_ref[...] = -ADAM_LR * (m_hat / (jnp.sqrt(v_hat) + ADAM_EPS) + ADAM_WD * w_ref[...])
        nm_ref[...] = nm
        nv_ref[...] = nv

    blk = pl.BlockSpec((tr, cols), lambda i: (i, 0))
    return pl.pallas_call(
        body, name=name, grid=(rows // tr,),
        in_specs=[blk] * 4, out_specs=[blk] * 3,
        out_shape=[jax.ShapeDtypeStruct((rows, cols), F32)] * 3,
        compiler_params=_cp(("parallel",)),
    )(w, g, m, v)


ANY = pl.BlockSpec(memory_space=pl.ANY)


def _place():
    return lax.axis_index("x"), lax.axis_index("y"), lax.axis_index("c")


def _other_chips(x, y):
    return [(1 - x, y), (x, 1 - y), (1 - x, 1 - y)]


def _allgather_chips(buf, name):
    R = buf.shape[1]
    H = R // 2

    def body(in_ref, out_ref, send_sems, recv_sems):
        del in_ref
        x, y, c = _place()
        me = 2 * x + y
        sibling = (x, y, 1 - c)
        chips = _other_chips(x, y)
        mine = pl.ds(pl.multiple_of(c * H, 16), H)
        theirs = pl.ds(pl.multiple_of((1 - c) * H, 16), H)

        def copy(k, p, rows, to):
            blk = out_ref.at[p, rows]
            return pltpu.make_async_remote_copy(src_ref=blk, dst_ref=blk, send_sem=send_sems.at[k],
                                                recv_sem=recv_sems.at[k], device_id=to, device_id_type=MESH)

        first = [copy(j, me, mine, (px, py, c)) for j, (px, py) in enumerate(chips)]
        for cp in first:
            cp.start()
        passed = []
        for j, (px, py) in enumerate(chips):
            p = 2 * px + py
            copy(j, p, mine, (px, py, c)).wait_recv()
            fwd = copy(3 + j, p, mine, sibling)
            fwd.start()
            passed.append(fwd)
        for j, (px, py) in enumerate(chips):
            copy(3 + j, 2 * px + py, theirs, sibling).wait_recv()
        for cp in first + passed:
            cp.wait_send()

    return pl.pallas_call(
        body, name=name,
        in_specs=[ANY], out_specs=ANY,
        out_shape=jax.ShapeDtypeStruct(buf.shape, buf.dtype),
        input_output_aliases={0: 0},
        scratch_shapes=[pltpu.SemaphoreType.DMA((6,)), pltpu.SemaphoreType.DMA((6,))],
    )(buf)


def _allreduce_small(v, name):
    R = v.shape[0]
    n_dev = 8

    def body(v_ref, sum_ref, all_ref, send_sems, recv_sems, local_sem):
        x, y, c = _place()
        me, sibling = (x, y, c), (x, y, 1 - c)
        chips = _other_chips(x, y)

        def rows(px, py, pc):
            return all_ref.at[pl.ds(pl.multiple_of((4 * px + 2 * py + pc) * R, 8), R), :]

        def copy(k, block, to, src=None):
            return pltpu.make_async_remote_copy(
                src_ref=rows(*block) if src is None else src, dst_ref=rows(*block),
                send_sem=send_sems.at[k], recv_sem=recv_sems.at[k], device_id=to, device_id_type=MESH)

        own = pltpu.make_async_copy(v_ref, rows(*me), local_sem)
        own.start()
        first = [copy(0, me, sibling, src=v_ref)]
        first += [copy(1 + j, me, (*chip, c), src=v_ref) for j, chip in enumerate(chips)]
        for cp in first:
            cp.start()
        passed = [copy(4 + j, (*chip, c), sibling) for j, chip in enumerate(chips)]
        for j, chip in enumerate(chips):
            copy(1 + j, (*chip, c), me).wait_recv()
            passed[j].start()
        copy(0, sibling, me).wait_recv()
        for j, chip in enumerate(chips):
            copy(4 + j, (*chip, 1 - c), me).wait_recv()
        for cp in first + passed:
            cp.wait_send()
        own.wait()
        acc = all_ref[pl.ds(0, R), :]
        for d in range(1, n_dev):
            acc = acc + all_ref[pl.ds(d * R, R), :]
        sum_ref[...] = acc

    vm = pl.BlockSpec(memory_space=pltpu.VMEM)
    return pl.pallas_call(
        body, name=name,
        in_specs=[vm], out_specs=[vm, vm],
        out_shape=[jax.ShapeDtypeStruct((R, LANES), F32), jax.ShapeDtypeStruct((n_dev * R, LANES), F32)],
        scratch_shapes=[pltpu.SemaphoreType.DMA((7,)), pltpu.SemaphoreType.DMA((7,)), pltpu.SemaphoreType.DMA],
    )(v)[0]


def _swap_other_half(g, name):
    R = g.shape[1]
    H = R // 2

    def body(g_ref, got_ref, send_sems, recv_sems):
        x, y, c = _place()
        theirs = pl.ds(pl.multiple_of((1 - c) * H, 16), H)
        copies = [pltpu.make_async_remote_copy(
            src_ref=g_ref.at[p, theirs], dst_ref=got_ref.at[p], send_sem=send_sems.at[p], recv_sem=recv_sems.at[p],
            device_id=(x, y, 1 - c), device_id_type=MESH) for p in range(N_CHIPS)]
        for cp in copies:
            cp.start()
        for cp in copies:
            cp.wait_recv()
        for cp in copies:
            cp.wait_send()

    return pl.pallas_call(
        body, name=name, in_specs=[ANY], out_specs=ANY,
        out_shape=jax.ShapeDtypeStruct((N_CHIPS, H, LANES), g.dtype),
        scratch_shapes=[pltpu.SemaphoreType.DMA((N_CHIPS,)), pltpu.SemaphoreType.DMA((N_CHIPS,))],
    )(g)


def _send_chip_partials(sb, name):
    H = sb.shape[1]

    def body(s_ref, got_ref, send_sems, recv_sems):
        x, y, c = _place()
        me = 2 * x + y
        chips = _other_chips(x, y)
        copies = [pltpu.make_async_remote_copy(
            src_ref=s_ref.at[2 * px + py], dst_ref=got_ref.at[me], send_sem=send_sems.at[j], recv_sem=recv_sems.at[j],
            device_id=(px, py, c), device_id_type=MESH) for j, (px, py) in enumerate(chips)]
        for cp in copies:
            cp.start()
        for j, (px, py) in enumerate(chips):
            pltpu.make_async_remote_copy(
                src_ref=s_ref.at[me], dst_ref=got_ref.at[2 * px + py], send_sem=send_sems.at[j],
                recv_sem=recv_sems.at[j], device_id=(px, py, c), device_id_type=MESH).wait_recv()
        for cp in copies:
            cp.wait_send()

    return pl.pallas_call(
        body, name=name, in_specs=[ANY], out_specs=ANY,
        out_shape=jax.ShapeDtypeStruct(sb.shape, sb.dtype),
        scratch_shapes=[pltpu.SemaphoreType.DMA((3,)), pltpu.SemaphoreType.DMA((3,))],
    )(sb)


def _join_halves(buf, name):
    H = buf.shape[0] // 2

    def body(in_ref, out_ref, send_sem, recv_sem):
        del in_ref
        x, y, c = _place()
        mine = out_ref.at[pl.ds(pl.multiple_of(c * H, 16), H)]
        theirs = out_ref.at[pl.ds(pl.multiple_of((1 - c) * H, 16), H)]
        cp = pltpu.make_async_remote_copy(src_ref=mine, dst_ref=mine, send_sem=send_sem,
                                          recv_sem=recv_sem, device_id=(x, y, 1 - c), device_id_type=MESH)
        cp.start()
        pltpu.make_async_remote_copy(src_ref=theirs, dst_ref=theirs, send_sem=send_sem,
                                     recv_sem=recv_sem, device_id=(x, y, 1 - c), device_id_type=MESH).wait_recv()
        cp.wait_send()

    return pl.pallas_call(
        body, name=name, in_specs=[ANY], out_specs=ANY,
        out_shape=jax.ShapeDtypeStruct(buf.shape, buf.dtype),
        input_output_aliases={0: 0},
        scratch_shapes=[pltpu.SemaphoreType.DMA, pltpu.SemaphoreType.DMA],
    )(buf)


RS_ROWS = 1024


def _add_sibling(g, got, c, me, name):
    H = got.shape[1]
    nb = H // RS_ROWS

    def body(sp_ref, g_ref, got_ref, sb_ref, sf_ref):
        p = pl.program_id(1)
        s = g_ref[0] + got_ref[0]
        sb_ref[0] = s.astype(BF16)

        @pl.when(p == sp_ref[1])
        def _():
            sf_ref[...] = s

    grid_spec = pltpu.PrefetchScalarGridSpec(
        num_scalar_prefetch=1, grid=(nb, N_CHIPS),
        in_specs=[pl.BlockSpec((1, RS_ROWS, LANES), lambda i, p, sp: (p, sp[0] * nb + i, 0)),
                  pl.BlockSpec((1, RS_ROWS, LANES), lambda i, p, sp: (p, i, 0))],
        out_specs=[pl.BlockSpec((1, RS_ROWS, LANES), lambda i, p, sp: (p, i, 0)),
                   pl.BlockSpec((RS_ROWS, LANES), lambda i, p, sp: (i, 0))])
    return pl.pallas_call(
        body, name=name, grid_spec=grid_spec,
        out_shape=[jax.ShapeDtypeStruct((N_CHIPS, H, LANES), BF16), jax.ShapeDtypeStruct((H, LANES), F32)],
        compiler_params=_cp(("arbitrary", "arbitrary")),
    )(jnp.stack([c, me]).astype(jnp.int32), g, got)


def _add_chips(sf, got, others_and_c, name):
    H = sf.shape[0]
    nb = H // RS_ROWS

    def body(sp_ref, sf_ref, r1_ref, r2_ref, r3_ref, out_ref):
        out_ref[...] = ((sf_ref[...] + r1_ref[0].astype(F32)) + r2_ref[0].astype(F32)) + r3_ref[0].astype(F32)

    def slot(k):
        return pl.BlockSpec((1, RS_ROWS, LANES), lambda i, sp: (sp[k], i, 0))

    blk = pl.BlockSpec((RS_ROWS, LANES), lambda i, sp: (i, 0))
    grid_spec = pltpu.PrefetchScalarGridSpec(
        num_scalar_prefetch=1, grid=(nb,), in_specs=[blk, slot(0), slot(1), slot(2)],
        out_specs=pl.BlockSpec((RS_ROWS, LANES), lambda i, sp: (sp[3] * nb + i, 0)))
    return pl.pallas_call(
        body, name=name, grid_spec=grid_spec,
        out_shape=jax.ShapeDtypeStruct((2 * H, LANES), F32),
        compiler_params=_cp(("arbitrary",)),
    )(others_and_c.astype(jnp.int32), sf, got, got, got)


SHARDED = (("gla_w_in", 2), ("gla_w_out", 1), ("attn_w_qkv", 2), ("attn_w_out", 1), ("ffn_w_up", 2),
           ("ffn_w_down", 1), ("gla_w_gate_up_f", 2), ("gla_w_gate_up_b", 2), ("ffn_w_conv", 2))
SHARDED_BF16 = SHARDED[:6]
SHARDED_F32 = SHARDED[6:]
REPLICATED = ("norm_mix", "norm_ffn", "gla_b_gate_f", "gla_b_gate_b", "gla_norm", "attn_q_norm", "attn_k_norm",
              "ffn_b_conv")


PIECE_ROWS = 16


def _piece_rows(shape):
    n = 1
    for s in shape:
        n *= s
    rows = n // LANES
    return rows, -(-rows // PIECE_ROWS) * PIECE_ROWS


def _pack(pieces, dtype, row_multiple):
    flat = []
    for p in pieces:
        rows, padded = _piece_rows(p.shape)
        flat.append(jnp.pad(p.astype(dtype).reshape(rows, LANES), ((0, padded - rows), (0, 0))))
    rows = sum(f.shape[0] for f in flat)
    padded = -(-rows // row_multiple) * row_multiple
    if padded > rows:
        flat.append(jnp.zeros((padded - rows, LANES), dtype))
    return jnp.concatenate(flat, axis=0)


def _unpack(buf, shapes):
    out, r = [], 0
    for shp in shapes:
        rows, padded = _piece_rows(shp)
        out.append(buf[r:r + rows].reshape(shp))
        r += padded
    return out


def _gather_weights(shards, layout, dtype, row_multiple, me, name):
    packed = _pack([shards[n] for n, _ in layout], dtype, row_multiple)
    buf = lax.dynamic_update_index_in_dim(jnp.zeros((N_CHIPS,) + packed.shape, dtype), packed, me, 0)
    allp = _allgather_chips(buf, name)
    shapes = [shards[n].shape for n, _ in layout]
    per_chip = [_unpack(allp[p], shapes) for p in range(N_CHIPS)]
    return {n: jnp.concatenate([per_chip[p][k] for p in range(N_CHIPS)], axis=ax)
            for k, (n, ax) in enumerate(layout)}


def _rope_tables(S):
    rows = S // GRID_W
    row_idx = jnp.repeat(jnp.arange(rows, dtype=F32), GRID_W)
    col_idx = jnp.tile(jnp.arange(GRID_W, dtype=F32), rows)
    pairs = ATT_HD // 4
    inv_freq = ROPE_THETA ** (-jnp.arange(pairs, dtype=F32) / pairs)
    ang = jnp.concatenate([row_idx[:, None] * inv_freq, col_idx[:, None] * inv_freq], axis=-1)
    cos, sin = jnp.cos(ang), jnp.sin(ang)
    return jnp.concatenate([cos, cos], axis=-1), jnp.concatenate([-sin, sin], axis=-1)


def _gate_rows(w, first_row):
    return jnp.zeros((LANES, GLA_KEY), F32).at[first_row:first_row + GLA_RANK].set(w.astype(F32))


def _local_step(x, target, W, P):
    S = x.shape[0]
    rc, rs = _rope_tables(S)
    row = lambda a: a.reshape(1, -1)
    saved = []
    for i in range(DEPTH):
        j = i // 2
        nm = row(P["norm_mix"][i])
        h1 = _rmsnorm_fwd(x, nm, f"norm_mix_fwd{i}")
        if i % 2 == 0:
            win = W["gla_w_in"][j]
            wgf = _gate_rows(W["gla_w_gate_up_f"][j], 0)
            wgb = _gate_rows(W["gla_w_gate_up_b"][j], GLA_RANK)
            bgf, bgb = row(P["gla_b_gate_f"][j]), row(P["gla_b_gate_b"][j])
            gn = row(P["gla_norm"][j])
            proj = _matmul(h1, win, 1, 0, f"gla_in{i}")
            laf, lab = _gla_gate_fwd(proj, wgf, bgf, wgb, bgb, f"gla_gate_fwd{i}")
            of, stf = _gla_scan_fwd(proj, laf, False, f"gla_scan_f_fwd{i}")
            ob, stb = _gla_scan_fwd(proj, lab, True, f"gla_scan_b_fwd{i}")
            z = _gla_out_fwd(of, ob, proj, gn, f"gla_out_fwd{i}")
            xm = _matmul(z, W["gla_w_out"][j], 1, 0, f"gla_outproj{i}", res=x)
            mix = dict(proj=proj, laf=laf, lab=lab, of=of, ob=ob, stf=stf, stb=stb, z=z, wgf=wgf, wgb=wgb)
        else:
            proj = _matmul(h1, W["attn_w_qkv"][j], 1, 0, f"attn_qkv{i}")
            qn, kn = row(P["attn_q_norm"][j]), row(P["attn_k_norm"][j])
            qk, vb, kt, vt = _qk_prep_fwd(proj, qn, kn, rc, rs, f"qk_prep_fwd{i}")
            o, lse = _attn_fwd(qk, vt, f"attn_fwd{i}")
            xm = _matmul(o, W["attn_w_out"][j], 1, 0, f"attn_outproj{i}", res=x)
            mix = dict(proj=proj, qk=qk, vb=vb, kt=kt, o=o, lse=lse)
        h2 = _rmsnorm_fwd(xm, row(P["norm_ffn"][i]), f"norm_ffn_fwd{i}")
        a, uv, ug = _ffn_mid_fwd(h2, W["ffn_w_up"][i], W["ffn_w_conv"][i], row(P["ffn_b_conv"][i]), f"ffn_mid_fwd{i}")
        xo = _matmul(a, W["ffn_w_down"][i], 1, 0, f"ffn_down{i}", res=xm)
        saved.append(dict(x=x, h1=h1, xm=xm, h2=h2, uv=uv, ug=ug, mix=mix))
        x = xo

    dx, dxb, loss = _loss_grad(x, target, "loss")

    G = {n: [None] * (DEPTH if n.startswith(("norm", "ffn")) else DEPTH // 2)
         for n in [n for n, _ in SHARDED] + list(REPLICATED)}
    for i in reversed(range(DEPTH)):
        j = i // 2
        sv = saved[i]
        mix = sv["mix"]
        duv, dug, a, gwv, gwg = _ffn_mid_bwd(dxb, W["ffn_w_down"][i], sv["uv"], sv["ug"], W["ffn_w_conv"][i],
                                             row(P["ffn_b_conv"][i]), f"ffn_mid_bwd{i}")
        G["ffn_w_down"][i] = _matmul(a, dxb, 0, 0, f"ffn_down_wgrad{i}")
        du = jnp.concatenate([duv, dug], axis=1)
        G["ffn_w_up"][i] = _matmul(sv["h2"], du, 0, 0, f"ffn_up_wgrad{i}")
        G["ffn_w_conv"][i] = jnp.concatenate([gwv[:3], gwg[:3]], axis=1)
        G["ffn_b_conv"][i] = jnp.concatenate([gwv[3], gwg[3]], axis=0)
        dxm, dxmb, dn = _dgrad_norm(du, W["ffn_w_up"][i], sv["xm"], row(P["norm_ffn"][i]), dx, f"ffn_up_dgrad{i}")
        G["norm_ffn"][i] = dn[0]
        if i % 2 == 0:
            proj = mix["proj"]
            bgf, bgb = row(P["gla_b_gate_f"][j]), row(P["gla_b_gate_b"][j])
            gn = row(P["gla_norm"][j])
            dz = _matmul(dxmb, W["gla_w_out"][j], 1, 1, f"gla_outproj_dgrad{i}")
            G["gla_w_out"][j] = _matmul(mix["z"], dxmb, 0, 0, f"gla_outproj_wgrad{i}")
            do, dg, dgn = _gla_out_bwd(dz, mix["of"], mix["ob"], proj, gn, f"gla_out_bwd{i}")
            G["gla_norm"][j] = dgn[0]
            dqf, dkf, dvf, dlaf = _gla_scan_bwd(do, proj, mix["laf"], mix["stf"], False, f"gla_scan_f_bwd{i}")
            dqb, dkb, dvb, dlab = _gla_scan_bwd(do, proj, mix["lab"], mix["stb"], True, f"gla_scan_b_bwd{i}")
            dr, dwf, dbf, dwb, dbb = _gla_gate_bwd(dlaf, dlab, proj, mix["wgf"], bgf, mix["wgb"], bgb,
                                                   f"gla_gate_bwd{i}")
            G["gla_w_gate_up_f"][j] = dwf[:GLA_RANK]
            G["gla_w_gate_up_b"][j] = dwb[GLA_RANK:2 * GLA_RANK]
            G["gla_b_gate_f"][j] = dbf[0]
            G["gla_b_gate_b"][j] = dbb[0]
            dproj = jnp.concatenate([dqf + dqb, dkf + dkb, dvf + dvb, dg, dr], axis=1).astype(BF16)
            G["gla_w_in"][j] = _matmul(sv["h1"], dproj, 0, 0, f"gla_in_wgrad{i}")[:, :GLA_IN]
            wmix = W["gla_w_in"][j]
        else:
            proj = mix["proj"]
            qn, kn = row(P["attn_q_norm"][j]), row(P["attn_k_norm"][j])
            do = _matmul(dxmb, W["attn_w_out"][j], 1, 1, f"attn_outproj_dgrad{i}")
            G["attn_w_out"][j] = _matmul(mix["o"], dxmb, 0, 0, f"attn_outproj_wgrad{i}")
            dq, dk, dv = _attn_bwd(do, mix["o"], mix["lse"], mix["qk"], mix["vb"], mix["kt"], f"attn_bwd{i}")
            dqk = jnp.concatenate([dq, dk], axis=1)
            dpqk, dqn, dkn = _qk_prep_bwd(dqk, proj, qn, kn, rc, rs, f"qk_prep_bwd{i}")
            G["attn_q_norm"][j] = dqn[0]
            G["attn_k_norm"][j] = dkn[0]
            dproj = jnp.concatenate([dpqk, dv], axis=1).astype(BF16)
            G["attn_w_qkv"][j] = _matmul(sv["h1"], dproj, 0, 0, f"attn_qkv_wgrad{i}")
            wmix = W["attn_w_qkv"][j]
        dx, dxb, dn = _dgrad_norm(dproj, wmix, sv["x"], row(P["norm_mix"][i]), dxm, f"mix_in_dgrad{i}")
        G["norm_mix"][i] = dn[0]
    return loss, dx, G


def _pad_gla_in(w):
    return jnp.pad(w, ((0, 0), (0, 0), (0, GLA_IN_PAD - GLA_IN)))


def kernel(x, norm_mix, norm_ffn, gla_w_in, gla_w_gate_up_f, gla_b_gate_f, gla_w_gate_up_b, gla_b_gate_b, gla_norm, gla_w_out, attn_w_qkv, attn_q_norm, attn_k_norm, attn_w_out, ffn_w_up, ffn_w_conv, ffn_b_conv, ffn_w_down, loss_target, m_norm_mix, m_norm_ffn, m_gla_w_in, m_gla_w_gate_up_f, m_gla_b_gate_f, m_gla_w_gate_up_b, m_gla_b_gate_b, m_gla_norm, m_gla_w_out, m_attn_w_qkv, m_attn_q_norm, m_attn_k_norm, m_attn_w_out, m_ffn_w_up, m_ffn_w_conv, m_ffn_b_conv, m_ffn_w_down, v_norm_mix, v_norm_ffn, v_gla_w_in, v_gla_w_gate_up_f, v_gla_b_gate_f, v_gla_w_gate_up_b, v_gla_b_gate_b, v_gla_norm, v_gla_w_out, v_attn_w_qkv, v_attn_q_norm, v_attn_k_norm, v_attn_w_out, v_ffn_w_up, v_ffn_w_conv, v_ffn_b_conv, v_ffn_w_down):
    names = ("norm_mix", "norm_ffn", "gla_w_in", "gla_w_gate_up_f", "gla_b_gate_f", "gla_w_gate_up_b",
             "gla_b_gate_b", "gla_norm", "gla_w_out", "attn_w_qkv", "attn_q_norm", "attn_k_norm", "attn_w_out",
             "ffn_w_up", "ffn_w_conv", "ffn_b_conv", "ffn_w_down")
    w = dict(zip(names, (norm_mix, norm_ffn, gla_w_in, gla_w_gate_up_f, gla_b_gate_f, gla_w_gate_up_b,
                         gla_b_gate_b, gla_norm, gla_w_out, attn_w_qkv, attn_q_norm, attn_k_norm, attn_w_out,
                         ffn_w_up, ffn_w_conv, ffn_b_conv, ffn_w_down)))
    m = dict(zip(names, (m_norm_mix, m_norm_ffn, m_gla_w_in, m_gla_w_gate_up_f, m_gla_b_gate_f,
                         m_gla_w_gate_up_b, m_gla_b_gate_b, m_gla_norm, m_gla_w_out, m_attn_w_qkv, m_attn_q_norm,
                         m_attn_k_norm, m_attn_w_out, m_ffn_w_up, m_ffn_w_conv, m_ffn_b_conv, m_ffn_w_down)))
    v = dict(zip(names, (v_norm_mix, v_norm_ffn, v_gla_w_in, v_gla_w_gate_up_f, v_gla_b_gate_f,
                         v_gla_w_gate_up_b, v_gla_b_gate_b, v_gla_norm, v_gla_w_out, v_attn_w_qkv, v_attn_q_norm,
                         v_attn_k_norm, v_attn_w_out, v_ffn_w_up, v_ffn_w_conv, v_ffn_b_conv, v_ffn_w_down)))
    px, py, pc = _place()
    me = 2 * px + py

    W = _gather_weights(w, SHARDED_BF16, BF16, 32, me, "gather_weights_bf16")
    W.update(_gather_weights(w, SHARDED_F32, F32, 32, me, "gather_weights_f32"))
    W["gla_w_in"] = _pad_gla_in(W["gla_w_in"])
    P = {n: w[n] for n in REPLICATED}

    loss_part, dx, grads = _local_step(x[0], loss_target[0], W, P)

    shard_rows = sum(_piece_rows(w[n].shape)[1] for n, _ in SHARDED)
    R = -(-shard_rows // (2 * RS_ROWS)) * (2 * RS_ROWS)
    flat = []
    for p in range(N_CHIPS):
        for n, ax in SHARDED:
            width = w[n].shape[ax]
            whole = _piece_rows(w[n].shape[1:])[0] % PIECE_ROWS == 0
            layers = [(g, ax - 1) for g in grads[n]] if whole else [(jnp.stack(grads[n]), ax)]
            for g, axis in layers:
                piece = lax.slice_in_dim(g, p * width, (p + 1) * width, axis=axis)
                rows, padded = _piece_rows(piece.shape)
                flat.append(jnp.pad(piece.reshape(rows, LANES), ((0, padded - rows), (0, 0))))
        flat.append(jnp.zeros((R - shard_rows, LANES), F32))
    g_all = jnp.concatenate(flat, axis=0).reshape(N_CHIPS, R, LANES)
    got = _swap_other_half(g_all, "grads_to_sibling")
    sb, sf = _add_sibling(g_all, got, pc, me, "grads_add_sibling")
    got_b = _send_chip_partials(sb, "grads_to_chips")
    others_and_c = jnp.stack([jnp.where(me <= k, k + 1, k) for k in range(N_CHIPS - 1)] + [pc])
    g_half = _add_chips(sf, got_b, others_and_c, "grads_add_chips")
    g_mine = _join_halves(g_half, "grads_join_halves")
    gsh = dict(zip([n for n, _ in SHARDED], _unpack(g_mine, [w[n].shape for n, _ in SHARDED])))

    small = _pack([jnp.stack(grads[n]) for n in REPLICATED] + [loss_part], F32, 16)
    small_sum = _allreduce_small(small, "small_allreduce")
    parts = _unpack(small_sum, [w[n].shape for n in REPLICATED] + [(1, LANES)])
    gsh.update(dict(zip(REPLICATED, parts[:-1])))
    loss = parts[-1][0, 0]

    delta, new_m, new_v = {}, {}, {}
    for n in names:
        shp = w[n].shape
        two_d = (-1, shp[-1])
        d, nm, nv = _adamw(w[n].reshape(two_d), gsh[n].reshape(two_d), m[n].reshape(two_d), v[n].reshape(two_d),
                           f"adamw_{n}")
        delta[n], new_m[n], new_v[n] = d.reshape(shp), nm.reshape(shp), nv.reshape(shp)

    return (loss, dx[None], *[gsh[n] for n in names], *[delta[n] for n in names],
            *[new_m[n] for n in names], *[new_v[n] for n in names])
```

```python
import jax
import jax.numpy as jnp
from jax import lax
from jax.experimental import pallas as pl
from jax.experimental.pallas import tpu as pltpu

F32 = jnp.float32
BF16 = jnp.bfloat16
MESH = pl.DeviceIdType.MESH
HIGHEST = lax.Precision.HIGHEST

D_MODEL = 1024
DEPTH = 4
GRID_W = 64
NORM_EPS = 1e-6
GLA_HEADS = 4
GLA_DK = 128
GLA_DV = 256
GLA_KEY = GLA_HEADS * GLA_DK
GLA_VAL = GLA_HEADS * GLA_DV
GLA_RANK = 16
GLA_CHUNK = 64
GLA_GATE_NORMALIZER = 16.0
GLA_IN = 2 * GLA_KEY + 2 * GLA_VAL + 2 * GLA_RANK
GLA_IN_PAD = 3200
GLA_R_BLOCK = (2 * GLA_KEY + 2 * GLA_VAL) // 128
ATT_HD = 128
ATT_QH = 8
ATT_KVH = 2
ATT_GROUP = ATT_QH // ATT_KVH
ATT_QKV = (ATT_QH + 2 * ATT_KVH) * ATT_HD
ROPE_THETA = 10000.0
D_FF = 2816
ADAM_LR = 0.001
ADAM_B1 = 0.9
ADAM_B2 = 0.999
ADAM_EPS = 1e-08
ADAM_WD = 0.01
ADAM_STEP = 10

N_CHIPS = 4
LANES = 128
VMEM_LIMIT = 48 * 1024 * 1024


def _cp(sem):
    return pltpu.CompilerParams(dimension_semantics=sem, vmem_limit_bytes=VMEM_LIMIT)


def _pick(n, cands):
    for c in cands:
        if n % c == 0:
            return c
    return n


def _dg(a, b, ca, cb):
    return lax.dot_general(a, b, (((ca,), (cb,)), ((), ())), preferred_element_type=F32)


def _sigmoid(x):
    return 1.0 / (1.0 + jnp.exp(-x))


def _rmsnorm_fwd(x, w, name):
    S, D = x.shape
    tm = _pick(S, (512, 256))

    def body(x_ref, w_ref, h_ref):
        xv = x_ref[...]
        r = lax.rsqrt(jnp.mean(xv * xv, axis=-1, keepdims=True) + NORM_EPS)
        h_ref[...] = (xv * r * w_ref[...]).astype(BF16)

    return pl.pallas_call(
        body, name=name, grid=(S // tm,),
        in_specs=[pl.BlockSpec((tm, D), lambda i: (i, 0)), pl.BlockSpec((1, D), lambda i: (0, 0))],
        out_specs=pl.BlockSpec((tm, D), lambda i: (i, 0)),
        out_shape=jax.ShapeDtypeStruct((S, D), BF16),
        compiler_params=_cp(("parallel",)),
    )(x, w)


def _loss_grad(y, t, name):
    S, D = y.shape
    tm = _pick(S, (512, 256))

    def body(y_ref, t_ref, dy_ref, dyb_ref, loss_ref):
        i = pl.program_id(0)
        d = y_ref[...] - t_ref[...]
        dy = d * (1.0 / D)
        dy_ref[...] = dy
        dyb_ref[...] = dy.astype(BF16)
        sq = jnp.sum(jnp.sum(d * d, axis=1, keepdims=True), axis=0, keepdims=True)
        part = jnp.broadcast_to(sq * (0.5 / D), (1, LANES))

        @pl.when(i == 0)
        def _():
            loss_ref[...] = part

        @pl.when(i > 0)
        def _():
            loss_ref[...] += part

    return pl.pallas_call(
        body, name=name, grid=(S // tm,),
        in_specs=[pl.BlockSpec((tm, D), lambda i: (i, 0)), pl.BlockSpec((tm, D), lambda i: (i, 0))],
        out_specs=[pl.BlockSpec((tm, D), lambda i: (i, 0)), pl.BlockSpec((tm, D), lambda i: (i, 0)),
                   pl.BlockSpec((1, LANES), lambda i: (0, 0))],
        out_shape=[jax.ShapeDtypeStruct((S, D), F32), jax.ShapeDtypeStruct((S, D), BF16),
                   jax.ShapeDtypeStruct((1, LANES), F32)],
        compiler_params=_cp(("arbitrary",)),
    )(y, t)


def _matmul(a, b, ca, cb, name, res=None, out_dtype=F32, b_layer=None, out_chips=None, into=None):
    M, K = a.shape[1 - ca], a.shape[ca]
    pair = isinstance(b, (tuple, list))
    if b_layer is not None:
        assert cb == 0 and b.shape[1] % K == 0
        N = N_CHIPS * b.shape[2]
    elif pair:
        assert cb == 0 and b[0].shape == b[1].shape and b[0].shape[0] == K
        N = 2 * b[0].shape[1]
    else:
        assert b.shape[cb] == K
        N = b.shape[1 - cb]
    how = out_chips[0] if out_chips else None
    tm = M if how == "rows" else _pick(M, (1024, 1408, 512, 256, 128))
    if b_layer is not None or how == "cols":
        tn = N // N_CHIPS
    else:
        tn = _pick(N // 2 if pair else N, (1024, 1408, 768, 640, 512, 256, 128))
    tk = _pick(K, (512, 1408, 256, 128))
    nk = K // tk
    n0 = (N // 2) // tn
    if ca == 1:
        a_spec = pl.BlockSpec((tm, tk), lambda i, j, k: (i, k))
    else:
        a_spec = pl.BlockSpec((tk, tm), lambda i, j, k: (k, i))
    if b_layer is not None:
        b_specs = [pl.BlockSpec((None, tk, tn), lambda i, j, k: (j, b_layer * nk + k, 0))]
    elif pair:
        b_specs = [pl.BlockSpec((tk, tn), lambda i, j, k: (k, jnp.minimum(j, n0 - 1))),
                   pl.BlockSpec((tk, tn), lambda i, j, k: (k, jnp.maximum(j - n0, 0)))]
    elif cb == 0:
        b_specs = [pl.BlockSpec((tk, tn), lambda i, j, k: (k, j))]
    else:
        b_specs = [pl.BlockSpec((tn, tk), lambda i, j, k: (j, k))]
    if how == "cols":
        _, layer, layers = out_chips
        o_spec = pl.BlockSpec((None, tm, tn), lambda i, j, k: (j, layer * (M // tm) + i, 0))
        out_shape = jax.ShapeDtypeStruct((N_CHIPS, layers * M, tn), out_dtype)
    elif how == "rows":
        _, layer, layers = out_chips
        o_spec = pl.BlockSpec((N_CHIPS, M // N_CHIPS, tn), lambda i, j, k: (0, layer, j))
        out_shape = jax.ShapeDtypeStruct((N_CHIPS, layers * M // N_CHIPS, N), out_dtype)
    else:
        o_spec = pl.BlockSpec((tm, tn), lambda i, j, k: (i, j))
        out_shape = jax.ShapeDtypeStruct((M, N), out_dtype)
    has_res = res is not None
    nb = len(b_specs)

    def body(*refs):
        a_ref, b_refs = refs[0], refs[1:1 + nb]
        r_ref = refs[1 + nb] if has_res else None
        o_ref, acc = refs[-2], refs[-1]
        j = pl.program_id(1)
        k = pl.program_id(2)

        @pl.when(k == 0)
        def _():
            acc[...] = jnp.zeros_like(acc)

        av = a_ref[...].astype(BF16)
        if pair:
            @pl.when(j < n0)
            def _():
                acc[...] += _dg(av, b_refs[0][...].astype(BF16), ca, cb)

            @pl.when(j >= n0)
            def _():
                acc[...] += _dg(av, b_refs[1][...].astype(BF16), ca, cb)
        else:
            acc[...] += _dg(av, b_refs[0][...].astype(BF16), ca, cb)

        @pl.when(k == nk - 1)
        def _():
            v = acc[...]
            if has_res:
                v = v + r_ref[...]
            if how == "rows":
                rows = M // N_CHIPS
                for p in range(N_CHIPS):
                    o_ref[p] = v[p * rows:(p + 1) * rows, :].astype(out_dtype)
            else:
                o_ref[...] = v.astype(out_dtype)

    in_specs = [a_spec] + b_specs + ([o_spec] if has_res else [])
    args = (a,) + (tuple(b) if pair else (b,)) + ((res,) if has_res else ())
    aliases = {}
    if into is not None:
        assert into.shape == out_shape.shape
        in_specs.append(ANY)
        aliases = {len(args): 0}
        args = args + (into,)
        inner = body

        def body(*refs):
            inner(*refs[:len(args) - 1], *refs[len(args):])

    return pl.pallas_call(
        body, name=name, grid=(M // tm, N // tn, nk),
        in_specs=in_specs, out_specs=o_spec, out_shape=out_shape,
        input_output_aliases=aliases,
        scratch_shapes=[pltpu.VMEM((tm, tn), F32)],
        compiler_params=_cp(("parallel", "parallel", "arbitrary")),
    )(*args)


def _dgrad_norm(dy, w, x, wn, dres, name, w_layer=None):
    pair = isinstance(dy, (tuple, list))
    M = dy[0].shape[0] if pair else dy.shape[0]
    Kp = 2 * dy[0].shape[1] if pair else dy.shape[1]
    if w_layer is not None:
        D = x.shape[1]
        tk = w.shape[2]
        assert N_CHIPS * tk == Kp and w.shape[1] % D == 0
        w_spec = pl.BlockSpec((None, D, tk), lambda i, k: (k, w_layer, 0))
    else:
        D = w.shape[0]
        tk = _pick(Kp // 2 if pair else Kp, (1408, 768, 640, 512, 256, 128))
        w_spec = pl.BlockSpec((D, tk), lambda i, k: (0, k))
    tm = _pick(M, (512, 256, 128))
    nk = Kp // tk
    n0 = (Kp // 2) // tk
    if pair:
        dy_specs = [pl.BlockSpec((tm, tk), lambda i, k: (i, jnp.minimum(k, n0 - 1))),
                    pl.BlockSpec((tm, tk), lambda i, k: (i, jnp.maximum(k - n0, 0)))]
    else:
        dy_specs = [pl.BlockSpec((tm, tk), lambda i, k: (i, k))]
    nd = len(dy_specs)

    def body(*refs):
        dy_refs = refs[:nd]
        w_ref, x_ref, wn_ref, dres_ref, dx_ref, dxb_ref, dwn_ref, acc = refs[nd:]
        i = pl.program_id(0)
        k = pl.program_id(1)

        @pl.when(k == 0)
        def _():
            acc[...] = jnp.zeros_like(acc)

        if pair:
            @pl.when(k < n0)
            def _():
                acc[...] += _dg(dy_refs[0][...], w_ref[...], 1, 1)

            @pl.when(k >= n0)
            def _():
                acc[...] += _dg(dy_refs[1][...], w_ref[...], 1, 1)
        else:
            acc[...] += _dg(dy_refs[0][...], w_ref[...], 1, 1)

        @pl.when(k == nk - 1)
        def _():
            dh = acc[...]
            xv = x_ref[...]
            r = lax.rsqrt(jnp.mean(xv * xv, axis=-1, keepdims=True) + NORM_EPS)
            yv = xv * r
            dyv = dh * wn_ref[...]
            dxv = r * (dyv - yv * jnp.mean(dyv * yv, axis=-1, keepdims=True)) + dres_ref[...]
            dx_ref[...] = dxv
            dxb_ref[...] = dxv.astype(BF16)
            part = jnp.sum(dh * yv, axis=0, keepdims=True)

            @pl.when(i == 0)
            def _():
                dwn_ref[...] = part

            @pl.when(i > 0)
            def _():
                dwn_ref[...] += part

    row = pl.BlockSpec((tm, D), lambda i, k: (i, 0))
    return pl.pallas_call(
        body, name=name, grid=(M // tm, nk),
        in_specs=dy_specs + [w_spec, row, pl.BlockSpec((1, D), lambda i, k: (0, 0)), row],
        out_specs=[row, row, pl.BlockSpec((1, D), lambda i, k: (0, 0))],
        out_shape=[jax.ShapeDtypeStruct((M, D), F32), jax.ShapeDtypeStruct((M, D), BF16),
                   jax.ShapeDtypeStruct((1, D), F32)],
        scratch_shapes=[pltpu.VMEM((tm, D), F32)],
        compiler_params=_cp(("arbitrary", "arbitrary")),
    )(*(tuple(dy) if pair else (dy,)), w, x, wn, dres)


FFN_TN = 128
FFN_ROWS = 256
PAD = 8


def _conv3(pad_ref, w, r0, tr):
    um = pad_ref[pl.ds(PAD - 1 + r0, tr), :]
    uc = pad_ref[pl.ds(PAD + r0, tr), :]
    up = pad_ref[pl.ds(PAD + 1 + r0, tr), :]
    return w[0:1, :] * um + w[1:2, :] * uc + w[2:3, :] * up, (um, uc, up)


def _zero_pads(pad_ref, S, tn):
    pad_ref[pl.ds(0, PAD), :] = jnp.zeros((PAD, tn), F32)
    pad_ref[pl.ds(PAD + S, PAD), :] = jnp.zeros((PAD, tn), F32)


def _ffn_mid_fwd(h, wup, layer, wconv, bconv, name):
    S, D = h.shape
    F = N_CHIPS * wup.shape[2] // 2
    tn = FFN_TN
    nb = F // tn
    per_chip = wup.shape[2] // tn
    tr = min(FFN_ROWS, S)

    def body(h_ref, wv_ref, wg_ref, cv_ref, cg_ref, bv_ref, bg_ref, a_ref, uv_ref, ug_ref, padv, padg):
        _zero_pads(padv, S, tn)
        _zero_pads(padg, S, tn)
        hv = h_ref[...]
        padv[pl.ds(PAD, S), :] = jnp.dot(hv, wv_ref[...], preferred_element_type=F32)
        padg[pl.ds(PAD, S), :] = jnp.dot(hv, wg_ref[...], preferred_element_type=F32)
        uv_ref[...] = padv[pl.ds(PAD, S), :]
        ug_ref[...] = padg[pl.ds(PAD, S), :]
        cwv, cwg, bv, bg = cv_ref[...], cg_ref[...], bv_ref[...], bg_ref[...]
        for r0 in range(0, S, tr):
            cv = _conv3(padv, cwv, r0, tr)[0] + bv
            cg = _conv3(padg, cwg, r0, tr)[0] + bg
            a_ref[pl.ds(r0, tr), :] = (cg * _sigmoid(cg) * cv).astype(BF16)

    col = lambda off: (lambda j: (0, j + off))
    wcol = lambda off: (lambda j: ((j + off) // per_chip, layer, (j + off) % per_chip))
    return pl.pallas_call(
        body, name=name, grid=(nb,),
        in_specs=[pl.BlockSpec((S, D), lambda j: (0, 0)),
                  pl.BlockSpec((None, D, tn), wcol(0)), pl.BlockSpec((None, D, tn), wcol(nb)),
                  pl.BlockSpec((3, tn), col(0)), pl.BlockSpec((3, tn), col(nb)),
                  pl.BlockSpec((1, tn), col(0)), pl.BlockSpec((1, tn), col(nb))],
        out_specs=[pl.BlockSpec((S, tn), col(0))] * 3,
        out_shape=[jax.ShapeDtypeStruct((S, F), BF16), jax.ShapeDtypeStruct((S, F), F32),
                   jax.ShapeDtypeStruct((S, F), F32)],
        scratch_shapes=[pltpu.VMEM((S + 2 * PAD, tn), F32)] * 2,
        compiler_params=_cp(("parallel",)),
    )(h, wup, wup, wconv, wconv, bconv, bconv)


def _rows8(rows):
    n = rows[0].shape[1]
    idx = lax.broadcasted_iota(jnp.int32, (8, n), 0)
    out = jnp.zeros((8, n), F32)
    for k, r in enumerate(rows):
        out = jnp.where(idx == k, r, out)
    return out


def _ffn_mid_bwd(dyb, wdown, uv, ug, wconv, bconv, name):
    S, D = dyb.shape
    F = wdown.shape[0]
    tn = FFN_TN
    nb = F // tn
    tr = min(FFN_ROWS, S)

    def body(dy_ref, wd_ref, uv_ref, ug_ref, cv_ref, cg_ref, bv_ref, bg_ref,
             duv_ref, dug_ref, a_ref, gwv_ref, gwg_ref, da_s, padv, padg, pdv, pdg):
        for p in (padv, padg, pdv, pdg):
            _zero_pads(p, S, tn)
        da_s[...] = _dg(dy_ref[...], wd_ref[...], 1, 1)
        padv[pl.ds(PAD, S), :] = uv_ref[...]
        padg[pl.ds(PAD, S), :] = ug_ref[...]
        cwv, cwg, bv, bg = cv_ref[...], cg_ref[...], bv_ref[...], bg_ref[...]
        zero = jnp.zeros((1, tn), F32)
        gv = [zero, zero, zero, zero]
        gg = [zero, zero, zero, zero]
        for r0 in range(0, S, tr):
            cv, shv = _conv3(padv, cwv, r0, tr)
            cg, shg = _conv3(padg, cwg, r0, tr)
            cv = cv + bv
            cg = cg + bg
            sg = _sigmoid(cg)
            sl = cg * sg
            a_ref[pl.ds(r0, tr), :] = (sl * cv).astype(BF16)
            da = da_s[pl.ds(r0, tr), :]
            dcv = da * sl
            dcg = da * cv * (sg * (1.0 + cg * (1.0 - sg)))
            pdv[pl.ds(PAD + r0, tr), :] = dcv
            pdg[pl.ds(PAD + r0, tr), :] = dcg
            for k in range(3):
                gv[k] = gv[k] + jnp.sum(dcv * shv[k], axis=0, keepdims=True)
                gg[k] = gg[k] + jnp.sum(dcg * shg[k], axis=0, keepdims=True)
            gv[3] = gv[3] + jnp.sum(dcv, axis=0, keepdims=True)
            gg[3] = gg[3] + jnp.sum(dcg, axis=0, keepdims=True)
        gwv_ref[...] = _rows8(gv)
        gwg_ref[...] = _rows8(gg)
        for r0 in range(0, S, tr):
            for pd, cw, out in ((pdv, cwv, duv_ref), (pdg, cwg, dug_ref)):
                dm = pd[pl.ds(PAD - 1 + r0, tr), :]
                dc = pd[pl.ds(PAD + r0, tr), :]
                dp = pd[pl.ds(PAD + 1 + r0, tr), :]
                out[pl.ds(r0, tr), :] = (cw[0:1, :] * dp + cw[1:2, :] * dc + cw[2:3, :] * dm).astype(BF16)

    col = lambda off: (lambda j: (0, j + off))
    blk = pl.BlockSpec((S, tn), col(0))
    g8 = pl.BlockSpec((8, tn), col(0))
    return pl.pallas_call(
        body, name=name, grid=(nb,),
        in_specs=[pl.BlockSpec((S, D), lambda j: (0, 0)), pl.BlockSpec((tn, D), lambda j: (j, 0)), blk, blk,
                  pl.BlockSpec((3, tn), col(0)), pl.BlockSpec((3, tn), col(nb)),
                  pl.BlockSpec((1, tn), col(0)), pl.BlockSpec((1, tn), col(nb))],
        out_specs=[blk, blk, blk, g8, g8],
        out_shape=[jax.ShapeDtypeStruct((S, F), BF16), jax.ShapeDtypeStruct((S, F), BF16),
                   jax.ShapeDtypeStruct((S, F), BF16), jax.ShapeDtypeStruct((8, F), F32),
                   jax.ShapeDtypeStruct((8, F), F32)],
        scratch_shapes=[pltpu.VMEM((S, tn), F32)] + [pltpu.VMEM((S + 2 * PAD, tn), F32)] * 4,
        compiler_params=_cp(("parallel",)),
    )(dyb, wdown, uv, ug, wconv, wconv, bconv, bconv)


def _log_sigmoid(x):
    return jnp.minimum(x, 0.0) - jnp.log(1.0 + jnp.exp(-jnp.abs(x)))


def _gla_gate_fwd(proj, wgf, bgf, wgb, bgb, name):
    S = proj.shape[0]
    tm = _pick(S, (512, 256))

    def body(r_ref, wf_ref, bf_ref, wb_ref, bb_ref, laf_ref, lab_ref):
        r = r_ref[...].astype(BF16)
        lf = jnp.dot(r, wf_ref[...].astype(BF16), preferred_element_type=F32) + bf_ref[...]
        lb = jnp.dot(r, wb_ref[...].astype(BF16), preferred_element_type=F32) + bb_ref[...]
        laf_ref[...] = _log_sigmoid(lf) * (1.0 / GLA_GATE_NORMALIZER)
        lab_ref[...] = _log_sigmoid(lb) * (1.0 / GLA_GATE_NORMALIZER)

    full = lambda shp: pl.BlockSpec(shp, lambda i: (0, 0))
    row = pl.BlockSpec((tm, GLA_KEY), lambda i: (i, 0))
    return pl.pallas_call(
        body, name=name, grid=(S // tm,),
        in_specs=[pl.BlockSpec((tm, LANES), lambda i: (i, GLA_R_BLOCK)),
                  full((LANES, GLA_KEY)), full((1, GLA_KEY)), full((LANES, GLA_KEY)), full((1, GLA_KEY))],
        out_specs=[row, row],
        out_shape=[jax.ShapeDtypeStruct((S, GLA_KEY), F32)] * 2,
        compiler_params=_cp(("parallel",)),
    )(proj, wgf, bgf, wgb, bgb)


def _gla_gate_bwd(dlaf, dlab, proj, wgf, bgf, wgb, bgb, name):
    S = proj.shape[0]
    tm = _pick(S, (512, 256))

    def body(dlf_ref, dlb_ref, r_ref, wf_ref, bf_ref, wb_ref, bb_ref, dr_ref, dwf_ref, dbf_ref, dwb_ref, dbb_ref):
        i = pl.program_id(0)
        r = r_ref[...].astype(BF16)
        wf = wf_ref[...].astype(BF16)
        wb = wb_ref[...].astype(BF16)
        lf = jnp.dot(r, wf, preferred_element_type=F32) + bf_ref[...]
        lb = jnp.dot(r, wb, preferred_element_type=F32) + bb_ref[...]
        glf = dlf_ref[...] * (1.0 / GLA_GATE_NORMALIZER) * (1.0 / (1.0 + jnp.exp(lf)))
        glb = dlb_ref[...] * (1.0 / GLA_GATE_NORMALIZER) * (1.0 / (1.0 + jnp.exp(lb)))
        gfb = glf.astype(BF16)
        gbb = glb.astype(BF16)
        dr_ref[...] = _dg(gfb, wf, 1, 1) + _dg(gbb, wb, 1, 1)
        parts = (_dg(r, gfb, 0, 0), jnp.sum(glf, axis=0, keepdims=True),
                 _dg(r, gbb, 0, 0), jnp.sum(glb, axis=0, keepdims=True))
        outs = (dwf_ref, dbf_ref, dwb_ref, dbb_ref)

        @pl.when(i == 0)
        def _():
            for o, p in zip(outs, parts):
                o[...] = p

        @pl.when(i > 0)
        def _():
            for o, p in zip(outs, parts):
                o[...] += p

    full = lambda shp: pl.BlockSpec(shp, lambda i: (0, 0))
    row = pl.BlockSpec((tm, GLA_KEY), lambda i: (i, 0))
    return pl.pallas_call(
        body, name=name, grid=(S // tm,),
        in_specs=[row, row, pl.BlockSpec((tm, LANES), lambda i: (i, GLA_R_BLOCK)),
                  full((LANES, GLA_KEY)), full((1, GLA_KEY)), full((LANES, GLA_KEY)), full((1, GLA_KEY))],
        out_specs=[pl.BlockSpec((tm, LANES), lambda i: (i, 0)),
                   full((LANES, GLA_KEY)), full((1, GLA_KEY)), full((LANES, GLA_KEY)), full((1, GLA_KEY))],
        out_shape=[jax.ShapeDtypeStruct((S, LANES), F32),
                   jax.ShapeDtypeStruct((LANES, GLA_KEY), F32), jax.ShapeDtypeStruct((1, GLA_KEY), F32),
                   jax.ShapeDtypeStruct((LANES, GLA_KEY), F32), jax.ShapeDtypeStruct((1, GLA_KEY), F32)],
        compiler_params=_cp(("arbitrary",)),
    )(dlaf, dlab, proj, wgf, bgf, wgb, bgb)


def _gla_masks(rev):
    C = GLA_CHUNK
    t = lax.broadcasted_iota(jnp.int32, (C, C), 0)
    s = lax.broadcasted_iota(jnp.int32, (C, C), 1)
    if rev:
        return (s >= t), (s > t), (t >= s), (t > s)
    return (s <= t), (s <= t), (t <= s), (t <= s)


def _gla_chunk_common(q, k, la, cum, end_row):
    b = jnp.dot(cum.astype(F32), la, precision=HIGHEST, preferred_element_type=F32)
    bend = b[end_row:end_row + 1, :]
    e = jnp.exp(b)
    qd = q * (GLA_DK ** -0.5) * e
    ei = jnp.exp(-b)
    ee = jnp.exp(bend - b)
    d = jnp.exp(bend)
    return e, ei, ee, d, qd, k * ei, k * ee


GLA_CB = 8


def _gla_specs(S, rev_order):
    n = S // GLA_CHUNK
    cb = min(GLA_CB, n)
    nblk = n // cb
    rows = cb * GLA_CHUNK
    ci = (lambda i: nblk - 1 - i) if rev_order else (lambda i: i)
    q_spec = pl.BlockSpec((rows, GLA_DK), lambda h, i: (ci(i), h))
    k_spec = pl.BlockSpec((rows, GLA_DK), lambda h, i: (ci(i), GLA_HEADS + h))
    v_spec = pl.BlockSpec((rows, GLA_DV), lambda h, i: (ci(i), GLA_KEY * 2 // GLA_DV + h))
    la_spec = pl.BlockSpec((rows, GLA_DK), lambda h, i: (ci(i), h))
    o_spec = pl.BlockSpec((rows, GLA_DV), lambda h, i: (ci(i), h))
    st_spec = pl.BlockSpec((1, cb, GLA_DV, GLA_DK), lambda h, i: (h, ci(i), 0, 0))
    return n, cb, nblk, q_spec, k_spec, v_spec, la_spec, o_spec, st_spec


def _gla_scan_fwd(proj, la, rev, name):
    S = proj.shape[0]
    C = GLA_CHUNK
    n, cb, nblk, q_spec, k_spec, v_spec, la_spec, o_spec, st_spec = _gla_specs(S, rev)
    end_row = 0 if rev else C - 1
    order = list(range(cb))[::-1] if rev else list(range(cb))

    def body(q_ref, k_ref, v_ref, la_ref, o_ref, st_ref, state):
        i = pl.program_id(1)

        @pl.when(i == 0)
        def _():
            state[...] = jnp.zeros_like(state)

        cum, mask, _, _ = _gla_masks(rev)
        st = state[...]
        for cc in order:
            rows = pl.ds(cc * C, C)
            q, k, v, lav = q_ref[rows, :], k_ref[rows, :], v_ref[rows, :], la_ref[rows, :]
            _, _, _, d, qd, ki, ke = _gla_chunk_common(q, k, lav, cum, end_row)
            qdb, kib, keb, vb = qd.astype(BF16), ki.astype(BF16), ke.astype(BF16), v.astype(BF16)
            att = jnp.where(mask, _dg(qdb, kib, 1, 1), 0.0)
            o_ref[rows, :] = (jnp.dot(att.astype(BF16), vb, preferred_element_type=F32)
                              + _dg(qdb, st.astype(BF16), 1, 1))
            st_ref[0, cc] = st
            st = st * d + _dg(vb, keb, 0, 0)
        state[...] = st

    return pl.pallas_call(
        body, name=name, grid=(GLA_HEADS, nblk),
        in_specs=[q_spec, k_spec, v_spec, la_spec],
        out_specs=[o_spec, st_spec],
        out_shape=[jax.ShapeDtypeStruct((S, GLA_VAL), F32),
                   jax.ShapeDtypeStruct((GLA_HEADS, n, GLA_DV, GLA_DK), F32)],
        scratch_shapes=[pltpu.VMEM((GLA_DV, GLA_DK), F32)],
        compiler_params=_cp(("parallel", "arbitrary")),
    )(proj, proj, proj, la)


def _gla_scan_bwd(do, proj, la, states, rev, name):
    S = proj.shape[0]
    C = GLA_CHUNK
    n, cb, nblk, q_spec, k_spec, v_spec, la_spec, o_spec, st_spec = _gla_specs(S, not rev)
    end_row = 0 if rev else C - 1
    order = list(range(cb)) if rev else list(range(cb))[::-1]

    def body(do_ref, q_ref, k_ref, v_ref, la_ref, st_ref, dq_ref, dk_ref, dv_ref, dla_ref, gstate):
        i = pl.program_id(1)

        @pl.when(i == 0)
        def _():
            gstate[...] = jnp.zeros_like(gstate)

        cum, mask, cum_t, mask_t = _gla_masks(rev)
        g = gstate[...]
        for cc in order:
            rows = pl.ds(cc * C, C)
            q, k, v, lav = q_ref[rows, :], k_ref[rows, :], v_ref[rows, :], la_ref[rows, :]
            dov = do_ref[rows, :]
            st = st_ref[0, cc]
            e, ei, ee, d, qd, ki, ke = _gla_chunk_common(q, k, lav, cum, end_row)
            qdb, kib, keb, vb = qd.astype(BF16), ki.astype(BF16), ke.astype(BF16), v.astype(BF16)
            dob, gb, stb = dov.astype(BF16), g.astype(BF16), st.astype(BF16)
            att_t = jnp.where(mask_t, _dg(kib, qdb, 1, 1), 0.0)
            da = jnp.where(mask, _dg(dob, vb, 1, 1), 0.0)
            da_t = jnp.where(mask_t, _dg(vb, dob, 1, 1), 0.0)
            dv_ref[rows, :] = jnp.dot(att_t.astype(BF16), dob, preferred_element_type=F32) + _dg(keb, gb, 1, 1)
            dqd = (jnp.dot(da.astype(BF16), kib, preferred_element_type=F32)
                   + jnp.dot(dob, stb, preferred_element_type=F32))
            dki = jnp.dot(da_t.astype(BF16), qdb, preferred_element_type=F32)
            dke = jnp.dot(vb, gb, preferred_element_type=F32)
            dd = jnp.sum(st * g, axis=0, keepdims=True)
            g = g * d + _dg(dob, qdb, 0, 0)
            dq_ref[rows, :] = dqd * e * (GLA_DK ** -0.5)
            dk_ref[rows, :] = dki * ei + dke * ee
            dkeke = dke * ke
            db = dqd * qd - dki * ki - dkeke
            dbend = jnp.sum(dkeke, axis=0, keepdims=True) + dd * d
            dla_ref[rows, :] = jnp.dot(cum_t.astype(F32), db, precision=HIGHEST, preferred_element_type=F32) + dbend
        gstate[...] = g

    key_out = la_spec
    return pl.pallas_call(
        body, name=name, grid=(GLA_HEADS, nblk),
        in_specs=[o_spec, q_spec, k_spec, v_spec, la_spec, st_spec],
        out_specs=[key_out, key_out, o_spec, key_out],
        out_shape=[jax.ShapeDtypeStruct((S, GLA_KEY), F32), jax.ShapeDtypeStruct((S, GLA_KEY), F32),
                   jax.ShapeDtypeStruct((S, GLA_VAL), F32), jax.ShapeDtypeStruct((S, GLA_KEY), F32)],
        scratch_shapes=[pltpu.VMEM((GLA_DV, GLA_DK), F32)],
        compiler_params=_cp(("parallel", "arbitrary")),
    )(do, proj, proj, proj, la, states)


def _gla_out_fwd(of, ob, proj, gn, name):
    S = of.shape[0]
    tm = _pick(S, (256, 128))
    gblk = (2 * GLA_KEY + GLA_VAL) // GLA_VAL

    def body(of_ref, ob_ref, g_ref, gn_ref, z_ref):
        gnv = gn_ref[...]
        for h in range(GLA_HEADS):
            cols = pl.ds(h * GLA_DV, GLA_DV)
            o = of_ref[:, cols] + ob_ref[:, cols]
            r = lax.rsqrt(jnp.mean(o * o, axis=-1, keepdims=True) + NORM_EPS)
            gv = g_ref[:, cols]
            z_ref[:, cols] = (o * r * gnv * (gv * _sigmoid(gv))).astype(BF16)

    row = pl.BlockSpec((tm, GLA_VAL), lambda i: (i, 0))
    return pl.pallas_call(
        body, name=name, grid=(S // tm,),
        in_specs=[row, row, pl.BlockSpec((tm, GLA_VAL), lambda i: (i, gblk)),
                  pl.BlockSpec((1, GLA_DV), lambda i: (0, 0))],
        out_specs=row,
        out_shape=jax.ShapeDtypeStruct((S, GLA_VAL), BF16),
        compiler_params=_cp(("parallel",)),
    )(of, ob, proj, gn)


def _gla_out_bwd(dz, of, ob, proj, gn, name):
    S = of.shape[0]
    tm = _pick(S, (256, 128))
    gblk = (2 * GLA_KEY + GLA_VAL) // GLA_VAL

    def body(dz_ref, of_ref, ob_ref, g_ref, gn_ref, do_ref, dg_ref, dgn_ref):
        i = pl.program_id(0)
        gnv = gn_ref[...]
        part = jnp.zeros((1, GLA_DV), F32)
        for h in range(GLA_HEADS):
            cols = pl.ds(h * GLA_DV, GLA_DV)
            o = of_ref[:, cols] + ob_ref[:, cols]
            r = lax.rsqrt(jnp.mean(o * o, axis=-1, keepdims=True) + NORM_EPS)
            y = o * r
            gv = g_ref[:, cols]
            sg = _sigmoid(gv)
            dzv = dz_ref[:, cols]
            dg_ref[:, cols] = dzv * (y * gnv) * (sg * (1.0 + gv * (1.0 - sg)))
            don = dzv * (gv * sg)
            part = part + jnp.sum(don * y, axis=0, keepdims=True)
            dy = don * gnv
            do_ref[:, cols] = r * (dy - y * jnp.mean(dy * y, axis=-1, keepdims=True))

        @pl.when(i == 0)
        def _():
            dgn_ref[...] = part

        @pl.when(i > 0)
        def _():
            dgn_ref[...] += part

    row = pl.BlockSpec((tm, GLA_VAL), lambda i: (i, 0))
    one = pl.BlockSpec((1, GLA_DV), lambda i: (0, 0))
    return pl.pallas_call(
        body, name=name, grid=(S // tm,),
        in_specs=[row, row, row, pl.BlockSpec((tm, GLA_VAL), lambda i: (i, gblk)), one],
        out_specs=[row, row, one],
        out_shape=[jax.ShapeDtypeStruct((S, GLA_VAL), F32), jax.ShapeDtypeStruct((S, GLA_VAL), F32),
                   jax.ShapeDtypeStruct((1, GLA_DV), F32)],
        compiler_params=_cp(("arbitrary",)),
    )(dz, of, ob, proj, gn)


N_QK_HEADS = ATT_QH + ATT_KVH


def _qk_prep_fwd(proj, qn, kn, rc, rs, name):
    S = proj.shape[0]
    tm = _pick(S, (256, 128))
    W = N_QK_HEADS * ATT_HD
    scale = ATT_HD ** -0.5

    def body(p_ref, qn_ref, kn_ref, rc_ref, rs_ref, v_in_ref, qk_ref, v_ref, kt_ref, vt_ref):
        c, s = rc_ref[...], rs_ref[...]
        for h in range(N_QK_HEADS):
            cols = pl.ds(h * ATT_HD, ATT_HD)
            w = qn_ref[...] if h < ATT_QH else kn_ref[...]
            xv = p_ref[:, cols]
            r = lax.rsqrt(jnp.mean(xv * xv, axis=-1, keepdims=True) + NORM_EPS)
            y = xv * r * w
            out = y * c + pltpu.roll(y, ATT_HD // 2, 1) * s
            if h < ATT_QH:
                qk_ref[:, cols] = (out * scale).astype(BF16)
            else:
                qk_ref[:, cols] = out.astype(BF16)
                kt_ref[pl.ds((h - ATT_QH) * ATT_HD, ATT_HD), :] = out.T.astype(BF16)
        v_ref[...] = v_in_ref[...].astype(BF16)
        for h in range(ATT_KVH):
            vt_ref[pl.ds(h * ATT_HD, ATT_HD), :] = v_in_ref[:, pl.ds(h * ATT_HD, ATT_HD)].T.astype(BF16)

    one = pl.BlockSpec((1, ATT_HD), lambda i: (0, 0))
    tab = pl.BlockSpec((tm, ATT_HD), lambda i: (i, 0))
    vw = ATT_KVH * ATT_HD
    tr = pl.BlockSpec((vw, tm), lambda i: (0, i))
    return pl.pallas_call(
        body, name=name, grid=(S // tm,),
        in_specs=[pl.BlockSpec((tm, W), lambda i: (i, 0)), one, one, tab, tab,
                  pl.BlockSpec((tm, vw), lambda i: (i, W // vw))],
        out_specs=[pl.BlockSpec((tm, W), lambda i: (i, 0)), pl.BlockSpec((tm, vw), lambda i: (i, 0)), tr, tr],
        out_shape=[jax.ShapeDtypeStruct((S, W), BF16), jax.ShapeDtypeStruct((S, vw), BF16),
                   jax.ShapeDtypeStruct((vw, S), BF16), jax.ShapeDtypeStruct((vw, S), BF16)],
        compiler_params=_cp(("parallel",)),
    )(proj, qn, kn, rc, rs, proj)


def _qk_prep_bwd(dqk, proj, qn, kn, rc, rs, name):
    S = proj.shape[0]
    tm = _pick(S, (256, 128))
    W = N_QK_HEADS * ATT_HD

    def body(d_ref, p_ref, qn_ref, kn_ref, rc_ref, rs_ref, dp_ref, dqn_ref, dkn_ref):
        i = pl.program_id(0)
        c, s = rc_ref[...], rs_ref[...]
        parts = [jnp.zeros((1, ATT_HD), F32), jnp.zeros((1, ATT_HD), F32)]
        for h in range(N_QK_HEADS):
            cols = pl.ds(h * ATT_HD, ATT_HD)
            w = qn_ref[...] if h < ATT_QH else kn_ref[...]
            dout = d_ref[:, cols]
            dy = dout * c + pltpu.roll(dout * s, ATT_HD // 2, 1)
            xv = p_ref[:, cols]
            r = lax.rsqrt(jnp.mean(xv * xv, axis=-1, keepdims=True) + NORM_EPS)
            xr = xv * r
            which = 0 if h < ATT_QH else 1
            parts[which] = parts[which] + jnp.sum(dy * xr, axis=0, keepdims=True)
            dxr = dy * w
            dp_ref[:, cols] = r * (dxr - xr * jnp.mean(dxr * xr, axis=-1, keepdims=True))

        @pl.when(i == 0)
        def _():
            dqn_ref[...] = parts[0]
            dkn_ref[...] = parts[1]

        @pl.when(i > 0)
        def _():
            dqn_ref[...] += parts[0]
            dkn_ref[...] += parts[1]

    one = pl.BlockSpec((1, ATT_HD), lambda i: (0, 0))
    tab = pl.BlockSpec((tm, ATT_HD), lambda i: (i, 0))
    row = pl.BlockSpec((tm, W), lambda i: (i, 0))
    return pl.pallas_call(
        body, name=name, grid=(S // tm,),
        in_specs=[row, row, one, one, tab, tab],
        out_specs=[row, one, one],
        out_shape=[jax.ShapeDtypeStruct((S, W), F32), jax.ShapeDtypeStruct((1, ATT_HD), F32),
                   jax.ShapeDtypeStruct((1, ATT_HD), F32)],
        compiler_params=_cp(("arbitrary",)),
    )(dqk, proj, qn, kn, rc, rs)


ATT_TQ = 256
LSE_ROWS = 8


def _attn_fwd(qk, vt, name):
    S = qk.shape[0]
    tq = min(ATT_TQ, S)

    def body(q_ref, k_ref, vt_ref, o_ref, lse_ref):
        st = _dg(k_ref[...], q_ref[...], 1, 1)
        m = jnp.max(st, axis=0, keepdims=True)
        pt = jnp.exp(st - m)
        l = jnp.sum(pt, axis=0, keepdims=True)
        ot = jnp.dot(vt_ref[...], pt.astype(BF16), preferred_element_type=F32)
        o_ref[...] = (ot * (1.0 / l)).T
        lse_ref[...] = jnp.broadcast_to(m + jnp.log(l), (LSE_ROWS, tq))

    qo = pl.BlockSpec((tq, ATT_HD), lambda h, i: (i, h))
    return pl.pallas_call(
        body, name=name, grid=(ATT_QH, S // tq),
        in_specs=[qo, pl.BlockSpec((S, ATT_HD), lambda h, i: (0, ATT_QH + h // ATT_GROUP)),
                  pl.BlockSpec((ATT_HD, S), lambda h, i: (h // ATT_GROUP, 0))],
        out_specs=[qo, pl.BlockSpec((LSE_ROWS, tq), lambda h, i: (h, i))],
        out_shape=[jax.ShapeDtypeStruct((S, ATT_QH * ATT_HD), F32),
                   jax.ShapeDtypeStruct((ATT_QH * LSE_ROWS, S), F32)],
        compiler_params=_cp(("parallel", "parallel")),
    )(qk, qk, vt)


def _attn_bwd(do, o, lse, qk, v, kt, name):
    S = qk.shape[0]
    tq = min(ATT_TQ, S)
    scale = ATT_HD ** -0.5

    def body(do_ref, o_ref, lse_ref, q_ref, k_ref, v_ref, kt_ref, dq_ref, dk_ref, dv_ref):
        g = pl.program_id(1)
        i = pl.program_id(2)

        @pl.when((g == 0) & (i == 0))
        def _():
            dk_ref[...] = jnp.zeros_like(dk_ref)
            dv_ref[...] = jnp.zeros_like(dv_ref)

        q = q_ref[...]
        dov = do_ref[...]
        dob = dov.astype(BF16)
        delta = jnp.sum((dov * o_ref[...]).T, axis=0, keepdims=True)
        st = _dg(k_ref[...], q, 1, 1)
        pt = jnp.exp(st - lse_ref[0:1, :])
        dpt = _dg(v_ref[...], dob, 1, 1)
        dst = (pt * (dpt - delta)).astype(BF16)
        dv_ref[...] += jnp.dot(pt.astype(BF16), dob, preferred_element_type=F32)
        dk_ref[...] += jnp.dot(dst, q, preferred_element_type=F32)
        dq_ref[...] = jnp.dot(kt_ref[...], dst, preferred_element_type=F32).T * scale

    qo = pl.BlockSpec((tq, ATT_HD), lambda kv, g, i: (i, kv * ATT_GROUP + g))
    kvo = pl.BlockSpec((S, ATT_HD), lambda kv, g, i: (0, kv))
    return pl.pallas_call(
        body, name=name, grid=(ATT_KVH, ATT_GROUP, S // tq),
        in_specs=[qo, qo, pl.BlockSpec((LSE_ROWS, tq), lambda kv, g, i: (kv * ATT_GROUP + g, i)), qo,
                  pl.BlockSpec((S, ATT_HD), lambda kv, g, i: (0, ATT_QH + kv)), kvo,
                  pl.BlockSpec((ATT_HD, S), lambda kv, g, i: (kv, 0))],
        out_specs=[qo, kvo, kvo],
        out_shape=[jax.ShapeDtypeStruct((S, ATT_QH * ATT_HD), F32),
                   jax.ShapeDtypeStruct((S, ATT_KVH * ATT_HD), F32),
                   jax.ShapeDtypeStruct((S, ATT_KVH * ATT_HD), F32)],
        compiler_params=_cp(("parallel", "arbitrary", "arbitrary")),
    )(do, o, lse, qk, qk, v, kt)


def _adamw(w, g, m, v, name):
    rows, cols = w.shape
    tr = rows
    for cand in (512, 256, 128, 64, 32, 16, 8):
        if rows % cand == 0 and cand * cols * 4 <= 2 * 1024 * 1024:
            tr = cand
            break

    def body(w_ref, g_ref, m_ref, v_ref, d_ref, nm_ref, nv_ref):
        gv = g_ref[...]
        nm = ADAM_B1 * m_ref[...] + (1.0 - ADAM_B1) * gv
        nv = ADAM_B2 * v_ref[...] + (1.0 - ADAM_B2) * (gv * gv)
        m_hat = nm / (1.0 - ADAM_B1 ** ADAM_STEP)
        v_hat = nv / (1.0 - ADAM_B2 ** ADAM_STEP)
        d_ref[...] = -ADAM_LR * (m_hat / (jnp.sqrt(v_hat) + ADAM_EPS) + ADAM_WD * w_ref[...])
        nm_ref[...] = nm
        nv_ref[...] = nv

    blk = pl.BlockSpec((tr, cols), lambda i: (i, 0))
    return pl.pallas_call(
        body, name=name, grid=(rows // tr,),
        in_specs=[blk] * 4, out_specs=[blk] * 3,
        out_shape=[jax.ShapeDtypeStruct((rows, cols), F32)] * 3,
        compiler_params=_cp(("parallel",)),
    )(w, g, m, v)


ANY = pl.BlockSpec(memory_space=pl.ANY)


def _place():
    return lax.axis_index("x"), lax.axis_index("y"), lax.axis_index("c")


def _other_chips(x, y):
    return [(1 - x, y), (x, 1 - y), (1 - x, 1 - y)]


def _half_rows(c, H):
    return pl.ds(pl.multiple_of(c * H, 8), H)


def _allgather_chips(bufs, name):
    n = len(bufs)
    halves = [b.shape[1] // 2 for b in bufs]

    def body(*refs):
        outs = refs[n:2 * n]
        send_sems, recv_sems = refs[2 * n:]
        x, y, c = _place()
        me = 2 * x + y
        sibling = (x, y, 1 - c)
        chips = _other_chips(x, y)

        def copy(k, sem, p, core, to):
            blk = outs[k].at[p, _half_rows(core, halves[k])]
            return pltpu.make_async_remote_copy(src_ref=blk, dst_ref=blk, send_sem=send_sems.at[6 * k + sem],
                                                recv_sem=recv_sems.at[6 * k + sem], device_id=to,
                                                device_id_type=MESH)

        first = [copy(k, j, me, c, (px, py, c)) for k in range(n) for j, (px, py) in enumerate(chips)]
        for cp in first:
            cp.start()
        passed = []
        for k in range(n):
            for j, (px, py) in enumerate(chips):
                copy(k, j, 2 * px + py, c, (px, py, c)).wait_recv()
                fwd = copy(k, 3 + j, 2 * px + py, c, sibling)
                fwd.start()
                passed.append(fwd)
        for k in range(n):
            for j, (px, py) in enumerate(chips):
                copy(k, 3 + j, 2 * px + py, 1 - c, sibling).wait_recv()
        for cp in first + passed:
            cp.wait_send()

    return pl.pallas_call(
        body, name=name,
        in_specs=[ANY] * n, out_specs=[ANY] * n,
        out_shape=[jax.ShapeDtypeStruct(b.shape, b.dtype) for b in bufs],
        input_output_aliases={k: k for k in range(n)},
        scratch_shapes=[pltpu.SemaphoreType.DMA((6 * n,)), pltpu.SemaphoreType.DMA((6 * n,))],
    )(*bufs)


def _allreduce_small(v, name):
    R = v.shape[0]
    n_dev = 8

    def body(v_ref, sum_ref, all_ref, send_sems, recv_sems, local_sem):
        x, y, c = _place()
        me, sibling = (x, y, c), (x, y, 1 - c)
        chips = _other_chips(x, y)

        def rows(px, py, pc):
            return all_ref.at[pl.ds(pl.multiple_of((4 * px + 2 * py + pc) * R, 8), R), :]

        def copy(k, block, to, src=None):
            return pltpu.make_async_remote_copy(
                src_ref=rows(*block) if src is None else src, dst_ref=rows(*block),
                send_sem=send_sems.at[k], recv_sem=recv_sems.at[k], device_id=to, device_id_type=MESH)

        own = pltpu.make_async_copy(v_ref, rows(*me), local_sem)
        own.start()
        first = [copy(0, me, sibling, src=v_ref)]
        first += [copy(1 + j, me, (*chip, c), src=v_ref) for j, chip in enumerate(chips)]
        for cp in first:
            cp.start()
        passed = [copy(4 + j, (*chip, c), sibling) for j, chip in enumerate(chips)]
        for j, chip in enumerate(chips):
            copy(1 + j, (*chip, c), me).wait_recv()
            passed[j].start()
        copy(0, sibling, me).wait_recv()
        for j, chip in enumerate(chips):
            copy(4 + j, (*chip, 1 - c), me).wait_recv()
        for cp in first + passed:
            cp.wait_send()
        own.wait()
        acc = all_ref[pl.ds(0, R), :]
        for d in range(1, n_dev):
            acc = acc + all_ref[pl.ds(d * R, R), :]
        sum_ref[...] = acc

    vm = pl.BlockSpec(memory_space=pltpu.VMEM)
    return pl.pallas_call(
        body, name=name,
        in_specs=[vm], out_specs=[vm, vm],
        out_shape=[jax.ShapeDtypeStruct((R, LANES), F32), jax.ShapeDtypeStruct((n_dev * R, LANES), F32)],
        scratch_shapes=[pltpu.SemaphoreType.DMA((7,)), pltpu.SemaphoreType.DMA((7,)), pltpu.SemaphoreType.DMA],
    )(v)[0]


def _swap_other_half(bufs, name):
    n = len(bufs)
    halves = [b.shape[1] // 2 for b in bufs]

    def body(*refs):
        g_refs, got_refs = refs[:n], refs[n:2 * n]
        send_sems, recv_sems = refs[2 * n:]
        x, y, c = _place()
        copies = [pltpu.make_async_remote_copy(
            src_ref=g_refs[k].at[p, _half_rows(1 - c, halves[k])], dst_ref=got_refs[k].at[p],
            send_sem=send_sems.at[N_CHIPS * k + p], recv_sem=recv_sems.at[N_CHIPS * k + p],
            device_id=(x, y, 1 - c), device_id_type=MESH) for k in range(n) for p in range(N_CHIPS)]
        for cp in copies:
            cp.start()
        for cp in copies:
            cp.wait_recv()
        for cp in copies:
            cp.wait_send()

    return pl.pallas_call(
        body, name=name, in_specs=[ANY] * n, out_specs=[ANY] * n,
        out_shape=[jax.ShapeDtypeStruct((N_CHIPS, h, b.shape[2]), b.dtype) for b, h in zip(bufs, halves)],
        scratch_shapes=[pltpu.SemaphoreType.DMA((N_CHIPS * n,)), pltpu.SemaphoreType.DMA((N_CHIPS * n,))],
    )(*bufs)


def _send_chip_partials(sbs, name):
    n = len(sbs)

    def body(*refs):
        s_refs, got_refs = refs[:n], refs[n:2 * n]
        send_sems, recv_sems = refs[2 * n:]
        x, y, c = _place()
        me = 2 * x + y
        chips = _other_chips(x, y)

        def copy(k, j, src_slot, dst_slot):
            px, py = chips[j]
            return pltpu.make_async_remote_copy(
                src_ref=s_refs[k].at[src_slot], dst_ref=got_refs[k].at[dst_slot], send_sem=send_sems.at[3 * k + j],
                recv_sem=recv_sems.at[3 * k + j], device_id=(px, py, c), device_id_type=MESH)

        copies = [copy(k, j, 2 * px + py, me) for k in range(n) for j, (px, py) in enumerate(chips)]
        for cp in copies:
            cp.start()
        for k in range(n):
            for j, (px, py) in enumerate(chips):
                copy(k, j, me, 2 * px + py).wait_recv()
        for cp in copies:
            cp.wait_send()

    return pl.pallas_call(
        body, name=name, in_specs=[ANY] * n, out_specs=[ANY] * n,
        out_shape=[jax.ShapeDtypeStruct(s.shape, s.dtype) for s in sbs],
        scratch_shapes=[pltpu.SemaphoreType.DMA((3 * n,)), pltpu.SemaphoreType.DMA((3 * n,))],
    )(*sbs)


def _join_halves(bufs, name):
    n = len(bufs)
    halves = [b.shape[0] // 2 for b in bufs]

    def body(*refs):
        outs = refs[n:2 * n]
        send_sems, recv_sems = refs[2 * n:]
        x, y, c = _place()

        def copy(k, core):
            blk = outs[k].at[_half_rows(core, halves[k])]
            return pltpu.make_async_remote_copy(src_ref=blk, dst_ref=blk, send_sem=send_sems.at[k],
                                                recv_sem=recv_sems.at[k], device_id=(x, y, 1 - c),
                                                device_id_type=MESH)

        sends = [copy(k, c) for k in range(n)]
        for cp in sends:
            cp.start()
        for k in range(n):
            copy(k, 1 - c).wait_recv()
        for cp in sends:
            cp.wait_send()

    return pl.pallas_call(
        body, name=name, in_specs=[ANY] * n, out_specs=[ANY] * n,
        out_shape=[jax.ShapeDtypeStruct(b.shape, b.dtype) for b in bufs],
        input_output_aliases={k: k for k in range(n)},
        scratch_shapes=[pltpu.SemaphoreType.DMA((n,)), pltpu.SemaphoreType.DMA((n,))],
    )(*bufs)


def _rs_rows(H, width):
    for cand in (1024, 512, 256, 128, 64, 32, 16):
        if H % cand == 0 and cand * width * 4 <= 1536 * 1024:
            return cand
    return H


def _add_sibling(g, got, c, me, name):
    _, H, width = got.shape
    tb = _rs_rows(H, width)
    nb = H // tb

    def body(sp_ref, g_ref, got_ref, sb_ref, sf_ref):
        p = pl.program_id(1)
        s = g_ref[0] + got_ref[0]
        sb_ref[0] = s.astype(BF16)

        @pl.when(p == sp_ref[1])
        def _():
            sf_ref[...] = s

    grid_spec = pltpu.PrefetchScalarGridSpec(
        num_scalar_prefetch=1, grid=(nb, N_CHIPS),
        in_specs=[pl.BlockSpec((1, tb, width), lambda i, p, sp: (p, sp[0] * nb + i, 0)),
                  pl.BlockSpec((1, tb, width), lambda i, p, sp: (p, i, 0))],
        out_specs=[pl.BlockSpec((1, tb, width), lambda i, p, sp: (p, i, 0)),
                   pl.BlockSpec((tb, width), lambda i, p, sp: (i, 0))])
    return pl.pallas_call(
        body, name=name, grid_spec=grid_spec,
        out_shape=[jax.ShapeDtypeStruct((N_CHIPS, H, width), BF16), jax.ShapeDtypeStruct((H, width), F32)],
        compiler_params=_cp(("arbitrary", "arbitrary")),
    )(jnp.stack([c, me]).astype(jnp.int32), g, got)


def _add_chips(sf, got, others_and_c, name):
    H, width = sf.shape
    tb = _rs_rows(H, width)
    nb = H // tb

    def body(sp_ref, sf_ref, r1_ref, r2_ref, r3_ref, out_ref):
        out_ref[...] = ((sf_ref[...] + r1_ref[0].astype(F32)) + r2_ref[0].astype(F32)) + r3_ref[0].astype(F32)

    def slot(k):
        return pl.BlockSpec((1, tb, width), lambda i, sp: (sp[k], i, 0))

    blk = pl.BlockSpec((tb, width), lambda i, sp: (i, 0))
    grid_spec = pltpu.PrefetchScalarGridSpec(
        num_scalar_prefetch=1, grid=(nb,), in_specs=[blk, slot(0), slot(1), slot(2)],
        out_specs=pl.BlockSpec((tb, width), lambda i, sp: (sp[3] * nb + i, 0)))
    return pl.pallas_call(
        body, name=name, grid_spec=grid_spec,
        out_shape=jax.ShapeDtypeStruct((2 * H, width), F32),
        compiler_params=_cp(("arbitrary",)),
    )(others_and_c.astype(jnp.int32), sf, got, got, got)


def _reduce_scatter(bufs, c, me, tag):
    n = len(bufs)
    gots = _swap_other_half(bufs, f"{tag}_to_sibling")
    sums = [_add_sibling(bufs[k], gots[k], c, me, f"{tag}_add_sibling{k}") for k in range(n)]
    landed = _send_chip_partials([s[0] for s in sums], f"{tag}_to_chips")
    others_and_c = jnp.stack([jnp.where(me <= k, k + 1, k) for k in range(N_CHIPS - 1)] + [c])
    halves = [_add_chips(sums[k][1], landed[k], others_and_c, f"{tag}_add_chips{k}") for k in range(n)]
    return _join_halves(halves, f"{tag}_join_halves")


REPLICATED = ("norm_mix", "norm_ffn", "gla_b_gate_f", "gla_b_gate_b", "gla_norm", "attn_q_norm", "attn_k_norm",
              "ffn_b_conv")


PIECE_ROWS = 16


def _piece_rows(shape):
    n = 1
    for s in shape:
        n *= s
    rows = n // LANES
    return rows, -(-rows // PIECE_ROWS) * PIECE_ROWS


def _pack(pieces, dtype, row_multiple):
    flat = []
    for p in pieces:
        rows, padded = _piece_rows(p.shape)
        flat.append(jnp.pad(p.astype(dtype).reshape(rows, LANES), ((0, padded - rows), (0, 0))))
    rows = sum(f.shape[0] for f in flat)
    padded = -(-rows // row_multiple) * row_multiple
    if padded > rows:
        flat.append(jnp.zeros((padded - rows, LANES), dtype))
    return jnp.concatenate(flat, axis=0)


def _unpack(buf, shapes):
    out, r = [], 0
    for shp in shapes:
        rows, padded = _piece_rows(shp)
        out.append(buf[r:r + rows].reshape(shp))
        r += padded
    return out


BIG = ("ffn_w_up", "ffn_w_down", "attn_w_qkv", "gla_w_out", "attn_w_out", "gla_w_in")
SMALL_SHARDED = ("gla_w_gate_up_f", "gla_w_gate_up_b", "ffn_w_conv")


def _own_slot(shard2d, me):
    return lax.dynamic_update_index_in_dim(jnp.zeros((N_CHIPS,) + shard2d.shape, shard2d.dtype), shard2d, me, 0)


def _gather_weights(w, me, name):
    two_d = lambda t: t.reshape(-1, t.shape[-1])
    bufs = [_own_slot(two_d(w[n]).astype(BF16), me) for n in BIG]
    bufs.append(_own_slot(_pack([w[n] for n in SMALL_SHARDED], F32, 32), me))
    got = _allgather_chips(bufs, name)
    G = dict(zip(BIG, got[:-1]))
    layers = lambda t, L: t.reshape(N_CHIPS, L, t.shape[1] // L, t.shape[2])
    rows_major = lambda t, L: [layers(t, L)[:, l].reshape(-1, t.shape[2]) for l in range(L)]
    small = [_unpack(got[-1][p], [w[n].shape for n in SMALL_SHARDED]) for p in range(N_CHIPS)]
    full_small = [jnp.concatenate([small[p][k] for p in range(N_CHIPS)], axis=2) for k in range(len(SMALL_SHARDED))]
    win = layers(G["gla_w_in"], DEPTH // 2)
    return dict(
        up=G["ffn_w_up"], qkv=G["attn_w_qkv"],
        down=rows_major(G["ffn_w_down"], DEPTH),
        gla_out=rows_major(G["gla_w_out"], DEPTH // 2),
        attn_out=rows_major(G["attn_w_out"], DEPTH // 2),
        gla_in=[jnp.pad(jnp.concatenate([win[p, l] for p in range(N_CHIPS)], axis=1),
                        ((0, 0), (0, GLA_IN_PAD - GLA_IN))) for l in range(DEPTH // 2)],
        gate_f=full_small[0], gate_b=full_small[1], conv=full_small[2])


def _rope_tables(S):
    rows = S // GRID_W
    row_idx = jnp.repeat(jnp.arange(rows, dtype=F32), GRID_W)
    col_idx = jnp.tile(jnp.arange(GRID_W, dtype=F32), rows)
    pairs = ATT_HD // 4
    inv_freq = ROPE_THETA ** (-jnp.arange(pairs, dtype=F32) / pairs)
    ang = jnp.concatenate([row_idx[:, None] * inv_freq, col_idx[:, None] * inv_freq], axis=-1)
    cos, sin = jnp.cos(ang), jnp.sin(ang)
    return jnp.concatenate([cos, cos], axis=-1), jnp.concatenate([-sin, sin], axis=-1)


def _gate_rows(w, first_row):
    return jnp.zeros((LANES, GLA_KEY), F32).at[first_row:first_row + GLA_RANK].set(w.astype(F32))


def _local_step(x, target, W, P):
    S = x.shape[0]
    rc, rs = _rope_tables(S)
    row = lambda a: a.reshape(1, -1)
    saved = []
    for i in range(DEPTH):
        j = i // 2
        nm = row(P["norm_mix"][i])
        h1 = _rmsnorm_fwd(x, nm, f"norm_mix_fwd{i}")
        if i % 2 == 0:
            wgf = _gate_rows(W["gate_f"][j], 0)
            wgb = _gate_rows(W["gate_b"][j], GLA_RANK)
            bgf, bgb = row(P["gla_b_gate_f"][j]), row(P["gla_b_gate_b"][j])
            gn = row(P["gla_norm"][j])
            proj = _matmul(h1, W["gla_in"][j], 1, 0, f"gla_in{i}")
            laf, lab = _gla_gate_fwd(proj, wgf, bgf, wgb, bgb, f"gla_gate_fwd{i}")
            of, stf = _gla_scan_fwd(proj, laf, False, f"gla_scan_f_fwd{i}")
            ob, stb = _gla_scan_fwd(proj, lab, True, f"gla_scan_b_fwd{i}")
            z = _gla_out_fwd(of, ob, proj, gn, f"gla_out_fwd{i}")
            xm = _matmul(z, W["gla_out"][j], 1, 0, f"gla_outproj{i}", res=x)
            mix = dict(proj=proj, laf=laf, lab=lab, of=of, ob=ob, stf=stf, stb=stb, z=z, wgf=wgf, wgb=wgb)
        else:
            proj = _matmul(h1, W["qkv"], 1, 0, f"attn_qkv{i}", b_layer=j)
            qn, kn = row(P["attn_q_norm"][j]), row(P["attn_k_norm"][j])
            qk, vb, kt, vt = _qk_prep_fwd(proj, qn, kn, rc, rs, f"qk_prep_fwd{i}")
            o, lse = _attn_fwd(qk, vt, f"attn_fwd{i}")
            xm = _matmul(o, W["attn_out"][j], 1, 0, f"attn_outproj{i}", res=x)
            mix = dict(proj=proj, qk=qk, vb=vb, kt=kt, o=o, lse=lse)
        h2 = _rmsnorm_fwd(xm, row(P["norm_ffn"][i]), f"norm_ffn_fwd{i}")
        a, uv, ug = _ffn_mid_fwd(h2, W["up"], i, W["conv"][i], row(P["ffn_b_conv"][i]), f"ffn_mid_fwd{i}")
        xo = _matmul(a, W["down"][i], 1, 0, f"ffn_down{i}", res=xm)
        saved.append(dict(x=x, h1=h1, xm=xm, h2=h2, uv=uv, ug=ug, mix=mix))
        x = xo

    dx, dxb, loss = _loss_grad(x, target, "loss")

    G = {n: [None] * (DEPTH if n.startswith(("norm", "ffn")) else DEPTH // 2)
         for n in ("gla_w_in",) + SMALL_SHARDED + REPLICATED}
    G.update(up=None, down=None, out=None, qkv=None)
    for i in reversed(range(DEPTH)):
        j = i // 2
        sv = saved[i]
        mix = sv["mix"]
        duv, dug, a, gwv, gwg = _ffn_mid_bwd(dxb, W["down"][i], sv["uv"], sv["ug"], W["conv"][i],
                                             row(P["ffn_b_conv"][i]), f"ffn_mid_bwd{i}")
        G["down"] = _matmul(a, dxb, 0, 0, f"ffn_down_wgrad{i}", out_chips=("rows", i, DEPTH), into=G["down"])
        G["up"] = _matmul(sv["h2"], (duv, dug), 0, 0, f"ffn_up_wgrad{i}", out_chips=("cols", i, DEPTH),
                          into=G["up"])
        G["ffn_w_conv"][i] = jnp.concatenate([gwv[:3], gwg[:3]], axis=1)
        G["ffn_b_conv"][i] = jnp.concatenate([gwv[3], gwg[3]], axis=0)
        dxm, dxmb, dn = _dgrad_norm((duv, dug), W["up"], sv["xm"], row(P["norm_ffn"][i]), dx, f"ffn_up_dgrad{i}",
                                    w_layer=i)
        G["norm_ffn"][i] = dn[0]
        if i % 2 == 0:
            proj = mix["proj"]
            bgf, bgb = row(P["gla_b_gate_f"][j]), row(P["gla_b_gate_b"][j])
            gn = row(P["gla_norm"][j])
            dz = _matmul(dxmb, W["gla_out"][j], 1, 1, f"gla_outproj_dgrad{i}")
            G["out"] = _matmul(mix["z"], dxmb, 0, 0, f"gla_outproj_wgrad{i}", out_chips=("rows", i, DEPTH),
                               into=G["out"])
            do, dg, dgn = _gla_out_bwd(dz, mix["of"], mix["ob"], proj, gn, f"gla_out_bwd{i}")
            G["gla_norm"][j] = dgn[0]
            dqf, dkf, dvf, dlaf = _gla_scan_bwd(do, proj, mix["laf"], mix["stf"], False, f"gla_scan_f_bwd{i}")
            dqb, dkb, dvb, dlab = _gla_scan_bwd(do, proj, mix["lab"], mix["stb"], True, f"gla_scan_b_bwd{i}")
            dr, dwf, dbf, dwb, dbb = _gla_gate_bwd(dlaf, dlab, proj, mix["wgf"], bgf, mix["wgb"], bgb,
                                                   f"gla_gate_bwd{i}")
            G["gla_w_gate_up_f"][j] = dwf[:GLA_RANK]
            G["gla_w_gate_up_b"][j] = dwb[GLA_RANK:2 * GLA_RANK]
            G["gla_b_gate_f"][j] = dbf[0]
            G["gla_b_gate_b"][j] = dbb[0]
            dproj = jnp.concatenate([dqf + dqb, dkf + dkb, dvf + dvb, dg, dr], axis=1).astype(BF16)
            G["gla_w_in"][j] = _matmul(sv["h1"], dproj, 0, 0, f"gla_in_wgrad{i}")
            dx, dxb, dn = _dgrad_norm(dproj, W["gla_in"][j], sv["x"], row(P["norm_mix"][i]), dxm, f"mix_in_dgrad{i}")
        else:
            proj = mix["proj"]
            qn, kn = row(P["attn_q_norm"][j]), row(P["attn_k_norm"][j])
            do = _matmul(dxmb, W["attn_out"][j], 1, 1, f"attn_outproj_dgrad{i}")
            G["out"] = _matmul(mix["o"], dxmb, 0, 0, f"attn_outproj_wgrad{i}", out_chips=("rows", i, DEPTH),
                               into=G["out"])
            dq, dk, dv = _attn_bwd(do, mix["o"], mix["lse"], mix["qk"], mix["vb"], mix["kt"], f"attn_bwd{i}")
            dqk = jnp.concatenate([dq, dk], axis=1)
            dpqk, dqn, dkn = _qk_prep_bwd(dqk, proj, qn, kn, rc, rs, f"qk_prep_bwd{i}")
            G["attn_q_norm"][j] = dqn[0]
            G["attn_k_norm"][j] = dkn[0]
            dproj = jnp.concatenate([dpqk, dv], axis=1).astype(BF16)
            G["qkv"] = _matmul(sv["h1"], dproj, 0, 0, f"attn_qkv_wgrad{i}", out_chips=("cols", j, DEPTH // 2),
                               into=G["qkv"])
            dx, dxb, dn = _dgrad_norm(dproj, W["qkv"], sv["x"], row(P["norm_mix"][i]), dxm, f"mix_in_dgrad{i}",
                                      w_layer=j)
        G["norm_mix"][i] = dn[0]
    return loss, dx, G


def kernel(x, norm_mix, norm_ffn, gla_w_in, gla_w_gate_up_f, gla_b_gate_f, gla_w_gate_up_b, gla_b_gate_b, gla_norm, gla_w_out, attn_w_qkv, attn_q_norm, attn_k_norm, attn_w_out, ffn_w_up, ffn_w_conv, ffn_b_conv, ffn_w_down, loss_target, m_norm_mix, m_norm_ffn, m_gla_w_in, m_gla_w_gate_up_f, m_gla_b_gate_f, m_gla_w_gate_up_b, m_gla_b_gate_b, m_gla_norm, m_gla_w_out, m_attn_w_qkv, m_attn_q_norm, m_attn_k_norm, m_attn_w_out, m_ffn_w_up, m_ffn_w_conv, m_ffn_b_conv, m_ffn_w_down, v_norm_mix, v_norm_ffn, v_gla_w_in, v_gla_w_gate_up_f, v_gla_b_gate_f, v_gla_w_gate_up_b, v_gla_b_gate_b, v_gla_norm, v_gla_w_out, v_attn_w_qkv, v_attn_q_norm, v_attn_k_norm, v_attn_w_out, v_ffn_w_up, v_ffn_w_conv, v_ffn_b_conv, v_ffn_w_down):
    names = ("norm_mix", "norm_ffn", "gla_w_in", "gla_w_gate_up_f", "gla_b_gate_f", "gla_w_gate_up_b",
             "gla_b_gate_b", "gla_norm", "gla_w_out", "attn_w_qkv", "attn_q_norm", "attn_k_norm", "attn_w_out",
             "ffn_w_up", "ffn_w_conv", "ffn_b_conv", "ffn_w_down")
    w = dict(zip(names, (norm_mix, norm_ffn, gla_w_in, gla_w_gate_up_f, gla_b_gate_f, gla_w_gate_up_b,
                         gla_b_gate_b, gla_norm, gla_w_out, attn_w_qkv, attn_q_norm, attn_k_norm, attn_w_out,
                         ffn_w_up, ffn_w_conv, ffn_b_conv, ffn_w_down)))
    m = dict(zip(names, (m_norm_mix, m_norm_ffn, m_gla_w_in, m_gla_w_gate_up_f, m_gla_b_gate_f,
                         m_gla_w_gate_up_b, m_gla_b_gate_b, m_gla_norm, m_gla_w_out, m_attn_w_qkv, m_attn_q_norm,
                         m_attn_k_norm, m_attn_w_out, m_ffn_w_up, m_ffn_w_conv, m_ffn_b_conv, m_ffn_w_down)))
    v = dict(zip(names, (v_norm_mix, v_norm_ffn, v_gla_w_in, v_gla_w_gate_up_f, v_gla_b_gate_f,
                         v_gla_w_gate_up_b, v_gla_b_gate_b, v_gla_norm, v_gla_w_out, v_attn_w_qkv, v_attn_q_norm,
                         v_attn_k_norm, v_attn_w_out, v_ffn_w_up, v_ffn_w_conv, v_ffn_b_conv, v_ffn_w_down)))
    px, py, pc = _place()
    me = 2 * px + py

    W = _gather_weights(w, me, "gather_weights")
    P = {n: w[n] for n in REPLICATED}

    loss_part, dx, grads = _local_step(x[0], loss_target[0], W, P)

    win_width = w["gla_w_in"].shape[2]
    win = jnp.stack([jnp.concatenate([g[:, p * win_width:(p + 1) * win_width] for g in grads["gla_w_in"]], axis=0)
                     for p in range(N_CHIPS)])
    small_g = []
    for p in range(N_CHIPS):
        cut = [lax.slice_in_dim(jnp.stack(grads[n]), p * w[n].shape[2], (p + 1) * w[n].shape[2], axis=2)
               for n in SMALL_SHARDED]
        small_g.append(_pack(cut, F32, 256))
    bufs = [grads["up"], grads["down"], grads["out"], grads["qkv"], win, jnp.stack(small_g)]
    up, down, out, qkv, win, small_g = _reduce_scatter(bufs, pc, me, "grads")
    out = out.reshape(DEPTH // 2, 2, -1, out.shape[1])
    gsh = dict(ffn_w_up=up, ffn_w_down=down, gla_w_out=out[:, 0], attn_w_out=out[:, 1], attn_w_qkv=qkv,
               gla_w_in=win)
    gsh = {n: g.reshape(w[n].shape) for n, g in gsh.items()}
    gsh.update(dict(zip(SMALL_SHARDED, _unpack(small_g, [w[n].shape for n in SMALL_SHARDED]))))

    small = _pack([jnp.stack(grads[n]) for n in REPLICATED] + [loss_part], F32, 16)
    small_sum = _allreduce_small(small, "small_allreduce")
    parts = _unpack(small_sum, [w[n].shape for n in REPLICATED] + [(1, LANES)])
    gsh.update(dict(zip(REPLICATED, parts[:-1])))
    loss = parts[-1][0, 0]

    delta, new_m, new_v = {}, {}, {}
    for n in names:
        shp = w[n].shape
        two_d = (-1, shp[-1])
        d, nm, nv = _adamw(w[n].reshape(two_d), gsh[n].reshape(two_d), m[n].reshape(two_d), v[n].reshape(two_d),
                           f"adamw_{n}")
        delta[n], new_m[n], new_v[n] = d.reshape(shp), nm.reshape(shp), nv.reshape(shp)

    return (loss, dx[None], *[gsh[n] for n in names], *[delta[n] for n in names],
            *[new_m[n] for n in names], *[new_v[n] for n in names])
```

```python
import jax
import jax.numpy as jnp
from jax import lax
from jax.experimental import pallas as pl
from jax.experimental.pallas import tpu as pltpu

F32 = jnp.float32
BF16 = jnp.bfloat16
MESH = pl.DeviceIdType.MESH
HIGHEST = lax.Precision.HIGHEST

D_MODEL = 1024
DEPTH = 4
GRID_W = 64
NORM_EPS = 1e-6
GLA_HEADS = 4
GLA_DK = 128
GLA_DV = 256
GLA_KEY = GLA_HEADS * GLA_DK
GLA_VAL = GLA_HEADS * GLA_DV
GLA_RANK = 16
GLA_CHUNK = 64
GLA_GATE_NORMALIZER = 16.0
GLA_IN = 2 * GLA_KEY + 2 * GLA_VAL + 2 * GLA_RANK
GLA_IN_PAD = 3200
GLA_R_BLOCK = (2 * GLA_KEY + 2 * GLA_VAL) // 128
ATT_HD = 128
ATT_QH = 8
ATT_KVH = 2
ATT_GROUP = ATT_QH // ATT_KVH
ATT_QKV = (ATT_QH + 2 * ATT_KVH) * ATT_HD
ROPE_THETA = 10000.0
D_FF = 2816
ADAM_LR = 0.001
ADAM_B1 = 0.9
ADAM_B2 = 0.999
ADAM_EPS = 1e-08
ADAM_WD = 0.01
ADAM_STEP = 10

N_CHIPS = 4
LANES = 128
VMEM_LIMIT = 56 * 1024 * 1024


def _cp(sem):
    return pltpu.CompilerParams(dimension_semantics=sem, vmem_limit_bytes=VMEM_LIMIT)


def _pick(n, cands):
    for c in cands:
        if n % c == 0:
            return c
    return n


def _dg(a, b, ca, cb):
    return lax.dot_general(a, b, (((ca,), (cb,)), ((), ())), preferred_element_type=F32)


def _sigmoid(x):
    return 0.5 * jnp.tanh(0.5 * x) + 0.5


def _rmsnorm_fwd(x, w, name):
    S, D = x.shape
    tm = _pick(S, (512, 256))

    def body(x_ref, w_ref, h_ref):
        xv = x_ref[...]
        r = lax.rsqrt(jnp.mean(xv * xv, axis=-1, keepdims=True) + NORM_EPS)
        h_ref[...] = (xv * r * w_ref[...]).astype(BF16)

    return pl.pallas_call(
        body, name=name, grid=(S // tm,),
        in_specs=[pl.BlockSpec((tm, D), lambda i: (i, 0)), pl.BlockSpec((1, D), lambda i: (0, 0))],
        out_specs=pl.BlockSpec((tm, D), lambda i: (i, 0)),
        out_shape=jax.ShapeDtypeStruct((S, D), BF16),
        compiler_params=_cp(("parallel",)),
    )(x, w)


def _loss_grad(y, t, name):
    S, D = y.shape
    tm = _pick(S, (512, 256))

    def body(y_ref, t_ref, dy_ref, dyb_ref, loss_ref):
        i = pl.program_id(0)
        d = y_ref[...] - t_ref[...]
        dy = d * (1.0 / D)
        dy_ref[...] = dy
        dyb_ref[...] = dy.astype(BF16)
        sq = jnp.sum(jnp.sum(d * d, axis=1, keepdims=True), axis=0, keepdims=True)
        part = jnp.broadcast_to(sq * (0.5 / D), (1, LANES))

        @pl.when(i == 0)
        def _():
            loss_ref[...] = part

        @pl.when(i > 0)
        def _():
            loss_ref[...] += part

    return pl.pallas_call(
        body, name=name, grid=(S // tm,),
        in_specs=[pl.BlockSpec((tm, D), lambda i: (i, 0)), pl.BlockSpec((tm, D), lambda i: (i, 0))],
        out_specs=[pl.BlockSpec((tm, D), lambda i: (i, 0)), pl.BlockSpec((tm, D), lambda i: (i, 0)),
                   pl.BlockSpec((1, LANES), lambda i: (0, 0))],
        out_shape=[jax.ShapeDtypeStruct((S, D), F32), jax.ShapeDtypeStruct((S, D), BF16),
                   jax.ShapeDtypeStruct((1, LANES), F32)],
        compiler_params=_cp(("arbitrary",)),
    )(y, t)


def _matmul(a, b, ca, cb, name, res=None, out_dtype=F32, out_chips=None, into=None):
    M, K = a.shape[1 - ca], a.shape[ca]
    pair = isinstance(b, (tuple, list))
    if pair:
        assert cb == 0 and b[0].shape == b[1].shape and b[0].shape[0] == K
        N = 2 * b[0].shape[1]
    else:
        assert b.shape[cb] == K
        N = b.shape[1 - cb]
    how = out_chips[0] if out_chips else None
    tm = M if how == "rows" else _pick(M, (1024, 1408, 512, 256, 128))
    if how == "cols":
        tn = N // N_CHIPS
    else:
        tn = _pick(N // 2 if pair else N, (1024, 1408, 768, 640, 512, 256, 128))
    tk = _pick(K, (512, 1408, 256, 128))
    nk = K // tk
    n0 = (N // 2) // tn
    if ca == 1:
        a_spec = pl.BlockSpec((tm, tk), lambda i, j, k: (i, k))
    else:
        a_spec = pl.BlockSpec((tk, tm), lambda i, j, k: (k, i))
    if pair:
        b_specs = [pl.BlockSpec((tk, tn), lambda i, j, k: (k, jnp.minimum(j, n0 - 1))),
                   pl.BlockSpec((tk, tn), lambda i, j, k: (k, jnp.maximum(j - n0, 0)))]
    elif cb == 0:
        b_specs = [pl.BlockSpec((tk, tn), lambda i, j, k: (k, j))]
    else:
        b_specs = [pl.BlockSpec((tn, tk), lambda i, j, k: (j, k))]
    if how == "cols":
        _, layer, layers = out_chips
        o_spec = pl.BlockSpec((None, tm, tn), lambda i, j, k: (j, layer * (M // tm) + i, 0))
        out_shape = jax.ShapeDtypeStruct((N_CHIPS, layers * M, tn), out_dtype)
    elif how == "rows":
        _, layer, layers = out_chips
        o_spec = pl.BlockSpec((N_CHIPS, M // N_CHIPS, tn), lambda i, j, k: (0, layer, j))
        out_shape = jax.ShapeDtypeStruct((N_CHIPS, layers * M // N_CHIPS, N), out_dtype)
    else:
        o_spec = pl.BlockSpec((tm, tn), lambda i, j, k: (i, j))
        out_shape = jax.ShapeDtypeStruct((M, N), out_dtype)
    has_res = res is not None
    nb = len(b_specs)

    def body(*refs):
        a_ref, b_refs = refs[0], refs[1:1 + nb]
        r_ref = refs[1 + nb] if has_res else None
        o_ref, acc = refs[-2], refs[-1]
        j = pl.program_id(1)
        k = pl.program_id(2)

        @pl.when(k == 0)
        def _():
            acc[...] = jnp.zeros_like(acc)

        av = a_ref[...].astype(BF16)
        if pair:
            @pl.when(j < n0)
            def _():
                acc[...] += _dg(av, b_refs[0][...].astype(BF16), ca, cb)

            @pl.when(j >= n0)
            def _():
                acc[...] += _dg(av, b_refs[1][...].astype(BF16), ca, cb)
        else:
            acc[...] += _dg(av, b_refs[0][...].astype(BF16), ca, cb)

        @pl.when(k == nk - 1)
        def _():
            v = acc[...]
            if has_res:
                v = v + r_ref[...]
            if how == "rows":
                rows = M // N_CHIPS
                for p in range(N_CHIPS):
                    o_ref[p] = v[p * rows:(p + 1) * rows, :].astype(out_dtype)
            else:
                o_ref[...] = v.astype(out_dtype)

    in_specs = [a_spec] + b_specs + ([o_spec] if has_res else [])
    args = (a,) + (tuple(b) if pair else (b,)) + ((res,) if has_res else ())
    aliases = {}
    if into is not None:
        assert into.shape == out_shape.shape
        in_specs.append(ANY)
        aliases = {len(args): 0}
        args = args + (into,)
        inner = body

        def body(*refs):
            inner(*refs[:len(args) - 1], *refs[len(args):])

    return pl.pallas_call(
        body, name=name, grid=(M // tm, N // tn, nk),
        in_specs=in_specs, out_specs=o_spec, out_shape=out_shape,
        input_output_aliases=aliases,
        scratch_shapes=[pltpu.VMEM((tm, tn), F32)],
        compiler_params=_cp(("parallel", "parallel", "arbitrary")),
    )(*args)


def _matmul_rows(a, w, name, res=None, w_layer=None, transposed=False):
    M, K = a.shape
    if w_layer is not None:
        cw = w.shape[2]
        N = N_CHIPS * cw
        w_spec = pl.BlockSpec((N_CHIPS, K, cw), lambda i: (0, w_layer, 0))
    else:
        N = w.shape[0] if transposed else w.shape[1]
        assert w.shape[1 if transposed else 0] == K
        w_spec = pl.BlockSpec(w.shape, lambda i: (0, 0))
    tm = _pick(M, (512, 256, 128))
    has_res = res is not None

    def body(*refs):
        a_ref, w_ref = refs[0], refs[1]
        r_ref = refs[2] if has_res else None
        o_ref = refs[-1]
        av = a_ref[...].astype(BF16)
        if w_layer is not None:
            for p in range(N_CHIPS):
                o_ref[:, pl.ds(p * cw, cw)] = jnp.dot(av, w_ref[p], preferred_element_type=F32)
        else:
            v = _dg(av, w_ref[...], 1, 1 if transposed else 0)
            o_ref[...] = v + r_ref[...] if has_res else v

    row = pl.BlockSpec((tm, N), lambda i: (i, 0))
    return pl.pallas_call(
        body, name=name, grid=(M // tm,),
        in_specs=[pl.BlockSpec((tm, K), lambda i: (i, 0)), w_spec] + ([row] if has_res else []),
        out_specs=row, out_shape=jax.ShapeDtypeStruct((M, N), F32),
        compiler_params=_cp(("parallel",)),
    )(*((a, w) + ((res,) if has_res else ())))


def _dgrad_norm(dy, w, x, wn, dres, name, w_layer=None):
    pair = isinstance(dy, (tuple, list))
    M = dy[0].shape[0] if pair else dy.shape[0]
    Kp = 2 * dy[0].shape[1] if pair else dy.shape[1]
    D = x.shape[1]
    if w_layer is not None:
        cw = w.shape[2]
        assert N_CHIPS * cw == Kp and w.shape[1] % D == 0
        w_spec = pl.BlockSpec((N_CHIPS, D, cw), lambda i: (0, w_layer, 0))
    else:
        assert w.shape == (D, Kp)
        w_spec = pl.BlockSpec((D, Kp), lambda i: (0, 0))
    tm = _pick(M, (256, 128))
    width = Kp // 2 if pair else Kp
    dy_specs = [pl.BlockSpec((tm, width), lambda i: (i, 0))] * (2 if pair else 1)
    nd = len(dy_specs)

    def body(*refs):
        dy_refs = refs[:nd]
        w_ref, x_ref, wn_ref, dres_ref, dx_ref, dxb_ref, dwn_ref = refs[nd:]
        i = pl.program_id(0)
        if w_layer is not None:
            dh = None
            for p in range(N_CHIPS):
                src, off = divmod(p * cw, width)
                part = _dg(dy_refs[src][:, pl.ds(off, cw)], w_ref[p], 1, 1)
                dh = part if dh is None else dh + part
        else:
            dh = _dg(dy_refs[0][...], w_ref[...], 1, 1)
        xv = x_ref[...]
        r = lax.rsqrt(jnp.mean(xv * xv, axis=-1, keepdims=True) + NORM_EPS)
        yv = xv * r
        dyv = dh * wn_ref[...]
        dxv = r * (dyv - yv * jnp.mean(dyv * yv, axis=-1, keepdims=True)) + dres_ref[...]
        dx_ref[...] = dxv
        dxb_ref[...] = dxv.astype(BF16)
        part = jnp.sum(dh * yv, axis=0, keepdims=True)

        @pl.when(i == 0)
        def _():
            dwn_ref[...] = part

        @pl.when(i > 0)
        def _():
            dwn_ref[...] += part

    row = pl.BlockSpec((tm, D), lambda i: (i, 0))
    one = pl.BlockSpec((1, D), lambda i: (0, 0))
    return pl.pallas_call(
        body, name=name, grid=(M // tm,),
        in_specs=dy_specs + [w_spec, row, one, row],
        out_specs=[row, row, one],
        out_shape=[jax.ShapeDtypeStruct((M, D), F32), jax.ShapeDtypeStruct((M, D), BF16),
                   jax.ShapeDtypeStruct((1, D), F32)],
        compiler_params=_cp(("arbitrary",)),
    )(*(tuple(dy) if pair else (dy,)), w, x, wn, dres)


FFN_TN_FWD = 256
FFN_TN_BWD = 128
FFN_ROWS = 256
PAD = 8


def _conv3(pad_ref, w, r0, tr):
    um = pad_ref[pl.ds(PAD - 1 + r0, tr), :]
    uc = pad_ref[pl.ds(PAD + r0, tr), :]
    up = pad_ref[pl.ds(PAD + 1 + r0, tr), :]
    return w[0:1, :] * um + w[1:2, :] * uc + w[2:3, :] * up, (um, uc, up)


def _zero_pads(pad_ref, S, tn):
    pad_ref[pl.ds(0, PAD), :] = jnp.zeros((PAD, tn), F32)
    pad_ref[pl.ds(PAD + S, PAD), :] = jnp.zeros((PAD, tn), F32)


def _ffn_mid_fwd(h, wup, wconv, bconv, name):
    S, D = h.shape
    F = wup.shape[1] // 2
    tn = FFN_TN_FWD
    nb = F // tn
    tr = min(FFN_ROWS, S)

    def body(h_ref, wv_ref, wg_ref, cv_ref, cg_ref, bv_ref, bg_ref, a_ref, uv_ref, ug_ref):
        _zero_pads(uv_ref, S, tn)
        _zero_pads(ug_ref, S, tn)
        hv = h_ref[...]
        uv_ref[pl.ds(PAD, S), :] = jnp.dot(hv, wv_ref[...], preferred_element_type=F32)
        ug_ref[pl.ds(PAD, S), :] = jnp.dot(hv, wg_ref[...], preferred_element_type=F32)
        cwv, cwg, bv, bg = cv_ref[...], cg_ref[...], bv_ref[...], bg_ref[...]
        for r0 in range(0, S, tr):
            cv = _conv3(uv_ref, cwv, r0, tr)[0] + bv
            cg = _conv3(ug_ref, cwg, r0, tr)[0] + bg
            a_ref[pl.ds(r0, tr), :] = (cg * _sigmoid(cg) * cv).astype(BF16)

    col = lambda off: (lambda j: (0, j + off))
    padded = pl.BlockSpec((S + 2 * PAD, tn), col(0))
    return pl.pallas_call(
        body, name=name, grid=(nb,),
        in_specs=[pl.BlockSpec((S, D), lambda j: (0, 0)),
                  pl.BlockSpec((D, tn), col(0)), pl.BlockSpec((D, tn), col(nb)),
                  pl.BlockSpec((3, tn), col(0)), pl.BlockSpec((3, tn), col(nb)),
                  pl.BlockSpec((1, tn), col(0)), pl.BlockSpec((1, tn), col(nb))],
        out_specs=[pl.BlockSpec((S, tn), col(0)), padded, padded],
        out_shape=[jax.ShapeDtypeStruct((S, F), BF16), jax.ShapeDtypeStruct((S + 2 * PAD, F), F32),
                   jax.ShapeDtypeStruct((S + 2 * PAD, F), F32)],
        compiler_params=_cp(("parallel",)),
    )(h, wup, wup, wconv, wconv, bconv, bconv)


def _rows8(rows):
    n = rows[0].shape[1]
    idx = lax.broadcasted_iota(jnp.int32, (8, n), 0)
    out = jnp.zeros((8, n), F32)
    for k, r in enumerate(rows):
        out = jnp.where(idx == k, r, out)
    return out


def _ffn_mid_bwd(dyb, wdown, uv, ug, wconv, bconv, name):
    S, D = dyb.shape
    F = wdown.shape[0]
    tn = FFN_TN_BWD
    nb = F // tn
    tr = min(FFN_ROWS, S)

    def body(dy_ref, wd_ref, uv_ref, ug_ref, cv_ref, cg_ref, bv_ref, bg_ref,
             duv_ref, dug_ref, a_ref, gwv_ref, gwg_ref, pdv, pdg):
        for p in (pdv, pdg):
            _zero_pads(p, S, tn)
        wd = wd_ref[...]
        cwv, cwg, bv, bg = cv_ref[...], cg_ref[...], bv_ref[...], bg_ref[...]
        zero = jnp.zeros((1, tn), F32)
        gv = [zero, zero, zero, zero]
        gg = [zero, zero, zero, zero]
        for r0 in range(0, S, tr):
            cv, shv = _conv3(uv_ref, cwv, r0, tr)
            cg, shg = _conv3(ug_ref, cwg, r0, tr)
            cv = cv + bv
            cg = cg + bg
            sg = _sigmoid(cg)
            sl = cg * sg
            a_ref[pl.ds(r0, tr), :] = (sl * cv).astype(BF16)
            da = _dg(dy_ref[pl.ds(r0, tr), :], wd, 1, 1)
            dcv = da * sl
            dcg = da * cv * (sg * (1.0 + cg * (1.0 - sg)))
            pdv[pl.ds(PAD + r0, tr), :] = dcv
            pdg[pl.ds(PAD + r0, tr), :] = dcg
            for k in range(3):
                gv[k] = gv[k] + jnp.sum(dcv * shv[k], axis=0, keepdims=True)
                gg[k] = gg[k] + jnp.sum(dcg * shg[k], axis=0, keepdims=True)
            gv[3] = gv[3] + jnp.sum(dcv, axis=0, keepdims=True)
            gg[3] = gg[3] + jnp.sum(dcg, axis=0, keepdims=True)
        gwv_ref[...] = _rows8(gv)
        gwg_ref[...] = _rows8(gg)
        for r0 in range(0, S, tr):
            for pd, cw, out in ((pdv, cwv, duv_ref), (pdg, cwg, dug_ref)):
                dm = pd[pl.ds(PAD - 1 + r0, tr), :]
                dc = pd[pl.ds(PAD + r0, tr), :]
                dp = pd[pl.ds(PAD + 1 + r0, tr), :]
                out[pl.ds(r0, tr), :] = (cw[0:1, :] * dp + cw[1:2, :] * dc + cw[2:3, :] * dm).astype(BF16)

    col = lambda off: (lambda j: (0, j + off))
    blk = pl.BlockSpec((S, tn), col(0))
    padded = pl.BlockSpec((S + 2 * PAD, tn), col(0))
    g8 = pl.BlockSpec((8, tn), col(0))
    return pl.pallas_call(
        body, name=name, grid=(nb,),
        in_specs=[pl.BlockSpec((S, D), lambda j: (0, 0)), pl.BlockSpec((tn, D), lambda j: (j, 0)), padded, padded,
                  pl.BlockSpec((3, tn), col(0)), pl.BlockSpec((3, tn), col(nb)),
                  pl.BlockSpec((1, tn), col(0)), pl.BlockSpec((1, tn), col(nb))],
        out_specs=[blk, blk, blk, g8, g8],
        out_shape=[jax.ShapeDtypeStruct((S, F), BF16), jax.ShapeDtypeStruct((S, F), BF16),
                   jax.ShapeDtypeStruct((S, F), BF16), jax.ShapeDtypeStruct((8, F), F32),
                   jax.ShapeDtypeStruct((8, F), F32)],
        scratch_shapes=[pltpu.VMEM((S + 2 * PAD, tn), F32)] * 2,
        compiler_params=_cp(("parallel",)),
    )(dyb, wdown, uv, ug, wconv, wconv, bconv, bconv)


def _log_sigmoid(x):
    return jnp.minimum(x, 0.0) - jnp.log(1.0 + jnp.exp(-jnp.abs(x)))


def _gla_gate_fwd(proj, wgf, bgf, wgb, bgb, name):
    S = proj.shape[0]
    tm = _pick(S, (512, 256))

    def body(r_ref, wf_ref, bf_ref, wb_ref, bb_ref, laf_ref, lab_ref):
        r = r_ref[...].astype(BF16)
        lf = jnp.dot(r, wf_ref[...].astype(BF16), preferred_element_type=F32) + bf_ref[...]
        lb = jnp.dot(r, wb_ref[...].astype(BF16), preferred_element_type=F32) + bb_ref[...]
        laf_ref[...] = _log_sigmoid(lf) * (1.0 / GLA_GATE_NORMALIZER)
        lab_ref[...] = _log_sigmoid(lb) * (1.0 / GLA_GATE_NORMALIZER)

    full = lambda shp: pl.BlockSpec(shp, lambda i: (0, 0))
    row = pl.BlockSpec((tm, GLA_KEY), lambda i: (i, 0))
    return pl.pallas_call(
        body, name=name, grid=(S // tm,),
        in_specs=[pl.BlockSpec((tm, LANES), lambda i: (i, GLA_R_BLOCK)),
                  full((LANES, GLA_KEY)), full((1, GLA_KEY)), full((LANES, GLA_KEY)), full((1, GLA_KEY))],
        out_specs=[row, row],
        out_shape=[jax.ShapeDtypeStruct((S, GLA_KEY), F32)] * 2,
        compiler_params=_cp(("parallel",)),
    )(proj, wgf, bgf, wgb, bgb)


def _gla_gate_bwd(dlaf, dlab, proj, wgf, bgf, wgb, bgb, name):
    S = proj.shape[0]
    tm = _pick(S, (512, 256))

    def body(dlf_ref, dlb_ref, r_ref, wf_ref, bf_ref, wb_ref, bb_ref, dr_ref, dwf_ref, dbf_ref, dwb_ref, dbb_ref):
        i = pl.program_id(0)
        r = r_ref[...].astype(BF16)
        wf = wf_ref[...].astype(BF16)
        wb = wb_ref[...].astype(BF16)
        lf = jnp.dot(r, wf, preferred_element_type=F32) + bf_ref[...]
        lb = jnp.dot(r, wb, preferred_element_type=F32) + bb_ref[...]
        glf = dlf_ref[...] * (1.0 / GLA_GATE_NORMALIZER) * (1.0 / (1.0 + jnp.exp(lf)))
        glb = dlb_ref[...] * (1.0 / GLA_GATE_NORMALIZER) * (1.0 / (1.0 + jnp.exp(lb)))
        gfb = glf.astype(BF16)
        gbb = glb.astype(BF16)
        dr_ref[...] = _dg(gfb, wf, 1, 1) + _dg(gbb, wb, 1, 1)
        parts = (_dg(r, gfb, 0, 0), jnp.sum(glf, axis=0, keepdims=True),
                 _dg(r, gbb, 0, 0), jnp.sum(glb, axis=0, keepdims=True))
        outs = (dwf_ref, dbf_ref, dwb_ref, dbb_ref)

        @pl.when(i == 0)
        def _():
            for o, p in zip(outs, parts):
                o[...] = p

        @pl.when(i > 0)
        def _():
            for o, p in zip(outs, parts):
                o[...] += p

    full = lambda shp: pl.BlockSpec(shp, lambda i: (0, 0))
    row = pl.BlockSpec((tm, GLA_KEY), lambda i: (i, 0))
    return pl.pallas_call(
        body, name=name, grid=(S // tm,),
        in_specs=[row, row, pl.BlockSpec((tm, LANES), lambda i: (i, GLA_R_BLOCK)),
                  full((LANES, GLA_KEY)), full((1, GLA_KEY)), full((LANES, GLA_KEY)), full((1, GLA_KEY))],
        out_specs=[pl.BlockSpec((tm, LANES), lambda i: (i, 0)),
                   full((LANES, GLA_KEY)), full((1, GLA_KEY)), full((LANES, GLA_KEY)), full((1, GLA_KEY))],
        out_shape=[jax.ShapeDtypeStruct((S, LANES), F32),
                   jax.ShapeDtypeStruct((LANES, GLA_KEY), F32), jax.ShapeDtypeStruct((1, GLA_KEY), F32),
                   jax.ShapeDtypeStruct((LANES, GLA_KEY), F32), jax.ShapeDtypeStruct((1, GLA_KEY), F32)],
        compiler_params=_cp(("arbitrary",)),
    )(dlaf, dlab, proj, wgf, bgf, wgb, bgb)


def _gla_masks(rev):
    C = GLA_CHUNK
    t = lax.broadcasted_iota(jnp.int32, (C, C), 0)
    s = lax.broadcasted_iota(jnp.int32, (C, C), 1)
    if rev:
        return (s >= t), (s > t), (t >= s), (t > s)
    return (s <= t), (s <= t), (t <= s), (t <= s)


def _cum_dot(cum, x):
    return jnp.dot(cum.astype(F32), x, precision=HIGHEST, preferred_element_type=F32)


def _gla_chunk_common(q, k, la, cum, end_row):
    b = _cum_dot(cum, la)
    bend = b[end_row:end_row + 1, :]
    e = jnp.exp(b)
    qd = q * (GLA_DK ** -0.5) * e
    ei = jnp.exp(-b)
    ee = jnp.exp(bend - b)
    d = jnp.exp(bend)
    return e, ei, ee, d, qd, k * ei, k * ee


GLA_CB = 8


def _gla_specs(S, rev_order):
    n = S // GLA_CHUNK
    cb = min(GLA_CB, n)
    nblk = n // cb
    rows = cb * GLA_CHUNK
    ci = (lambda i: nblk - 1 - i) if rev_order else (lambda i: i)
    q_spec = pl.BlockSpec((rows, GLA_DK), lambda h, i: (ci(i), h))
    k_spec = pl.BlockSpec((rows, GLA_DK), lambda h, i: (ci(i), GLA_HEADS + h))
    v_spec = pl.BlockSpec((rows, GLA_DV), lambda h, i: (ci(i), GLA_KEY * 2 // GLA_DV + h))
    la_spec = pl.BlockSpec((rows, GLA_DK), lambda h, i: (ci(i), h))
    o_spec = pl.BlockSpec((rows, GLA_DV), lambda h, i: (ci(i), h))
    st_spec = pl.BlockSpec((1, cb, GLA_DV, GLA_DK), lambda h, i: (h, ci(i), 0, 0))
    return n, cb, nblk, q_spec, k_spec, v_spec, la_spec, o_spec, st_spec


def _gla_scan_fwd(proj, la, rev, name):
    S = proj.shape[0]
    C = GLA_CHUNK
    n, cb, nblk, q_spec, k_spec, v_spec, la_spec, o_spec, st_spec = _gla_specs(S, rev)
    end_row = 0 if rev else C - 1
    order = list(range(cb))[::-1] if rev else list(range(cb))

    def body(q_ref, k_ref, v_ref, la_ref, o_ref, st_ref, state):
        i = pl.program_id(1)

        @pl.when(i == 0)
        def _():
            state[...] = jnp.zeros_like(state)

        cum, mask, _, _ = _gla_masks(rev)
        pre, intra, kv = {}, {}, {}
        for cc in order:
            rows = pl.ds(cc * C, C)
            q, k, v, lav = q_ref[rows, :], k_ref[rows, :], v_ref[rows, :], la_ref[rows, :]
            _, _, _, d, qd, ki, ke = _gla_chunk_common(q, k, lav, cum, end_row)
            qdb, kib, keb, vb = qd.astype(BF16), ki.astype(BF16), ke.astype(BF16), v.astype(BF16)
            pre[cc] = (d, qdb)
            att = jnp.where(mask, _dg(qdb, kib, 1, 1), 0.0)
            intra[cc] = jnp.dot(att.astype(BF16), vb, preferred_element_type=F32)
            kv[cc] = _dg(vb, keb, 0, 0)
        st = state[...]
        for cc in order:
            d, qdb = pre[cc]
            o_ref[pl.ds(cc * C, C), :] = intra[cc] + _dg(qdb, st.astype(BF16), 1, 1)
            st_ref[0, cc] = st
            st = st * d + kv[cc]
        state[...] = st

    return pl.pallas_call(
        body, name=name, grid=(GLA_HEADS, nblk),
        in_specs=[q_spec, k_spec, v_spec, la_spec],
        out_specs=[o_spec, st_spec],
        out_shape=[jax.ShapeDtypeStruct((S, GLA_VAL), F32),
                   jax.ShapeDtypeStruct((GLA_HEADS, n, GLA_DV, GLA_DK), F32)],
        scratch_shapes=[pltpu.VMEM((GLA_DV, GLA_DK), F32)],
        compiler_params=_cp(("parallel", "arbitrary")),
    )(proj, proj, proj, la)


def _gla_scan_bwd(do, proj, la, states, rev, name):
    S = proj.shape[0]
    C = GLA_CHUNK
    n, cb, nblk, q_spec, k_spec, v_spec, la_spec, o_spec, st_spec = _gla_specs(S, not rev)
    end_row = 0 if rev else C - 1
    order = list(range(cb)) if rev else list(range(cb))[::-1]

    def body(do_ref, q_ref, k_ref, v_ref, la_ref, st_ref, dq_ref, dk_ref, dv_ref, dla_ref, gstate):
        i = pl.program_id(1)

        @pl.when(i == 0)
        def _():
            gstate[...] = jnp.zeros_like(gstate)

        cum, mask, cum_t, mask_t = _gla_masks(rev)
        g = gstate[...]
        for cc in order:
            rows = pl.ds(cc * C, C)
            q, k, v, lav = q_ref[rows, :], k_ref[rows, :], v_ref[rows, :], la_ref[rows, :]
            dov = do_ref[rows, :]
            st = st_ref[0, cc]
            e, ei, ee, d, qd, ki, ke = _gla_chunk_common(q, k, lav, cum, end_row)
            qdb, kib, keb, vb = qd.astype(BF16), ki.astype(BF16), ke.astype(BF16), v.astype(BF16)
            dob, gb, stb = dov.astype(BF16), g.astype(BF16), st.astype(BF16)
            att_t = jnp.where(mask_t, _dg(kib, qdb, 1, 1), 0.0)
            da = jnp.where(mask, _dg(dob, vb, 1, 1), 0.0)
            da_t = jnp.where(mask_t, _dg(vb, dob, 1, 1), 0.0)
            dv_ref[rows, :] = jnp.dot(att_t.astype(BF16), dob, preferred_element_type=F32) + _dg(keb, gb, 1, 1)
            dqd = (jnp.dot(da.astype(BF16), kib, preferred_element_type=F32)
                   + jnp.dot(dob, stb, preferred_element_type=F32))
            dki = jnp.dot(da_t.astype(BF16), qdb, preferred_element_type=F32)
            dke = jnp.dot(vb, gb, preferred_element_type=F32)
            dd = jnp.sum(st * g, axis=0, keepdims=True)
            g = g * d + _dg(dob, qdb, 0, 0)
            dq_ref[rows, :] = dqd * e * (GLA_DK ** -0.5)
            dk_ref[rows, :] = dki * ei + dke * ee
            dkeke = dke * ke
            db = dqd * qd - dki * ki - dkeke
            dbend = jnp.sum(dkeke, axis=0, keepdims=True) + dd * d
            dla_ref[rows, :] = _cum_dot(cum_t, db) + dbend
        gstate[...] = g

    key_out = la_spec
    return pl.pallas_call(
        body, name=name, grid=(GLA_HEADS, nblk),
        in_specs=[o_spec, q_spec, k_spec, v_spec, la_spec, st_spec],
        out_specs=[key_out, key_out, o_spec, key_out],
        out_shape=[jax.ShapeDtypeStruct((S, GLA_KEY), F32), jax.ShapeDtypeStruct((S, GLA_KEY), F32),
                   jax.ShapeDtypeStruct((S, GLA_VAL), F32), jax.ShapeDtypeStruct((S, GLA_KEY), F32)],
        scratch_shapes=[pltpu.VMEM((GLA_DV, GLA_DK), F32)],
        compiler_params=_cp(("parallel", "arbitrary")),
    )(do, proj, proj, proj, la, states)


def _gla_out_fwd(of, ob, proj, gn, name):
    S = of.shape[0]
    tm = _pick(S, (256, 128))
    gblk = (2 * GLA_KEY + GLA_VAL) // GLA_VAL

    def body(of_ref, ob_ref, g_ref, gn_ref, z_ref):
        gnv = gn_ref[...]
        for h in range(GLA_HEADS):
            cols = pl.ds(h * GLA_DV, GLA_DV)
            o = of_ref[:, cols] + ob_ref[:, cols]
            r = lax.rsqrt(jnp.mean(o * o, axis=-1, keepdims=True) + NORM_EPS)
            gv = g_ref[:, cols]
            z_ref[:, cols] = (o * r * gnv * (gv * _sigmoid(gv))).astype(BF16)

    row = pl.BlockSpec((tm, GLA_VAL), lambda i: (i, 0))
    return pl.pallas_call(
        body, name=name, grid=(S // tm,),
        in_specs=[row, row, pl.BlockSpec((tm, GLA_VAL), lambda i: (i, gblk)),
                  pl.BlockSpec((1, GLA_DV), lambda i: (0, 0))],
        out_specs=row,
        out_shape=jax.ShapeDtypeStruct((S, GLA_VAL), BF16),
        compiler_params=_cp(("parallel",)),
    )(of, ob, proj, gn)


def _gla_out_bwd(dz, of, ob, proj, gn, name):
    S = of.shape[0]
    tm = _pick(S, (256, 128))
    gblk = (2 * GLA_KEY + GLA_VAL) // GLA_VAL

    def body(dz_ref, of_ref, ob_ref, g_ref, gn_ref, do_ref, dg_ref, dgn_ref):
        i = pl.program_id(0)
        gnv = gn_ref[...]
        part = jnp.zeros((1, GLA_DV), F32)
        for h in range(GLA_HEADS):
            cols = pl.ds(h * GLA_DV, GLA_DV)
            o = of_ref[:, cols] + ob_ref[:, cols]
            r = lax.rsqrt(jnp.mean(o * o, axis=-1, keepdims=True) + NORM_EPS)
            y = o * r
            gv = g_ref[:, cols]
            sg = _sigmoid(gv)
            dzv = dz_ref[:, cols]
            dg_ref[:, cols] = dzv * (y * gnv) * (sg * (1.0 + gv * (1.0 - sg)))
            don = dzv * (gv * sg)
            part = part + jnp.sum(don * y, axis=0, keepdims=True)
            dy = don * gnv
            do_ref[:, cols] = r * (dy - y * jnp.mean(dy * y, axis=-1, keepdims=True))

        @pl.when(i == 0)
        def _():
            dgn_ref[...] = part

        @pl.when(i > 0)
        def _():
            dgn_ref[...] += part

    row = pl.BlockSpec((tm, GLA_VAL), lambda i: (i, 0))
    one = pl.BlockSpec((1, GLA_DV), lambda i: (0, 0))
    return pl.pallas_call(
        body, name=name, grid=(S // tm,),
        in_specs=[row, row, row, pl.BlockSpec((tm, GLA_VAL), lambda i: (i, gblk)), one],
        out_specs=[row, row, one],
        out_shape=[jax.ShapeDtypeStruct((S, GLA_VAL), F32), jax.ShapeDtypeStruct((S, GLA_VAL), F32),
                   jax.ShapeDtypeStruct((1, GLA_DV), F32)],
        compiler_params=_cp(("arbitrary",)),
    )(dz, of, ob, proj, gn)


N_QK_HEADS = ATT_QH + ATT_KVH


def _qk_prep_fwd(proj, qn, kn, rc, rs, name):
    S = proj.shape[0]
    tm = _pick(S, (256, 128))
    W = N_QK_HEADS * ATT_HD
    scale = ATT_HD ** -0.5

    def body(p_ref, qn_ref, kn_ref, rc_ref, rs_ref, v_in_ref, qk_ref, v_ref, kt_ref, vt_ref):
        c, s = rc_ref[...], rs_ref[...]
        for h in range(N_QK_HEADS):
            cols = pl.ds(h * ATT_HD, ATT_HD)
            w = qn_ref[...] if h < ATT_QH else kn_ref[...]
            xv = p_ref[:, cols]
            r = lax.rsqrt(jnp.mean(xv * xv, axis=-1, keepdims=True) + NORM_EPS)
            y = xv * r * w
            out = y * c + pltpu.roll(y, ATT_HD // 2, 1) * s
            if h < ATT_QH:
                qk_ref[:, cols] = (out * scale).astype(BF16)
            else:
                qk_ref[:, cols] = out.astype(BF16)
                kt_ref[pl.ds((h - ATT_QH) * ATT_HD, ATT_HD), :] = out.T.astype(BF16)
        v_ref[...] = v_in_ref[...].astype(BF16)
        for h in range(ATT_KVH):
            vt_ref[pl.ds(h * ATT_HD, ATT_HD), :] = v_in_ref[:, pl.ds(h * ATT_HD, ATT_HD)].T.astype(BF16)

    one = pl.BlockSpec((1, ATT_HD), lambda i: (0, 0))
    tab = pl.BlockSpec((tm, ATT_HD), lambda i: (i, 0))
    vw = ATT_KVH * ATT_HD
    tr = pl.BlockSpec((vw, tm), lambda i: (0, i))
    return pl.pallas_call(
        body, name=name, grid=(S // tm,),
        in_specs=[pl.BlockSpec((tm, W), lambda i: (i, 0)), one, one, tab, tab,
                  pl.BlockSpec((tm, vw), lambda i: (i, W // vw))],
        out_specs=[pl.BlockSpec((tm, W), lambda i: (i, 0)), pl.BlockSpec((tm, vw), lambda i: (i, 0)), tr, tr],
        out_shape=[jax.ShapeDtypeStruct((S, W), BF16), jax.ShapeDtypeStruct((S, vw), BF16),
                   jax.ShapeDtypeStruct((vw, S), BF16), jax.ShapeDtypeStruct((vw, S), BF16)],
        compiler_params=_cp(("parallel",)),
    )(proj, qn, kn, rc, rs, proj)


def _qk_prep_bwd(dqk, proj, qn, kn, rc, rs, name):
    S = proj.shape[0]
    tm = _pick(S, (256, 128))
    W = N_QK_HEADS * ATT_HD

    def body(d_ref, p_ref, qn_ref, kn_ref, rc_ref, rs_ref, dp_ref, dqn_ref, dkn_ref):
        i = pl.program_id(0)
        c, s = rc_ref[...], rs_ref[...]
        parts = [jnp.zeros((1, ATT_HD), F32), jnp.zeros((1, ATT_HD), F32)]
        for h in range(N_QK_HEADS):
            cols = pl.ds(h * ATT_HD, ATT_HD)
            w = qn_ref[...] if h < ATT_QH else kn_ref[...]
            dout = d_ref[:, cols]
            dy = dout * c + pltpu.roll(dout * s, ATT_HD // 2, 1)
            xv = p_ref[:, cols]
            r = lax.rsqrt(jnp.mean(xv * xv, axis=-1, keepdims=True) + NORM_EPS)
            xr = xv * r
            which = 0 if h < ATT_QH else 1
            parts[which] = parts[which] + jnp.sum(dy * xr, axis=0, keepdims=True)
            dxr = dy * w
            dp_ref[:, cols] = r * (dxr - xr * jnp.mean(dxr * xr, axis=-1, keepdims=True))

        @pl.when(i == 0)
        def _():
            dqn_ref[...] = parts[0]
            dkn_ref[...] = parts[1]

        @pl.when(i > 0)
        def _():
            dqn_ref[...] += parts[0]
            dkn_ref[...] += parts[1]

    one = pl.BlockSpec((1, ATT_HD), lambda i: (0, 0))
    tab = pl.BlockSpec((tm, ATT_HD), lambda i: (i, 0))
    row = pl.BlockSpec((tm, W), lambda i: (i, 0))
    return pl.pallas_call(
        body, name=name, grid=(S // tm,),
        in_specs=[row, row, one, one, tab, tab],
        out_specs=[row, one, one],
        out_shape=[jax.ShapeDtypeStruct((S, W), F32), jax.ShapeDtypeStruct((1, ATT_HD), F32),
                   jax.ShapeDtypeStruct((1, ATT_HD), F32)],
        compiler_params=_cp(("arbitrary",)),
    )(dqk, proj, qn, kn, rc, rs)


ATT_TQ = 256
LSE_ROWS = 8


def _attn_fwd(qk, vt, name):
    S = qk.shape[0]
    tq = min(ATT_TQ, S)

    def body(q_ref, k_ref, vt_ref, o_ref, lse_ref):
        st = _dg(k_ref[...], q_ref[...], 1, 1)
        m = jnp.max(st, axis=0, keepdims=True)
        pt = jnp.exp(st - m)
        l = jnp.sum(pt, axis=0, keepdims=True)
        ot = jnp.dot(vt_ref[...], pt.astype(BF16), preferred_element_type=F32)
        o_ref[...] = (ot * (1.0 / l)).T
        lse_ref[...] = jnp.broadcast_to(m + jnp.log(l), (LSE_ROWS, tq))

    qo = pl.BlockSpec((tq, ATT_HD), lambda h, i: (i, h))
    return pl.pallas_call(
        body, name=name, grid=(ATT_QH, S // tq),
        in_specs=[qo, pl.BlockSpec((S, ATT_HD), lambda h, i: (0, ATT_QH + h // ATT_GROUP)),
                  pl.BlockSpec((ATT_HD, S), lambda h, i: (h // ATT_GROUP, 0))],
        out_specs=[qo, pl.BlockSpec((LSE_ROWS, tq), lambda h, i: (h, i))],
        out_shape=[jax.ShapeDtypeStruct((S, ATT_QH * ATT_HD), F32),
                   jax.ShapeDtypeStruct((ATT_QH * LSE_ROWS, S), F32)],
        compiler_params=_cp(("parallel", "parallel")),
    )(qk, qk, vt)


def _attn_bwd(do, o, lse, qk, v, kt, name):
    S = qk.shape[0]
    tq = min(ATT_TQ, S)
    scale = ATT_HD ** -0.5

    def body(do_ref, o_ref, lse_ref, q_ref, k_ref, v_ref, kt_ref, dq_ref, dk_ref, dv_ref):
        g = pl.program_id(1)
        i = pl.program_id(2)

        @pl.when((g == 0) & (i == 0))
        def _():
            dk_ref[...] = jnp.zeros_like(dk_ref)
            dv_ref[...] = jnp.zeros_like(dv_ref)

        q = q_ref[...]
        dov = do_ref[...]
        dob = dov.astype(BF16)
        delta = jnp.sum((dov * o_ref[...]).T, axis=0, keepdims=True)
        st = _dg(k_ref[...], q, 1, 1)
        pt = jnp.exp(st - lse_ref[0:1, :])
        dpt = _dg(v_ref[...], dob, 1, 1)
        dst = (pt * (dpt - delta)).astype(BF16)
        dv_ref[...] += jnp.dot(pt.astype(BF16), dob, preferred_element_type=F32)
        dk_ref[...] += jnp.dot(dst, q, preferred_element_type=F32)
        dq_ref[...] = jnp.dot(kt_ref[...], dst, preferred_element_type=F32).T * scale

    qo = pl.BlockSpec((tq, ATT_HD), lambda kv, g, i: (i, kv * ATT_GROUP + g))
    kvo = pl.BlockSpec((S, ATT_HD), lambda kv, g, i: (0, kv))
    return pl.pallas_call(
        body, name=name, grid=(ATT_KVH, ATT_GROUP, S // tq),
        in_specs=[qo, qo, pl.BlockSpec((LSE_ROWS, tq), lambda kv, g, i: (kv * ATT_GROUP + g, i)), qo,
                  pl.BlockSpec((S, ATT_HD), lambda kv, g, i: (0, ATT_QH + kv)), kvo,
                  pl.BlockSpec((ATT_HD, S), lambda kv, g, i: (kv, 0))],
        out_specs=[qo, kvo, kvo],
        out_shape=[jax.ShapeDtypeStruct((S, ATT_QH * ATT_HD), F32),
                   jax.ShapeDtypeStruct((S, ATT_KVH * ATT_HD), F32),
                   jax.ShapeDtypeStruct((S, ATT_KVH * ATT_HD), F32)],
        compiler_params=_cp(("parallel", "arbitrary", "arbitrary")),
    )(do, o, lse, qk, qk, v, kt)


def _adamw(w, g, m, v, name):
    rows, cols = w.shape
    tr = rows
    for cand in (512, 256, 128, 64, 32, 16, 8):
        if rows % cand == 0 and cand * cols * 4 <= 2 * 1024 * 1024:
            tr = cand
            break

    def body(w_ref, g_ref, m_ref, v_ref, d_ref, nm_ref, nv_ref):
        gv = g_ref[...]
        nm = ADAM_B1 * m_ref[...] + (1.0 - ADAM_B1) * gv
        nv = ADAM_B2 * v_ref[...] + (1.0 - ADAM_B2) * (gv * gv)
        m_hat = nm / (1.0 - ADAM_B1 ** ADAM_STEP)
        v_hat = nv / (1.0 - ADAM_B2 ** ADAM_STEP)
        d_ref[...] = -ADAM_LR * (m_hat / (jnp.sqrt(v_hat) + ADAM_EPS) + ADAM_WD * w_ref[...])
        nm_ref[...] = nm
        nv_ref[...] = nv

    blk = pl.BlockSpec((tr, cols), lambda i: (i, 0))
    return pl.pallas_call(
        body, name=name, grid=(rows // tr,),
        in_specs=[blk] * 4, out_specs=[blk] * 3,
        out_shape=[jax.ShapeDtypeStruct((rows, cols), F32)] * 3,
        compiler_params=_cp(("parallel",)),
    )(w, g, m, v)


ANY = pl.BlockSpec(memory_space=pl.ANY)


def _place():
    return lax.axis_index("x"), lax.axis_index("y"), lax.axis_index("c")


def _other_chips(x, y):
    return [(1 - x, y), (x, 1 - y), (1 - x, 1 - y)]


def _half_rows(c, H):
    return pl.ds(pl.multiple_of(c * H, 8), H)


def _allgather_chips(bufs, name):
    n = len(bufs)
    halves = [b.shape[1] // 2 for b in bufs]

    def body(*refs):
        outs = refs[n:2 * n]
        send_sems, recv_sems = refs[2 * n:]
        x, y, c = _place()
        me = 2 * x + y
        sibling = (x, y, 1 - c)
        chips = _other_chips(x, y)

        def copy(k, sem, p, core, to):
            blk = outs[k].at[p, _half_rows(core, halves[k])]
            return pltpu.make_async_remote_copy(src_ref=blk, dst_ref=blk, send_sem=send_sems.at[6 * k + sem],
                                                recv_sem=recv_sems.at[6 * k + sem], device_id=to,
                                                device_id_type=MESH)

        first = [copy(k, j, me, c, (px, py, c)) for k in range(n) for j, (px, py) in enumerate(chips)]
        for cp in first:
            cp.start()
        passed = []
        for k in range(n):
            for j, (px, py) in enumerate(chips):
                copy(k, j, 2 * px + py, c, (px, py, c)).wait_recv()
                fwd = copy(k, 3 + j, 2 * px + py, c, sibling)
                fwd.start()
                passed.append(fwd)
        for k in range(n):
            for j, (px, py) in enumerate(chips):
                copy(k, 3 + j, 2 * px + py, 1 - c, sibling).wait_recv()
        for cp in first + passed:
            cp.wait_send()

    return pl.pallas_call(
        body, name=name,
        in_specs=[ANY] * n, out_specs=[ANY] * n,
        out_shape=[jax.ShapeDtypeStruct(b.shape, b.dtype) for b in bufs],
        input_output_aliases={k: k for k in range(n)},
        scratch_shapes=[pltpu.SemaphoreType.DMA((6 * n,)), pltpu.SemaphoreType.DMA((6 * n,))],
    )(*bufs)


def _allreduce_small(v, name):
    R = v.shape[0]
    n_dev = 8

    def body(v_ref, sum_ref, all_ref, send_sems, recv_sems, local_sem):
        x, y, c = _place()
        me, sibling = (x, y, c), (x, y, 1 - c)
        chips = _other_chips(x, y)

        def rows(px, py, pc):
            return all_ref.at[pl.ds(pl.multiple_of((4 * px + 2 * py + pc) * R, 8), R), :]

        def copy(k, block, to, src=None):
            return pltpu.make_async_remote_copy(
                src_ref=rows(*block) if src is None else src, dst_ref=rows(*block),
                send_sem=send_sems.at[k], recv_sem=recv_sems.at[k], device_id=to, device_id_type=MESH)

        own = pltpu.make_async_copy(v_ref, rows(*me), local_sem)
        own.start()
        first = [copy(0, me, sibling, src=v_ref)]
        first += [copy(1 + j, me, (*chip, c), src=v_ref) for j, chip in enumerate(chips)]
        for cp in first:
            cp.start()
        passed = [copy(4 + j, (*chip, c), sibling) for j, chip in enumerate(chips)]
        for j, chip in enumerate(chips):
            copy(1 + j, (*chip, c), me).wait_recv()
            passed[j].start()
        copy(0, sibling, me).wait_recv()
        for j, chip in enumerate(chips):
            copy(4 + j, (*chip, 1 - c), me).wait_recv()
        for cp in first + passed:
            cp.wait_send()
        own.wait()
        acc = all_ref[pl.ds(0, R), :]
        for d in range(1, n_dev):
            acc = acc + all_ref[pl.ds(d * R, R), :]
        sum_ref[...] = acc

    vm = pl.BlockSpec(memory_space=pltpu.VMEM)
    return pl.pallas_call(
        body, name=name,
        in_specs=[vm], out_specs=[vm, vm],
        out_shape=[jax.ShapeDtypeStruct((R, LANES), F32), jax.ShapeDtypeStruct((n_dev * R, LANES), F32)],
        scratch_shapes=[pltpu.SemaphoreType.DMA((7,)), pltpu.SemaphoreType.DMA((7,)), pltpu.SemaphoreType.DMA],
    )(v)[0]


def _swap_other_half(bufs, name):
    n = len(bufs)
    halves = [b.shape[1] // 2 for b in bufs]

    def body(*refs):
        g_refs, got_refs = refs[:n], refs[n:2 * n]
        send_sems, recv_sems = refs[2 * n:]
        x, y, c = _place()
        copies = [pltpu.make_async_remote_copy(
            src_ref=g_refs[k].at[p, _half_rows(1 - c, halves[k])], dst_ref=got_refs[k].at[p],
            send_sem=send_sems.at[N_CHIPS * k + p], recv_sem=recv_sems.at[N_CHIPS * k + p],
            device_id=(x, y, 1 - c), device_id_type=MESH) for k in range(n) for p in range(N_CHIPS)]
        for cp in copies:
            cp.start()
        for cp in copies:
            cp.wait_recv()
        for cp in copies:
            cp.wait_send()

    return pl.pallas_call(
        body, name=name, in_specs=[ANY] * n, out_specs=[ANY] * n,
        out_shape=[jax.ShapeDtypeStruct((N_CHIPS, h, b.shape[2]), b.dtype) for b, h in zip(bufs, halves)],
        scratch_shapes=[pltpu.SemaphoreType.DMA((N_CHIPS * n,)), pltpu.SemaphoreType.DMA((N_CHIPS * n,))],
    )(*bufs)


def _send_chip_partials(sbs, name):
    n = len(sbs)

    def body(*refs):
        s_refs, got_refs = refs[:n], refs[n:2 * n]
        send_sems, recv_sems = refs[2 * n:]
        x, y, c = _place()
        me = 2 * x + y
        chips = _other_chips(x, y)

        def copy(k, j, src_slot, dst_slot):
            px, py = chips[j]
            return pltpu.make_async_remote_copy(
                src_ref=s_refs[k].at[src_slot], dst_ref=got_refs[k].at[dst_slot], send_sem=send_sems.at[3 * k + j],
                recv_sem=recv_sems.at[3 * k + j], device_id=(px, py, c), device_id_type=MESH)

        copies = [copy(k, j, 2 * px + py, me) for k in range(n) for j, (px, py) in enumerate(chips)]
        for cp in copies:
            cp.start()
        for k in range(n):
            for j, (px, py) in enumerate(chips):
                copy(k, j, me, 2 * px + py).wait_recv()
        for cp in copies:
            cp.wait_send()

    return pl.pallas_call(
        body, name=name, in_specs=[ANY] * n, out_specs=[ANY] * n,
        out_shape=[jax.ShapeDtypeStruct(s.shape, s.dtype) for s in sbs],
        scratch_shapes=[pltpu.SemaphoreType.DMA((3 * n,)), pltpu.SemaphoreType.DMA((3 * n,))],
    )(*sbs)


def _join_halves(bufs, name):
    n = len(bufs)
    halves = [b.shape[0] // 2 for b in bufs]

    def body(*refs):
        outs = refs[n:2 * n]
        send_sems, recv_sems = refs[2 * n:]
        x, y, c = _place()

        def copy(k, core):
            blk = outs[k].at[_half_rows(core, halves[k])]
            return pltpu.make_async_remote_copy(src_ref=blk, dst_ref=blk, send_sem=send_sems.at[k],
                                                recv_sem=recv_sems.at[k], device_id=(x, y, 1 - c),
                                                device_id_type=MESH)

        sends = [copy(k, c) for k in range(n)]
        for cp in sends:
            cp.start()
        for k in range(n):
            copy(k, 1 - c).wait_recv()
        for cp in sends:
            cp.wait_send()

    return pl.pallas_call(
        body, name=name, in_specs=[ANY] * n, out_specs=[ANY] * n,
        out_shape=[jax.ShapeDtypeStruct(b.shape, b.dtype) for b in bufs],
        input_output_aliases={k: k for k in range(n)},
        scratch_shapes=[pltpu.SemaphoreType.DMA((n,)), pltpu.SemaphoreType.DMA((n,))],
    )(*bufs)


def _rs_rows(H, width):
    for cand in (1024, 512, 256, 128, 64, 32, 16):
        if H % cand == 0 and cand * width * 4 <= 1536 * 1024:
            return cand
    return H


def _add_sibling(g, got, c, me, name):
    _, H, width = got.shape
    tb = _rs_rows(H, width)
    nb = H // tb

    def body(sp_ref, g_ref, got_ref, sb_ref, sf_ref):
        p = pl.program_id(1)
        s = g_ref[0] + got_ref[0]
        sb_ref[0] = s.astype(BF16)

        @pl.when(p == sp_ref[1])
        def _():
            sf_ref[...] = s

    grid_spec = pltpu.PrefetchScalarGridSpec(
        num_scalar_prefetch=1, grid=(nb, N_CHIPS),
        in_specs=[pl.BlockSpec((1, tb, width), lambda i, p, sp: (p, sp[0] * nb + i, 0)),
                  pl.BlockSpec((1, tb, width), lambda i, p, sp: (p, i, 0))],
        out_specs=[pl.BlockSpec((1, tb, width), lambda i, p, sp: (p, i, 0)),
                   pl.BlockSpec((tb, width), lambda i, p, sp: (i, 0))])
    return pl.pallas_call(
        body, name=name, grid_spec=grid_spec,
        out_shape=[jax.ShapeDtypeStruct((N_CHIPS, H, width), BF16), jax.ShapeDtypeStruct((H, width), F32)],
        compiler_params=_cp(("arbitrary", "arbitrary")),
    )(jnp.stack([c, me]).astype(jnp.int32), g, got)


def _add_chips(sf, got, others_and_c, name):
    H, width = sf.shape
    tb = _rs_rows(H, width)
    nb = H // tb

    def body(sp_ref, sf_ref, r1_ref, r2_ref, r3_ref, out_ref):
        out_ref[...] = ((sf_ref[...] + r1_ref[0].astype(F32)) + r2_ref[0].astype(F32)) + r3_ref[0].astype(F32)

    def slot(k):
        return pl.BlockSpec((1, tb, width), lambda i, sp: (sp[k], i, 0))

    blk = pl.BlockSpec((tb, width), lambda i, sp: (i, 0))
    grid_spec = pltpu.PrefetchScalarGridSpec(
        num_scalar_prefetch=1, grid=(nb,), in_specs=[blk, slot(0), slot(1), slot(2)],
        out_specs=pl.BlockSpec((tb, width), lambda i, sp: (sp[3] * nb + i, 0)))
    return pl.pallas_call(
        body, name=name, grid_spec=grid_spec,
        out_shape=jax.ShapeDtypeStruct((2 * H, width), F32),
        compiler_params=_cp(("arbitrary",)),
    )(others_and_c.astype(jnp.int32), sf, got, got, got)


def _reduce_scatter(bufs, c, me, tag):
    n = len(bufs)
    gots = _swap_other_half(bufs, f"{tag}_to_sibling")
    sums = [_add_sibling(bufs[k], gots[k], c, me, f"{tag}_add_sibling{k}") for k in range(n)]
    landed = _send_chip_partials([s[0] for s in sums], f"{tag}_to_chips")
    others_and_c = jnp.stack([jnp.where(me <= k, k + 1, k) for k in range(N_CHIPS - 1)] + [c])
    halves = [_add_chips(sums[k][1], landed[k], others_and_c, f"{tag}_add_chips{k}") for k in range(n)]
    return _join_halves(halves, f"{tag}_join_halves")


REPLICATED = ("norm_mix", "norm_ffn", "gla_b_gate_f", "gla_b_gate_b", "gla_norm", "attn_q_norm", "attn_k_norm",
              "ffn_b_conv")


PIECE_ROWS = 16


def _piece_rows(shape):
    n = 1
    for s in shape:
        n *= s
    rows = n // LANES
    return rows, -(-rows // PIECE_ROWS) * PIECE_ROWS


def _pack(pieces, dtype, row_multiple):
    flat = []
    for p in pieces:
        rows, padded = _piece_rows(p.shape)
        flat.append(jnp.pad(p.astype(dtype).reshape(rows, LANES), ((0, padded - rows), (0, 0))))
    rows = sum(f.shape[0] for f in flat)
    padded = -(-rows // row_multiple) * row_multiple
    if padded > rows:
        flat.append(jnp.zeros((padded - rows, LANES), dtype))
    return jnp.concatenate(flat, axis=0)


def _unpack(buf, shapes):
    out, r = [], 0
    for shp in shapes:
        rows, padded = _piece_rows(shp)
        out.append(buf[r:r + rows].reshape(shp))
        r += padded
    return out


BIG = ("ffn_w_up", "ffn_w_down", "attn_w_qkv", "gla_w_out", "attn_w_out", "gla_w_in")
SMALL_SHARDED = ("gla_w_gate_up_f", "gla_w_gate_up_b", "ffn_w_conv")


def _own_slot(shard2d, me):
    return lax.dynamic_update_index_in_dim(jnp.zeros((N_CHIPS,) + shard2d.shape, shard2d.dtype), shard2d, me, 0)


def _gather_weights(w, me, name):
    two_d = lambda t: t.reshape(-1, t.shape[-1])
    bufs = [_own_slot(two_d(w[n]).astype(BF16), me) for n in BIG]
    bufs.append(_own_slot(_pack([w[n] for n in SMALL_SHARDED], F32, 32), me))
    got = _allgather_chips(bufs, name)
    G = dict(zip(BIG, got[:-1]))
    layers = lambda t, L: t.reshape(N_CHIPS, L, t.shape[1] // L, t.shape[2])
    rows_major = lambda t, L: [layers(t, L)[:, l].reshape(-1, t.shape[2]) for l in range(L)]
    small = [_unpack(got[-1][p], [w[n].shape for n in SMALL_SHARDED]) for p in range(N_CHIPS)]
    full_small = [jnp.concatenate([small[p][k] for p in range(N_CHIPS)], axis=2) for k in range(len(SMALL_SHARDED))]
    win = layers(G["gla_w_in"], DEPTH // 2)
    wup = layers(G["ffn_w_up"], DEPTH)
    return dict(
        up=G["ffn_w_up"], qkv=G["attn_w_qkv"],
        up_full=[jnp.concatenate([wup[p, l] for p in range(N_CHIPS)], axis=1) for l in range(DEPTH)],
        down=rows_major(G["ffn_w_down"], DEPTH),
        gla_out=rows_major(G["gla_w_out"], DEPTH // 2),
        attn_out=rows_major(G["attn_w_out"], DEPTH // 2),
        gla_in=[jnp.pad(jnp.concatenate([win[p, l] for p in range(N_CHIPS)], axis=1),
                        ((0, 0), (0, GLA_IN_PAD - GLA_IN))) for l in range(DEPTH // 2)],
        gate_f=full_small[0], gate_b=full_small[1], conv=full_small[2])


def _rope_tables(S):
    rows = S // GRID_W
    row_idx = jnp.repeat(jnp.arange(rows, dtype=F32), GRID_W)
    col_idx = jnp.tile(jnp.arange(GRID_W, dtype=F32), rows)
    pairs = ATT_HD // 4
    inv_freq = ROPE_THETA ** (-jnp.arange(pairs, dtype=F32) / pairs)
    ang = jnp.concatenate([row_idx[:, None] * inv_freq, col_idx[:, None] * inv_freq], axis=-1)
    cos, sin = jnp.cos(ang), jnp.sin(ang)
    return jnp.concatenate([cos, cos], axis=-1), jnp.concatenate([-sin, sin], axis=-1)


def _gate_rows(w, first_row):
    return jnp.zeros((LANES, GLA_KEY), F32).at[first_row:first_row + GLA_RANK].set(w.astype(F32))


def _local_step(x, target, W, P):
    S = x.shape[0]
    rc, rs = _rope_tables(S)
    row = lambda a: a.reshape(1, -1)
    saved = []
    for i in range(DEPTH):
        j = i // 2
        nm = row(P["norm_mix"][i])
        h1 = _rmsnorm_fwd(x, nm, f"norm_mix_fwd{i}")
        if i % 2 == 0:
            wgf = _gate_rows(W["gate_f"][j], 0)
            wgb = _gate_rows(W["gate_b"][j], GLA_RANK)
            bgf, bgb = row(P["gla_b_gate_f"][j]), row(P["gla_b_gate_b"][j])
            gn = row(P["gla_norm"][j])
            proj = _matmul_rows(h1, W["gla_in"][j], f"gla_in{i}")
            laf, lab = _gla_gate_fwd(proj, wgf, bgf, wgb, bgb, f"gla_gate_fwd{i}")
            of, stf = _gla_scan_fwd(proj, laf, False, f"gla_scan_f_fwd{i}")
            ob, stb = _gla_scan_fwd(proj, lab, True, f"gla_scan_b_fwd{i}")
            z = _gla_out_fwd(of, ob, proj, gn, f"gla_out_fwd{i}")
            xm = _matmul_rows(z, W["gla_out"][j], f"gla_outproj{i}", res=x)
            mix = dict(proj=proj, laf=laf, lab=lab, of=of, ob=ob, stf=stf, stb=stb, z=z, wgf=wgf, wgb=wgb)
        else:
            proj = _matmul_rows(h1, W["qkv"], f"attn_qkv{i}", w_layer=j)
            qn, kn = row(P["attn_q_norm"][j]), row(P["attn_k_norm"][j])
            qk, vb, kt, vt = _qk_prep_fwd(proj, qn, kn, rc, rs, f"qk_prep_fwd{i}")
            o, lse = _attn_fwd(qk, vt, f"attn_fwd{i}")
            xm = _matmul_rows(o, W["attn_out"][j], f"attn_outproj{i}", res=x)
            mix = dict(proj=proj, qk=qk, vb=vb, kt=kt, o=o, lse=lse)
        h2 = _rmsnorm_fwd(xm, row(P["norm_ffn"][i]), f"norm_ffn_fwd{i}")
        a, uv, ug = _ffn_mid_fwd(h2, W["up_full"][i], W["conv"][i], row(P["ffn_b_conv"][i]), f"ffn_mid_fwd{i}")
        xo = _matmul_rows(a, W["down"][i], f"ffn_down{i}", res=xm)
        saved.append(dict(x=x, h1=h1, xm=xm, h2=h2, uv=uv, ug=ug, mix=mix))
        x = xo

    dx, dxb, loss = _loss_grad(x, target, "loss")

    G = {n: [None] * (DEPTH if n.startswith(("norm", "ffn")) else DEPTH // 2)
         for n in ("gla_w_in",) + SMALL_SHARDED + REPLICATED}
    G.update(up=None, down=None, out=None, qkv=None)
    for i in reversed(range(DEPTH)):
        j = i // 2
        sv = saved[i]
        mix = sv["mix"]
        duv, dug, a, gwv, gwg = _ffn_mid_bwd(dxb, W["down"][i], sv["uv"], sv["ug"], W["conv"][i],
                                             row(P["ffn_b_conv"][i]), f"ffn_mid_bwd{i}")
        G["down"] = _matmul(a, dxb, 0, 0, f"ffn_down_wgrad{i}", out_chips=("rows", i, DEPTH), into=G["down"])
        G["up"] = _matmul(sv["h2"], (duv, dug), 0, 0, f"ffn_up_wgrad{i}", out_chips=("cols", i, DEPTH),
                          into=G["up"])
        G["ffn_w_conv"][i] = jnp.concatenate([gwv[:3], gwg[:3]], axis=1)
        G["ffn_b_conv"][i] = jnp.concatenate([gwv[3], gwg[3]], axis=0)
        dxm, dxmb, dn = _dgrad_norm((duv, dug), W["up"], sv["xm"], row(P["norm_ffn"][i]), dx, f"ffn_up_dgrad{i}",
                                    w_layer=i)
        G["norm_ffn"][i] = dn[0]
        if i % 2 == 0:
            proj = mix["proj"]
            bgf, bgb = row(P["gla_b_gate_f"][j]), row(P["gla_b_gate_b"][j])
            gn = row(P["gla_norm"][j])
            dz = _matmul_rows(dxmb, W["gla_out"][j], f"gla_outproj_dgrad{i}", transposed=True)
            G["out"] = _matmul(mix["z"], dxmb, 0, 0, f"gla_outproj_wgrad{i}", out_chips=("rows", i, DEPTH),
                               into=G["out"])
            do, dg, dgn = _gla_out_bwd(dz, mix["of"], mix["ob"], proj, gn, f"gla_out_bwd{i}")
            G["gla_norm"][j] = dgn[0]
            dqf, dkf, dvf, dlaf = _gla_scan_bwd(do, proj, mix["laf"], mix["stf"], False, f"gla_scan_f_bwd{i}")
            dqb, dkb, dvb, dlab = _gla_scan_bwd(do, proj, mix["lab"], mix["stb"], True, f"gla_scan_b_bwd{i}")
            dr, dwf, dbf, dwb, dbb = _gla_gate_bwd(dlaf, dlab, proj, mix["wgf"], bgf, mix["wgb"], bgb,
                                                   f"gla_gate_bwd{i}")
            G["gla_w_gate_up_f"][j] = dwf[:GLA_RANK]
            G["gla_w_gate_up_b"][j] = dwb[GLA_RANK:2 * GLA_RANK]
            G["gla_b_gate_f"][j] = dbf[0]
            G["gla_b_gate_b"][j] = dbb[0]
            dproj = jnp.concatenate([dqf + dqb, dkf + dkb, dvf + dvb, dg, dr], axis=1).astype(BF16)
            G["gla_w_in"][j] = _matmul(sv["h1"], dproj, 0, 0, f"gla_in_wgrad{i}")
            dx, dxb, dn = _dgrad_norm(dproj, W["gla_in"][j], sv["x"], row(P["norm_mix"][i]), dxm, f"mix_in_dgrad{i}")
        else:
            proj = mix["proj"]
            qn, kn = row(P["attn_q_norm"][j]), row(P["attn_k_norm"][j])
            do = _matmul_rows(dxmb, W["attn_out"][j], f"attn_outproj_dgrad{i}", transposed=True)
            G["out"] = _matmul(mix["o"], dxmb, 0, 0, f"attn_outproj_wgrad{i}", out_chips=("rows", i, DEPTH),
                               into=G["out"])
            dq, dk, dv = _attn_bwd(do, mix["o"], mix["lse"], mix["qk"], mix["vb"], mix["kt"], f"attn_bwd{i}")
            dqk = jnp.concatenate([dq, dk], axis=1)
            dpqk, dqn, dkn = _qk_prep_bwd(dqk, proj, qn, kn, rc, rs, f"qk_prep_bwd{i}")
            G["attn_q_norm"][j] = dqn[0]
            G["attn_k_norm"][j] = dkn[0]
            dproj = jnp.concatenate([dpqk, dv], axis=1).astype(BF16)
            G["qkv"] = _matmul(sv["h1"], dproj, 0, 0, f"attn_qkv_wgrad{i}", out_chips=("cols", j, DEPTH // 2),
                               into=G["qkv"])
            dx, dxb, dn = _dgrad_norm(dproj, W["qkv"], sv["x"], row(P["norm_mix"][i]), dxm, f"mix_in_dgrad{i}",
                                      w_layer=j)
        G["norm_mix"][i] = dn[0]
    return loss, dx, G


def kernel(x, norm_mix, norm_ffn, gla_w_in, gla_w_gate_up_f, gla_b_gate_f, gla_w_gate_up_b, gla_b_gate_b, gla_norm, gla_w_out, attn_w_qkv, attn_q_norm, attn_k_norm, attn_w_out, ffn_w_up, ffn_w_conv, ffn_b_conv, ffn_w_down, loss_target, m_norm_mix, m_norm_ffn, m_gla_w_in, m_gla_w_gate_up_f, m_gla_b_gate_f, m_gla_w_gate_up_b, m_gla_b_gate_b, m_gla_norm, m_gla_w_out, m_attn_w_qkv, m_attn_q_norm, m_attn_k_norm, m_attn_w_out, m_ffn_w_up, m_ffn_w_conv, m_ffn_b_conv, m_ffn_w_down, v_norm_mix, v_norm_ffn, v_gla_w_in, v_gla_w_gate_up_f, v_gla_b_gate_f, v_gla_w_gate_up_b, v_gla_b_gate_b, v_gla_norm, v_gla_w_out, v_attn_w_qkv, v_attn_q_norm, v_attn_k_norm, v_attn_w_out, v_ffn_w_up, v_ffn_w_conv, v_ffn_b_conv, v_ffn_w_down):
    names = ("norm_mix", "norm_ffn", "gla_w_in", "gla_w_gate_up_f", "gla_b_gate_f", "gla_w_gate_up_b",
             "gla_b_gate_b", "gla_norm", "gla_w_out", "attn_w_qkv", "attn_q_norm", "attn_k_norm", "attn_w_out",
             "ffn_w_up", "ffn_w_conv", "ffn_b_conv", "ffn_w_down")
    w = dict(zip(names, (norm_mix, norm_ffn, gla_w_in, gla_w_gate_up_f, gla_b_gate_f, gla_w_gate_up_b,
                         gla_b_gate_b, gla_norm, gla_w_out, attn_w_qkv, attn_q_norm, attn_k_norm, attn_w_out,
                         ffn_w_up, ffn_w_conv, ffn_b_conv, ffn_w_down)))
    m = dict(zip(names, (m_norm_mix, m_norm_ffn, m_gla_w_in, m_gla_w_gate_up_f, m_gla_b_gate_f,
                         m_gla_w_gate_up_b, m_gla_b_gate_b, m_gla_norm, m_gla_w_out, m_attn_w_qkv, m_attn_q_norm,
                         m_attn_k_norm, m_attn_w_out, m_ffn_w_up, m_ffn_w_conv, m_ffn_b_conv, m_ffn_w_down)))
    v = dict(zip(names, (v_norm_mix, v_norm_ffn, v_gla_w_in, v_gla_w_gate_up_f, v_gla_b_gate_f,
                         v_gla_w_gate_up_b, v_gla_b_gate_b, v_gla_norm, v_gla_w_out, v_attn_w_qkv, v_attn_q_norm,
                         v_attn_k_norm, v_attn_w_out, v_ffn_w_up, v_ffn_w_conv, v_ffn_b_conv, v_ffn_w_down)))
    px, py, pc = _place()
    me = 2 * px + py

    W = _gather_weights(w, me, "gather_weights")
    P = {n: w[n] for n in REPLICATED}

    loss_part, dx, grads = _local_step(x[0], loss_target[0], W, P)

    win_width = w["gla_w_in"].shape[2]
    win = jnp.stack([jnp.concatenate([g[:, p * win_width:(p + 1) * win_width] for g in grads["gla_w_in"]], axis=0)
                     for p in range(N_CHIPS)])
    small_g = []
    for p in range(N_CHIPS):
        cut = [lax.slice_in_dim(jnp.stack(grads[n]), p * w[n].shape[2], (p + 1) * w[n].shape[2], axis=2)
               for n in SMALL_SHARDED]
        small_g.append(_pack(cut, F32, 256))
    bufs = [grads["up"], grads["down"], grads["out"], grads["qkv"], win, jnp.stack(small_g)]
    up, down, out, qkv, win, small_g = _reduce_scatter(bufs, pc, me, "grads")
    out = out.reshape(DEPTH // 2, 2, -1, out.shape[1])
    gsh = dict(ffn_w_up=up, ffn_w_down=down, gla_w_out=out[:, 0], attn_w_out=out[:, 1], attn_w_qkv=qkv,
               gla_w_in=win)
    gsh = {n: g.reshape(w[n].shape) for n, g in gsh.items()}
    gsh.update(dict(zip(SMALL_SHARDED, _unpack(small_g, [w[n].shape for n in SMALL_SHARDED]))))

    small = _pack([jnp.stack(grads[n]) for n in REPLICATED] + [loss_part], F32, 16)
    small_sum = _allreduce_small(small, "small_allreduce")
    parts = _unpack(small_sum, [w[n].shape for n in REPLICATED] + [(1, LANES)])
    gsh.update(dict(zip(REPLICATED, parts[:-1])))
    loss = parts[-1][0, 0]

    delta, new_m, new_v = {}, {}, {}
    for n in names:
        shp = w[n].shape
        two_d = (-1, shp[-1])
        d, nm, nv = _adamw(w[n].reshape(two_d), gsh[n].reshape(two_d), m[n].reshape(two_d), v[n].reshape(two_d),
                           f"adamw_{n}")
        delta[n], new_m[n], new_v[n] = d.reshape(shp), nm.reshape(shp), nv.reshape(shp)

    return (loss, dx[None], *[gsh[n] for n in names], *[delta[n] for n in names],
            *[new_m[n] for n in names], *[new_v[n] for n in names])
```

```python
import jax
import jax.numpy as jnp
from jax import lax
from jax.experimental import pallas as pl
from jax.experimental.pallas import tpu as pltpu

F32 = jnp.float32
BF16 = jnp.bfloat16
MESH = pl.DeviceIdType.MESH
HIGHEST = lax.Precision.HIGHEST

D_MODEL = 1024
DEPTH = 4
GRID_W = 64
NORM_EPS = 1e-6
GLA_HEADS = 4
GLA_DK = 128
GLA_DV = 256
GLA_KEY = GLA_HEADS * GLA_DK
GLA_VAL = GLA_HEADS * GLA_DV
GLA_RANK = 16
GLA_CHUNK = 64
GLA_GATE_NORMALIZER = 16.0
GLA_IN = 2 * GLA_KEY + 2 * GLA_VAL + 2 * GLA_RANK
GLA_IN_PAD = 3200
GLA_R_BLOCK = (2 * GLA_KEY + 2 * GLA_VAL) // 128
ATT_HD = 128
ATT_QH = 8
ATT_KVH = 2
ATT_GROUP = ATT_QH // ATT_KVH
ATT_QKV = (ATT_QH + 2 * ATT_KVH) * ATT_HD
ROPE_THETA = 10000.0
D_FF = 2816
ADAM_LR = 0.001
ADAM_B1 = 0.9
ADAM_B2 = 0.999
ADAM_EPS = 1e-08
ADAM_WD = 0.01
ADAM_STEP = 10

N_CHIPS = 4
LANES = 128
VMEM_LIMIT = 56 * 1024 * 1024


def _cp(sem):
    return pltpu.CompilerParams(dimension_semantics=sem, vmem_limit_bytes=VMEM_LIMIT)


def _pick(n, cands):
    for c in cands:
        if n % c == 0:
            return c
    return n


def _dg(a, b, ca, cb):
    return lax.dot_general(a, b, (((ca,), (cb,)), ((), ())), preferred_element_type=F32)


def _sigmoid(x):
    return 0.5 * jnp.tanh(0.5 * x) + 0.5


def _rmsnorm_fwd(x, w, name):
    S, D = x.shape
    tm = _pick(S, (512, 256))

    def body(x_ref, w_ref, h_ref):
        xv = x_ref[...]
        r = lax.rsqrt(jnp.mean(xv * xv, axis=-1, keepdims=True) + NORM_EPS)
        h_ref[...] = (xv * r * w_ref[...]).astype(BF16)

    return pl.pallas_call(
        body, name=name, grid=(S // tm,),
        in_specs=[pl.BlockSpec((tm, D), lambda i: (i, 0)), pl.BlockSpec((1, D), lambda i: (0, 0))],
        out_specs=pl.BlockSpec((tm, D), lambda i: (i, 0)),
        out_shape=jax.ShapeDtypeStruct((S, D), BF16),
        compiler_params=_cp(("parallel",)),
    )(x, w)


def _loss_grad(y, t, name):
    S, D = y.shape
    tm = _pick(S, (512, 256))

    def body(y_ref, t_ref, dy_ref, dyb_ref, loss_ref):
        i = pl.program_id(0)
        d = y_ref[...] - t_ref[...]
        dy = d * (1.0 / D)
        dy_ref[...] = dy
        dyb_ref[...] = dy.astype(BF16)
        sq = jnp.sum(jnp.sum(d * d, axis=1, keepdims=True), axis=0, keepdims=True)
        part = jnp.broadcast_to(sq * (0.5 / D), (1, LANES))

        @pl.when(i == 0)
        def _():
            loss_ref[...] = part

        @pl.when(i > 0)
        def _():
            loss_ref[...] += part

    return pl.pallas_call(
        body, name=name, grid=(S // tm,),
        in_specs=[pl.BlockSpec((tm, D), lambda i: (i, 0)), pl.BlockSpec((tm, D), lambda i: (i, 0))],
        out_specs=[pl.BlockSpec((tm, D), lambda i: (i, 0)), pl.BlockSpec((tm, D), lambda i: (i, 0)),
                   pl.BlockSpec((1, LANES), lambda i: (0, 0))],
        out_shape=[jax.ShapeDtypeStruct((S, D), F32), jax.ShapeDtypeStruct((S, D), BF16),
                   jax.ShapeDtypeStruct((1, LANES), F32)],
        compiler_params=_cp(("arbitrary",)),
    )(y, t)


def _matmul(a, b, ca, cb, name, res=None, out_dtype=F32, out_chips=None, into=None):
    M, K = a.shape[1 - ca], a.shape[ca]
    pair = isinstance(b, (tuple, list))
    if pair:
        assert cb == 0 and b[0].shape == b[1].shape and b[0].shape[0] == K
        N = 2 * b[0].shape[1]
    else:
        assert b.shape[cb] == K
        N = b.shape[1 - cb]
    how = out_chips[0] if out_chips else None
    tm = M if how == "rows" else _pick(M, (1024, 1408, 512, 256, 128))
    if how == "cols":
        tn = N // N_CHIPS
    else:
        tn = _pick(N // 2 if pair else N, (1024, 1408, 768, 640, 512, 256, 128))
    tk = _pick(K, (512, 1408, 256, 128))
    nk = K // tk
    n0 = (N // 2) // tn
    if ca == 1:
        a_spec = pl.BlockSpec((tm, tk), lambda i, j, k: (i, k))
    else:
        a_spec = pl.BlockSpec((tk, tm), lambda i, j, k: (k, i))
    if pair:
        b_specs = [pl.BlockSpec((tk, tn), lambda i, j, k: (k, jnp.minimum(j, n0 - 1))),
                   pl.BlockSpec((tk, tn), lambda i, j, k: (k, jnp.maximum(j - n0, 0)))]
    elif cb == 0:
        b_specs = [pl.BlockSpec((tk, tn), lambda i, j, k: (k, j))]
    else:
        b_specs = [pl.BlockSpec((tn, tk), lambda i, j, k: (j, k))]
    if how == "cols":
        _, layer, layers = out_chips
        o_spec = pl.BlockSpec((None, tm, tn), lambda i, j, k: (j, layer * (M // tm) + i, 0))
        out_shape = jax.ShapeDtypeStruct((N_CHIPS, layers * M, tn), out_dtype)
    elif how == "rows":
        _, layer, layers = out_chips
        o_spec = pl.BlockSpec((N_CHIPS, M // N_CHIPS, tn), lambda i, j, k: (0, layer, j))
        out_shape = jax.ShapeDtypeStruct((N_CHIPS, layers * M // N_CHIPS, N), out_dtype)
    else:
        o_spec = pl.BlockSpec((tm, tn), lambda i, j, k: (i, j))
        out_shape = jax.ShapeDtypeStruct((M, N), out_dtype)
    has_res = res is not None
    nb = len(b_specs)

    def body(*refs):
        a_ref, b_refs = refs[0], refs[1:1 + nb]
        r_ref = refs[1 + nb] if has_res else None
        o_ref, acc = refs[-2], refs[-1]
        j = pl.program_id(1)
        k = pl.program_id(2)

        @pl.when(k == 0)
        def _():
            acc[...] = jnp.zeros_like(acc)

        av = a_ref[...].astype(BF16)
        if pair:
            @pl.when(j < n0)
            def _():
                acc[...] += _dg(av, b_refs[0][...].astype(BF16), ca, cb)

            @pl.when(j >= n0)
            def _():
                acc[...] += _dg(av, b_refs[1][...].astype(BF16), ca, cb)
        else:
            acc[...] += _dg(av, b_refs[0][...].astype(BF16), ca, cb)

        @pl.when(k == nk - 1)
        def _():
            v = acc[...]
            if has_res:
                v = v + r_ref[...]
            if how == "rows":
                rows = M // N_CHIPS
                for p in range(N_CHIPS):
                    o_ref[p] = v[p * rows:(p + 1) * rows, :].astype(out_dtype)
            else:
                o_ref[...] = v.astype(out_dtype)

    in_specs = [a_spec] + b_specs + ([o_spec] if has_res else [])
    args = (a,) + (tuple(b) if pair else (b,)) + ((res,) if has_res else ())
    aliases = {}
    if into is not None:
        assert into.shape == out_shape.shape
        in_specs.append(ANY)
        aliases = {len(args): 0}
        args = args + (into,)
        inner = body

        def body(*refs):
            inner(*refs[:len(args) - 1], *refs[len(args):])

    return pl.pallas_call(
        body, name=name, grid=(M // tm, N // tn, nk),
        in_specs=in_specs, out_specs=o_spec, out_shape=out_shape,
        input_output_aliases=aliases,
        scratch_shapes=[pltpu.VMEM((tm, tn), F32)],
        compiler_params=_cp(("parallel", "parallel", "arbitrary")),
    )(*args)


def _matmul_rows(a, w, name, res=None, w_layer=None, transposed=False):
    M, K = a.shape
    if w_layer is not None:
        cw = w.shape[2]
        N = N_CHIPS * cw
        w_spec = pl.BlockSpec((N_CHIPS, K, cw), lambda i: (0, w_layer, 0))
    else:
        N = w.shape[0] if transposed else w.shape[1]
        assert w.shape[1 if transposed else 0] == K
        w_spec = pl.BlockSpec(w.shape, lambda i: (0, 0))
    tm = _pick(M, (512, 256, 128))
    has_res = res is not None

    def body(*refs):
        a_ref, w_ref = refs[0], refs[1]
        r_ref = refs[2] if has_res else None
        o_ref = refs[-1]
        av = a_ref[...].astype(BF16)
        if w_layer is not None:
            for p in range(N_CHIPS):
                o_ref[:, pl.ds(p * cw, cw)] = jnp.dot(av, w_ref[p], preferred_element_type=F32)
        else:
            v = _dg(av, w_ref[...], 1, 1 if transposed else 0)
            o_ref[...] = v + r_ref[...] if has_res else v

    row = pl.BlockSpec((tm, N), lambda i: (i, 0))
    return pl.pallas_call(
        body, name=name, grid=(M // tm,),
        in_specs=[pl.BlockSpec((tm, K), lambda i: (i, 0)), w_spec] + ([row] if has_res else []),
        out_specs=row, out_shape=jax.ShapeDtypeStruct((M, N), F32),
        compiler_params=_cp(("parallel",)),
    )(*((a, w) + ((res,) if has_res else ())))


def _dgrad_norm(dy, w, x, wn, dres, name, w_layer=None):
    pair = isinstance(dy, (tuple, list))
    M = dy[0].shape[0] if pair else dy.shape[0]
    Kp = 2 * dy[0].shape[1] if pair else dy.shape[1]
    D = x.shape[1]
    if w_layer is not None:
        cw = w.shape[2]
        assert N_CHIPS * cw == Kp and w.shape[1] % D == 0
        w_spec = pl.BlockSpec((N_CHIPS, D, cw), lambda i: (0, w_layer, 0))
    else:
        assert w.shape == (D, Kp)
        w_spec = pl.BlockSpec((D, Kp), lambda i: (0, 0))
    tm = _pick(M, (256, 128))
    width = Kp // 2 if pair else Kp
    dy_specs = [pl.BlockSpec((tm, width), lambda i: (i, 0))] * (2 if pair else 1)
    nd = len(dy_specs)

    def body(*refs):
        dy_refs = refs[:nd]
        w_ref, x_ref, wn_ref, dres_ref, dx_ref, dxb_ref, dwn_ref = refs[nd:]
        i = pl.program_id(0)
        if w_layer is not None:
            dh = None
            for p in range(N_CHIPS):
                src, off = divmod(p * cw, width)
                part = _dg(dy_refs[src][:, pl.ds(off, cw)], w_ref[p], 1, 1)
                dh = part if dh is None else dh + part
        else:
            dh = _dg(dy_refs[0][...], w_ref[...], 1, 1)
        xv = x_ref[...]
        r = lax.rsqrt(jnp.mean(xv * xv, axis=-1, keepdims=True) + NORM_EPS)
        yv = xv * r
        dyv = dh * wn_ref[...]
        dxv = r * (dyv - yv * jnp.mean(dyv * yv, axis=-1, keepdims=True)) + dres_ref[...]
        dx_ref[...] = dxv
        dxb_ref[...] = dxv.astype(BF16)
        part = jnp.sum(dh * yv, axis=0, keepdims=True)

        @pl.when(i == 0)
        def _():
            dwn_ref[...] = part

        @pl.when(i > 0)
        def _():
            dwn_ref[...] += part

    row = pl.BlockSpec((tm, D), lambda i: (i, 0))
    one = pl.BlockSpec((1, D), lambda i: (0, 0))
    return pl.pallas_call(
        body, name=name, grid=(M // tm,),
        in_specs=dy_specs + [w_spec, row, one, row],
        out_specs=[row, row, one],
        out_shape=[jax.ShapeDtypeStruct((M, D), F32), jax.ShapeDtypeStruct((M, D), BF16),
                   jax.ShapeDtypeStruct((1, D), F32)],
        compiler_params=_cp(("arbitrary",)),
    )(*(tuple(dy) if pair else (dy,)), w, x, wn, dres)


FFN_TN_FWD = 256
FFN_TN_BWD = 128
FFN_ROWS = 256
PAD = 8


def _conv3(pad_ref, w, r0, tr):
    um = pad_ref[pl.ds(PAD - 1 + r0, tr), :]
    uc = pad_ref[pl.ds(PAD + r0, tr), :]
    up = pad_ref[pl.ds(PAD + 1 + r0, tr), :]
    return w[0:1, :] * um + w[1:2, :] * uc + w[2:3, :] * up, (um, uc, up)


def _zero_pads(pad_ref, S, tn):
    pad_ref[pl.ds(0, PAD), :] = jnp.zeros((PAD, tn), F32)
    pad_ref[pl.ds(PAD + S, PAD), :] = jnp.zeros((PAD, tn), F32)


def _ffn_mid_fwd(h, wup, wconv, bconv, name):
    S, D = h.shape
    F = wup.shape[1] // 2
    tn = FFN_TN_FWD
    nb = F // tn
    tr = min(FFN_ROWS, S)

    def body(h_ref, wv_ref, wg_ref, cv_ref, cg_ref, bv_ref, bg_ref, a_ref, uv_ref, ug_ref):
        _zero_pads(uv_ref, S, tn)
        _zero_pads(ug_ref, S, tn)
        hv = h_ref[...]
        uv_ref[pl.ds(PAD, S), :] = jnp.dot(hv, wv_ref[...], preferred_element_type=F32)
        ug_ref[pl.ds(PAD, S), :] = jnp.dot(hv, wg_ref[...], preferred_element_type=F32)
        cwv, cwg, bv, bg = cv_ref[...], cg_ref[...], bv_ref[...], bg_ref[...]
        for r0 in range(0, S, tr):
            cv = _conv3(uv_ref, cwv, r0, tr)[0] + bv
            cg = _conv3(ug_ref, cwg, r0, tr)[0] + bg
            a_ref[pl.ds(r0, tr), :] = (cg * _sigmoid(cg) * cv).astype(BF16)

    col = lambda off: (lambda j: (0, j + off))
    padded = pl.BlockSpec((S + 2 * PAD, tn), col(0))
    return pl.pallas_call(
        body, name=name, grid=(nb,),
        in_specs=[pl.BlockSpec((S, D), lambda j: (0, 0)),
                  pl.BlockSpec((D, tn), col(0)), pl.BlockSpec((D, tn), col(nb)),
                  pl.BlockSpec((3, tn), col(0)), pl.BlockSpec((3, tn), col(nb)),
                  pl.BlockSpec((1, tn), col(0)), pl.BlockSpec((1, tn), col(nb))],
        out_specs=[pl.BlockSpec((S, tn), col(0)), padded, padded],
        out_shape=[jax.ShapeDtypeStruct((S, F), BF16), jax.ShapeDtypeStruct((S + 2 * PAD, F), F32),
                   jax.ShapeDtypeStruct((S + 2 * PAD, F), F32)],
        compiler_params=_cp(("parallel",)),
    )(h, wup, wup, wconv, wconv, bconv, bconv)


def _rows8(rows):
    n = rows[0].shape[1]
    idx = lax.broadcasted_iota(jnp.int32, (8, n), 0)
    out = jnp.zeros((8, n), F32)
    for k, r in enumerate(rows):
        out = jnp.where(idx == k, r, out)
    return out


def _ffn_mid_bwd(dyb, wdown, uv, ug, wconv, bconv, name):
    S, D = dyb.shape
    F = wdown.shape[0]
    tn = FFN_TN_BWD
    nb = F // tn
    tr = min(FFN_ROWS, S)

    def body(dy_ref, wd_ref, uv_ref, ug_ref, cv_ref, cg_ref, bv_ref, bg_ref,
             duv_ref, dug_ref, a_ref, gwv_ref, gwg_ref, pdv, pdg):
        for p in (pdv, pdg):
            _zero_pads(p, S, tn)
        wd = wd_ref[...]
        cwv, cwg, bv, bg = cv_ref[...], cg_ref[...], bv_ref[...], bg_ref[...]
        zero = jnp.zeros((1, tn), F32)
        gv = [zero, zero, zero, zero]
        gg = [zero, zero, zero, zero]
        for r0 in range(0, S, tr):
            cv, shv = _conv3(uv_ref, cwv, r0, tr)
            cg, shg = _conv3(ug_ref, cwg, r0, tr)
            cv = cv + bv
            cg = cg + bg
            sg = _sigmoid(cg)
            sl = cg * sg
            a_ref[pl.ds(r0, tr), :] = (sl * cv).astype(BF16)
            da = _dg(dy_ref[pl.ds(r0, tr), :], wd, 1, 1)
            dcv = da * sl
            dcg = da * cv * (sg * (1.0 + cg * (1.0 - sg)))
            pdv[pl.ds(PAD + r0, tr), :] = dcv
            pdg[pl.ds(PAD + r0, tr), :] = dcg
            for k in range(3):
                gv[k] = gv[k] + jnp.sum(dcv * shv[k], axis=0, keepdims=True)
                gg[k] = gg[k] + jnp.sum(dcg * shg[k], axis=0, keepdims=True)
            gv[3] = gv[3] + jnp.sum(dcv, axis=0, keepdims=True)
            gg[3] = gg[3] + jnp.sum(dcg, axis=0, keepdims=True)
        gwv_ref[...] = _rows8(gv)
        gwg_ref[...] = _rows8(gg)
        for r0 in range(0, S, tr):
            for pd, cw, out in ((pdv, cwv, duv_ref), (pdg, cwg, dug_ref)):
                dm = pd[pl.ds(PAD - 1 + r0, tr), :]
                dc = pd[pl.ds(PAD + r0, tr), :]
                dp = pd[pl.ds(PAD + 1 + r0, tr), :]
                out[pl.ds(r0, tr), :] = (cw[0:1, :] * dp + cw[1:2, :] * dc + cw[2:3, :] * dm).astype(BF16)

    col = lambda off: (lambda j: (0, j + off))
    blk = pl.BlockSpec((S, tn), col(0))
    padded = pl.BlockSpec((S + 2 * PAD, tn), col(0))
    g8 = pl.BlockSpec((8, tn), col(0))
    return pl.pallas_call(
        body, name=name, grid=(nb,),
        in_specs=[pl.BlockSpec((S, D), lambda j: (0, 0)), pl.BlockSpec((tn, D), lambda j: (j, 0)), padded, padded,
                  pl.BlockSpec((3, tn), col(0)), pl.BlockSpec((3, tn), col(nb)),
                  pl.BlockSpec((1, tn), col(0)), pl.BlockSpec((1, tn), col(nb))],
        out_specs=[blk, blk, blk, g8, g8],
        out_shape=[jax.ShapeDtypeStruct((S, F), BF16), jax.ShapeDtypeStruct((S, F), BF16),
                   jax.ShapeDtypeStruct((S, F), BF16), jax.ShapeDtypeStruct((8, F), F32),
                   jax.ShapeDtypeStruct((8, F), F32)],
        scratch_shapes=[pltpu.VMEM((S + 2 * PAD, tn), F32)] * 2,
        compiler_params=_cp(("parallel",)),
    )(dyb, wdown, uv, ug, wconv, wconv, bconv, bconv)


def _log_sigmoid(x):
    return jnp.minimum(x, 0.0) - jnp.log(1.0 + jnp.exp(-jnp.abs(x)))


def _gla_gate_fwd(proj, wgf, bgf, wgb, bgb, name):
    S = proj.shape[0]
    tm = _pick(S, (512, 256))

    def body(r_ref, wf_ref, bf_ref, wb_ref, bb_ref, laf_ref, lab_ref):
        r = r_ref[...].astype(BF16)
        lf = jnp.dot(r, wf_ref[...].astype(BF16), preferred_element_type=F32) + bf_ref[...]
        lb = jnp.dot(r, wb_ref[...].astype(BF16), preferred_element_type=F32) + bb_ref[...]
        laf_ref[...] = _log_sigmoid(lf) * (1.0 / GLA_GATE_NORMALIZER)
        lab_ref[...] = _log_sigmoid(lb) * (1.0 / GLA_GATE_NORMALIZER)

    full = lambda shp: pl.BlockSpec(shp, lambda i: (0, 0))
    row = pl.BlockSpec((tm, GLA_KEY), lambda i: (i, 0))
    return pl.pallas_call(
        body, name=name, grid=(S // tm,),
        in_specs=[pl.BlockSpec((tm, LANES), lambda i: (i, GLA_R_BLOCK)),
                  full((LANES, GLA_KEY)), full((1, GLA_KEY)), full((LANES, GLA_KEY)), full((1, GLA_KEY))],
        out_specs=[row, row],
        out_shape=[jax.ShapeDtypeStruct((S, GLA_KEY), F32)] * 2,
        compiler_params=_cp(("parallel",)),
    )(proj, wgf, bgf, wgb, bgb)


def _gla_gate_bwd(dlaf, dlab, proj, wgf, bgf, wgb, bgb, name):
    S = proj.shape[0]
    tm = _pick(S, (512, 256))

    def body(dlf_ref, dlb_ref, r_ref, wf_ref, bf_ref, wb_ref, bb_ref, dr_ref, dwf_ref, dbf_ref, dwb_ref, dbb_ref):
        i = pl.program_id(0)
        r = r_ref[...].astype(BF16)
        wf = wf_ref[...].astype(BF16)
        wb = wb_ref[...].astype(BF16)
        lf = jnp.dot(r, wf, preferred_element_type=F32) + bf_ref[...]
        lb = jnp.dot(r, wb, preferred_element_type=F32) + bb_ref[...]
        glf = dlf_ref[...] * (1.0 / GLA_GATE_NORMALIZER) * (1.0 / (1.0 + jnp.exp(lf)))
        glb = dlb_ref[...] * (1.0 / GLA_GATE_NORMALIZER) * (1.0 / (1.0 + jnp.exp(lb)))
        gfb = glf.astype(BF16)
        gbb = glb.astype(BF16)
        dr_ref[...] = _dg(gfb, wf, 1, 1) + _dg(gbb, wb, 1, 1)
        parts = (_dg(r, gfb, 0, 0), jnp.sum(glf, axis=0, keepdims=True),
                 _dg(r, gbb, 0, 0), jnp.sum(glb, axis=0, keepdims=True))
        outs = (dwf_ref, dbf_ref, dwb_ref, dbb_ref)

        @pl.when(i == 0)
        def _():
            for o, p in zip(outs, parts):
                o[...] = p

        @pl.when(i > 0)
        def _():
            for o, p in zip(outs, parts):
                o[...] += p

    full = lambda shp: pl.BlockSpec(shp, lambda i: (0, 0))
    row = pl.BlockSpec((tm, GLA_KEY), lambda i: (i, 0))
    return pl.pallas_call(
        body, name=name, grid=(S // tm,),
        in_specs=[row, row, pl.BlockSpec((tm, LANES), lambda i: (i, GLA_R_BLOCK)),
                  full((LANES, GLA_KEY)), full((1, GLA_KEY)), full((LANES, GLA_KEY)), full((1, GLA_KEY))],
        out_specs=[pl.BlockSpec((tm, LANES), lambda i: (i, 0)),
                   full((LANES, GLA_KEY)), full((1, GLA_KEY)), full((LANES, GLA_KEY)), full((1, GLA_KEY))],
        out_shape=[jax.ShapeDtypeStruct((S, LANES), F32),
                   jax.ShapeDtypeStruct((LANES, GLA_KEY), F32), jax.ShapeDtypeStruct((1, GLA_KEY), F32),
                   jax.ShapeDtypeStruct((LANES, GLA_KEY), F32), jax.ShapeDtypeStruct((1, GLA_KEY), F32)],
        compiler_params=_cp(("arbitrary",)),
    )(dlaf, dlab, proj, wgf, bgf, wgb, bgb)


def _gla_masks(rev):
    C = GLA_CHUNK
    t = lax.broadcasted_iota(jnp.int32, (C, C), 0)
    s = lax.broadcasted_iota(jnp.int32, (C, C), 1)
    if rev:
        return (s >= t), (s > t), (t >= s), (t > s)
    return (s <= t), (s <= t), (t <= s), (t <= s)


def _cum_dot(cum, x):
    return jnp.dot(cum.astype(F32), x, precision=HIGHEST, preferred_element_type=F32)


def _gla_chunk_common(q, k, la, cum, end_row):
    b = _cum_dot(cum, la)
    bend = b[end_row:end_row + 1, :]
    e = jnp.exp(b)
    qd = q * (GLA_DK ** -0.5) * e
    ei = jnp.exp(-b)
    ee = jnp.exp(bend - b)
    d = jnp.exp(bend)
    return e, ei, ee, d, qd, k * ei, k * ee


GLA_CB = 8


def _gla_specs(S, rev_order):
    n = S // GLA_CHUNK
    cb = min(GLA_CB, n)
    nblk = n // cb
    rows = cb * GLA_CHUNK
    ci = (lambda i: nblk - 1 - i) if rev_order else (lambda i: i)
    q_spec = pl.BlockSpec((rows, GLA_DK), lambda h, i: (ci(i), h))
    k_spec = pl.BlockSpec((rows, GLA_DK), lambda h, i: (ci(i), GLA_HEADS + h))
    v_spec = pl.BlockSpec((rows, GLA_DV), lambda h, i: (ci(i), GLA_KEY * 2 // GLA_DV + h))
    la_spec = pl.BlockSpec((rows, GLA_DK), lambda h, i: (ci(i), h))
    o_spec = pl.BlockSpec((rows, GLA_DV), lambda h, i: (ci(i), h))
    st_spec = pl.BlockSpec((1, cb, GLA_DV, GLA_DK), lambda h, i: (h, ci(i), 0, 0))
    return n, cb, nblk, q_spec, k_spec, v_spec, la_spec, o_spec, st_spec


def _gla_scan_fwd(proj, la, rev, name):
    S = proj.shape[0]
    C = GLA_CHUNK
    n, cb, nblk, q_spec, k_spec, v_spec, la_spec, o_spec, st_spec = _gla_specs(S, rev)
    end_row = 0 if rev else C - 1
    order = list(range(cb))[::-1] if rev else list(range(cb))

    def body(q_ref, k_ref, v_ref, la_ref, o_ref, st_ref, state):
        i = pl.program_id(1)

        @pl.when(i == 0)
        def _():
            state[...] = jnp.zeros_like(state)

        cum, mask, _, _ = _gla_masks(rev)
        pre, intra, kv = {}, {}, {}
        for cc in order:
            rows = pl.ds(cc * C, C)
            q, k, v, lav = q_ref[rows, :], k_ref[rows, :], v_ref[rows, :], la_ref[rows, :]
            _, _, _, d, qd, ki, ke = _gla_chunk_common(q, k, lav, cum, end_row)
            qdb, kib, keb, vb = qd.astype(BF16), ki.astype(BF16), ke.astype(BF16), v.astype(BF16)
            pre[cc] = (d, qdb)
            att = jnp.where(mask, _dg(qdb, kib, 1, 1), 0.0)
            intra[cc] = jnp.dot(att.astype(BF16), vb, preferred_element_type=F32)
            kv[cc] = _dg(vb, keb, 0, 0)
        st = state[...]
        for cc in order:
            d, qdb = pre[cc]
            o_ref[pl.ds(cc * C, C), :] = intra[cc] + _dg(qdb, st.astype(BF16), 1, 1)
            st_ref[0, cc] = st
            st = st * d + kv[cc]
        state[...] = st

    return pl.pallas_call(
        body, name=name, grid=(GLA_HEADS, nblk),
        in_specs=[q_spec, k_spec, v_spec, la_spec],
        out_specs=[o_spec, st_spec],
        out_shape=[jax.ShapeDtypeStruct((S, GLA_VAL), F32),
                   jax.ShapeDtypeStruct((GLA_HEADS, n, GLA_DV, GLA_DK), F32)],
        scratch_shapes=[pltpu.VMEM((GLA_DV, GLA_DK), F32)],
        compiler_params=_cp(("parallel", "arbitrary")),
    )(proj, proj, proj, la)


def _gla_scan_bwd(do, proj, la, states, rev, name):
    S = proj.shape[0]
    C = GLA_CHUNK
    n, cb, nblk, q_spec, k_spec, v_spec, la_spec, o_spec, st_spec = _gla_specs(S, not rev)
    end_row = 0 if rev else C - 1
    order = list(range(cb)) if rev else list(range(cb))[::-1]

    def body(do_ref, q_ref, k_ref, v_ref, la_ref, st_ref, dq_ref, dk_ref, dv_ref, dla_ref, gstate):
        i = pl.program_id(1)

        @pl.when(i == 0)
        def _():
            gstate[...] = jnp.zeros_like(gstate)

        cum, mask, cum_t, mask_t = _gla_masks(rev)
        g = gstate[...]
        for cc in order:
            rows = pl.ds(cc * C, C)
            q, k, v, lav = q_ref[rows, :], k_ref[rows, :], v_ref[rows, :], la_ref[rows, :]
            dov = do_ref[rows, :]
            st = st_ref[0, cc]
            e, ei, ee, d, qd, ki, ke = _gla_chunk_common(q, k, lav, cum, end_row)
            qdb, kib, keb, vb = qd.astype(BF16), ki.astype(BF16), ke.astype(BF16), v.astype(BF16)
            dob, gb, stb = dov.astype(BF16), g.astype(BF16), st.astype(BF16)
            att_t = jnp.where(mask_t, _dg(kib, qdb, 1, 1), 0.0)
            da = jnp.where(mask, _dg(dob, vb, 1, 1), 0.0)
            da_t = jnp.where(mask_t, _dg(vb, dob, 1, 1), 0.0)
            dv_ref[rows, :] = jnp.dot(att_t.astype(BF16), dob, preferred_element_type=F32) + _dg(keb, gb, 1, 1)
            dqd = (jnp.dot(da.astype(BF16), kib, preferred_element_type=F32)
                   + jnp.dot(dob, stb, preferred_element_type=F32))
            dki = jnp.dot(da_t.astype(BF16), qdb, preferred_element_type=F32)
            dke = jnp.dot(vb, gb, preferred_element_type=F32)
            dd = jnp.sum(st * g, axis=0, keepdims=True)
            g = g * d + _dg(dob, qdb, 0, 0)
            dq_ref[rows, :] = dqd * e * (GLA_DK ** -0.5)
            dk_ref[rows, :] = dki * ei + dke * ee
            dkeke = dke * ke
            db = dqd * qd - dki * ki - dkeke
            dbend = jnp.sum(dkeke, axis=0, keepdims=True) + dd * d
            dla_ref[rows, :] = _cum_dot(cum_t, db) + dbend
        gstate[...] = g

    key_out = la_spec
    return pl.pallas_call(
        body, name=name, grid=(GLA_HEADS, nblk),
        in_specs=[o_spec, q_spec, k_spec, v_spec, la_spec, st_spec],
        out_specs=[key_out, key_out, o_spec, key_out],
        out_shape=[jax.ShapeDtypeStruct((S, GLA_KEY), F32), jax.ShapeDtypeStruct((S, GLA_KEY), F32),
                   jax.ShapeDtypeStruct((S, GLA_VAL), F32), jax.ShapeDtypeStruct((S, GLA_KEY), F32)],
        scratch_shapes=[pltpu.VMEM((GLA_DV, GLA_DK), F32)],
        compiler_params=_cp(("parallel", "arbitrary")),
    )(do, proj, proj, proj, la, states)


def _gla_out_fwd(of, ob, proj, gn, name):
    S = of.shape[0]
    tm = _pick(S, (256, 128))
    gblk = (2 * GLA_KEY + GLA_VAL) // GLA_VAL

    def body(of_ref, ob_ref, g_ref, gn_ref, z_ref):
        gnv = gn_ref[...]
        for h in range(GLA_HEADS):
            cols = pl.ds(h * GLA_DV, GLA_DV)
            o = of_ref[:, cols] + ob_ref[:, cols]
            r = lax.rsqrt(jnp.mean(o * o, axis=-1, keepdims=True) + NORM_EPS)
            gv = g_ref[:, cols]
            z_ref[:, cols] = (o * r * gnv * (gv * _sigmoid(gv))).astype(BF16)

    row = pl.BlockSpec((tm, GLA_VAL), lambda i: (i, 0))
    return pl.pallas_call(
        body, name=name, grid=(S // tm,),
        in_specs=[row, row, pl.BlockSpec((tm, GLA_VAL), lambda i: (i, gblk)),
                  pl.BlockSpec((1, GLA_DV), lambda i: (0, 0))],
        out_specs=row,
        out_shape=jax.ShapeDtypeStruct((S, GLA_VAL), BF16),
        compiler_params=_cp(("parallel",)),
    )(of, ob, proj, gn)


def _gla_out_bwd(dz, of, ob, proj, gn, name):
    S = of.shape[0]
    tm = _pick(S, (256, 128))
    gblk = (2 * GLA_KEY + GLA_VAL) // GLA_VAL

    def body(dz_ref, of_ref, ob_ref, g_ref, gn_ref, do_ref, dg_ref, dgn_ref):
        i = pl.program_id(0)
        gnv = gn_ref[...]
        part = jnp.zeros((1, GLA_DV), F32)
        for h in range(GLA_HEADS):
            cols = pl.ds(h * GLA_DV, GLA_DV)
            o = of_ref[:, cols] + ob_ref[:, cols]
            r = lax.rsqrt(jnp.mean(o * o, axis=-1, keepdims=True) + NORM_EPS)
            y = o * r
            gv = g_ref[:, cols]
            sg = _sigmoid(gv)
            dzv = dz_ref[:, cols]
            dg_ref[:, cols] = dzv * (y * gnv) * (sg * (1.0 + gv * (1.0 - sg)))
            don = dzv * (gv * sg)
            part = part + jnp.sum(don * y, axis=0, keepdims=True)
            dy = don * gnv
            do_ref[:, cols] = r * (dy - y * jnp.mean(dy * y, axis=-1, keepdims=True))

        @pl.when(i == 0)
        def _():
            dgn_ref[...] = part

        @pl.when(i > 0)
        def _():
            dgn_ref[...] += part

    row = pl.BlockSpec((tm, GLA_VAL), lambda i: (i, 0))
    one = pl.BlockSpec((1, GLA_DV), lambda i: (0, 0))
    return pl.pallas_call(
        body, name=name, grid=(S // tm,),
        in_specs=[row, row, row, pl.BlockSpec((tm, GLA_VAL), lambda i: (i, gblk)), one],
        out_specs=[row, row, one],
        out_shape=[jax.ShapeDtypeStruct((S, GLA_VAL), F32), jax.ShapeDtypeStruct((S, GLA_VAL), F32),
                   jax.ShapeDtypeStruct((1, GLA_DV), F32)],
        compiler_params=_cp(("arbitrary",)),
    )(dz, of, ob, proj, gn)


N_QK_HEADS = ATT_QH + ATT_KVH


def _qk_prep_fwd(proj, qn, kn, rc, rs, name):
    S = proj.shape[0]
    tm = _pick(S, (256, 128))
    W = N_QK_HEADS * ATT_HD
    scale = ATT_HD ** -0.5

    def body(p_ref, qn_ref, kn_ref, rc_ref, rs_ref, v_in_ref, qk_ref, v_ref, kt_ref, vt_ref):
        c, s = rc_ref[...], rs_ref[...]
        for h in range(N_QK_HEADS):
            cols = pl.ds(h * ATT_HD, ATT_HD)
            w = qn_ref[...] if h < ATT_QH else kn_ref[...]
            xv = p_ref[:, cols]
            r = lax.rsqrt(jnp.mean(xv * xv, axis=-1, keepdims=True) + NORM_EPS)
            y = xv * r * w
            out = y * c + pltpu.roll(y, ATT_HD // 2, 1) * s
            if h < ATT_QH:
                qk_ref[:, cols] = (out * scale).astype(BF16)
            else:
                qk_ref[:, cols] = out.astype(BF16)
                kt_ref[pl.ds((h - ATT_QH) * ATT_HD, ATT_HD), :] = out.T.astype(BF16)
        v_ref[...] = v_in_ref[...].astype(BF16)
        for h in range(ATT_KVH):
            vt_ref[pl.ds(h * ATT_HD, ATT_HD), :] = v_in_ref[:, pl.ds(h * ATT_HD, ATT_HD)].T.astype(BF16)

    one = pl.BlockSpec((1, ATT_HD), lambda i: (0, 0))
    tab = pl.BlockSpec((tm, ATT_HD), lambda i: (i, 0))
    vw = ATT_KVH * ATT_HD
    tr = pl.BlockSpec((vw, tm), lambda i: (0, i))
    return pl.pallas_call(
        body, name=name, grid=(S // tm,),
        in_specs=[pl.BlockSpec((tm, W), lambda i: (i, 0)), one, one, tab, tab,
                  pl.BlockSpec((tm, vw), lambda i: (i, W // vw))],
        out_specs=[pl.BlockSpec((tm, W), lambda i: (i, 0)), pl.BlockSpec((tm, vw), lambda i: (i, 0)), tr, tr],
        out_shape=[jax.ShapeDtypeStruct((S, W), BF16), jax.ShapeDtypeStruct((S, vw), BF16),
                   jax.ShapeDtypeStruct((vw, S), BF16), jax.ShapeDtypeStruct((vw, S), BF16)],
        compiler_params=_cp(("parallel",)),
    )(proj, qn, kn, rc, rs, proj)


def _qk_prep_bwd(dqk, proj, qn, kn, rc, rs, name):
    S = proj.shape[0]
    tm = _pick(S, (256, 128))
    W = N_QK_HEADS * ATT_HD

    def body(d_ref, p_ref, qn_ref, kn_ref, rc_ref, rs_ref, dp_ref, dqn_ref, dkn_ref):
        i = pl.program_id(0)
        c, s = rc_ref[...], rs_ref[...]
        parts = [jnp.zeros((1, ATT_HD), F32), jnp.zeros((1, ATT_HD), F32)]
        for h in range(N_QK_HEADS):
            cols = pl.ds(h * ATT_HD, ATT_HD)
            w = qn_ref[...] if h < ATT_QH else kn_ref[...]
            dout = d_ref[:, cols]
            dy = dout * c + pltpu.roll(dout * s, ATT_HD // 2, 1)
            xv = p_ref[:, cols]
            r = lax.rsqrt(jnp.mean(xv * xv, axis=-1, keepdims=True) + NORM_EPS)
            xr = xv * r
            which = 0 if h < ATT_QH else 1
            parts[which] = parts[which] + jnp.sum(dy * xr, axis=0, keepdims=True)
            dxr = dy * w
            dp_ref[:, cols] = r * (dxr - xr * jnp.mean(dxr * xr, axis=-1, keepdims=True))

        @pl.when(i == 0)
        def _():
            dqn_ref[...] = parts[0]
            dkn_ref[...] = parts[1]

        @pl.when(i > 0)
        def _():
            dqn_ref[...] += parts[0]
            dkn_ref[...] += parts[1]

    one = pl.BlockSpec((1, ATT_HD), lambda i: (0, 0))
    tab = pl.BlockSpec((tm, ATT_HD), lambda i: (i, 0))
    row = pl.BlockSpec((tm, W), lambda i: (i, 0))
    return pl.pallas_call(
        body, name=name, grid=(S // tm,),
        in_specs=[row, row, one, one, tab, tab],
        out_specs=[row, one, one],
        out_shape=[jax.ShapeDtypeStruct((S, W), F32), jax.ShapeDtypeStruct((1, ATT_HD), F32),
                   jax.ShapeDtypeStruct((1, ATT_HD), F32)],
        compiler_params=_cp(("arbitrary",)),
    )(dqk, proj, qn, kn, rc, rs)


ATT_TQ = 256
LSE_ROWS = 8


def _attn_fwd(qk, vt, name):
    S = qk.shape[0]
    tq = min(ATT_TQ, S)

    def body(q_ref, k_ref, vt_ref, o_ref, lse_ref):
        st = _dg(k_ref[...], q_ref[...], 1, 1)
        m = jnp.max(st, axis=0, keepdims=True)
        pt = jnp.exp(st - m)
        l = jnp.sum(pt, axis=0, keepdims=True)
        ot = jnp.dot(vt_ref[...], pt.astype(BF16), preferred_element_type=F32)
        o_ref[...] = (ot * (1.0 / l)).T
        lse_ref[...] = jnp.broadcast_to(m + jnp.log(l), (LSE_ROWS, tq))

    qo = pl.BlockSpec((tq, ATT_HD), lambda h, i: (i, h))
    return pl.pallas_call(
        body, name=name, grid=(ATT_QH, S // tq),
        in_specs=[qo, pl.BlockSpec((S, ATT_HD), lambda h, i: (0, ATT_QH + h // ATT_GROUP)),
                  pl.BlockSpec((ATT_HD, S), lambda h, i: (h // ATT_GROUP, 0))],
        out_specs=[qo, pl.BlockSpec((LSE_ROWS, tq), lambda h, i: (h, i))],
        out_shape=[jax.ShapeDtypeStruct((S, ATT_QH * ATT_HD), F32),
                   jax.ShapeDtypeStruct((ATT_QH * LSE_ROWS, S), F32)],
        compiler_params=_cp(("parallel", "parallel")),
    )(qk, qk, vt)


def _attn_bwd(do, o, lse, qk, v, kt, name):
    S = qk.shape[0]
    tq = min(ATT_TQ, S)
    scale = ATT_HD ** -0.5

    def body(do_ref, o_ref, lse_ref, q_ref, k_ref, v_ref, kt_ref, dq_ref, dk_ref, dv_ref):
        g = pl.program_id(1)
        i = pl.program_id(2)

        @pl.when((g == 0) & (i == 0))
        def _():
            dk_ref[...] = jnp.zeros_like(dk_ref)
            dv_ref[...] = jnp.zeros_like(dv_ref)

        q = q_ref[...]
        dov = do_ref[...]
        dob = dov.astype(BF16)
        delta = jnp.sum((dov * o_ref[...]).T, axis=0, keepdims=True)
        st = _dg(k_ref[...], q, 1, 1)
        pt = jnp.exp(st - lse_ref[0:1, :])
        dpt = _dg(v_ref[...], dob, 1, 1)
        dst = (pt * (dpt - delta)).astype(BF16)
        dv_ref[...] += jnp.dot(pt.astype(BF16), dob, preferred_element_type=F32)
        dk_ref[...] += jnp.dot(dst, q, preferred_element_type=F32)
        dq_ref[...] = jnp.dot(kt_ref[...], dst, preferred_element_type=F32).T * scale

    qo = pl.BlockSpec((tq, ATT_HD), lambda kv, g, i: (i, kv * ATT_GROUP + g))
    kvo = pl.BlockSpec((S, ATT_HD), lambda kv, g, i: (0, kv))
    return pl.pallas_call(
        body, name=name, grid=(ATT_KVH, ATT_GROUP, S // tq),
        in_specs=[qo, qo, pl.BlockSpec((LSE_ROWS, tq), lambda kv, g, i: (kv * ATT_GROUP + g, i)), qo,
                  pl.BlockSpec((S, ATT_HD), lambda kv, g, i: (0, ATT_QH + kv)), kvo,
                  pl.BlockSpec((ATT_HD, S), lambda kv, g, i: (kv, 0))],
        out_specs=[qo, kvo, kvo],
        out_shape=[jax.ShapeDtypeStruct((S, ATT_QH * ATT_HD), F32),
                   jax.ShapeDtypeStruct((S, ATT_KVH * ATT_HD), F32),
                   jax.ShapeDtypeStruct((S, ATT_KVH * ATT_HD), F32)],
        compiler_params=_cp(("parallel", "arbitrary", "arbitrary")),
    )(do, o, lse, qk, qk, v, kt)


def _adamw(w, g, m, v, name):
    rows, cols = w.shape
    tr = rows
    for cand in (512, 256, 128, 64, 32, 16, 8):
        if rows % cand == 0 and cand * cols * 4 <= 2 * 1024 * 1024:
            tr = cand
            break

    def body(w_ref, g_ref, m_ref, v_ref, d_ref, nm_ref, nv_ref):
        gv = g_ref[...]
        nm = ADAM_B1 * m_ref[...] + (1.0 - ADAM_B1) * gv
        nv = ADAM_B2 * v_ref[...] + (1.0 - ADAM_B2) * (gv * gv)
        m_hat = nm / (1.0 - ADAM_B1 ** ADAM_STEP)
        v_hat = nv / (1.0 - ADAM_B2 ** ADAM_STEP)
        d_ref[...] = -ADAM_LR * (m_hat / (jnp.sqrt(v_hat) + ADAM_EPS) + ADAM_WD * w_ref[...])
        nm_ref[...] = nm
        nv_ref[...] = nv

    blk = pl.BlockSpec((tr, cols), lambda i: (i, 0))
    return pl.pallas_call(
        body, name=name, grid=(rows // tr,),
        in_specs=[blk] * 4, out_specs=[blk] * 3,
        out_shape=[jax.ShapeDtypeStruct((rows, cols), F32)] * 3,
        compiler_params=_cp(("parallel",)),
    )(w, g, m, v)


ANY = pl.BlockSpec(memory_space=pl.ANY)


def _place():
    return lax.axis_index("x"), lax.axis_index("y"), lax.axis_index("c")


def _other_chips(x, y):
    return [(1 - x, y), (x, 1 - y), (1 - x, 1 - y)]


def _half_rows(c, H):
    return pl.ds(pl.multiple_of(c * H, 8), H)


def _allreduce_small(v, name):
    R = v.shape[0]
    n_dev = 8

    def body(v_ref, sum_ref, all_ref, send_sems, recv_sems, local_sem):
        x, y, c = _place()
        me, sibling = (x, y, c), (x, y, 1 - c)
        chips = _other_chips(x, y)

        def rows(px, py, pc):
            return all_ref.at[pl.ds(pl.multiple_of((4 * px + 2 * py + pc) * R, 8), R), :]

        def copy(k, block, to, src=None):
            return pltpu.make_async_remote_copy(
                src_ref=rows(*block) if src is None else src, dst_ref=rows(*block),
                send_sem=send_sems.at[k], recv_sem=recv_sems.at[k], device_id=to, device_id_type=MESH)

        own = pltpu.make_async_copy(v_ref, rows(*me), local_sem)
        own.start()
        first = [copy(0, me, sibling, src=v_ref)]
        first += [copy(1 + j, me, (*chip, c), src=v_ref) for j, chip in enumerate(chips)]
        for cp in first:
            cp.start()
        passed = [copy(4 + j, (*chip, c), sibling) for j, chip in enumerate(chips)]
        for j, chip in enumerate(chips):
            copy(1 + j, (*chip, c), me).wait_recv()
            passed[j].start()
        copy(0, sibling, me).wait_recv()
        for j, chip in enumerate(chips):
            copy(4 + j, (*chip, 1 - c), me).wait_recv()
        for cp in first + passed:
            cp.wait_send()
        own.wait()
        acc = all_ref[pl.ds(0, R), :]
        for d in range(1, n_dev):
            acc = acc + all_ref[pl.ds(d * R, R), :]
        sum_ref[...] = acc

    vm = pl.BlockSpec(memory_space=pltpu.VMEM)
    return pl.pallas_call(
        body, name=name,
        in_specs=[vm], out_specs=[vm, vm],
        out_shape=[jax.ShapeDtypeStruct((R, LANES), F32), jax.ShapeDtypeStruct((n_dev * R, LANES), F32)],
        scratch_shapes=[pltpu.SemaphoreType.DMA((7,)), pltpu.SemaphoreType.DMA((7,)), pltpu.SemaphoreType.DMA],
    )(v)[0]


def _swap_other_half(bufs, name):
    n = len(bufs)
    halves = [b.shape[1] // 2 for b in bufs]

    def body(*refs):
        g_refs, got_refs = refs[:n], refs[n:2 * n]
        send_sems, recv_sems = refs[2 * n:]
        x, y, c = _place()
        copies = [pltpu.make_async_remote_copy(
            src_ref=g_refs[k].at[p, _half_rows(1 - c, halves[k])], dst_ref=got_refs[k].at[p],
            send_sem=send_sems.at[N_CHIPS * k + p], recv_sem=recv_sems.at[N_CHIPS * k + p],
            device_id=(x, y, 1 - c), device_id_type=MESH) for k in range(n) for p in range(N_CHIPS)]
        for cp in copies:
            cp.start()
        for cp in copies:
            cp.wait_recv()
        for cp in copies:
            cp.wait_send()

    return pl.pallas_call(
        body, name=name, in_specs=[ANY] * n, out_specs=[ANY] * n,
        out_shape=[jax.ShapeDtypeStruct((N_CHIPS, h, b.shape[2]), b.dtype) for b, h in zip(bufs, halves)],
        scratch_shapes=[pltpu.SemaphoreType.DMA((N_CHIPS * n,)), pltpu.SemaphoreType.DMA((N_CHIPS * n,))],
    )(*bufs)


def _send_chip_partials(sbs, name):
    n = len(sbs)

    def body(*refs):
        s_refs, got_refs = refs[:n], refs[n:2 * n]
        send_sems, recv_sems = refs[2 * n:]
        x, y, c = _place()
        me = 2 * x + y
        chips = _other_chips(x, y)

        def copy(k, j, src_slot, dst_slot):
            px, py = chips[j]
            return pltpu.make_async_remote_copy(
                src_ref=s_refs[k].at[src_slot], dst_ref=got_refs[k].at[dst_slot], send_sem=send_sems.at[3 * k + j],
                recv_sem=recv_sems.at[3 * k + j], device_id=(px, py, c), device_id_type=MESH)

        copies = [copy(k, j, 2 * px + py, me) for k in range(n) for j, (px, py) in enumerate(chips)]
        for cp in copies:
            cp.start()
        for k in range(n):
            for j, (px, py) in enumerate(chips):
                copy(k, j, me, 2 * px + py).wait_recv()
        for cp in copies:
            cp.wait_send()

    return pl.pallas_call(
        body, name=name, in_specs=[ANY] * n, out_specs=[ANY] * n,
        out_shape=[jax.ShapeDtypeStruct(s.shape, s.dtype) for s in sbs],
        scratch_shapes=[pltpu.SemaphoreType.DMA((3 * n,)), pltpu.SemaphoreType.DMA((3 * n,))],
    )(*sbs)


def _join_halves(bufs, name):
    n = len(bufs)
    halves = [b.shape[0] // 2 for b in bufs]

    def body(*refs):
        outs = refs[n:2 * n]
        send_sems, recv_sems = refs[2 * n:]
        x, y, c = _place()

        def copy(k, core):
            blk = outs[k].at[_half_rows(core, halves[k])]
            return pltpu.make_async_remote_copy(src_ref=blk, dst_ref=blk, send_sem=send_sems.at[k],
                                                recv_sem=recv_sems.at[k], device_id=(x, y, 1 - c),
                                                device_id_type=MESH)

        sends = [copy(k, c) for k in range(n)]
        for cp in sends:
            cp.start()
        for k in range(n):
            copy(k, 1 - c).wait_recv()
        for cp in sends:
            cp.wait_send()

    return pl.pallas_call(
        body, name=name, in_specs=[ANY] * n, out_specs=[ANY] * n,
        out_shape=[jax.ShapeDtypeStruct(b.shape, b.dtype) for b in bufs],
        input_output_aliases={k: k for k in range(n)},
        scratch_shapes=[pltpu.SemaphoreType.DMA((n,)), pltpu.SemaphoreType.DMA((n,))],
    )(*bufs)


def _rs_rows(H, width):
    for cand in (1024, 512, 256, 128, 64, 32, 16):
        if H % cand == 0 and cand * width * 4 <= 1536 * 1024:
            return cand
    return H


def _add_sibling(g, got, c, me, name):
    _, H, width = got.shape
    tb = _rs_rows(H, width)
    nb = H // tb

    def body(sp_ref, g_ref, got_ref, sb_ref, sf_ref):
        p = pl.program_id(1)
        s = g_ref[0] + got_ref[0]
        sb_ref[0] = s.astype(BF16)

        @pl.when(p == sp_ref[1])
        def _():
            sf_ref[...] = s

    grid_spec = pltpu.PrefetchScalarGridSpec(
        num_scalar_prefetch=1, grid=(nb, N_CHIPS),
        in_specs=[pl.BlockSpec((1, tb, width), lambda i, p, sp: (p, sp[0] * nb + i, 0)),
                  pl.BlockSpec((1, tb, width), lambda i, p, sp: (p, i, 0))],
        out_specs=[pl.BlockSpec((1, tb, width), lambda i, p, sp: (p, i, 0)),
                   pl.BlockSpec((tb, width), lambda i, p, sp: (i, 0))])
    return pl.pallas_call(
        body, name=name, grid_spec=grid_spec,
        out_shape=[jax.ShapeDtypeStruct((N_CHIPS, H, width), BF16), jax.ShapeDtypeStruct((H, width), F32)],
        compiler_params=_cp(("arbitrary", "arbitrary")),
    )(jnp.stack([c, me]).astype(jnp.int32), g, got)


def _add_chips(sf, got, others_and_c, name):
    H, width = sf.shape
    tb = _rs_rows(H, width)
    nb = H // tb

    def body(sp_ref, sf_ref, r1_ref, r2_ref, r3_ref, out_ref):
        out_ref[...] = ((sf_ref[...] + r1_ref[0].astype(F32)) + r2_ref[0].astype(F32)) + r3_ref[0].astype(F32)

    def slot(k):
        return pl.BlockSpec((1, tb, width), lambda i, sp: (sp[k], i, 0))

    blk = pl.BlockSpec((tb, width), lambda i, sp: (i, 0))
    grid_spec = pltpu.PrefetchScalarGridSpec(
        num_scalar_prefetch=1, grid=(nb,), in_specs=[blk, slot(0), slot(1), slot(2)],
        out_specs=pl.BlockSpec((tb, width), lambda i, sp: (sp[3] * nb + i, 0)))
    return pl.pallas_call(
        body, name=name, grid_spec=grid_spec,
        out_shape=jax.ShapeDtypeStruct((2 * H, width), F32),
        compiler_params=_cp(("arbitrary",)),
    )(others_and_c.astype(jnp.int32), sf, got, got, got)


def _reduce_scatter(bufs, c, me, tag):
    n = len(bufs)
    gots = _swap_other_half(bufs, f"{tag}_to_sibling")
    sums = [_add_sibling(bufs[k], gots[k], c, me, f"{tag}_add_sibling{k}") for k in range(n)]
    landed = _send_chip_partials([s[0] for s in sums], f"{tag}_to_chips")
    others_and_c = jnp.stack([jnp.where(me <= k, k + 1, k) for k in range(N_CHIPS - 1)] + [c])
    halves = [_add_chips(sums[k][1], landed[k], others_and_c, f"{tag}_add_chips{k}") for k in range(n)]
    return _join_halves(halves, f"{tag}_join_halves")


REPLICATED = ("norm_mix", "norm_ffn", "gla_b_gate_f", "gla_b_gate_b", "gla_norm", "attn_q_norm", "attn_k_norm",
              "ffn_b_conv")


PIECE_ROWS = 16


def _piece_rows(shape):
    n = 1
    for s in shape:
        n *= s
    rows = n // LANES
    return rows, -(-rows // PIECE_ROWS) * PIECE_ROWS


def _pack(pieces, dtype, row_multiple):
    flat = []
    for p in pieces:
        rows, padded = _piece_rows(p.shape)
        flat.append(jnp.pad(p.astype(dtype).reshape(rows, LANES), ((0, padded - rows), (0, 0))))
    rows = sum(f.shape[0] for f in flat)
    padded = -(-rows // row_multiple) * row_multiple
    if padded > rows:
        flat.append(jnp.zeros((padded - rows, LANES), dtype))
    return jnp.concatenate(flat, axis=0)


def _unpack(buf, shapes):
    out, r = [], 0
    for shp in shapes:
        rows, padded = _piece_rows(shp)
        out.append(buf[r:r + rows].reshape(shp))
        r += padded
    return out


SMALL_SHARDED = ("gla_w_gate_up_f", "gla_w_gate_up_b", "ffn_w_conv")


def _own_slot(shard2d, me):
    return lax.dynamic_update_index_in_dim(jnp.zeros((N_CHIPS,) + shard2d.shape, shard2d.dtype), shard2d, me, 0)


def _layer_small(w, l):
    j = l // 2
    if l % 2 == 0:
        return [w["gla_w_gate_up_f"][j], w["gla_w_gate_up_b"][j], w["ffn_w_conv"][l]]
    return [w["ffn_w_conv"][l]]


def _layer_weight_bufs(w, l, me):
    j = l // 2
    mixer = ("gla_w_in", "gla_w_out") if l % 2 == 0 else ("attn_w_qkv", "attn_w_out")
    bufs = [_own_slot(w["ffn_w_up"][l].astype(BF16), me), _own_slot(w["ffn_w_down"][l].astype(BF16), me)]
    bufs += [_own_slot(w[n][j].astype(BF16), me) for n in mixer]
    bufs.append(_own_slot(_pack(_layer_small(w, l), F32, 32), me))
    return bufs


def _layer_weights(w, l, got):
    up, down, mix_in, mix_out, small = got
    rows = lambda t: t.reshape(-1, t.shape[2])
    cols = lambda t: jnp.concatenate([t[p] for p in range(N_CHIPS)], axis=1)
    shapes = [t.shape for t in _layer_small(w, l)]
    parts = [_unpack(small[p], shapes) for p in range(N_CHIPS)]
    full_small = [jnp.concatenate([parts[p][k] for p in range(N_CHIPS)], axis=-1) for k in range(len(shapes))]
    out = dict(up=up, up_full=cols(up), down=rows(down), conv=full_small[-1])
    if l % 2 == 0:
        out.update(gla_in=jnp.pad(cols(mix_in), ((0, 0), (0, GLA_IN_PAD - GLA_IN))), gla_out=rows(mix_out),
                   gate_f=full_small[0], gate_b=full_small[1])
    else:
        out.update(qkv=mix_in, attn_out=rows(mix_out))
    return out


HBM = pl.BlockSpec(memory_space=pltpu.HBM)
SEM = pl.BlockSpec(memory_space=pltpu.SEMAPHORE)
SIDE_EFFECT = pltpu.SideEffectType.DATAFLOW_SIDE_EFFECTING


def _gather_start(bufs, name):
    n = len(bufs)
    halves = [b.shape[1] // 2 for b in bufs]

    def body(*refs):
        send_sems, recv_sems = refs[n:2 * n], refs[2 * n:3 * n]
        outs, token = refs[3 * n:4 * n], refs[4 * n]
        x, y, c = _place()
        me = 2 * x + y
        for k in range(n):
            blk = outs[k].at[me, _half_rows(c, halves[k])]
            for px, py in _other_chips(x, y):
                pltpu.make_async_remote_copy(src_ref=blk, dst_ref=blk, send_sem=send_sems[k], recv_sem=recv_sems[k],
                                             device_id=(px, py, c), device_id_type=MESH).start()
        token[...] = jnp.zeros_like(token)

    res = pl.pallas_call(
        body, name=name,
        in_specs=[HBM] * n,
        out_specs=[SEM] * (2 * n) + [HBM] * n + [pl.BlockSpec(memory_space=pltpu.VMEM)],
        out_shape=[pltpu.SemaphoreType.DMA(())] * (2 * n) + [pltpu.HBM(b.shape, b.dtype) for b in bufs]
        + [jax.ShapeDtypeStruct((8, LANES), F32)],
        input_output_aliases={k: 2 * n + k for k in range(n)},
        compiler_params=pltpu.CompilerParams(has_side_effects=SIDE_EFFECT),
    )(*[pltpu.with_memory_space_constraint(b, pltpu.HBM) for b in bufs])
    return res[:n], res[n:2 * n], res[2 * n:3 * n], res[3 * n]


def _gather_wait(send_sems, recv_sems, thru, after, name):
    n = len(thru)
    halves = [b.shape[1] // 2 for b in thru]

    def body(*refs):
        ss, rs = refs[n:2 * n], refs[2 * n:3 * n]
        outs = refs[3 * n + 1:]
        x, y, c = _place()
        for k in range(n):
            three = outs[k].at[pl.ds(0, N_CHIPS - 1), _half_rows(c, halves[k])]
            cp = pltpu.make_async_remote_copy(src_ref=three, dst_ref=three, send_sem=ss[k], recv_sem=rs[k],
                                              device_id=(x, y, c), device_id_type=MESH)
            cp.wait_send()
            cp.wait_recv()

    return pl.pallas_call(
        body, name=name,
        in_specs=[HBM] * n + [SEM] * (2 * n) + [ANY],
        out_specs=[HBM] * n,
        out_shape=[pltpu.HBM(b.shape, b.dtype) for b in thru],
        input_output_aliases={k: k for k in range(n)},
        compiler_params=pltpu.CompilerParams(has_side_effects=SIDE_EFFECT),
    )(*thru, *send_sems, *recv_sems, after)


def _pass_to_sibling(bufs, name):
    n = len(bufs)
    halves = [b.shape[1] // 2 for b in bufs]

    def body(*refs):
        outs = refs[n:2 * n]
        send_sems, recv_sems = refs[2 * n:]
        x, y, c = _place()
        chips = _other_chips(x, y)

        def copy(k, j, core):
            px, py = chips[j]
            blk = outs[k].at[2 * px + py, _half_rows(core, halves[k])]
            return pltpu.make_async_remote_copy(src_ref=blk, dst_ref=blk, send_sem=send_sems.at[3 * k + j],
                                                recv_sem=recv_sems.at[3 * k + j], device_id=(x, y, 1 - c),
                                                device_id_type=MESH)

        sends = [copy(k, j, c) for k in range(n) for j in range(3)]
        for cp in sends:
            cp.start()
        for k in range(n):
            for j in range(3):
                copy(k, j, 1 - c).wait_recv()
        for cp in sends:
            cp.wait_send()

    return pl.pallas_call(
        body, name=name,
        in_specs=[ANY] * n, out_specs=[ANY] * n,
        out_shape=[jax.ShapeDtypeStruct(b.shape, b.dtype) for b in bufs],
        input_output_aliases={k: k for k in range(n)},
        scratch_shapes=[pltpu.SemaphoreType.DMA((3 * n,)), pltpu.SemaphoreType.DMA((3 * n,))],
    )(*bufs)


def _rope_tables(S):
    rows = S // GRID_W
    row_idx = jnp.repeat(jnp.arange(rows, dtype=F32), GRID_W)
    col_idx = jnp.tile(jnp.arange(GRID_W, dtype=F32), rows)
    pairs = ATT_HD // 4
    inv_freq = ROPE_THETA ** (-jnp.arange(pairs, dtype=F32) / pairs)
    ang = jnp.concatenate([row_idx[:, None] * inv_freq, col_idx[:, None] * inv_freq], axis=-1)
    cos, sin = jnp.cos(ang), jnp.sin(ang)
    return jnp.concatenate([cos, cos], axis=-1), jnp.concatenate([-sin, sin], axis=-1)


def _gate_rows(w, first_row):
    return jnp.zeros((LANES, GLA_KEY), F32).at[first_row:first_row + GLA_RANK].set(w.astype(F32))


def _local_step(x, target, weights_of, P):
    S = x.shape[0]
    rc, rs = _rope_tables(S)
    row = lambda a: a.reshape(1, -1)
    saved = []
    for i in range(DEPTH):
        j = i // 2
        W = weights_of(i, x)
        nm = row(P["norm_mix"][i])
        h1 = _rmsnorm_fwd(x, nm, f"norm_mix_fwd{i}")
        if i % 2 == 0:
            wgf = _gate_rows(W["gate_f"], 0)
            wgb = _gate_rows(W["gate_b"], GLA_RANK)
            bgf, bgb = row(P["gla_b_gate_f"][j]), row(P["gla_b_gate_b"][j])
            gn = row(P["gla_norm"][j])
            proj = _matmul_rows(h1, W["gla_in"], f"gla_in{i}")
            laf, lab = _gla_gate_fwd(proj, wgf, bgf, wgb, bgb, f"gla_gate_fwd{i}")
            of, stf = _gla_scan_fwd(proj, laf, False, f"gla_scan_f_fwd{i}")
            ob, stb = _gla_scan_fwd(proj, lab, True, f"gla_scan_b_fwd{i}")
            z = _gla_out_fwd(of, ob, proj, gn, f"gla_out_fwd{i}")
            xm = _matmul_rows(z, W["gla_out"], f"gla_outproj{i}", res=x)
            mix = dict(proj=proj, laf=laf, lab=lab, of=of, ob=ob, stf=stf, stb=stb, z=z, wgf=wgf, wgb=wgb)
        else:
            proj = _matmul_rows(h1, W["qkv"], f"attn_qkv{i}", w_layer=0)
            qn, kn = row(P["attn_q_norm"][j]), row(P["attn_k_norm"][j])
            qk, vb, kt, vt = _qk_prep_fwd(proj, qn, kn, rc, rs, f"qk_prep_fwd{i}")
            o, lse = _attn_fwd(qk, vt, f"attn_fwd{i}")
            xm = _matmul_rows(o, W["attn_out"], f"attn_outproj{i}", res=x)
            mix = dict(proj=proj, qk=qk, vb=vb, kt=kt, o=o, lse=lse)
        h2 = _rmsnorm_fwd(xm, row(P["norm_ffn"][i]), f"norm_ffn_fwd{i}")
        a, uv, ug = _ffn_mid_fwd(h2, W["up_full"], W["conv"], row(P["ffn_b_conv"][i]), f"ffn_mid_fwd{i}")
        xo = _matmul_rows(a, W["down"], f"ffn_down{i}", res=xm)
        saved.append(dict(x=x, h1=h1, xm=xm, h2=h2, uv=uv, ug=ug, mix=mix, W=W))
        x = xo

    dx, dxb, loss = _loss_grad(x, target, "loss")

    G = {n: [None] * (DEPTH if n.startswith(("norm", "ffn")) else DEPTH // 2)
         for n in ("gla_w_in",) + SMALL_SHARDED + REPLICATED}
    G.update(up=None, down=None, out=None, qkv=None)
    for i in reversed(range(DEPTH)):
        j = i // 2
        sv = saved[i]
        mix = sv["mix"]
        W = sv["W"]
        duv, dug, a, gwv, gwg = _ffn_mid_bwd(dxb, W["down"], sv["uv"], sv["ug"], W["conv"],
                                             row(P["ffn_b_conv"][i]), f"ffn_mid_bwd{i}")
        G["down"] = _matmul(a, dxb, 0, 0, f"ffn_down_wgrad{i}", out_chips=("rows", i, DEPTH), into=G["down"])
        G["up"] = _matmul(sv["h2"], (duv, dug), 0, 0, f"ffn_up_wgrad{i}", out_chips=("cols", i, DEPTH),
                          into=G["up"])
        G["ffn_w_conv"][i] = jnp.concatenate([gwv[:3], gwg[:3]], axis=1)
        G["ffn_b_conv"][i] = jnp.concatenate([gwv[3], gwg[3]], axis=0)
        dxm, dxmb, dn = _dgrad_norm((duv, dug), W["up"], sv["xm"], row(P["norm_ffn"][i]), dx, f"ffn_up_dgrad{i}",
                                    w_layer=0)
        G["norm_ffn"][i] = dn[0]
        if i % 2 == 0:
            proj = mix["proj"]
            bgf, bgb = row(P["gla_b_gate_f"][j]), row(P["gla_b_gate_b"][j])
            gn = row(P["gla_norm"][j])
            dz = _matmul_rows(dxmb, W["gla_out"], f"gla_outproj_dgrad{i}", transposed=True)
            G["out"] = _matmul(mix["z"], dxmb, 0, 0, f"gla_outproj_wgrad{i}", out_chips=("rows", i, DEPTH),
                               into=G["out"])
            do, dg, dgn = _gla_out_bwd(dz, mix["of"], mix["ob"], proj, gn, f"gla_out_bwd{i}")
            G["gla_norm"][j] = dgn[0]
            dqf, dkf, dvf, dlaf = _gla_scan_bwd(do, proj, mix["laf"], mix["stf"], False, f"gla_scan_f_bwd{i}")
            dqb, dkb, dvb, dlab = _gla_scan_bwd(do, proj, mix["lab"], mix["stb"], True, f"gla_scan_b_bwd{i}")
            dr, dwf, dbf, dwb, dbb = _gla_gate_bwd(dlaf, dlab, proj, mix["wgf"], bgf, mix["wgb"], bgb,
                                                   f"gla_gate_bwd{i}")
            G["gla_w_gate_up_f"][j] = dwf[:GLA_RANK]
            G["gla_w_gate_up_b"][j] = dwb[GLA_RANK:2 * GLA_RANK]
            G["gla_b_gate_f"][j] = dbf[0]
            G["gla_b_gate_b"][j] = dbb[0]
            dproj = jnp.concatenate([dqf + dqb, dkf + dkb, dvf + dvb, dg, dr], axis=1).astype(BF16)
            G["gla_w_in"][j] = _matmul(sv["h1"], dproj, 0, 0, f"gla_in_wgrad{i}")
            dx, dxb, dn = _dgrad_norm(dproj, W["gla_in"], sv["x"], row(P["norm_mix"][i]), dxm, f"mix_in_dgrad{i}")
        else:
            proj = mix["proj"]
            qn, kn = row(P["attn_q_norm"][j]), row(P["attn_k_norm"][j])
            do = _matmul_rows(dxmb, W["attn_out"], f"attn_outproj_dgrad{i}", transposed=True)
            G["out"] = _matmul(mix["o"], dxmb, 0, 0, f"attn_outproj_wgrad{i}", out_chips=("rows", i, DEPTH),
                               into=G["out"])
            dq, dk, dv = _attn_bwd(do, mix["o"], mix["lse"], mix["qk"], mix["vb"], mix["kt"], f"attn_bwd{i}")
            dqk = jnp.concatenate([dq, dk], axis=1)
            dpqk, dqn, dkn = _qk_prep_bwd(dqk, proj, qn, kn, rc, rs, f"qk_prep_bwd{i}")
            G["attn_q_norm"][j] = dqn[0]
            G["attn_k_norm"][j] = dkn[0]
            dproj = jnp.concatenate([dpqk, dv], axis=1).astype(BF16)
            G["qkv"] = _matmul(sv["h1"], dproj, 0, 0, f"attn_qkv_wgrad{i}", out_chips=("cols", j, DEPTH // 2),
                               into=G["qkv"])
            dx, dxb, dn = _dgrad_norm(dproj, W["qkv"], sv["x"], row(P["norm_mix"][i]), dxm, f"mix_in_dgrad{i}",
                                      w_layer=0)
        G["norm_mix"][i] = dn[0]
    return loss, dx, G


def kernel(x, norm_mix, norm_ffn, gla_w_in, gla_w_gate_up_f, gla_b_gate_f, gla_w_gate_up_b, gla_b_gate_b, gla_norm, gla_w_out, attn_w_qkv, attn_q_norm, attn_k_norm, attn_w_out, ffn_w_up, ffn_w_conv, ffn_b_conv, ffn_w_down, loss_target, m_norm_mix, m_norm_ffn, m_gla_w_in, m_gla_w_gate_up_f, m_gla_b_gate_f, m_gla_w_gate_up_b, m_gla_b_gate_b, m_gla_norm, m_gla_w_out, m_attn_w_qkv, m_attn_q_norm, m_attn_k_norm, m_attn_w_out, m_ffn_w_up, m_ffn_w_conv, m_ffn_b_conv, m_ffn_w_down, v_norm_mix, v_norm_ffn, v_gla_w_in, v_gla_w_gate_up_f, v_gla_b_gate_f, v_gla_w_gate_up_b, v_gla_b_gate_b, v_gla_norm, v_gla_w_out, v_attn_w_qkv, v_attn_q_norm, v_attn_k_norm, v_attn_w_out, v_ffn_w_up, v_ffn_w_conv, v_ffn_b_conv, v_ffn_w_down):
    names = ("norm_mix", "norm_ffn", "gla_w_in", "gla_w_gate_up_f", "gla_b_gate_f", "gla_w_gate_up_b",
             "gla_b_gate_b", "gla_norm", "gla_w_out", "attn_w_qkv", "attn_q_norm", "attn_k_norm", "attn_w_out",
             "ffn_w_up", "ffn_w_conv", "ffn_b_conv", "ffn_w_down")
    w = dict(zip(names, (norm_mix, norm_ffn, gla_w_in, gla_w_gate_up_f, gla_b_gate_f, gla_w_gate_up_b,
                         gla_b_gate_b, gla_norm, gla_w_out, attn_w_qkv, attn_q_norm, attn_k_norm, attn_w_out,
                         ffn_w_up, ffn_w_conv, ffn_b_conv, ffn_w_down)))
    m = dict(zip(names, (m_norm_mix, m_norm_ffn, m_gla_w_in, m_gla_w_gate_up_f, m_gla_b_gate_f,
                         m_gla_w_gate_up_b, m_gla_b_gate_b, m_gla_norm, m_gla_w_out, m_attn_w_qkv, m_attn_q_norm,
                         m_attn_k_norm, m_attn_w_out, m_ffn_w_up, m_ffn_w_conv, m_ffn_b_conv, m_ffn_w_down)))
    v = dict(zip(names, (v_norm_mix, v_norm_ffn, v_gla_w_in, v_gla_w_gate_up_f, v_gla_b_gate_f,
                         v_gla_w_gate_up_b, v_gla_b_gate_b, v_gla_norm, v_gla_w_out, v_attn_w_qkv, v_attn_q_norm,
                         v_attn_k_norm, v_attn_w_out, v_ffn_w_up, v_ffn_w_conv, v_ffn_b_conv, v_ffn_w_down)))
    px, py, pc = _place()
    me = 2 * px + py

    started = [_gather_start(_layer_weight_bufs(w, l, me), f"gather_start{l}") for l in range(DEPTH)]
    all_started = jnp.stack([s[3] for s in started])

    def weights_of(l, after):
        send_sems, recv_sems, thru, _ = started[l]
        landed = _gather_wait(send_sems, recv_sems, thru, all_started if l == 0 else after, f"gather_wait{l}")
        return _layer_weights(w, l, _pass_to_sibling(landed, f"gather_pass{l}"))

    P = {n: w[n] for n in REPLICATED}

    loss_part, dx, grads = _local_step(x[0], loss_target[0], weights_of, P)

    win_width = w["gla_w_in"].shape[2]
    win = jnp.stack([jnp.concatenate([g[:, p * win_width:(p + 1) * win_width] for g in grads["gla_w_in"]], axis=0)
                     for p in range(N_CHIPS)])
    small_g = []
    for p in range(N_CHIPS):
        cut = [lax.slice_in_dim(jnp.stack(grads[n]), p * w[n].shape[2], (p + 1) * w[n].shape[2], axis=2)
               for n in SMALL_SHARDED]
        small_g.append(_pack(cut, F32, 256))
    bufs = [grads["up"], grads["down"], grads["out"], grads["qkv"], win, jnp.stack(small_g)]
    up, down, out, qkv, win, small_g = _reduce_scatter(bufs, pc, me, "grads")
    out = out.reshape(DEPTH // 2, 2, -1, out.shape[1])
    gsh = dict(ffn_w_up=up, ffn_w_down=down, gla_w_out=out[:, 0], attn_w_out=out[:, 1], attn_w_qkv=qkv,
               gla_w_in=win)
    gsh = {n: g.reshape(w[n].shape) for n, g in gsh.items()}
    gsh.update(dict(zip(SMALL_SHARDED, _unpack(small_g, [w[n].shape for n in SMALL_SHARDED]))))

    small = _pack([jnp.stack(grads[n]) for n in REPLICATED] + [loss_part], F32, 16)
    small_sum = _allreduce_small(small, "small_allreduce")
    parts = _unpack(small_sum, [w[n].shape for n in REPLICATED] + [(1, LANES)])
    gsh.update(dict(zip(REPLICATED, parts[:-1])))
    loss = parts[-1][0, 0]

    delta, new_m, new_v = {}, {}, {}
    for n in names:
        shp = w[n].shape
        two_d = (-1, shp[-1])
        d, nm, nv = _adamw(w[n].reshape(two_d), gsh[n].reshape(two_d), m[n].reshape(two_d), v[n].reshape(two_d),
                           f"adamw_{n}")
        delta[n], new_m[n], new_v[n] = d.reshape(shp), nm.reshape(shp), nv.reshape(shp)

    return (loss, dx[None], *[gsh[n] for n in names], *[delta[n] for n in names],
            *[new_m[n] for n in names], *[new_v[n] for n in names])
```

```python
import jax
import jax.numpy as jnp
from jax import lax
from jax.experimental import pallas as pl
from jax.experimental.pallas import tpu as pltpu

F32 = jnp.float32
BF16 = jnp.bfloat16
MESH = pl.DeviceIdType.MESH
HIGHEST = lax.Precision.HIGHEST

D_MODEL = 1024
DEPTH = 4
GRID_W = 64
NORM_EPS = 1e-6
GLA_HEADS = 4
GLA_DK = 128
GLA_DV = 256
GLA_KEY = GLA_HEADS * GLA_DK
GLA_VAL = GLA_HEADS * GLA_DV
GLA_RANK = 16
GLA_CHUNK = 64
GLA_GATE_NORMALIZER = 16.0
GLA_IN = 2 * GLA_KEY + 2 * GLA_VAL + 2 * GLA_RANK
GLA_IN_PAD = 3200
GLA_R_BLOCK = (2 * GLA_KEY + 2 * GLA_VAL) // 128
ATT_HD = 128
ATT_QH = 8
ATT_KVH = 2
ATT_GROUP = ATT_QH // ATT_KVH
ATT_QKV = (ATT_QH + 2 * ATT_KVH) * ATT_HD
ROPE_THETA = 10000.0
D_FF = 2816
ADAM_LR = 0.001
ADAM_B1 = 0.9
ADAM_B2 = 0.999
ADAM_EPS = 1e-08
ADAM_WD = 0.01
ADAM_STEP = 10

N_CHIPS = 4
LANES = 128
VMEM_LIMIT = 56 * 1024 * 1024


def _cp(sem):
    return pltpu.CompilerParams(dimension_semantics=sem, vmem_limit_bytes=VMEM_LIMIT)


def _pick(n, cands):
    for c in cands:
        if n % c == 0:
            return c
    return n


def _dg(a, b, ca, cb):
    return lax.dot_general(a, b, (((ca,), (cb,)), ((), ())), preferred_element_type=F32)


def _sigmoid(x):
    return 0.5 * jnp.tanh(0.5 * x) + 0.5


def _rmsnorm_fwd(x, w, name):
    S, D = x.shape
    tm = _pick(S, (512, 256))

    def body(x_ref, w_ref, h_ref):
        xv = x_ref[...]
        r = lax.rsqrt(jnp.mean(xv * xv, axis=-1, keepdims=True) + NORM_EPS)
        h_ref[...] = (xv * r * w_ref[...]).astype(BF16)

    return pl.pallas_call(
        body, name=name, grid=(S // tm,),
        in_specs=[pl.BlockSpec((tm, D), lambda i: (i, 0)), pl.BlockSpec((1, D), lambda i: (0, 0))],
        out_specs=pl.BlockSpec((tm, D), lambda i: (i, 0)),
        out_shape=jax.ShapeDtypeStruct((S, D), BF16),
        compiler_params=_cp(("parallel",)),
    )(x, w)


def _loss_grad(y, t, name):
    S, D = y.shape
    tm = _pick(S, (512, 256))

    def body(y_ref, t_ref, dy_ref, dyb_ref, loss_ref):
        i = pl.program_id(0)
        d = y_ref[...] - t_ref[...]
        dy = d * (1.0 / D)
        dy_ref[...] = dy
        dyb_ref[...] = dy.astype(BF16)
        sq = jnp.sum(jnp.sum(d * d, axis=1, keepdims=True), axis=0, keepdims=True)
        part = jnp.broadcast_to(sq * (0.5 / D), (1, LANES))

        @pl.when(i == 0)
        def _():
            loss_ref[...] = part

        @pl.when(i > 0)
        def _():
            loss_ref[...] += part

    return pl.pallas_call(
        body, name=name, grid=(S // tm,),
        in_specs=[pl.BlockSpec((tm, D), lambda i: (i, 0)), pl.BlockSpec((tm, D), lambda i: (i, 0))],
        out_specs=[pl.BlockSpec((tm, D), lambda i: (i, 0)), pl.BlockSpec((tm, D), lambda i: (i, 0)),
                   pl.BlockSpec((1, LANES), lambda i: (0, 0))],
        out_shape=[jax.ShapeDtypeStruct((S, D), F32), jax.ShapeDtypeStruct((S, D), BF16),
                   jax.ShapeDtypeStruct((1, LANES), F32)],
        compiler_params=_cp(("arbitrary",)),
    )(y, t)


def _matmul(a, b, ca, cb, name, res=None, out_dtype=F32, out_chips=None, into=None):
    M, K = a.shape[1 - ca], a.shape[ca]
    pair = isinstance(b, (tuple, list))
    if pair:
        assert cb == 0 and b[0].shape == b[1].shape and b[0].shape[0] == K
        N = 2 * b[0].shape[1]
    else:
        assert b.shape[cb] == K
        N = b.shape[1 - cb]
    how = out_chips[0] if out_chips else None
    tm = M if how == "rows" else _pick(M, (1024, 1408, 512, 256, 128))
    if how == "cols":
        tn = N // N_CHIPS
    else:
        tn = _pick(N // 2 if pair else N, (1024, 1408, 768, 640, 512, 256, 128))
    tk = _pick(K, (512, 1408, 256, 128))
    nk = K // tk
    n0 = (N // 2) // tn
    if ca == 1:
        a_spec = pl.BlockSpec((tm, tk), lambda i, j, k: (i, k))
    else:
        a_spec = pl.BlockSpec((tk, tm), lambda i, j, k: (k, i))
    if pair:
        b_specs = [pl.BlockSpec((tk, tn), lambda i, j, k: (k, jnp.minimum(j, n0 - 1))),
                   pl.BlockSpec((tk, tn), lambda i, j, k: (k, jnp.maximum(j - n0, 0)))]
    elif cb == 0:
        b_specs = [pl.BlockSpec((tk, tn), lambda i, j, k: (k, j))]
    else:
        b_specs = [pl.BlockSpec((tn, tk), lambda i, j, k: (j, k))]
    if how == "cols":
        _, layer, layers = out_chips
        o_spec = pl.BlockSpec((None, tm, tn), lambda i, j, k: (j, layer * (M // tm) + i, 0))
        out_shape = jax.ShapeDtypeStruct((N_CHIPS, layers * M, tn), out_dtype)
    elif how == "rows":
        _, layer, layers = out_chips
        o_spec = pl.BlockSpec((N_CHIPS, M // N_CHIPS, tn), lambda i, j, k: (0, layer, j))
        out_shape = jax.ShapeDtypeStruct((N_CHIPS, layers * M // N_CHIPS, N), out_dtype)
    else:
        o_spec = pl.BlockSpec((tm, tn), lambda i, j, k: (i, j))
        out_shape = jax.ShapeDtypeStruct((M, N), out_dtype)
    has_res = res is not None
    nb = len(b_specs)

    def body(*refs):
        a_ref, b_refs = refs[0], refs[1:1 + nb]
        r_ref = refs[1 + nb] if has_res else None
        o_ref, acc = refs[-2], refs[-1]
        j = pl.program_id(1)
        k = pl.program_id(2)

        @pl.when(k == 0)
        def _():
            acc[...] = jnp.zeros_like(acc)

        av = a_ref[...].astype(BF16)
        if pair:
            @pl.when(j < n0)
            def _():
                acc[...] += _dg(av, b_refs[0][...].astype(BF16), ca, cb)

            @pl.when(j >= n0)
            def _():
                acc[...] += _dg(av, b_refs[1][...].astype(BF16), ca, cb)
        else:
            acc[...] += _dg(av, b_refs[0][...].astype(BF16), ca, cb)

        @pl.when(k == nk - 1)
        def _():
            v = acc[...]
            if has_res:
                v = v + r_ref[...]
            if how == "rows":
                rows = M // N_CHIPS
                for p in range(N_CHIPS):
                    o_ref[p] = v[p * rows:(p + 1) * rows, :].astype(out_dtype)
            else:
                o_ref[...] = v.astype(out_dtype)

    in_specs = [a_spec] + b_specs + ([o_spec] if has_res else [])
    args = (a,) + (tuple(b) if pair else (b,)) + ((res,) if has_res else ())
    aliases = {}
    if into is not None:
        assert into.shape == out_shape.shape
        in_specs.append(ANY)
        aliases = {len(args): 0}
        args = args + (into,)
        inner = body

        def body(*refs):
            inner(*refs[:len(args) - 1], *refs[len(args):])

    return pl.pallas_call(
        body, name=name, grid=(M // tm, N // tn, nk),
        in_specs=in_specs, out_specs=o_spec, out_shape=out_shape,
        input_output_aliases=aliases,
        scratch_shapes=[pltpu.VMEM((tm, tn), F32)],
        compiler_params=_cp(("parallel", "parallel", "arbitrary")),
    )(*args)


def _matmul_rows(a, w, name, res=None, w_layer=None, transposed=False):
    M, K = a.shape
    if w_layer is not None:
        cw = w.shape[2]
        N = N_CHIPS * cw
        w_spec = pl.BlockSpec((N_CHIPS, K, cw), lambda i: (0, w_layer, 0))
    else:
        N = w.shape[0] if transposed else w.shape[1]
        assert w.shape[1 if transposed else 0] == K
        w_spec = pl.BlockSpec(w.shape, lambda i: (0, 0))
    tm = _pick(M, (512, 256, 128))
    has_res = res is not None

    def body(*refs):
        a_ref, w_ref = refs[0], refs[1]
        r_ref = refs[2] if has_res else None
        o_ref = refs[-1]
        av = a_ref[...].astype(BF16)
        if w_layer is not None:
            for p in range(N_CHIPS):
                o_ref[:, pl.ds(p * cw, cw)] = jnp.dot(av, w_ref[p], preferred_element_type=F32)
        else:
            v = _dg(av, w_ref[...], 1, 1 if transposed else 0)
            o_ref[...] = v + r_ref[...] if has_res else v

    row = pl.BlockSpec((tm, N), lambda i: (i, 0))
    return pl.pallas_call(
        body, name=name, grid=(M // tm,),
        in_specs=[pl.BlockSpec((tm, K), lambda i: (i, 0)), w_spec] + ([row] if has_res else []),
        out_specs=row, out_shape=jax.ShapeDtypeStruct((M, N), F32),
        compiler_params=_cp(("parallel",)),
    )(*((a, w) + ((res,) if has_res else ())))


def _dgrad_norm(dy, w, x, wn, dres, name, w_layer=None):
    pair = isinstance(dy, (tuple, list))
    M = dy[0].shape[0] if pair else dy.shape[0]
    Kp = 2 * dy[0].shape[1] if pair else dy.shape[1]
    D = x.shape[1]
    if w_layer is not None:
        cw = w.shape[2]
        assert N_CHIPS * cw == Kp and w.shape[1] % D == 0
        w_spec = pl.BlockSpec((N_CHIPS, D, cw), lambda i: (0, w_layer, 0))
    else:
        assert w.shape == (D, Kp)
        w_spec = pl.BlockSpec((D, Kp), lambda i: (0, 0))
    tm = _pick(M, (256, 128))
    width = Kp // 2 if pair else Kp
    dy_specs = [pl.BlockSpec((tm, width), lambda i: (i, 0))] * (2 if pair else 1)
    nd = len(dy_specs)

    def body(*refs):
        dy_refs = refs[:nd]
        w_ref, x_ref, wn_ref, dres_ref, dx_ref, dxb_ref, dwn_ref = refs[nd:]
        i = pl.program_id(0)
        if w_layer is not None:
            dh = None
            for p in range(N_CHIPS):
                src, off = divmod(p * cw, width)
                part = _dg(dy_refs[src][:, pl.ds(off, cw)], w_ref[p], 1, 1)
                dh = part if dh is None else dh + part
        else:
            dh = _dg(dy_refs[0][...], w_ref[...], 1, 1)
        xv = x_ref[...]
        r = lax.rsqrt(jnp.mean(xv * xv, axis=-1, keepdims=True) + NORM_EPS)
        yv = xv * r
        dyv = dh * wn_ref[...]
        dxv = r * (dyv - yv * jnp.mean(dyv * yv, axis=-1, keepdims=True)) + dres_ref[...]
        dx_ref[...] = dxv
        dxb_ref[...] = dxv.astype(BF16)
        part = jnp.sum(dh * yv, axis=0, keepdims=True)

        @pl.when(i == 0)
        def _():
            dwn_ref[...] = part

        @pl.when(i > 0)
        def _():
            dwn_ref[...] += part

    row = pl.BlockSpec((tm, D), lambda i: (i, 0))
    one = pl.BlockSpec((1, D), lambda i: (0, 0))
    return pl.pallas_call(
        body, name=name, grid=(M // tm,),
        in_specs=dy_specs + [w_spec, row, one, row],
        out_specs=[row, row, one],
        out_shape=[jax.ShapeDtypeStruct((M, D), F32), jax.ShapeDtypeStruct((M, D), BF16),
                   jax.ShapeDtypeStruct((1, D), F32)],
        compiler_params=_cp(("arbitrary",)),
    )(*(tuple(dy) if pair else (dy,)), w, x, wn, dres)


FFN_TN_FWD = 256
FFN_TN_BWD = 128
FFN_ROWS = 256
PAD = 8


def _conv3(pad_ref, w, r0, tr):
    um = pad_ref[pl.ds(PAD - 1 + r0, tr), :]
    uc = pad_ref[pl.ds(PAD + r0, tr), :]
    up = pad_ref[pl.ds(PAD + 1 + r0, tr), :]
    return w[0:1, :] * um + w[1:2, :] * uc + w[2:3, :] * up, (um, uc, up)


def _zero_pads(pad_ref, S, tn):
    pad_ref[pl.ds(0, PAD), :] = jnp.zeros((PAD, tn), F32)
    pad_ref[pl.ds(PAD + S, PAD), :] = jnp.zeros((PAD, tn), F32)


def _ffn_mid_fwd(h, wup, wconv, bconv, name):
    S, D = h.shape
    F = wup.shape[1] // 2
    tn = FFN_TN_FWD
    nb = F // tn
    tr = min(FFN_ROWS, S)

    def body(h_ref, wv_ref, wg_ref, cv_ref, cg_ref, bv_ref, bg_ref, a_ref, uv_ref, ug_ref):
        _zero_pads(uv_ref, S, tn)
        _zero_pads(ug_ref, S, tn)
        hv = h_ref[...]
        uv_ref[pl.ds(PAD, S), :] = jnp.dot(hv, wv_ref[...], preferred_element_type=F32)
        ug_ref[pl.ds(PAD, S), :] = jnp.dot(hv, wg_ref[...], preferred_element_type=F32)
        cwv, cwg, bv, bg = cv_ref[...], cg_ref[...], bv_ref[...], bg_ref[...]
        for r0 in range(0, S, tr):
            cv = _conv3(uv_ref, cwv, r0, tr)[0] + bv
            cg = _conv3(ug_ref, cwg, r0, tr)[0] + bg
            a_ref[pl.ds(r0, tr), :] = (cg * _sigmoid(cg) * cv).astype(BF16)

    col = lambda off: (lambda j: (0, j + off))
    padded = pl.BlockSpec((S + 2 * PAD, tn), col(0))
    return pl.pallas_call(
        body, name=name, grid=(nb,),
        in_specs=[pl.BlockSpec((S, D), lambda j: (0, 0)),
                  pl.BlockSpec((D, tn), col(0)), pl.BlockSpec((D, tn), col(nb)),
                  pl.BlockSpec((3, tn), col(0)), pl.BlockSpec((3, tn), col(nb)),
                  pl.BlockSpec((1, tn), col(0)), pl.BlockSpec((1, tn), col(nb))],
        out_specs=[pl.BlockSpec((S, tn), col(0)), padded, padded],
        out_shape=[jax.ShapeDtypeStruct((S, F), BF16), jax.ShapeDtypeStruct((S + 2 * PAD, F), F32),
                   jax.ShapeDtypeStruct((S + 2 * PAD, F), F32)],
        compiler_params=_cp(("parallel",)),
    )(h, wup, wup, wconv, wconv, bconv, bconv)


def _rows8(rows):
    n = rows[0].shape[1]
    idx = lax.broadcasted_iota(jnp.int32, (8, n), 0)
    out = jnp.zeros((8, n), F32)
    for k, r in enumerate(rows):
        out = jnp.where(idx == k, r, out)
    return out


def _ffn_mid_bwd(dyb, wdown, uv, ug, wconv, bconv, name):
    S, D = dyb.shape
    F = wdown.shape[0]
    tn = FFN_TN_BWD
    nb = F // tn
    tr = min(FFN_ROWS, S)

    def body(dy_ref, wd_ref, uv_ref, ug_ref, cv_ref, cg_ref, bv_ref, bg_ref,
             duv_ref, dug_ref, a_ref, gwv_ref, gwg_ref, pdv, pdg):
        for p in (pdv, pdg):
            _zero_pads(p, S, tn)
        wd = wd_ref[...]
        cwv, cwg, bv, bg = cv_ref[...], cg_ref[...], bv_ref[...], bg_ref[...]
        zero = jnp.zeros((1, tn), F32)
        gv = [zero, zero, zero, zero]
        gg = [zero, zero, zero, zero]
        for r0 in range(0, S, tr):
            cv, shv = _conv3(uv_ref, cwv, r0, tr)
            cg, shg = _conv3(ug_ref, cwg, r0, tr)
            cv = cv + bv
            cg = cg + bg
            sg = _sigmoid(cg)
            sl = cg * sg
            a_ref[pl.ds(r0, tr), :] = (sl * cv).astype(BF16)
            da = _dg(dy_ref[pl.ds(r0, tr), :], wd, 1, 1)
            dcv = da * sl
            dcg = da * cv * (sg * (1.0 + cg * (1.0 - sg)))
            pdv[pl.ds(PAD + r0, tr), :] = dcv
            pdg[pl.ds(PAD + r0, tr), :] = dcg
            for k in range(3):
                gv[k] = gv[k] + jnp.sum(dcv * shv[k], axis=0, keepdims=True)
                gg[k] = gg[k] + jnp.sum(dcg * shg[k], axis=0, keepdims=True)
            gv[3] = gv[3] + jnp.sum(dcv, axis=0, keepdims=True)
            gg[3] = gg[3] + jnp.sum(dcg, axis=0, keepdims=True)
        gwv_ref[...] = _rows8(gv)
        gwg_ref[...] = _rows8(gg)
        for r0 in range(0, S, tr):
            for pd, cw, out in ((pdv, cwv, duv_ref), (pdg, cwg, dug_ref)):
                dm = pd[pl.ds(PAD - 1 + r0, tr), :]
                dc = pd[pl.ds(PAD + r0, tr), :]
                dp = pd[pl.ds(PAD + 1 + r0, tr), :]
                out[pl.ds(r0, tr), :] = (cw[0:1, :] * dp + cw[1:2, :] * dc + cw[2:3, :] * dm).astype(BF16)

    col = lambda off: (lambda j: (0, j + off))
    blk = pl.BlockSpec((S, tn), col(0))
    padded = pl.BlockSpec((S + 2 * PAD, tn), col(0))
    g8 = pl.BlockSpec((8, tn), col(0))
    return pl.pallas_call(
        body, name=name, grid=(nb,),
        in_specs=[pl.BlockSpec((S, D), lambda j: (0, 0)), pl.BlockSpec((tn, D), lambda j: (j, 0)), padded, padded,
                  pl.BlockSpec((3, tn), col(0)), pl.BlockSpec((3, tn), col(nb)),
                  pl.BlockSpec((1, tn), col(0)), pl.BlockSpec((1, tn), col(nb))],
        out_specs=[blk, blk, blk, g8, g8],
        out_shape=[jax.ShapeDtypeStruct((S, F), BF16), jax.ShapeDtypeStruct((S, F), BF16),
                   jax.ShapeDtypeStruct((S, F), BF16), jax.ShapeDtypeStruct((8, F), F32),
                   jax.ShapeDtypeStruct((8, F), F32)],
        scratch_shapes=[pltpu.VMEM((S + 2 * PAD, tn), F32)] * 2,
        compiler_params=_cp(("parallel",)),
    )(dyb, wdown, uv, ug, wconv, wconv, bconv, bconv)


def _log_sigmoid(x):
    return jnp.minimum(x, 0.0) - jnp.log(1.0 + jnp.exp(-jnp.abs(x)))


def _gla_gate_fwd(proj, wgf, bgf, wgb, bgb, name):
    S = proj.shape[0]
    tm = _pick(S, (512, 256))

    def body(r_ref, wf_ref, bf_ref, wb_ref, bb_ref, laf_ref, lab_ref):
        r = r_ref[...].astype(BF16)
        lf = jnp.dot(r, wf_ref[...].astype(BF16), preferred_element_type=F32) + bf_ref[...]
        lb = jnp.dot(r, wb_ref[...].astype(BF16), preferred_element_type=F32) + bb_ref[...]
        laf_ref[...] = _log_sigmoid(lf) * (1.0 / GLA_GATE_NORMALIZER)
        lab_ref[...] = _log_sigmoid(lb) * (1.0 / GLA_GATE_NORMALIZER)

    full = lambda shp: pl.BlockSpec(shp, lambda i: (0, 0))
    row = pl.BlockSpec((tm, GLA_KEY), lambda i: (i, 0))
    return pl.pallas_call(
        body, name=name, grid=(S // tm,),
        in_specs=[pl.BlockSpec((tm, LANES), lambda i: (i, GLA_R_BLOCK)),
                  full((LANES, GLA_KEY)), full((1, GLA_KEY)), full((LANES, GLA_KEY)), full((1, GLA_KEY))],
        out_specs=[row, row],
        out_shape=[jax.ShapeDtypeStruct((S, GLA_KEY), F32)] * 2,
        compiler_params=_cp(("parallel",)),
    )(proj, wgf, bgf, wgb, bgb)


def _gla_gate_bwd(dlaf, dlab, proj, wgf, bgf, wgb, bgb, name):
    S = proj.shape[0]
    tm = _pick(S, (512, 256))

    def body(dlf_ref, dlb_ref, r_ref, wf_ref, bf_ref, wb_ref, bb_ref, dr_ref, dwf_ref, dbf_ref, dwb_ref, dbb_ref):
        i = pl.program_id(0)
        r = r_ref[...].astype(BF16)
        wf = wf_ref[...].astype(BF16)
        wb = wb_ref[...].astype(BF16)
        lf = jnp.dot(r, wf, preferred_element_type=F32) + bf_ref[...]
        lb = jnp.dot(r, wb, preferred_element_type=F32) + bb_ref[...]
        glf = dlf_ref[...] * (1.0 / GLA_GATE_NORMALIZER) * (1.0 / (1.0 + jnp.exp(lf)))
        glb = dlb_ref[...] * (1.0 / GLA_GATE_NORMALIZER) * (1.0 / (1.0 + jnp.exp(lb)))
        gfb = glf.astype(BF16)
        gbb = glb.astype(BF16)
        dr_ref[...] = _dg(gfb, wf, 1, 1) + _dg(gbb, wb, 1, 1)
        parts = (_dg(r, gfb, 0, 0), jnp.sum(glf, axis=0, keepdims=True),
                 _dg(r, gbb, 0, 0), jnp.sum(glb, axis=0, keepdims=True))
        outs = (dwf_ref, dbf_ref, dwb_ref, dbb_ref)

        @pl.when(i == 0)
        def _():
            for o, p in zip(outs, parts):
                o[...] = p

        @pl.when(i > 0)
        def _():
            for o, p in zip(outs, parts):
                o[...] += p

    full = lambda shp: pl.BlockSpec(shp, lambda i: (0, 0))
    row = pl.BlockSpec((tm, GLA_KEY), lambda i: (i, 0))
    return pl.pallas_call(
        body, name=name, grid=(S // tm,),
        in_specs=[row, row, pl.BlockSpec((tm, LANES), lambda i: (i, GLA_R_BLOCK)),
                  full((LANES, GLA_KEY)), full((1, GLA_KEY)), full((LANES, GLA_KEY)), full((1, GLA_KEY))],
        out_specs=[pl.BlockSpec((tm, LANES), lambda i: (i, 0)),
                   full((LANES, GLA_KEY)), full((1, GLA_KEY)), full((LANES, GLA_KEY)), full((1, GLA_KEY))],
        out_shape=[jax.ShapeDtypeStruct((S, LANES), F32),
                   jax.ShapeDtypeStruct((LANES, GLA_KEY), F32), jax.ShapeDtypeStruct((1, GLA_KEY), F32),
                   jax.ShapeDtypeStruct((LANES, GLA_KEY), F32), jax.ShapeDtypeStruct((1, GLA_KEY), F32)],
        compiler_params=_cp(("arbitrary",)),
    )(dlaf, dlab, proj, wgf, bgf, wgb, bgb)


def _gla_masks(rev):
    C = GLA_CHUNK
    t = lax.broadcasted_iota(jnp.int32, (C, C), 0)
    s = lax.broadcasted_iota(jnp.int32, (C, C), 1)
    if rev:
        return (s >= t), (s > t), (t >= s), (t > s)
    return (s <= t), (s <= t), (t <= s), (t <= s)


def _cum_dot(cum, x):
    return jnp.dot(cum.astype(F32), x, precision=HIGHEST, preferred_element_type=F32)


def _gla_chunk_common(q, k, la, cum, end_row):
    b = _cum_dot(cum, la)
    bend = b[end_row:end_row + 1, :]
    e = jnp.exp(b)
    qd = q * (GLA_DK ** -0.5) * e
    ei = jnp.exp(-b)
    ee = jnp.exp(bend - b)
    d = jnp.exp(bend)
    return e, ei, ee, d, qd, k * ei, k * ee


GLA_CB = 8


def _gla_specs(S, rev_order):
    n = S // GLA_CHUNK
    cb = min(GLA_CB, n)
    nblk = n // cb
    rows = cb * GLA_CHUNK
    ci = (lambda i: nblk - 1 - i) if rev_order else (lambda i: i)
    q_spec = pl.BlockSpec((rows, GLA_DK), lambda h, i: (ci(i), h))
    k_spec = pl.BlockSpec((rows, GLA_DK), lambda h, i: (ci(i), GLA_HEADS + h))
    v_spec = pl.BlockSpec((rows, GLA_DV), lambda h, i: (ci(i), GLA_KEY * 2 // GLA_DV + h))
    la_spec = pl.BlockSpec((rows, GLA_DK), lambda h, i: (ci(i), h))
    o_spec = pl.BlockSpec((rows, GLA_DV), lambda h, i: (ci(i), h))
    st_spec = pl.BlockSpec((1, cb, GLA_DV, GLA_DK), lambda h, i: (h, ci(i), 0, 0))
    return n, cb, nblk, q_spec, k_spec, v_spec, la_spec, o_spec, st_spec


def _gla_scan_fwd(proj, la, rev, name):
    S = proj.shape[0]
    C = GLA_CHUNK
    n, cb, nblk, q_spec, k_spec, v_spec, la_spec, o_spec, st_spec = _gla_specs(S, rev)
    end_row = 0 if rev else C - 1
    order = list(range(cb))[::-1] if rev else list(range(cb))

    def body(q_ref, k_ref, v_ref, la_ref, o_ref, st_ref, state):
        i = pl.program_id(1)

        @pl.when(i == 0)
        def _():
            state[...] = jnp.zeros_like(state)

        cum, mask, _, _ = _gla_masks(rev)
        pre, intra, kv = {}, {}, {}
        for cc in order:
            rows = pl.ds(cc * C, C)
            q, k, v, lav = q_ref[rows, :], k_ref[rows, :], v_ref[rows, :], la_ref[rows, :]
            _, _, _, d, qd, ki, ke = _gla_chunk_common(q, k, lav, cum, end_row)
            qdb, kib, keb, vb = qd.astype(BF16), ki.astype(BF16), ke.astype(BF16), v.astype(BF16)
            pre[cc] = (d, qdb)
            att = jnp.where(mask, _dg(qdb, kib, 1, 1), 0.0)
            intra[cc] = jnp.dot(att.astype(BF16), vb, preferred_element_type=F32)
            kv[cc] = _dg(vb, keb, 0, 0)
        st = state[...]
        for cc in order:
            d, qdb = pre[cc]
            o_ref[pl.ds(cc * C, C), :] = intra[cc] + _dg(qdb, st.astype(BF16), 1, 1)
            st_ref[0, cc] = st
            st = st * d + kv[cc]
        state[...] = st

    return pl.pallas_call(
        body, name=name, grid=(GLA_HEADS, nblk),
        in_specs=[q_spec, k_spec, v_spec, la_spec],
        out_specs=[o_spec, st_spec],
        out_shape=[jax.ShapeDtypeStruct((S, GLA_VAL), F32),
                   jax.ShapeDtypeStruct((GLA_HEADS, n, GLA_DV, GLA_DK), F32)],
        scratch_shapes=[pltpu.VMEM((GLA_DV, GLA_DK), F32)],
        compiler_params=_cp(("parallel", "arbitrary")),
    )(proj, proj, proj, la)


def _gla_scan_bwd(do, proj, la, states, rev, name):
    S = proj.shape[0]
    C = GLA_CHUNK
    n, cb, nblk, q_spec, k_spec, v_spec, la_spec, o_spec, st_spec = _gla_specs(S, not rev)
    end_row = 0 if rev else C - 1
    order = list(range(cb)) if rev else list(range(cb))[::-1]

    def body(do_ref, q_ref, k_ref, v_ref, la_ref, st_ref, dq_ref, dk_ref, dv_ref, dla_ref, gstate):
        i = pl.program_id(1)

        @pl.when(i == 0)
        def _():
            gstate[...] = jnp.zeros_like(gstate)

        cum, mask, cum_t, mask_t = _gla_masks(rev)
        g = gstate[...]
        for cc in order:
            rows = pl.ds(cc * C, C)
            q, k, v, lav = q_ref[rows, :], k_ref[rows, :], v_ref[rows, :], la_ref[rows, :]
            dov = do_ref[rows, :]
            st = st_ref[0, cc]
            e, ei, ee, d, qd, ki, ke = _gla_chunk_common(q, k, lav, cum, end_row)
            qdb, kib, keb, vb = qd.astype(BF16), ki.astype(BF16), ke.astype(BF16), v.astype(BF16)
            dob, gb, stb = dov.astype(BF16), g.astype(BF16), st.astype(BF16)
            att_t = jnp.where(mask_t, _dg(kib, qdb, 1, 1), 0.0)
            da = jnp.where(mask, _dg(dob, vb, 1, 1), 0.0)
            da_t = jnp.where(mask_t, _dg(vb, dob, 1, 1), 0.0)
            dv_ref[rows, :] = jnp.dot(att_t.astype(BF16), dob, preferred_element_type=F32) + _dg(keb, gb, 1, 1)
            dqd = (jnp.dot(da.astype(BF16), kib, preferred_element_type=F32)
                   + jnp.dot(dob, stb, preferred_element_type=F32))
            dki = jnp.dot(da_t.astype(BF16), qdb, preferred_element_type=F32)
            dke = jnp.dot(vb, gb, preferred_element_type=F32)
            dd = jnp.sum(st * g, axis=0, keepdims=True)
            g = g * d + _dg(dob, qdb, 0, 0)
            dq_ref[rows, :] = dqd * e * (GLA_DK ** -0.5)
            dk_ref[rows, :] = dki * ei + dke * ee
            dkeke = dke * ke
            db = dqd * qd - dki * ki - dkeke
            dbend = jnp.sum(dkeke, axis=0, keepdims=True) + dd * d
            dla_ref[rows, :] = _cum_dot(cum_t, db) + dbend
        gstate[...] = g

    key_out = la_spec
    return pl.pallas_call(
        body, name=name, grid=(GLA_HEADS, nblk),
        in_specs=[o_spec, q_spec, k_spec, v_spec, la_spec, st_spec],
        out_specs=[key_out, key_out, o_spec, key_out],
        out_shape=[jax.ShapeDtypeStruct((S, GLA_KEY), F32), jax.ShapeDtypeStruct((S, GLA_KEY), F32),
                   jax.ShapeDtypeStruct((S, GLA_VAL), F32), jax.ShapeDtypeStruct((S, GLA_KEY), F32)],
        scratch_shapes=[pltpu.VMEM((GLA_DV, GLA_DK), F32)],
        compiler_params=_cp(("parallel", "arbitrary")),
    )(do, proj, proj, proj, la, states)


def _gla_out_fwd(of, ob, proj, gn, name):
    S = of.shape[0]
    tm = _pick(S, (256, 128))
    gblk = (2 * GLA_KEY + GLA_VAL) // GLA_VAL

    def body(of_ref, ob_ref, g_ref, gn_ref, z_ref):
        gnv = gn_ref[...]
        for h in range(GLA_HEADS):
            cols = pl.ds(h * GLA_DV, GLA_DV)
            o = of_ref[:, cols] + ob_ref[:, cols]
            r = lax.rsqrt(jnp.mean(o * o, axis=-1, keepdims=True) + NORM_EPS)
            gv = g_ref[:, cols]
            z_ref[:, cols] = (o * r * gnv * (gv * _sigmoid(gv))).astype(BF16)

    row = pl.BlockSpec((tm, GLA_VAL), lambda i: (i, 0))
    return pl.pallas_call(
        body, name=name, grid=(S // tm,),
        in_specs=[row, row, pl.BlockSpec((tm, GLA_VAL), lambda i: (i, gblk)),
                  pl.BlockSpec((1, GLA_DV), lambda i: (0, 0))],
        out_specs=row,
        out_shape=jax.ShapeDtypeStruct((S, GLA_VAL), BF16),
        compiler_params=_cp(("parallel",)),
    )(of, ob, proj, gn)


def _gla_out_bwd(dz, of, ob, proj, gn, name):
    S = of.shape[0]
    tm = _pick(S, (256, 128))
    gblk = (2 * GLA_KEY + GLA_VAL) // GLA_VAL

    def body(dz_ref, of_ref, ob_ref, g_ref, gn_ref, do_ref, dg_ref, dgn_ref):
        i = pl.program_id(0)
        gnv = gn_ref[...]
        part = jnp.zeros((1, GLA_DV), F32)
        for h in range(GLA_HEADS):
            cols = pl.ds(h * GLA_DV, GLA_DV)
            o = of_ref[:, cols] + ob_ref[:, cols]
            r = lax.rsqrt(jnp.mean(o * o, axis=-1, keepdims=True) + NORM_EPS)
            y = o * r
            gv = g_ref[:, cols]
            sg = _sigmoid(gv)
            dzv = dz_ref[:, cols]
            dg_ref[:, cols] = dzv * (y * gnv) * (sg * (1.0 + gv * (1.0 - sg)))
            don = dzv * (gv * sg)
            part = part + jnp.sum(don * y, axis=0, keepdims=True)
            dy = don * gnv
            do_ref[:, cols] = r * (dy - y * jnp.mean(dy * y, axis=-1, keepdims=True))

        @pl.when(i == 0)
        def _():
            dgn_ref[...] = part

        @pl.when(i > 0)
        def _():
            dgn_ref[...] += part

    row = pl.BlockSpec((tm, GLA_VAL), lambda i: (i, 0))
    one = pl.BlockSpec((1, GLA_DV), lambda i: (0, 0))
    return pl.pallas_call(
        body, name=name, grid=(S // tm,),
        in_specs=[row, row, row, pl.BlockSpec((tm, GLA_VAL), lambda i: (i, gblk)), one],
        out_specs=[row, row, one],
        out_shape=[jax.ShapeDtypeStruct((S, GLA_VAL), F32), jax.ShapeDtypeStruct((S, GLA_VAL), F32),
                   jax.ShapeDtypeStruct((1, GLA_DV), F32)],
        compiler_params=_cp(("arbitrary",)),
    )(dz, of, ob, proj, gn)


N_QK_HEADS = ATT_QH + ATT_KVH


def _qk_prep_fwd(proj, qn, kn, rc, rs, name):
    S = proj.shape[0]
    tm = _pick(S, (256, 128))
    W = N_QK_HEADS * ATT_HD
    scale = ATT_HD ** -0.5

    def body(p_ref, qn_ref, kn_ref, rc_ref, rs_ref, v_in_ref, qk_ref, v_ref, kt_ref, vt_ref):
        c, s = rc_ref[...], rs_ref[...]
        for h in range(N_QK_HEADS):
            cols = pl.ds(h * ATT_HD, ATT_HD)
            w = qn_ref[...] if h < ATT_QH else kn_ref[...]
            xv = p_ref[:, cols]
            r = lax.rsqrt(jnp.mean(xv * xv, axis=-1, keepdims=True) + NORM_EPS)
            y = xv * r * w
            out = y * c + pltpu.roll(y, ATT_HD // 2, 1) * s
            if h < ATT_QH:
                qk_ref[:, cols] = (out * scale).astype(BF16)
            else:
                qk_ref[:, cols] = out.astype(BF16)
                kt_ref[pl.ds((h - ATT_QH) * ATT_HD, ATT_HD), :] = out.T.astype(BF16)
        v_ref[...] = v_in_ref[...].astype(BF16)
        for h in range(ATT_KVH):
            vt_ref[pl.ds(h * ATT_HD, ATT_HD), :] = v_in_ref[:, pl.ds(h * ATT_HD, ATT_HD)].T.astype(BF16)

    one = pl.BlockSpec((1, ATT_HD), lambda i: (0, 0))
    tab = pl.BlockSpec((tm, ATT_HD), lambda i: (i, 0))
    vw = ATT_KVH * ATT_HD
    tr = pl.BlockSpec((vw, tm), lambda i: (0, i))
    return pl.pallas_call(
        body, name=name, grid=(S // tm,),
        in_specs=[pl.BlockSpec((tm, W), lambda i: (i, 0)), one, one, tab, tab,
                  pl.BlockSpec((tm, vw), lambda i: (i, W // vw))],
        out_specs=[pl.BlockSpec((tm, W), lambda i: (i, 0)), pl.BlockSpec((tm, vw), lambda i: (i, 0)), tr, tr],
        out_shape=[jax.ShapeDtypeStruct((S, W), BF16), jax.ShapeDtypeStruct((S, vw), BF16),
                   jax.ShapeDtypeStruct((vw, S), BF16), jax.ShapeDtypeStruct((vw, S), BF16)],
        compiler_params=_cp(("parallel",)),
    )(proj, qn, kn, rc, rs, proj)


def _qk_prep_bwd(dqk, proj, qn, kn, rc, rs, name):
    S = proj.shape[0]
    tm = _pick(S, (256, 128))
    W = N_QK_HEADS * ATT_HD

    def body(d_ref, p_ref, qn_ref, kn_ref, rc_ref, rs_ref, dp_ref, dqn_ref, dkn_ref):
        i = pl.program_id(0)
        c, s = rc_ref[...], rs_ref[...]
        parts = [jnp.zeros((1, ATT_HD), F32), jnp.zeros((1, ATT_HD), F32)]
        for h in range(N_QK_HEADS):
            cols = pl.ds(h * ATT_HD, ATT_HD)
            w = qn_ref[...] if h < ATT_QH else kn_ref[...]
            dout = d_ref[:, cols]
            dy = dout * c + pltpu.roll(dout * s, ATT_HD // 2, 1)
            xv = p_ref[:, cols]
            r = lax.rsqrt(jnp.mean(xv * xv, axis=-1, keepdims=True) + NORM_EPS)
            xr = xv * r
            which = 0 if h < ATT_QH else 1
            parts[which] = parts[which] + jnp.sum(dy * xr, axis=0, keepdims=True)
            dxr = dy * w
            dp_ref[:, cols] = r * (dxr - xr * jnp.mean(dxr * xr, axis=-1, keepdims=True))

        @pl.when(i == 0)
        def _():
            dqn_ref[...] = parts[0]
            dkn_ref[...] = parts[1]

        @pl.when(i > 0)
        def _():
            dqn_ref[...] += parts[0]
            dkn_ref[...] += parts[1]

    one = pl.BlockSpec((1, ATT_HD), lambda i: (0, 0))
    tab = pl.BlockSpec((tm, ATT_HD), lambda i: (i, 0))
    row = pl.BlockSpec((tm, W), lambda i: (i, 0))
    return pl.pallas_call(
        body, name=name, grid=(S // tm,),
        in_specs=[row, row, one, one, tab, tab],
        out_specs=[row, one, one],
        out_shape=[jax.ShapeDtypeStruct((S, W), F32), jax.ShapeDtypeStruct((1, ATT_HD), F32),
                   jax.ShapeDtypeStruct((1, ATT_HD), F32)],
        compiler_params=_cp(("arbitrary",)),
    )(dqk, proj, qn, kn, rc, rs)


ATT_TQ = 256
LSE_ROWS = 8


def _attn_fwd(qk, vt, name):
    S = qk.shape[0]
    tq = min(ATT_TQ, S)

    def body(q_ref, k_ref, vt_ref, o_ref, lse_ref):
        st = _dg(k_ref[...], q_ref[...], 1, 1)
        m = jnp.max(st, axis=0, keepdims=True)
        pt = jnp.exp(st - m)
        l = jnp.sum(pt, axis=0, keepdims=True)
        ot = jnp.dot(vt_ref[...], pt.astype(BF16), preferred_element_type=F32)
        o_ref[...] = (ot * (1.0 / l)).T
        lse_ref[...] = jnp.broadcast_to(m + jnp.log(l), (LSE_ROWS, tq))

    qo = pl.BlockSpec((tq, ATT_HD), lambda h, i: (i, h))
    return pl.pallas_call(
        body, name=name, grid=(ATT_QH, S // tq),
        in_specs=[qo, pl.BlockSpec((S, ATT_HD), lambda h, i: (0, ATT_QH + h // ATT_GROUP)),
                  pl.BlockSpec((ATT_HD, S), lambda h, i: (h // ATT_GROUP, 0))],
        out_specs=[qo, pl.BlockSpec((LSE_ROWS, tq), lambda h, i: (h, i))],
        out_shape=[jax.ShapeDtypeStruct((S, ATT_QH * ATT_HD), F32),
                   jax.ShapeDtypeStruct((ATT_QH * LSE_ROWS, S), F32)],
        compiler_params=_cp(("parallel", "parallel")),
    )(qk, qk, vt)


def _attn_bwd(do, o, lse, qk, v, kt, name):
    S = qk.shape[0]
    tq = min(ATT_TQ, S)
    scale = ATT_HD ** -0.5

    def body(do_ref, o_ref, lse_ref, q_ref, k_ref, v_ref, kt_ref, dq_ref, dk_ref, dv_ref):
        g = pl.program_id(1)
        i = pl.program_id(2)

        @pl.when((g == 0) & (i == 0))
        def _():
            dk_ref[...] = jnp.zeros_like(dk_ref)
            dv_ref[...] = jnp.zeros_like(dv_ref)

        q = q_ref[...]
        dov = do_ref[...]
        dob = dov.astype(BF16)
        delta = jnp.sum((dov * o_ref[...]).T, axis=0, keepdims=True)
        st = _dg(k_ref[...], q, 1, 1)
        pt = jnp.exp(st - lse_ref[0:1, :])
        dpt = _dg(v_ref[...], dob, 1, 1)
        dst = (pt * (dpt - delta)).astype(BF16)
        dv_ref[...] += jnp.dot(pt.astype(BF16), dob, preferred_element_type=F32)
        dk_ref[...] += jnp.dot(dst, q, preferred_element_type=F32)
        dq_ref[...] = jnp.dot(kt_ref[...], dst, preferred_element_type=F32).T * scale

    qo = pl.BlockSpec((tq, ATT_HD), lambda kv, g, i: (i, kv * ATT_GROUP + g))
    kvo = pl.BlockSpec((S, ATT_HD), lambda kv, g, i: (0, kv))
    return pl.pallas_call(
        body, name=name, grid=(ATT_KVH, ATT_GROUP, S // tq),
        in_specs=[qo, qo, pl.BlockSpec((LSE_ROWS, tq), lambda kv, g, i: (kv * ATT_GROUP + g, i)), qo,
                  pl.BlockSpec((S, ATT_HD), lambda kv, g, i: (0, ATT_QH + kv)), kvo,
                  pl.BlockSpec((ATT_HD, S), lambda kv, g, i: (kv, 0))],
        out_specs=[qo, kvo, kvo],
        out_shape=[jax.ShapeDtypeStruct((S, ATT_QH * ATT_HD), F32),
                   jax.ShapeDtypeStruct((S, ATT_KVH * ATT_HD), F32),
                   jax.ShapeDtypeStruct((S, ATT_KVH * ATT_HD), F32)],
        compiler_params=_cp(("parallel", "arbitrary", "arbitrary")),
    )(do, o, lse, qk, qk, v, kt)


def _adamw(w, g, m, v, name):
    rows, cols = w.shape
    tr = rows
    for cand in (512, 256, 128, 64, 32, 16, 8):
        if rows % cand == 0 and cand * cols * 4 <= 2 * 1024 * 1024:
            tr = cand
            break

    def body(w_ref, g_ref, m_ref, v_ref, d_ref, nm_ref, nv_ref):
        gv = g_ref[...]
        nm = ADAM_B1 * m_ref[...] + (1.0 - ADAM_B1) * gv
        nv = ADAM_B2 * v_ref[...] + (1.0 - ADAM_B2) * (gv * gv)
        m_hat = nm / (1.0 - ADAM_B1 ** ADAM_STEP)
        v_hat = nv / (1.0 - ADAM_B2 ** ADAM_STEP)
        d_ref[...] = -ADAM_LR * (m_hat / (jnp.sqrt(v_hat) + ADAM_EPS) + ADAM_WD * w_ref[...])
        nm_ref[...] = nm
        nv_ref[...] = nv

    blk = pl.BlockSpec((tr, cols), lambda i: (i, 0))
    return pl.pallas_call(
        body, name=name, grid=(rows // tr,),
        in_specs=[blk] * 4, out_specs=[blk] * 3,
        out_shape=[jax.ShapeDtypeStruct((rows, cols), F32)] * 3,
        compiler_params=_cp(("parallel",)),
    )(w, g, m, v)


ANY = pl.BlockSpec(memory_space=pl.ANY)


def _place():
    return lax.axis_index("x"), lax.axis_index("y"), lax.axis_index("c")


def _other_chips(x, y):
    return [(1 - x, y), (x, 1 - y), (1 - x, 1 - y)]


def _half_rows(c, H):
    return pl.ds(pl.multiple_of(c * H, 8), H)


def _allreduce_small(v, name):
    R = v.shape[0]
    n_dev = 8

    def body(v_ref, sum_ref, all_ref, send_sems, recv_sems, local_sem):
        x, y, c = _place()
        me, sibling = (x, y, c), (x, y, 1 - c)
        chips = _other_chips(x, y)

        def rows(px, py, pc):
            return all_ref.at[pl.ds(pl.multiple_of((4 * px + 2 * py + pc) * R, 8), R), :]

        def copy(k, block, to, src=None):
            return pltpu.make_async_remote_copy(
                src_ref=rows(*block) if src is None else src, dst_ref=rows(*block),
                send_sem=send_sems.at[k], recv_sem=recv_sems.at[k], device_id=to, device_id_type=MESH)

        own = pltpu.make_async_copy(v_ref, rows(*me), local_sem)
        own.start()
        first = [copy(0, me, sibling, src=v_ref)]
        first += [copy(1 + j, me, (*chip, c), src=v_ref) for j, chip in enumerate(chips)]
        for cp in first:
            cp.start()
        passed = [copy(4 + j, (*chip, c), sibling) for j, chip in enumerate(chips)]
        for j, chip in enumerate(chips):
            copy(1 + j, (*chip, c), me).wait_recv()
            passed[j].start()
        copy(0, sibling, me).wait_recv()
        for j, chip in enumerate(chips):
            copy(4 + j, (*chip, 1 - c), me).wait_recv()
        for cp in first + passed:
            cp.wait_send()
        own.wait()
        acc = all_ref[pl.ds(0, R), :]
        for d in range(1, n_dev):
            acc = acc + all_ref[pl.ds(d * R, R), :]
        sum_ref[...] = acc

    vm = pl.BlockSpec(memory_space=pltpu.VMEM)
    return pl.pallas_call(
        body, name=name,
        in_specs=[vm], out_specs=[vm, vm],
        out_shape=[jax.ShapeDtypeStruct((R, LANES), F32), jax.ShapeDtypeStruct((n_dev * R, LANES), F32)],
        scratch_shapes=[pltpu.SemaphoreType.DMA((7,)), pltpu.SemaphoreType.DMA((7,)), pltpu.SemaphoreType.DMA],
    )(v)[0]


def _swap_other_half(bufs, name):
    n = len(bufs)
    halves = [b.shape[1] // 2 for b in bufs]

    def body(*refs):
        g_refs, got_refs = refs[:n], refs[n:2 * n]
        send_sems, recv_sems = refs[2 * n:]
        x, y, c = _place()
        copies = [pltpu.make_async_remote_copy(
            src_ref=g_refs[k].at[p, _half_rows(1 - c, halves[k])], dst_ref=got_refs[k].at[p],
            send_sem=send_sems.at[N_CHIPS * k + p], recv_sem=recv_sems.at[N_CHIPS * k + p],
            device_id=(x, y, 1 - c), device_id_type=MESH) for k in range(n) for p in range(N_CHIPS)]
        for cp in copies:
            cp.start()
        for cp in copies:
            cp.wait_recv()
        for cp in copies:
            cp.wait_send()

    return pl.pallas_call(
        body, name=name, in_specs=[ANY] * n, out_specs=[ANY] * n,
        out_shape=[jax.ShapeDtypeStruct((N_CHIPS, h, b.shape[2]), b.dtype) for b, h in zip(bufs, halves)],
        scratch_shapes=[pltpu.SemaphoreType.DMA((N_CHIPS * n,)), pltpu.SemaphoreType.DMA((N_CHIPS * n,))],
    )(*bufs)


def _join_halves(bufs, name):
    n = len(bufs)
    halves = [b.shape[0] // 2 for b in bufs]

    def body(*refs):
        outs = refs[n:2 * n]
        send_sems, recv_sems = refs[2 * n:]
        x, y, c = _place()

        def copy(k, core):
            blk = outs[k].at[_half_rows(core, halves[k])]
            return pltpu.make_async_remote_copy(src_ref=blk, dst_ref=blk, send_sem=send_sems.at[k],
                                                recv_sem=recv_sems.at[k], device_id=(x, y, 1 - c),
                                                device_id_type=MESH)

        sends = [copy(k, c) for k in range(n)]
        for cp in sends:
            cp.start()
        for k in range(n):
            copy(k, 1 - c).wait_recv()
        for cp in sends:
            cp.wait_send()

    return pl.pallas_call(
        body, name=name, in_specs=[ANY] * n, out_specs=[ANY] * n,
        out_shape=[jax.ShapeDtypeStruct(b.shape, b.dtype) for b in bufs],
        input_output_aliases={k: k for k in range(n)},
        scratch_shapes=[pltpu.SemaphoreType.DMA((n,)), pltpu.SemaphoreType.DMA((n,))],
    )(*bufs)


def _rs_rows(H, width):
    for cand in (1024, 512, 256, 128, 64, 32, 16):
        if H % cand == 0 and cand * width * 4 <= 1536 * 1024:
            return cand
    return H


def _add_sibling(g, got, c, me, name):
    _, H, width = got.shape
    tb = _rs_rows(H, width)
    nb = H // tb

    def body(sp_ref, g_ref, got_ref, sb_ref, sf_ref):
        p = pl.program_id(1)
        s = g_ref[0] + got_ref[0]
        sb_ref[0] = s.astype(BF16)

        @pl.when(p == sp_ref[1])
        def _():
            sf_ref[...] = s

    grid_spec = pltpu.PrefetchScalarGridSpec(
        num_scalar_prefetch=1, grid=(nb, N_CHIPS),
        in_specs=[pl.BlockSpec((1, tb, width), lambda i, p, sp: (p, sp[0] * nb + i, 0)),
                  pl.BlockSpec((1, tb, width), lambda i, p, sp: (p, i, 0))],
        out_specs=[pl.BlockSpec((1, tb, width), lambda i, p, sp: (p, i, 0)),
                   pl.BlockSpec((tb, width), lambda i, p, sp: (i, 0))])
    return pl.pallas_call(
        body, name=name, grid_spec=grid_spec,
        out_shape=[jax.ShapeDtypeStruct((N_CHIPS, H, width), BF16), jax.ShapeDtypeStruct((H, width), F32)],
        compiler_params=_cp(("arbitrary", "arbitrary")),
    )(jnp.stack([c, me]).astype(jnp.int32), g, got)


def _add_chips(sf, got, others_and_c, name):
    H, width = sf.shape
    tb = _rs_rows(H, width)
    nb = H // tb

    def body(sp_ref, sf_ref, r1_ref, r2_ref, r3_ref, out_ref):
        out_ref[...] = ((sf_ref[...] + r1_ref[0].astype(F32)) + r2_ref[0].astype(F32)) + r3_ref[0].astype(F32)

    def slot(k):
        return pl.BlockSpec((1, tb, width), lambda i, sp: (sp[k], i, 0))

    blk = pl.BlockSpec((tb, width), lambda i, sp: (i, 0))
    grid_spec = pltpu.PrefetchScalarGridSpec(
        num_scalar_prefetch=1, grid=(nb,), in_specs=[blk, slot(0), slot(1), slot(2)],
        out_specs=pl.BlockSpec((tb, width), lambda i, sp: (sp[3] * nb + i, 0)))
    return pl.pallas_call(
        body, name=name, grid_spec=grid_spec,
        out_shape=jax.ShapeDtypeStruct((2 * H, width), F32),
        compiler_params=_cp(("arbitrary",)),
    )(others_and_c.astype(jnp.int32), sf, got, got, got)


REPLICATED = ("norm_mix", "norm_ffn", "gla_b_gate_f", "gla_b_gate_b", "gla_norm", "attn_q_norm", "attn_k_norm",
              "ffn_b_conv")


PIECE_ROWS = 16


def _piece_rows(shape):
    n = 1
    for s in shape:
        n *= s
    rows = n // LANES
    return rows, -(-rows // PIECE_ROWS) * PIECE_ROWS


def _pack(pieces, dtype, row_multiple):
    flat = []
    for p in pieces:
        rows, padded = _piece_rows(p.shape)
        flat.append(jnp.pad(p.astype(dtype).reshape(rows, LANES), ((0, padded - rows), (0, 0))))
    rows = sum(f.shape[0] for f in flat)
    padded = -(-rows // row_multiple) * row_multiple
    if padded > rows:
        flat.append(jnp.zeros((padded - rows, LANES), dtype))
    return jnp.concatenate(flat, axis=0)


def _unpack(buf, shapes):
    out, r = [], 0
    for shp in shapes:
        rows, padded = _piece_rows(shp)
        out.append(buf[r:r + rows].reshape(shp))
        r += padded
    return out


def _own_slot(shard2d, me):
    return lax.dynamic_update_index_in_dim(jnp.zeros((N_CHIPS,) + shard2d.shape, shard2d.dtype), shard2d, me, 0)


def _layer_small(w, l):
    j = l // 2
    if l % 2 == 0:
        return [w["gla_w_gate_up_f"][j], w["gla_w_gate_up_b"][j], w["ffn_w_conv"][l]]
    return [w["ffn_w_conv"][l]]


def _layer_weight_bufs(w, l, me):
    j = l // 2
    mixer = ("gla_w_in", "gla_w_out") if l % 2 == 0 else ("attn_w_qkv", "attn_w_out")
    bufs = [_own_slot(w[n][j].astype(BF16), me) for n in mixer]
    bufs.append(_own_slot(_pack(_layer_small(w, l), F32, 32), me))
    bufs += [_own_slot(w["ffn_w_up"][l].astype(BF16), me), _own_slot(w["ffn_w_down"][l].astype(BF16), me)]
    return bufs


N_MIXER_BUFS = 3


def _layer_weights(w, l, got):
    rows = lambda t: t.reshape(-1, t.shape[2])
    cols = lambda t: jnp.concatenate([t[p] for p in range(N_CHIPS)], axis=1)
    out = {}
    if len(got) != N_MIXER_BUFS:
        up, down = got[-2:]
        out.update(up=up, up_full=cols(up), down=rows(down))
    if len(got) != 2:
        mix_in, mix_out, small = got[:N_MIXER_BUFS]
        shapes = [t.shape for t in _layer_small(w, l)]
        parts = [_unpack(small[p], shapes) for p in range(N_CHIPS)]
        full_small = [jnp.concatenate([parts[p][k] for p in range(N_CHIPS)], axis=-1) for k in range(len(shapes))]
        out.update(conv=full_small[-1])
        if l % 2 == 0:
            out.update(gla_in=jnp.pad(cols(mix_in), ((0, 0), (0, GLA_IN_PAD - GLA_IN))), gla_out=rows(mix_out),
                       gate_f=full_small[0], gate_b=full_small[1])
        else:
            out.update(qkv=mix_in, attn_out=rows(mix_out))
    return out


HBM = pl.BlockSpec(memory_space=pltpu.HBM)
SEM = pl.BlockSpec(memory_space=pltpu.SEMAPHORE)
SIDE_EFFECT = pltpu.SideEffectType.DATAFLOW_SIDE_EFFECTING


def _gather_start(bufs, after, name):
    n = len(bufs)
    halves = [b.shape[1] // 2 for b in bufs]

    def body(*refs):
        refs = refs[:n] + refs[n + 1:]
        send_sems, recv_sems = refs[n:2 * n], refs[2 * n:3 * n]
        outs, token = refs[3 * n:4 * n], refs[4 * n]
        x, y, c = _place()
        me = 2 * x + y
        for k in range(n):
            blk = outs[k].at[me, _half_rows(c, halves[k])]
            for px, py in _other_chips(x, y):
                pltpu.make_async_remote_copy(src_ref=blk, dst_ref=blk, send_sem=send_sems[k], recv_sem=recv_sems[k],
                                             device_id=(px, py, c), device_id_type=MESH).start()
        token[...] = jnp.zeros_like(token)

    res = pl.pallas_call(
        body, name=name,
        in_specs=[HBM] * n + [ANY],
        out_specs=[SEM] * (2 * n) + [HBM] * n + [pl.BlockSpec(memory_space=pltpu.VMEM)],
        out_shape=[pltpu.SemaphoreType.DMA(())] * (2 * n) + [pltpu.HBM(b.shape, b.dtype) for b in bufs]
        + [jax.ShapeDtypeStruct((8, LANES), F32)],
        input_output_aliases={k: 2 * n + k for k in range(n)},
        compiler_params=pltpu.CompilerParams(has_side_effects=SIDE_EFFECT),
    )(*[pltpu.with_memory_space_constraint(b, pltpu.HBM) for b in bufs], after)
    return res[:n], res[n:2 * n], res[2 * n:3 * n], res[3 * n]


def _gather_wait(send_sems, recv_sems, thru, after, name):
    n = len(thru)
    halves = [b.shape[1] // 2 for b in thru]

    def body(*refs):
        ss, rs = refs[n:2 * n], refs[2 * n:3 * n]
        outs = refs[3 * n + 1:]
        x, y, c = _place()
        for k in range(n):
            three = outs[k].at[pl.ds(0, N_CHIPS - 1), _half_rows(c, halves[k])]
            cp = pltpu.make_async_remote_copy(src_ref=three, dst_ref=three, send_sem=ss[k], recv_sem=rs[k],
                                              device_id=(x, y, c), device_id_type=MESH)
            cp.wait_send()
            cp.wait_recv()

    return pl.pallas_call(
        body, name=name,
        in_specs=[HBM] * n + [SEM] * (2 * n) + [ANY],
        out_specs=[HBM] * n,
        out_shape=[pltpu.HBM(b.shape, b.dtype) for b in thru],
        input_output_aliases={k: k for k in range(n)},
        compiler_params=pltpu.CompilerParams(has_side_effects=SIDE_EFFECT),
    )(*thru, *send_sems, *recv_sems, after)


def _send_start(sbs, name):
    n = len(sbs)

    def body(*refs):
        send_sems, recv_sems = refs[2 * n:3 * n], refs[3 * n:4 * n]
        srcs, lands, token = refs[4 * n:5 * n], refs[5 * n:6 * n], refs[6 * n]
        x, y, c = _place()
        me = 2 * x + y
        for k in range(n):
            for px, py in _other_chips(x, y):
                pltpu.make_async_remote_copy(src_ref=srcs[k].at[2 * px + py], dst_ref=lands[k].at[me],
                                             send_sem=send_sems[k], recv_sem=recv_sems[k],
                                             device_id=(px, py, c), device_id_type=MESH).start()
        token[...] = jnp.zeros_like(token)

    hbm = lambda a: pltpu.with_memory_space_constraint(a, pltpu.HBM)
    res = pl.pallas_call(
        body, name=name,
        in_specs=[HBM] * (2 * n),
        out_specs=[SEM] * (2 * n) + [HBM] * (2 * n) + [pl.BlockSpec(memory_space=pltpu.VMEM)],
        out_shape=[pltpu.SemaphoreType.DMA(())] * (2 * n) + [pltpu.HBM(s.shape, s.dtype) for s in sbs] * 2
        + [jax.ShapeDtypeStruct((8, LANES), F32)],
        input_output_aliases={k: 2 * n + k for k in range(2 * n)},
        compiler_params=pltpu.CompilerParams(has_side_effects=SIDE_EFFECT),
    )(*[hbm(s) for s in sbs], *[hbm(lax.empty(s.shape, s.dtype)) for s in sbs])
    return res[:n], res[n:2 * n], res[2 * n:3 * n], res[3 * n:4 * n], res[4 * n]


def _send_wait(send_sems, recv_sems, srcs, lands, after, name):
    n = len(srcs)

    def body(*refs):
        ss, rs = refs[2 * n:3 * n], refs[3 * n:4 * n]
        s_out, l_out = refs[4 * n + 1:5 * n + 1], refs[5 * n + 1:]
        x, y, c = _place()
        for k in range(n):
            cp = pltpu.make_async_remote_copy(src_ref=s_out[k].at[pl.ds(0, N_CHIPS - 1)],
                                              dst_ref=l_out[k].at[pl.ds(0, N_CHIPS - 1)], send_sem=ss[k],
                                              recv_sem=rs[k], device_id=(x, y, c), device_id_type=MESH)
            cp.wait_send()
            cp.wait_recv()

    res = pl.pallas_call(
        body, name=name,
        in_specs=[HBM] * (2 * n) + [SEM] * (2 * n) + [ANY],
        out_specs=[HBM] * (2 * n),
        out_shape=[pltpu.HBM(s.shape, s.dtype) for s in srcs] * 2,
        input_output_aliases={k: k for k in range(2 * n)},
        compiler_params=pltpu.CompilerParams(has_side_effects=SIDE_EFFECT),
    )(*srcs, *lands, *send_sems, *recv_sems, after)
    return res[n:]


def _pass_to_sibling(bufs, name):
    n = len(bufs)
    halves = [b.shape[1] // 2 for b in bufs]

    def body(*refs):
        outs = refs[n:2 * n]
        send_sems, recv_sems = refs[2 * n:]
        x, y, c = _place()
        chips = _other_chips(x, y)

        def copy(k, j, core):
            px, py = chips[j]
            blk = outs[k].at[2 * px + py, _half_rows(core, halves[k])]
            return pltpu.make_async_remote_copy(src_ref=blk, dst_ref=blk, send_sem=send_sems.at[3 * k + j],
                                                recv_sem=recv_sems.at[3 * k + j], device_id=(x, y, 1 - c),
                                                device_id_type=MESH)

        sends = [copy(k, j, c) for k in range(n) for j in range(3)]
        for cp in sends:
            cp.start()
        for k in range(n):
            for j in range(3):
                copy(k, j, 1 - c).wait_recv()
        for cp in sends:
            cp.wait_send()

    return pl.pallas_call(
        body, name=name,
        in_specs=[ANY] * n, out_specs=[ANY] * n,
        out_shape=[jax.ShapeDtypeStruct(b.shape, b.dtype) for b in bufs],
        input_output_aliases={k: k for k in range(n)},
        scratch_shapes=[pltpu.SemaphoreType.DMA((3 * n,)), pltpu.SemaphoreType.DMA((3 * n,))],
    )(*bufs)


def _rope_tables(S):
    rows = S // GRID_W
    row_idx = jnp.repeat(jnp.arange(rows, dtype=F32), GRID_W)
    col_idx = jnp.tile(jnp.arange(GRID_W, dtype=F32), rows)
    pairs = ATT_HD // 4
    inv_freq = ROPE_THETA ** (-jnp.arange(pairs, dtype=F32) / pairs)
    ang = jnp.concatenate([row_idx[:, None] * inv_freq, col_idx[:, None] * inv_freq], axis=-1)
    cos, sin = jnp.cos(ang), jnp.sin(ang)
    return jnp.concatenate([cos, cos], axis=-1), jnp.concatenate([-sin, sin], axis=-1)


def _gate_rows(w, first_row):
    return jnp.zeros((LANES, GLA_KEY), F32).at[first_row:first_row + GLA_RANK].set(w.astype(F32))


def _local_step(x, target, weights_of, grads_out, P):
    S = x.shape[0]
    rc, rs = _rope_tables(S)
    row = lambda a: a.reshape(1, -1)
    saved = []
    for i in range(DEPTH):
        j = i // 2
        W = dict(weights_of(i, "mix", x))
        nm = row(P["norm_mix"][i])
        h1 = _rmsnorm_fwd(x, nm, f"norm_mix_fwd{i}")
        if i % 2 == 0:
            wgf = _gate_rows(W["gate_f"], 0)
            wgb = _gate_rows(W["gate_b"], GLA_RANK)
            bgf, bgb = row(P["gla_b_gate_f"][j]), row(P["gla_b_gate_b"][j])
            gn = row(P["gla_norm"][j])
            proj = _matmul_rows(h1, W["gla_in"], f"gla_in{i}")
            laf, lab = _gla_gate_fwd(proj, wgf, bgf, wgb, bgb, f"gla_gate_fwd{i}")
            of, stf = _gla_scan_fwd(proj, laf, False, f"gla_scan_f_fwd{i}")
            ob, stb = _gla_scan_fwd(proj, lab, True, f"gla_scan_b_fwd{i}")
            z = _gla_out_fwd(of, ob, proj, gn, f"gla_out_fwd{i}")
            xm = _matmul_rows(z, W["gla_out"], f"gla_outproj{i}", res=x)
            mix = dict(proj=proj, laf=laf, lab=lab, of=of, ob=ob, stf=stf, stb=stb, z=z, wgf=wgf, wgb=wgb)
        else:
            proj = _matmul_rows(h1, W["qkv"], f"attn_qkv{i}", w_layer=0)
            qn, kn = row(P["attn_q_norm"][j]), row(P["attn_k_norm"][j])
            qk, vb, kt, vt = _qk_prep_fwd(proj, qn, kn, rc, rs, f"qk_prep_fwd{i}")
            o, lse = _attn_fwd(qk, vt, f"attn_fwd{i}")
            xm = _matmul_rows(o, W["attn_out"], f"attn_outproj{i}", res=x)
            mix = dict(proj=proj, qk=qk, vb=vb, kt=kt, o=o, lse=lse)
        W.update(weights_of(i, "ffn", xm))
        h2 = _rmsnorm_fwd(xm, row(P["norm_ffn"][i]), f"norm_ffn_fwd{i}")
        a, uv, ug = _ffn_mid_fwd(h2, W["up_full"], W["conv"], row(P["ffn_b_conv"][i]), f"ffn_mid_fwd{i}")
        xo = _matmul_rows(a, W["down"], f"ffn_down{i}", res=xm)
        saved.append(dict(x=x, h1=h1, xm=xm, h2=h2, uv=uv, ug=ug, mix=mix, W=W))
        x = xo

    dx, dxb, loss = _loss_grad(x, target, "loss")

    G = {n: [None] * (DEPTH if n.startswith(("norm", "ffn")) else DEPTH // 2) for n in REPLICATED}
    token = None
    for i in reversed(range(DEPTH)):
        j = i // 2
        sv = saved[i]
        mix = sv["mix"]
        W = sv["W"]
        bconv = row(P["ffn_b_conv"][i])
        if token is not None:
            bconv, _ = lax.optimization_barrier((bconv, token))
        duv, dug, a, gwv, gwg = _ffn_mid_bwd(dxb, W["down"], sv["uv"], sv["ug"], W["conv"], bconv, f"ffn_mid_bwd{i}")
        L = dict(down=_matmul(a, dxb, 0, 0, f"ffn_down_wgrad{i}", out_chips=("rows", 0, 1)),
                 up=_matmul(sv["h2"], (duv, dug), 0, 0, f"ffn_up_wgrad{i}", out_chips=("cols", 0, 1)),
                 small=[jnp.concatenate([gwv[:3], gwg[:3]], axis=1)])
        G["ffn_b_conv"][i] = jnp.concatenate([gwv[3], gwg[3]], axis=0)
        dxm, dxmb, dn = _dgrad_norm((duv, dug), W["up"], sv["xm"], row(P["norm_ffn"][i]), dx, f"ffn_up_dgrad{i}",
                                    w_layer=0)
        G["norm_ffn"][i] = dn[0]
        if i % 2 == 0:
            proj = mix["proj"]
            bgf, bgb = row(P["gla_b_gate_f"][j]), row(P["gla_b_gate_b"][j])
            gn = row(P["gla_norm"][j])
            dz = _matmul_rows(dxmb, W["gla_out"], f"gla_outproj_dgrad{i}", transposed=True)
            L["out"] = _matmul(mix["z"], dxmb, 0, 0, f"gla_outproj_wgrad{i}", out_chips=("rows", 0, 1))
            do, dg, dgn = _gla_out_bwd(dz, mix["of"], mix["ob"], proj, gn, f"gla_out_bwd{i}")
            G["gla_norm"][j] = dgn[0]
            dqf, dkf, dvf, dlaf = _gla_scan_bwd(do, proj, mix["laf"], mix["stf"], False, f"gla_scan_f_bwd{i}")
            dqb, dkb, dvb, dlab = _gla_scan_bwd(do, proj, mix["lab"], mix["stb"], True, f"gla_scan_b_bwd{i}")
            dr, dwf, dbf, dwb, dbb = _gla_gate_bwd(dlaf, dlab, proj, mix["wgf"], bgf, mix["wgb"], bgb,
                                                   f"gla_gate_bwd{i}")
            L["small"] = [dwf[:GLA_RANK], dwb[GLA_RANK:2 * GLA_RANK]] + L["small"]
            G["gla_b_gate_f"][j] = dbf[0]
            G["gla_b_gate_b"][j] = dbb[0]
            dproj = jnp.concatenate([dqf + dqb, dkf + dkb, dvf + dvb, dg, dr], axis=1).astype(BF16)
            L["mix_in"] = _matmul(sv["h1"], dproj, 0, 0, f"gla_in_wgrad{i}")
            dx, dxb, dn = _dgrad_norm(dproj, W["gla_in"], sv["x"], row(P["norm_mix"][i]), dxm, f"mix_in_dgrad{i}")
        else:
            proj = mix["proj"]
            qn, kn = row(P["attn_q_norm"][j]), row(P["attn_k_norm"][j])
            do = _matmul_rows(dxmb, W["attn_out"], f"attn_outproj_dgrad{i}", transposed=True)
            L["out"] = _matmul(mix["o"], dxmb, 0, 0, f"attn_outproj_wgrad{i}", out_chips=("rows", 0, 1))
            dq, dk, dv = _attn_bwd(do, mix["o"], mix["lse"], mix["qk"], mix["vb"], mix["kt"], f"attn_bwd{i}")
            dqk = jnp.concatenate([dq, dk], axis=1)
            dpqk, dqn, dkn = _qk_prep_bwd(dqk, proj, qn, kn, rc, rs, f"qk_prep_bwd{i}")
            G["attn_q_norm"][j] = dqn[0]
            G["attn_k_norm"][j] = dkn[0]
            dproj = jnp.concatenate([dpqk, dv], axis=1).astype(BF16)
            L["mix_in"] = _matmul(sv["h1"], dproj, 0, 0, f"attn_qkv_wgrad{i}", out_chips=("cols", 0, 1))
            dx, dxb, dn = _dgrad_norm(dproj, W["qkv"], sv["x"], row(P["norm_mix"][i]), dxm, f"mix_in_dgrad{i}",
                                      w_layer=0)
        G["norm_mix"][i] = dn[0]
        token = grads_out(i, L)
    return loss, dx, G


def kernel(x, norm_mix, norm_ffn, gla_w_in, gla_w_gate_up_f, gla_b_gate_f, gla_w_gate_up_b, gla_b_gate_b, gla_norm, gla_w_out, attn_w_qkv, attn_q_norm, attn_k_norm, attn_w_out, ffn_w_up, ffn_w_conv, ffn_b_conv, ffn_w_down, loss_target, m_norm_mix, m_norm_ffn, m_gla_w_in, m_gla_w_gate_up_f, m_gla_b_gate_f, m_gla_w_gate_up_b, m_gla_b_gate_b, m_gla_norm, m_gla_w_out, m_attn_w_qkv, m_attn_q_norm, m_attn_k_norm, m_attn_w_out, m_ffn_w_up, m_ffn_w_conv, m_ffn_b_conv, m_ffn_w_down, v_norm_mix, v_norm_ffn, v_gla_w_in, v_gla_w_gate_up_f, v_gla_b_gate_f, v_gla_w_gate_up_b, v_gla_b_gate_b, v_gla_norm, v_gla_w_out, v_attn_w_qkv, v_attn_q_norm, v_attn_k_norm, v_attn_w_out, v_ffn_w_up, v_ffn_w_conv, v_ffn_b_conv, v_ffn_w_down):
    names = ("norm_mix", "norm_ffn", "gla_w_in", "gla_w_gate_up_f", "gla_b_gate_f", "gla_w_gate_up_b",
             "gla_b_gate_b", "gla_norm", "gla_w_out", "attn_w_qkv", "attn_q_norm", "attn_k_norm", "attn_w_out",
             "ffn_w_up", "ffn_w_conv", "ffn_b_conv", "ffn_w_down")
    w = dict(zip(names, (norm_mix, norm_ffn, gla_w_in, gla_w_gate_up_f, gla_b_gate_f, gla_w_gate_up_b,
                         gla_b_gate_b, gla_norm, gla_w_out, attn_w_qkv, attn_q_norm, attn_k_norm, attn_w_out,
                         ffn_w_up, ffn_w_conv, ffn_b_conv, ffn_w_down)))
    m = dict(zip(names, (m_norm_mix, m_norm_ffn, m_gla_w_in, m_gla_w_gate_up_f, m_gla_b_gate_f,
                         m_gla_w_gate_up_b, m_gla_b_gate_b, m_gla_norm, m_gla_w_out, m_attn_w_qkv, m_attn_q_norm,
                         m_attn_k_norm, m_attn_w_out, m_ffn_w_up, m_ffn_w_conv, m_ffn_b_conv, m_ffn_w_down)))
    v = dict(zip(names, (v_norm_mix, v_norm_ffn, v_gla_w_in, v_gla_w_gate_up_f, v_gla_b_gate_f,
                         v_gla_w_gate_up_b, v_gla_b_gate_b, v_gla_norm, v_gla_w_out, v_attn_w_qkv, v_attn_q_norm,
                         v_attn_k_norm, v_attn_w_out, v_ffn_w_up, v_ffn_w_conv, v_ffn_b_conv, v_ffn_w_down)))
    px, py, pc = _place()
    me = 2 * px + py

    started, token = [], w["norm_mix"]
    for l in range(DEPTH):
        started.append(_gather_start(_layer_weight_bufs(w, l, me), token, f"gather_start{l}"))
        token = started[-1][3]
    fetched = {}

    def weights_of(l, part, after):
        send_sems, recv_sems, thru, _ = started[l]
        if l == 0:
            pick = slice(0, N_MIXER_BUFS) if part == "mix" else slice(N_MIXER_BUFS, None)
            landed = _gather_wait(send_sems[pick], recv_sems[pick], thru[pick], token if part == "mix" else after,
                                  f"gather_wait{l}_{part}")
            return _layer_weights(w, l, _pass_to_sibling(landed, f"gather_pass{l}_{part}"))
        if part == "mix":
            landed = _gather_wait(send_sems, recv_sems, thru, after, f"gather_wait{l}")
            fetched[l] = _layer_weights(w, l, _pass_to_sibling(landed, f"gather_pass{l}"))
        return fetched[l]

    sent = {}

    def grads_out(l, L):
        mix_in = L["mix_in"]
        if l % 2 == 0:
            width = w["gla_w_in"].shape[2]
            mix_in = jnp.stack([mix_in[:, p * width:(p + 1) * width] for p in range(N_CHIPS)])
        cut = lambda t, p: lax.slice_in_dim(t, p * (t.shape[-1] // N_CHIPS), (p + 1) * (t.shape[-1] // N_CHIPS),
                                            axis=t.ndim - 1)
        small = jnp.stack([_pack([cut(t, p) for t in L["small"]], F32, 32) for p in range(N_CHIPS)])
        bufs = [mix_in, L["out"], small, L["up"], L["down"]]
        gots = _swap_other_half(bufs, f"grads{l}_to_sibling")
        sums = [_add_sibling(b, g, pc, me, f"grads{l}_add_sibling{k}") for k, (b, g) in enumerate(zip(bufs, gots))]
        send_sems, recv_sems, srcs, lands, tok = _send_start([s[0] for s in sums], f"grads{l}_start")
        sent[l] = (send_sems, recv_sems, srcs, lands, [s[1] for s in sums])
        return tok

    P = {n: w[n] for n in REPLICATED}

    loss_part, dx, grads = _local_step(x[0], loss_target[0], weights_of, grads_out, P)

    others_and_c = jnp.stack([jnp.where(me <= k, k + 1, k) for k in range(N_CHIPS - 1)] + [pc])
    mine = {}
    for l in reversed(range(DEPTH)):
        send_sems, recv_sems, srcs, lands, own = sent[l]
        landed = _send_wait(send_sems, recv_sems, srcs, lands, dx, f"grads{l}_wait")
        halves = [_add_chips(own[k], landed[k], others_and_c, f"grads{l}_add_chips{k}") for k in range(len(own))]
        mine[l] = _join_halves(halves, f"grads{l}_join_halves")
    gsh = {}
    for n, k, layers in (("ffn_w_up", 3, range(DEPTH)), ("ffn_w_down", 4, range(DEPTH)),
                         ("gla_w_in", 0, range(0, DEPTH, 2)), ("gla_w_out", 1, range(0, DEPTH, 2)),
                         ("attn_w_qkv", 0, range(1, DEPTH, 2)), ("attn_w_out", 1, range(1, DEPTH, 2))):
        gsh[n] = jnp.stack([mine[l][k] for l in layers])
    small_mine = [_unpack(mine[l][2], [t.shape for t in _layer_small(w, l)]) for l in range(DEPTH)]
    gsh["ffn_w_conv"] = jnp.stack([small_mine[l][-1] for l in range(DEPTH)])
    gsh["gla_w_gate_up_f"] = jnp.stack([small_mine[l][0] for l in range(0, DEPTH, 2)])
    gsh["gla_w_gate_up_b"] = jnp.stack([small_mine[l][1] for l in range(0, DEPTH, 2)])

    small = _pack([jnp.stack(grads[n]) for n in REPLICATED] + [loss_part], F32, 16)
    small_sum = _allreduce_small(small, "small_allreduce")
    parts = _unpack(small_sum, [w[n].shape for n in REPLICATED] + [(1, LANES)])
    gsh.update(dict(zip(REPLICATED, parts[:-1])))
    loss = parts[-1][0, 0]

    delta, new_m, new_v = {}, {}, {}
    for n in names:
        shp = w[n].shape
        two_d = (-1, shp[-1])
        d, nm, nv = _adamw(w[n].reshape(two_d), gsh[n].reshape(two_d), m[n].reshape(two_d), v[n].reshape(two_d),
                           f"adamw_{n}")
        delta[n], new_m[n], new_v[n] = d.reshape(shp), nm.reshape(shp), nv.reshape(shp)

    return (loss, dx[None], *[gsh[n] for n in names], *[delta[n] for n in names],
            *[new_m[n] for n in names], *[new_v[n] for n in names])
```

```python
import jax
import jax.numpy as jnp
from jax import lax
from jax.experimental import pallas as pl
from jax.experimental.pallas import tpu as pltpu

F32 = jnp.float32
BF16 = jnp.bfloat16
MESH = pl.DeviceIdType.MESH
HIGHEST = lax.Precision.HIGHEST

D_MODEL = 1024
DEPTH = 4
GRID_W = 64
NORM_EPS = 1e-6
GLA_HEADS = 4
GLA_DK = 128
GLA_DV = 256
GLA_KEY = GLA_HEADS * GLA_DK
GLA_VAL = GLA_HEADS * GLA_DV
GLA_RANK = 16
GLA_CHUNK = 64
GLA_GATE_NORMALIZER = 16.0
GLA_IN = 2 * GLA_KEY + 2 * GLA_VAL + 2 * GLA_RANK
GLA_IN_PAD = 3200
GLA_R_BLOCK = (2 * GLA_KEY + 2 * GLA_VAL) // 128
ATT_HD = 128
ATT_QH = 8
ATT_KVH = 2
ATT_GROUP = ATT_QH // ATT_KVH
ATT_QKV = (ATT_QH + 2 * ATT_KVH) * ATT_HD
ROPE_THETA = 10000.0
D_FF = 2816
ADAM_LR = 0.001
ADAM_B1 = 0.9
ADAM_B2 = 0.999
ADAM_EPS = 1e-08
ADAM_WD = 0.01
ADAM_STEP = 10

N_CHIPS = 4
LANES = 128
VMEM_LIMIT = 56 * 1024 * 1024


def _cp(sem):
    return pltpu.CompilerParams(dimension_semantics=sem, vmem_limit_bytes=VMEM_LIMIT)


def _pick(n, cands):
    for c in cands:
        if n % c == 0:
            return c
    return n


def _dg(a, b, ca, cb):
    return lax.dot_general(a, b, (((ca,), (cb,)), ((), ())), preferred_element_type=F32)


def _sigmoid(x):
    return 0.5 * jnp.tanh(0.5 * x) + 0.5


def _rmsnorm_fwd(x, w, name):
    S, D = x.shape
    tm = _pick(S, (512, 256))

    def body(x_ref, w_ref, h_ref):
        xv = x_ref[...]
        r = lax.rsqrt(jnp.mean(xv * xv, axis=-1, keepdims=True) + NORM_EPS)
        h_ref[...] = (xv * r * w_ref[...]).astype(BF16)

    return pl.pallas_call(
        body, name=name, grid=(S // tm,),
        in_specs=[pl.BlockSpec((tm, D), lambda i: (i, 0)), pl.BlockSpec((1, D), lambda i: (0, 0))],
        out_specs=pl.BlockSpec((tm, D), lambda i: (i, 0)),
        out_shape=jax.ShapeDtypeStruct((S, D), BF16),
        compiler_params=_cp(("parallel",)),
    )(x, w)


def _loss_grad(y, t, name):
    S, D = y.shape
    tm = _pick(S, (512, 256))

    def body(y_ref, t_ref, dy_ref, dyb_ref, loss_ref):
        i = pl.program_id(0)
        d = y_ref[...] - t_ref[...]
        dy = d * (1.0 / D)
        dy_ref[...] = dy
        dyb_ref[...] = dy.astype(BF16)
        sq = jnp.sum(jnp.sum(d * d, axis=1, keepdims=True), axis=0, keepdims=True)
        part = jnp.broadcast_to(sq * (0.5 / D), (1, LANES))

        @pl.when(i == 0)
        def _():
            loss_ref[...] = part

        @pl.when(i > 0)
        def _():
            loss_ref[...] += part

    return pl.pallas_call(
        body, name=name, grid=(S // tm,),
        in_specs=[pl.BlockSpec((tm, D), lambda i: (i, 0)), pl.BlockSpec((tm, D), lambda i: (i, 0))],
        out_specs=[pl.BlockSpec((tm, D), lambda i: (i, 0)), pl.BlockSpec((tm, D), lambda i: (i, 0)),
                   pl.BlockSpec((1, LANES), lambda i: (0, 0))],
        out_shape=[jax.ShapeDtypeStruct((S, D), F32), jax.ShapeDtypeStruct((S, D), BF16),
                   jax.ShapeDtypeStruct((1, LANES), F32)],
        compiler_params=_cp(("arbitrary",)),
    )(y, t)


def _matmul(a, b, ca, cb, name, res=None, out_dtype=F32, out_chips=None, into=None):
    M, K = a.shape[1 - ca], a.shape[ca]
    pair = isinstance(b, (tuple, list))
    if pair:
        assert cb == 0 and b[0].shape == b[1].shape and b[0].shape[0] == K
        N = 2 * b[0].shape[1]
    else:
        assert b.shape[cb] == K
        N = b.shape[1 - cb]
    how = out_chips[0] if out_chips else None
    tm = M if how == "rows" else _pick(M, (1024, 1408, 512, 256, 128))
    if how == "cols":
        tn = N // N_CHIPS
    else:
        tn = _pick(N // 2 if pair else N, (1024, 1408, 768, 640, 512, 256, 128))
    tk = _pick(K, (512, 1408, 256, 128))
    nk = K // tk
    n0 = (N // 2) // tn
    if ca == 1:
        a_spec = pl.BlockSpec((tm, tk), lambda i, j, k: (i, k))
    else:
        a_spec = pl.BlockSpec((tk, tm), lambda i, j, k: (k, i))
    if pair:
        b_specs = [pl.BlockSpec((tk, tn), lambda i, j, k: (k, jnp.minimum(j, n0 - 1))),
                   pl.BlockSpec((tk, tn), lambda i, j, k: (k, jnp.maximum(j - n0, 0)))]
    elif cb == 0:
        b_specs = [pl.BlockSpec((tk, tn), lambda i, j, k: (k, j))]
    else:
        b_specs = [pl.BlockSpec((tn, tk), lambda i, j, k: (j, k))]
    if how == "cols":
        _, layer, layers = out_chips
        o_spec = pl.BlockSpec((None, tm, tn), lambda i, j, k: (j, layer * (M // tm) + i, 0))
        out_shape = jax.ShapeDtypeStruct((N_CHIPS, layers * M, tn), out_dtype)
    elif how == "rows":
        _, layer, layers = out_chips
        o_spec = pl.BlockSpec((N_CHIPS, M // N_CHIPS, tn), lambda i, j, k: (0, layer, j))
        out_shape = jax.ShapeDtypeStruct((N_CHIPS, layers * M // N_CHIPS, N), out_dtype)
    else:
        o_spec = pl.BlockSpec((tm, tn), lambda i, j, k: (i, j))
        out_shape = jax.ShapeDtypeStruct((M, N), out_dtype)
    has_res = res is not None
    nb = len(b_specs)

    def body(*refs):
        a_ref, b_refs = refs[0], refs[1:1 + nb]
        r_ref = refs[1 + nb] if has_res else None
        o_ref, acc = refs[-2], refs[-1]
        j = pl.program_id(1)
        k = pl.program_id(2)

        @pl.when(k == 0)
        def _():
            acc[...] = jnp.zeros_like(acc)

        av = a_ref[...].astype(BF16)
        if pair:
            @pl.when(j < n0)
            def _():
                acc[...] += _dg(av, b_refs[0][...].astype(BF16), ca, cb)

            @pl.when(j >= n0)
            def _():
                acc[...] += _dg(av, b_refs[1][...].astype(BF16), ca, cb)
        else:
            acc[...] += _dg(av, b_refs[0][...].astype(BF16), ca, cb)

        @pl.when(k == nk - 1)
        def _():
            v = acc[...]
            if has_res:
                v = v + r_ref[...]
            if how == "rows":
                rows = M // N_CHIPS
                for p in range(N_CHIPS):
                    o_ref[p] = v[p * rows:(p + 1) * rows, :].astype(out_dtype)
            else:
                o_ref[...] = v.astype(out_dtype)

    in_specs = [a_spec] + b_specs + ([o_spec] if has_res else [])
    args = (a,) + (tuple(b) if pair else (b,)) + ((res,) if has_res else ())
    aliases = {}
    if into is not None:
        assert into.shape == out_shape.shape
        in_specs.append(ANY)
        aliases = {len(args): 0}
        args = args + (into,)
        inner = body

        def body(*refs):
            inner(*refs[:len(args) - 1], *refs[len(args):])

    return pl.pallas_call(
        body, name=name, grid=(M // tm, N // tn, nk),
        in_specs=in_specs, out_specs=o_spec, out_shape=out_shape,
        input_output_aliases=aliases,
        scratch_shapes=[pltpu.VMEM((tm, tn), F32)],
        compiler_params=_cp(("parallel", "parallel", "arbitrary")),
    )(*args)


def _matmul_rows(a, w, name, res=None, w_layer=None, transposed=False):
    M, K = a.shape
    if w_layer is not None:
        cw = w.shape[2]
        N = N_CHIPS * cw
        w_spec = pl.BlockSpec((N_CHIPS, K, cw), lambda i: (0, w_layer, 0))
    else:
        N = w.shape[0] if transposed else w.shape[1]
        assert w.shape[1 if transposed else 0] == K
        w_spec = pl.BlockSpec(w.shape, lambda i: (0, 0))
    tm = _pick(M, (512, 256, 128))
    has_res = res is not None

    def body(*refs):
        a_ref, w_ref = refs[0], refs[1]
        r_ref = refs[2] if has_res else None
        o_ref = refs[-1]
        av = a_ref[...].astype(BF16)
        if w_layer is not None:
            for p in range(N_CHIPS):
                o_ref[:, pl.ds(p * cw, cw)] = jnp.dot(av, w_ref[p], preferred_element_type=F32)
        else:
            v = _dg(av, w_ref[...], 1, 1 if transposed else 0)
            o_ref[...] = v + r_ref[...] if has_res else v

    row = pl.BlockSpec((tm, N), lambda i: (i, 0))
    return pl.pallas_call(
        body, name=name, grid=(M // tm,),
        in_specs=[pl.BlockSpec((tm, K), lambda i: (i, 0)), w_spec] + ([row] if has_res else []),
        out_specs=row, out_shape=jax.ShapeDtypeStruct((M, N), F32),
        compiler_params=_cp(("parallel",)),
    )(*((a, w) + ((res,) if has_res else ())))


def _dgrad_norm(dy, w, x, wn, dres, name, w_layer=None):
    pair = isinstance(dy, (tuple, list))
    M = dy[0].shape[0] if pair else dy.shape[0]
    Kp = 2 * dy[0].shape[1] if pair else dy.shape[1]
    D = x.shape[1]
    if w_layer is not None:
        cw = w.shape[2]
        assert N_CHIPS * cw == Kp and w.shape[1] % D == 0
        w_spec = pl.BlockSpec((N_CHIPS, D, cw), lambda i: (0, w_layer, 0))
    else:
        assert w.shape == (D, Kp)
        w_spec = pl.BlockSpec((D, Kp), lambda i: (0, 0))
    tm = _pick(M, (256, 128))
    width = Kp // 2 if pair else Kp
    dy_specs = [pl.BlockSpec((tm, width), lambda i: (i, 0))] * (2 if pair else 1)
    nd = len(dy_specs)

    def body(*refs):
        dy_refs = refs[:nd]
        w_ref, x_ref, wn_ref, dres_ref, dx_ref, dxb_ref, dwn_ref = refs[nd:]
        i = pl.program_id(0)
        if w_layer is not None:
            dh = None
            for p in range(N_CHIPS):
                src, off = divmod(p * cw, width)
                part = _dg(dy_refs[src][:, pl.ds(off, cw)], w_ref[p], 1, 1)
                dh = part if dh is None else dh + part
        else:
            dh = _dg(dy_refs[0][...], w_ref[...], 1, 1)
        xv = x_ref[...]
        r = lax.rsqrt(jnp.mean(xv * xv, axis=-1, keepdims=True) + NORM_EPS)
        yv = xv * r
        dyv = dh * wn_ref[...]
        dxv = r * (dyv - yv * jnp.mean(dyv * yv, axis=-1, keepdims=True)) + dres_ref[...]
        dx_ref[...] = dxv
        dxb_ref[...] = dxv.astype(BF16)
        part = jnp.sum(dh * yv, axis=0, keepdims=True)

        @pl.when(i == 0)
        def _():
            dwn_ref[...] = part

        @pl.when(i > 0)
        def _():
            dwn_ref[...] += part

    row = pl.BlockSpec((tm, D), lambda i: (i, 0))
    one = pl.BlockSpec((1, D), lambda i: (0, 0))
    return pl.pallas_call(
        body, name=name, grid=(M // tm,),
        in_specs=dy_specs + [w_spec, row, one, row],
        out_specs=[row, row, one],
        out_shape=[jax.ShapeDtypeStruct((M, D), F32), jax.ShapeDtypeStruct((M, D), BF16),
                   jax.ShapeDtypeStruct((1, D), F32)],
        compiler_params=_cp(("arbitrary",)),
    )(*(tuple(dy) if pair else (dy,)), w, x, wn, dres)


FFN_TN_FWD = 256
FFN_TN_BWD = 128
FFN_ROWS = 256
PAD = 8


def _conv3(pad_ref, w, r0, tr):
    um = pad_ref[pl.ds(PAD - 1 + r0, tr), :]
    uc = pad_ref[pl.ds(PAD + r0, tr), :]
    up = pad_ref[pl.ds(PAD + 1 + r0, tr), :]
    return w[0:1, :] * um + w[1:2, :] * uc + w[2:3, :] * up, (um, uc, up)


def _zero_pads(pad_ref, S, tn):
    pad_ref[pl.ds(0, PAD), :] = jnp.zeros((PAD, tn), F32)
    pad_ref[pl.ds(PAD + S, PAD), :] = jnp.zeros((PAD, tn), F32)


def _ffn_mid_fwd(h, wup, wconv, bconv, name):
    S, D = h.shape
    F = wup.shape[1] // 2
    tn = FFN_TN_FWD
    nb = F // tn
    tr = min(FFN_ROWS, S)

    def body(h_ref, wv_ref, wg_ref, cv_ref, cg_ref, bv_ref, bg_ref, a_ref, uv_ref, ug_ref):
        _zero_pads(uv_ref, S, tn)
        _zero_pads(ug_ref, S, tn)
        hv = h_ref[...]
        uv_ref[pl.ds(PAD, S), :] = jnp.dot(hv, wv_ref[...], preferred_element_type=F32)
        ug_ref[pl.ds(PAD, S), :] = jnp.dot(hv, wg_ref[...], preferred_element_type=F32)
        cwv, cwg, bv, bg = cv_ref[...], cg_ref[...], bv_ref[...], bg_ref[...]
        for r0 in range(0, S, tr):
            cv = _conv3(uv_ref, cwv, r0, tr)[0] + bv
            cg = _conv3(ug_ref, cwg, r0, tr)[0] + bg
            a_ref[pl.ds(r0, tr), :] = (cg * _sigmoid(cg) * cv).astype(BF16)

    col = lambda off: (lambda j: (0, j + off))
    padded = pl.BlockSpec((S + 2 * PAD, tn), col(0))
    return pl.pallas_call(
        body, name=name, grid=(nb,),
        in_specs=[pl.BlockSpec((S, D), lambda j: (0, 0)),
                  pl.BlockSpec((D, tn), col(0)), pl.BlockSpec((D, tn), col(nb)),
                  pl.BlockSpec((3, tn), col(0)), pl.BlockSpec((3, tn), col(nb)),
                  pl.BlockSpec((1, tn), col(0)), pl.BlockSpec((1, tn), col(nb))],
        out_specs=[pl.BlockSpec((S, tn), col(0)), padded, padded],
        out_shape=[jax.ShapeDtypeStruct((S, F), BF16), jax.ShapeDtypeStruct((S + 2 * PAD, F), F32),
                   jax.ShapeDtypeStruct((S + 2 * PAD, F), F32)],
        compiler_params=_cp(("parallel",)),
    )(h, wup, wup, wconv, wconv, bconv, bconv)


def _rows8(rows):
    n = rows[0].shape[1]
    idx = lax.broadcasted_iota(jnp.int32, (8, n), 0)
    out = jnp.zeros((8, n), F32)
    for k, r in enumerate(rows):
        out = jnp.where(idx == k, r, out)
    return out


def _ffn_mid_bwd(dyb, wdown, uv, ug, wconv, bconv, name):
    S, D = dyb.shape
    F = wdown.shape[0]
    tn = FFN_TN_BWD
    nb = F // tn
    tr = min(FFN_ROWS, S)

    def body(dy_ref, wd_ref, uv_ref, ug_ref, cv_ref, cg_ref, bv_ref, bg_ref,
             duv_ref, dug_ref, a_ref, gwv_ref, gwg_ref, pdv, pdg):
        for p in (pdv, pdg):
            _zero_pads(p, S, tn)
        wd = wd_ref[...]
        cwv, cwg, bv, bg = cv_ref[...], cg_ref[...], bv_ref[...], bg_ref[...]
        zero = jnp.zeros((1, tn), F32)
        gv = [zero, zero, zero, zero]
        gg = [zero, zero, zero, zero]
        for r0 in range(0, S, tr):
            cv, shv = _conv3(uv_ref, cwv, r0, tr)
            cg, shg = _conv3(ug_ref, cwg, r0, tr)
            cv = cv + bv
            cg = cg + bg
            sg = _sigmoid(cg)
            sl = cg * sg
            a_ref[pl.ds(r0, tr), :] = (sl * cv).astype(BF16)
            da = _dg(dy_ref[pl.ds(r0, tr), :], wd, 1, 1)
            dcv = da * sl
            dcg = da * cv * (sg * (1.0 + cg * (1.0 - sg)))
            pdv[pl.ds(PAD + r0, tr), :] = dcv
            pdg[pl.ds(PAD + r0, tr), :] = dcg
            for k in range(3):
                gv[k] = gv[k] + jnp.sum(dcv * shv[k], axis=0, keepdims=True)
                gg[k] = gg[k] + jnp.sum(dcg * shg[k], axis=0, keepdims=True)
            gv[3] = gv[3] + jnp.sum(dcv, axis=0, keepdims=True)
            gg[3] = gg[3] + jnp.sum(dcg, axis=0, keepdims=True)
        gwv_ref[...] = _rows8(gv)
        gwg_ref[...] = _rows8(gg)
        for r0 in range(0, S, tr):
            for pd, cw, out in ((pdv, cwv, duv_ref), (pdg, cwg, dug_ref)):
                dm = pd[pl.ds(PAD - 1 + r0, tr), :]
                dc = pd[pl.ds(PAD + r0, tr), :]
                dp = pd[pl.ds(PAD + 1 + r0, tr), :]
                out[pl.ds(r0, tr), :] = (cw[0:1, :] * dp + cw[1:2, :] * dc + cw[2:3, :] * dm).astype(BF16)

    col = lambda off: (lambda j: (0, j + off))
    blk = pl.BlockSpec((S, tn), col(0))
    padded = pl.BlockSpec((S + 2 * PAD, tn), col(0))
    g8 = pl.BlockSpec((8, tn), col(0))
    return pl.pallas_call(
        body, name=name, grid=(nb,),
        in_specs=[pl.BlockSpec((S, D), lambda j: (0, 0)), pl.BlockSpec((tn, D), lambda j: (j, 0)), padded, padded,
                  pl.BlockSpec((3, tn), col(0)), pl.BlockSpec((3, tn), col(nb)),
                  pl.BlockSpec((1, tn), col(0)), pl.BlockSpec((1, tn), col(nb))],
        out_specs=[blk, blk, blk, g8, g8],
        out_shape=[jax.ShapeDtypeStruct((S, F), BF16), jax.ShapeDtypeStruct((S, F), BF16),
                   jax.ShapeDtypeStruct((S, F), BF16), jax.ShapeDtypeStruct((8, F), F32),
                   jax.ShapeDtypeStruct((8, F), F32)],
        scratch_shapes=[pltpu.VMEM((S + 2 * PAD, tn), F32)] * 2,
        compiler_params=_cp(("parallel",)),
    )(dyb, wdown, uv, ug, wconv, wconv, bconv, bconv)


def _log_sigmoid(x):
    return jnp.minimum(x, 0.0) - jnp.log(1.0 + jnp.exp(-jnp.abs(x)))


def _gla_gate_fwd(proj, wgf, bgf, wgb, bgb, name):
    S = proj.shape[0]
    tm = _pick(S, (512, 256))

    def body(r_ref, wf_ref, bf_ref, wb_ref, bb_ref, laf_ref, lab_ref):
        r = r_ref[...].astype(BF16)
        lf = jnp.dot(r, wf_ref[...].astype(BF16), preferred_element_type=F32) + bf_ref[...]
        lb = jnp.dot(r, wb_ref[...].astype(BF16), preferred_element_type=F32) + bb_ref[...]
        laf_ref[...] = _log_sigmoid(lf) * (1.0 / GLA_GATE_NORMALIZER)
        lab_ref[...] = _log_sigmoid(lb) * (1.0 / GLA_GATE_NORMALIZER)

    full = lambda shp: pl.BlockSpec(shp, lambda i: (0, 0))
    row = pl.BlockSpec((tm, GLA_KEY), lambda i: (i, 0))
    return pl.pallas_call(
        body, name=name, grid=(S // tm,),
        in_specs=[pl.BlockSpec((tm, LANES), lambda i: (i, GLA_R_BLOCK)),
                  full((LANES, GLA_KEY)), full((1, GLA_KEY)), full((LANES, GLA_KEY)), full((1, GLA_KEY))],
        out_specs=[row, row],
        out_shape=[jax.ShapeDtypeStruct((S, GLA_KEY), F32)] * 2,
        compiler_params=_cp(("parallel",)),
    )(proj, wgf, bgf, wgb, bgb)


def _gla_gate_bwd(dlaf, dlab, proj, wgf, bgf, wgb, bgb, name):
    S = proj.shape[0]
    tm = _pick(S, (512, 256))

    def body(dlf_ref, dlb_ref, r_ref, wf_ref, bf_ref, wb_ref, bb_ref, dr_ref, dwf_ref, dbf_ref, dwb_ref, dbb_ref):
        i = pl.program_id(0)
        r = r_ref[...].astype(BF16)
        wf = wf_ref[...].astype(BF16)
        wb = wb_ref[...].astype(BF16)
        lf = jnp.dot(r, wf, preferred_element_type=F32) + bf_ref[...]
        lb = jnp.dot(r, wb, preferred_element_type=F32) + bb_ref[...]
        glf = dlf_ref[...] * (1.0 / GLA_GATE_NORMALIZER) * (1.0 / (1.0 + jnp.exp(lf)))
        glb = dlb_ref[...] * (1.0 / GLA_GATE_NORMALIZER) * (1.0 / (1.0 + jnp.exp(lb)))
        gfb = glf.astype(BF16)
        gbb = glb.astype(BF16)
        dr_ref[...] = _dg(gfb, wf, 1, 1) + _dg(gbb, wb, 1, 1)
        parts = (_dg(r, gfb, 0, 0), jnp.sum(glf, axis=0, keepdims=True),
                 _dg(r, gbb, 0, 0), jnp.sum(glb, axis=0, keepdims=True))
        outs = (dwf_ref, dbf_ref, dwb_ref, dbb_ref)

        @pl.when(i == 0)
        def _():
            for o, p in zip(outs, parts):
                o[...] = p

        @pl.when(i > 0)
        def _():
            for o, p in zip(outs, parts):
                o[...] += p

    full = lambda shp: pl.BlockSpec(shp, lambda i: (0, 0))
    row = pl.BlockSpec((tm, GLA_KEY), lambda i: (i, 0))
    return pl.pallas_call(
        body, name=name, grid=(S // tm,),
        in_specs=[row, row, pl.BlockSpec((tm, LANES), lambda i: (i, GLA_R_BLOCK)),
                  full((LANES, GLA_KEY)), full((1, GLA_KEY)), full((LANES, GLA_KEY)), full((1, GLA_KEY))],
        out_specs=[pl.BlockSpec((tm, LANES), lambda i: (i, 0)),
                   full((LANES, GLA_KEY)), full((1, GLA_KEY)), full((LANES, GLA_KEY)), full((1, GLA_KEY))],
        out_shape=[jax.ShapeDtypeStruct((S, LANES), F32),
                   jax.ShapeDtypeStruct((LANES, GLA_KEY), F32), jax.ShapeDtypeStruct((1, GLA_KEY), F32),
                   jax.ShapeDtypeStruct((LANES, GLA_KEY), F32), jax.ShapeDtypeStruct((1, GLA_KEY), F32)],
        compiler_params=_cp(("arbitrary",)),
    )(dlaf, dlab, proj, wgf, bgf, wgb, bgb)


def _gla_masks(rev):
    C = GLA_CHUNK
    t = lax.broadcasted_iota(jnp.int32, (C, C), 0)
    s = lax.broadcasted_iota(jnp.int32, (C, C), 1)
    if rev:
        return (s >= t), (s > t), (t >= s), (t > s)
    return (s <= t), (s <= t), (t <= s), (t <= s)


def _cum_dot(cum, x):
    return jnp.dot(cum.astype(F32), x, precision=HIGHEST, preferred_element_type=F32)


def _gla_chunk_common(q, k, la, cum, end_row):
    b = _cum_dot(cum, la)
    bend = b[end_row:end_row + 1, :]
    e = jnp.exp(b)
    qd = q * (GLA_DK ** -0.5) * e
    ei = jnp.exp(-b)
    ee = jnp.exp(bend - b)
    d = jnp.exp(bend)
    return e, ei, ee, d, qd, k * ei, k * ee


GLA_CB = 8


def _gla_specs(S, rev_order):
    n = S // GLA_CHUNK
    cb = min(GLA_CB, n)
    nblk = n // cb
    rows = cb * GLA_CHUNK
    ci = (lambda i: nblk - 1 - i) if rev_order else (lambda i: i)
    q_spec = pl.BlockSpec((rows, GLA_DK), lambda h, i: (ci(i), h))
    k_spec = pl.BlockSpec((rows, GLA_DK), lambda h, i: (ci(i), GLA_HEADS + h))
    v_spec = pl.BlockSpec((rows, GLA_DV), lambda h, i: (ci(i), GLA_KEY * 2 // GLA_DV + h))
    la_spec = pl.BlockSpec((rows, GLA_DK), lambda h, i: (ci(i), h))
    o_spec = pl.BlockSpec((rows, GLA_DV), lambda h, i: (ci(i), h))
    st_spec = pl.BlockSpec((1, cb, GLA_DV, GLA_DK), lambda h, i: (h, ci(i), 0, 0))
    return n, cb, nblk, q_spec, k_spec, v_spec, la_spec, o_spec, st_spec


def _gla_scan_fwd(proj, la, rev, name):
    S = proj.shape[0]
    C = GLA_CHUNK
    n, cb, nblk, q_spec, k_spec, v_spec, la_spec, o_spec, st_spec = _gla_specs(S, rev)
    end_row = 0 if rev else C - 1
    order = list(range(cb))[::-1] if rev else list(range(cb))

    def body(q_ref, k_ref, v_ref, la_ref, o_ref, st_ref, state):
        i = pl.program_id(1)

        @pl.when(i == 0)
        def _():
            state[...] = jnp.zeros_like(state)

        cum, mask, _, _ = _gla_masks(rev)
        pre, intra, kv = {}, {}, {}
        for cc in order:
            rows = pl.ds(cc * C, C)
            q, k, v, lav = q_ref[rows, :], k_ref[rows, :], v_ref[rows, :], la_ref[rows, :]
            _, _, _, d, qd, ki, ke = _gla_chunk_common(q, k, lav, cum, end_row)
            qdb, kib, keb, vb = qd.astype(BF16), ki.astype(BF16), ke.astype(BF16), v.astype(BF16)
            pre[cc] = (d, qdb)
            att = jnp.where(mask, _dg(qdb, kib, 1, 1), 0.0)
            intra[cc] = jnp.dot(att.astype(BF16), vb, preferred_element_type=F32)
            kv[cc] = _dg(vb, keb, 0, 0)
        st = state[...]
        for cc in order:
            d, qdb = pre[cc]
            o_ref[pl.ds(cc * C, C), :] = intra[cc] + _dg(qdb, st.astype(BF16), 1, 1)
            st_ref[0, cc] = st
            st = st * d + kv[cc]
        state[...] = st

    return pl.pallas_call(
        body, name=name, grid=(GLA_HEADS, nblk),
        in_specs=[q_spec, k_spec, v_spec, la_spec],
        out_specs=[o_spec, st_spec],
        out_shape=[jax.ShapeDtypeStruct((S, GLA_VAL), F32),
                   jax.ShapeDtypeStruct((GLA_HEADS, n, GLA_DV, GLA_DK), F32)],
        scratch_shapes=[pltpu.VMEM((GLA_DV, GLA_DK), F32)],
        compiler_params=_cp(("parallel", "arbitrary")),
    )(proj, proj, proj, la)


def _gla_scan_bwd(do, proj, la, states, rev, name):
    S = proj.shape[0]
    C = GLA_CHUNK
    n, cb, nblk, q_spec, k_spec, v_spec, la_spec, o_spec, st_spec = _gla_specs(S, not rev)
    end_row = 0 if rev else C - 1
    order = list(range(cb)) if rev else list(range(cb))[::-1]

    def body(do_ref, q_ref, k_ref, v_ref, la_ref, st_ref, dq_ref, dk_ref, dv_ref, dla_ref, gstate):
        i = pl.program_id(1)

        @pl.when(i == 0)
        def _():
            gstate[...] = jnp.zeros_like(gstate)

        cum, mask, cum_t, mask_t = _gla_masks(rev)
        g = gstate[...]
        for cc in order:
            rows = pl.ds(cc * C, C)
            q, k, v, lav = q_ref[rows, :], k_ref[rows, :], v_ref[rows, :], la_ref[rows, :]
            dov = do_ref[rows, :]
            st = st_ref[0, cc]
            e, ei, ee, d, qd, ki, ke = _gla_chunk_common(q, k, lav, cum, end_row)
            qdb, kib, keb, vb = qd.astype(BF16), ki.astype(BF16), ke.astype(BF16), v.astype(BF16)
            dob, gb, stb = dov.astype(BF16), g.astype(BF16), st.astype(BF16)
            att_t = jnp.where(mask_t, _dg(kib, qdb, 1, 1), 0.0)
            da = jnp.where(mask, _dg(dob, vb, 1, 1), 0.0)
            da_t = jnp.where(mask_t, _dg(vb, dob, 1, 1), 0.0)
            dv_ref[rows, :] = jnp.dot(att_t.astype(BF16), dob, preferred_element_type=F32) + _dg(keb, gb, 1, 1)
            dqd = (jnp.dot(da.astype(BF16), kib, preferred_element_type=F32)
                   + jnp.dot(dob, stb, preferred_element_type=F32))
            dki = jnp.dot(da_t.astype(BF16), qdb, preferred_element_type=F32)
            dke = jnp.dot(vb, gb, preferred_element_type=F32)
            dd = jnp.sum(st * g, axis=0, keepdims=True)
            g = g * d + _dg(dob, qdb, 0, 0)
            dq_ref[rows, :] = dqd * e * (GLA_DK ** -0.5)
            dk_ref[rows, :] = dki * ei + dke * ee
            dkeke = dke * ke
            db = dqd * qd - dki * ki - dkeke
            dbend = jnp.sum(dkeke, axis=0, keepdims=True) + dd * d
            dla_ref[rows, :] = _cum_dot(cum_t, db) + dbend
        gstate[...] = g

    key_out = la_spec
    return pl.pallas_call(
        body, name=name, grid=(GLA_HEADS, nblk),
        in_specs=[o_spec, q_spec, k_spec, v_spec, la_spec, st_spec],
        out_specs=[key_out, key_out, o_spec, key_out],
        out_shape=[jax.ShapeDtypeStruct((S, GLA_KEY), F32), jax.ShapeDtypeStruct((S, GLA_KEY), F32),
                   jax.ShapeDtypeStruct((S, GLA_VAL), F32), jax.ShapeDtypeStruct((S, GLA_KEY), F32)],
        scratch_shapes=[pltpu.VMEM((GLA_DV, GLA_DK), F32)],
        compiler_params=_cp(("parallel", "arbitrary")),
    )(do, proj, proj, proj, la, states)


def _gla_out_fwd(of, ob, proj, gn, name):
    S = of.shape[0]
    tm = _pick(S, (256, 128))
    gblk = (2 * GLA_KEY + GLA_VAL) // GLA_VAL

    def body(of_ref, ob_ref, g_ref, gn_ref, z_ref):
        gnv = gn_ref[...]
        for h in range(GLA_HEADS):
            cols = pl.ds(h * GLA_DV, GLA_DV)
            o = of_ref[:, cols] + ob_ref[:, cols]
            r = lax.rsqrt(jnp.mean(o * o, axis=-1, keepdims=True) + NORM_EPS)
            gv = g_ref[:, cols]
            z_ref[:, cols] = (o * r * gnv * (gv * _sigmoid(gv))).astype(BF16)

    row = pl.BlockSpec((tm, GLA_VAL), lambda i: (i, 0))
    return pl.pallas_call(
        body, name=name, grid=(S // tm,),
        in_specs=[row, row, pl.BlockSpec((tm, GLA_VAL), lambda i: (i, gblk)),
                  pl.BlockSpec((1, GLA_DV), lambda i: (0, 0))],
        out_specs=row,
        out_shape=jax.ShapeDtypeStruct((S, GLA_VAL), BF16),
        compiler_params=_cp(("parallel",)),
    )(of, ob, proj, gn)


def _gla_out_bwd(dz, of, ob, proj, gn, name):
    S = of.shape[0]
    tm = _pick(S, (256, 128))
    gblk = (2 * GLA_KEY + GLA_VAL) // GLA_VAL

    def body(dz_ref, of_ref, ob_ref, g_ref, gn_ref, do_ref, dg_ref, dgn_ref):
        i = pl.program_id(0)
        gnv = gn_ref[...]
        part = jnp.zeros((1, GLA_DV), F32)
        for h in range(GLA_HEADS):
            cols = pl.ds(h * GLA_DV, GLA_DV)
            o = of_ref[:, cols] + ob_ref[:, cols]
            r = lax.rsqrt(jnp.mean(o * o, axis=-1, keepdims=True) + NORM_EPS)
            y = o * r
            gv = g_ref[:, cols]
            sg = _sigmoid(gv)
            dzv = dz_ref[:, cols]
            dg_ref[:, cols] = dzv * (y * gnv) * (sg * (1.0 + gv * (1.0 - sg)))
            don = dzv * (gv * sg)
            part = part + jnp.sum(don * y, axis=0, keepdims=True)
            dy = don * gnv
            do_ref[:, cols] = r * (dy - y * jnp.mean(dy * y, axis=-1, keepdims=True))

        @pl.when(i == 0)
        def _():
            dgn_ref[...] = part

        @pl.when(i > 0)
        def _():
            dgn_ref[...] += part

    row = pl.BlockSpec((tm, GLA_VAL), lambda i: (i, 0))
    one = pl.BlockSpec((1, GLA_DV), lambda i: (0, 0))
    return pl.pallas_call(
        body, name=name, grid=(S // tm,),
        in_specs=[row, row, row, pl.BlockSpec((tm, GLA_VAL), lambda i: (i, gblk)), one],
        out_specs=[row, row, one],
        out_shape=[jax.ShapeDtypeStruct((S, GLA_VAL), F32), jax.ShapeDtypeStruct((S, GLA_VAL), F32),
                   jax.ShapeDtypeStruct((1, GLA_DV), F32)],
        compiler_params=_cp(("arbitrary",)),
    )(dz, of, ob, proj, gn)


N_QK_HEADS = ATT_QH + ATT_KVH


def _qk_prep_fwd(proj, qn, kn, rc, rs, name):
    S = proj.shape[0]
    tm = _pick(S, (256, 128))
    W = N_QK_HEADS * ATT_HD
    scale = ATT_HD ** -0.5

    def body(p_ref, qn_ref, kn_ref, rc_ref, rs_ref, v_in_ref, qk_ref, v_ref, kt_ref, vt_ref):
        c, s = rc_ref[...], rs_ref[...]
        for h in range(N_QK_HEADS):
            cols = pl.ds(h * ATT_HD, ATT_HD)
            w = qn_ref[...] if h < ATT_QH else kn_ref[...]
            xv = p_ref[:, cols]
            r = lax.rsqrt(jnp.mean(xv * xv, axis=-1, keepdims=True) + NORM_EPS)
            y = xv * r * w
            out = y * c + pltpu.roll(y, ATT_HD // 2, 1) * s
            if h < ATT_QH:
                qk_ref[:, cols] = (out * scale).astype(BF16)
            else:
                qk_ref[:, cols] = out.astype(BF16)
                kt_ref[pl.ds((h - ATT_QH) * ATT_HD, ATT_HD), :] = out.T.astype(BF16)
        v_ref[...] = v_in_ref[...].astype(BF16)
        for h in range(ATT_KVH):
            vt_ref[pl.ds(h * ATT_HD, ATT_HD), :] = v_in_ref[:, pl.ds(h * ATT_HD, ATT_HD)].T.astype(BF16)

    one = pl.BlockSpec((1, ATT_HD), lambda i: (0, 0))
    tab = pl.BlockSpec((tm, ATT_HD), lambda i: (i, 0))
    vw = ATT_KVH * ATT_HD
    tr = pl.BlockSpec((vw, tm), lambda i: (0, i))
    return pl.pallas_call(
        body, name=name, grid=(S // tm,),
        in_specs=[pl.BlockSpec((tm, W), lambda i: (i, 0)), one, one, tab, tab,
                  pl.BlockSpec((tm, vw), lambda i: (i, W // vw))],
        out_specs=[pl.BlockSpec((tm, W), lambda i: (i, 0)), pl.BlockSpec((tm, vw), lambda i: (i, 0)), tr, tr],
        out_shape=[jax.ShapeDtypeStruct((S, W), BF16), jax.ShapeDtypeStruct((S, vw), BF16),
                   jax.ShapeDtypeStruct((vw, S), BF16), jax.ShapeDtypeStruct((vw, S), BF16)],
        compiler_params=_cp(("parallel",)),
    )(proj, qn, kn, rc, rs, proj)


def _qk_prep_bwd(dqk, proj, qn, kn, rc, rs, name):
    S = proj.shape[0]
    tm = _pick(S, (256, 128))
    W = N_QK_HEADS * ATT_HD

    def body(d_ref, p_ref, qn_ref, kn_ref, rc_ref, rs_ref, dp_ref, dqn_ref, dkn_ref):
        i = pl.program_id(0)
        c, s = rc_ref[...], rs_ref[...]
        parts = [jnp.zeros((1, ATT_HD), F32), jnp.zeros((1, ATT_HD), F32)]
        for h in range(N_QK_HEADS):
            cols = pl.ds(h * ATT_HD, ATT_HD)
            w = qn_ref[...] if h < ATT_QH else kn_ref[...]
            dout = d_ref[:, cols]
            dy = dout * c + pltpu.roll(dout * s, ATT_HD // 2, 1)
            xv = p_ref[:, cols]
            r = lax.rsqrt(jnp.mean(xv * xv, axis=-1, keepdims=True) + NORM_EPS)
            xr = xv * r
            which = 0 if h < ATT_QH else 1
            parts[which] = parts[which] + jnp.sum(dy * xr, axis=0, keepdims=True)
            dxr = dy * w
            dp_ref[:, cols] = r * (dxr - xr * jnp.mean(dxr * xr, axis=-1, keepdims=True))

        @pl.when(i == 0)
        def _():
            dqn_ref[...] = parts[0]
            dkn_ref[...] = parts[1]

        @pl.when(i > 0)
        def _():
            dqn_ref[...] += parts[0]
            dkn_ref[...] += parts[1]

    one = pl.BlockSpec((1, ATT_HD), lambda i: (0, 0))
    tab = pl.BlockSpec((tm, ATT_HD), lambda i: (i, 0))
    row = pl.BlockSpec((tm, W), lambda i: (i, 0))
    return pl.pallas_call(
        body, name=name, grid=(S // tm,),
        in_specs=[row, row, one, one, tab, tab],
        out_specs=[row, one, one],
        out_shape=[jax.ShapeDtypeStruct((S, W), F32), jax.ShapeDtypeStruct((1, ATT_HD), F32),
                   jax.ShapeDtypeStruct((1, ATT_HD), F32)],
        compiler_params=_cp(("arbitrary",)),
    )(dqk, proj, qn, kn, rc, rs)


ATT_TQ = 256
LSE_ROWS = 8


def _attn_fwd(qk, vt, name):
    S = qk.shape[0]
    tq = min(ATT_TQ, S)

    def body(q_ref, k_ref, vt_ref, o_ref, lse_ref):
        st = _dg(k_ref[...], q_ref[...], 1, 1)
        m = jnp.max(st, axis=0, keepdims=True)
        pt = jnp.exp(st - m)
        l = jnp.sum(pt, axis=0, keepdims=True)
        ot = jnp.dot(vt_ref[...], pt.astype(BF16), preferred_element_type=F32)
        o_ref[...] = (ot * (1.0 / l)).T
        lse_ref[...] = jnp.broadcast_to(m + jnp.log(l), (LSE_ROWS, tq))

    qo = pl.BlockSpec((tq, ATT_HD), lambda h, i: (i, h))
    return pl.pallas_call(
        body, name=name, grid=(ATT_QH, S // tq),
        in_specs=[qo, pl.BlockSpec((S, ATT_HD), lambda h, i: (0, ATT_QH + h // ATT_GROUP)),
                  pl.BlockSpec((ATT_HD, S), lambda h, i: (h // ATT_GROUP, 0))],
        out_specs=[qo, pl.BlockSpec((LSE_ROWS, tq), lambda h, i: (h, i))],
        out_shape=[jax.ShapeDtypeStruct((S, ATT_QH * ATT_HD), F32),
                   jax.ShapeDtypeStruct((ATT_QH * LSE_ROWS, S), F32)],
        compiler_params=_cp(("parallel", "parallel")),
    )(qk, qk, vt)


def _attn_bwd(do, o, lse, qk, v, kt, name):
    S = qk.shape[0]
    tq = min(ATT_TQ, S)
    scale = ATT_HD ** -0.5

    def body(do_ref, o_ref, lse_ref, q_ref, k_ref, v_ref, kt_ref, dq_ref, dk_ref, dv_ref):
        g = pl.program_id(1)
        i = pl.program_id(2)

        @pl.when((g == 0) & (i == 0))
        def _():
            dk_ref[...] = jnp.zeros_like(dk_ref)
            dv_ref[...] = jnp.zeros_like(dv_ref)

        q = q_ref[...]
        dov = do_ref[...]
        dob = dov.astype(BF16)
        delta = jnp.sum((dov * o_ref[...]).T, axis=0, keepdims=True)
        st = _dg(k_ref[...], q, 1, 1)
        pt = jnp.exp(st - lse_ref[0:1, :])
        dpt = _dg(v_ref[...], dob, 1, 1)
        dst = (pt * (dpt - delta)).astype(BF16)
        dv_ref[...] += jnp.dot(pt.astype(BF16), dob, preferred_element_type=F32)
        dk_ref[...] += jnp.dot(dst, q, preferred_element_type=F32)
        dq_ref[...] = jnp.dot(kt_ref[...], dst, preferred_element_type=F32).T * scale

    qo = pl.BlockSpec((tq, ATT_HD), lambda kv, g, i: (i, kv * ATT_GROUP + g))
    kvo = pl.BlockSpec((S, ATT_HD), lambda kv, g, i: (0, kv))
    return pl.pallas_call(
        body, name=name, grid=(ATT_KVH, ATT_GROUP, S // tq),
        in_specs=[qo, qo, pl.BlockSpec((LSE_ROWS, tq), lambda kv, g, i: (kv * ATT_GROUP + g, i)), qo,
                  pl.BlockSpec((S, ATT_HD), lambda kv, g, i: (0, ATT_QH + kv)), kvo,
                  pl.BlockSpec((ATT_HD, S), lambda kv, g, i: (kv, 0))],
        out_specs=[qo, kvo, kvo],
        out_shape=[jax.ShapeDtypeStruct((S, ATT_QH * ATT_HD), F32),
                   jax.ShapeDtypeStruct((S, ATT_KVH * ATT_HD), F32),
                   jax.ShapeDtypeStruct((S, ATT_KVH * ATT_HD), F32)],
        compiler_params=_cp(("parallel", "arbitrary", "arbitrary")),
    )(do, o, lse, qk, qk, v, kt)


def _adamw(w, g, m, v, name):
    rows, cols = w.shape
    tr = rows
    for cand in (512, 256, 128, 64, 32, 16, 8):
        if rows % cand == 0 and cand * cols * 4 <= 2 * 1024 * 1024:
            tr = cand
            break

    def body(w_ref, g_ref, m_ref, v_ref, d_ref, nm_ref, nv_ref):
        gv = g_ref[...]
        nm = ADAM_B1 * m_ref[...] + (1.0 - ADAM_B1) * gv
        nv = ADAM_B2 * v_ref[...] + (1.0 - ADAM_B2) * (gv * gv)
        m_hat = nm / (1.0 - ADAM_B1 ** ADAM_STEP)
        v_hat = nv / (1.0 - ADAM_B2 ** ADAM_STEP)
        d_ref[...] = -ADAM_LR * (m_hat / (jnp.sqrt(v_hat) + ADAM_EPS) + ADAM_WD * w_ref[...])
        nm_ref[...] = nm
        nv_ref[...] = nv

    blk = pl.BlockSpec((tr, cols), lambda i: (i, 0))
    return pl.pallas_call(
        body, name=name, grid=(rows // tr,),
        in_specs=[blk] * 4, out_specs=[blk] * 3,
        out_shape=[jax.ShapeDtypeStruct((rows, cols), F32)] * 3,
        compiler_params=_cp(("parallel",)),
    )(w, g, m, v)


ANY = pl.BlockSpec(memory_space=pl.ANY)


def _place():
    return lax.axis_index("x"), lax.axis_index("y"), lax.axis_index("c")


def _other_chips(x, y):
    return [(1 - x, y), (x, 1 - y), (1 - x, 1 - y)]


def _half_rows(c, H):
    return pl.ds(pl.multiple_of(c * H, 8), H)


def _allreduce_small(v, name):
    R = v.shape[0]
    n_dev = 8

    def body(v_ref, sum_ref, all_ref, send_sems, recv_sems, local_sem):
        x, y, c = _place()
        me, sibling = (x, y, c), (x, y, 1 - c)
        chips = _other_chips(x, y)

        def rows(px, py, pc):
            return all_ref.at[pl.ds(pl.multiple_of((4 * px + 2 * py + pc) * R, 8), R), :]

        def copy(k, block, to, src=None):
            return pltpu.make_async_remote_copy(
                src_ref=rows(*block) if src is None else src, dst_ref=rows(*block),
                send_sem=send_sems.at[k], recv_sem=recv_sems.at[k], device_id=to, device_id_type=MESH)

        own = pltpu.make_async_copy(v_ref, rows(*me), local_sem)
        own.start()
        first = [copy(0, me, sibling, src=v_ref)]
        first += [copy(1 + j, me, (*chip, c), src=v_ref) for j, chip in enumerate(chips)]
        for cp in first:
            cp.start()
        passed = [copy(4 + j, (*chip, c), sibling) for j, chip in enumerate(chips)]
        for j, chip in enumerate(chips):
            copy(1 + j, (*chip, c), me).wait_recv()
            passed[j].start()
        copy(0, sibling, me).wait_recv()
        for j, chip in enumerate(chips):
            copy(4 + j, (*chip, 1 - c), me).wait_recv()
        for cp in first + passed:
            cp.wait_send()
        own.wait()
        acc = all_ref[pl.ds(0, R), :]
        for d in range(1, n_dev):
            acc = acc + all_ref[pl.ds(d * R, R), :]
        sum_ref[...] = acc

    vm = pl.BlockSpec(memory_space=pltpu.VMEM)
    return pl.pallas_call(
        body, name=name,
        in_specs=[vm], out_specs=[vm, vm],
        out_shape=[jax.ShapeDtypeStruct((R, LANES), F32), jax.ShapeDtypeStruct((n_dev * R, LANES), F32)],
        scratch_shapes=[pltpu.SemaphoreType.DMA((7,)), pltpu.SemaphoreType.DMA((7,)), pltpu.SemaphoreType.DMA],
    )(v)[0]


def _swap_other_half(bufs, name):
    n = len(bufs)
    halves = [b.shape[1] // 2 for b in bufs]

    def body(*refs):
        g_refs, got_refs = refs[:n], refs[n:2 * n]
        send_sems, recv_sems = refs[2 * n:]
        x, y, c = _place()
        copies = [pltpu.make_async_remote_copy(
            src_ref=g_refs[k].at[p, _half_rows(1 - c, halves[k])], dst_ref=got_refs[k].at[p],
            send_sem=send_sems.at[N_CHIPS * k + p], recv_sem=recv_sems.at[N_CHIPS * k + p],
            device_id=(x, y, 1 - c), device_id_type=MESH) for k in range(n) for p in range(N_CHIPS)]
        for cp in copies:
            cp.start()
        for cp in copies:
            cp.wait_recv()
        for cp in copies:
            cp.wait_send()

    return pl.pallas_call(
        body, name=name, in_specs=[ANY] * n, out_specs=[ANY] * n,
        out_shape=[jax.ShapeDtypeStruct((N_CHIPS, h, b.shape[2]), b.dtype) for b, h in zip(bufs, halves)],
        scratch_shapes=[pltpu.SemaphoreType.DMA((N_CHIPS * n,)), pltpu.SemaphoreType.DMA((N_CHIPS * n,))],
    )(*bufs)


def _join_halves(bufs, name):
    n = len(bufs)
    halves = [b.shape[0] // 2 for b in bufs]

    def body(*refs):
        outs = refs[n:2 * n]
        send_sems, recv_sems = refs[2 * n:]
        x, y, c = _place()

        def copy(k, core):
            blk = outs[k].at[_half_rows(core, halves[k])]
            return pltpu.make_async_remote_copy(src_ref=blk, dst_ref=blk, send_sem=send_sems.at[k],
                                                recv_sem=recv_sems.at[k], device_id=(x, y, 1 - c),
                                                device_id_type=MESH)

        sends = [copy(k, c) for k in range(n)]
        for cp in sends:
            cp.start()
        for k in range(n):
            copy(k, 1 - c).wait_recv()
        for cp in sends:
            cp.wait_send()

    return pl.pallas_call(
        body, name=name, in_specs=[ANY] * n, out_specs=[ANY] * n,
        out_shape=[jax.ShapeDtypeStruct(b.shape, b.dtype) for b in bufs],
        input_output_aliases={k: k for k in range(n)},
        scratch_shapes=[pltpu.SemaphoreType.DMA((n,)), pltpu.SemaphoreType.DMA((n,))],
    )(*bufs)


def _rs_rows(H, width):
    for cand in (1024, 512, 256, 128, 64, 32, 16):
        if H % cand == 0 and cand * width * 4 <= 1536 * 1024:
            return cand
    return H


def _add_sibling(g, got, c, me, name):
    _, H, width = got.shape
    tb = _rs_rows(H, width)
    nb = H // tb

    def body(sp_ref, g_ref, got_ref, sb_ref, sf_ref):
        p = pl.program_id(1)
        s = g_ref[0] + got_ref[0]
        sb_ref[0] = s.astype(BF16)

        @pl.when(p == sp_ref[1])
        def _():
            sf_ref[...] = s

    grid_spec = pltpu.PrefetchScalarGridSpec(
        num_scalar_prefetch=1, grid=(nb, N_CHIPS),
        in_specs=[pl.BlockSpec((1, tb, width), lambda i, p, sp: (p, sp[0] * nb + i, 0)),
                  pl.BlockSpec((1, tb, width), lambda i, p, sp: (p, i, 0))],
        out_specs=[pl.BlockSpec((1, tb, width), lambda i, p, sp: (p, i, 0)),
                   pl.BlockSpec((tb, width), lambda i, p, sp: (i, 0))])
    return pl.pallas_call(
        body, name=name, grid_spec=grid_spec,
        out_shape=[jax.ShapeDtypeStruct((N_CHIPS, H, width), BF16), jax.ShapeDtypeStruct((H, width), F32)],
        compiler_params=_cp(("arbitrary", "arbitrary")),
    )(jnp.stack([c, me]).astype(jnp.int32), g, got)


def _add_chips(sf, got, others_and_c, name):
    H, width = sf.shape
    tb = _rs_rows(H, width)
    nb = H // tb

    def body(sp_ref, sf_ref, r1_ref, r2_ref, r3_ref, out_ref):
        out_ref[...] = ((sf_ref[...] + r1_ref[0].astype(F32)) + r2_ref[0].astype(F32)) + r3_ref[0].astype(F32)

    def slot(k):
        return pl.BlockSpec((1, tb, width), lambda i, sp: (sp[k], i, 0))

    blk = pl.BlockSpec((tb, width), lambda i, sp: (i, 0))
    grid_spec = pltpu.PrefetchScalarGridSpec(
        num_scalar_prefetch=1, grid=(nb,), in_specs=[blk, slot(0), slot(1), slot(2)],
        out_specs=pl.BlockSpec((tb, width), lambda i, sp: (sp[3] * nb + i, 0)))
    return pl.pallas_call(
        body, name=name, grid_spec=grid_spec,
        out_shape=jax.ShapeDtypeStruct((2 * H, width), F32),
        compiler_params=_cp(("arbitrary",)),
    )(others_and_c.astype(jnp.int32), sf, got, got, got)


REPLICATED = ("norm_mix", "norm_ffn", "gla_b_gate_f", "gla_b_gate_b", "gla_norm", "attn_q_norm", "attn_k_norm",
              "ffn_b_conv")


PIECE_ROWS = 16


def _piece_rows(shape):
    n = 1
    for s in shape:
        n *= s
    rows = n // LANES
    return rows, -(-rows // PIECE_ROWS) * PIECE_ROWS


def _pack(pieces, dtype, row_multiple):
    flat = []
    for p in pieces:
        rows, padded = _piece_rows(p.shape)
        flat.append(jnp.pad(p.astype(dtype).reshape(rows, LANES), ((0, padded - rows), (0, 0))))
    rows = sum(f.shape[0] for f in flat)
    padded = -(-rows // row_multiple) * row_multiple
    if padded > rows:
        flat.append(jnp.zeros((padded - rows, LANES), dtype))
    return jnp.concatenate(flat, axis=0)


def _unpack(buf, shapes):
    out, r = [], 0
    for shp in shapes:
        rows, padded = _piece_rows(shp)
        out.append(buf[r:r + rows].reshape(shp))
        r += padded
    return out


def _own_slot(shard2d, me):
    return lax.dynamic_update_index_in_dim(lax.empty((N_CHIPS,) + shard2d.shape, shard2d.dtype), shard2d, me, 0)


def _layer_small(w, l):
    j = l // 2
    if l % 2 == 0:
        return [w["gla_w_gate_up_f"][j], w["gla_w_gate_up_b"][j], w["ffn_w_conv"][l]]
    return [w["ffn_w_conv"][l]]


def _layer_weight_bufs(w, l, me):
    j = l // 2
    mixer = ("gla_w_in", "gla_w_out") if l % 2 == 0 else ("attn_w_qkv", "attn_w_out")
    bufs = [_own_slot(w[n][j].astype(BF16), me) for n in mixer]
    bufs.append(_own_slot(_pack(_layer_small(w, l), F32, 32), me))
    bufs += [_own_slot(w["ffn_w_up"][l].astype(BF16), me), _own_slot(w["ffn_w_down"][l].astype(BF16), me)]
    return bufs


N_MIXER_BUFS = 3


def _layer_weights(w, l, got):
    rows = lambda t: t.reshape(-1, t.shape[2])
    cols = lambda t: jnp.concatenate([t[p] for p in range(N_CHIPS)], axis=1)
    out = {}
    if len(got) != N_MIXER_BUFS:
        up, down = got[-2:]
        out.update(up=up, up_full=cols(up), down=rows(down))
    if len(got) != 2:
        mix_in, mix_out, small = got[:N_MIXER_BUFS]
        shapes = [t.shape for t in _layer_small(w, l)]
        parts = [_unpack(small[p], shapes) for p in range(N_CHIPS)]
        full_small = [jnp.concatenate([parts[p][k] for p in range(N_CHIPS)], axis=-1) for k in range(len(shapes))]
        out.update(conv=full_small[-1])
        if l % 2 == 0:
            out.update(gla_in=jnp.pad(cols(mix_in), ((0, 0), (0, GLA_IN_PAD - GLA_IN))), gla_out=rows(mix_out),
                       gate_f=full_small[0], gate_b=full_small[1])
        else:
            out.update(qkv=mix_in, attn_out=rows(mix_out))
    return out


HBM = pl.BlockSpec(memory_space=pltpu.HBM)
SEM = pl.BlockSpec(memory_space=pltpu.SEMAPHORE)
SIDE_EFFECT = pltpu.SideEffectType.DATAFLOW_SIDE_EFFECTING


def _gather_start(bufs, after, name):
    n = len(bufs)
    halves = [b.shape[1] // 2 for b in bufs]

    def body(*refs):
        refs = refs[:n] + refs[n + 1:]
        send_sems, recv_sems = refs[n:2 * n], refs[2 * n:3 * n]
        outs, token = refs[3 * n:4 * n], refs[4 * n]
        x, y, c = _place()
        me = 2 * x + y
        for k in range(n):
            blk = outs[k].at[me, _half_rows(c, halves[k])]
            for px, py in _other_chips(x, y):
                pltpu.make_async_remote_copy(src_ref=blk, dst_ref=blk, send_sem=send_sems[k], recv_sem=recv_sems[k],
                                             device_id=(px, py, c), device_id_type=MESH).start()
        token[...] = jnp.zeros_like(token)

    res = pl.pallas_call(
        body, name=name,
        in_specs=[HBM] * n + [ANY],
        out_specs=[SEM] * (2 * n) + [HBM] * n + [pl.BlockSpec(memory_space=pltpu.VMEM)],
        out_shape=[pltpu.SemaphoreType.DMA(())] * (2 * n) + [pltpu.HBM(b.shape, b.dtype) for b in bufs]
        + [jax.ShapeDtypeStruct((8, LANES), F32)],
        input_output_aliases={k: 2 * n + k for k in range(n)},
        compiler_params=pltpu.CompilerParams(has_side_effects=SIDE_EFFECT),
    )(*[pltpu.with_memory_space_constraint(b, pltpu.HBM) for b in bufs], after)
    return res[:n], res[n:2 * n], res[2 * n:3 * n], res[3 * n]


def _gather_wait(send_sems, recv_sems, thru, after, name):
    n = len(thru)
    halves = [b.shape[1] // 2 for b in thru]

    def body(*refs):
        ss, rs = refs[n:2 * n], refs[2 * n:3 * n]
        outs = refs[3 * n + 1:]
        x, y, c = _place()
        for k in range(n):
            three = outs[k].at[pl.ds(0, N_CHIPS - 1), _half_rows(c, halves[k])]
            cp = pltpu.make_async_remote_copy(src_ref=three, dst_ref=three, send_sem=ss[k], recv_sem=rs[k],
                                              device_id=(x, y, c), device_id_type=MESH)
            cp.wait_send()
            cp.wait_recv()

    return pl.pallas_call(
        body, name=name,
        in_specs=[HBM] * n + [SEM] * (2 * n) + [ANY],
        out_specs=[HBM] * n,
        out_shape=[pltpu.HBM(b.shape, b.dtype) for b in thru],
        input_output_aliases={k: k for k in range(n)},
        compiler_params=pltpu.CompilerParams(has_side_effects=SIDE_EFFECT),
    )(*thru, *send_sems, *recv_sems, after)


def _send_start(sbs, name):
    n = len(sbs)

    def body(*refs):
        send_sems, recv_sems = refs[2 * n:3 * n], refs[3 * n:4 * n]
        srcs, lands, token = refs[4 * n:5 * n], refs[5 * n:6 * n], refs[6 * n]
        x, y, c = _place()
        me = 2 * x + y
        for k in range(n):
            for px, py in _other_chips(x, y):
                pltpu.make_async_remote_copy(src_ref=srcs[k].at[2 * px + py], dst_ref=lands[k].at[me],
                                             send_sem=send_sems[k], recv_sem=recv_sems[k],
                                             device_id=(px, py, c), device_id_type=MESH).start()
        token[...] = jnp.zeros_like(token)

    hbm = lambda a: pltpu.with_memory_space_constraint(a, pltpu.HBM)
    res = pl.pallas_call(
        body, name=name,
        in_specs=[HBM] * (2 * n),
        out_specs=[SEM] * (2 * n) + [HBM] * (2 * n) + [pl.BlockSpec(memory_space=pltpu.VMEM)],
        out_shape=[pltpu.SemaphoreType.DMA(())] * (2 * n) + [pltpu.HBM(s.shape, s.dtype) for s in sbs] * 2
        + [jax.ShapeDtypeStruct((8, LANES), F32)],
        input_output_aliases={k: 2 * n + k for k in range(2 * n)},
        compiler_params=pltpu.CompilerParams(has_side_effects=SIDE_EFFECT),
    )(*[hbm(s) for s in sbs], *[hbm(lax.empty(s.shape, s.dtype)) for s in sbs])
    return res[:n], res[n:2 * n], res[2 * n:3 * n], res[3 * n:4 * n], res[4 * n]


def _send_wait(send_sems, recv_sems, srcs, lands, after, name):
    n = len(srcs)

    def body(*refs):
        ss, rs = refs[2 * n:3 * n], refs[3 * n:4 * n]
        s_out, l_out = refs[4 * n + 1:5 * n + 1], refs[5 * n + 1:]
        x, y, c = _place()
        for k in range(n):
            cp = pltpu.make_async_remote_copy(src_ref=s_out[k].at[pl.ds(0, N_CHIPS - 1)],
                                              dst_ref=l_out[k].at[pl.ds(0, N_CHIPS - 1)], send_sem=ss[k],
                                              recv_sem=rs[k], device_id=(x, y, c), device_id_type=MESH)
            cp.wait_send()
            cp.wait_recv()

    res = pl.pallas_call(
        body, name=name,
        in_specs=[HBM] * (2 * n) + [SEM] * (2 * n) + [ANY],
        out_specs=[HBM] * (2 * n),
        out_shape=[pltpu.HBM(s.shape, s.dtype) for s in srcs] * 2,
        input_output_aliases={k: k for k in range(2 * n)},
        compiler_params=pltpu.CompilerParams(has_side_effects=SIDE_EFFECT),
    )(*srcs, *lands, *send_sems, *recv_sems, after)
    return res[n:]


def _pass_to_sibling(bufs, name):
    n = len(bufs)
    halves = [b.shape[1] // 2 for b in bufs]

    def body(*refs):
        outs = refs[n:2 * n]
        send_sems, recv_sems = refs[2 * n:]
        x, y, c = _place()
        chips = _other_chips(x, y)

        def copy(k, j, core):
            px, py = chips[j]
            blk = outs[k].at[2 * px + py, _half_rows(core, halves[k])]
            return pltpu.make_async_remote_copy(src_ref=blk, dst_ref=blk, send_sem=send_sems.at[3 * k + j],
                                                recv_sem=recv_sems.at[3 * k + j], device_id=(x, y, 1 - c),
                                                device_id_type=MESH)

        sends = [copy(k, j, c) for k in range(n) for j in range(3)]
        for cp in sends:
            cp.start()
        for k in range(n):
            for j in range(3):
                copy(k, j, 1 - c).wait_recv()
        for cp in sends:
            cp.wait_send()

    return pl.pallas_call(
        body, name=name,
        in_specs=[ANY] * n, out_specs=[ANY] * n,
        out_shape=[jax.ShapeDtypeStruct(b.shape, b.dtype) for b in bufs],
        input_output_aliases={k: k for k in range(n)},
        scratch_shapes=[pltpu.SemaphoreType.DMA((3 * n,)), pltpu.SemaphoreType.DMA((3 * n,))],
    )(*bufs)


def _rope_tables(S):
    rows = S // GRID_W
    row_idx = jnp.repeat(jnp.arange(rows, dtype=F32), GRID_W)
    col_idx = jnp.tile(jnp.arange(GRID_W, dtype=F32), rows)
    pairs = ATT_HD // 4
    inv_freq = ROPE_THETA ** (-jnp.arange(pairs, dtype=F32) / pairs)
    ang = jnp.concatenate([row_idx[:, None] * inv_freq, col_idx[:, None] * inv_freq], axis=-1)
    cos, sin = jnp.cos(ang), jnp.sin(ang)
    return jnp.concatenate([cos, cos], axis=-1), jnp.concatenate([-sin, sin], axis=-1)


def _gate_rows(w, first_row):
    return jnp.zeros((LANES, GLA_KEY), F32).at[first_row:first_row + GLA_RANK].set(w.astype(F32))


def _local_step(x, target, weights_of, grads_out, P):
    S = x.shape[0]
    rc, rs = _rope_tables(S)
    row = lambda a: a.reshape(1, -1)
    saved = []
    for i in range(DEPTH):
        j = i // 2
        W = dict(weights_of(i, "mix", x))
        nm = row(P["norm_mix"][i])
        h1 = _rmsnorm_fwd(x, nm, f"norm_mix_fwd{i}")
        if i % 2 == 0:
            wgf = _gate_rows(W["gate_f"], 0)
            wgb = _gate_rows(W["gate_b"], GLA_RANK)
            bgf, bgb = row(P["gla_b_gate_f"][j]), row(P["gla_b_gate_b"][j])
            gn = row(P["gla_norm"][j])
            proj = _matmul_rows(h1, W["gla_in"], f"gla_in{i}")
            laf, lab = _gla_gate_fwd(proj, wgf, bgf, wgb, bgb, f"gla_gate_fwd{i}")
            of, stf = _gla_scan_fwd(proj, laf, False, f"gla_scan_f_fwd{i}")
            ob, stb = _gla_scan_fwd(proj, lab, True, f"gla_scan_b_fwd{i}")
            z = _gla_out_fwd(of, ob, proj, gn, f"gla_out_fwd{i}")
            xm = _matmul_rows(z, W["gla_out"], f"gla_outproj{i}", res=x)
            mix = dict(proj=proj, laf=laf, lab=lab, of=of, ob=ob, stf=stf, stb=stb, z=z, wgf=wgf, wgb=wgb)
        else:
            proj = _matmul_rows(h1, W["qkv"], f"attn_qkv{i}", w_layer=0)
            qn, kn = row(P["attn_q_norm"][j]), row(P["attn_k_norm"][j])
            qk, vb, kt, vt = _qk_prep_fwd(proj, qn, kn, rc, rs, f"qk_prep_fwd{i}")
            o, lse = _attn_fwd(qk, vt, f"attn_fwd{i}")
            xm = _matmul_rows(o, W["attn_out"], f"attn_outproj{i}", res=x)
            mix = dict(proj=proj, qk=qk, vb=vb, kt=kt, o=o, lse=lse)
        W.update(weights_of(i, "ffn", xm))
        h2 = _rmsnorm_fwd(xm, row(P["norm_ffn"][i]), f"norm_ffn_fwd{i}")
        a, uv, ug = _ffn_mid_fwd(h2, W["up_full"], W["conv"], row(P["ffn_b_conv"][i]), f"ffn_mid_fwd{i}")
        xo = _matmul_rows(a, W["down"], f"ffn_down{i}", res=xm)
        saved.append(dict(x=x, h1=h1, xm=xm, h2=h2, uv=uv, ug=ug, mix=mix, W=W))
        x = xo

    dx, dxb, loss = _loss_grad(x, target, "loss")

    G = {n: [None] * (DEPTH if n.startswith(("norm", "ffn")) else DEPTH // 2) for n in REPLICATED}
    token = None
    for i in reversed(range(DEPTH)):
        j = i // 2
        sv = saved[i]
        mix = sv["mix"]
        W = sv["W"]
        bconv = row(P["ffn_b_conv"][i])
        if token is not None:
            t = token[0:1, 0:1]
            bconv = jnp.where(t == 0.0, bconv, t)
        duv, dug, a, gwv, gwg = _ffn_mid_bwd(dxb, W["down"], sv["uv"], sv["ug"], W["conv"], bconv, f"ffn_mid_bwd{i}")
        L = dict(down=_matmul(a, dxb, 0, 0, f"ffn_down_wgrad{i}", out_chips=("rows", 0, 1)),
                 up=_matmul(sv["h2"], (duv, dug), 0, 0, f"ffn_up_wgrad{i}", out_chips=("cols", 0, 1)),
                 small=[jnp.concatenate([gwv[:3], gwg[:3]], axis=1)])
        G["ffn_b_conv"][i] = jnp.concatenate([gwv[3], gwg[3]], axis=0)
        dxm, dxmb, dn = _dgrad_norm((duv, dug), W["up"], sv["xm"], row(P["norm_ffn"][i]), dx, f"ffn_up_dgrad{i}",
                                    w_layer=0)
        G["norm_ffn"][i] = dn[0]
        if i % 2 == 0:
            proj = mix["proj"]
            bgf, bgb = row(P["gla_b_gate_f"][j]), row(P["gla_b_gate_b"][j])
            gn = row(P["gla_norm"][j])
            dz = _matmul_rows(dxmb, W["gla_out"], f"gla_outproj_dgrad{i}", transposed=True)
            L["out"] = _matmul(mix["z"], dxmb, 0, 0, f"gla_outproj_wgrad{i}", out_chips=("rows", 0, 1))
            do, dg, dgn = _gla_out_bwd(dz, mix["of"], mix["ob"], proj, gn, f"gla_out_bwd{i}")
            G["gla_norm"][j] = dgn[0]
            dqf, dkf, dvf, dlaf = _gla_scan_bwd(do, proj, mix["laf"], mix["stf"], False, f"gla_scan_f_bwd{i}")
            dqb, dkb, dvb, dlab = _gla_scan_bwd(do, proj, mix["lab"], mix["stb"], True, f"gla_scan_b_bwd{i}")
            dr, dwf, dbf, dwb, dbb = _gla_gate_bwd(dlaf, dlab, proj, mix["wgf"], bgf, mix["wgb"], bgb,
                                                   f"gla_gate_bwd{i}")
            L["small"] = [dwf[:GLA_RANK], dwb[GLA_RANK:2 * GLA_RANK]] + L["small"]
            G["gla_b_gate_f"][j] = dbf[0]
            G["gla_b_gate_b"][j] = dbb[0]
            dproj = jnp.concatenate([dqf + dqb, dkf + dkb, dvf + dvb, dg, dr], axis=1).astype(BF16)
            L["mix_in"] = _matmul(sv["h1"], dproj, 0, 0, f"gla_in_wgrad{i}")
            dx, dxb, dn = _dgrad_norm(dproj, W["gla_in"], sv["x"], row(P["norm_mix"][i]), dxm, f"mix_in_dgrad{i}")
        else:
            proj = mix["proj"]
            qn, kn = row(P["attn_q_norm"][j]), row(P["attn_k_norm"][j])
            do = _matmul_rows(dxmb, W["attn_out"], f"attn_outproj_dgrad{i}", transposed=True)
            L["out"] = _matmul(mix["o"], dxmb, 0, 0, f"attn_outproj_wgrad{i}", out_chips=("rows", 0, 1))
            dq, dk, dv = _attn_bwd(do, mix["o"], mix["lse"], mix["qk"], mix["vb"], mix["kt"], f"attn_bwd{i}")
            dqk = jnp.concatenate([dq, dk], axis=1)
            dpqk, dqn, dkn = _qk_prep_bwd(dqk, proj, qn, kn, rc, rs, f"qk_prep_bwd{i}")
            G["attn_q_norm"][j] = dqn[0]
            G["attn_k_norm"][j] = dkn[0]
            dproj = jnp.concatenate([dpqk, dv], axis=1).astype(BF16)
            L["mix_in"] = _matmul(sv["h1"], dproj, 0, 0, f"attn_qkv_wgrad{i}", out_chips=("cols", 0, 1))
            dx, dxb, dn = _dgrad_norm(dproj, W["qkv"], sv["x"], row(P["norm_mix"][i]), dxm, f"mix_in_dgrad{i}",
                                      w_layer=0)
        G["norm_mix"][i] = dn[0]
        token = grads_out(i, L)
    return loss, dx, G


def kernel(x, norm_mix, norm_ffn, gla_w_in, gla_w_gate_up_f, gla_b_gate_f, gla_w_gate_up_b, gla_b_gate_b, gla_norm, gla_w_out, attn_w_qkv, attn_q_norm, attn_k_norm, attn_w_out, ffn_w_up, ffn_w_conv, ffn_b_conv, ffn_w_down, loss_target, m_norm_mix, m_norm_ffn, m_gla_w_in, m_gla_w_gate_up_f, m_gla_b_gate_f, m_gla_w_gate_up_b, m_gla_b_gate_b, m_gla_norm, m_gla_w_out, m_attn_w_qkv, m_attn_q_norm, m_attn_k_norm, m_attn_w_out, m_ffn_w_up, m_ffn_w_conv, m_ffn_b_conv, m_ffn_w_down, v_norm_mix, v_norm_ffn, v_gla_w_in, v_gla_w_gate_up_f, v_gla_b_gate_f, v_gla_w_gate_up_b, v_gla_b_gate_b, v_gla_norm, v_gla_w_out, v_attn_w_qkv, v_attn_q_norm, v_attn_k_norm, v_attn_w_out, v_ffn_w_up, v_ffn_w_conv, v_ffn_b_conv, v_ffn_w_down):
    names = ("norm_mix", "norm_ffn", "gla_w_in", "gla_w_gate_up_f", "gla_b_gate_f", "gla_w_gate_up_b",
             "gla_b_gate_b", "gla_norm", "gla_w_out", "attn_w_qkv", "attn_q_norm", "attn_k_norm", "attn_w_out",
             "ffn_w_up", "ffn_w_conv", "ffn_b_conv", "ffn_w_down")
    w = dict(zip(names, (norm_mix, norm_ffn, gla_w_in, gla_w_gate_up_f, gla_b_gate_f, gla_w_gate_up_b,
                         gla_b_gate_b, gla_norm, gla_w_out, attn_w_qkv, attn_q_norm, attn_k_norm, attn_w_out,
                         ffn_w_up, ffn_w_conv, ffn_b_conv, ffn_w_down)))
    m = dict(zip(names, (m_norm_mix, m_norm_ffn, m_gla_w_in, m_gla_w_gate_up_f, m_gla_b_gate_f,
                         m_gla_w_gate_up_b, m_gla_b_gate_b, m_gla_norm, m_gla_w_out, m_attn_w_qkv, m_attn_q_norm,
                         m_attn_k_norm, m_attn_w_out, m_ffn_w_up, m_ffn_w_conv, m_ffn_b_conv, m_ffn_w_down)))
    v = dict(zip(names, (v_norm_mix, v_norm_ffn, v_gla_w_in, v_gla_w_gate_up_f, v_gla_b_gate_f,
                         v_gla_w_gate_up_b, v_gla_b_gate_b, v_gla_norm, v_gla_w_out, v_attn_w_qkv, v_attn_q_norm,
                         v_attn_k_norm, v_attn_w_out, v_ffn_w_up, v_ffn_w_conv, v_ffn_b_conv, v_ffn_w_down)))
    px, py, pc = _place()
    me = 2 * px + py

    started, token = [], w["norm_mix"]
    for l in range(DEPTH):
        started.append(_gather_start(_layer_weight_bufs(w, l, me), token, f"gather_start{l}"))
        token = started[-1][3]
    fetched = {}

    def weights_of(l, part, after):
        send_sems, recv_sems, thru, _ = started[l]
        if l == 0:
            pick = slice(0, N_MIXER_BUFS) if part == "mix" else slice(N_MIXER_BUFS, None)
            landed = _gather_wait(send_sems[pick], recv_sems[pick], thru[pick], token if part == "mix" else after,
                                  f"gather_wait{l}_{part}")
            return _layer_weights(w, l, _pass_to_sibling(landed, f"gather_pass{l}_{part}"))
        if part == "mix":
            landed = _gather_wait(send_sems, recv_sems, thru, after, f"gather_wait{l}")
            fetched[l] = _layer_weights(w, l, _pass_to_sibling(landed, f"gather_pass{l}"))
        return fetched[l]

    sent = {}

    def grads_out(l, L):
        mix_in = L["mix_in"]
        if l % 2 == 0:
            width = w["gla_w_in"].shape[2]
            mix_in = jnp.stack([mix_in[:, p * width:(p + 1) * width] for p in range(N_CHIPS)])
        cut = lambda t, p: lax.slice_in_dim(t, p * (t.shape[-1] // N_CHIPS), (p + 1) * (t.shape[-1] // N_CHIPS),
                                            axis=t.ndim - 1)
        small = jnp.stack([_pack([cut(t, p) for t in L["small"]], F32, 32) for p in range(N_CHIPS)])
        bufs = [mix_in, L["out"], small, L["up"], L["down"]]
        gots = _swap_other_half(bufs, f"grads{l}_to_sibling")
        sums = [_add_sibling(b, g, pc, me, f"grads{l}_add_sibling{k}") for k, (b, g) in enumerate(zip(bufs, gots))]
        send_sems, recv_sems, srcs, lands, tok = _send_start([s[0] for s in sums], f"grads{l}_start")
        sent[l] = (send_sems, recv_sems, srcs, lands, [s[1] for s in sums])
        return tok

    P = {n: w[n] for n in REPLICATED}

    loss_part, dx, grads = _local_step(x[0], loss_target[0], weights_of, grads_out, P)

    others_and_c = jnp.stack([jnp.where(me <= k, k + 1, k) for k in range(N_CHIPS - 1)] + [pc])
    mine = {}
    for l in reversed(range(DEPTH)):
        send_sems, recv_sems, srcs, lands, own = sent[l]
        landed = _send_wait(send_sems, recv_sems, srcs, lands, dx, f"grads{l}_wait")
        halves = [_add_chips(own[k], landed[k], others_and_c, f"grads{l}_add_chips{k}") for k in range(len(own))]
        mine[l] = _join_halves(halves, f"grads{l}_join_halves")
    gsh = {}
    for n, k, layers in (("ffn_w_up", 3, range(DEPTH)), ("ffn_w_down", 4, range(DEPTH)),
                         ("gla_w_in", 0, range(0, DEPTH, 2)), ("gla_w_out", 1, range(0, DEPTH, 2)),
                         ("attn_w_qkv", 0, range(1, DEPTH, 2)), ("attn_w_out", 1, range(1, DEPTH, 2))):
        gsh[n] = jnp.stack([mine[l][k] for l in layers])
    small_mine = [_unpack(mine[l][2], [t.shape for t in _layer_small(w, l)]) for l in range(DEPTH)]
    gsh["ffn_w_conv"] = jnp.stack([small_mine[l][-1] for l in range(DEPTH)])
    gsh["gla_w_gate_up_f"] = jnp.stack([small_mine[l][0] for l in range(0, DEPTH, 2)])
    gsh["gla_w_gate_up_b"] = jnp.stack([small_mine[l][1] for l in range(0, DEPTH, 2)])

    small = _pack([jnp.stack(grads[n]) for n in REPLICATED] + [loss_part], F32, 16)
    small_sum = _allreduce_small(small, "small_allreduce")
    parts = _unpack(small_sum, [w[n].shape for n in REPLICATED] + [(1, LANES)])
    gsh.update(dict(zip(REPLICATED, parts[:-1])))
    loss = parts[-1][0, 0]

    delta, new_m, new_v = {}, {}, {}
    for n in names:
        shp = w[n].shape
        two_d = (-1, shp[-1])
        d, nm, nv = _adamw(w[n].reshape(two_d), gsh[n].reshape(two_d), m[n].reshape(two_d), v[n].reshape(two_d),
                           f"adamw_{n}")
        delta[n], new_m[n], new_v[n] = d.reshape(shp), nm.reshape(shp), nv.reshape(shp)

    return (loss, dx[None], *[gsh[n] for n in names], *[delta[n] for n in names],
            *[new_m[n] for n in names], *[new_v[n] for n in names])
```

```python
import jax
import jax.numpy as jnp
from jax import lax
from jax.experimental import pallas as pl
from jax.experimental.pallas import tpu as pltpu

F32 = jnp.float32
BF16 = jnp.bfloat16
MESH = pl.DeviceIdType.MESH
HIGHEST = lax.Precision.HIGHEST

D_MODEL = 1024
DEPTH = 4
GRID_W = 64
NORM_EPS = 1e-6
GLA_HEADS = 4
GLA_DK = 128
GLA_DV = 256
GLA_KEY = GLA_HEADS * GLA_DK
GLA_VAL = GLA_HEADS * GLA_DV
GLA_RANK = 16
GLA_CHUNK = 64
GLA_GATE_NORMALIZER = 16.0
GLA_IN = 2 * GLA_KEY + 2 * GLA_VAL + 2 * GLA_RANK
GLA_IN_PAD = 3200
GLA_R_BLOCK = (2 * GLA_KEY + 2 * GLA_VAL) // 128
ATT_HD = 128
ATT_QH = 8
ATT_KVH = 2
ATT_GROUP = ATT_QH // ATT_KVH
ATT_QKV = (ATT_QH + 2 * ATT_KVH) * ATT_HD
ROPE_THETA = 10000.0
D_FF = 2816
ADAM_LR = 0.001
ADAM_B1 = 0.9
ADAM_B2 = 0.999
ADAM_EPS = 1e-08
ADAM_WD = 0.01
ADAM_STEP = 10

N_CHIPS = 4
LANES = 128
VMEM_LIMIT = 56 * 1024 * 1024


def _cp(sem):
    return pltpu.CompilerParams(dimension_semantics=sem, vmem_limit_bytes=VMEM_LIMIT)


def _pick(n, cands):
    for c in cands:
        if n % c == 0:
            return c
    return n


def _dg(a, b, ca, cb):
    return lax.dot_general(a, b, (((ca,), (cb,)), ((), ())), preferred_element_type=F32)


def _sigmoid(x):
    return 0.5 * jnp.tanh(0.5 * x) + 0.5


def _rmsnorm_fwd(x, w, name):
    S, D = x.shape
    tm = _pick(S, (512, 256))

    def body(x_ref, w_ref, h_ref):
        xv = x_ref[...]
        r = lax.rsqrt(jnp.mean(xv * xv, axis=-1, keepdims=True) + NORM_EPS)
        h_ref[...] = (xv * r * w_ref[...]).astype(BF16)

    return pl.pallas_call(
        body, name=name, grid=(S // tm,),
        in_specs=[pl.BlockSpec((tm, D), lambda i: (i, 0)), pl.BlockSpec((1, D), lambda i: (0, 0))],
        out_specs=pl.BlockSpec((tm, D), lambda i: (i, 0)),
        out_shape=jax.ShapeDtypeStruct((S, D), BF16),
        compiler_params=_cp(("parallel",)),
    )(x, w)


def _loss_grad(y, t, name):
    S, D = y.shape
    tm = _pick(S, (512, 256))

    def body(y_ref, t_ref, dy_ref, dyb_ref, loss_ref):
        i = pl.program_id(0)
        d = y_ref[...] - t_ref[...]
        dy = d * (1.0 / D)
        dy_ref[...] = dy
        dyb_ref[...] = dy.astype(BF16)
        sq = jnp.sum(jnp.sum(d * d, axis=1, keepdims=True), axis=0, keepdims=True)
        part = jnp.broadcast_to(sq * (0.5 / D), (1, LANES))

        @pl.when(i == 0)
        def _():
            loss_ref[...] = part

        @pl.when(i > 0)
        def _():
            loss_ref[...] += part

    return pl.pallas_call(
        body, name=name, grid=(S // tm,),
        in_specs=[pl.BlockSpec((tm, D), lambda i: (i, 0)), pl.BlockSpec((tm, D), lambda i: (i, 0))],
        out_specs=[pl.BlockSpec((tm, D), lambda i: (i, 0)), pl.BlockSpec((tm, D), lambda i: (i, 0)),
                   pl.BlockSpec((1, LANES), lambda i: (0, 0))],
        out_shape=[jax.ShapeDtypeStruct((S, D), F32), jax.ShapeDtypeStruct((S, D), BF16),
                   jax.ShapeDtypeStruct((1, LANES), F32)],
        compiler_params=_cp(("arbitrary",)),
    )(y, t)


WGRAD_TK = 512


def _wgrad(a, b, name, chips=None):
    S, Kw = a.shape
    pair = isinstance(b, (tuple, list))
    tn = b[0].shape[1] if pair else b.shape[1]
    N = 2 * tn if pair else tn
    tk = _pick(S, (WGRAD_TK, 256, 128))
    nk = S // tk
    if pair:
        b_specs = [pl.BlockSpec((tk, tn), lambda j, k: (jnp.where(j == 0, k, nk - 1), 0)),
                   pl.BlockSpec((tk, tn), lambda j, k: (jnp.where(j == 1, k, 0), 0))]
    else:
        b_specs = [pl.BlockSpec((tk, tn), lambda j, k: (k, 0))]
    if chips == "cols":
        cw = N // N_CHIPS
        span = tn // cw
        o_spec = pl.BlockSpec((span, Kw, cw), lambda j, k: (j, 0, 0))
        out_shape = jax.ShapeDtypeStruct((N_CHIPS, Kw, cw), F32)
    elif chips == "rows":
        assert not pair
        o_spec = pl.BlockSpec((N_CHIPS, Kw // N_CHIPS, N), lambda j, k: (0, 0, 0))
        out_shape = jax.ShapeDtypeStruct((N_CHIPS, Kw // N_CHIPS, N), F32)
    else:
        assert not pair
        o_spec = pl.BlockSpec((Kw, N), lambda j, k: (0, 0))
        out_shape = jax.ShapeDtypeStruct((Kw, N), F32)
    nb = len(b_specs)

    def body(*refs):
        a_ref, b_refs, o_ref, acc = refs[0], refs[1:1 + nb], refs[-2], refs[-1]
        j = pl.program_id(0)
        k = pl.program_id(1)

        @pl.when(k == 0)
        def _():
            acc[...] = jnp.zeros_like(acc)

        av = a_ref[...].astype(BF16)
        for h in range(nb):
            @pl.when(j == h)
            def _():
                acc[...] += _dg(av, b_refs[h][...].astype(BF16), 0, 0)

        @pl.when(k == nk - 1)
        def _():
            v = acc[...]
            if chips == "cols":
                for s in range(span):
                    o_ref[s] = v[:, s * cw:(s + 1) * cw]
            elif chips == "rows":
                rows = Kw // N_CHIPS
                for p in range(N_CHIPS):
                    o_ref[p] = v[p * rows:(p + 1) * rows, :]
            else:
                o_ref[...] = v

    return pl.pallas_call(
        body, name=name, grid=(nb, nk),
        in_specs=[pl.BlockSpec((tk, Kw), lambda j, k: (k, 0))] + b_specs,
        out_specs=o_spec, out_shape=out_shape,
        scratch_shapes=[pltpu.VMEM((Kw, tn), F32)],
        compiler_params=_cp(("parallel", "arbitrary")),
    )(a, *(tuple(b) if pair else (b,)))


def _matmul_rows(a, w, name, res=None, w_layer=None, transposed=False):
    M, K = a.shape
    if w_layer is not None:
        cw = w.shape[2]
        N = N_CHIPS * cw
        w_spec = pl.BlockSpec((N_CHIPS, K, cw), lambda i: (0, w_layer, 0))
    else:
        N = w.shape[0] if transposed else w.shape[1]
        assert w.shape[1 if transposed else 0] == K
        w_spec = pl.BlockSpec(w.shape, lambda i: (0, 0))
    tm = _pick(M, (512, 256, 128))
    has_res = res is not None

    def body(*refs):
        a_ref, w_ref = refs[0], refs[1]
        r_ref = refs[2] if has_res else None
        o_ref = refs[-1]
        av = a_ref[...].astype(BF16)
        if w_layer is not None:
            for p in range(N_CHIPS):
                o_ref[:, pl.ds(p * cw, cw)] = jnp.dot(av, w_ref[p], preferred_element_type=F32)
        else:
            v = _dg(av, w_ref[...], 1, 1 if transposed else 0)
            o_ref[...] = v + r_ref[...] if has_res else v

    row = pl.BlockSpec((tm, N), lambda i: (i, 0))
    return pl.pallas_call(
        body, name=name, grid=(M // tm,),
        in_specs=[pl.BlockSpec((tm, K), lambda i: (i, 0)), w_spec] + ([row] if has_res else []),
        out_specs=row, out_shape=jax.ShapeDtypeStruct((M, N), F32),
        compiler_params=_cp(("parallel",)),
    )(*((a, w) + ((res,) if has_res else ())))


def _dgrad_norm(dy, w, x, wn, dres, name, w_layer=None):
    pair = isinstance(dy, (tuple, list))
    M = dy[0].shape[0] if pair else dy.shape[0]
    Kp = 2 * dy[0].shape[1] if pair else dy.shape[1]
    D = x.shape[1]
    if w_layer is not None:
        cw = w.shape[2]
        assert N_CHIPS * cw == Kp and w.shape[1] % D == 0
        w_spec = pl.BlockSpec((N_CHIPS, D, cw), lambda i: (0, w_layer, 0))
    else:
        assert w.shape == (D, Kp)
        w_spec = pl.BlockSpec((D, Kp), lambda i: (0, 0))
    tm = _pick(M, (256, 128))
    width = Kp // 2 if pair else Kp
    dy_specs = [pl.BlockSpec((tm, width), lambda i: (i, 0))] * (2 if pair else 1)
    nd = len(dy_specs)

    def body(*refs):
        dy_refs = refs[:nd]
        w_ref, x_ref, wn_ref, dres_ref, dx_ref, dxb_ref, dwn_ref = refs[nd:]
        i = pl.program_id(0)
        if w_layer is not None:
            dh = None
            for p in range(N_CHIPS):
                src, off = divmod(p * cw, width)
                part = _dg(dy_refs[src][:, pl.ds(off, cw)], w_ref[p], 1, 1)
                dh = part if dh is None else dh + part
        else:
            dh = _dg(dy_refs[0][...], w_ref[...], 1, 1)
        xv = x_ref[...]
        r = lax.rsqrt(jnp.mean(xv * xv, axis=-1, keepdims=True) + NORM_EPS)
        yv = xv * r
        dyv = dh * wn_ref[...]
        dxv = r * (dyv - yv * jnp.mean(dyv * yv, axis=-1, keepdims=True)) + dres_ref[...]
        dx_ref[...] = dxv
        dxb_ref[...] = dxv.astype(BF16)
        part = jnp.sum(dh * yv, axis=0, keepdims=True)

        @pl.when(i == 0)
        def _():
            dwn_ref[...] = part

        @pl.when(i > 0)
        def _():
            dwn_ref[...] += part

    row = pl.BlockSpec((tm, D), lambda i: (i, 0))
    one = pl.BlockSpec((1, D), lambda i: (0, 0))
    return pl.pallas_call(
        body, name=name, grid=(M // tm,),
        in_specs=dy_specs + [w_spec, row, one, row],
        out_specs=[row, row, one],
        out_shape=[jax.ShapeDtypeStruct((M, D), F32), jax.ShapeDtypeStruct((M, D), BF16),
                   jax.ShapeDtypeStruct((1, D), F32)],
        compiler_params=_cp(("arbitrary",)),
    )(*(tuple(dy) if pair else (dy,)), w, x, wn, dres)


FFN_TN_FWD = 256
FFN_TN_BWD = 128
FFN_ROWS = 256
PAD = 8


def _conv3(pad_ref, w, r0, tr):
    um = pad_ref[pl.ds(PAD - 1 + r0, tr), :]
    uc = pad_ref[pl.ds(PAD + r0, tr), :]
    up = pad_ref[pl.ds(PAD + 1 + r0, tr), :]
    return w[0:1, :] * um + w[1:2, :] * uc + w[2:3, :] * up, (um, uc, up)


def _zero_pads(pad_ref, S, tn):
    pad_ref[pl.ds(0, PAD), :] = jnp.zeros((PAD, tn), F32)
    pad_ref[pl.ds(PAD + S, PAD), :] = jnp.zeros((PAD, tn), F32)


def _ffn_mid_fwd(h, wup, wconv, bconv, name):
    S, D = h.shape
    F = wup.shape[1] // 2
    tn = FFN_TN_FWD
    nb = F // tn
    tr = min(FFN_ROWS, S)

    def body(h_ref, wv_ref, wg_ref, cv_ref, cg_ref, bv_ref, bg_ref, a_ref, uv_ref, ug_ref):
        _zero_pads(uv_ref, S, tn)
        _zero_pads(ug_ref, S, tn)
        hv = h_ref[...]
        uv_ref[pl.ds(PAD, S), :] = jnp.dot(hv, wv_ref[...], preferred_element_type=F32)
        ug_ref[pl.ds(PAD, S), :] = jnp.dot(hv, wg_ref[...], preferred_element_type=F32)
        cwv, cwg, bv, bg = cv_ref[...], cg_ref[...], bv_ref[...], bg_ref[...]
        for r0 in range(0, S, tr):
            cv = _conv3(uv_ref, cwv, r0, tr)[0] + bv
            cg = _conv3(ug_ref, cwg, r0, tr)[0] + bg
            a_ref[pl.ds(r0, tr), :] = (cg * _sigmoid(cg) * cv).astype(BF16)

    col = lambda off: (lambda j: (0, j + off))
    padded = pl.BlockSpec((S + 2 * PAD, tn), col(0))
    return pl.pallas_call(
        body, name=name, grid=(nb,),
        in_specs=[pl.BlockSpec((S, D), lambda j: (0, 0)),
                  pl.BlockSpec((D, tn), col(0)), pl.BlockSpec((D, tn), col(nb)),
                  pl.BlockSpec((3, tn), col(0)), pl.BlockSpec((3, tn), col(nb)),
                  pl.BlockSpec((1, tn), col(0)), pl.BlockSpec((1, tn), col(nb))],
        out_specs=[pl.BlockSpec((S, tn), col(0)), padded, padded],
        out_shape=[jax.ShapeDtypeStruct((S, F), BF16), jax.ShapeDtypeStruct((S + 2 * PAD, F), F32),
                   jax.ShapeDtypeStruct((S + 2 * PAD, F), F32)],
        compiler_params=_cp(("parallel",)),
    )(h, wup, wup, wconv, wconv, bconv, bconv)


def _rows8(rows):
    n = rows[0].shape[1]
    idx = lax.broadcasted_iota(jnp.int32, (8, n), 0)
    out = jnp.zeros((8, n), F32)
    for k, r in enumerate(rows):
        out = jnp.where(idx == k, r, out)
    return out


def _ffn_mid_bwd(dyb, wdown, uv, ug, wconv, bconv, name):
    S, D = dyb.shape
    F = wdown.shape[0]
    tn = FFN_TN_BWD
    nb = F // tn
    tr = min(FFN_ROWS, S)

    def body(dy_ref, wd_ref, uv_ref, ug_ref, cv_ref, cg_ref, bv_ref, bg_ref,
             duv_ref, dug_ref, a_ref, gwv_ref, gwg_ref, pdv, pdg):
        for p in (pdv, pdg):
            _zero_pads(p, S, tn)
        wd = wd_ref[...]
        cwv, cwg, bv, bg = cv_ref[...], cg_ref[...], bv_ref[...], bg_ref[...]
        zero = jnp.zeros((1, tn), F32)
        gv = [zero, zero, zero, zero]
        gg = [zero, zero, zero, zero]
        for r0 in range(0, S, tr):
            cv, shv = _conv3(uv_ref, cwv, r0, tr)
            cg, shg = _conv3(ug_ref, cwg, r0, tr)
            cv = cv + bv
            cg = cg + bg
            sg = _sigmoid(cg)
            sl = cg * sg
            a_ref[pl.ds(r0, tr), :] = (sl * cv).astype(BF16)
            da = _dg(dy_ref[pl.ds(r0, tr), :], wd, 1, 1)
            dcv = da * sl
            dcg = da * cv * (sg * (1.0 + cg * (1.0 - sg)))
            pdv[pl.ds(PAD + r0, tr), :] = dcv
            pdg[pl.ds(PAD + r0, tr), :] = dcg
            for k in range(3):
                gv[k] = gv[k] + jnp.sum(dcv * shv[k], axis=0, keepdims=True)
                gg[k] = gg[k] + jnp.sum(dcg * shg[k], axis=0, keepdims=True)
            gv[3] = gv[3] + jnp.sum(dcv, axis=0, keepdims=True)
            gg[3] = gg[3] + jnp.sum(dcg, axis=0, keepdims=True)
        gwv_ref[...] = _rows8(gv)
        gwg_ref[...] = _rows8(gg)
        for r0 in range(0, S, tr):
            for pd, cw, out in ((pdv, cwv, duv_ref), (pdg, cwg, dug_ref)):
                dm = pd[pl.ds(PAD - 1 + r0, tr), :]
                dc = pd[pl.ds(PAD + r0, tr), :]
                dp = pd[pl.ds(PAD + 1 + r0, tr), :]
                out[pl.ds(r0, tr), :] = (cw[0:1, :] * dp + cw[1:2, :] * dc + cw[2:3, :] * dm).astype(BF16)

    col = lambda off: (lambda j: (0, j + off))
    blk = pl.BlockSpec((S, tn), col(0))
    padded = pl.BlockSpec((S + 2 * PAD, tn), col(0))
    g8 = pl.BlockSpec((8, tn), col(0))
    return pl.pallas_call(
        body, name=name, grid=(nb,),
        in_specs=[pl.BlockSpec((S, D), lambda j: (0, 0)), pl.BlockSpec((tn, D), lambda j: (j, 0)), padded, padded,
                  pl.BlockSpec((3, tn), col(0)), pl.BlockSpec((3, tn), col(nb)),
                  pl.BlockSpec((1, tn), col(0)), pl.BlockSpec((1, tn), col(nb))],
        out_specs=[blk, blk, blk, g8, g8],
        out_shape=[jax.ShapeDtypeStruct((S, F), BF16), jax.ShapeDtypeStruct((S, F), BF16),
                   jax.ShapeDtypeStruct((S, F), BF16), jax.ShapeDtypeStruct((8, F), F32),
                   jax.ShapeDtypeStruct((8, F), F32)],
        scratch_shapes=[pltpu.VMEM((S + 2 * PAD, tn), F32)] * 2,
        compiler_params=_cp(("parallel",)),
    )(dyb, wdown, uv, ug, wconv, wconv, bconv, bconv)


def _log_sigmoid(x):
    return jnp.minimum(x, 0.0) - jnp.log(1.0 + jnp.exp(-jnp.abs(x)))


def _gla_gate_fwd(proj, wgf, bgf, wgb, bgb, name):
    S = proj.shape[0]
    tm = _pick(S, (512, 256))

    def body(r_ref, wf_ref, bf_ref, wb_ref, bb_ref, laf_ref, lab_ref):
        r = r_ref[...].astype(BF16)
        lf = jnp.dot(r, wf_ref[...].astype(BF16), preferred_element_type=F32) + bf_ref[...]
        lb = jnp.dot(r, wb_ref[...].astype(BF16), preferred_element_type=F32) + bb_ref[...]
        laf_ref[...] = _log_sigmoid(lf) * (1.0 / GLA_GATE_NORMALIZER)
        lab_ref[...] = _log_sigmoid(lb) * (1.0 / GLA_GATE_NORMALIZER)

    full = lambda shp: pl.BlockSpec(shp, lambda i: (0, 0))
    row = pl.BlockSpec((tm, GLA_KEY), lambda i: (i, 0))
    return pl.pallas_call(
        body, name=name, grid=(S // tm,),
        in_specs=[pl.BlockSpec((tm, LANES), lambda i: (i, GLA_R_BLOCK)),
                  full((LANES, GLA_KEY)), full((1, GLA_KEY)), full((LANES, GLA_KEY)), full((1, GLA_KEY))],
        out_specs=[row, row],
        out_shape=[jax.ShapeDtypeStruct((S, GLA_KEY), F32)] * 2,
        compiler_params=_cp(("parallel",)),
    )(proj, wgf, bgf, wgb, bgb)


def _gla_gate_bwd(dlaf, dlab, proj, wgf, bgf, wgb, bgb, name):
    S = proj.shape[0]
    tm = _pick(S, (512, 256))

    def body(dlf_ref, dlb_ref, r_ref, wf_ref, bf_ref, wb_ref, bb_ref, dr_ref, dwf_ref, dbf_ref, dwb_ref, dbb_ref):
        i = pl.program_id(0)
        r = r_ref[...].astype(BF16)
        wf = wf_ref[...].astype(BF16)
        wb = wb_ref[...].astype(BF16)
        lf = jnp.dot(r, wf, preferred_element_type=F32) + bf_ref[...]
        lb = jnp.dot(r, wb, preferred_element_type=F32) + bb_ref[...]
        glf = dlf_ref[...] * (1.0 / GLA_GATE_NORMALIZER) * (1.0 / (1.0 + jnp.exp(lf)))
        glb = dlb_ref[...] * (1.0 / GLA_GATE_NORMALIZER) * (1.0 / (1.0 + jnp.exp(lb)))
        gfb = glf.astype(BF16)
        gbb = glb.astype(BF16)
        dr_ref[...] = _dg(gfb, wf, 1, 1) + _dg(gbb, wb, 1, 1)
        parts = (_dg(r, gfb, 0, 0), jnp.sum(glf, axis=0, keepdims=True),
                 _dg(r, gbb, 0, 0), jnp.sum(glb, axis=0, keepdims=True))
        outs = (dwf_ref, dbf_ref, dwb_ref, dbb_ref)

        @pl.when(i == 0)
        def _():
            for o, p in zip(outs, parts):
                o[...] = p

        @pl.when(i > 0)
        def _():
            for o, p in zip(outs, parts):
                o[...] += p

    full = lambda shp: pl.BlockSpec(shp, lambda i: (0, 0))
    row = pl.BlockSpec((tm, GLA_KEY), lambda i: (i, 0))
    return pl.pallas_call(
        body, name=name, grid=(S // tm,),
        in_specs=[row, row, pl.BlockSpec((tm, LANES), lambda i: (i, GLA_R_BLOCK)),
                  full((LANES, GLA_KEY)), full((1, GLA_KEY)), full((LANES, GLA_KEY)), full((1, GLA_KEY))],
        out_specs=[pl.BlockSpec((tm, LANES), lambda i: (i, 0)),
                   full((LANES, GLA_KEY)), full((1, GLA_KEY)), full((LANES, GLA_KEY)), full((1, GLA_KEY))],
        out_shape=[jax.ShapeDtypeStruct((S, LANES), F32),
                   jax.ShapeDtypeStruct((LANES, GLA_KEY), F32), jax.ShapeDtypeStruct((1, GLA_KEY), F32),
                   jax.ShapeDtypeStruct((LANES, GLA_KEY), F32), jax.ShapeDtypeStruct((1, GLA_KEY), F32)],
        compiler_params=_cp(("arbitrary",)),
    )(dlaf, dlab, proj, wgf, bgf, wgb, bgb)


def _gla_masks(rev):
    C = GLA_CHUNK
    t = lax.broadcasted_iota(jnp.int32, (C, C), 0)
    s = lax.broadcasted_iota(jnp.int32, (C, C), 1)
    if rev:
        return (s >= t), (s > t), (t >= s), (t > s)
    return (s <= t), (s <= t), (t <= s), (t <= s)


def _cum_dot(cum, x):
    return jnp.dot(cum.astype(F32), x, precision=HIGHEST, preferred_element_type=F32)


def _gla_chunk_common(q, k, la, cum, end_row):
    b = _cum_dot(cum, la)
    bend = b[end_row:end_row + 1, :]
    e = jnp.exp(b)
    qd = q * (GLA_DK ** -0.5) * e
    ei = jnp.exp(-b)
    ee = jnp.exp(bend - b)
    d = jnp.exp(bend)
    return e, ei, ee, d, qd, k * ei, k * ee


GLA_CB = 16


def _gla_specs(S, rev_order):
    n = S // GLA_CHUNK
    cb = min(GLA_CB, n)
    nblk = n // cb
    rows = cb * GLA_CHUNK
    ci = (lambda i: nblk - 1 - i) if rev_order else (lambda i: i)
    q_spec = pl.BlockSpec((rows, GLA_DK), lambda h, i: (ci(i), h))
    k_spec = pl.BlockSpec((rows, GLA_DK), lambda h, i: (ci(i), GLA_HEADS + h))
    v_spec = pl.BlockSpec((rows, GLA_DV), lambda h, i: (ci(i), GLA_KEY * 2 // GLA_DV + h))
    la_spec = pl.BlockSpec((rows, GLA_DK), lambda h, i: (ci(i), h))
    o_spec = pl.BlockSpec((rows, GLA_DV), lambda h, i: (ci(i), h))
    st_spec = pl.BlockSpec((1, cb, GLA_DV, GLA_DK), lambda h, i: (h, ci(i), 0, 0))
    return n, cb, nblk, q_spec, k_spec, v_spec, la_spec, o_spec, st_spec


def _gla_scan_fwd(proj, la, rev, name):
    S = proj.shape[0]
    C = GLA_CHUNK
    n, cb, nblk, q_spec, k_spec, v_spec, la_spec, o_spec, st_spec = _gla_specs(S, rev)
    end_row = 0 if rev else C - 1
    order = list(range(cb))[::-1] if rev else list(range(cb))

    def body(q_ref, k_ref, v_ref, la_ref, o_ref, st_ref, state):
        i = pl.program_id(1)

        @pl.when(i == 0)
        def _():
            state[...] = jnp.zeros_like(state)

        cum, mask, _, _ = _gla_masks(rev)
        pre, intra, kv = {}, {}, {}
        for cc in order:
            rows = pl.ds(cc * C, C)
            q, k, v, lav = q_ref[rows, :], k_ref[rows, :], v_ref[rows, :], la_ref[rows, :]
            _, _, _, d, qd, ki, ke = _gla_chunk_common(q, k, lav, cum, end_row)
            qdb, kib, keb, vb = qd.astype(BF16), ki.astype(BF16), ke.astype(BF16), v.astype(BF16)
            pre[cc] = (d, qdb)
            att = jnp.where(mask, _dg(qdb, kib, 1, 1), 0.0)
            intra[cc] = jnp.dot(att.astype(BF16), vb, preferred_element_type=F32)
            kv[cc] = _dg(vb, keb, 0, 0)
        st = state[...]
        for cc in order:
            d, qdb = pre[cc]
            o_ref[pl.ds(cc * C, C), :] = intra[cc] + _dg(qdb, st.astype(BF16), 1, 1)
            st_ref[0, cc] = st
            st = st * d + kv[cc]
        state[...] = st

    return pl.pallas_call(
        body, name=name, grid=(GLA_HEADS, nblk),
        in_specs=[q_spec, k_spec, v_spec, la_spec],
        out_specs=[o_spec, st_spec],
        out_shape=[jax.ShapeDtypeStruct((S, GLA_VAL), F32),
                   jax.ShapeDtypeStruct((GLA_HEADS, n, GLA_DV, GLA_DK), F32)],
        scratch_shapes=[pltpu.VMEM((GLA_DV, GLA_DK), F32)],
        compiler_params=_cp(("parallel", "arbitrary")),
    )(proj, proj, proj, la)


def _gla_scan_bwd(do, proj, la, states, rev, name):
    S = proj.shape[0]
    C = GLA_CHUNK
    n, cb, nblk, q_spec, k_spec, v_spec, la_spec, o_spec, st_spec = _gla_specs(S, not rev)
    end_row = 0 if rev else C - 1
    order = list(range(cb)) if rev else list(range(cb))[::-1]

    def body(do_ref, q_ref, k_ref, v_ref, la_ref, st_ref, dq_ref, dk_ref, dv_ref, dla_ref, gstate):
        i = pl.program_id(1)

        @pl.when(i == 0)
        def _():
            gstate[...] = jnp.zeros_like(gstate)

        cum, mask, cum_t, mask_t = _gla_masks(rev)
        g = gstate[...]
        for cc in order:
            rows = pl.ds(cc * C, C)
            q, k, v, lav = q_ref[rows, :], k_ref[rows, :], v_ref[rows, :], la_ref[rows, :]
            dov = do_ref[rows, :]
            st = st_ref[0, cc]
            e, ei, ee, d, qd, ki, ke = _gla_chunk_common(q, k, lav, cum, end_row)
            qdb, kib, keb, vb = qd.astype(BF16), ki.astype(BF16), ke.astype(BF16), v.astype(BF16)
            dob, gb, stb = dov.astype(BF16), g.astype(BF16), st.astype(BF16)
            att_t = jnp.where(mask_t, _dg(kib, qdb, 1, 1), 0.0)
            da = jnp.where(mask, _dg(dob, vb, 1, 1), 0.0)
            da_t = jnp.where(mask_t, _dg(vb, dob, 1, 1), 0.0)
            dv_ref[rows, :] = jnp.dot(att_t.astype(BF16), dob, preferred_element_type=F32) + _dg(keb, gb, 1, 1)
            dqd = (jnp.dot(da.astype(BF16), kib, preferred_element_type=F32)
                   + jnp.dot(dob, stb, preferred_element_type=F32))
            dki = jnp.dot(da_t.astype(BF16), qdb, preferred_element_type=F32)
            dke = jnp.dot(vb, gb, preferred_element_type=F32)
            dd = jnp.sum(st * g, axis=0, keepdims=True)
            g = g * d + _dg(dob, qdb, 0, 0)
            dq_ref[rows, :] = dqd * e * (GLA_DK ** -0.5)
            dk_ref[rows, :] = dki * ei + dke * ee
            dkeke = dke * ke
            db = dqd * qd - dki * ki - dkeke
            dbend = jnp.sum(dkeke, axis=0, keepdims=True) + dd * d
            dla_ref[rows, :] = _cum_dot(cum_t, db) + dbend
        gstate[...] = g

    key_out = la_spec
    return pl.pallas_call(
        body, name=name, grid=(GLA_HEADS, nblk),
        in_specs=[o_spec, q_spec, k_spec, v_spec, la_spec, st_spec],
        out_specs=[key_out, key_out, o_spec, key_out],
        out_shape=[jax.ShapeDtypeStruct((S, GLA_KEY), F32), jax.ShapeDtypeStruct((S, GLA_KEY), F32),
                   jax.ShapeDtypeStruct((S, GLA_VAL), F32), jax.ShapeDtypeStruct((S, GLA_KEY), F32)],
        scratch_shapes=[pltpu.VMEM((GLA_DV, GLA_DK), F32)],
        compiler_params=_cp(("parallel", "arbitrary")),
    )(do, proj, proj, proj, la, states)


def _gla_out_fwd(of, ob, proj, gn, name):
    S = of.shape[0]
    tm = _pick(S, (256, 128))
    gblk = (2 * GLA_KEY + GLA_VAL) // GLA_VAL

    def body(of_ref, ob_ref, g_ref, gn_ref, z_ref):
        gnv = gn_ref[...]
        for h in range(GLA_HEADS):
            cols = pl.ds(h * GLA_DV, GLA_DV)
            o = of_ref[:, cols] + ob_ref[:, cols]
            r = lax.rsqrt(jnp.mean(o * o, axis=-1, keepdims=True) + NORM_EPS)
            gv = g_ref[:, cols]
            z_ref[:, cols] = (o * r * gnv * (gv * _sigmoid(gv))).astype(BF16)

    row = pl.BlockSpec((tm, GLA_VAL), lambda i: (i, 0))
    return pl.pallas_call(
        body, name=name, grid=(S // tm,),
        in_specs=[row, row, pl.BlockSpec((tm, GLA_VAL), lambda i: (i, gblk)),
                  pl.BlockSpec((1, GLA_DV), lambda i: (0, 0))],
        out_specs=row,
        out_shape=jax.ShapeDtypeStruct((S, GLA_VAL), BF16),
        compiler_params=_cp(("parallel",)),
    )(of, ob, proj, gn)


def _gla_out_bwd(dz, of, ob, proj, gn, name):
    S = of.shape[0]
    tm = _pick(S, (256, 128))
    gblk = (2 * GLA_KEY + GLA_VAL) // GLA_VAL

    def body(dz_ref, of_ref, ob_ref, g_ref, gn_ref, do_ref, dg_ref, dgn_ref):
        i = pl.program_id(0)
        gnv = gn_ref[...]
        part = jnp.zeros((1, GLA_DV), F32)
        for h in range(GLA_HEADS):
            cols = pl.ds(h * GLA_DV, GLA_DV)
            o = of_ref[:, cols] + ob_ref[:, cols]
            r = lax.rsqrt(jnp.mean(o * o, axis=-1, keepdims=True) + NORM_EPS)
            y = o * r
            gv = g_ref[:, cols]
            sg = _sigmoid(gv)
            dzv = dz_ref[:, cols]
            dg_ref[:, cols] = dzv * (y * gnv) * (sg * (1.0 + gv * (1.0 - sg)))
            don = dzv * (gv * sg)
            part = part + jnp.sum(don * y, axis=0, keepdims=True)
            dy = don * gnv
            do_ref[:, cols] = r * (dy - y * jnp.mean(dy * y, axis=-1, keepdims=True))

        @pl.when(i == 0)
        def _():
            dgn_ref[...] = part

        @pl.when(i > 0)
        def _():
            dgn_ref[...] += part

    row = pl.BlockSpec((tm, GLA_VAL), lambda i: (i, 0))
    one = pl.BlockSpec((1, GLA_DV), lambda i: (0, 0))
    return pl.pallas_call(
        body, name=name, grid=(S // tm,),
        in_specs=[row, row, row, pl.BlockSpec((tm, GLA_VAL), lambda i: (i, gblk)), one],
        out_specs=[row, row, one],
        out_shape=[jax.ShapeDtypeStruct((S, GLA_VAL), F32), jax.ShapeDtypeStruct((S, GLA_VAL), F32),
                   jax.ShapeDtypeStruct((1, GLA_DV), F32)],
        compiler_params=_cp(("arbitrary",)),
    )(dz, of, ob, proj, gn)


N_QK_HEADS = ATT_QH + ATT_KVH


def _qk_prep_fwd(proj, qn, kn, rc, rs, name):
    S = proj.shape[0]
    tm = _pick(S, (256, 128))
    W = N_QK_HEADS * ATT_HD
    scale = ATT_HD ** -0.5

    def body(p_ref, qn_ref, kn_ref, rc_ref, rs_ref, v_in_ref, qk_ref, v_ref, kt_ref, vt_ref):
        c, s = rc_ref[...], rs_ref[...]
        for h in range(N_QK_HEADS):
            cols = pl.ds(h * ATT_HD, ATT_HD)
            w = qn_ref[...] if h < ATT_QH else kn_ref[...]
            xv = p_ref[:, cols]
            r = lax.rsqrt(jnp.mean(xv * xv, axis=-1, keepdims=True) + NORM_EPS)
            y = xv * r * w
            out = y * c + pltpu.roll(y, ATT_HD // 2, 1) * s
            if h < ATT_QH:
                qk_ref[:, cols] = (out * scale).astype(BF16)
            else:
                qk_ref[:, cols] = out.astype(BF16)
                kt_ref[pl.ds((h - ATT_QH) * ATT_HD, ATT_HD), :] = out.T.astype(BF16)
        v_ref[...] = v_in_ref[...].astype(BF16)
        for h in range(ATT_KVH):
            vt_ref[pl.ds(h * ATT_HD, ATT_HD), :] = v_in_ref[:, pl.ds(h * ATT_HD, ATT_HD)].T.astype(BF16)

    one = pl.BlockSpec((1, ATT_HD), lambda i: (0, 0))
    tab = pl.BlockSpec((tm, ATT_HD), lambda i: (i, 0))
    vw = ATT_KVH * ATT_HD
    tr = pl.BlockSpec((vw, tm), lambda i: (0, i))
    return pl.pallas_call(
        body, name=name, grid=(S // tm,),
        in_specs=[pl.BlockSpec((tm, W), lambda i: (i, 0)), one, one, tab, tab,
                  pl.BlockSpec((tm, vw), lambda i: (i, W // vw))],
        out_specs=[pl.BlockSpec((tm, W), lambda i: (i, 0)), pl.BlockSpec((tm, vw), lambda i: (i, 0)), tr, tr],
        out_shape=[jax.ShapeDtypeStruct((S, W), BF16), jax.ShapeDtypeStruct((S, vw), BF16),
                   jax.ShapeDtypeStruct((vw, S), BF16), jax.ShapeDtypeStruct((vw, S), BF16)],
        compiler_params=_cp(("parallel",)),
    )(proj, qn, kn, rc, rs, proj)


def _qk_prep_bwd(dqk, proj, qn, kn, rc, rs, name):
    S = proj.shape[0]
    tm = _pick(S, (256, 128))
    W = N_QK_HEADS * ATT_HD

    def body(d_ref, p_ref, qn_ref, kn_ref, rc_ref, rs_ref, dp_ref, dqn_ref, dkn_ref):
        i = pl.program_id(0)
        c, s = rc_ref[...], rs_ref[...]
        parts = [jnp.zeros((1, ATT_HD), F32), jnp.zeros((1, ATT_HD), F32)]
        for h in range(N_QK_HEADS):
            cols = pl.ds(h * ATT_HD, ATT_HD)
            w = qn_ref[...] if h < ATT_QH else kn_ref[...]
            dout = d_ref[:, cols]
            dy = dout * c + pltpu.roll(dout * s, ATT_HD // 2, 1)
            xv = p_ref[:, cols]
            r = lax.rsqrt(jnp.mean(xv * xv, axis=-1, keepdims=True) + NORM_EPS)
            xr = xv * r
            which = 0 if h < ATT_QH else 1
            parts[which] = parts[which] + jnp.sum(dy * xr, axis=0, keepdims=True)
            dxr = dy * w
            dp_ref[:, cols] = r * (dxr - xr * jnp.mean(dxr * xr, axis=-1, keepdims=True))

        @pl.when(i == 0)
        def _():
            dqn_ref[...] = parts[0]
            dkn_ref[...] = parts[1]

        @pl.when(i > 0)
        def _():
            dqn_ref[...] += parts[0]
            dkn_ref[...] += parts[1]

    one = pl.BlockSpec((1, ATT_HD), lambda i: (0, 0))
    tab = pl.BlockSpec((tm, ATT_HD), lambda i: (i, 0))
    row = pl.BlockSpec((tm, W), lambda i: (i, 0))
    return pl.pallas_call(
        body, name=name, grid=(S // tm,),
        in_specs=[row, row, one, one, tab, tab],
        out_specs=[row, one, one],
        out_shape=[jax.ShapeDtypeStruct((S, W), F32), jax.ShapeDtypeStruct((1, ATT_HD), F32),
                   jax.ShapeDtypeStruct((1, ATT_HD), F32)],
        compiler_params=_cp(("arbitrary",)),
    )(dqk, proj, qn, kn, rc, rs)


ATT_TQ = 256
LSE_ROWS = 8


def _attn_fwd(qk, vt, name):
    S = qk.shape[0]
    tq = min(ATT_TQ, S)

    def body(q_ref, k_ref, vt_ref, o_ref, lse_ref):
        st = _dg(k_ref[...], q_ref[...], 1, 1)
        m = jnp.max(st, axis=0, keepdims=True)
        pt = jnp.exp(st - m)
        l = jnp.sum(pt, axis=0, keepdims=True)
        ot = jnp.dot(vt_ref[...], pt.astype(BF16), preferred_element_type=F32)
        o_ref[...] = (ot * (1.0 / l)).T
        lse_ref[...] = jnp.broadcast_to(m + jnp.log(l), (LSE_ROWS, tq))

    qo = pl.BlockSpec((tq, ATT_HD), lambda h, i: (i, h))
    return pl.pallas_call(
        body, name=name, grid=(ATT_QH, S // tq),
        in_specs=[qo, pl.BlockSpec((S, ATT_HD), lambda h, i: (0, ATT_QH + h // ATT_GROUP)),
                  pl.BlockSpec((ATT_HD, S), lambda h, i: (h // ATT_GROUP, 0))],
        out_specs=[qo, pl.BlockSpec((LSE_ROWS, tq), lambda h, i: (h, i))],
        out_shape=[jax.ShapeDtypeStruct((S, ATT_QH * ATT_HD), F32),
                   jax.ShapeDtypeStruct((ATT_QH * LSE_ROWS, S), F32)],
        compiler_params=_cp(("parallel", "parallel")),
    )(qk, qk, vt)


def _attn_bwd(do, o, lse, qk, v, kt, name):
    S = qk.shape[0]
    tq = min(ATT_TQ, S)
    scale = ATT_HD ** -0.5

    def body(do_ref, o_ref, lse_ref, q_ref, k_ref, v_ref, kt_ref, dq_ref, dk_ref, dv_ref):
        g = pl.program_id(1)
        i = pl.program_id(2)

        @pl.when((g == 0) & (i == 0))
        def _():
            dk_ref[...] = jnp.zeros_like(dk_ref)
            dv_ref[...] = jnp.zeros_like(dv_ref)

        q = q_ref[...]
        dov = do_ref[...]
        dob = dov.astype(BF16)
        delta = jnp.sum((dov * o_ref[...]).T, axis=0, keepdims=True)
        st = _dg(k_ref[...], q, 1, 1)
        pt = jnp.exp(st - lse_ref[0:1, :])
        dpt = _dg(v_ref[...], dob, 1, 1)
        dst = (pt * (dpt - delta)).astype(BF16)
        dv_ref[...] += jnp.dot(pt.astype(BF16), dob, preferred_element_type=F32)
        dk_ref[...] += jnp.dot(dst, q, preferred_element_type=F32)
        dq_ref[...] = jnp.dot(kt_ref[...], dst, preferred_element_type=F32).T * scale

    qo = pl.BlockSpec((tq, ATT_HD), lambda kv, g, i: (i, kv * ATT_GROUP + g))
    kvo = pl.BlockSpec((S, ATT_HD), lambda kv, g, i: (0, kv))
    return pl.pallas_call(
        body, name=name, grid=(ATT_KVH, ATT_GROUP, S // tq),
        in_specs=[qo, qo, pl.BlockSpec((LSE_ROWS, tq), lambda kv, g, i: (kv * ATT_GROUP + g, i)), qo,
                  pl.BlockSpec((S, ATT_HD), lambda kv, g, i: (0, ATT_QH + kv)), kvo,
                  pl.BlockSpec((ATT_HD, S), lambda kv, g, i: (kv, 0))],
        out_specs=[qo, kvo, kvo],
        out_shape=[jax.ShapeDtypeStruct((S, ATT_QH * ATT_HD), F32),
                   jax.ShapeDtypeStruct((S, ATT_KVH * ATT_HD), F32),
                   jax.ShapeDtypeStruct((S, ATT_KVH * ATT_HD), F32)],
        compiler_params=_cp(("parallel", "arbitrary", "arbitrary")),
    )(do, o, lse, qk, qk, v, kt)


def _adamw(w, g, m, v, name):
    rows, cols = w.shape
    tr = rows
    for cand in (512, 256, 128, 64, 32, 16, 8):
        if rows % cand == 0 and cand * cols * 4 <= 2 * 1024 * 1024:
            tr = cand
            break

    def body(w_ref, g_ref, m_ref, v_ref, d_ref, nm_ref, nv_ref):
        gv = g_ref[...]
        nm = ADAM_B1 * m_ref[...] + (1.0 - ADAM_B1) * gv
        nv = ADAM_B2 * v_ref[...] + (1.0 - ADAM_B2) * (gv * gv)
        m_hat = nm / (1.0 - ADAM_B1 ** ADAM_STEP)
        v_hat = nv / (1.0 - ADAM_B2 ** ADAM_STEP)
        d_ref[...] = -ADAM_LR * (m_hat / (jnp.sqrt(v_hat) + ADAM_EPS) + ADAM_WD * w_ref[...])
        nm_ref[...] = nm
        nv_ref[...] = nv

    blk = pl.BlockSpec((tr, cols), lambda i: (i, 0))
    return pl.pallas_call(
        body, name=name, grid=(rows // tr,),
        in_specs=[blk] * 4, out_specs=[blk] * 3,
        out_shape=[jax.ShapeDtypeStruct((rows, cols), F32)] * 3,
        compiler_params=_cp(("parallel",)),
    )(w, g, m, v)


ANY = pl.BlockSpec(memory_space=pl.ANY)


def _place():
    return lax.axis_index("x"), lax.axis_index("y"), lax.axis_index("c")


def _other_chips(x, y):
    return [(1 - x, y), (x, 1 - y), (1 - x, 1 - y)]


def _half_rows(c, H):
    return pl.ds(pl.multiple_of(c * H, 8), H)


def _allreduce_small(v, name):
    R = v.shape[0]
    n_dev = 8

    def body(v_ref, sum_ref, all_ref, send_sems, recv_sems, local_sem):
        x, y, c = _place()
        me, sibling = (x, y, c), (x, y, 1 - c)
        chips = _other_chips(x, y)

        def rows(px, py, pc):
            return all_ref.at[pl.ds(pl.multiple_of((4 * px + 2 * py + pc) * R, 8), R), :]

        def copy(k, block, to, src=None):
            return pltpu.make_async_remote_copy(
                src_ref=rows(*block) if src is None else src, dst_ref=rows(*block),
                send_sem=send_sems.at[k], recv_sem=recv_sems.at[k], device_id=to, device_id_type=MESH)

        own = pltpu.make_async_copy(v_ref, rows(*me), local_sem)
        own.start()
        first = [copy(0, me, sibling, src=v_ref)]
        first += [copy(1 + j, me, (*chip, c), src=v_ref) for j, chip in enumerate(chips)]
        for cp in first:
            cp.start()
        passed = [copy(4 + j, (*chip, c), sibling) for j, chip in enumerate(chips)]
        for j, chip in enumerate(chips):
            copy(1 + j, (*chip, c), me).wait_recv()
            passed[j].start()
        copy(0, sibling, me).wait_recv()
        for j, chip in enumerate(chips):
            copy(4 + j, (*chip, 1 - c), me).wait_recv()
        for cp in first + passed:
            cp.wait_send()
        own.wait()
        acc = all_ref[pl.ds(0, R), :]
        for d in range(1, n_dev):
            acc = acc + all_ref[pl.ds(d * R, R), :]
        sum_ref[...] = acc

    vm = pl.BlockSpec(memory_space=pltpu.VMEM)
    return pl.pallas_call(
        body, name=name,
        in_specs=[vm], out_specs=[vm, vm],
        out_shape=[jax.ShapeDtypeStruct((R, LANES), F32), jax.ShapeDtypeStruct((n_dev * R, LANES), F32)],
        scratch_shapes=[pltpu.SemaphoreType.DMA((7,)), pltpu.SemaphoreType.DMA((7,)), pltpu.SemaphoreType.DMA],
    )(v)[0]


def _swap_other_half(bufs, name):
    n = len(bufs)
    halves = [b.shape[1] // 2 for b in bufs]

    def body(*refs):
        g_refs, got_refs = refs[:n], refs[n:2 * n]
        send_sems, recv_sems = refs[2 * n:]
        x, y, c = _place()
        copies = [pltpu.make_async_remote_copy(
            src_ref=g_refs[k].at[p, _half_rows(1 - c, halves[k])], dst_ref=got_refs[k].at[p],
            send_sem=send_sems.at[N_CHIPS * k + p], recv_sem=recv_sems.at[N_CHIPS * k + p],
            device_id=(x, y, 1 - c), device_id_type=MESH) for k in range(n) for p in range(N_CHIPS)]
        for cp in copies:
            cp.start()
        for cp in copies:
            cp.wait_recv()
        for cp in copies:
            cp.wait_send()

    return pl.pallas_call(
        body, name=name, in_specs=[ANY] * n, out_specs=[ANY] * n,
        out_shape=[jax.ShapeDtypeStruct((N_CHIPS, h, b.shape[2]), b.dtype) for b, h in zip(bufs, halves)],
        scratch_shapes=[pltpu.SemaphoreType.DMA((N_CHIPS * n,)), pltpu.SemaphoreType.DMA((N_CHIPS * n,))],
    )(*bufs)


def _join_halves(bufs, name):
    n = len(bufs)
    halves = [b.shape[0] // 2 for b in bufs]

    def body(*refs):
        outs = refs[n:2 * n]
        send_sems, recv_sems = refs[2 * n:]
        x, y, c = _place()

        def copy(k, core):
            blk = outs[k].at[_half_rows(core, halves[k])]
            return pltpu.make_async_remote_copy(src_ref=blk, dst_ref=blk, send_sem=send_sems.at[k],
                                                recv_sem=recv_sems.at[k], device_id=(x, y, 1 - c),
                                                device_id_type=MESH)

        sends = [copy(k, c) for k in range(n)]
        for cp in sends:
            cp.start()
        for k in range(n):
            copy(k, 1 - c).wait_recv()
        for cp in sends:
            cp.wait_send()

    return pl.pallas_call(
        body, name=name, in_specs=[ANY] * n, out_specs=[ANY] * n,
        out_shape=[jax.ShapeDtypeStruct(b.shape, b.dtype) for b in bufs],
        input_output_aliases={k: k for k in range(n)},
        scratch_shapes=[pltpu.SemaphoreType.DMA((n,)), pltpu.SemaphoreType.DMA((n,))],
    )(*bufs)


def _rs_rows(H, width):
    for cand in (1024, 512, 256, 128, 64, 32, 16):
        if H % cand == 0 and cand * width * 4 <= 1536 * 1024:
            return cand
    return H


def _add_sibling(g, got, c, me, name):
    _, H, width = got.shape
    tb = _rs_rows(H, width)
    nb = H // tb

    def body(sp_ref, g_ref, got_ref, sb_ref, sf_ref):
        p = pl.program_id(1)
        s = g_ref[0] + got_ref[0]
        sb_ref[0] = s.astype(BF16)

        @pl.when(p == sp_ref[1])
        def _():
            sf_ref[...] = s

    grid_spec = pltpu.PrefetchScalarGridSpec(
        num_scalar_prefetch=1, grid=(nb, N_CHIPS),
        in_specs=[pl.BlockSpec((1, tb, width), lambda i, p, sp: (p, sp[0] * nb + i, 0)),
                  pl.BlockSpec((1, tb, width), lambda i, p, sp: (p, i, 0))],
        out_specs=[pl.BlockSpec((1, tb, width), lambda i, p, sp: (p, i, 0)),
                   pl.BlockSpec((tb, width), lambda i, p, sp: (i, 0))])
    return pl.pallas_call(
        body, name=name, grid_spec=grid_spec,
        out_shape=[jax.ShapeDtypeStruct((N_CHIPS, H, width), BF16), jax.ShapeDtypeStruct((H, width), F32)],
        compiler_params=_cp(("arbitrary", "arbitrary")),
    )(jnp.stack([c, me]).astype(jnp.int32), g, got)


def _add_chips(sf, got, others_and_c, name):
    H, width = sf.shape
    tb = _rs_rows(H, width)
    nb = H // tb

    def body(sp_ref, sf_ref, r1_ref, r2_ref, r3_ref, out_ref):
        out_ref[...] = ((sf_ref[...] + r1_ref[0].astype(F32)) + r2_ref[0].astype(F32)) + r3_ref[0].astype(F32)

    def slot(k):
        return pl.BlockSpec((1, tb, width), lambda i, sp: (sp[k], i, 0))

    blk = pl.BlockSpec((tb, width), lambda i, sp: (i, 0))
    grid_spec = pltpu.PrefetchScalarGridSpec(
        num_scalar_prefetch=1, grid=(nb,), in_specs=[blk, slot(0), slot(1), slot(2)],
        out_specs=pl.BlockSpec((tb, width), lambda i, sp: (sp[3] * nb + i, 0)))
    return pl.pallas_call(
        body, name=name, grid_spec=grid_spec,
        out_shape=jax.ShapeDtypeStruct((2 * H, width), F32),
        compiler_params=_cp(("arbitrary",)),
    )(others_and_c.astype(jnp.int32), sf, got, got, got)


REPLICATED = ("norm_mix", "norm_ffn", "gla_b_gate_f", "gla_b_gate_b", "gla_norm", "attn_q_norm", "attn_k_norm",
              "ffn_b_conv")


PIECE_ROWS = 16


def _piece_rows(shape):
    n = 1
    for s in shape:
        n *= s
    rows = n // LANES
    return rows, -(-rows // PIECE_ROWS) * PIECE_ROWS


def _pack(pieces, dtype, row_multiple):
    flat = []
    for p in pieces:
        rows, padded = _piece_rows(p.shape)
        flat.append(jnp.pad(p.astype(dtype).reshape(rows, LANES), ((0, padded - rows), (0, 0))))
    rows = sum(f.shape[0] for f in flat)
    padded = -(-rows // row_multiple) * row_multiple
    if padded > rows:
        flat.append(jnp.zeros((padded - rows, LANES), dtype))
    return jnp.concatenate(flat, axis=0)


def _unpack(buf, shapes):
    out, r = [], 0
    for shp in shapes:
        rows, padded = _piece_rows(shp)
        out.append(buf[r:r + rows].reshape(shp))
        r += padded
    return out


def _own_slot(shard2d, me):
    return lax.dynamic_update_index_in_dim(lax.empty((N_CHIPS,) + shard2d.shape, shard2d.dtype), shard2d, me, 0)


def _layer_small(w, l):
    j = l // 2
    if l % 2 == 0:
        return [w["gla_w_gate_up_f"][j], w["gla_w_gate_up_b"][j], w["ffn_w_conv"][l]]
    return [w["ffn_w_conv"][l]]


def _layer_weight_bufs(w, l, me):
    j = l // 2
    mixer = ("gla_w_in", "gla_w_out") if l % 2 == 0 else ("attn_w_qkv", "attn_w_out")
    bufs = [_own_slot(w[n][j].astype(BF16), me) for n in mixer]
    bufs.append(_own_slot(_pack(_layer_small(w, l), F32, 32), me))
    bufs += [_own_slot(w["ffn_w_up"][l].astype(BF16), me), _own_slot(w["ffn_w_down"][l].astype(BF16), me)]
    return bufs


N_MIXER_BUFS = 3


def _layer_weights(w, l, got):
    rows = lambda t: t.reshape(-1, t.shape[2])
    cols = lambda t: jnp.concatenate([t[p] for p in range(N_CHIPS)], axis=1)
    out = {}
    if len(got) != N_MIXER_BUFS:
        up, down = got[-2:]
        out.update(up=up, up_full=cols(up), down=rows(down))
    if len(got) != 2:
        mix_in, mix_out, small = got[:N_MIXER_BUFS]
        shapes = [t.shape for t in _layer_small(w, l)]
        parts = [_unpack(small[p], shapes) for p in range(N_CHIPS)]
        full_small = [jnp.concatenate([parts[p][k] for p in range(N_CHIPS)], axis=-1) for k in range(len(shapes))]
        out.update(conv=full_small[-1])
        if l % 2 == 0:
            out.update(gla_in=jnp.pad(cols(mix_in), ((0, 0), (0, GLA_IN_PAD - GLA_IN))), gla_out=rows(mix_out),
                       gate_f=full_small[0], gate_b=full_small[1])
        else:
            out.update(qkv=mix_in, attn_out=rows(mix_out))
    return out


HBM = pl.BlockSpec(memory_space=pltpu.HBM)
SEM = pl.BlockSpec(memory_space=pltpu.SEMAPHORE)
SIDE_EFFECT = pltpu.SideEffectType.DATAFLOW_SIDE_EFFECTING


def _gather_start(bufs, after, name):
    n = len(bufs)
    halves = [b.shape[1] // 2 for b in bufs]

    def body(*refs):
        refs = refs[:n] + refs[n + 1:]
        send_sems, recv_sems = refs[n:2 * n], refs[2 * n:3 * n]
        outs, token = refs[3 * n:4 * n], refs[4 * n]
        x, y, c = _place()
        me = 2 * x + y
        for k in range(n):
            blk = outs[k].at[me, _half_rows(c, halves[k])]
            for px, py in _other_chips(x, y):
                pltpu.make_async_remote_copy(src_ref=blk, dst_ref=blk, send_sem=send_sems[k], recv_sem=recv_sems[k],
                                             device_id=(px, py, c), device_id_type=MESH).start()
        token[...] = jnp.zeros_like(token)

    res = pl.pallas_call(
        body, name=name,
        in_specs=[HBM] * n + [ANY],
        out_specs=[SEM] * (2 * n) + [HBM] * n + [pl.BlockSpec(memory_space=pltpu.VMEM)],
        out_shape=[pltpu.SemaphoreType.DMA(())] * (2 * n) + [pltpu.HBM(b.shape, b.dtype) for b in bufs]
        + [jax.ShapeDtypeStruct((8, LANES), F32)],
        input_output_aliases={k: 2 * n + k for k in range(n)},
        compiler_params=pltpu.CompilerParams(has_side_effects=SIDE_EFFECT),
    )(*[pltpu.with_memory_space_constraint(b, pltpu.HBM) for b in bufs], after)
    return res[:n], res[n:2 * n], res[2 * n:3 * n], res[3 * n]


def _gather_wait(send_sems, recv_sems, thru, after, name):
    n = len(thru)
    halves = [b.shape[1] // 2 for b in thru]

    def body(*refs):
        ss, rs = refs[n:2 * n], refs[2 * n:3 * n]
        outs = refs[3 * n + 1:]
        x, y, c = _place()
        for k in range(n):
            three = outs[k].at[pl.ds(0, N_CHIPS - 1), _half_rows(c, halves[k])]
            cp = pltpu.make_async_remote_copy(src_ref=three, dst_ref=three, send_sem=ss[k], recv_sem=rs[k],
                                              device_id=(x, y, c), device_id_type=MESH)
            cp.wait_send()
            cp.wait_recv()

    return pl.pallas_call(
        body, name=name,
        in_specs=[HBM] * n + [SEM] * (2 * n) + [ANY],
        out_specs=[HBM] * n,
        out_shape=[pltpu.HBM(b.shape, b.dtype) for b in thru],
        input_output_aliases={k: k for k in range(n)},
        compiler_params=pltpu.CompilerParams(has_side_effects=SIDE_EFFECT),
    )(*thru, *send_sems, *recv_sems, after)


def _send_start(sbs, name):
    n = len(sbs)

    def body(*refs):
        send_sems, recv_sems = refs[2 * n:3 * n], refs[3 * n:4 * n]
        srcs, lands, token = refs[4 * n:5 * n], refs[5 * n:6 * n], refs[6 * n]
        x, y, c = _place()
        me = 2 * x + y
        for k in range(n):
            for px, py in _other_chips(x, y):
                pltpu.make_async_remote_copy(src_ref=srcs[k].at[2 * px + py], dst_ref=lands[k].at[me],
                                             send_sem=send_sems[k], recv_sem=recv_sems[k],
                                             device_id=(px, py, c), device_id_type=MESH).start()
        token[...] = jnp.zeros_like(token)

    hbm = lambda a: pltpu.with_memory_space_constraint(a, pltpu.HBM)
    res = pl.pallas_call(
        body, name=name,
        in_specs=[HBM] * (2 * n),
        out_specs=[SEM] * (2 * n) + [HBM] * (2 * n) + [pl.BlockSpec(memory_space=pltpu.VMEM)],
        out_shape=[pltpu.SemaphoreType.DMA(())] * (2 * n) + [pltpu.HBM(s.shape, s.dtype) for s in sbs] * 2
        + [jax.ShapeDtypeStruct((8, LANES), F32)],
        input_output_aliases={k: 2 * n + k for k in range(2 * n)},
        compiler_params=pltpu.CompilerParams(has_side_effects=SIDE_EFFECT),
    )(*[hbm(s) for s in sbs], *[hbm(lax.empty(s.shape, s.dtype)) for s in sbs])
    return res[:n], res[n:2 * n], res[2 * n:3 * n], res[3 * n:4 * n], res[4 * n]


def _send_wait(send_sems, recv_sems, srcs, lands, after, name):
    n = len(srcs)

    def body(*refs):
        ss, rs = refs[2 * n:3 * n], refs[3 * n:4 * n]
        s_out, l_out = refs[4 * n + 1:5 * n + 1], refs[5 * n + 1:]
        x, y, c = _place()
        for k in range(n):
            cp = pltpu.make_async_remote_copy(src_ref=s_out[k].at[pl.ds(0, N_CHIPS - 1)],
                                              dst_ref=l_out[k].at[pl.ds(0, N_CHIPS - 1)], send_sem=ss[k],
                                              recv_sem=rs[k], device_id=(x, y, c), device_id_type=MESH)
            cp.wait_send()
            cp.wait_recv()

    res = pl.pallas_call(
        body, name=name,
        in_specs=[HBM] * (2 * n) + [SEM] * (2 * n) + [ANY],
        out_specs=[HBM] * (2 * n),
        out_shape=[pltpu.HBM(s.shape, s.dtype) for s in srcs] * 2,
        input_output_aliases={k: k for k in range(2 * n)},
        compiler_params=pltpu.CompilerParams(has_side_effects=SIDE_EFFECT),
    )(*srcs, *lands, *send_sems, *recv_sems, after)
    return res[n:]


def _pass_to_sibling(bufs, name):
    n = len(bufs)
    halves = [b.shape[1] // 2 for b in bufs]

    def body(*refs):
        outs = refs[n:2 * n]
        send_sems, recv_sems = refs[2 * n:]
        x, y, c = _place()
        chips = _other_chips(x, y)

        def copy(k, j, core):
            px, py = chips[j]
            blk = outs[k].at[2 * px + py, _half_rows(core, halves[k])]
            return pltpu.make_async_remote_copy(src_ref=blk, dst_ref=blk, send_sem=send_sems.at[3 * k + j],
                                                recv_sem=recv_sems.at[3 * k + j], device_id=(x, y, 1 - c),
                                                device_id_type=MESH)

        sends = [copy(k, j, c) for k in range(n) for j in range(3)]
        for cp in sends:
            cp.start()
        for k in range(n):
            for j in range(3):
                copy(k, j, 1 - c).wait_recv()
        for cp in sends:
            cp.wait_send()

    return pl.pallas_call(
        body, name=name,
        in_specs=[ANY] * n, out_specs=[ANY] * n,
        out_shape=[jax.ShapeDtypeStruct(b.shape, b.dtype) for b in bufs],
        input_output_aliases={k: k for k in range(n)},
        scratch_shapes=[pltpu.SemaphoreType.DMA((3 * n,)), pltpu.SemaphoreType.DMA((3 * n,))],
    )(*bufs)


def _rope_tables(S):
    rows = S // GRID_W
    row_idx = jnp.repeat(jnp.arange(rows, dtype=F32), GRID_W)
    col_idx = jnp.tile(jnp.arange(GRID_W, dtype=F32), rows)
    pairs = ATT_HD // 4
    inv_freq = ROPE_THETA ** (-jnp.arange(pairs, dtype=F32) / pairs)
    ang = jnp.concatenate([row_idx[:, None] * inv_freq, col_idx[:, None] * inv_freq], axis=-1)
    cos, sin = jnp.cos(ang), jnp.sin(ang)
    return jnp.concatenate([cos, cos], axis=-1), jnp.concatenate([-sin, sin], axis=-1)


def _gate_rows(w, first_row):
    return jnp.zeros((LANES, GLA_KEY), F32).at[first_row:first_row + GLA_RANK].set(w.astype(F32))


def _local_step(x, target, weights_of, grads_out, P):
    S = x.shape[0]
    rc, rs = _rope_tables(S)
    row = lambda a: a.reshape(1, -1)
    saved = []
    for i in range(DEPTH):
        j = i // 2
        W = dict(weights_of(i, "mix", x))
        nm = row(P["norm_mix"][i])
        h1 = _rmsnorm_fwd(x, nm, f"norm_mix_fwd{i}")
        if i % 2 == 0:
            wgf = _gate_rows(W["gate_f"], 0)
            wgb = _gate_rows(W["gate_b"], GLA_RANK)
            bgf, bgb = row(P["gla_b_gate_f"][j]), row(P["gla_b_gate_b"][j])
            gn = row(P["gla_norm"][j])
            proj = _matmul_rows(h1, W["gla_in"], f"gla_in{i}")
            laf, lab = _gla_gate_fwd(proj, wgf, bgf, wgb, bgb, f"gla_gate_fwd{i}")
            of, stf = _gla_scan_fwd(proj, laf, False, f"gla_scan_f_fwd{i}")
            ob, stb = _gla_scan_fwd(proj, lab, True, f"gla_scan_b_fwd{i}")
            z = _gla_out_fwd(of, ob, proj, gn, f"gla_out_fwd{i}")
            xm = _matmul_rows(z, W["gla_out"], f"gla_outproj{i}", res=x)
            mix = dict(proj=proj, laf=laf, lab=lab, of=of, ob=ob, stf=stf, stb=stb, z=z, wgf=wgf, wgb=wgb)
        else:
            proj = _matmul_rows(h1, W["qkv"], f"attn_qkv{i}", w_layer=0)
            qn, kn = row(P["attn_q_norm"][j]), row(P["attn_k_norm"][j])
            qk, vb, kt, vt = _qk_prep_fwd(proj, qn, kn, rc, rs, f"qk_prep_fwd{i}")
            o, lse = _attn_fwd(qk, vt, f"attn_fwd{i}")
            xm = _matmul_rows(o, W["attn_out"], f"attn_outproj{i}", res=x)
            mix = dict(proj=proj, qk=qk, vb=vb, kt=kt, o=o, lse=lse)
        W.update(weights_of(i, "ffn", xm))
        h2 = _rmsnorm_fwd(xm, row(P["norm_ffn"][i]), f"norm_ffn_fwd{i}")
        a, uv, ug = _ffn_mid_fwd(h2, W["up_full"], W["conv"], row(P["ffn_b_conv"][i]), f"ffn_mid_fwd{i}")
        xo = _matmul_rows(a, W["down"], f"ffn_down{i}", res=xm)
        saved.append(dict(x=x, h1=h1, xm=xm, h2=h2, uv=uv, ug=ug, mix=mix, W=W))
        x = xo

    dx, dxb, loss = _loss_grad(x, target, "loss")

    G = {n: [None] * (DEPTH if n.startswith(("norm", "ffn")) else DEPTH // 2) for n in REPLICATED}
    token = None
    for i in reversed(range(DEPTH)):
        j = i // 2
        sv = saved[i]
        mix = sv["mix"]
        W = sv["W"]
        bconv = row(P["ffn_b_conv"][i])
        if token is not None:
            t = token[0:1, 0:1]
            bconv = jnp.where(t == 0.0, bconv, t)
        duv, dug, a, gwv, gwg = _ffn_mid_bwd(dxb, W["down"], sv["uv"], sv["ug"], W["conv"], bconv, f"ffn_mid_bwd{i}")
        L = dict(down=_wgrad(a, dxb, f"ffn_down_wgrad{i}", chips="rows"),
                 up=_wgrad(sv["h2"], (duv, dug), f"ffn_up_wgrad{i}", chips="cols"),
                 small=[jnp.concatenate([gwv[:3], gwg[:3]], axis=1)])
        G["ffn_b_conv"][i] = jnp.concatenate([gwv[3], gwg[3]], axis=0)
        dxm, dxmb, dn = _dgrad_norm((duv, dug), W["up"], sv["xm"], row(P["norm_ffn"][i]), dx, f"ffn_up_dgrad{i}",
                                    w_layer=0)
        G["norm_ffn"][i] = dn[0]
        if i % 2 == 0:
            proj = mix["proj"]
            bgf, bgb = row(P["gla_b_gate_f"][j]), row(P["gla_b_gate_b"][j])
            gn = row(P["gla_norm"][j])
            dz = _matmul_rows(dxmb, W["gla_out"], f"gla_outproj_dgrad{i}", transposed=True)
            L["out"] = _wgrad(mix["z"], dxmb, f"gla_outproj_wgrad{i}", chips="rows")
            do, dg, dgn = _gla_out_bwd(dz, mix["of"], mix["ob"], proj, gn, f"gla_out_bwd{i}")
            G["gla_norm"][j] = dgn[0]
            dqf, dkf, dvf, dlaf = _gla_scan_bwd(do, proj, mix["laf"], mix["stf"], False, f"gla_scan_f_bwd{i}")
            dqb, dkb, dvb, dlab = _gla_scan_bwd(do, proj, mix["lab"], mix["stb"], True, f"gla_scan_b_bwd{i}")
            dr, dwf, dbf, dwb, dbb = _gla_gate_bwd(dlaf, dlab, proj, mix["wgf"], bgf, mix["wgb"], bgb,
                                                   f"gla_gate_bwd{i}")
            L["small"] = [dwf[:GLA_RANK], dwb[GLA_RANK:2 * GLA_RANK]] + L["small"]
            G["gla_b_gate_f"][j] = dbf[0]
            G["gla_b_gate_b"][j] = dbb[0]
            dproj = jnp.concatenate([dqf + dqb, dkf + dkb, dvf + dvb, dg, dr], axis=1).astype(BF16)
            L["mix_in"] = _wgrad(sv["h1"], dproj, f"gla_in_wgrad{i}")
            dx, dxb, dn = _dgrad_norm(dproj, W["gla_in"], sv["x"], row(P["norm_mix"][i]), dxm, f"mix_in_dgrad{i}")
        else:
            proj = mix["proj"]
            qn, kn = row(P["attn_q_norm"][j]), row(P["attn_k_norm"][j])
            do = _matmul_rows(dxmb, W["attn_out"], f"attn_outproj_dgrad{i}", transposed=True)
            L["out"] = _wgrad(mix["o"], dxmb, f"attn_outproj_wgrad{i}", chips="rows")
            dq, dk, dv = _attn_bwd(do, mix["o"], mix["lse"], mix["qk"], mix["vb"], mix["kt"], f"attn_bwd{i}")
            dqk = jnp.concatenate([dq, dk], axis=1)
            dpqk, dqn, dkn = _qk_prep_bwd(dqk, proj, qn, kn, rc, rs, f"qk_prep_bwd{i}")
            G["attn_q_norm"][j] = dqn[0]
            G["attn_k_norm"][j] = dkn[0]
            dproj = jnp.concatenate([dpqk, dv], axis=1).astype(BF16)
            L["mix_in"] = _wgrad(sv["h1"], dproj, f"attn_qkv_wgrad{i}", chips="cols")
            dx, dxb, dn = _dgrad_norm(dproj, W["qkv"], sv["x"], row(P["norm_mix"][i]), dxm, f"mix_in_dgrad{i}",
                                      w_layer=0)
        G["norm_mix"][i] = dn[0]
        token = grads_out(i, L)
    return loss, dx, G


def kernel(x, norm_mix, norm_ffn, gla_w_in, gla_w_gate_up_f, gla_b_gate_f, gla_w_gate_up_b, gla_b_gate_b, gla_norm, gla_w_out, attn_w_qkv, attn_q_norm, attn_k_norm, attn_w_out, ffn_w_up, ffn_w_conv, ffn_b_conv, ffn_w_down, loss_target, m_norm_mix, m_norm_ffn, m_gla_w_in, m_gla_w_gate_up_f, m_gla_b_gate_f, m_gla_w_gate_up_b, m_gla_b_gate_b, m_gla_norm, m_gla_w_out, m_attn_w_qkv, m_attn_q_norm, m_attn_k_norm, m_attn_w_out, m_ffn_w_up, m_ffn_w_conv, m_ffn_b_conv, m_ffn_w_down, v_norm_mix, v_norm_ffn, v_gla_w_in, v_gla_w_gate_up_f, v_gla_b_gate_f, v_gla_w_gate_up_b, v_gla_b_gate_b, v_gla_norm, v_gla_w_out, v_attn_w_qkv, v_attn_q_norm, v_attn_k_norm, v_attn_w_out, v_ffn_w_up, v_ffn_w_conv, v_ffn_b_conv, v_ffn_w_down):
    names = ("norm_mix", "norm_ffn", "gla_w_in", "gla_w_gate_up_f", "gla_b_gate_f", "gla_w_gate_up_b",
             "gla_b_gate_b", "gla_norm", "gla_w_out", "attn_w_qkv", "attn_q_norm", "attn_k_norm", "attn_w_out",
             "ffn_w_up", "ffn_w_conv", "ffn_b_conv", "ffn_w_down")
    w = dict(zip(names, (norm_mix, norm_ffn, gla_w_in, gla_w_gate_up_f, gla_b_gate_f, gla_w_gate_up_b,
                         gla_b_gate_b, gla_norm, gla_w_out, attn_w_qkv, attn_q_norm, attn_k_norm, attn_w_out,
                         ffn_w_up, ffn_w_conv, ffn_b_conv, ffn_w_down)))
    m = dict(zip(names, (m_norm_mix, m_norm_ffn, m_gla_w_in, m_gla_w_gate_up_f, m_gla_b_gate_f,
                         m_gla_w_gate_up_b, m_gla_b_gate_b, m_gla_norm, m_gla_w_out, m_attn_w_qkv, m_attn_q_norm,
                         m_attn_k_norm, m_attn_w_out, m_ffn_w_up, m_ffn_w_conv, m_ffn_b_conv, m_ffn_w_down)))
    v = dict(zip(names, (v_norm_mix, v_norm_ffn, v_gla_w_in, v_gla_w_gate_up_f, v_gla_b_gate_f,
                         v_gla_w_gate_up_b, v_gla_b_gate_b, v_gla_norm, v_gla_w_out, v_attn_w_qkv, v_attn_q_norm,
                         v_attn_k_norm, v_attn_w_out, v_ffn_w_up, v_ffn_w_conv, v_ffn_b_conv, v_ffn_w_down)))
    px, py, pc = _place()
    me = 2 * px + py

    started, token = [], w["norm_mix"]
    for l in range(DEPTH):
        started.append(_gather_start(_layer_weight_bufs(w, l, me), token, f"gather_start{l}"))
        token = started[-1][3]
    fetched = {}

    def weights_of(l, part, after):
        send_sems, recv_sems, thru, _ = started[l]
        if l == 0:
            pick = slice(0, N_MIXER_BUFS) if part == "mix" else slice(N_MIXER_BUFS, None)
            landed = _gather_wait(send_sems[pick], recv_sems[pick], thru[pick], token if part == "mix" else after,
                                  f"gather_wait{l}_{part}")
            return _layer_weights(w, l, _pass_to_sibling(landed, f"gather_pass{l}_{part}"))
        if part == "mix":
            landed = _gather_wait(send_sems, recv_sems, thru, after, f"gather_wait{l}")
            fetched[l] = _layer_weights(w, l, _pass_to_sibling(landed, f"gather_pass{l}"))
        return fetched[l]

    sent = {}

    def grads_out(l, L):
        mix_in = L["mix_in"]
        if l % 2 == 0:
            width = w["gla_w_in"].shape[2]
            mix_in = jnp.stack([mix_in[:, p * width:(p + 1) * width] for p in range(N_CHIPS)])
        cut = lambda t, p: lax.slice_in_dim(t, p * (t.shape[-1] // N_CHIPS), (p + 1) * (t.shape[-1] // N_CHIPS),
                                            axis=t.ndim - 1)
        small = jnp.stack([_pack([cut(t, p) for t in L["small"]], F32, 32) for p in range(N_CHIPS)])
        bufs = [mix_in, L["out"], small, L["up"], L["down"]]
        gots = _swap_other_half(bufs, f"grads{l}_to_sibling")
        sums = [_add_sibling(b, g, pc, me, f"grads{l}_add_sibling{k}") for k, (b, g) in enumerate(zip(bufs, gots))]
        send_sems, recv_sems, srcs, lands, tok = _send_start([s[0] for s in sums], f"grads{l}_start")
        sent[l] = (send_sems, recv_sems, srcs, lands, [s[1] for s in sums])
        return tok

    P = {n: w[n] for n in REPLICATED}

    loss_part, dx, grads = _local_step(x[0], loss_target[0], weights_of, grads_out, P)

    others_and_c = jnp.stack([jnp.where(me <= k, k + 1, k) for k in range(N_CHIPS - 1)] + [pc])
    mine = {}
    for l in reversed(range(DEPTH)):
        send_sems, recv_sems, srcs, lands, own = sent[l]
        landed = _send_wait(send_sems, recv_sems, srcs, lands, dx, f"grads{l}_wait")
        halves = [_add_chips(own[k], landed[k], others_and_c, f"grads{l}_add_chips{k}") for k in range(len(own))]
        mine[l] = _join_halves(halves, f"grads{l}_join_halves")
    gsh = {}
    for n, k, layers in (("ffn_w_up", 3, range(DEPTH)), ("ffn_w_down", 4, range(DEPTH)),
                         ("gla_w_in", 0, range(0, DEPTH, 2)), ("gla_w_out", 1, range(0, DEPTH, 2)),
                         ("attn_w_qkv", 0, range(1, DEPTH, 2)), ("attn_w_out", 1, range(1, DEPTH, 2))):
        gsh[n] = jnp.stack([mine[l][k] for l in layers])
    small_mine = [_unpack(mine[l][2], [t.shape for t in _layer_small(w, l)]) for l in range(DEPTH)]
    gsh["ffn_w_conv"] = jnp.stack([small_mine[l][-1] for l in range(DEPTH)])
    gsh["gla_w_gate_up_f"] = jnp.stack([small_mine[l][0] for l in range(0, DEPTH, 2)])
    gsh["gla_w_gate_up_b"] = jnp.stack([small_mine[l][1] for l in range(0, DEPTH, 2)])

    small = _pack([jnp.stack(grads[n]) for n in REPLICATED] + [loss_part], F32, 16)
    small_sum = _allreduce_small(small, "small_allreduce")
    parts = _unpack(small_sum, [w[n].shape for n in REPLICATED] + [(1, LANES)])
    gsh.update(dict(zip(REPLICATED, parts[:-1])))
    loss = parts[-1][0, 0]

    delta, new_m, new_v = {}, {}, {}
    for n in names:
        shp = w[n].shape
        two_d = (-1, shp[-1])
        d, nm, nv = _adamw(w[n].reshape(two_d), gsh[n].reshape(two_d), m[n].reshape(two_d), v[n].reshape(two_d),
                           f"adamw_{n}")
        delta[n], new_m[n], new_v[n] = d.reshape(shp), nm.reshape(shp), nv.reshape(shp)

    return (loss, dx[None], *[gsh[n] for n in names], *[delta[n] for n in names],
            *[new_m[n] for n in names], *[new_v[n] for n in names])
```

```python
import jax
import jax.numpy as jnp
from jax import lax
from jax.experimental import pallas as pl
from jax.experimental.pallas import tpu as pltpu

F32 = jnp.float32
BF16 = jnp.bfloat16
MESH = pl.DeviceIdType.MESH
HIGHEST = lax.Precision.HIGHEST

D_MODEL = 1024
DEPTH = 4
GRID_W = 64
NORM_EPS = 1e-6
GLA_HEADS = 4
GLA_DK = 128
GLA_DV = 256
GLA_KEY = GLA_HEADS * GLA_DK
GLA_VAL = GLA_HEADS * GLA_DV
GLA_RANK = 16
GLA_CHUNK = 64
GLA_GATE_NORMALIZER = 16.0
GLA_IN = 2 * GLA_KEY + 2 * GLA_VAL + 2 * GLA_RANK
GLA_IN_PAD = 3200
GLA_R_BLOCK = (2 * GLA_KEY + 2 * GLA_VAL) // 128
ATT_HD = 128
ATT_QH = 8
ATT_KVH = 2
ATT_GROUP = ATT_QH // ATT_KVH
ATT_QKV = (ATT_QH + 2 * ATT_KVH) * ATT_HD
ROPE_THETA = 10000.0
D_FF = 2816
ADAM_LR = 0.001
ADAM_B1 = 0.9
ADAM_B2 = 0.999
ADAM_EPS = 1e-08
ADAM_WD = 0.01
ADAM_STEP = 10

N_CHIPS = 4
LANES = 128
VMEM_LIMIT = 56 * 1024 * 1024


def _cp(sem):
    return pltpu.CompilerParams(dimension_semantics=sem, vmem_limit_bytes=VMEM_LIMIT)


def _pick(n, cands):
    for c in cands:
        if n % c == 0:
            return c
    return n


def _dg(a, b, ca, cb):
    return lax.dot_general(a, b, (((ca,), (cb,)), ((), ())), preferred_element_type=F32)


def _sigmoid(x):
    return 0.5 * jnp.tanh(0.5 * x) + 0.5


def _rmsnorm_fwd(x, w, name):
    S, D = x.shape
    tm = _pick(S, (512, 256))

    def body(x_ref, w_ref, h_ref):
        xv = x_ref[...]
        r = lax.rsqrt(jnp.mean(xv * xv, axis=-1, keepdims=True) + NORM_EPS)
        h_ref[...] = (xv * r * w_ref[...]).astype(BF16)

    return pl.pallas_call(
        body, name=name, grid=(S // tm,),
        in_specs=[pl.BlockSpec((tm, D), lambda i: (i, 0)), pl.BlockSpec((1, D), lambda i: (0, 0))],
        out_specs=pl.BlockSpec((tm, D), lambda i: (i, 0)),
        out_shape=jax.ShapeDtypeStruct((S, D), BF16),
        compiler_params=_cp(("parallel",)),
    )(x, w)


def _loss_grad(y, t, name):
    S, D = y.shape
    tm = _pick(S, (512, 256))

    def body(y_ref, t_ref, dy_ref, dyb_ref, loss_ref):
        i = pl.program_id(0)
        d = y_ref[...] - t_ref[...]
        dy = d * (1.0 / D)
        dy_ref[...] = dy
        dyb_ref[...] = dy.astype(BF16)
        sq = jnp.sum(jnp.sum(d * d, axis=1, keepdims=True), axis=0, keepdims=True)
        part = jnp.broadcast_to(sq * (0.5 / D), (1, LANES))

        @pl.when(i == 0)
        def _():
            loss_ref[...] = part

        @pl.when(i > 0)
        def _():
            loss_ref[...] += part

    return pl.pallas_call(
        body, name=name, grid=(S // tm,),
        in_specs=[pl.BlockSpec((tm, D), lambda i: (i, 0)), pl.BlockSpec((tm, D), lambda i: (i, 0))],
        out_specs=[pl.BlockSpec((tm, D), lambda i: (i, 0)), pl.BlockSpec((tm, D), lambda i: (i, 0)),
                   pl.BlockSpec((1, LANES), lambda i: (0, 0))],
        out_shape=[jax.ShapeDtypeStruct((S, D), F32), jax.ShapeDtypeStruct((S, D), BF16),
                   jax.ShapeDtypeStruct((1, LANES), F32)],
        compiler_params=_cp(("arbitrary",)),
    )(y, t)


WGRAD_TK = 512


def _wgrad(a, b, name, chips=None):
    S, Kw = a.shape
    pair = isinstance(b, (tuple, list))
    tn = b[0].shape[1] if pair else b.shape[1]
    N = 2 * tn if pair else tn
    tk = _pick(S, (WGRAD_TK, 256, 128))
    nk = S // tk
    if pair:
        b_specs = [pl.BlockSpec((tk, tn), lambda j, k: (jnp.where(j == 0, k, nk - 1), 0)),
                   pl.BlockSpec((tk, tn), lambda j, k: (jnp.where(j == 1, k, 0), 0))]
    else:
        b_specs = [pl.BlockSpec((tk, tn), lambda j, k: (k, 0))]
    if chips == "cols":
        cw = N // N_CHIPS
        span = tn // cw
        o_spec = pl.BlockSpec((span, Kw, cw), lambda j, k: (j, 0, 0))
        out_shape = jax.ShapeDtypeStruct((N_CHIPS, Kw, cw), F32)
    elif chips == "rows":
        assert not pair
        o_spec = pl.BlockSpec((N_CHIPS, Kw // N_CHIPS, N), lambda j, k: (0, 0, 0))
        out_shape = jax.ShapeDtypeStruct((N_CHIPS, Kw // N_CHIPS, N), F32)
    else:
        assert not pair
        o_spec = pl.BlockSpec((Kw, N), lambda j, k: (0, 0))
        out_shape = jax.ShapeDtypeStruct((Kw, N), F32)
    nb = len(b_specs)

    def body(*refs):
        a_ref, b_refs, o_ref, acc = refs[0], refs[1:1 + nb], refs[-2], refs[-1]
        j = pl.program_id(0)
        k = pl.program_id(1)

        @pl.when(k == 0)
        def _():
            acc[...] = jnp.zeros_like(acc)

        av = a_ref[...].astype(BF16)
        for h in range(nb):
            @pl.when(j == h)
            def _():
                acc[...] += _dg(av, b_refs[h][...].astype(BF16), 0, 0)

        @pl.when(k == nk - 1)
        def _():
            v = acc[...]
            if chips == "cols":
                for s in range(span):
                    o_ref[s] = v[:, s * cw:(s + 1) * cw]
            elif chips == "rows":
                rows = Kw // N_CHIPS
                for p in range(N_CHIPS):
                    o_ref[p] = v[p * rows:(p + 1) * rows, :]
            else:
                o_ref[...] = v

    return pl.pallas_call(
        body, name=name, grid=(nb, nk),
        in_specs=[pl.BlockSpec((tk, Kw), lambda j, k: (k, 0))] + b_specs,
        out_specs=o_spec, out_shape=out_shape,
        scratch_shapes=[pltpu.VMEM((Kw, tn), F32)],
        compiler_params=_cp(("parallel", "arbitrary")),
    )(a, *(tuple(b) if pair else (b,)))


def _matmul_rows(a, w, name, res=None, w_layer=None, transposed=False):
    M, K = a.shape
    if w_layer is not None:
        cw = w.shape[2]
        N = N_CHIPS * cw
        w_spec = pl.BlockSpec((N_CHIPS, K, cw), lambda i: (0, w_layer, 0))
    else:
        N = w.shape[0] if transposed else w.shape[1]
        assert w.shape[1 if transposed else 0] == K
        w_spec = pl.BlockSpec(w.shape, lambda i: (0, 0))
    tm = _pick(M, (512, 256, 128))
    has_res = res is not None

    def body(*refs):
        a_ref, w_ref = refs[0], refs[1]
        r_ref = refs[2] if has_res else None
        o_ref = refs[-1]
        av = a_ref[...].astype(BF16)
        if w_layer is not None:
            for p in range(N_CHIPS):
                o_ref[:, pl.ds(p * cw, cw)] = jnp.dot(av, w_ref[p], preferred_element_type=F32)
        else:
            v = _dg(av, w_ref[...], 1, 1 if transposed else 0)
            o_ref[...] = v + r_ref[...] if has_res else v

    row = pl.BlockSpec((tm, N), lambda i: (i, 0))
    return pl.pallas_call(
        body, name=name, grid=(M // tm,),
        in_specs=[pl.BlockSpec((tm, K), lambda i: (i, 0)), w_spec] + ([row] if has_res else []),
        out_specs=row, out_shape=jax.ShapeDtypeStruct((M, N), F32),
        compiler_params=_cp(("parallel",)),
    )(*((a, w) + ((res,) if has_res else ())))


def _dgrad_norm(dy, w, x, wn, dres, name, w_layer=None):
    pair = isinstance(dy, (tuple, list))
    M = dy[0].shape[0] if pair else dy.shape[0]
    Kp = 2 * dy[0].shape[1] if pair else dy.shape[1]
    D = x.shape[1]
    if w_layer is not None:
        cw = w.shape[2]
        assert N_CHIPS * cw == Kp and w.shape[1] % D == 0
        w_spec = pl.BlockSpec((N_CHIPS, D, cw), lambda i: (0, w_layer, 0))
    else:
        assert w.shape == (D, Kp)
        w_spec = pl.BlockSpec((D, Kp), lambda i: (0, 0))
    tm = _pick(M, (256, 128))
    width = Kp // 2 if pair else Kp
    dy_specs = [pl.BlockSpec((tm, width), lambda i: (i, 0))] * (2 if pair else 1)
    nd = len(dy_specs)

    def body(*refs):
        dy_refs = refs[:nd]
        w_ref, x_ref, wn_ref, dres_ref, dx_ref, dxb_ref, dwn_ref = refs[nd:]
        i = pl.program_id(0)
        if w_layer is not None:
            dh = None
            for p in range(N_CHIPS):
                src, off = divmod(p * cw, width)
                part = _dg(dy_refs[src][:, pl.ds(off, cw)], w_ref[p], 1, 1)
                dh = part if dh is None else dh + part
        else:
            dh = _dg(dy_refs[0][...], w_ref[...], 1, 1)
        xv = x_ref[...]
        r = lax.rsqrt(jnp.mean(xv * xv, axis=-1, keepdims=True) + NORM_EPS)
        yv = xv * r
        dyv = dh * wn_ref[...]
        dxv = r * (dyv - yv * jnp.mean(dyv * yv, axis=-1, keepdims=True)) + dres_ref[...]
        dx_ref[...] = dxv
        dxb_ref[...] = dxv.astype(BF16)
        part = jnp.sum(dh * yv, axis=0, keepdims=True)

        @pl.when(i == 0)
        def _():
            dwn_ref[...] = part

        @pl.when(i > 0)
        def _():
            dwn_ref[...] += part

    row = pl.BlockSpec((tm, D), lambda i: (i, 0))
    one = pl.BlockSpec((1, D), lambda i: (0, 0))
    return pl.pallas_call(
        body, name=name, grid=(M // tm,),
        in_specs=dy_specs + [w_spec, row, one, row],
        out_specs=[row, row, one],
        out_shape=[jax.ShapeDtypeStruct((M, D), F32), jax.ShapeDtypeStruct((M, D), BF16),
                   jax.ShapeDtypeStruct((1, D), F32)],
        compiler_params=_cp(("arbitrary",)),
    )(*(tuple(dy) if pair else (dy,)), w, x, wn, dres)


FFN_TN_FWD = 256
FFN_TN_BWD = 128
FFN_ROWS = 256
PAD = 8


def _conv3(pad_ref, w, r0, tr):
    um = pad_ref[pl.ds(PAD - 1 + r0, tr), :]
    uc = pad_ref[pl.ds(PAD + r0, tr), :]
    up = pad_ref[pl.ds(PAD + 1 + r0, tr), :]
    return w[0:1, :] * um + w[1:2, :] * uc + w[2:3, :] * up, (um, uc, up)


def _zero_pads(pad_ref, S, tn):
    pad_ref[pl.ds(0, PAD), :] = jnp.zeros((PAD, tn), F32)
    pad_ref[pl.ds(PAD + S, PAD), :] = jnp.zeros((PAD, tn), F32)


def _ffn_mid_fwd(h, wup, wconv, bconv, name):
    S, D = h.shape
    F = wup.shape[1] // 2
    tn = FFN_TN_FWD
    nb = F // tn
    tr = min(FFN_ROWS, S)

    def body(h_ref, wv_ref, wg_ref, cv_ref, cg_ref, bv_ref, bg_ref, a_ref, uv_ref, ug_ref):
        _zero_pads(uv_ref, S, tn)
        _zero_pads(ug_ref, S, tn)
        hv = h_ref[...]
        uv_ref[pl.ds(PAD, S), :] = jnp.dot(hv, wv_ref[...], preferred_element_type=F32)
        ug_ref[pl.ds(PAD, S), :] = jnp.dot(hv, wg_ref[...], preferred_element_type=F32)
        cwv, cwg, bv, bg = cv_ref[...], cg_ref[...], bv_ref[...], bg_ref[...]
        for r0 in range(0, S, tr):
            cv = _conv3(uv_ref, cwv, r0, tr)[0] + bv
            cg = _conv3(ug_ref, cwg, r0, tr)[0] + bg
            a_ref[pl.ds(r0, tr), :] = (cg * _sigmoid(cg) * cv).astype(BF16)

    col = lambda off: (lambda j: (0, j + off))
    padded = pl.BlockSpec((S + 2 * PAD, tn), col(0))
    return pl.pallas_call(
        body, name=name, grid=(nb,),
        in_specs=[pl.BlockSpec((S, D), lambda j: (0, 0)),
                  pl.BlockSpec((D, tn), col(0)), pl.BlockSpec((D, tn), col(nb)),
                  pl.BlockSpec((3, tn), col(0)), pl.BlockSpec((3, tn), col(nb)),
                  pl.BlockSpec((1, tn), col(0)), pl.BlockSpec((1, tn), col(nb))],
        out_specs=[pl.BlockSpec((S, tn), col(0)), padded, padded],
        out_shape=[jax.ShapeDtypeStruct((S, F), BF16), jax.ShapeDtypeStruct((S + 2 * PAD, F), F32),
                   jax.ShapeDtypeStruct((S + 2 * PAD, F), F32)],
        compiler_params=_cp(("parallel",)),
    )(h, wup, wup, wconv, wconv, bconv, bconv)


def _rows8(rows):
    n = rows[0].shape[1]
    idx = lax.broadcasted_iota(jnp.int32, (8, n), 0)
    out = jnp.zeros((8, n), F32)
    for k, r in enumerate(rows):
        out = jnp.where(idx == k, r, out)
    return out


def _ffn_mid_bwd(dyb, wdown, uv, ug, wconv, bconv, name):
    S, D = dyb.shape
    F = wdown.shape[0]
    tn = FFN_TN_BWD
    nb = F // tn
    tr = min(FFN_ROWS, S)

    def body(dy_ref, wd_ref, uv_ref, ug_ref, cv_ref, cg_ref, bv_ref, bg_ref,
             duv_ref, dug_ref, a_ref, gwv_ref, gwg_ref, pdv, pdg):
        for p in (pdv, pdg):
            _zero_pads(p, S, tn)
        wd = wd_ref[...]
        cwv, cwg, bv, bg = cv_ref[...], cg_ref[...], bv_ref[...], bg_ref[...]
        zero = jnp.zeros((1, tn), F32)
        gv = [zero, zero, zero, zero]
        gg = [zero, zero, zero, zero]
        for r0 in range(0, S, tr):
            cv, shv = _conv3(uv_ref, cwv, r0, tr)
            cg, shg = _conv3(ug_ref, cwg, r0, tr)
            cv = cv + bv
            cg = cg + bg
            sg = _sigmoid(cg)
            sl = cg * sg
            a_ref[pl.ds(r0, tr), :] = (sl * cv).astype(BF16)
            da = _dg(dy_ref[pl.ds(r0, tr), :], wd, 1, 1)
            dcv = da * sl
            dcg = da * cv * (sg * (1.0 + cg * (1.0 - sg)))
            pdv[pl.ds(PAD + r0, tr), :] = dcv
            pdg[pl.ds(PAD + r0, tr), :] = dcg
            for k in range(3):
                gv[k] = gv[k] + jnp.sum(dcv * shv[k], axis=0, keepdims=True)
                gg[k] = gg[k] + jnp.sum(dcg * shg[k], axis=0, keepdims=True)
            gv[3] = gv[3] + jnp.sum(dcv, axis=0, keepdims=True)
            gg[3] = gg[3] + jnp.sum(dcg, axis=0, keepdims=True)
        gwv_ref[...] = _rows8(gv)
        gwg_ref[...] = _rows8(gg)
        for r0 in range(0, S, tr):
            for pd, cw, out in ((pdv, cwv, duv_ref), (pdg, cwg, dug_ref)):
                dm = pd[pl.ds(PAD - 1 + r0, tr), :]
                dc = pd[pl.ds(PAD + r0, tr), :]
                dp = pd[pl.ds(PAD + 1 + r0, tr), :]
                out[pl.ds(r0, tr), :] = (cw[0:1, :] * dp + cw[1:2, :] * dc + cw[2:3, :] * dm).astype(BF16)

    col = lambda off: (lambda j: (0, j + off))
    blk = pl.BlockSpec((S, tn), col(0))
    padded = pl.BlockSpec((S + 2 * PAD, tn), col(0))
    g8 = pl.BlockSpec((8, tn), col(0))
    return pl.pallas_call(
        body, name=name, grid=(nb,),
        in_specs=[pl.BlockSpec((S, D), lambda j: (0, 0)), pl.BlockSpec((tn, D), lambda j: (j, 0)), padded, padded,
                  pl.BlockSpec((3, tn), col(0)), pl.BlockSpec((3, tn), col(nb)),
                  pl.BlockSpec((1, tn), col(0)), pl.BlockSpec((1, tn), col(nb))],
        out_specs=[blk, blk, blk, g8, g8],
        out_shape=[jax.ShapeDtypeStruct((S, F), BF16), jax.ShapeDtypeStruct((S, F), BF16),
                   jax.ShapeDtypeStruct((S, F), BF16), jax.ShapeDtypeStruct((8, F), F32),
                   jax.ShapeDtypeStruct((8, F), F32)],
        scratch_shapes=[pltpu.VMEM((S + 2 * PAD, tn), F32)] * 2,
        compiler_params=_cp(("parallel",)),
    )(dyb, wdown, uv, ug, wconv, wconv, bconv, bconv)


def _log_sigmoid(x):
    return jnp.minimum(x, 0.0) - jnp.log(1.0 + jnp.exp(-jnp.abs(x)))


def _gla_gate_fwd(proj, wgf, bgf, wgb, bgb, name):
    S = proj.shape[0]
    tm = _pick(S, (512, 256))

    def body(r_ref, wf_ref, bf_ref, wb_ref, bb_ref, laf_ref, lab_ref):
        r = r_ref[...].astype(BF16)
        lf = jnp.dot(r, wf_ref[...].astype(BF16), preferred_element_type=F32) + bf_ref[...]
        lb = jnp.dot(r, wb_ref[...].astype(BF16), preferred_element_type=F32) + bb_ref[...]
        laf_ref[...] = _log_sigmoid(lf) * (1.0 / GLA_GATE_NORMALIZER)
        lab_ref[...] = _log_sigmoid(lb) * (1.0 / GLA_GATE_NORMALIZER)

    full = lambda shp: pl.BlockSpec(shp, lambda i: (0, 0))
    row = pl.BlockSpec((tm, GLA_KEY), lambda i: (i, 0))
    return pl.pallas_call(
        body, name=name, grid=(S // tm,),
        in_specs=[pl.BlockSpec((tm, LANES), lambda i: (i, GLA_R_BLOCK)),
                  full((LANES, GLA_KEY)), full((1, GLA_KEY)), full((LANES, GLA_KEY)), full((1, GLA_KEY))],
        out_specs=[row, row],
        out_shape=[jax.ShapeDtypeStruct((S, GLA_KEY), F32)] * 2,
        compiler_params=_cp(("parallel",)),
    )(proj, wgf, bgf, wgb, bgb)


def _gla_gate_bwd(dlaf, dlab, proj, wgf, bgf, wgb, bgb, name):
    S = proj.shape[0]
    tm = _pick(S, (512, 256))

    def body(dlf_ref, dlb_ref, r_ref, wf_ref, bf_ref, wb_ref, bb_ref, dr_ref, dwf_ref, dbf_ref, dwb_ref, dbb_ref):
        i = pl.program_id(0)
        r = r_ref[...].astype(BF16)
        wf = wf_ref[...].astype(BF16)
        wb = wb_ref[...].astype(BF16)
        lf = jnp.dot(r, wf, preferred_element_type=F32) + bf_ref[...]
        lb = jnp.dot(r, wb, preferred_element_type=F32) + bb_ref[...]
        glf = dlf_ref[...] * (1.0 / GLA_GATE_NORMALIZER) * (1.0 / (1.0 + jnp.exp(lf)))
        glb = dlb_ref[...] * (1.0 / GLA_GATE_NORMALIZER) * (1.0 / (1.0 + jnp.exp(lb)))
        gfb = glf.astype(BF16)
        gbb = glb.astype(BF16)
        dr_ref[...] = _dg(gfb, wf, 1, 1) + _dg(gbb, wb, 1, 1)
        parts = (_dg(r, gfb, 0, 0), jnp.sum(glf, axis=0, keepdims=True),
                 _dg(r, gbb, 0, 0), jnp.sum(glb, axis=0, keepdims=True))
        outs = (dwf_ref, dbf_ref, dwb_ref, dbb_ref)

        @pl.when(i == 0)
        def _():
            for o, p in zip(outs, parts):
                o[...] = p

        @pl.when(i > 0)
        def _():
            for o, p in zip(outs, parts):
                o[...] += p

    full = lambda shp: pl.BlockSpec(shp, lambda i: (0, 0))
    row = pl.BlockSpec((tm, GLA_KEY), lambda i: (i, 0))
    return pl.pallas_call(
        body, name=name, grid=(S // tm,),
        in_specs=[row, row, pl.BlockSpec((tm, LANES), lambda i: (i, GLA_R_BLOCK)),
                  full((LANES, GLA_KEY)), full((1, GLA_KEY)), full((LANES, GLA_KEY)), full((1, GLA_KEY))],
        out_specs=[pl.BlockSpec((tm, LANES), lambda i: (i, 0)),
                   full((LANES, GLA_KEY)), full((1, GLA_KEY)), full((LANES, GLA_KEY)), full((1, GLA_KEY))],
        out_shape=[jax.ShapeDtypeStruct((S, LANES), F32),
                   jax.ShapeDtypeStruct((LANES, GLA_KEY), F32), jax.ShapeDtypeStruct((1, GLA_KEY), F32),
                   jax.ShapeDtypeStruct((LANES, GLA_KEY), F32), jax.ShapeDtypeStruct((1, GLA_KEY), F32)],
        compiler_params=_cp(("arbitrary",)),
    )(dlaf, dlab, proj, wgf, bgf, wgb, bgb)


def _gla_masks(rev):
    C = GLA_CHUNK
    t = lax.broadcasted_iota(jnp.int32, (C, C), 0)
    s = lax.broadcasted_iota(jnp.int32, (C, C), 1)
    if rev:
        return (s >= t), (s > t), (t >= s), (t > s)
    return (s <= t), (s <= t), (t <= s), (t <= s)


def _cum_dot(cum, x):
    return jnp.dot(cum.astype(F32), x, precision=HIGHEST, preferred_element_type=F32)


def _gla_chunk_common(q, k, la, cum, end_row):
    b = _cum_dot(cum, la)
    bend = b[end_row:end_row + 1, :]
    e = jnp.exp(b)
    qd = q * (GLA_DK ** -0.5) * e
    ei = jnp.exp(-b)
    ee = jnp.exp(bend - b)
    d = jnp.exp(bend)
    return e, ei, ee, d, qd, k * ei, k * ee


GLA_CB = 16


def _gla_specs(S, rev_order):
    n = S // GLA_CHUNK
    cb = min(GLA_CB, n)
    nblk = n // cb
    rows = cb * GLA_CHUNK
    ci = (lambda i: nblk - 1 - i) if rev_order else (lambda i: i)
    q_spec = pl.BlockSpec((rows, GLA_DK), lambda h, i: (ci(i), h))
    k_spec = pl.BlockSpec((rows, GLA_DK), lambda h, i: (ci(i), GLA_HEADS + h))
    v_spec = pl.BlockSpec((rows, GLA_DV), lambda h, i: (ci(i), GLA_KEY * 2 // GLA_DV + h))
    la_spec = pl.BlockSpec((rows, GLA_DK), lambda h, i: (ci(i), h))
    o_spec = pl.BlockSpec((rows, GLA_DV), lambda h, i: (ci(i), h))
    st_spec = pl.BlockSpec((1, cb, GLA_DV, GLA_DK), lambda h, i: (h, ci(i), 0, 0))
    return n, cb, nblk, q_spec, k_spec, v_spec, la_spec, o_spec, st_spec


def _gla_scan_fwd(proj, la, rev, name):
    S = proj.shape[0]
    C = GLA_CHUNK
    n, cb, nblk, q_spec, k_spec, v_spec, la_spec, o_spec, st_spec = _gla_specs(S, rev)
    end_row = 0 if rev else C - 1
    order = list(range(cb))[::-1] if rev else list(range(cb))

    def body(q_ref, k_ref, v_ref, la_ref, o_ref, st_ref, state):
        i = pl.program_id(1)

        @pl.when(i == 0)
        def _():
            state[...] = jnp.zeros_like(state)

        cum, mask, _, _ = _gla_masks(rev)
        pre, intra, kv = {}, {}, {}
        for cc in order:
            rows = pl.ds(cc * C, C)
            q, k, v, lav = q_ref[rows, :], k_ref[rows, :], v_ref[rows, :], la_ref[rows, :]
            _, _, _, d, qd, ki, ke = _gla_chunk_common(q, k, lav, cum, end_row)
            qdb, kib, keb, vb = qd.astype(BF16), ki.astype(BF16), ke.astype(BF16), v.astype(BF16)
            pre[cc] = (d, qdb)
            att = jnp.where(mask, _dg(qdb, kib, 1, 1), 0.0)
            intra[cc] = jnp.dot(att.astype(BF16), vb, preferred_element_type=F32)
            kv[cc] = _dg(vb, keb, 0, 0)
        st = state[...]
        for cc in order:
            d, qdb = pre[cc]
            o_ref[pl.ds(cc * C, C), :] = intra[cc] + _dg(qdb, st.astype(BF16), 1, 1)
            st_ref[0, cc] = st
            st = st * d + kv[cc]
        state[...] = st

    return pl.pallas_call(
        body, name=name, grid=(GLA_HEADS, nblk),
        in_specs=[q_spec, k_spec, v_spec, la_spec],
        out_specs=[o_spec, st_spec],
        out_shape=[jax.ShapeDtypeStruct((S, GLA_VAL), F32),
                   jax.ShapeDtypeStruct((GLA_HEADS, n, GLA_DV, GLA_DK), F32)],
        scratch_shapes=[pltpu.VMEM((GLA_DV, GLA_DK), F32)],
        compiler_params=_cp(("parallel", "arbitrary")),
    )(proj, proj, proj, la)


def _gla_scan_bwd(do, proj, la, states, rev, name):
    S = proj.shape[0]
    C = GLA_CHUNK
    n, cb, nblk, q_spec, k_spec, v_spec, la_spec, o_spec, st_spec = _gla_specs(S, not rev)
    end_row = 0 if rev else C - 1
    order = list(range(cb)) if rev else list(range(cb))[::-1]

    def body(do_ref, q_ref, k_ref, v_ref, la_ref, st_ref, dq_ref, dk_ref, dv_ref, dla_ref, gstate):
        i = pl.program_id(1)

        @pl.when(i == 0)
        def _():
            gstate[...] = jnp.zeros_like(gstate)

        cum, mask, cum_t, mask_t = _gla_masks(rev)
        g = gstate[...]
        for cc in order:
            rows = pl.ds(cc * C, C)
            q, k, v, lav = q_ref[rows, :], k_ref[rows, :], v_ref[rows, :], la_ref[rows, :]
            dov = do_ref[rows, :]
            st = st_ref[0, cc]
            e, ei, ee, d, qd, ki, ke = _gla_chunk_common(q, k, lav, cum, end_row)
            qdb, kib, keb, vb = qd.astype(BF16), ki.astype(BF16), ke.astype(BF16), v.astype(BF16)
            dob, gb, stb = dov.astype(BF16), g.astype(BF16), st.astype(BF16)
            att_t = jnp.where(mask_t, _dg(kib, qdb, 1, 1), 0.0)
            da = jnp.where(mask, _dg(dob, vb, 1, 1), 0.0)
            da_t = jnp.where(mask_t, _dg(vb, dob, 1, 1), 0.0)
            dv_ref[rows, :] = jnp.dot(att_t.astype(BF16), dob, preferred_element_type=F32) + _dg(keb, gb, 1, 1)
            dqd = (jnp.dot(da.astype(BF16), kib, preferred_element_type=F32)
                   + jnp.dot(dob, stb, preferred_element_type=F32))
            dki = jnp.dot(da_t.astype(BF16), qdb, preferred_element_type=F32)
            dke = jnp.dot(vb, gb, preferred_element_type=F32)
            dd = jnp.sum(st * g, axis=0, keepdims=True)
            g = g * d + _dg(dob, qdb, 0, 0)
            dq_ref[rows, :] = dqd * e * (GLA_DK ** -0.5)
            dk_ref[rows, :] = dki * ei + dke * ee
            dkeke = dke * ke
            db = dqd * qd - dki * ki - dkeke
            dbend = jnp.sum(dkeke, axis=0, keepdims=True) + dd * d
            dla_ref[rows, :] = _cum_dot(cum_t, db) + dbend
        gstate[...] = g

    key_out = la_spec
    return pl.pallas_call(
        body, name=name, grid=(GLA_HEADS, nblk),
        in_specs=[o_spec, q_spec, k_spec, v_spec, la_spec, st_spec],
        out_specs=[key_out, key_out, o_spec, key_out],
        out_shape=[jax.ShapeDtypeStruct((S, GLA_KEY), F32), jax.ShapeDtypeStruct((S, GLA_KEY), F32),
                   jax.ShapeDtypeStruct((S, GLA_VAL), F32), jax.ShapeDtypeStruct((S, GLA_KEY), F32)],
        scratch_shapes=[pltpu.VMEM((GLA_DV, GLA_DK), F32)],
        compiler_params=_cp(("parallel", "arbitrary")),
    )(do, proj, proj, proj, la, states)


def _gla_out_fwd(of, ob, proj, gn, name):
    S = of.shape[0]
    tm = _pick(S, (256, 128))
    gblk = (2 * GLA_KEY + GLA_VAL) // GLA_VAL

    def body(of_ref, ob_ref, g_ref, gn_ref, z_ref):
        gnv = gn_ref[...]
        for h in range(GLA_HEADS):
            cols = pl.ds(h * GLA_DV, GLA_DV)
            o = of_ref[:, cols] + ob_ref[:, cols]
            r = lax.rsqrt(jnp.mean(o * o, axis=-1, keepdims=True) + NORM_EPS)
            gv = g_ref[:, cols]
            z_ref[:, cols] = (o * r * gnv * (gv * _sigmoid(gv))).astype(BF16)

    row = pl.BlockSpec((tm, GLA_VAL), lambda i: (i, 0))
    return pl.pallas_call(
        body, name=name, grid=(S // tm,),
        in_specs=[row, row, pl.BlockSpec((tm, GLA_VAL), lambda i: (i, gblk)),
                  pl.BlockSpec((1, GLA_DV), lambda i: (0, 0))],
        out_specs=row,
        out_shape=jax.ShapeDtypeStruct((S, GLA_VAL), BF16),
        compiler_params=_cp(("parallel",)),
    )(of, ob, proj, gn)


def _gla_out_bwd(dz, of, ob, proj, gn, name):
    S = of.shape[0]
    tm = _pick(S, (256, 128))
    gblk = (2 * GLA_KEY + GLA_VAL) // GLA_VAL

    def body(dz_ref, of_ref, ob_ref, g_ref, gn_ref, do_ref, dg_ref, dgn_ref):
        i = pl.program_id(0)
        gnv = gn_ref[...]
        part = jnp.zeros((1, GLA_DV), F32)
        for h in range(GLA_HEADS):
            cols = pl.ds(h * GLA_DV, GLA_DV)
            o = of_ref[:, cols] + ob_ref[:, cols]
            r = lax.rsqrt(jnp.mean(o * o, axis=-1, keepdims=True) + NORM_EPS)
            y = o * r
            gv = g_ref[:, cols]
            sg = _sigmoid(gv)
            dzv = dz_ref[:, cols]
            dg_ref[:, cols] = dzv * (y * gnv) * (sg * (1.0 + gv * (1.0 - sg)))
            don = dzv * (gv * sg)
            part = part + jnp.sum(don * y, axis=0, keepdims=True)
            dy = don * gnv
            do_ref[:, cols] = r * (dy - y * jnp.mean(dy * y, axis=-1, keepdims=True))

        @pl.when(i == 0)
        def _():
            dgn_ref[...] = part

        @pl.when(i > 0)
        def _():
            dgn_ref[...] += part

    row = pl.BlockSpec((tm, GLA_VAL), lambda i: (i, 0))
    one = pl.BlockSpec((1, GLA_DV), lambda i: (0, 0))
    return pl.pallas_call(
        body, name=name, grid=(S // tm,),
        in_specs=[row, row, row, pl.BlockSpec((tm, GLA_VAL), lambda i: (i, gblk)), one],
        out_specs=[row, row, one],
        out_shape=[jax.ShapeDtypeStruct((S, GLA_VAL), F32), jax.ShapeDtypeStruct((S, GLA_VAL), F32),
                   jax.ShapeDtypeStruct((1, GLA_DV), F32)],
        compiler_params=_cp(("arbitrary",)),
    )(dz, of, ob, proj, gn)


N_QK_HEADS = ATT_QH + ATT_KVH


def _qk_prep_fwd(proj, qn, kn, rc, rs, name):
    S = proj.shape[0]
    tm = _pick(S, (256, 128))
    W = N_QK_HEADS * ATT_HD
    scale = ATT_HD ** -0.5

    def body(p_ref, qn_ref, kn_ref, rc_ref, rs_ref, v_in_ref, qk_ref, v_ref, kt_ref, vt_ref):
        c, s = rc_ref[...], rs_ref[...]
        for h in range(N_QK_HEADS):
            cols = pl.ds(h * ATT_HD, ATT_HD)
            w = qn_ref[...] if h < ATT_QH else kn_ref[...]
            xv = p_ref[:, cols]
            r = lax.rsqrt(jnp.mean(xv * xv, axis=-1, keepdims=True) + NORM_EPS)
            y = xv * r * w
            out = y * c + pltpu.roll(y, ATT_HD // 2, 1) * s
            if h < ATT_QH:
                qk_ref[:, cols] = (out * scale).astype(BF16)
            else:
                qk_ref[:, cols] = out.astype(BF16)
                kt_ref[pl.ds((h - ATT_QH) * ATT_HD, ATT_HD), :] = out.T.astype(BF16)
        v_ref[...] = v_in_ref[...].astype(BF16)
        for h in range(ATT_KVH):
            vt_ref[pl.ds(h * ATT_HD, ATT_HD), :] = v_in_ref[:, pl.ds(h * ATT_HD, ATT_HD)].T.astype(BF16)

    one = pl.BlockSpec((1, ATT_HD), lambda i: (0, 0))
    tab = pl.BlockSpec((tm, ATT_HD), lambda i: (i, 0))
    vw = ATT_KVH * ATT_HD
    tr = pl.BlockSpec((vw, tm), lambda i: (0, i))
    return pl.pallas_call(
        body, name=name, grid=(S // tm,),
        in_specs=[pl.BlockSpec((tm, W), lambda i: (i, 0)), one, one, tab, tab,
                  pl.BlockSpec((tm, vw), lambda i: (i, W // vw))],
        out_specs=[pl.BlockSpec((tm, W), lambda i: (i, 0)), pl.BlockSpec((tm, vw), lambda i: (i, 0)), tr, tr],
        out_shape=[jax.ShapeDtypeStruct((S, W), BF16), jax.ShapeDtypeStruct((S, vw), BF16),
                   jax.ShapeDtypeStruct((vw, S), BF16), jax.ShapeDtypeStruct((vw, S), BF16)],
        compiler_params=_cp(("parallel",)),
    )(proj, qn, kn, rc, rs, proj)


def _qk_prep_bwd(dqk, proj, qn, kn, rc, rs, name):
    S = proj.shape[0]
    tm = _pick(S, (256, 128))
    W = N_QK_HEADS * ATT_HD

    def body(d_ref, p_ref, qn_ref, kn_ref, rc_ref, rs_ref, dp_ref, dqn_ref, dkn_ref):
        i = pl.program_id(0)
        c, s = rc_ref[...], rs_ref[...]
        parts = [jnp.zeros((1, ATT_HD), F32), jnp.zeros((1, ATT_HD), F32)]
        for h in range(N_QK_HEADS):
            cols = pl.ds(h * ATT_HD, ATT_HD)
            w = qn_ref[...] if h < ATT_QH else kn_ref[...]
            dout = d_ref[:, cols]
            dy = dout * c + pltpu.roll(dout * s, ATT_HD // 2, 1)
            xv = p_ref[:, cols]
            r = lax.rsqrt(jnp.mean(xv * xv, axis=-1, keepdims=True) + NORM_EPS)
            xr = xv * r
            which = 0 if h < ATT_QH else 1
            parts[which] = parts[which] + jnp.sum(dy * xr, axis=0, keepdims=True)
            dxr = dy * w
            dp_ref[:, cols] = r * (dxr - xr * jnp.mean(dxr * xr, axis=-1, keepdims=True))

        @pl.when(i == 0)
        def _():
            dqn_ref[...] = parts[0]
            dkn_ref[...] = parts[1]

        @pl.when(i > 0)
        def _():
            dqn_ref[...] += parts[0]
            dkn_ref[...] += parts[1]

    one = pl.BlockSpec((1, ATT_HD), lambda i: (0, 0))
    tab = pl.BlockSpec((tm, ATT_HD), lambda i: (i, 0))
    row = pl.BlockSpec((tm, W), lambda i: (i, 0))
    return pl.pallas_call(
        body, name=name, grid=(S // tm,),
        in_specs=[row, row, one, one, tab, tab],
        out_specs=[row, one, one],
        out_shape=[jax.ShapeDtypeStruct((S, W), F32), jax.ShapeDtypeStruct((1, ATT_HD), F32),
                   jax.ShapeDtypeStruct((1, ATT_HD), F32)],
        compiler_params=_cp(("arbitrary",)),
    )(dqk, proj, qn, kn, rc, rs)


ATT_TQ = 1024
LSE_ROWS = 8


def _attn_fwd(qk, vt, name):
    S = qk.shape[0]
    tq = min(ATT_TQ, S)

    def body(q_ref, k_ref, vt_ref, o_ref, lse_ref):
        st = _dg(k_ref[...], q_ref[...], 1, 1)
        m = jnp.max(st, axis=0, keepdims=True)
        pt = jnp.exp(st - m)
        l = jnp.sum(pt, axis=0, keepdims=True)
        ot = jnp.dot(vt_ref[...], pt.astype(BF16), preferred_element_type=F32)
        o_ref[...] = (ot * (1.0 / l)).T
        lse_ref[...] = jnp.broadcast_to(m + jnp.log(l), (LSE_ROWS, tq))

    qo = pl.BlockSpec((tq, ATT_HD), lambda h, i: (i, h))
    return pl.pallas_call(
        body, name=name, grid=(ATT_QH, S // tq),
        in_specs=[qo, pl.BlockSpec((S, ATT_HD), lambda h, i: (0, ATT_QH + h // ATT_GROUP)),
                  pl.BlockSpec((ATT_HD, S), lambda h, i: (h // ATT_GROUP, 0))],
        out_specs=[qo, pl.BlockSpec((LSE_ROWS, tq), lambda h, i: (h, i))],
        out_shape=[jax.ShapeDtypeStruct((S, ATT_QH * ATT_HD), F32),
                   jax.ShapeDtypeStruct((ATT_QH * LSE_ROWS, S), F32)],
        compiler_params=_cp(("parallel", "parallel")),
    )(qk, qk, vt)


def _attn_bwd(do, o, lse, qk, v, kt, name):
    S = qk.shape[0]
    tq = min(ATT_TQ, S)
    scale = ATT_HD ** -0.5

    def body(do_ref, o_ref, lse_ref, q_ref, k_ref, v_ref, kt_ref, dq_ref, dk_ref, dv_ref):
        g = pl.program_id(1)
        i = pl.program_id(2)

        @pl.when((g == 0) & (i == 0))
        def _():
            dk_ref[...] = jnp.zeros_like(dk_ref)
            dv_ref[...] = jnp.zeros_like(dv_ref)

        q = q_ref[...]
        dov = do_ref[...]
        dob = dov.astype(BF16)
        delta = jnp.sum((dov * o_ref[...]).T, axis=0, keepdims=True)
        st = _dg(k_ref[...], q, 1, 1)
        pt = jnp.exp(st - lse_ref[0:1, :])
        dpt = _dg(v_ref[...], dob, 1, 1)
        dst = (pt * (dpt - delta)).astype(BF16)
        dv_ref[...] += jnp.dot(pt.astype(BF16), dob, preferred_element_type=F32)
        dk_ref[...] += jnp.dot(dst, q, preferred_element_type=F32)
        dq_ref[...] = jnp.dot(kt_ref[...], dst, preferred_element_type=F32).T * scale

    qo = pl.BlockSpec((tq, ATT_HD), lambda kv, g, i: (i, kv * ATT_GROUP + g))
    kvo = pl.BlockSpec((S, ATT_HD), lambda kv, g, i: (0, kv))
    return pl.pallas_call(
        body, name=name, grid=(ATT_KVH, ATT_GROUP, S // tq),
        in_specs=[qo, qo, pl.BlockSpec((LSE_ROWS, tq), lambda kv, g, i: (kv * ATT_GROUP + g, i)), qo,
                  pl.BlockSpec((S, ATT_HD), lambda kv, g, i: (0, ATT_QH + kv)), kvo,
                  pl.BlockSpec((ATT_HD, S), lambda kv, g, i: (kv, 0))],
        out_specs=[qo, kvo, kvo],
        out_shape=[jax.ShapeDtypeStruct((S, ATT_QH * ATT_HD), F32),
                   jax.ShapeDtypeStruct((S, ATT_KVH * ATT_HD), F32),
                   jax.ShapeDtypeStruct((S, ATT_KVH * ATT_HD), F32)],
        compiler_params=_cp(("parallel", "arbitrary", "arbitrary")),
    )(do, o, lse, qk, qk, v, kt)


def _adamw(w, g, m, v, name):
    rows, cols = w.shape
    tr = rows
    for cand in (512, 256, 128, 64, 32, 16, 8):
        if rows % cand == 0 and cand * cols * 4 <= 2 * 1024 * 1024:
            tr = cand
            break

    def body(w_ref, g_ref, m_ref, v_ref, d_ref, nm_ref, nv_ref):
        gv = g_ref[...]
        nm = ADAM_B1 * m_ref[...] + (1.0 - ADAM_B1) * gv
        nv = ADAM_B2 * v_ref[...] + (1.0 - ADAM_B2) * (gv * gv)
        m_hat = nm / (1.0 - ADAM_B1 ** ADAM_STEP)
        v_hat = nv / (1.0 - ADAM_B2 ** ADAM_STEP)
        d_ref[...] = -ADAM_LR * (m_hat / (jnp.sqrt(v_hat) + ADAM_EPS) + ADAM_WD * w_ref[...])
        nm_ref[...] = nm
        nv_ref[...] = nv

    blk = pl.BlockSpec((tr, cols), lambda i: (i, 0))
    return pl.pallas_call(
        body, name=name, grid=(rows // tr,),
        in_specs=[blk] * 4, out_specs=[blk] * 3,
        out_shape=[jax.ShapeDtypeStruct((rows, cols), F32)] * 3,
        compiler_params=_cp(("parallel",)),
    )(w, g, m, v)


ANY = pl.BlockSpec(memory_space=pl.ANY)


def _place():
    return lax.axis_index("x"), lax.axis_index("y"), lax.axis_index("c")


def _other_chips(x, y):
    return [(1 - x, y), (x, 1 - y), (1 - x, 1 - y)]


def _half_rows(c, H):
    return pl.ds(pl.multiple_of(c * H, 8), H)


def _allreduce_small(v, name):
    R = v.shape[0]
    n_dev = 8

    def body(v_ref, sum_ref, all_ref, send_sems, recv_sems, local_sem):
        x, y, c = _place()
        me, sibling = (x, y, c), (x, y, 1 - c)
        chips = _other_chips(x, y)

        def rows(px, py, pc):
            return all_ref.at[pl.ds(pl.multiple_of((4 * px + 2 * py + pc) * R, 8), R), :]

        def copy(k, block, to, src=None):
            return pltpu.make_async_remote_copy(
                src_ref=rows(*block) if src is None else src, dst_ref=rows(*block),
                send_sem=send_sems.at[k], recv_sem=recv_sems.at[k], device_id=to, device_id_type=MESH)

        own = pltpu.make_async_copy(v_ref, rows(*me), local_sem)
        own.start()
        first = [copy(0, me, sibling, src=v_ref)]
        first += [copy(1 + j, me, (*chip, c), src=v_ref) for j, chip in enumerate(chips)]
        for cp in first:
            cp.start()
        passed = [copy(4 + j, (*chip, c), sibling) for j, chip in enumerate(chips)]
        for j, chip in enumerate(chips):
            copy(1 + j, (*chip, c), me).wait_recv()
            passed[j].start()
        copy(0, sibling, me).wait_recv()
        for j, chip in enumerate(chips):
            copy(4 + j, (*chip, 1 - c), me).wait_recv()
        for cp in first + passed:
            cp.wait_send()
        own.wait()
        acc = all_ref[pl.ds(0, R), :]
        for d in range(1, n_dev):
            acc = acc + all_ref[pl.ds(d * R, R), :]
        sum_ref[...] = acc

    vm = pl.BlockSpec(memory_space=pltpu.VMEM)
    return pl.pallas_call(
        body, name=name,
        in_specs=[vm], out_specs=[vm, vm],
        out_shape=[jax.ShapeDtypeStruct((R, LANES), F32), jax.ShapeDtypeStruct((n_dev * R, LANES), F32)],
        scratch_shapes=[pltpu.SemaphoreType.DMA((7,)), pltpu.SemaphoreType.DMA((7,)), pltpu.SemaphoreType.DMA],
    )(v)[0]


def _swap_other_half(bufs, name):
    n = len(bufs)
    halves = [b.shape[1] // 2 for b in bufs]

    def body(*refs):
        g_refs, got_refs = refs[:n], refs[n:2 * n]
        send_sems, recv_sems = refs[2 * n:]
        x, y, c = _place()
        copies = [pltpu.make_async_remote_copy(
            src_ref=g_refs[k].at[p, _half_rows(1 - c, halves[k])], dst_ref=got_refs[k].at[p],
            send_sem=send_sems.at[N_CHIPS * k + p], recv_sem=recv_sems.at[N_CHIPS * k + p],
            device_id=(x, y, 1 - c), device_id_type=MESH) for k in range(n) for p in range(N_CHIPS)]
        for cp in copies:
            cp.start()
        for cp in copies:
            cp.wait_recv()
        for cp in copies:
            cp.wait_send()

    return pl.pallas_call(
        body, name=name, in_specs=[ANY] * n, out_specs=[ANY] * n,
        out_shape=[jax.ShapeDtypeStruct((N_CHIPS, h, b.shape[2]), b.dtype) for b, h in zip(bufs, halves)],
        scratch_shapes=[pltpu.SemaphoreType.DMA((N_CHIPS * n,)), pltpu.SemaphoreType.DMA((N_CHIPS * n,))],
    )(*bufs)


def _join_halves(bufs, name):
    n = len(bufs)
    halves = [b.shape[0] // 2 for b in bufs]

    def body(*refs):
        outs = refs[n:2 * n]
        send_sems, recv_sems = refs[2 * n:]
        x, y, c = _place()

        def copy(k, core):
            blk = outs[k].at[_half_rows(core, halves[k])]
            return pltpu.make_async_remote_copy(src_ref=blk, dst_ref=blk, send_sem=send_sems.at[k],
                                                recv_sem=recv_sems.at[k], device_id=(x, y, 1 - c),
                                                device_id_type=MESH)

        sends = [copy(k, c) for k in range(n)]
        for cp in sends:
            cp.start()
        for k in range(n):
            copy(k, 1 - c).wait_recv()
        for cp in sends:
            cp.wait_send()

    return pl.pallas_call(
        body, name=name, in_specs=[ANY] * n, out_specs=[ANY] * n,
        out_shape=[jax.ShapeDtypeStruct(b.shape, b.dtype) for b in bufs],
        input_output_aliases={k: k for k in range(n)},
        scratch_shapes=[pltpu.SemaphoreType.DMA((n,)), pltpu.SemaphoreType.DMA((n,))],
    )(*bufs)


def _rs_rows(H, width):
    for cand in (1024, 512, 256, 128, 64, 32, 16):
        if H % cand == 0 and cand * width * 4 <= 1536 * 1024:
            return cand
    return H


def _add_sibling(g, got, c, me, name):
    _, H, width = got.shape
    tb = _rs_rows(H, width)
    nb = H // tb

    def body(sp_ref, g_ref, got_ref, sb_ref, sf_ref):
        p = pl.program_id(1)
        s = g_ref[0] + got_ref[0]
        sb_ref[0] = s.astype(BF16)

        @pl.when(p == sp_ref[1])
        def _():
            sf_ref[...] = s

    grid_spec = pltpu.PrefetchScalarGridSpec(
        num_scalar_prefetch=1, grid=(nb, N_CHIPS),
        in_specs=[pl.BlockSpec((1, tb, width), lambda i, p, sp: (p, sp[0] * nb + i, 0)),
                  pl.BlockSpec((1, tb, width), lambda i, p, sp: (p, i, 0))],
        out_specs=[pl.BlockSpec((1, tb, width), lambda i, p, sp: (p, i, 0)),
                   pl.BlockSpec((tb, width), lambda i, p, sp: (i, 0))])
    return pl.pallas_call(
        body, name=name, grid_spec=grid_spec,
        out_shape=[jax.ShapeDtypeStruct((N_CHIPS, H, width), BF16), jax.ShapeDtypeStruct((H, width), F32)],
        compiler_params=_cp(("arbitrary", "arbitrary")),
    )(jnp.stack([c, me]).astype(jnp.int32), g, got)


def _add_chips(sf, got, others_and_c, name):
    H, width = sf.shape
    tb = _rs_rows(H, width)
    nb = H // tb

    def body(sp_ref, sf_ref, r1_ref, r2_ref, r3_ref, out_ref):
        out_ref[...] = ((sf_ref[...] + r1_ref[0].astype(F32)) + r2_ref[0].astype(F32)) + r3_ref[0].astype(F32)

    def slot(k):
        return pl.BlockSpec((1, tb, width), lambda i, sp: (sp[k], i, 0))

    blk = pl.BlockSpec((tb, width), lambda i, sp: (i, 0))
    grid_spec = pltpu.PrefetchScalarGridSpec(
        num_scalar_prefetch=1, grid=(nb,), in_specs=[blk, slot(0), slot(1), slot(2)],
        out_specs=pl.BlockSpec((tb, width), lambda i, sp: (sp[3] * nb + i, 0)))
    return pl.pallas_call(
        body, name=name, grid_spec=grid_spec,
        out_shape=jax.ShapeDtypeStruct((2 * H, width), F32),
        compiler_params=_cp(("arbitrary",)),
    )(others_and_c.astype(jnp.int32), sf, got, got, got)


REPLICATED = ("norm_mix", "norm_ffn", "gla_b_gate_f", "gla_b_gate_b", "gla_norm", "attn_q_norm", "attn_k_norm",
              "ffn_b_conv")


PIECE_ROWS = 16


def _piece_rows(shape):
    n = 1
    for s in shape:
        n *= s
    rows = n // LANES
    return rows, -(-rows // PIECE_ROWS) * PIECE_ROWS


def _pack(pieces, dtype, row_multiple):
    flat = []
    for p in pieces:
        rows, padded = _piece_rows(p.shape)
        flat.append(jnp.pad(p.astype(dtype).reshape(rows, LANES), ((0, padded - rows), (0, 0))))
    rows = sum(f.shape[0] for f in flat)
    padded = -(-rows // row_multiple) * row_multiple
    if padded > rows:
        flat.append(jnp.zeros((padded - rows, LANES), dtype))
    return jnp.concatenate(flat, axis=0)


def _unpack(buf, shapes):
    out, r = [], 0
    for shp in shapes:
        rows, padded = _piece_rows(shp)
        out.append(buf[r:r + rows].reshape(shp))
        r += padded
    return out


def _own_slot(shard2d, me):
    return lax.dynamic_update_index_in_dim(lax.empty((N_CHIPS,) + shard2d.shape, shard2d.dtype), shard2d, me, 0)


def _layer_small(w, l):
    j = l // 2
    if l % 2 == 0:
        return [w["gla_w_gate_up_f"][j], w["gla_w_gate_up_b"][j], w["ffn_w_conv"][l]]
    return [w["ffn_w_conv"][l]]


def _layer_weight_bufs(w, l, me):
    j = l // 2
    mixer = ("gla_w_in", "gla_w_out") if l % 2 == 0 else ("attn_w_qkv", "attn_w_out")
    bufs = [_own_slot(w[n][j].astype(BF16), me) for n in mixer]
    bufs.append(_own_slot(_pack(_layer_small(w, l), F32, 32), me))
    bufs += [_own_slot(w["ffn_w_up"][l].astype(BF16), me), _own_slot(w["ffn_w_down"][l].astype(BF16), me)]
    return bufs


N_MIXER_BUFS = 3


def _layer_weights(w, l, got):
    rows = lambda t: t.reshape(-1, t.shape[2])
    cols = lambda t: jnp.concatenate([t[p] for p in range(N_CHIPS)], axis=1)
    out = {}
    if len(got) != N_MIXER_BUFS:
        up, down = got[-2:]
        out.update(up=up, up_full=cols(up), down=rows(down))
    if len(got) != 2:
        mix_in, mix_out, small = got[:N_MIXER_BUFS]
        shapes = [t.shape for t in _layer_small(w, l)]
        parts = [_unpack(small[p], shapes) for p in range(N_CHIPS)]
        full_small = [jnp.concatenate([parts[p][k] for p in range(N_CHIPS)], axis=-1) for k in range(len(shapes))]
        out.update(conv=full_small[-1])
        if l % 2 == 0:
            out.update(gla_in=jnp.pad(cols(mix_in), ((0, 0), (0, GLA_IN_PAD - GLA_IN))), gla_out=rows(mix_out),
                       gate_f=full_small[0], gate_b=full_small[1])
        else:
            out.update(qkv=mix_in, attn_out=rows(mix_out))
    return out


HBM = pl.BlockSpec(memory_space=pltpu.HBM)
SEM = pl.BlockSpec(memory_space=pltpu.SEMAPHORE)
SIDE_EFFECT = pltpu.SideEffectType.DATAFLOW_SIDE_EFFECTING


def _gather_start(bufs, after, name):
    n = len(bufs)
    halves = [b.shape[1] // 2 for b in bufs]

    def body(*refs):
        refs = refs[:n] + refs[n + 1:]
        send_sems, recv_sems = refs[n:2 * n], refs[2 * n:3 * n]
        outs, token = refs[3 * n:4 * n], refs[4 * n]
        x, y, c = _place()
        me = 2 * x + y
        for k in range(n):
            blk = outs[k].at[me, _half_rows(c, halves[k])]
            for px, py in _other_chips(x, y):
                pltpu.make_async_remote_copy(src_ref=blk, dst_ref=blk, send_sem=send_sems[k], recv_sem=recv_sems[k],
                                             device_id=(px, py, c), device_id_type=MESH).start()
        token[...] = jnp.zeros_like(token)

    res = pl.pallas_call(
        body, name=name,
        in_specs=[HBM] * n + [ANY],
        out_specs=[SEM] * (2 * n) + [HBM] * n + [pl.BlockSpec(memory_space=pltpu.VMEM)],
        out_shape=[pltpu.SemaphoreType.DMA(())] * (2 * n) + [pltpu.HBM(b.shape, b.dtype) for b in bufs]
        + [jax.ShapeDtypeStruct((8, LANES), F32)],
        input_output_aliases={k: 2 * n + k for k in range(n)},
        compiler_params=pltpu.CompilerParams(has_side_effects=SIDE_EFFECT),
    )(*[pltpu.with_memory_space_constraint(b, pltpu.HBM) for b in bufs], after)
    return res[:n], res[n:2 * n], res[2 * n:3 * n], res[3 * n]


def _gather_wait(send_sems, recv_sems, thru, after, name):
    n = len(thru)
    halves = [b.shape[1] // 2 for b in thru]

    def body(*refs):
        ss, rs = refs[n:2 * n], refs[2 * n:3 * n]
        outs = refs[3 * n + 1:]
        x, y, c = _place()
        for k in range(n):
            three = outs[k].at[pl.ds(0, N_CHIPS - 1), _half_rows(c, halves[k])]
            cp = pltpu.make_async_remote_copy(src_ref=three, dst_ref=three, send_sem=ss[k], recv_sem=rs[k],
                                              device_id=(x, y, c), device_id_type=MESH)
            cp.wait_send()
            cp.wait_recv()

    return pl.pallas_call(
        body, name=name,
        in_specs=[HBM] * n + [SEM] * (2 * n) + [ANY],
        out_specs=[HBM] * n,
        out_shape=[pltpu.HBM(b.shape, b.dtype) for b in thru],
        input_output_aliases={k: k for k in range(n)},
        compiler_params=pltpu.CompilerParams(has_side_effects=SIDE_EFFECT),
    )(*thru, *send_sems, *recv_sems, after)


def _send_start(sbs, name):
    n = len(sbs)

    def body(*refs):
        send_sems, recv_sems = refs[2 * n:3 * n], refs[3 * n:4 * n]
        srcs, lands, token = refs[4 * n:5 * n], refs[5 * n:6 * n], refs[6 * n]
        x, y, c = _place()
        me = 2 * x + y
        for k in range(n):
            for px, py in _other_chips(x, y):
                pltpu.make_async_remote_copy(src_ref=srcs[k].at[2 * px + py], dst_ref=lands[k].at[me],
                                             send_sem=send_sems[k], recv_sem=recv_sems[k],
                                             device_id=(px, py, c), device_id_type=MESH).start()
        token[...] = jnp.zeros_like(token)

    hbm = lambda a: pltpu.with_memory_space_constraint(a, pltpu.HBM)
    res = pl.pallas_call(
        body, name=name,
        in_specs=[HBM] * (2 * n),
        out_specs=[SEM] * (2 * n) + [HBM] * (2 * n) + [pl.BlockSpec(memory_space=pltpu.VMEM)],
        out_shape=[pltpu.SemaphoreType.DMA(())] * (2 * n) + [pltpu.HBM(s.shape, s.dtype) for s in sbs] * 2
        + [jax.ShapeDtypeStruct((8, LANES), F32)],
        input_output_aliases={k: 2 * n + k for k in range(2 * n)},
        compiler_params=pltpu.CompilerParams(has_side_effects=SIDE_EFFECT),
    )(*[hbm(s) for s in sbs], *[hbm(lax.empty(s.shape, s.dtype)) for s in sbs])
    return res[:n], res[n:2 * n], res[2 * n:3 * n], res[3 * n:4 * n], res[4 * n]


def _send_wait(send_sems, recv_sems, srcs, lands, after, name):
    n = len(srcs)

    def body(*refs):
        ss, rs = refs[2 * n:3 * n], refs[3 * n:4 * n]
        s_out, l_out = refs[4 * n + 1:5 * n + 1], refs[5 * n + 1:]
        x, y, c = _place()
        for k in range(n):
            cp = pltpu.make_async_remote_copy(src_ref=s_out[k].at[pl.ds(0, N_CHIPS - 1)],
                                              dst_ref=l_out[k].at[pl.ds(0, N_CHIPS - 1)], send_sem=ss[k],
                                              recv_sem=rs[k], device_id=(x, y, c), device_id_type=MESH)
            cp.wait_send()
            cp.wait_recv()

    res = pl.pallas_call(
        body, name=name,
        in_specs=[HBM] * (2 * n) + [SEM] * (2 * n) + [ANY],
        out_specs=[HBM] * (2 * n),
        out_shape=[pltpu.HBM(s.shape, s.dtype) for s in srcs] * 2,
        input_output_aliases={k: k for k in range(2 * n)},
        compiler_params=pltpu.CompilerParams(has_side_effects=SIDE_EFFECT),
    )(*srcs, *lands, *send_sems, *recv_sems, after)
    return res[n:]


def _pass_to_sibling(bufs, name):
    n = len(bufs)
    halves = [b.shape[1] // 2 for b in bufs]

    def body(*refs):
        outs = refs[n:2 * n]
        send_sems, recv_sems = refs[2 * n:]
        x, y, c = _place()
        chips = _other_chips(x, y)

        def copy(k, j, core):
            px, py = chips[j]
            blk = outs[k].at[2 * px + py, _half_rows(core, halves[k])]
            return pltpu.make_async_remote_copy(src_ref=blk, dst_ref=blk, send_sem=send_sems.at[3 * k + j],
                                                recv_sem=recv_sems.at[3 * k + j], device_id=(x, y, 1 - c),
                                                device_id_type=MESH)

        sends = [copy(k, j, c) for k in range(n) for j in range(3)]
        for cp in sends:
            cp.start()
        for k in range(n):
            for j in range(3):
                copy(k, j, 1 - c).wait_recv()
        for cp in sends:
            cp.wait_send()

    return pl.pallas_call(
        body, name=name,
        in_specs=[ANY] * n, out_specs=[ANY] * n,
        out_shape=[jax.ShapeDtypeStruct(b.shape, b.dtype) for b in bufs],
        input_output_aliases={k: k for k in range(n)},
        scratch_shapes=[pltpu.SemaphoreType.DMA((3 * n,)), pltpu.SemaphoreType.DMA((3 * n,))],
    )(*bufs)


def _rope_tables(S):
    rows = S // GRID_W
    row_idx = jnp.repeat(jnp.arange(rows, dtype=F32), GRID_W)
    col_idx = jnp.tile(jnp.arange(GRID_W, dtype=F32), rows)
    pairs = ATT_HD // 4
    inv_freq = ROPE_THETA ** (-jnp.arange(pairs, dtype=F32) / pairs)
    ang = jnp.concatenate([row_idx[:, None] * inv_freq, col_idx[:, None] * inv_freq], axis=-1)
    cos, sin = jnp.cos(ang), jnp.sin(ang)
    return jnp.concatenate([cos, cos], axis=-1), jnp.concatenate([-sin, sin], axis=-1)


def _gate_rows(w, first_row):
    return jnp.zeros((LANES, GLA_KEY), F32).at[first_row:first_row + GLA_RANK].set(w.astype(F32))


def _local_step(x, target, weights_of, grads_out, P):
    S = x.shape[0]
    rc, rs = _rope_tables(S)
    row = lambda a: a.reshape(1, -1)
    saved = []
    for i in range(DEPTH):
        j = i // 2
        W = dict(weights_of(i, "mix", x))
        nm = row(P["norm_mix"][i])
        h1 = _rmsnorm_fwd(x, nm, f"norm_mix_fwd{i}")
        if i % 2 == 0:
            wgf = _gate_rows(W["gate_f"], 0)
            wgb = _gate_rows(W["gate_b"], GLA_RANK)
            bgf, bgb = row(P["gla_b_gate_f"][j]), row(P["gla_b_gate_b"][j])
            gn = row(P["gla_norm"][j])
            proj = _matmul_rows(h1, W["gla_in"], f"gla_in{i}")
            laf, lab = _gla_gate_fwd(proj, wgf, bgf, wgb, bgb, f"gla_gate_fwd{i}")
            of, stf = _gla_scan_fwd(proj, laf, False, f"gla_scan_f_fwd{i}")
            ob, stb = _gla_scan_fwd(proj, lab, True, f"gla_scan_b_fwd{i}")
            z = _gla_out_fwd(of, ob, proj, gn, f"gla_out_fwd{i}")
            xm = _matmul_rows(z, W["gla_out"], f"gla_outproj{i}", res=x)
            mix = dict(proj=proj, laf=laf, lab=lab, of=of, ob=ob, stf=stf, stb=stb, z=z, wgf=wgf, wgb=wgb)
        else:
            proj = _matmul_rows(h1, W["qkv"], f"attn_qkv{i}", w_layer=0)
            qn, kn = row(P["attn_q_norm"][j]), row(P["attn_k_norm"][j])
            qk, vb, kt, vt = _qk_prep_fwd(proj, qn, kn, rc, rs, f"qk_prep_fwd{i}")
            o, lse = _attn_fwd(qk, vt, f"attn_fwd{i}")
            xm = _matmul_rows(o, W["attn_out"], f"attn_outproj{i}", res=x)
            mix = dict(proj=proj, qk=qk, vb=vb, kt=kt, o=o, lse=lse)
        W.update(weights_of(i, "ffn", xm))
        h2 = _rmsnorm_fwd(xm, row(P["norm_ffn"][i]), f"norm_ffn_fwd{i}")
        a, uv, ug = _ffn_mid_fwd(h2, W["up_full"], W["conv"], row(P["ffn_b_conv"][i]), f"ffn_mid_fwd{i}")
        xo = _matmul_rows(a, W["down"], f"ffn_down{i}", res=xm)
        saved.append(dict(x=x, h1=h1, xm=xm, h2=h2, uv=uv, ug=ug, mix=mix, W=W))
        x = xo

    dx, dxb, loss = _loss_grad(x, target, "loss")

    G = {n: [None] * (DEPTH if n.startswith(("norm", "ffn")) else DEPTH // 2) for n in REPLICATED}
    token = None
    for i in reversed(range(DEPTH)):
        j = i // 2
        sv = saved[i]
        mix = sv["mix"]
        W = sv["W"]
        bconv = row(P["ffn_b_conv"][i])
        if token is not None:
            t = token[0:1, 0:1]
            bconv = jnp.where(t == 0.0, bconv, t)
        duv, dug, a, gwv, gwg = _ffn_mid_bwd(dxb, W["down"], sv["uv"], sv["ug"], W["conv"], bconv, f"ffn_mid_bwd{i}")
        L = dict(down=_wgrad(a, dxb, f"ffn_down_wgrad{i}", chips="rows"),
                 up=_wgrad(sv["h2"], (duv, dug), f"ffn_up_wgrad{i}", chips="cols"),
                 small=[jnp.concatenate([gwv[:3], gwg[:3]], axis=1)])
        G["ffn_b_conv"][i] = jnp.concatenate([gwv[3], gwg[3]], axis=0)
        dxm, dxmb, dn = _dgrad_norm((duv, dug), W["up"], sv["xm"], row(P["norm_ffn"][i]), dx, f"ffn_up_dgrad{i}",
                                    w_layer=0)
        G["norm_ffn"][i] = dn[0]
        if i % 2 == 0:
            proj = mix["proj"]
            bgf, bgb = row(P["gla_b_gate_f"][j]), row(P["gla_b_gate_b"][j])
            gn = row(P["gla_norm"][j])
            dz = _matmul_rows(dxmb, W["gla_out"], f"gla_outproj_dgrad{i}", transposed=True)
            L["out"] = _wgrad(mix["z"], dxmb, f"gla_outproj_wgrad{i}", chips="rows")
            do, dg, dgn = _gla_out_bwd(dz, mix["of"], mix["ob"], proj, gn, f"gla_out_bwd{i}")
            G["gla_norm"][j] = dgn[0]
            dqf, dkf, dvf, dlaf = _gla_scan_bwd(do, proj, mix["laf"], mix["stf"], False, f"gla_scan_f_bwd{i}")
            dqb, dkb, dvb, dlab = _gla_scan_bwd(do, proj, mix["lab"], mix["stb"], True, f"gla_scan_b_bwd{i}")
            dr, dwf, dbf, dwb, dbb = _gla_gate_bwd(dlaf, dlab, proj, mix["wgf"], bgf, mix["wgb"], bgb,
                                                   f"gla_gate_bwd{i}")
            L["small"] = [dwf[:GLA_RANK], dwb[GLA_RANK:2 * GLA_RANK]] + L["small"]
            G["gla_b_gate_f"][j] = dbf[0]
            G["gla_b_gate_b"][j] = dbb[0]
            dproj = jnp.concatenate([dqf + dqb, dkf + dkb, dvf + dvb, dg, dr], axis=1).astype(BF16)
            L["mix_in"] = _wgrad(sv["h1"], dproj, f"gla_in_wgrad{i}")
            dx, dxb, dn = _dgrad_norm(dproj, W["gla_in"], sv["x"], row(P["norm_mix"][i]), dxm, f"mix_in_dgrad{i}")
        else:
            proj = mix["proj"]
            qn, kn = row(P["attn_q_norm"][j]), row(P["attn_k_norm"][j])
            do = _matmul_rows(dxmb, W["attn_out"], f"attn_outproj_dgrad{i}", transposed=True)
            L["out"] = _wgrad(mix["o"], dxmb, f"attn_outproj_wgrad{i}", chips="rows")
            dq, dk, dv = _attn_bwd(do, mix["o"], mix["lse"], mix["qk"], mix["vb"], mix["kt"], f"attn_bwd{i}")
            dqk = jnp.concatenate([dq, dk], axis=1)
            dpqk, dqn, dkn = _qk_prep_bwd(dqk, proj, qn, kn, rc, rs, f"qk_prep_bwd{i}")
            G["attn_q_norm"][j] = dqn[0]
            G["attn_k_norm"][j] = dkn[0]
            dproj = jnp.concatenate([dpqk, dv], axis=1).astype(BF16)
            L["mix_in"] = _wgrad(sv["h1"], dproj, f"attn_qkv_wgrad{i}", chips="cols")
            dx, dxb, dn = _dgrad_norm(dproj, W["qkv"], sv["x"], row(P["norm_mix"][i]), dxm, f"mix_in_dgrad{i}",
                                      w_layer=0)
        G["norm_mix"][i] = dn[0]
        token = grads_out(i, L)
    return loss, dx, G


def kernel(x, norm_mix, norm_ffn, gla_w_in, gla_w_gate_up_f, gla_b_gate_f, gla_w_gate_up_b, gla_b_gate_b, gla_norm, gla_w_out, attn_w_qkv, attn_q_norm, attn_k_norm, attn_w_out, ffn_w_up, ffn_w_conv, ffn_b_conv, ffn_w_down, loss_target, m_norm_mix, m_norm_ffn, m_gla_w_in, m_gla_w_gate_up_f, m_gla_b_gate_f, m_gla_w_gate_up_b, m_gla_b_gate_b, m_gla_norm, m_gla_w_out, m_attn_w_qkv, m_attn_q_norm, m_attn_k_norm, m_attn_w_out, m_ffn_w_up, m_ffn_w_conv, m_ffn_b_conv, m_ffn_w_down, v_norm_mix, v_norm_ffn, v_gla_w_in, v_gla_w_gate_up_f, v_gla_b_gate_f, v_gla_w_gate_up_b, v_gla_b_gate_b, v_gla_norm, v_gla_w_out, v_attn_w_qkv, v_attn_q_norm, v_attn_k_norm, v_attn_w_out, v_ffn_w_up, v_ffn_w_conv, v_ffn_b_conv, v_ffn_w_down):
    names = ("norm_mix", "norm_ffn", "gla_w_in", "gla_w_gate_up_f", "gla_b_gate_f", "gla_w_gate_up_b",
             "gla_b_gate_b", "gla_norm", "gla_w_out", "attn_w_qkv", "attn_q_norm", "attn_k_norm", "attn_w_out",
             "ffn_w_up", "ffn_w_conv", "ffn_b_conv", "ffn_w_down")
    w = dict(zip(names, (norm_mix, norm_ffn, gla_w_in, gla_w_gate_up_f, gla_b_gate_f, gla_w_gate_up_b,
                         gla_b_gate_b, gla_norm, gla_w_out, attn_w_qkv, attn_q_norm, attn_k_norm, attn_w_out,
                         ffn_w_up, ffn_w_conv, ffn_b_conv, ffn_w_down)))
    m = dict(zip(names, (m_norm_mix, m_norm_ffn, m_gla_w_in, m_gla_w_gate_up_f, m_gla_b_gate_f,
                         m_gla_w_gate_up_b, m_gla_b_gate_b, m_gla_norm, m_gla_w_out, m_attn_w_qkv, m_attn_q_norm,
                         m_attn_k_norm, m_attn_w_out, m_ffn_w_up, m_ffn_w_conv, m_ffn_b_conv, m_ffn_w_down)))
    v = dict(zip(names, (v_norm_mix, v_norm_ffn, v_gla_w_in, v_gla_w_gate_up_f, v_gla_b_gate_f,
                         v_gla_w_gate_up_b, v_gla_b_gate_b, v_gla_norm, v_gla_w_out, v_attn_w_qkv, v_attn_q_norm,
                         v_attn_k_norm, v_attn_w_out, v_ffn_w_up, v_ffn_w_conv, v_ffn_b_conv, v_ffn_w_down)))
    px, py, pc = _place()
    me = 2 * px + py

    started, token = [], w["norm_mix"]
    for l in range(DEPTH):
        started.append(_gather_start(_layer_weight_bufs(w, l, me), token, f"gather_start{l}"))
        token = started[-1][3]
    fetched = {}

    def weights_of(l, part, after):
        send_sems, recv_sems, thru, _ = started[l]
        if l == 0:
            pick = slice(0, N_MIXER_BUFS) if part == "mix" else slice(N_MIXER_BUFS, None)
            landed = _gather_wait(send_sems[pick], recv_sems[pick], thru[pick], token if part == "mix" else after,
                                  f"gather_wait{l}_{part}")
            return _layer_weights(w, l, _pass_to_sibling(landed, f"gather_pass{l}_{part}"))
        if part == "mix":
            landed = _gather_wait(send_sems, recv_sems, thru, after, f"gather_wait{l}")
            fetched[l] = _layer_weights(w, l, _pass_to_sibling(landed, f"gather_pass{l}"))
        return fetched[l]

    sent = {}

    def grads_out(l, L):
        mix_in = L["mix_in"]
        if l % 2 == 0:
            width = w["gla_w_in"].shape[2]
            mix_in = jnp.stack([mix_in[:, p * width:(p + 1) * width] for p in range(N_CHIPS)])
        cut = lambda t, p: lax.slice_in_dim(t, p * (t.shape[-1] // N_CHIPS), (p + 1) * (t.shape[-1] // N_CHIPS),
                                            axis=t.ndim - 1)
        small = jnp.stack([_pack([cut(t, p) for t in L["small"]], F32, 32) for p in range(N_CHIPS)])
        bufs = [mix_in, L["out"], small, L["up"], L["down"]]
        gots = _swap_other_half(bufs, f"grads{l}_to_sibling")
        sums = [_add_sibling(b, g, pc, me, f"grads{l}_add_sibling{k}") for k, (b, g) in enumerate(zip(bufs, gots))]
        send_sems, recv_sems, srcs, lands, tok = _send_start([s[0] for s in sums], f"grads{l}_start")
        sent[l] = (send_sems, recv_sems, srcs, lands, [s[1] for s in sums])
        return tok

    P = {n: w[n] for n in REPLICATED}

    loss_part, dx, grads = _local_step(x[0], loss_target[0], weights_of, grads_out, P)

    others_and_c = jnp.stack([jnp.where(me <= k, k + 1, k) for k in range(N_CHIPS - 1)] + [pc])
    mine, after = {}, dx
    for l in reversed(range(DEPTH)):
        send_sems, recv_sems, srcs, lands, own = sent[l]
        landed = _send_wait(send_sems, recv_sems, srcs, lands, after, f"grads{l}_wait")
        halves = [_add_chips(own[k], landed[k], others_and_c, f"grads{l}_add_chips{k}") for k in range(len(own))]
        mine[l] = _join_halves(halves, f"grads{l}_join_halves")
        after = mine[l][0]
    gsh = {}
    for n, k, layers in (("ffn_w_up", 3, range(DEPTH)), ("ffn_w_down", 4, range(DEPTH)),
                         ("gla_w_in", 0, range(0, DEPTH, 2)), ("gla_w_out", 1, range(0, DEPTH, 2)),
                         ("attn_w_qkv", 0, range(1, DEPTH, 2)), ("attn_w_out", 1, range(1, DEPTH, 2))):
        gsh[n] = jnp.stack([mine[l][k] for l in layers])
    small_mine = [_unpack(mine[l][2], [t.shape for t in _layer_small(w, l)]) for l in range(DEPTH)]
    gsh["ffn_w_conv"] = jnp.stack([small_mine[l][-1] for l in range(DEPTH)])
    gsh["gla_w_gate_up_f"] = jnp.stack([small_mine[l][0] for l in range(0, DEPTH, 2)])
    gsh["gla_w_gate_up_b"] = jnp.stack([small_mine[l][1] for l in range(0, DEPTH, 2)])

    small = _pack([jnp.stack(grads[n]) for n in REPLICATED] + [loss_part], F32, 16)
    small_sum = _allreduce_small(small, "small_allreduce")
    parts = _unpack(small_sum, [w[n].shape for n in REPLICATED] + [(1, LANES)])
    gsh.update(dict(zip(REPLICATED, parts[:-1])))
    loss = parts[-1][0, 0]

    delta, new_m, new_v = {}, {}, {}
    for n in names:
        shp = w[n].shape
        two_d = (-1, shp[-1])
        d, nm, nv = _adamw(w[n].reshape(two_d), gsh[n].reshape(two_d), m[n].reshape(two_d), v[n].reshape(two_d),
                           f"adamw_{n}")
        delta[n], new_m[n], new_v[n] = d.reshape(shp), nm.reshape(shp), nv.reshape(shp)

    return (loss, dx[None], *[gsh[n] for n in names], *[delta[n] for n in names],
            *[new_m[n] for n in names], *[new_v[n] for n in names])
```

```python
import jax
import jax.numpy as jnp
from jax import lax
from jax.experimental import pallas as pl
from jax.experimental.pallas import tpu as pltpu

F32 = jnp.float32
BF16 = jnp.bfloat16
MESH = pl.DeviceIdType.MESH
HIGHEST = lax.Precision.HIGHEST

D_MODEL = 1024
DEPTH = 4
GRID_W = 64
NORM_EPS = 1e-6
GLA_HEADS = 4
GLA_DK = 128
GLA_DV = 256
GLA_KEY = GLA_HEADS * GLA_DK
GLA_VAL = GLA_HEADS * GLA_DV
GLA_RANK = 16
GLA_CHUNK = 64
GLA_GATE_NORMALIZER = 16.0
GLA_IN = 2 * GLA_KEY + 2 * GLA_VAL + 2 * GLA_RANK
GLA_IN_PAD = 3200
GLA_R_BLOCK = (2 * GLA_KEY + 2 * GLA_VAL) // 128
ATT_HD = 128
ATT_QH = 8
ATT_KVH = 2
ATT_GROUP = ATT_QH // ATT_KVH
ATT_QKV = (ATT_QH + 2 * ATT_KVH) * ATT_HD
ROPE_THETA = 10000.0
D_FF = 2816
ADAM_LR = 0.001
ADAM_B1 = 0.9
ADAM_B2 = 0.999
ADAM_EPS = 1e-08
ADAM_WD = 0.01
ADAM_STEP = 10

N_CHIPS = 4
LANES = 128
VMEM_LIMIT = 56 * 1024 * 1024


def _cp(sem):
    return pltpu.CompilerParams(dimension_semantics=sem, vmem_limit_bytes=VMEM_LIMIT)


def _pick(n, cands):
    for c in cands:
        if n % c == 0:
            return c
    return n


def _dg(a, b, ca, cb):
    return lax.dot_general(a, b, (((ca,), (cb,)), ((), ())), preferred_element_type=F32)


def _sigmoid(x):
    return 0.5 * jnp.tanh(0.5 * x) + 0.5


def _rmsnorm_fwd(x, w, name):
    S, D = x.shape
    tm = _pick(S, (512, 256))

    def body(x_ref, w_ref, h_ref):
        xv = x_ref[...]
        r = lax.rsqrt(jnp.mean(xv * xv, axis=-1, keepdims=True) + NORM_EPS)
        h_ref[...] = (xv * r * w_ref[...]).astype(BF16)

    return pl.pallas_call(
        body, name=name, grid=(S // tm,),
        in_specs=[pl.BlockSpec((tm, D), lambda i: (i, 0)), pl.BlockSpec((1, D), lambda i: (0, 0))],
        out_specs=pl.BlockSpec((tm, D), lambda i: (i, 0)),
        out_shape=jax.ShapeDtypeStruct((S, D), BF16),
        compiler_params=_cp(("parallel",)),
    )(x, w)


def _loss_grad(y, t, name):
    S, D = y.shape
    tm = _pick(S, (512, 256))

    def body(y_ref, t_ref, dy_ref, dyb_ref, loss_ref):
        i = pl.program_id(0)
        d = y_ref[...] - t_ref[...]
        dy = d * (1.0 / D)
        dy_ref[...] = dy
        dyb_ref[...] = dy.astype(BF16)
        sq = jnp.sum(jnp.sum(d * d, axis=1, keepdims=True), axis=0, keepdims=True)
        part = jnp.broadcast_to(sq * (0.5 / D), (1, LANES))

        @pl.when(i == 0)
        def _():
            loss_ref[...] = part

        @pl.when(i > 0)
        def _():
            loss_ref[...] += part

    return pl.pallas_call(
        body, name=name, grid=(S // tm,),
        in_specs=[pl.BlockSpec((tm, D), lambda i: (i, 0)), pl.BlockSpec((tm, D), lambda i: (i, 0))],
        out_specs=[pl.BlockSpec((tm, D), lambda i: (i, 0)), pl.BlockSpec((tm, D), lambda i: (i, 0)),
                   pl.BlockSpec((1, LANES), lambda i: (0, 0))],
        out_shape=[jax.ShapeDtypeStruct((S, D), F32), jax.ShapeDtypeStruct((S, D), BF16),
                   jax.ShapeDtypeStruct((1, LANES), F32)],
        compiler_params=_cp(("arbitrary",)),
    )(y, t)


WGRAD_TK = 512


def _wgrad(a, b, name, chips=None):
    S, Kw = a.shape
    pair = isinstance(b, (tuple, list))
    tn = b[0].shape[1] if pair else b.shape[1]
    N = 2 * tn if pair else tn
    tk = _pick(S, (WGRAD_TK, 256, 128))
    nk = S // tk
    if pair:
        b_specs = [pl.BlockSpec((tk, tn), lambda j, k: (jnp.where(j == 0, k, nk - 1), 0)),
                   pl.BlockSpec((tk, tn), lambda j, k: (jnp.where(j == 1, k, 0), 0))]
    else:
        b_specs = [pl.BlockSpec((tk, tn), lambda j, k: (k, 0))]
    if chips == "cols":
        cw = N // N_CHIPS
        span = tn // cw
        o_spec = pl.BlockSpec((span, Kw, cw), lambda j, k: (j, 0, 0))
        out_shape = jax.ShapeDtypeStruct((N_CHIPS, Kw, cw), F32)
    elif chips == "rows":
        assert not pair
        o_spec = pl.BlockSpec((N_CHIPS, Kw // N_CHIPS, N), lambda j, k: (0, 0, 0))
        out_shape = jax.ShapeDtypeStruct((N_CHIPS, Kw // N_CHIPS, N), F32)
    else:
        assert not pair
        o_spec = pl.BlockSpec((Kw, N), lambda j, k: (0, 0))
        out_shape = jax.ShapeDtypeStruct((Kw, N), F32)
    nb = len(b_specs)

    def body(*refs):
        a_ref, b_refs, o_ref, acc = refs[0], refs[1:1 + nb], refs[-2], refs[-1]
        j = pl.program_id(0)
        k = pl.program_id(1)

        @pl.when(k == 0)
        def _():
            acc[...] = jnp.zeros_like(acc)

        av = a_ref[...].astype(BF16)
        for h in range(nb):
            @pl.when(j == h)
            def _():
                acc[...] += _dg(av, b_refs[h][...].astype(BF16), 0, 0)

        @pl.when(k == nk - 1)
        def _():
            v = acc[...]
            if chips == "cols":
                for s in range(span):
                    o_ref[s] = v[:, s * cw:(s + 1) * cw]
            elif chips == "rows":
                rows = Kw // N_CHIPS
                for p in range(N_CHIPS):
                    o_ref[p] = v[p * rows:(p + 1) * rows, :]
            else:
                o_ref[...] = v

    return pl.pallas_call(
        body, name=name, grid=(nb, nk),
        in_specs=[pl.BlockSpec((tk, Kw), lambda j, k: (k, 0))] + b_specs,
        out_specs=o_spec, out_shape=out_shape,
        scratch_shapes=[pltpu.VMEM((Kw, tn), F32)],
        compiler_params=_cp(("parallel", "arbitrary")),
    )(a, *(tuple(b) if pair else (b,)))


def _matmul_rows(a, w, name, res=None, w_layer=None, transposed=False):
    M, K = a.shape
    if w_layer is not None:
        cw = w.shape[2]
        N = N_CHIPS * cw
        w_spec = pl.BlockSpec((N_CHIPS, K, cw), lambda i: (0, w_layer, 0))
    else:
        N = w.shape[0] if transposed else w.shape[1]
        assert w.shape[1 if transposed else 0] == K
        w_spec = pl.BlockSpec(w.shape, lambda i: (0, 0))
    tm = _pick(M, (512, 256, 128))
    has_res = res is not None

    def body(*refs):
        a_ref, w_ref = refs[0], refs[1]
        r_ref = refs[2] if has_res else None
        o_ref = refs[-1]
        av = a_ref[...].astype(BF16)
        if w_layer is not None:
            for p in range(N_CHIPS):
                o_ref[:, pl.ds(p * cw, cw)] = jnp.dot(av, w_ref[p], preferred_element_type=F32)
        else:
            v = _dg(av, w_ref[...], 1, 1 if transposed else 0)
            o_ref[...] = v + r_ref[...] if has_res else v

    row = pl.BlockSpec((tm, N), lambda i: (i, 0))
    return pl.pallas_call(
        body, name=name, grid=(M // tm,),
        in_specs=[pl.BlockSpec((tm, K), lambda i: (i, 0)), w_spec] + ([row] if has_res else []),
        out_specs=row, out_shape=jax.ShapeDtypeStruct((M, N), F32),
        compiler_params=_cp(("parallel",)),
    )(*((a, w) + ((res,) if has_res else ())))


def _dgrad_norm(dy, w, x, wn, dres, name, w_layer=None):
    pair = isinstance(dy, (tuple, list))
    M = dy[0].shape[0] if pair else dy.shape[0]
    Kp = 2 * dy[0].shape[1] if pair else dy.shape[1]
    D = x.shape[1]
    if w_layer is not None:
        cw = w.shape[2]
        assert N_CHIPS * cw == Kp and w.shape[1] % D == 0
        w_spec = pl.BlockSpec((N_CHIPS, D, cw), lambda i: (0, w_layer, 0))
    else:
        assert w.shape == (D, Kp)
        w_spec = pl.BlockSpec((D, Kp), lambda i: (0, 0))
    tm = _pick(M, (256, 128))
    width = Kp // 2 if pair else Kp
    dy_specs = [pl.BlockSpec((tm, width), lambda i: (i, 0))] * (2 if pair else 1)
    nd = len(dy_specs)

    def body(*refs):
        dy_refs = refs[:nd]
        w_ref, x_ref, wn_ref, dres_ref, dx_ref, dxb_ref, dwn_ref = refs[nd:]
        i = pl.program_id(0)
        if w_layer is not None:
            dh = None
            for p in range(N_CHIPS):
                src, off = divmod(p * cw, width)
                part = _dg(dy_refs[src][:, pl.ds(off, cw)], w_ref[p], 1, 1)
                dh = part if dh is None else dh + part
        else:
            dh = _dg(dy_refs[0][...], w_ref[...], 1, 1)
        xv = x_ref[...]
        r = lax.rsqrt(jnp.mean(xv * xv, axis=-1, keepdims=True) + NORM_EPS)
        yv = xv * r
        dyv = dh * wn_ref[...]
        dxv = r * (dyv - yv * jnp.mean(dyv * yv, axis=-1, keepdims=True)) + dres_ref[...]
        dx_ref[...] = dxv
        dxb_ref[...] = dxv.astype(BF16)
        part = jnp.sum(dh * yv, axis=0, keepdims=True)

        @pl.when(i == 0)
        def _():
            dwn_ref[...] = part

        @pl.when(i > 0)
        def _():
            dwn_ref[...] += part

    row = pl.BlockSpec((tm, D), lambda i: (i, 0))
    one = pl.BlockSpec((1, D), lambda i: (0, 0))
    return pl.pallas_call(
        body, name=name, grid=(M // tm,),
        in_specs=dy_specs + [w_spec, row, one, row],
        out_specs=[row, row, one],
        out_shape=[jax.ShapeDtypeStruct((M, D), F32), jax.ShapeDtypeStruct((M, D), BF16),
                   jax.ShapeDtypeStruct((1, D), F32)],
        compiler_params=_cp(("arbitrary",)),
    )(*(tuple(dy) if pair else (dy,)), w, x, wn, dres)


FFN_TN_FWD = 256
FFN_TN_BWD = 128
FFN_ROWS = 256
PAD = 8


def _conv3(pad_ref, w, r0, tr):
    um = pad_ref[pl.ds(PAD - 1 + r0, tr), :]
    uc = pad_ref[pl.ds(PAD + r0, tr), :]
    up = pad_ref[pl.ds(PAD + 1 + r0, tr), :]
    return w[0:1, :] * um + w[1:2, :] * uc + w[2:3, :] * up, (um, uc, up)


def _zero_pads(pad_ref, S, tn):
    pad_ref[pl.ds(0, PAD), :] = jnp.zeros((PAD, tn), F32)
    pad_ref[pl.ds(PAD + S, PAD), :] = jnp.zeros((PAD, tn), F32)


def _ffn_mid_fwd(h, wup, wconv, bconv, name):
    S, D = h.shape
    F = wup.shape[1] // 2
    tn = FFN_TN_FWD
    nb = F // tn
    tr = min(FFN_ROWS, S)

    def body(h_ref, wv_ref, wg_ref, cv_ref, cg_ref, bv_ref, bg_ref, a_ref, uv_ref, ug_ref):
        _zero_pads(uv_ref, S, tn)
        _zero_pads(ug_ref, S, tn)
        hv = h_ref[...]
        uv_ref[pl.ds(PAD, S), :] = jnp.dot(hv, wv_ref[...], preferred_element_type=F32)
        ug_ref[pl.ds(PAD, S), :] = jnp.dot(hv, wg_ref[...], preferred_element_type=F32)
        cwv, cwg, bv, bg = cv_ref[...], cg_ref[...], bv_ref[...], bg_ref[...]
        for r0 in range(0, S, tr):
            cv = _conv3(uv_ref, cwv, r0, tr)[0] + bv
            cg = _conv3(ug_ref, cwg, r0, tr)[0] + bg
            a_ref[pl.ds(r0, tr), :] = (cg * _sigmoid(cg) * cv).astype(BF16)

    col = lambda off: (lambda j: (0, j + off))
    padded = pl.BlockSpec((S + 2 * PAD, tn), col(0))
    return pl.pallas_call(
        body, name=name, grid=(nb,),
        in_specs=[pl.BlockSpec((S, D), lambda j: (0, 0)),
                  pl.BlockSpec((D, tn), col(0)), pl.BlockSpec((D, tn), col(nb)),
                  pl.BlockSpec((3, tn), col(0)), pl.BlockSpec((3, tn), col(nb)),
                  pl.BlockSpec((1, tn), col(0)), pl.BlockSpec((1, tn), col(nb))],
        out_specs=[pl.BlockSpec((S, tn), col(0)), padded, padded],
        out_shape=[jax.ShapeDtypeStruct((S, F), BF16), jax.ShapeDtypeStruct((S + 2 * PAD, F), F32),
                   jax.ShapeDtypeStruct((S + 2 * PAD, F), F32)],
        compiler_params=_cp(("parallel",)),
    )(h, wup, wup, wconv, wconv, bconv, bconv)


def _rows8(rows):
    n = rows[0].shape[1]
    idx = lax.broadcasted_iota(jnp.int32, (8, n), 0)
    out = jnp.zeros((8, n), F32)
    for k, r in enumerate(rows):
        out = jnp.where(idx == k, r, out)
    return out


def _ffn_mid_bwd(dyb, wdown, uv, ug, wconv, bconv, name):
    S, D = dyb.shape
    F = wdown.shape[0]
    tn = FFN_TN_BWD
    nb = F // tn
    tr = min(FFN_ROWS, S)

    def body(dy_ref, wd_ref, uv_ref, ug_ref, cv_ref, cg_ref, bv_ref, bg_ref,
             duv_ref, dug_ref, a_ref, gwv_ref, gwg_ref, pdv, pdg):
        for p in (pdv, pdg):
            _zero_pads(p, S, tn)
        wd = wd_ref[...]
        cwv, cwg, bv, bg = cv_ref[...], cg_ref[...], bv_ref[...], bg_ref[...]
        zero = jnp.zeros((1, tn), F32)
        gv = [zero, zero, zero, zero]
        gg = [zero, zero, zero, zero]
        for r0 in range(0, S, tr):
            cv, shv = _conv3(uv_ref, cwv, r0, tr)
            cg, shg = _conv3(ug_ref, cwg, r0, tr)
            cv = cv + bv
            cg = cg + bg
            sg = _sigmoid(cg)
            sl = cg * sg
            a_ref[pl.ds(r0, tr), :] = (sl * cv).astype(BF16)
            da = _dg(dy_ref[pl.ds(r0, tr), :], wd, 1, 1)
            dcv = da * sl
            dcg = da * cv * (sg * (1.0 + cg * (1.0 - sg)))
            pdv[pl.ds(PAD + r0, tr), :] = dcv
            pdg[pl.ds(PAD + r0, tr), :] = dcg
            for k in range(3):
                gv[k] = gv[k] + jnp.sum(dcv * shv[k], axis=0, keepdims=True)
                gg[k] = gg[k] + jnp.sum(dcg * shg[k], axis=0, keepdims=True)
            gv[3] = gv[3] + jnp.sum(dcv, axis=0, keepdims=True)
            gg[3] = gg[3] + jnp.sum(dcg, axis=0, keepdims=True)
        gwv_ref[...] = _rows8(gv)
        gwg_ref[...] = _rows8(gg)
        for r0 in range(0, S, tr):
            for pd, cw, out in ((pdv, cwv, duv_ref), (pdg, cwg, dug_ref)):
                dm = pd[pl.ds(PAD - 1 + r0, tr), :]
                dc = pd[pl.ds(PAD + r0, tr), :]
                dp = pd[pl.ds(PAD + 1 + r0, tr), :]
                out[pl.ds(r0, tr), :] = (cw[0:1, :] * dp + cw[1:2, :] * dc + cw[2:3, :] * dm).astype(BF16)

    col = lambda off: (lambda j: (0, j + off))
    blk = pl.BlockSpec((S, tn), col(0))
    padded = pl.BlockSpec((S + 2 * PAD, tn), col(0))
    g8 = pl.BlockSpec((8, tn), col(0))
    return pl.pallas_call(
        body, name=name, grid=(nb,),
        in_specs=[pl.BlockSpec((S, D), lambda j: (0, 0)), pl.BlockSpec((tn, D), lambda j: (j, 0)), padded, padded,
                  pl.BlockSpec((3, tn), col(0)), pl.BlockSpec((3, tn), col(nb)),
                  pl.BlockSpec((1, tn), col(0)), pl.BlockSpec((1, tn), col(nb))],
        out_specs=[blk, blk, blk, g8, g8],
        out_shape=[jax.ShapeDtypeStruct((S, F), BF16), jax.ShapeDtypeStruct((S, F), BF16),
                   jax.ShapeDtypeStruct((S, F), BF16), jax.ShapeDtypeStruct((8, F), F32),
                   jax.ShapeDtypeStruct((8, F), F32)],
        scratch_shapes=[pltpu.VMEM((S + 2 * PAD, tn), F32)] * 2,
        compiler_params=_cp(("parallel",)),
    )(dyb, wdown, uv, ug, wconv, wconv, bconv, bconv)


def _log_sigmoid(x):
    return jnp.minimum(x, 0.0) - jnp.log(1.0 + jnp.exp(-jnp.abs(x)))


def _gla_gate_fwd(proj, wgf, bgf, wgb, bgb, name):
    S = proj.shape[0]
    tm = _pick(S, (512, 256))

    def body(r_ref, wf_ref, bf_ref, wb_ref, bb_ref, laf_ref, lab_ref):
        r = r_ref[...].astype(BF16)
        lf = jnp.dot(r, wf_ref[...].astype(BF16), preferred_element_type=F32) + bf_ref[...]
        lb = jnp.dot(r, wb_ref[...].astype(BF16), preferred_element_type=F32) + bb_ref[...]
        laf_ref[...] = _log_sigmoid(lf) * (1.0 / GLA_GATE_NORMALIZER)
        lab_ref[...] = _log_sigmoid(lb) * (1.0 / GLA_GATE_NORMALIZER)

    full = lambda shp: pl.BlockSpec(shp, lambda i: (0, 0))
    row = pl.BlockSpec((tm, GLA_KEY), lambda i: (i, 0))
    return pl.pallas_call(
        body, name=name, grid=(S // tm,),
        in_specs=[pl.BlockSpec((tm, LANES), lambda i: (i, GLA_R_BLOCK)),
                  full((LANES, GLA_KEY)), full((1, GLA_KEY)), full((LANES, GLA_KEY)), full((1, GLA_KEY))],
        out_specs=[row, row],
        out_shape=[jax.ShapeDtypeStruct((S, GLA_KEY), F32)] * 2,
        compiler_params=_cp(("parallel",)),
    )(proj, wgf, bgf, wgb, bgb)


def _gla_gate_bwd(dlaf, dlab, proj, wgf, bgf, wgb, bgb, name):
    S = proj.shape[0]
    tm = _pick(S, (512, 256))

    def body(dlf_ref, dlb_ref, r_ref, wf_ref, bf_ref, wb_ref, bb_ref, dr_ref, dwf_ref, dbf_ref, dwb_ref, dbb_ref):
        i = pl.program_id(0)
        r = r_ref[...].astype(BF16)
        wf = wf_ref[...].astype(BF16)
        wb = wb_ref[...].astype(BF16)
        lf = jnp.dot(r, wf, preferred_element_type=F32) + bf_ref[...]
        lb = jnp.dot(r, wb, preferred_element_type=F32) + bb_ref[...]
        glf = dlf_ref[...] * (1.0 / GLA_GATE_NORMALIZER) * (1.0 / (1.0 + jnp.exp(lf)))
        glb = dlb_ref[...] * (1.0 / GLA_GATE_NORMALIZER) * (1.0 / (1.0 + jnp.exp(lb)))
        gfb = glf.astype(BF16)
        gbb = glb.astype(BF16)
        dr_ref[...] = _dg(gfb, wf, 1, 1) + _dg(gbb, wb, 1, 1)
        parts = (_dg(r, gfb, 0, 0), jnp.sum(glf, axis=0, keepdims=True),
                 _dg(r, gbb, 0, 0), jnp.sum(glb, axis=0, keepdims=True))
        outs = (dwf_ref, dbf_ref, dwb_ref, dbb_ref)

        @pl.when(i == 0)
        def _():
            for o, p in zip(outs, parts):
                o[...] = p

        @pl.when(i > 0)
        def _():
            for o, p in zip(outs, parts):
                o[...] += p

    full = lambda shp: pl.BlockSpec(shp, lambda i: (0, 0))
    row = pl.BlockSpec((tm, GLA_KEY), lambda i: (i, 0))
    return pl.pallas_call(
        body, name=name, grid=(S // tm,),
        in_specs=[row, row, pl.BlockSpec((tm, LANES), lambda i: (i, GLA_R_BLOCK)),
                  full((LANES, GLA_KEY)), full((1, GLA_KEY)), full((LANES, GLA_KEY)), full((1, GLA_KEY))],
        out_specs=[pl.BlockSpec((tm, LANES), lambda i: (i, 0)),
                   full((LANES, GLA_KEY)), full((1, GLA_KEY)), full((LANES, GLA_KEY)), full((1, GLA_KEY))],
        out_shape=[jax.ShapeDtypeStruct((S, LANES), F32),
                   jax.ShapeDtypeStruct((LANES, GLA_KEY), F32), jax.ShapeDtypeStruct((1, GLA_KEY), F32),
                   jax.ShapeDtypeStruct((LANES, GLA_KEY), F32), jax.ShapeDtypeStruct((1, GLA_KEY), F32)],
        compiler_params=_cp(("arbitrary",)),
    )(dlaf, dlab, proj, wgf, bgf, wgb, bgb)


def _gla_masks(rev):
    C = GLA_CHUNK
    t = lax.broadcasted_iota(jnp.int32, (C, C), 0)
    s = lax.broadcasted_iota(jnp.int32, (C, C), 1)
    if rev:
        return (s >= t), (s > t), (t >= s), (t > s)
    return (s <= t), (s <= t), (t <= s), (t <= s)


def _cum_dot(cum, x):
    return jnp.dot(cum.astype(F32), x, precision=HIGHEST, preferred_element_type=F32)


def _gla_chunk_common(q, k, la, cum, end_row):
    b = _cum_dot(cum, la)
    bend = b[end_row:end_row + 1, :]
    e = jnp.exp(b)
    qd = q * (GLA_DK ** -0.5) * e
    ei = jnp.exp(-b)
    ee = jnp.exp(bend - b)
    d = jnp.exp(bend)
    return e, ei, ee, d, qd, k * ei, k * ee


GLA_CB = 16


def _gla_specs(S, rev_order):
    n = S // GLA_CHUNK
    cb = min(GLA_CB, n)
    nblk = n // cb
    rows = cb * GLA_CHUNK
    ci = (lambda i: nblk - 1 - i) if rev_order else (lambda i: i)
    q_spec = pl.BlockSpec((rows, GLA_DK), lambda h, i: (ci(i), h))
    k_spec = pl.BlockSpec((rows, GLA_DK), lambda h, i: (ci(i), GLA_HEADS + h))
    v_spec = pl.BlockSpec((rows, GLA_DV), lambda h, i: (ci(i), GLA_KEY * 2 // GLA_DV + h))
    la_spec = pl.BlockSpec((rows, GLA_DK), lambda h, i: (ci(i), h))
    o_spec = pl.BlockSpec((rows, GLA_DV), lambda h, i: (ci(i), h))
    st_spec = pl.BlockSpec((1, cb, GLA_DV, GLA_DK), lambda h, i: (h, ci(i), 0, 0))
    return n, cb, nblk, q_spec, k_spec, v_spec, la_spec, o_spec, st_spec


def _gla_scan_fwd(proj, la, rev, name):
    S = proj.shape[0]
    C = GLA_CHUNK
    n, cb, nblk, q_spec, k_spec, v_spec, la_spec, o_spec, st_spec = _gla_specs(S, rev)
    end_row = 0 if rev else C - 1
    order = list(range(cb))[::-1] if rev else list(range(cb))

    def body(q_ref, k_ref, v_ref, la_ref, o_ref, st_ref, state):
        i = pl.program_id(1)

        @pl.when(i == 0)
        def _():
            state[...] = jnp.zeros_like(state)

        cum, mask, _, _ = _gla_masks(rev)
        pre, intra, kv = {}, {}, {}
        for cc in order:
            rows = pl.ds(cc * C, C)
            q, k, v, lav = q_ref[rows, :], k_ref[rows, :], v_ref[rows, :], la_ref[rows, :]
            _, _, _, d, qd, ki, ke = _gla_chunk_common(q, k, lav, cum, end_row)
            qdb, kib, keb, vb = qd.astype(BF16), ki.astype(BF16), ke.astype(BF16), v.astype(BF16)
            pre[cc] = (d, qdb)
            att = jnp.where(mask, _dg(qdb, kib, 1, 1), 0.0)
            intra[cc] = jnp.dot(att.astype(BF16), vb, preferred_element_type=F32)
            kv[cc] = _dg(vb, keb, 0, 0)
        st = state[...]
        for cc in order:
            d, qdb = pre[cc]
            o_ref[pl.ds(cc * C, C), :] = intra[cc] + _dg(qdb, st.astype(BF16), 1, 1)
            st_ref[0, cc] = st
            st = st * d + kv[cc]
        state[...] = st

    return pl.pallas_call(
        body, name=name, grid=(GLA_HEADS, nblk),
        in_specs=[q_spec, k_spec, v_spec, la_spec],
        out_specs=[o_spec, st_spec],
        out_shape=[jax.ShapeDtypeStruct((S, GLA_VAL), F32),
                   jax.ShapeDtypeStruct((GLA_HEADS, n, GLA_DV, GLA_DK), F32)],
        scratch_shapes=[pltpu.VMEM((GLA_DV, GLA_DK), F32)],
        compiler_params=_cp(("parallel", "arbitrary")),
    )(proj, proj, proj, la)


def _gla_scan_bwd(do, proj, la, states, rev, name):
    S = proj.shape[0]
    C = GLA_CHUNK
    n, cb, nblk, q_spec, k_spec, v_spec, la_spec, o_spec, st_spec = _gla_specs(S, not rev)
    end_row = 0 if rev else C - 1
    order = list(range(cb)) if rev else list(range(cb))[::-1]

    def body(do_ref, q_ref, k_ref, v_ref, la_ref, st_ref, dq_ref, dk_ref, dv_ref, dla_ref, gstate):
        i = pl.program_id(1)

        @pl.when(i == 0)
        def _():
            gstate[...] = jnp.zeros_like(gstate)

        cum, mask, cum_t, mask_t = _gla_masks(rev)
        g = gstate[...]
        for cc in order:
            rows = pl.ds(cc * C, C)
            q, k, v, lav = q_ref[rows, :], k_ref[rows, :], v_ref[rows, :], la_ref[rows, :]
            dov = do_ref[rows, :]
            st = st_ref[0, cc]
            e, ei, ee, d, qd, ki, ke = _gla_chunk_common(q, k, lav, cum, end_row)
            qdb, kib, keb, vb = qd.astype(BF16), ki.astype(BF16), ke.astype(BF16), v.astype(BF16)
            dob, gb, stb = dov.astype(BF16), g.astype(BF16), st.astype(BF16)
            att_t = jnp.where(mask_t, _dg(kib, qdb, 1, 1), 0.0)
            da = jnp.where(mask, _dg(dob, vb, 1, 1), 0.0)
            da_t = jnp.where(mask_t, _dg(vb, dob, 1, 1), 0.0)
            dv_ref[rows, :] = jnp.dot(att_t.astype(BF16), dob, preferred_element_type=F32) + _dg(keb, gb, 1, 1)
            dqd = (jnp.dot(da.astype(BF16), kib, preferred_element_type=F32)
                   + jnp.dot(dob, stb, preferred_element_type=F32))
            dki = jnp.dot(da_t.astype(BF16), qdb, preferred_element_type=F32)
            dke = jnp.dot(vb, gb, preferred_element_type=F32)
            dd = jnp.sum(st * g, axis=0, keepdims=True)
            g = g * d + _dg(dob, qdb, 0, 0)
            dq_ref[rows, :] = dqd * e * (GLA_DK ** -0.5)
            dk_ref[rows, :] = dki * ei + dke * ee
            dkeke = dke * ke
            db = dqd * qd - dki * ki - dkeke
            dbend = jnp.sum(dkeke, axis=0, keepdims=True) + dd * d
            dla_ref[rows, :] = _cum_dot(cum_t, db) + dbend
        gstate[...] = g

    key_out = la_spec
    return pl.pallas_call(
        body, name=name, grid=(GLA_HEADS, nblk),
        in_specs=[o_spec, q_spec, k_spec, v_spec, la_spec, st_spec],
        out_specs=[key_out, key_out, o_spec, key_out],
        out_shape=[jax.ShapeDtypeStruct((S, GLA_KEY), F32), jax.ShapeDtypeStruct((S, GLA_KEY), F32),
                   jax.ShapeDtypeStruct((S, GLA_VAL), F32), jax.ShapeDtypeStruct((S, GLA_KEY), F32)],
        scratch_shapes=[pltpu.VMEM((GLA_DV, GLA_DK), F32)],
        compiler_params=_cp(("parallel", "arbitrary")),
    )(do, proj, proj, proj, la, states)


def _gla_out_fwd(of, ob, proj, gn, name):
    S = of.shape[0]
    tm = _pick(S, (256, 128))
    gblk = (2 * GLA_KEY + GLA_VAL) // GLA_VAL

    def body(of_ref, ob_ref, g_ref, gn_ref, z_ref):
        gnv = gn_ref[...]
        for h in range(GLA_HEADS):
            cols = pl.ds(h * GLA_DV, GLA_DV)
            o = of_ref[:, cols] + ob_ref[:, cols]
            r = lax.rsqrt(jnp.mean(o * o, axis=-1, keepdims=True) + NORM_EPS)
            gv = g_ref[:, cols]
            z_ref[:, cols] = (o * r * gnv * (gv * _sigmoid(gv))).astype(BF16)

    row = pl.BlockSpec((tm, GLA_VAL), lambda i: (i, 0))
    return pl.pallas_call(
        body, name=name, grid=(S // tm,),
        in_specs=[row, row, pl.BlockSpec((tm, GLA_VAL), lambda i: (i, gblk)),
                  pl.BlockSpec((1, GLA_DV), lambda i: (0, 0))],
        out_specs=row,
        out_shape=jax.ShapeDtypeStruct((S, GLA_VAL), BF16),
        compiler_params=_cp(("parallel",)),
    )(of, ob, proj, gn)


def _gla_out_bwd(dz, of, ob, proj, gn, name):
    S = of.shape[0]
    tm = _pick(S, (256, 128))
    gblk = (2 * GLA_KEY + GLA_VAL) // GLA_VAL

    def body(dz_ref, of_ref, ob_ref, g_ref, gn_ref, do_ref, dg_ref, dgn_ref):
        i = pl.program_id(0)
        gnv = gn_ref[...]
        part = jnp.zeros((1, GLA_DV), F32)
        for h in range(GLA_HEADS):
            cols = pl.ds(h * GLA_DV, GLA_DV)
            o = of_ref[:, cols] + ob_ref[:, cols]
            r = lax.rsqrt(jnp.mean(o * o, axis=-1, keepdims=True) + NORM_EPS)
            y = o * r
            gv = g_ref[:, cols]
            sg = _sigmoid(gv)
            dzv = dz_ref[:, cols]
            dg_ref[:, cols] = dzv * (y * gnv) * (sg * (1.0 + gv * (1.0 - sg)))
            don = dzv * (gv * sg)
            part = part + jnp.sum(don * y, axis=0, keepdims=True)
            dy = don * gnv
            do_ref[:, cols] = r * (dy - y * jnp.mean(dy * y, axis=-1, keepdims=True))

        @pl.when(i == 0)
        def _():
            dgn_ref[...] = part

        @pl.when(i > 0)
        def _():
            dgn_ref[...] += part

    row = pl.BlockSpec((tm, GLA_VAL), lambda i: (i, 0))
    one = pl.BlockSpec((1, GLA_DV), lambda i: (0, 0))
    return pl.pallas_call(
        body, name=name, grid=(S // tm,),
        in_specs=[row, row, row, pl.BlockSpec((tm, GLA_VAL), lambda i: (i, gblk)), one],
        out_specs=[row, row, one],
        out_shape=[jax.ShapeDtypeStruct((S, GLA_VAL), F32), jax.ShapeDtypeStruct((S, GLA_VAL), F32),
                   jax.ShapeDtypeStruct((1, GLA_DV), F32)],
        compiler_params=_cp(("arbitrary",)),
    )(dz, of, ob, proj, gn)


N_QK_HEADS = ATT_QH + ATT_KVH


def _qk_prep_fwd(proj, qn, kn, rc, rs, name):
    S = proj.shape[0]
    tm = _pick(S, (256, 128))
    W = N_QK_HEADS * ATT_HD
    scale = ATT_HD ** -0.5

    def body(p_ref, qn_ref, kn_ref, rc_ref, rs_ref, v_in_ref, qk_ref, v_ref, kt_ref, vt_ref):
        c, s = rc_ref[...], rs_ref[...]
        for h in range(N_QK_HEADS):
            cols = pl.ds(h * ATT_HD, ATT_HD)
            w = qn_ref[...] if h < ATT_QH else kn_ref[...]
            xv = p_ref[:, cols]
            r = lax.rsqrt(jnp.mean(xv * xv, axis=-1, keepdims=True) + NORM_EPS)
            y = xv * r * w
            out = y * c + pltpu.roll(y, ATT_HD // 2, 1) * s
            if h < ATT_QH:
                qk_ref[:, cols] = (out * scale).astype(BF16)
            else:
                qk_ref[:, cols] = out.astype(BF16)
                kt_ref[pl.ds((h - ATT_QH) * ATT_HD, ATT_HD), :] = out.T.astype(BF16)
        v_ref[...] = v_in_ref[...].astype(BF16)
        for h in range(ATT_KVH):
            vt_ref[pl.ds(h * ATT_HD, ATT_HD), :] = v_in_ref[:, pl.ds(h * ATT_HD, ATT_HD)].T.astype(BF16)

    one = pl.BlockSpec((1, ATT_HD), lambda i: (0, 0))
    tab = pl.BlockSpec((tm, ATT_HD), lambda i: (i, 0))
    vw = ATT_KVH * ATT_HD
    tr = pl.BlockSpec((vw, tm), lambda i: (0, i))
    return pl.pallas_call(
        body, name=name, grid=(S // tm,),
        in_specs=[pl.BlockSpec((tm, W), lambda i: (i, 0)), one, one, tab, tab,
                  pl.BlockSpec((tm, vw), lambda i: (i, W // vw))],
        out_specs=[pl.BlockSpec((tm, W), lambda i: (i, 0)), pl.BlockSpec((tm, vw), lambda i: (i, 0)), tr, tr],
        out_shape=[jax.ShapeDtypeStruct((S, W), BF16), jax.ShapeDtypeStruct((S, vw), BF16),
                   jax.ShapeDtypeStruct((vw, S), BF16), jax.ShapeDtypeStruct((vw, S), BF16)],
        compiler_params=_cp(("parallel",)),
    )(proj, qn, kn, rc, rs, proj)


def _qk_prep_bwd(dqk, proj, qn, kn, rc, rs, name):
    S = proj.shape[0]
    tm = _pick(S, (256, 128))
    W = N_QK_HEADS * ATT_HD

    def body(d_ref, p_ref, qn_ref, kn_ref, rc_ref, rs_ref, dp_ref, dqn_ref, dkn_ref):
        i = pl.program_id(0)
        c, s = rc_ref[...], rs_ref[...]
        parts = [jnp.zeros((1, ATT_HD), F32), jnp.zeros((1, ATT_HD), F32)]
        for h in range(N_QK_HEADS):
            cols = pl.ds(h * ATT_HD, ATT_HD)
            w = qn_ref[...] if h < ATT_QH else kn_ref[...]
            dout = d_ref[:, cols]
            dy = dout * c + pltpu.roll(dout * s, ATT_HD // 2, 1)
            xv = p_ref[:, cols]
            r = lax.rsqrt(jnp.mean(xv * xv, axis=-1, keepdims=True) + NORM_EPS)
            xr = xv * r
            which = 0 if h < ATT_QH else 1
            parts[which] = parts[which] + jnp.sum(dy * xr, axis=0, keepdims=True)
            dxr = dy * w
            dp_ref[:, cols] = r * (dxr - xr * jnp.mean(dxr * xr, axis=-1, keepdims=True))

        @pl.when(i == 0)
        def _():
            dqn_ref[...] = parts[0]
            dkn_ref[...] = parts[1]

        @pl.when(i > 0)
        def _():
            dqn_ref[...] += parts[0]
            dkn_ref[...] += parts[1]

    one = pl.BlockSpec((1, ATT_HD), lambda i: (0, 0))
    tab = pl.BlockSpec((tm, ATT_HD), lambda i: (i, 0))
    row = pl.BlockSpec((tm, W), lambda i: (i, 0))
    return pl.pallas_call(
        body, name=name, grid=(S // tm,),
        in_specs=[row, row, one, one, tab, tab],
        out_specs=[row, one, one],
        out_shape=[jax.ShapeDtypeStruct((S, W), F32), jax.ShapeDtypeStruct((1, ATT_HD), F32),
                   jax.ShapeDtypeStruct((1, ATT_HD), F32)],
        compiler_params=_cp(("arbitrary",)),
    )(dqk, proj, qn, kn, rc, rs)


ATT_TQ = 1024
LSE_ROWS = 8


def _attn_fwd(qk, vt, name):
    S = qk.shape[0]
    tq = min(ATT_TQ, S)

    def body(q_ref, k_ref, vt_ref, o_ref, lse_ref):
        st = _dg(k_ref[...], q_ref[...], 1, 1)
        m = jnp.max(st, axis=0, keepdims=True)
        pt = jnp.exp(st - m)
        l = jnp.sum(pt, axis=0, keepdims=True)
        ot = jnp.dot(vt_ref[...], pt.astype(BF16), preferred_element_type=F32)
        o_ref[...] = (ot * (1.0 / l)).T
        lse_ref[...] = jnp.broadcast_to(m + jnp.log(l), (LSE_ROWS, tq))

    qo = pl.BlockSpec((tq, ATT_HD), lambda h, i: (i, h))
    return pl.pallas_call(
        body, name=name, grid=(ATT_QH, S // tq),
        in_specs=[qo, pl.BlockSpec((S, ATT_HD), lambda h, i: (0, ATT_QH + h // ATT_GROUP)),
                  pl.BlockSpec((ATT_HD, S), lambda h, i: (h // ATT_GROUP, 0))],
        out_specs=[qo, pl.BlockSpec((LSE_ROWS, tq), lambda h, i: (h, i))],
        out_shape=[jax.ShapeDtypeStruct((S, ATT_QH * ATT_HD), F32),
                   jax.ShapeDtypeStruct((ATT_QH * LSE_ROWS, S), F32)],
        compiler_params=_cp(("parallel", "parallel")),
    )(qk, qk, vt)


def _attn_bwd(do, o, lse, qk, v, kt, name):
    S = qk.shape[0]
    tq = min(ATT_TQ, S)
    scale = ATT_HD ** -0.5

    def body(do_ref, o_ref, lse_ref, q_ref, k_ref, v_ref, kt_ref, dq_ref, dk_ref, dv_ref):
        g = pl.program_id(1)
        i = pl.program_id(2)

        @pl.when((g == 0) & (i == 0))
        def _():
            dk_ref[...] = jnp.zeros_like(dk_ref)
            dv_ref[...] = jnp.zeros_like(dv_ref)

        q = q_ref[...]
        dov = do_ref[...]
        dob = dov.astype(BF16)
        delta = jnp.sum((dov * o_ref[...]).T, axis=0, keepdims=True)
        st = _dg(k_ref[...], q, 1, 1)
        pt = jnp.exp(st - lse_ref[0:1, :])
        dpt = _dg(v_ref[...], dob, 1, 1)
        dst = (pt * (dpt - delta)).astype(BF16)
        dv_ref[...] += jnp.dot(pt.astype(BF16), dob, preferred_element_type=F32)
        dk_ref[...] += jnp.dot(dst, q, preferred_element_type=F32)
        dq_ref[...] = jnp.dot(kt_ref[...], dst, preferred_element_type=F32).T * scale

    qo = pl.BlockSpec((tq, ATT_HD), lambda kv, g, i: (i, kv * ATT_GROUP + g))
    kvo = pl.BlockSpec((S, ATT_HD), lambda kv, g, i: (0, kv))
    return pl.pallas_call(
        body, name=name, grid=(ATT_KVH, ATT_GROUP, S // tq),
        in_specs=[qo, qo, pl.BlockSpec((LSE_ROWS, tq), lambda kv, g, i: (kv * ATT_GROUP + g, i)), qo,
                  pl.BlockSpec((S, ATT_HD), lambda kv, g, i: (0, ATT_QH + kv)), kvo,
                  pl.BlockSpec((ATT_HD, S), lambda kv, g, i: (kv, 0))],
        out_specs=[qo, kvo, kvo],
        out_shape=[jax.ShapeDtypeStruct((S, ATT_QH * ATT_HD), F32),
                   jax.ShapeDtypeStruct((S, ATT_KVH * ATT_HD), F32),
                   jax.ShapeDtypeStruct((S, ATT_KVH * ATT_HD), F32)],
        compiler_params=_cp(("parallel", "arbitrary", "arbitrary")),
    )(do, o, lse, qk, qk, v, kt)


def _adamw(w, g, m, v, name):
    rows, cols = w.shape
    tr = rows
    for cand in (512, 256, 128, 64, 32, 16, 8):
        if rows % cand == 0 and cand * cols * 4 <= 2 * 1024 * 1024:
            tr = cand
            break

    def body(w_ref, g_ref, m_ref, v_ref, d_ref, nm_ref, nv_ref):
        gv = g_ref[...]
        nm = ADAM_B1 * m_ref[...] + (1.0 - ADAM_B1) * gv
        nv = ADAM_B2 * v_ref[...] + (1.0 - ADAM_B2) * (gv * gv)
        m_hat = nm / (1.0 - ADAM_B1 ** ADAM_STEP)
        v_hat = nv / (1.0 - ADAM_B2 ** ADAM_STEP)
        d_ref[...] = -ADAM_LR * (m_hat / (jnp.sqrt(v_hat) + ADAM_EPS) + ADAM_WD * w_ref[...])
        nm_ref[...] = nm
        nv_ref[...] = nv

    blk = pl.BlockSpec((tr, cols), lambda i: (i, 0))
    return pl.pallas_call(
        body, name=name, grid=(rows // tr,),
        in_specs=[blk] * 4, out_specs=[blk] * 3,
        out_shape=[jax.ShapeDtypeStruct((rows, cols), F32)] * 3,
        compiler_params=_cp(("parallel",)),
    )(w, g, m, v)


ANY = pl.BlockSpec(memory_space=pl.ANY)


def _place():
    return lax.axis_index("x"), lax.axis_index("y"), lax.axis_index("c")


def _other_chips(x, y):
    return [(1 - x, y), (x, 1 - y), (1 - x, 1 - y)]


def _half_rows(c, H):
    return pl.ds(pl.multiple_of(c * H, 8), H)


def _allreduce_small(v, name):
    R = v.shape[0]
    n_dev = 8

    def body(v_ref, sum_ref, all_ref, send_sems, recv_sems, local_sem):
        x, y, c = _place()
        me, sibling = (x, y, c), (x, y, 1 - c)
        chips = _other_chips(x, y)

        def rows(px, py, pc):
            return all_ref.at[pl.ds(pl.multiple_of((4 * px + 2 * py + pc) * R, 8), R), :]

        def copy(k, block, to, src=None):
            return pltpu.make_async_remote_copy(
                src_ref=rows(*block) if src is None else src, dst_ref=rows(*block),
                send_sem=send_sems.at[k], recv_sem=recv_sems.at[k], device_id=to, device_id_type=MESH)

        own = pltpu.make_async_copy(v_ref, rows(*me), local_sem)
        own.start()
        first = [copy(0, me, sibling, src=v_ref)]
        first += [copy(1 + j, me, (*chip, c), src=v_ref) for j, chip in enumerate(chips)]
        for cp in first:
            cp.start()
        passed = [copy(4 + j, (*chip, c), sibling) for j, chip in enumerate(chips)]
        for j, chip in enumerate(chips):
            copy(1 + j, (*chip, c), me).wait_recv()
            passed[j].start()
        copy(0, sibling, me).wait_recv()
        for j, chip in enumerate(chips):
            copy(4 + j, (*chip, 1 - c), me).wait_recv()
        for cp in first + passed:
            cp.wait_send()
        own.wait()
        acc = all_ref[pl.ds(0, R), :]
        for d in range(1, n_dev):
            acc = acc + all_ref[pl.ds(d * R, R), :]
        sum_ref[...] = acc

    vm = pl.BlockSpec(memory_space=pltpu.VMEM)
    return pl.pallas_call(
        body, name=name,
        in_specs=[vm], out_specs=[vm, vm],
        out_shape=[jax.ShapeDtypeStruct((R, LANES), F32), jax.ShapeDtypeStruct((n_dev * R, LANES), F32)],
        scratch_shapes=[pltpu.SemaphoreType.DMA((7,)), pltpu.SemaphoreType.DMA((7,)), pltpu.SemaphoreType.DMA],
    )(v)[0]


def _swap_other_half(bufs, name):
    n = len(bufs)
    halves = [b.shape[1] // 2 for b in bufs]

    def body(*refs):
        g_refs, got_refs = refs[:n], refs[n:2 * n]
        send_sems, recv_sems = refs[2 * n:]
        x, y, c = _place()
        copies = [pltpu.make_async_remote_copy(
            src_ref=g_refs[k].at[p, _half_rows(1 - c, halves[k])], dst_ref=got_refs[k].at[p],
            send_sem=send_sems.at[N_CHIPS * k + p], recv_sem=recv_sems.at[N_CHIPS * k + p],
            device_id=(x, y, 1 - c), device_id_type=MESH) for k in range(n) for p in range(N_CHIPS)]
        for cp in copies:
            cp.start()
        for cp in copies:
            cp.wait_recv()
        for cp in copies:
            cp.wait_send()

    return pl.pallas_call(
        body, name=name, in_specs=[ANY] * n, out_specs=[ANY] * n,
        out_shape=[jax.ShapeDtypeStruct((N_CHIPS, h, b.shape[2]), b.dtype) for b, h in zip(bufs, halves)],
        scratch_shapes=[pltpu.SemaphoreType.DMA((N_CHIPS * n,)), pltpu.SemaphoreType.DMA((N_CHIPS * n,))],
    )(*bufs)


def _join_halves(bufs, name):
    n = len(bufs)
    halves = [b.shape[0] // 2 for b in bufs]

    def body(*refs):
        outs = refs[n:2 * n]
        send_sems, recv_sems = refs[2 * n:]
        x, y, c = _place()

        def copy(k, core):
            blk = outs[k].at[_half_rows(core, halves[k])]
            return pltpu.make_async_remote_copy(src_ref=blk, dst_ref=blk, send_sem=send_sems.at[k],
                                                recv_sem=recv_sems.at[k], device_id=(x, y, 1 - c),
                                                device_id_type=MESH)

        sends = [copy(k, c) for k in range(n)]
        for cp in sends:
            cp.start()
        for k in range(n):
            copy(k, 1 - c).wait_recv()
        for cp in sends:
            cp.wait_send()

    return pl.pallas_call(
        body, name=name, in_specs=[ANY] * n, out_specs=[ANY] * n,
        out_shape=[jax.ShapeDtypeStruct(b.shape, b.dtype) for b in bufs],
        input_output_aliases={k: k for k in range(n)},
        scratch_shapes=[pltpu.SemaphoreType.DMA((n,)), pltpu.SemaphoreType.DMA((n,))],
    )(*bufs)


def _rs_rows(H, width):
    for cand in (1024, 512, 256, 128, 64, 32, 16):
        if H % cand == 0 and cand * width * 4 <= 1536 * 1024:
            return cand
    return H


def _add_sibling(g, got, c, me, name):
    _, H, width = got.shape
    tb = _rs_rows(H, width)
    nb = H // tb

    def body(sp_ref, g_ref, got_ref, sb_ref, sf_ref):
        p = pl.program_id(1)
        s = g_ref[0] + got_ref[0]
        sb_ref[0] = s.astype(BF16)

        @pl.when(p == sp_ref[1])
        def _():
            sf_ref[...] = s

    grid_spec = pltpu.PrefetchScalarGridSpec(
        num_scalar_prefetch=1, grid=(nb, N_CHIPS),
        in_specs=[pl.BlockSpec((1, tb, width), lambda i, p, sp: (p, sp[0] * nb + i, 0)),
                  pl.BlockSpec((1, tb, width), lambda i, p, sp: (p, i, 0))],
        out_specs=[pl.BlockSpec((1, tb, width), lambda i, p, sp: (p, i, 0)),
                   pl.BlockSpec((tb, width), lambda i, p, sp: (i, 0))])
    return pl.pallas_call(
        body, name=name, grid_spec=grid_spec,
        out_shape=[jax.ShapeDtypeStruct((N_CHIPS, H, width), BF16), jax.ShapeDtypeStruct((H, width), F32)],
        compiler_params=_cp(("arbitrary", "arbitrary")),
    )(jnp.stack([c, me]).astype(jnp.int32), g, got)


def _add_chips(sf, got, others_and_c, name):
    H, width = sf.shape
    tb = _rs_rows(H, width)
    nb = H // tb

    def body(sp_ref, sf_ref, r1_ref, r2_ref, r3_ref, out_ref):
        out_ref[...] = ((sf_ref[...] + r1_ref[0].astype(F32)) + r2_ref[0].astype(F32)) + r3_ref[0].astype(F32)

    def slot(k):
        return pl.BlockSpec((1, tb, width), lambda i, sp: (sp[k], i, 0))

    blk = pl.BlockSpec((tb, width), lambda i, sp: (i, 0))
    grid_spec = pltpu.PrefetchScalarGridSpec(
        num_scalar_prefetch=1, grid=(nb,), in_specs=[blk, slot(0), slot(1), slot(2)],
        out_specs=pl.BlockSpec((tb, width), lambda i, sp: (sp[3] * nb + i, 0)))
    return pl.pallas_call(
        body, name=name, grid_spec=grid_spec,
        out_shape=jax.ShapeDtypeStruct((2 * H, width), F32),
        compiler_params=_cp(("arbitrary",)),
    )(others_and_c.astype(jnp.int32), sf, got, got, got)


REPLICATED = ("norm_mix", "norm_ffn", "gla_b_gate_f", "gla_b_gate_b", "gla_norm", "attn_q_norm", "attn_k_norm",
              "ffn_b_conv")


PIECE_ROWS = 16


def _piece_rows(shape):
    n = 1
    for s in shape:
        n *= s
    rows = n // LANES
    return rows, -(-rows // PIECE_ROWS) * PIECE_ROWS


def _pack(pieces, dtype, row_multiple):
    flat = []
    for p in pieces:
        rows, padded = _piece_rows(p.shape)
        flat.append(jnp.pad(p.astype(dtype).reshape(rows, LANES), ((0, padded - rows), (0, 0))))
    rows = sum(f.shape[0] for f in flat)
    padded = -(-rows // row_multiple) * row_multiple
    if padded > rows:
        flat.append(jnp.zeros((padded - rows, LANES), dtype))
    return jnp.concatenate(flat, axis=0)


def _unpack(buf, shapes):
    out, r = [], 0
    for shp in shapes:
        rows, padded = _piece_rows(shp)
        out.append(buf[r:r + rows].reshape(shp))
        r += padded
    return out


def _own_slot(shard2d, me):
    return lax.dynamic_update_index_in_dim(lax.empty((N_CHIPS,) + shard2d.shape, shard2d.dtype), shard2d, me, 0)


def _layer_small(w, l):
    j = l // 2
    if l % 2 == 0:
        return [w["gla_w_gate_up_f"][j], w["gla_w_gate_up_b"][j], w["ffn_w_conv"][l]]
    return [w["ffn_w_conv"][l]]


def _layer_weight_bufs(w, l, me):
    j = l // 2
    mixer = ("gla_w_in", "gla_w_out") if l % 2 == 0 else ("attn_w_qkv", "attn_w_out")
    bufs = [_own_slot(w[n][j].astype(BF16), me) for n in mixer]
    bufs.append(_own_slot(_pack(_layer_small(w, l), F32, 32), me))
    bufs += [_own_slot(w["ffn_w_up"][l].astype(BF16), me), _own_slot(w["ffn_w_down"][l].astype(BF16), me)]
    return bufs


N_MIXER_BUFS = 3


def _layer_weights(w, l, got):
    rows = lambda t: t.reshape(-1, t.shape[2])
    cols = lambda t: jnp.concatenate([t[p] for p in range(N_CHIPS)], axis=1)
    out = {}
    if len(got) != N_MIXER_BUFS:
        up, down = got[-2:]
        out.update(up=up, up_full=cols(up), down=rows(down))
    if len(got) != 2:
        mix_in, mix_out, small = got[:N_MIXER_BUFS]
        shapes = [t.shape for t in _layer_small(w, l)]
        parts = [_unpack(small[p], shapes) for p in range(N_CHIPS)]
        full_small = [jnp.concatenate([parts[p][k] for p in range(N_CHIPS)], axis=-1) for k in range(len(shapes))]
        out.update(conv=full_small[-1])
        if l % 2 == 0:
            out.update(gla_in=jnp.pad(cols(mix_in), ((0, 0), (0, GLA_IN_PAD - GLA_IN))), gla_out=rows(mix_out),
                       gate_f=full_small[0], gate_b=full_small[1])
        else:
            out.update(qkv=mix_in, attn_out=rows(mix_out))
    return out


HBM = pl.BlockSpec(memory_space=pltpu.HBM)
SEM = pl.BlockSpec(memory_space=pltpu.SEMAPHORE)
SIDE_EFFECT = pltpu.SideEffectType.DATAFLOW_SIDE_EFFECTING


def _gather_start(bufs, after, name):
    n = len(bufs)
    halves = [b.shape[1] // 2 for b in bufs]

    def body(*refs):
        refs = refs[:n] + refs[n + 1:]
        send_sems, recv_sems = refs[n:2 * n], refs[2 * n:3 * n]
        outs, token = refs[3 * n:4 * n], refs[4 * n]
        x, y, c = _place()
        me = 2 * x + y
        for k in range(n):
            blk = outs[k].at[me, _half_rows(c, halves[k])]
            for px, py in _other_chips(x, y):
                pltpu.make_async_remote_copy(src_ref=blk, dst_ref=blk, send_sem=send_sems[k], recv_sem=recv_sems[k],
                                             device_id=(px, py, c), device_id_type=MESH).start()
        token[...] = jnp.zeros_like(token)

    res = pl.pallas_call(
        body, name=name,
        in_specs=[HBM] * n + [ANY],
        out_specs=[SEM] * (2 * n) + [HBM] * n + [pl.BlockSpec(memory_space=pltpu.VMEM)],
        out_shape=[pltpu.SemaphoreType.DMA(())] * (2 * n) + [pltpu.HBM(b.shape, b.dtype) for b in bufs]
        + [jax.ShapeDtypeStruct((8, LANES), F32)],
        input_output_aliases={k: 2 * n + k for k in range(n)},
        compiler_params=pltpu.CompilerParams(has_side_effects=SIDE_EFFECT),
    )(*[pltpu.with_memory_space_constraint(b, pltpu.HBM) for b in bufs], after)
    return res[:n], res[n:2 * n], res[2 * n:3 * n], res[3 * n]


def _gather_wait(send_sems, recv_sems, thru, after, name):
    n = len(thru)
    halves = [b.shape[1] // 2 for b in thru]

    def body(*refs):
        ss, rs = refs[n:2 * n], refs[2 * n:3 * n]
        outs = refs[3 * n + 1:]
        x, y, c = _place()
        for k in range(n):
            three = outs[k].at[pl.ds(0, N_CHIPS - 1), _half_rows(c, halves[k])]
            cp = pltpu.make_async_remote_copy(src_ref=three, dst_ref=three, send_sem=ss[k], recv_sem=rs[k],
                                              device_id=(x, y, c), device_id_type=MESH)
            cp.wait_send()
            cp.wait_recv()

    return pl.pallas_call(
        body, name=name,
        in_specs=[HBM] * n + [SEM] * (2 * n) + [ANY],
        out_specs=[HBM] * n,
        out_shape=[pltpu.HBM(b.shape, b.dtype) for b in thru],
        input_output_aliases={k: k for k in range(n)},
        compiler_params=pltpu.CompilerParams(has_side_effects=SIDE_EFFECT),
    )(*thru, *send_sems, *recv_sems, after)


def _send_start(sbs, name):
    n = len(sbs)

    def body(*refs):
        send_sems, recv_sems = refs[2 * n:3 * n], refs[3 * n:4 * n]
        srcs, lands, token = refs[4 * n:5 * n], refs[5 * n:6 * n], refs[6 * n]
        x, y, c = _place()
        me = 2 * x + y
        for k in range(n):
            for px, py in _other_chips(x, y):
                pltpu.make_async_remote_copy(src_ref=srcs[k].at[2 * px + py], dst_ref=lands[k].at[me],
                                             send_sem=send_sems[k], recv_sem=recv_sems[k],
                                             device_id=(px, py, c), device_id_type=MESH).start()
        token[...] = jnp.zeros_like(token)

    hbm = lambda a: pltpu.with_memory_space_constraint(a, pltpu.HBM)
    res = pl.pallas_call(
        body, name=name,
        in_specs=[HBM] * (2 * n),
        out_specs=[SEM] * (2 * n) + [HBM] * (2 * n) + [pl.BlockSpec(memory_space=pltpu.VMEM)],
        out_shape=[pltpu.SemaphoreType.DMA(())] * (2 * n) + [pltpu.HBM(s.shape, s.dtype) for s in sbs] * 2
        + [jax.ShapeDtypeStruct((8, LANES), F32)],
        input_output_aliases={k: 2 * n + k for k in range(2 * n)},
        compiler_params=pltpu.CompilerParams(has_side_effects=SIDE_EFFECT),
    )(*[hbm(s) for s in sbs], *[hbm(lax.empty(s.shape, s.dtype)) for s in sbs])
    return res[:n], res[n:2 * n], res[2 * n:3 * n], res[3 * n:4 * n], res[4 * n]


def _send_wait(send_sems, recv_sems, srcs, lands, after, name):
    n = len(srcs)

    def body(*refs):
        ss, rs = refs[2 * n:3 * n], refs[3 * n:4 * n]
        s_out, l_out = refs[4 * n + 1:5 * n + 1], refs[5 * n + 1:]
        x, y, c = _place()
        for k in range(n):
            cp = pltpu.make_async_remote_copy(src_ref=s_out[k].at[pl.ds(0, N_CHIPS - 1)],
                                              dst_ref=l_out[k].at[pl.ds(0, N_CHIPS - 1)], send_sem=ss[k],
                                              recv_sem=rs[k], device_id=(x, y, c), device_id_type=MESH)
            cp.wait_send()
            cp.wait_recv()

    res = pl.pallas_call(
        body, name=name,
        in_specs=[HBM] * (2 * n) + [SEM] * (2 * n) + [ANY],
        out_specs=[HBM] * (2 * n),
        out_shape=[pltpu.HBM(s.shape, s.dtype) for s in srcs] * 2,
        input_output_aliases={k: k for k in range(2 * n)},
        compiler_params=pltpu.CompilerParams(has_side_effects=SIDE_EFFECT),
    )(*srcs, *lands, *send_sems, *recv_sems, after)
    return res[n:]


def _pass_to_sibling(bufs, name):
    n = len(bufs)
    halves = [b.shape[1] // 2 for b in bufs]

    def body(*refs):
        outs = refs[n:2 * n]
        send_sems, recv_sems = refs[2 * n:]
        x, y, c = _place()
        chips = _other_chips(x, y)

        def copy(k, j, core):
            px, py = chips[j]
            blk = outs[k].at[2 * px + py, _half_rows(core, halves[k])]
            return pltpu.make_async_remote_copy(src_ref=blk, dst_ref=blk, send_sem=send_sems.at[3 * k + j],
                                                recv_sem=recv_sems.at[3 * k + j], device_id=(x, y, 1 - c),
                                                device_id_type=MESH)

        sends = [copy(k, j, c) for k in range(n) for j in range(3)]
        for cp in sends:
            cp.start()
        for k in range(n):
            for j in range(3):
                copy(k, j, 1 - c).wait_recv()
        for cp in sends:
            cp.wait_send()

    return pl.pallas_call(
        body, name=name,
        in_specs=[ANY] * n, out_specs=[ANY] * n,
        out_shape=[jax.ShapeDtypeStruct(b.shape, b.dtype) for b in bufs],
        input_output_aliases={k: k for k in range(n)},
        scratch_shapes=[pltpu.SemaphoreType.DMA((3 * n,)), pltpu.SemaphoreType.DMA((3 * n,))],
    )(*bufs)


def _rope_tables(S):
    rows = S // GRID_W
    row_idx = jnp.repeat(jnp.arange(rows, dtype=F32), GRID_W)
    col_idx = jnp.tile(jnp.arange(GRID_W, dtype=F32), rows)
    pairs = ATT_HD // 4
    inv_freq = ROPE_THETA ** (-jnp.arange(pairs, dtype=F32) / pairs)
    ang = jnp.concatenate([row_idx[:, None] * inv_freq, col_idx[:, None] * inv_freq], axis=-1)
    cos, sin = jnp.cos(ang), jnp.sin(ang)
    return jnp.concatenate([cos, cos], axis=-1), jnp.concatenate([-sin, sin], axis=-1)


def _gate_rows(w, first_row):
    return jnp.zeros((LANES, GLA_KEY), F32).at[first_row:first_row + GLA_RANK].set(w.astype(F32))


def _local_step(x, target, weights_of, grads_out, P):
    S = x.shape[0]
    rc, rs = _rope_tables(S)
    row = lambda a: a.reshape(1, -1)
    saved = []
    for i in range(DEPTH):
        j = i // 2
        W = dict(weights_of(i, "mix", x))
        nm = row(P["norm_mix"][i])
        h1 = _rmsnorm_fwd(x, nm, f"norm_mix_fwd{i}")
        if i % 2 == 0:
            wgf = _gate_rows(W["gate_f"], 0)
            wgb = _gate_rows(W["gate_b"], GLA_RANK)
            bgf, bgb = row(P["gla_b_gate_f"][j]), row(P["gla_b_gate_b"][j])
            gn = row(P["gla_norm"][j])
            proj = _matmul_rows(h1, W["gla_in"], f"gla_in{i}")
            laf, lab = _gla_gate_fwd(proj, wgf, bgf, wgb, bgb, f"gla_gate_fwd{i}")
            of, stf = _gla_scan_fwd(proj, laf, False, f"gla_scan_f_fwd{i}")
            ob, stb = _gla_scan_fwd(proj, lab, True, f"gla_scan_b_fwd{i}")
            z = _gla_out_fwd(of, ob, proj, gn, f"gla_out_fwd{i}")
            xm = _matmul_rows(z, W["gla_out"], f"gla_outproj{i}", res=x)
            mix = dict(proj=proj, laf=laf, lab=lab, of=of, ob=ob, stf=stf, stb=stb, z=z, wgf=wgf, wgb=wgb)
        else:
            proj = _matmul_rows(h1, W["qkv"], f"attn_qkv{i}", w_layer=0)
            qn, kn = row(P["attn_q_norm"][j]), row(P["attn_k_norm"][j])
            qk, vb, kt, vt = _qk_prep_fwd(proj, qn, kn, rc, rs, f"qk_prep_fwd{i}")
            o, lse = _attn_fwd(qk, vt, f"attn_fwd{i}")
            xm = _matmul_rows(o, W["attn_out"], f"attn_outproj{i}", res=x)
            mix = dict(proj=proj, qk=qk, vb=vb, kt=kt, o=o, lse=lse)
        W.update(weights_of(i, "ffn", xm))
        h2 = _rmsnorm_fwd(xm, row(P["norm_ffn"][i]), f"norm_ffn_fwd{i}")
        a, uv, ug = _ffn_mid_fwd(h2, W["up_full"], W["conv"], row(P["ffn_b_conv"][i]), f"ffn_mid_fwd{i}")
        xo = _matmul_rows(a, W["down"], f"ffn_down{i}", res=xm)
        saved.append(dict(x=x, h1=h1, xm=xm, h2=h2, uv=uv, ug=ug, mix=mix, W=W))
        x = xo

    dx, dxb, loss = _loss_grad(x, target, "loss")

    G = {n: [None] * (DEPTH if n.startswith(("norm", "ffn")) else DEPTH // 2) for n in REPLICATED}
    token = None
    for i in reversed(range(DEPTH)):
        j = i // 2
        sv = saved[i]
        mix = sv["mix"]
        W = sv["W"]
        bconv = row(P["ffn_b_conv"][i])
        if token is not None:
            t = token[0:1, 0:1]
            bconv = jnp.where(t == 0.0, bconv, t)
        duv, dug, a, gwv, gwg = _ffn_mid_bwd(dxb, W["down"], sv["uv"], sv["ug"], W["conv"], bconv, f"ffn_mid_bwd{i}")
        L = dict(down=_wgrad(a, dxb, f"ffn_down_wgrad{i}", chips="rows"),
                 up=_wgrad(sv["h2"], (duv, dug), f"ffn_up_wgrad{i}", chips="cols"),
                 small=[jnp.concatenate([gwv[:3], gwg[:3]], axis=1)])
        G["ffn_b_conv"][i] = jnp.concatenate([gwv[3], gwg[3]], axis=0)
        dxm, dxmb, dn = _dgrad_norm((duv, dug), W["up"], sv["xm"], row(P["norm_ffn"][i]), dx, f"ffn_up_dgrad{i}",
                                    w_layer=0)
        G["norm_ffn"][i] = dn[0]
        if i % 2 == 0:
            proj = mix["proj"]
            bgf, bgb = row(P["gla_b_gate_f"][j]), row(P["gla_b_gate_b"][j])
            gn = row(P["gla_norm"][j])
            dz = _matmul_rows(dxmb, W["gla_out"], f"gla_outproj_dgrad{i}", transposed=True)
            L["out"] = _wgrad(mix["z"], dxmb, f"gla_outproj_wgrad{i}", chips="rows")
            do, dg, dgn = _gla_out_bwd(dz, mix["of"], mix["ob"], proj, gn, f"gla_out_bwd{i}")
            G["gla_norm"][j] = dgn[0]
            dqf, dkf, dvf, dlaf = _gla_scan_bwd(do, proj, mix["laf"], mix["stf"], False, f"gla_scan_f_bwd{i}")
            dqb, dkb, dvb, dlab = _gla_scan_bwd(do, proj, mix["lab"], mix["stb"], True, f"gla_scan_b_bwd{i}")
            dr, dwf, dbf, dwb, dbb = _gla_gate_bwd(dlaf, dlab, proj, mix["wgf"], bgf, mix["wgb"], bgb,
                                                   f"gla_gate_bwd{i}")
            L["small"] = [dwf[:GLA_RANK], dwb[GLA_RANK:2 * GLA_RANK]] + L["small"]
            G["gla_b_gate_f"][j] = dbf[0]
            G["gla_b_gate_b"][j] = dbb[0]
            dproj = jnp.concatenate([dqf + dqb, dkf + dkb, dvf + dvb, dg, dr], axis=1).astype(BF16)
            L["mix_in"] = _wgrad(sv["h1"], dproj, f"gla_in_wgrad{i}")
            dx, dxb, dn = _dgrad_norm(dproj, W["gla_in"], sv["x"], row(P["norm_mix"][i]), dxm, f"mix_in_dgrad{i}")
        else:
            proj = mix["proj"]
            qn, kn = row(P["attn_q_norm"][j]), row(P["attn_k_norm"][j])
            do = _matmul_rows(dxmb, W["attn_out"], f"attn_outproj_dgrad{i}", transposed=True)
            L["out"] = _wgrad(mix["o"], dxmb, f"attn_outproj_wgrad{i}", chips="rows")
            dq, dk, dv = _attn_bwd(do, mix["o"], mix["lse"], mix["qk"], mix["vb"], mix["kt"], f"attn_bwd{i}")
            dqk = jnp.concatenate([dq, dk], axis=1)
            dpqk, dqn, dkn = _qk_prep_bwd(dqk, proj, qn, kn, rc, rs, f"qk_prep_bwd{i}")
            G["attn_q_norm"][j] = dqn[0]
            G["attn_k_norm"][j] = dkn[0]
            dproj = jnp.concatenate([dpqk, dv], axis=1).astype(BF16)
            L["mix_in"] = _wgrad(sv["h1"], dproj, f"attn_qkv_wgrad{i}", chips="cols")
            dx, dxb, dn = _dgrad_norm(dproj, W["qkv"], sv["x"], row(P["norm_mix"][i]), dxm, f"mix_in_dgrad{i}",
                                      w_layer=0)
        G["norm_mix"][i] = dn[0]
        token = grads_out(i, L)
    return loss, dx, G


def kernel(x, norm_mix, norm_ffn, gla_w_in, gla_w_gate_up_f, gla_b_gate_f, gla_w_gate_up_b, gla_b_gate_b, gla_norm, gla_w_out, attn_w_qkv, attn_q_norm, attn_k_norm, attn_w_out, ffn_w_up, ffn_w_conv, ffn_b_conv, ffn_w_down, loss_target, m_norm_mix, m_norm_ffn, m_gla_w_in, m_gla_w_gate_up_f, m_gla_b_gate_f, m_gla_w_gate_up_b, m_gla_b_gate_b, m_gla_norm, m_gla_w_out, m_attn_w_qkv, m_attn_q_norm, m_attn_k_norm, m_attn_w_out, m_ffn_w_up, m_ffn_w_conv, m_ffn_b_conv, m_ffn_w_down, v_norm_mix, v_norm_ffn, v_gla_w_in, v_gla_w_gate_up_f, v_gla_b_gate_f, v_gla_w_gate_up_b, v_gla_b_gate_b, v_gla_norm, v_gla_w_out, v_attn_w_qkv, v_attn_q_norm, v_attn_k_norm, v_attn_w_out, v_ffn_w_up, v_ffn_w_conv, v_ffn_b_conv, v_ffn_w_down):
    names = ("norm_mix", "norm_ffn", "gla_w_in", "gla_w_gate_up_f", "gla_b_gate_f", "gla_w_gate_up_b",
             "gla_b_gate_b", "gla_norm", "gla_w_out", "attn_w_qkv", "attn_q_norm", "attn_k_norm", "attn_w_out",
             "ffn_w_up", "ffn_w_conv", "ffn_b_conv", "ffn_w_down")
    w = dict(zip(names, (norm_mix, norm_ffn, gla_w_in, gla_w_gate_up_f, gla_b_gate_f, gla_w_gate_up_b,
                         gla_b_gate_b, gla_norm, gla_w_out, attn_w_qkv, attn_q_norm, attn_k_norm, attn_w_out,
                         ffn_w_up, ffn_w_conv, ffn_b_conv, ffn_w_down)))
    m = dict(zip(names, (m_norm_mix, m_norm_ffn, m_gla_w_in, m_gla_w_gate_up_f, m_gla_b_gate_f,
                         m_gla_w_gate_up_b, m_gla_b_gate_b, m_gla_norm, m_gla_w_out, m_attn_w_qkv, m_attn_q_norm,
                         m_attn_k_norm, m_attn_w_out, m_ffn_w_up, m_ffn_w_conv, m_ffn_b_conv, m_ffn_w_down)))
    v = dict(zip(names, (v_norm_mix, v_norm_ffn, v_gla_w_in, v_gla_w_gate_up_f, v_gla_b_gate_f,
                         v_gla_w_gate_up_b, v_gla_b_gate_b, v_gla_norm, v_gla_w_out, v_attn_w_qkv, v_attn_q_norm,
                         v_attn_k_norm, v_attn_w_out, v_ffn_w_up, v_ffn_w_conv, v_ffn_b_conv, v_ffn_w_down)))
    px, py, pc = _place()
    me = 2 * px + py

    started, token = [], w["norm_mix"]
    for l in range(DEPTH):
        started.append(_gather_start(_layer_weight_bufs(w, l, me), token, f"gather_start{l}"))
        token = started[-1][3]
    fetched = {}

    def weights_of(l, part, after):
        send_sems, recv_sems, thru, _ = started[l]
        if l == 0:
            pick = slice(0, N_MIXER_BUFS) if part == "mix" else slice(N_MIXER_BUFS, None)
            landed = _gather_wait(send_sems[pick], recv_sems[pick], thru[pick], token if part == "mix" else after,
                                  f"gather_wait{l}_{part}")
            return _layer_weights(w, l, _pass_to_sibling(landed, f"gather_pass{l}_{part}"))
        if part == "mix":
            landed = _gather_wait(send_sems, recv_sems, thru, after, f"gather_wait{l}")
            fetched[l] = _layer_weights(w, l, _pass_to_sibling(landed, f"gather_pass{l}"))
        return fetched[l]

    sent = {}

    def grads_out(l, L):
        mix_in = L["mix_in"]
        if l % 2 == 0:
            width = w["gla_w_in"].shape[2]
            mix_in = jnp.stack([mix_in[:, p * width:(p + 1) * width] for p in range(N_CHIPS)])
        cut = lambda t, p: lax.slice_in_dim(t, p * (t.shape[-1] // N_CHIPS), (p + 1) * (t.shape[-1] // N_CHIPS),
                                            axis=t.ndim - 1)
        small = jnp.stack([_pack([cut(t, p) for t in L["small"]], F32, 32) for p in range(N_CHIPS)])
        bufs = [mix_in, L["out"], small, L["up"], L["down"]]
        gots = _swap_other_half(bufs, f"grads{l}_to_sibling")
        sums = [_add_sibling(b, g, pc, me, f"grads{l}_add_sibling{k}") for k, (b, g) in enumerate(zip(bufs, gots))]
        send_sems, recv_sems, srcs, lands, tok = _send_start([s[0] for s in sums], f"grads{l}_start")
        sent[l] = (send_sems, recv_sems, srcs, lands, [s[1] for s in sums])
        sent["last_token"] = tok
        return tok

    P = {n: w[n] for n in REPLICATED}

    loss_part, dx, grads = _local_step(x[0], loss_target[0], weights_of, grads_out, P)

    small = _pack([jnp.stack(grads[n]) for n in REPLICATED] + [loss_part], F32, 16)
    t = sent["last_token"][0:1, 0:1]
    small_sum = _allreduce_small(jnp.where(t == 0.0, small, t), "small_allreduce")
    others_and_c = jnp.stack([jnp.where(me <= k, k + 1, k) for k in range(N_CHIPS - 1)] + [pc])
    mine, after = {}, small_sum
    for l in reversed(range(DEPTH)):
        send_sems, recv_sems, srcs, lands, own = sent[l]
        landed = _send_wait(send_sems, recv_sems, srcs, lands, after, f"grads{l}_wait")
        halves = [_add_chips(own[k], landed[k], others_and_c, f"grads{l}_add_chips{k}") for k in range(len(own))]
        mine[l] = _join_halves(halves, f"grads{l}_join_halves")
        after = mine[l][0]
    gsh = {}
    for n, k, layers in (("ffn_w_up", 3, range(DEPTH)), ("ffn_w_down", 4, range(DEPTH)),
                         ("gla_w_in", 0, range(0, DEPTH, 2)), ("gla_w_out", 1, range(0, DEPTH, 2)),
                         ("attn_w_qkv", 0, range(1, DEPTH, 2)), ("attn_w_out", 1, range(1, DEPTH, 2))):
        gsh[n] = jnp.stack([mine[l][k] for l in layers])
    small_mine = [_unpack(mine[l][2], [t.shape for t in _layer_small(w, l)]) for l in range(DEPTH)]
    gsh["ffn_w_conv"] = jnp.stack([small_mine[l][-1] for l in range(DEPTH)])
    gsh["gla_w_gate_up_f"] = jnp.stack([small_mine[l][0] for l in range(0, DEPTH, 2)])
    gsh["gla_w_gate_up_b"] = jnp.stack([small_mine[l][1] for l in range(0, DEPTH, 2)])

    parts = _unpack(small_sum, [w[n].shape for n in REPLICATED] + [(1, LANES)])
    gsh.update(dict(zip(REPLICATED, parts[:-1])))
    loss = parts[-1][0, 0]

    delta, new_m, new_v = {}, {}, {}
    for n in names:
        shp = w[n].shape
        two_d = (-1, shp[-1])
        d, nm, nv = _adamw(w[n].reshape(two_d), gsh[n].reshape(two_d), m[n].reshape(two_d), v[n].reshape(two_d),
                           f"adamw_{n}")
        delta[n], new_m[n], new_v[n] = d.reshape(shp), nm.reshape(shp), nv.reshape(shp)

    return (loss, dx[None], *[gsh[n] for n in names], *[delta[n] for n in names],
            *[new_m[n] for n in names], *[new_v[n] for n in names])
```

```python
import jax
import jax.numpy as jnp
from jax import lax
from jax.experimental import pallas as pl
from jax.experimental.pallas import tpu as pltpu

F32 = jnp.float32
BF16 = jnp.bfloat16
MESH = pl.DeviceIdType.MESH
HIGHEST = lax.Precision.HIGHEST

D_MODEL = 1024
DEPTH = 4
GRID_W = 64
NORM_EPS = 1e-6
GLA_HEADS = 4
GLA_DK = 128
GLA_DV = 256
GLA_KEY = GLA_HEADS * GLA_DK
GLA_VAL = GLA_HEADS * GLA_DV
GLA_RANK = 16
GLA_CHUNK = 64
GLA_GATE_NORMALIZER = 16.0
GLA_IN = 2 * GLA_KEY + 2 * GLA_VAL + 2 * GLA_RANK
GLA_IN_PAD = 3200
GLA_R_BLOCK = (2 * GLA_KEY + 2 * GLA_VAL) // 128
ATT_HD = 128
ATT_QH = 8
ATT_KVH = 2
ATT_GROUP = ATT_QH // ATT_KVH
ATT_QKV = (ATT_QH + 2 * ATT_KVH) * ATT_HD
ROPE_THETA = 10000.0
D_FF = 2816
ADAM_LR = 0.001
ADAM_B1 = 0.9
ADAM_B2 = 0.999
ADAM_EPS = 1e-08
ADAM_WD = 0.01
ADAM_STEP = 10

N_CHIPS = 4
LANES = 128
VMEM_LIMIT = 56 * 1024 * 1024


def _cp(sem):
    return pltpu.CompilerParams(dimension_semantics=sem, vmem_limit_bytes=VMEM_LIMIT)


def _pick(n, cands):
    for c in cands:
        if n % c == 0:
            return c
    return n


def _dg(a, b, ca, cb):
    return lax.dot_general(a, b, (((ca,), (cb,)), ((), ())), preferred_element_type=F32)


def _sigmoid(x):
    return 0.5 * jnp.tanh(0.5 * x) + 0.5


def _rmsnorm_fwd(x, w, name):
    S, D = x.shape
    tm = _pick(S, (512, 256))

    def body(x_ref, w_ref, h_ref):
        xv = x_ref[...]
        r = lax.rsqrt(jnp.mean(xv * xv, axis=-1, keepdims=True) + NORM_EPS)
        h_ref[...] = (xv * r * w_ref[...]).astype(BF16)

    return pl.pallas_call(
        body, name=name, grid=(S // tm,),
        in_specs=[pl.BlockSpec((tm, D), lambda i: (i, 0)), pl.BlockSpec((1, D), lambda i: (0, 0))],
        out_specs=pl.BlockSpec((tm, D), lambda i: (i, 0)),
        out_shape=jax.ShapeDtypeStruct((S, D), BF16),
        compiler_params=_cp(("parallel",)),
    )(x, w)


def _loss_grad(y, t, name):
    S, D = y.shape
    tm = _pick(S, (512, 256))

    def body(y_ref, t_ref, dy_ref, dyb_ref, loss_ref):
        i = pl.program_id(0)
        d = y_ref[...] - t_ref[...]
        dy = d * (1.0 / D)
        dy_ref[...] = dy
        dyb_ref[...] = dy.astype(BF16)
        sq = jnp.sum(jnp.sum(d * d, axis=1, keepdims=True), axis=0, keepdims=True)
        part = jnp.broadcast_to(sq * (0.5 / D), (1, LANES))

        @pl.when(i == 0)
        def _():
            loss_ref[...] = part

        @pl.when(i > 0)
        def _():
            loss_ref[...] += part

    return pl.pallas_call(
        body, name=name, grid=(S // tm,),
        in_specs=[pl.BlockSpec((tm, D), lambda i: (i, 0)), pl.BlockSpec((tm, D), lambda i: (i, 0))],
        out_specs=[pl.BlockSpec((tm, D), lambda i: (i, 0)), pl.BlockSpec((tm, D), lambda i: (i, 0)),
                   pl.BlockSpec((1, LANES), lambda i: (0, 0))],
        out_shape=[jax.ShapeDtypeStruct((S, D), F32), jax.ShapeDtypeStruct((S, D), BF16),
                   jax.ShapeDtypeStruct((1, LANES), F32)],
        compiler_params=_cp(("arbitrary",)),
    )(y, t)


WGRAD_TK = 512


def _wgrad(a, b, name, chips=None):
    S, Kw = a.shape
    pair = isinstance(b, (tuple, list))
    tn = b[0].shape[1] if pair else b.shape[1]
    N = 2 * tn if pair else tn
    tk = _pick(S, (WGRAD_TK, 256, 128))
    nk = S // tk
    if pair:
        b_specs = [pl.BlockSpec((tk, tn), lambda j, k: (jnp.where(j == 0, k, nk - 1), 0)),
                   pl.BlockSpec((tk, tn), lambda j, k: (jnp.where(j == 1, k, 0), 0))]
    else:
        b_specs = [pl.BlockSpec((tk, tn), lambda j, k: (k, 0))]
    if chips == "cols":
        cw = N // N_CHIPS
        span = tn // cw
        o_spec = pl.BlockSpec((span, Kw, cw), lambda j, k: (j, 0, 0))
        out_shape = jax.ShapeDtypeStruct((N_CHIPS, Kw, cw), F32)
    elif chips == "rows":
        assert not pair
        o_spec = pl.BlockSpec((N_CHIPS, Kw // N_CHIPS, N), lambda j, k: (0, 0, 0))
        out_shape = jax.ShapeDtypeStruct((N_CHIPS, Kw // N_CHIPS, N), F32)
    else:
        assert not pair
        o_spec = pl.BlockSpec((Kw, N), lambda j, k: (0, 0))
        out_shape = jax.ShapeDtypeStruct((Kw, N), F32)
    nb = len(b_specs)

    def body(*refs):
        a_ref, b_refs, o_ref, acc = refs[0], refs[1:1 + nb], refs[-2], refs[-1]
        j = pl.program_id(0)
        k = pl.program_id(1)

        @pl.when(k == 0)
        def _():
            acc[...] = jnp.zeros_like(acc)

        av = a_ref[...].astype(BF16)
        for h in range(nb):
            @pl.when(j == h)
            def _():
                acc[...] += _dg(av, b_refs[h][...].astype(BF16), 0, 0)

        @pl.when(k == nk - 1)
        def _():
            v = acc[...]
            if chips == "cols":
                for s in range(span):
                    o_ref[s] = v[:, s * cw:(s + 1) * cw]
            elif chips == "rows":
                rows = Kw // N_CHIPS
                for p in range(N_CHIPS):
                    o_ref[p] = v[p * rows:(p + 1) * rows, :]
            else:
                o_ref[...] = v

    return pl.pallas_call(
        body, name=name, grid=(nb, nk),
        in_specs=[pl.BlockSpec((tk, Kw), lambda j, k: (k, 0))] + b_specs,
        out_specs=o_spec, out_shape=out_shape,
        scratch_shapes=[pltpu.VMEM((Kw, tn), F32)],
        compiler_params=_cp(("parallel", "arbitrary")),
    )(a, *(tuple(b) if pair else (b,)))


def _matmul_rows(a, w, name, res=None, w_layer=None, transposed=False):
    M, K = a.shape
    if w_layer is not None:
        cw = w.shape[2]
        N = N_CHIPS * cw
        w_spec = pl.BlockSpec((N_CHIPS, K, cw), lambda i: (0, w_layer, 0))
    else:
        N = w.shape[0] if transposed else w.shape[1]
        assert w.shape[1 if transposed else 0] == K
        w_spec = pl.BlockSpec(w.shape, lambda i: (0, 0))
    tm = _pick(M, (512, 256, 128))
    has_res = res is not None

    def body(*refs):
        a_ref, w_ref = refs[0], refs[1]
        r_ref = refs[2] if has_res else None
        o_ref = refs[-1]
        av = a_ref[...].astype(BF16)
        if w_layer is not None:
            for p in range(N_CHIPS):
                o_ref[:, pl.ds(p * cw, cw)] = jnp.dot(av, w_ref[p], preferred_element_type=F32)
        else:
            v = _dg(av, w_ref[...], 1, 1 if transposed else 0)
            o_ref[...] = v + r_ref[...] if has_res else v

    row = pl.BlockSpec((tm, N), lambda i: (i, 0))
    return pl.pallas_call(
        body, name=name, grid=(M // tm,),
        in_specs=[pl.BlockSpec((tm, K), lambda i: (i, 0)), w_spec] + ([row] if has_res else []),
        out_specs=row, out_shape=jax.ShapeDtypeStruct((M, N), F32),
        compiler_params=_cp(("parallel",)),
    )(*((a, w) + ((res,) if has_res else ())))


def _dgrad_norm(dy, w, x, wn, dres, name, w_layer=None):
    pair = isinstance(dy, (tuple, list))
    M = dy[0].shape[0] if pair else dy.shape[0]
    Kp = 2 * dy[0].shape[1] if pair else dy.shape[1]
    D = x.shape[1]
    if w_layer is not None:
        cw = w.shape[2]
        assert N_CHIPS * cw == Kp and w.shape[1] % D == 0
        w_spec = pl.BlockSpec((N_CHIPS, D, cw), lambda i: (0, w_layer, 0))
    else:
        assert w.shape == (D, Kp)
        w_spec = pl.BlockSpec((D, Kp), lambda i: (0, 0))
    tm = _pick(M, (256, 128))
    width = Kp // 2 if pair else Kp
    dy_specs = [pl.BlockSpec((tm, width), lambda i: (i, 0))] * (2 if pair else 1)
    nd = len(dy_specs)

    def body(*refs):
        dy_refs = refs[:nd]
        w_ref, x_ref, wn_ref, dres_ref, dx_ref, dxb_ref, dwn_ref = refs[nd:]
        i = pl.program_id(0)
        if w_layer is not None:
            dh = None
            for p in range(N_CHIPS):
                src, off = divmod(p * cw, width)
                part = _dg(dy_refs[src][:, pl.ds(off, cw)], w_ref[p], 1, 1)
                dh = part if dh is None else dh + part
        else:
            dh = _dg(dy_refs[0][...], w_ref[...], 1, 1)
        xv = x_ref[...]
        r = lax.rsqrt(jnp.mean(xv * xv, axis=-1, keepdims=True) + NORM_EPS)
        yv = xv * r
        dyv = dh * wn_ref[...]
        dxv = r * (dyv - yv * jnp.mean(dyv * yv, axis=-1, keepdims=True)) + dres_ref[...]
        dx_ref[...] = dxv
        dxb_ref[...] = dxv.astype(BF16)
        part = jnp.sum(dh * yv, axis=0, keepdims=True)

        @pl.when(i == 0)
        def _():
            dwn_ref[...] = part

        @pl.when(i > 0)
        def _():
            dwn_ref[...] += part

    row = pl.BlockSpec((tm, D), lambda i: (i, 0))
    one = pl.BlockSpec((1, D), lambda i: (0, 0))
    return pl.pallas_call(
        body, name=name, grid=(M // tm,),
        in_specs=dy_specs + [w_spec, row, one, row],
        out_specs=[row, row, one],
        out_shape=[jax.ShapeDtypeStruct((M, D), F32), jax.ShapeDtypeStruct((M, D), BF16),
                   jax.ShapeDtypeStruct((1, D), F32)],
        compiler_params=_cp(("arbitrary",)),
    )(*(tuple(dy) if pair else (dy,)), w, x, wn, dres)


FFN_TN_FWD = 256
FFN_TN_BWD = 128
FFN_ROWS = 256
PAD = 8


def _conv3(pad_ref, w, r0, tr):
    um = pad_ref[pl.ds(PAD - 1 + r0, tr), :]
    uc = pad_ref[pl.ds(PAD + r0, tr), :]
    up = pad_ref[pl.ds(PAD + 1 + r0, tr), :]
    return w[0:1, :] * um + w[1:2, :] * uc + w[2:3, :] * up, (um, uc, up)


def _zero_pads(pad_ref, S, tn):
    pad_ref[pl.ds(0, PAD), :] = jnp.zeros((PAD, tn), F32)
    pad_ref[pl.ds(PAD + S, PAD), :] = jnp.zeros((PAD, tn), F32)


def _ffn_mid_fwd(h, wup, wconv, bconv, name):
    S, D = h.shape
    F = wup.shape[1] // 2
    tn = FFN_TN_FWD
    nb = F // tn
    tr = min(FFN_ROWS, S)

    def body(h_ref, wv_ref, wg_ref, cv_ref, cg_ref, bv_ref, bg_ref, a_ref, uv_ref, ug_ref):
        _zero_pads(uv_ref, S, tn)
        _zero_pads(ug_ref, S, tn)
        hv = h_ref[...]
        uv_ref[pl.ds(PAD, S), :] = jnp.dot(hv, wv_ref[...], preferred_element_type=F32)
        ug_ref[pl.ds(PAD, S), :] = jnp.dot(hv, wg_ref[...], preferred_element_type=F32)
        cwv, cwg, bv, bg = cv_ref[...], cg_ref[...], bv_ref[...], bg_ref[...]
        for r0 in range(0, S, tr):
            cv = _conv3(uv_ref, cwv, r0, tr)[0] + bv
            cg = _conv3(ug_ref, cwg, r0, tr)[0] + bg
            a_ref[pl.ds(r0, tr), :] = (cg * _sigmoid(cg) * cv).astype(BF16)

    col = lambda off: (lambda j: (0, j + off))
    padded = pl.BlockSpec((S + 2 * PAD, tn), col(0))
    return pl.pallas_call(
        body, name=name, grid=(nb,),
        in_specs=[pl.BlockSpec((S, D), lambda j: (0, 0)),
                  pl.BlockSpec((D, tn), col(0)), pl.BlockSpec((D, tn), col(nb)),
                  pl.BlockSpec((3, tn), col(0)), pl.BlockSpec((3, tn), col(nb)),
                  pl.BlockSpec((1, tn), col(0)), pl.BlockSpec((1, tn), col(nb))],
        out_specs=[pl.BlockSpec((S, tn), col(0)), padded, padded],
        out_shape=[jax.ShapeDtypeStruct((S, F), BF16), jax.ShapeDtypeStruct((S + 2 * PAD, F), F32),
                   jax.ShapeDtypeStruct((S + 2 * PAD, F), F32)],
        compiler_params=_cp(("parallel",)),
    )(h, wup, wup, wconv, wconv, bconv, bconv)


def _rows8(rows):
    n = rows[0].shape[1]
    idx = lax.broadcasted_iota(jnp.int32, (8, n), 0)
    out = jnp.zeros((8, n), F32)
    for k, r in enumerate(rows):
        out = jnp.where(idx == k, r, out)
    return out


def _ffn_mid_bwd(dyb, wdown, uv, ug, wconv, bconv, name):
    S, D = dyb.shape
    F = wdown.shape[0]
    tn = FFN_TN_BWD
    nb = F // tn
    tr = min(FFN_ROWS, S)

    def body(dy_ref, wd_ref, uv_ref, ug_ref, cv_ref, cg_ref, bv_ref, bg_ref,
             duv_ref, dug_ref, a_ref, gwv_ref, gwg_ref, pdv, pdg):
        for p in (pdv, pdg):
            _zero_pads(p, S, tn)
        wd = wd_ref[...]
        cwv, cwg, bv, bg = cv_ref[...], cg_ref[...], bv_ref[...], bg_ref[...]
        zero = jnp.zeros((1, tn), F32)
        gv = [zero, zero, zero, zero]
        gg = [zero, zero, zero, zero]
        for r0 in range(0, S, tr):
            cv, shv = _conv3(uv_ref, cwv, r0, tr)
            cg, shg = _conv3(ug_ref, cwg, r0, tr)
            cv = cv + bv
            cg = cg + bg
            sg = _sigmoid(cg)
            sl = cg * sg
            a_ref[pl.ds(r0, tr), :] = (sl * cv).astype(BF16)
            da = _dg(dy_ref[pl.ds(r0, tr), :], wd, 1, 1)
            dcv = da * sl
            dcg = da * cv * (sg * (1.0 + cg * (1.0 - sg)))
            pdv[pl.ds(PAD + r0, tr), :] = dcv
            pdg[pl.ds(PAD + r0, tr), :] = dcg
            for k in range(3):
                gv[k] = gv[k] + jnp.sum(dcv * shv[k], axis=0, keepdims=True)
                gg[k] = gg[k] + jnp.sum(dcg * shg[k], axis=0, keepdims=True)
            gv[3] = gv[3] + jnp.sum(dcv, axis=0, keepdims=True)
            gg[3] = gg[3] + jnp.sum(dcg, axis=0, keepdims=True)
        gwv_ref[...] = _rows8(gv)
        gwg_ref[...] = _rows8(gg)
        for r0 in range(0, S, tr):
            for pd, cw, out in ((pdv, cwv, duv_ref), (pdg, cwg, dug_ref)):
                dm = pd[pl.ds(PAD - 1 + r0, tr), :]
                dc = pd[pl.ds(PAD + r0, tr), :]
                dp = pd[pl.ds(PAD + 1 + r0, tr), :]
                out[pl.ds(r0, tr), :] = (cw[0:1, :] * dp + cw[1:2, :] * dc + cw[2:3, :] * dm).astype(BF16)

    col = lambda off: (lambda j: (0, j + off))
    blk = pl.BlockSpec((S, tn), col(0))
    padded = pl.BlockSpec((S + 2 * PAD, tn), col(0))
    g8 = pl.BlockSpec((8, tn), col(0))
    return pl.pallas_call(
        body, name=name, grid=(nb,),
        in_specs=[pl.BlockSpec((S, D), lambda j: (0, 0)), pl.BlockSpec((tn, D), lambda j: (j, 0)), padded, padded,
                  pl.BlockSpec((3, tn), col(0)), pl.BlockSpec((3, tn), col(nb)),
                  pl.BlockSpec((1, tn), col(0)), pl.BlockSpec((1, tn), col(nb))],
        out_specs=[blk, blk, blk, g8, g8],
        out_shape=[jax.ShapeDtypeStruct((S, F), BF16), jax.ShapeDtypeStruct((S, F), BF16),
                   jax.ShapeDtypeStruct((S, F), BF16), jax.ShapeDtypeStruct((8, F), F32),
                   jax.ShapeDtypeStruct((8, F), F32)],
        scratch_shapes=[pltpu.VMEM((S + 2 * PAD, tn), F32)] * 2,
        compiler_params=_cp(("parallel",)),
    )(dyb, wdown, uv, ug, wconv, wconv, bconv, bconv)


def _log_sigmoid(x):
    return jnp.minimum(x, 0.0) - jnp.log(1.0 + jnp.exp(-jnp.abs(x)))


def _gla_gate_fwd(proj, wgf, bgf, wgb, bgb, name):
    S = proj.shape[0]
    tm = _pick(S, (512, 256))

    def body(r_ref, wf_ref, bf_ref, wb_ref, bb_ref, laf_ref, lab_ref):
        r = r_ref[...].astype(BF16)
        lf = jnp.dot(r, wf_ref[...].astype(BF16), preferred_element_type=F32) + bf_ref[...]
        lb = jnp.dot(r, wb_ref[...].astype(BF16), preferred_element_type=F32) + bb_ref[...]
        laf_ref[...] = _log_sigmoid(lf) * (1.0 / GLA_GATE_NORMALIZER)
        lab_ref[...] = _log_sigmoid(lb) * (1.0 / GLA_GATE_NORMALIZER)

    full = lambda shp: pl.BlockSpec(shp, lambda i: (0, 0))
    row = pl.BlockSpec((tm, GLA_KEY), lambda i: (i, 0))
    return pl.pallas_call(
        body, name=name, grid=(S // tm,),
        in_specs=[pl.BlockSpec((tm, LANES), lambda i: (i, GLA_R_BLOCK)),
                  full((LANES, GLA_KEY)), full((1, GLA_KEY)), full((LANES, GLA_KEY)), full((1, GLA_KEY))],
        out_specs=[row, row],
        out_shape=[jax.ShapeDtypeStruct((S, GLA_KEY), F32)] * 2,
        compiler_params=_cp(("parallel",)),
    )(proj, wgf, bgf, wgb, bgb)


def _gla_gate_bwd(dlaf, dlab, proj, wgf, bgf, wgb, bgb, name):
    S = proj.shape[0]
    tm = _pick(S, (512, 256))

    def body(dlf_ref, dlb_ref, r_ref, wf_ref, bf_ref, wb_ref, bb_ref, dr_ref, dwf_ref, dbf_ref, dwb_ref, dbb_ref):
        i = pl.program_id(0)
        r = r_ref[...].astype(BF16)
        wf = wf_ref[...].astype(BF16)
        wb = wb_ref[...].astype(BF16)
        lf = jnp.dot(r, wf, preferred_element_type=F32) + bf_ref[...]
        lb = jnp.dot(r, wb, preferred_element_type=F32) + bb_ref[...]
        glf = dlf_ref[...] * (1.0 / GLA_GATE_NORMALIZER) * (1.0 / (1.0 + jnp.exp(lf)))
        glb = dlb_ref[...] * (1.0 / GLA_GATE_NORMALIZER) * (1.0 / (1.0 + jnp.exp(lb)))
        gfb = glf.astype(BF16)
        gbb = glb.astype(BF16)
        dr_ref[...] = _dg(gfb, wf, 1, 1) + _dg(gbb, wb, 1, 1)
        parts = (_dg(r, gfb, 0, 0), jnp.sum(glf, axis=0, keepdims=True),
                 _dg(r, gbb, 0, 0), jnp.sum(glb, axis=0, keepdims=True))
        outs = (dwf_ref, dbf_ref, dwb_ref, dbb_ref)

        @pl.when(i == 0)
        def _():
            for o, p in zip(outs, parts):
                o[...] = p

        @pl.when(i > 0)
        def _():
            for o, p in zip(outs, parts):
                o[...] += p

    full = lambda shp: pl.BlockSpec(shp, lambda i: (0, 0))
    row = pl.BlockSpec((tm, GLA_KEY), lambda i: (i, 0))
    return pl.pallas_call(
        body, name=name, grid=(S // tm,),
        in_specs=[row, row, pl.BlockSpec((tm, LANES), lambda i: (i, GLA_R_BLOCK)),
                  full((LANES, GLA_KEY)), full((1, GLA_KEY)), full((LANES, GLA_KEY)), full((1, GLA_KEY))],
        out_specs=[pl.BlockSpec((tm, LANES), lambda i: (i, 0)),
                   full((LANES, GLA_KEY)), full((1, GLA_KEY)), full((LANES, GLA_KEY)), full((1, GLA_KEY))],
        out_shape=[jax.ShapeDtypeStruct((S, LANES), F32),
                   jax.ShapeDtypeStruct((LANES, GLA_KEY), F32), jax.ShapeDtypeStruct((1, GLA_KEY), F32),
                   jax.ShapeDtypeStruct((LANES, GLA_KEY), F32), jax.ShapeDtypeStruct((1, GLA_KEY), F32)],
        compiler_params=_cp(("arbitrary",)),
    )(dlaf, dlab, proj, wgf, bgf, wgb, bgb)


def _gla_masks(rev):
    C = GLA_CHUNK
    t = lax.broadcasted_iota(jnp.int32, (C, C), 0)
    s = lax.broadcasted_iota(jnp.int32, (C, C), 1)
    if rev:
        return (s >= t), (s > t), (t >= s), (t > s)
    return (s <= t), (s <= t), (t <= s), (t <= s)


def _cum_dot(cum, x):
    return jnp.dot(cum.astype(F32), x, precision=HIGHEST, preferred_element_type=F32)


def _gla_chunk_common(q, k, la, cum, end_row):
    b = _cum_dot(cum, la)
    bend = b[end_row:end_row + 1, :]
    e = jnp.exp(b)
    qd = q * (GLA_DK ** -0.5) * e
    ei = jnp.exp(-b)
    ee = jnp.exp(bend - b)
    d = jnp.exp(bend)
    return e, ei, ee, d, qd, k * ei, k * ee


GLA_CB = 16


def _gla_specs(S, rev_order):
    n = S // GLA_CHUNK
    cb = min(GLA_CB, n)
    nblk = n // cb
    rows = cb * GLA_CHUNK
    ci = (lambda i: nblk - 1 - i) if rev_order else (lambda i: i)
    q_spec = pl.BlockSpec((rows, GLA_DK), lambda h, i: (ci(i), h))
    k_spec = pl.BlockSpec((rows, GLA_DK), lambda h, i: (ci(i), GLA_HEADS + h))
    v_spec = pl.BlockSpec((rows, GLA_DV), lambda h, i: (ci(i), GLA_KEY * 2 // GLA_DV + h))
    la_spec = pl.BlockSpec((rows, GLA_DK), lambda h, i: (ci(i), h))
    o_spec = pl.BlockSpec((rows, GLA_DV), lambda h, i: (ci(i), h))
    st_spec = pl.BlockSpec((1, cb, GLA_DV, GLA_DK), lambda h, i: (h, ci(i), 0, 0))
    return n, cb, nblk, q_spec, k_spec, v_spec, la_spec, o_spec, st_spec


def _gla_scan_fwd(proj, la, rev, name):
    S = proj.shape[0]
    C = GLA_CHUNK
    n, cb, nblk, q_spec, k_spec, v_spec, la_spec, o_spec, st_spec = _gla_specs(S, rev)
    end_row = 0 if rev else C - 1
    order = list(range(cb))[::-1] if rev else list(range(cb))

    def body(q_ref, k_ref, v_ref, la_ref, o_ref, st_ref, state):
        i = pl.program_id(1)

        @pl.when(i == 0)
        def _():
            state[...] = jnp.zeros_like(state)

        cum, mask, _, _ = _gla_masks(rev)
        pre, intra, kv = {}, {}, {}
        for cc in order:
            rows = pl.ds(cc * C, C)
            q, k, v, lav = q_ref[rows, :], k_ref[rows, :], v_ref[rows, :], la_ref[rows, :]
            _, _, _, d, qd, ki, ke = _gla_chunk_common(q, k, lav, cum, end_row)
            qdb, kib, keb, vb = qd.astype(BF16), ki.astype(BF16), ke.astype(BF16), v.astype(BF16)
            pre[cc] = (d, qdb)
            att = jnp.where(mask, _dg(qdb, kib, 1, 1), 0.0)
            intra[cc] = jnp.dot(att.astype(BF16), vb, preferred_element_type=F32)
            kv[cc] = _dg(vb, keb, 0, 0)
        st = state[...]
        for cc in order:
            d, qdb = pre[cc]
            o_ref[pl.ds(cc * C, C), :] = intra[cc] + _dg(qdb, st.astype(BF16), 1, 1)
            st_ref[0, cc] = st
            st = st * d + kv[cc]
        state[...] = st

    return pl.pallas_call(
        body, name=name, grid=(GLA_HEADS, nblk),
        in_specs=[q_spec, k_spec, v_spec, la_spec],
        out_specs=[o_spec, st_spec],
        out_shape=[jax.ShapeDtypeStruct((S, GLA_VAL), F32),
                   jax.ShapeDtypeStruct((GLA_HEADS, n, GLA_DV, GLA_DK), F32)],
        scratch_shapes=[pltpu.VMEM((GLA_DV, GLA_DK), F32)],
        compiler_params=_cp(("parallel", "arbitrary")),
    )(proj, proj, proj, la)


def _gla_scan_bwd(do, proj, la, states, rev, name):
    S = proj.shape[0]
    C = GLA_CHUNK
    n, cb, nblk, q_spec, k_spec, v_spec, la_spec, o_spec, st_spec = _gla_specs(S, not rev)
    end_row = 0 if rev else C - 1
    order = list(range(cb)) if rev else list(range(cb))[::-1]

    def body(do_ref, q_ref, k_ref, v_ref, la_ref, st_ref, dq_ref, dk_ref, dv_ref, dla_ref, gstate):
        i = pl.program_id(1)

        @pl.when(i == 0)
        def _():
            gstate[...] = jnp.zeros_like(gstate)

        cum, mask, cum_t, mask_t = _gla_masks(rev)
        g = gstate[...]
        for cc in order:
            rows = pl.ds(cc * C, C)
            q, k, v, lav = q_ref[rows, :], k_ref[rows, :], v_ref[rows, :], la_ref[rows, :]
            dov = do_ref[rows, :]
            st = st_ref[0, cc]
            e, ei, ee, d, qd, ki, ke = _gla_chunk_common(q, k, lav, cum, end_row)
            qdb, kib, keb, vb = qd.astype(BF16), ki.astype(BF16), ke.astype(BF16), v.astype(BF16)
            dob, gb, stb = dov.astype(BF16), g.astype(BF16), st.astype(BF16)
            att_t = jnp.where(mask_t, _dg(kib, qdb, 1, 1), 0.0)
            da = jnp.where(mask, _dg(dob, vb, 1, 1), 0.0)
            da_t = jnp.where(mask_t, _dg(vb, dob, 1, 1), 0.0)
            dv_ref[rows, :] = jnp.dot(att_t.astype(BF16), dob, preferred_element_type=F32) + _dg(keb, gb, 1, 1)
            dqd = (jnp.dot(da.astype(BF16), kib, preferred_element_type=F32)
                   + jnp.dot(dob, stb, preferred_element_type=F32))
            dki = jnp.dot(da_t.astype(BF16), qdb, preferred_element_type=F32)
            dke = jnp.dot(vb, gb, preferred_element_type=F32)
            dd = jnp.sum(st * g, axis=0, keepdims=True)
            g = g * d + _dg(dob, qdb, 0, 0)
            dq_ref[rows, :] = dqd * e * (GLA_DK ** -0.5)
            dk_ref[rows, :] = dki * ei + dke * ee
            dkeke = dke * ke
            db = dqd * qd - dki * ki - dkeke
            dbend = jnp.sum(dkeke, axis=0, keepdims=True) + dd * d
            dla_ref[rows, :] = _cum_dot(cum_t, db) + dbend
        gstate[...] = g

    key_out = la_spec
    return pl.pallas_call(
        body, name=name, grid=(GLA_HEADS, nblk),
        in_specs=[o_spec, q_spec, k_spec, v_spec, la_spec, st_spec],
        out_specs=[key_out, key_out, o_spec, key_out],
        out_shape=[jax.ShapeDtypeStruct((S, GLA_KEY), F32), jax.ShapeDtypeStruct((S, GLA_KEY), F32),
                   jax.ShapeDtypeStruct((S, GLA_VAL), F32), jax.ShapeDtypeStruct((S, GLA_KEY), F32)],
        scratch_shapes=[pltpu.VMEM((GLA_DV, GLA_DK), F32)],
        compiler_params=_cp(("parallel", "arbitrary")),
    )(do, proj, proj, proj, la, states)


def _gla_out_fwd(of, ob, proj, gn, name):
    S = of.shape[0]
    tm = _pick(S, (256, 128))
    gblk = (2 * GLA_KEY + GLA_VAL) // GLA_VAL

    def body(of_ref, ob_ref, g_ref, gn_ref, z_ref):
        gnv = gn_ref[...]
        for h in range(GLA_HEADS):
            cols = pl.ds(h * GLA_DV, GLA_DV)
            o = of_ref[:, cols] + ob_ref[:, cols]
            r = lax.rsqrt(jnp.mean(o * o, axis=-1, keepdims=True) + NORM_EPS)
            gv = g_ref[:, cols]
            z_ref[:, cols] = (o * r * gnv * (gv * _sigmoid(gv))).astype(BF16)

    row = pl.BlockSpec((tm, GLA_VAL), lambda i: (i, 0))
    return pl.pallas_call(
        body, name=name, grid=(S // tm,),
        in_specs=[row, row, pl.BlockSpec((tm, GLA_VAL), lambda i: (i, gblk)),
                  pl.BlockSpec((1, GLA_DV), lambda i: (0, 0))],
        out_specs=row,
        out_shape=jax.ShapeDtypeStruct((S, GLA_VAL), BF16),
        compiler_params=_cp(("parallel",)),
    )(of, ob, proj, gn)


def _gla_out_bwd(dz, of, ob, proj, gn, name):
    S = of.shape[0]
    tm = _pick(S, (256, 128))
    gblk = (2 * GLA_KEY + GLA_VAL) // GLA_VAL

    def body(dz_ref, of_ref, ob_ref, g_ref, gn_ref, do_ref, dg_ref, dgn_ref):
        i = pl.program_id(0)
        gnv = gn_ref[...]
        part = jnp.zeros((1, GLA_DV), F32)
        for h in range(GLA_HEADS):
            cols = pl.ds(h * GLA_DV, GLA_DV)
            o = of_ref[:, cols] + ob_ref[:, cols]
            r = lax.rsqrt(jnp.mean(o * o, axis=-1, keepdims=True) + NORM_EPS)
            y = o * r
            gv = g_ref[:, cols]
            sg = _sigmoid(gv)
            dzv = dz_ref[:, cols]
            dg_ref[:, cols] = dzv * (y * gnv) * (sg * (1.0 + gv * (1.0 - sg)))
            don = dzv * (gv * sg)
            part = part + jnp.sum(don * y, axis=0, keepdims=True)
            dy = don * gnv
            do_ref[:, cols] = r * (dy - y * jnp.mean(dy * y, axis=-1, keepdims=True))

        @pl.when(i == 0)
        def _():
            dgn_ref[...] = part

        @pl.when(i > 0)
        def _():
            dgn_ref[...] += part

    row = pl.BlockSpec((tm, GLA_VAL), lambda i: (i, 0))
    one = pl.BlockSpec((1, GLA_DV), lambda i: (0, 0))
    return pl.pallas_call(
        body, name=name, grid=(S // tm,),
        in_specs=[row, row, row, pl.BlockSpec((tm, GLA_VAL), lambda i: (i, gblk)), one],
        out_specs=[row, row, one],
        out_shape=[jax.ShapeDtypeStruct((S, GLA_VAL), F32), jax.ShapeDtypeStruct((S, GLA_VAL), F32),
                   jax.ShapeDtypeStruct((1, GLA_DV), F32)],
        compiler_params=_cp(("arbitrary",)),
    )(dz, of, ob, proj, gn)


N_QK_HEADS = ATT_QH + ATT_KVH


def _qk_prep_fwd(proj, qn, kn, rc, rs, name):
    S = proj.shape[0]
    tm = _pick(S, (256, 128))
    W = N_QK_HEADS * ATT_HD
    scale = ATT_HD ** -0.5

    def body(p_ref, qn_ref, kn_ref, rc_ref, rs_ref, v_in_ref, qk_ref, v_ref, kt_ref, vt_ref):
        c, s = rc_ref[...], rs_ref[...]
        for h in range(N_QK_HEADS):
            cols = pl.ds(h * ATT_HD, ATT_HD)
            w = qn_ref[...] if h < ATT_QH else kn_ref[...]
            xv = p_ref[:, cols]
            r = lax.rsqrt(jnp.mean(xv * xv, axis=-1, keepdims=True) + NORM_EPS)
            y = xv * r * w
            out = y * c + pltpu.roll(y, ATT_HD // 2, 1) * s
            if h < ATT_QH:
                qk_ref[:, cols] = (out * scale).astype(BF16)
            else:
                qk_ref[:, cols] = out.astype(BF16)
                kt_ref[pl.ds((h - ATT_QH) * ATT_HD, ATT_HD), :] = out.T.astype(BF16)
        v_ref[...] = v_in_ref[...].astype(BF16)
        for h in range(ATT_KVH):
            vt_ref[pl.ds(h * ATT_HD, ATT_HD), :] = v_in_ref[:, pl.ds(h * ATT_HD, ATT_HD)].T.astype(BF16)

    one = pl.BlockSpec((1, ATT_HD), lambda i: (0, 0))
    tab = pl.BlockSpec((tm, ATT_HD), lambda i: (i, 0))
    vw = ATT_KVH * ATT_HD
    tr = pl.BlockSpec((vw, tm), lambda i: (0, i))
    return pl.pallas_call(
        body, name=name, grid=(S // tm,),
        in_specs=[pl.BlockSpec((tm, W), lambda i: (i, 0)), one, one, tab, tab,
                  pl.BlockSpec((tm, vw), lambda i: (i, W // vw))],
        out_specs=[pl.BlockSpec((tm, W), lambda i: (i, 0)), pl.BlockSpec((tm, vw), lambda i: (i, 0)), tr, tr],
        out_shape=[jax.ShapeDtypeStruct((S, W), BF16), jax.ShapeDtypeStruct((S, vw), BF16),
                   jax.ShapeDtypeStruct((vw, S), BF16), jax.ShapeDtypeStruct((vw, S), BF16)],
        compiler_params=_cp(("parallel",)),
    )(proj, qn, kn, rc, rs, proj)


def _qk_prep_bwd(dqk, proj, qn, kn, rc, rs, name):
    S = proj.shape[0]
    tm = _pick(S, (256, 128))
    W = N_QK_HEADS * ATT_HD

    def body(d_ref, p_ref, qn_ref, kn_ref, rc_ref, rs_ref, dp_ref, dqn_ref, dkn_ref):
        i = pl.program_id(0)
        c, s = rc_ref[...], rs_ref[...]
        parts = [jnp.zeros((1, ATT_HD), F32), jnp.zeros((1, ATT_HD), F32)]
        for h in range(N_QK_HEADS):
            cols = pl.ds(h * ATT_HD, ATT_HD)
            w = qn_ref[...] if h < ATT_QH else kn_ref[...]
            dout = d_ref[:, cols]
            dy = dout * c + pltpu.roll(dout * s, ATT_HD // 2, 1)
            xv = p_ref[:, cols]
            r = lax.rsqrt(jnp.mean(xv * xv, axis=-1, keepdims=True) + NORM_EPS)
            xr = xv * r
            which = 0 if h < ATT_QH else 1
            parts[which] = parts[which] + jnp.sum(dy * xr, axis=0, keepdims=True)
            dxr = dy * w
            dp_ref[:, cols] = r * (dxr - xr * jnp.mean(dxr * xr, axis=-1, keepdims=True))

        @pl.when(i == 0)
        def _():
            dqn_ref[...] = parts[0]
            dkn_ref[...] = parts[1]

        @pl.when(i > 0)
        def _():
            dqn_ref[...] += parts[0]
            dkn_ref[...] += parts[1]

    one = pl.BlockSpec((1, ATT_HD), lambda i: (0, 0))
    tab = pl.BlockSpec((tm, ATT_HD), lambda i: (i, 0))
    row = pl.BlockSpec((tm, W), lambda i: (i, 0))
    return pl.pallas_call(
        body, name=name, grid=(S // tm,),
        in_specs=[row, row, one, one, tab, tab],
        out_specs=[row, one, one],
        out_shape=[jax.ShapeDtypeStruct((S, W), F32), jax.ShapeDtypeStruct((1, ATT_HD), F32),
                   jax.ShapeDtypeStruct((1, ATT_HD), F32)],
        compiler_params=_cp(("arbitrary",)),
    )(dqk, proj, qn, kn, rc, rs)


ATT_TQ = 1024
LSE_ROWS = 8


def _attn_fwd(qk, vt, name):
    S = qk.shape[0]
    tq = min(ATT_TQ, S)

    def body(q_ref, k_ref, vt_ref, o_ref, lse_ref):
        st = _dg(k_ref[...], q_ref[...], 1, 1)
        m = jnp.max(st, axis=0, keepdims=True)
        pt = jnp.exp(st - m)
        l = jnp.sum(pt, axis=0, keepdims=True)
        ot = jnp.dot(vt_ref[...], pt.astype(BF16), preferred_element_type=F32)
        o_ref[...] = (ot * (1.0 / l)).T
        lse_ref[...] = jnp.broadcast_to(m + jnp.log(l), (LSE_ROWS, tq))

    qo = pl.BlockSpec((tq, ATT_HD), lambda h, i: (i, h))
    return pl.pallas_call(
        body, name=name, grid=(ATT_QH, S // tq),
        in_specs=[qo, pl.BlockSpec((S, ATT_HD), lambda h, i: (0, ATT_QH + h // ATT_GROUP)),
                  pl.BlockSpec((ATT_HD, S), lambda h, i: (h // ATT_GROUP, 0))],
        out_specs=[qo, pl.BlockSpec((LSE_ROWS, tq), lambda h, i: (h, i))],
        out_shape=[jax.ShapeDtypeStruct((S, ATT_QH * ATT_HD), F32),
                   jax.ShapeDtypeStruct((ATT_QH * LSE_ROWS, S), F32)],
        compiler_params=_cp(("parallel", "parallel")),
    )(qk, qk, vt)


def _attn_bwd(do, o, lse, qk, v, kt, name):
    S = qk.shape[0]
    tq = min(ATT_TQ, S)
    scale = ATT_HD ** -0.5

    def body(do_ref, o_ref, lse_ref, q_ref, k_ref, v_ref, kt_ref, dq_ref, dk_ref, dv_ref):
        g = pl.program_id(1)
        i = pl.program_id(2)

        @pl.when((g == 0) & (i == 0))
        def _():
            dk_ref[...] = jnp.zeros_like(dk_ref)
            dv_ref[...] = jnp.zeros_like(dv_ref)

        q = q_ref[...]
        dov = do_ref[...]
        dob = dov.astype(BF16)
        delta = jnp.sum((dov * o_ref[...]).T, axis=0, keepdims=True)
        st = _dg(k_ref[...], q, 1, 1)
        pt = jnp.exp(st - lse_ref[0:1, :])
        dpt = _dg(v_ref[...], dob, 1, 1)
        dst = (pt * (dpt - delta)).astype(BF16)
        dv_ref[...] += jnp.dot(pt.astype(BF16), dob, preferred_element_type=F32)
        dk_ref[...] += jnp.dot(dst, q, preferred_element_type=F32)
        dq_ref[...] = jnp.dot(kt_ref[...], dst, preferred_element_type=F32).T * scale

    qo = pl.BlockSpec((tq, ATT_HD), lambda kv, g, i: (i, kv * ATT_GROUP + g))
    kvo = pl.BlockSpec((S, ATT_HD), lambda kv, g, i: (0, kv))
    return pl.pallas_call(
        body, name=name, grid=(ATT_KVH, ATT_GROUP, S // tq),
        in_specs=[qo, qo, pl.BlockSpec((LSE_ROWS, tq), lambda kv, g, i: (kv * ATT_GROUP + g, i)), qo,
                  pl.BlockSpec((S, ATT_HD), lambda kv, g, i: (0, ATT_QH + kv)), kvo,
                  pl.BlockSpec((ATT_HD, S), lambda kv, g, i: (kv, 0))],
        out_specs=[qo, kvo, kvo],
        out_shape=[jax.ShapeDtypeStruct((S, ATT_QH * ATT_HD), F32),
                   jax.ShapeDtypeStruct((S, ATT_KVH * ATT_HD), F32),
                   jax.ShapeDtypeStruct((S, ATT_KVH * ATT_HD), F32)],
        compiler_params=_cp(("parallel", "arbitrary", "arbitrary")),
    )(do, o, lse, qk, qk, v, kt)


def _adamw(w, g, m, v, name):
    rows, cols = w.shape
    tr = rows
    for cand in (512, 256, 128, 64, 32, 16, 8):
        if rows % cand == 0 and cand * cols * 4 <= 2 * 1024 * 1024:
            tr = cand
            break

    def body(w_ref, g_ref, m_ref, v_ref, d_ref, nm_ref, nv_ref):
        gv = g_ref[...]
        nm = ADAM_B1 * m_ref[...] + (1.0 - ADAM_B1) * gv
        nv = ADAM_B2 * v_ref[...] + (1.0 - ADAM_B2) * (gv * gv)
        m_hat = nm / (1.0 - ADAM_B1 ** ADAM_STEP)
        v_hat = nv / (1.0 - ADAM_B2 ** ADAM_STEP)
        d_ref[...] = -ADAM_LR * (m_hat / (jnp.sqrt(v_hat) + ADAM_EPS) + ADAM_WD * w_ref[...])
        nm_ref[...] = nm
        nv_ref[...] = nv

    blk = pl.BlockSpec((tr, cols), lambda i: (i, 0))
    return pl.pallas_call(
        body, name=name, grid=(rows // tr,),
        in_specs=[blk] * 4, out_specs=[blk] * 3,
        out_shape=[jax.ShapeDtypeStruct((rows, cols), F32)] * 3,
        compiler_params=_cp(("parallel",)),
    )(w, g, m, v)


ANY = pl.BlockSpec(memory_space=pl.ANY)


def _place():
    return lax.axis_index("x"), lax.axis_index("y"), lax.axis_index("c")


def _other_chips(x, y):
    return [(1 - x, y), (x, 1 - y), (1 - x, 1 - y)]


def _half_rows(c, H):
    return pl.ds(pl.multiple_of(c * H, 8), H)


def _allreduce_small(v, name):
    R = v.shape[0]
    n_dev = 8

    def body(v_ref, sum_ref, all_ref, token_ref, send_sems, recv_sems, local_sem):
        token_ref[...] = jnp.zeros_like(token_ref)
        x, y, c = _place()
        me, sibling = (x, y, c), (x, y, 1 - c)
        chips = _other_chips(x, y)

        def rows(px, py, pc):
            return all_ref.at[pl.ds(pl.multiple_of((4 * px + 2 * py + pc) * R, 8), R), :]

        def copy(k, block, to, src=None):
            return pltpu.make_async_remote_copy(
                src_ref=rows(*block) if src is None else src, dst_ref=rows(*block),
                send_sem=send_sems.at[k], recv_sem=recv_sems.at[k], device_id=to, device_id_type=MESH)

        own = pltpu.make_async_copy(v_ref, rows(*me), local_sem)
        own.start()
        first = [copy(0, me, sibling, src=v_ref)]
        first += [copy(1 + j, me, (*chip, c), src=v_ref) for j, chip in enumerate(chips)]
        for cp in first:
            cp.start()
        passed = [copy(4 + j, (*chip, c), sibling) for j, chip in enumerate(chips)]
        for j, chip in enumerate(chips):
            copy(1 + j, (*chip, c), me).wait_recv()
            passed[j].start()
        copy(0, sibling, me).wait_recv()
        for j, chip in enumerate(chips):
            copy(4 + j, (*chip, 1 - c), me).wait_recv()
        for cp in first + passed:
            cp.wait_send()
        own.wait()
        acc = all_ref[pl.ds(0, R), :]
        for d in range(1, n_dev):
            acc = acc + all_ref[pl.ds(d * R, R), :]
        sum_ref[...] = acc

    vm = pl.BlockSpec(memory_space=pltpu.VMEM)
    total, _, token = pl.pallas_call(
        body, name=name,
        in_specs=[vm], out_specs=[vm, vm, vm],
        out_shape=[jax.ShapeDtypeStruct((R, LANES), F32), jax.ShapeDtypeStruct((n_dev * R, LANES), F32),
                   jax.ShapeDtypeStruct((8, LANES), F32)],
        scratch_shapes=[pltpu.SemaphoreType.DMA((7,)), pltpu.SemaphoreType.DMA((7,)), pltpu.SemaphoreType.DMA],
    )(v)
    return total, token


def _swap_other_half(bufs, name):
    n = len(bufs)
    halves = [b.shape[1] // 2 for b in bufs]

    def body(*refs):
        g_refs, got_refs = refs[:n], refs[n:2 * n]
        send_sems, recv_sems = refs[2 * n:]
        x, y, c = _place()
        copies = [pltpu.make_async_remote_copy(
            src_ref=g_refs[k].at[p, _half_rows(1 - c, halves[k])], dst_ref=got_refs[k].at[p],
            send_sem=send_sems.at[N_CHIPS * k + p], recv_sem=recv_sems.at[N_CHIPS * k + p],
            device_id=(x, y, 1 - c), device_id_type=MESH) for k in range(n) for p in range(N_CHIPS)]
        for cp in copies:
            cp.start()
        for cp in copies:
            cp.wait_recv()
        for cp in copies:
            cp.wait_send()

    return pl.pallas_call(
        body, name=name, in_specs=[ANY] * n, out_specs=[ANY] * n,
        out_shape=[jax.ShapeDtypeStruct((N_CHIPS, h, b.shape[2]), b.dtype) for b, h in zip(bufs, halves)],
        scratch_shapes=[pltpu.SemaphoreType.DMA((N_CHIPS * n,)), pltpu.SemaphoreType.DMA((N_CHIPS * n,))],
    )(*bufs)


def _join_halves(bufs, name):
    n = len(bufs)
    halves = [b.shape[0] // 2 for b in bufs]

    def body(*refs):
        outs = refs[n:2 * n]
        send_sems, recv_sems = refs[2 * n:]
        x, y, c = _place()

        def copy(k, core):
            blk = outs[k].at[_half_rows(core, halves[k])]
            return pltpu.make_async_remote_copy(src_ref=blk, dst_ref=blk, send_sem=send_sems.at[k],
                                                recv_sem=recv_sems.at[k], device_id=(x, y, 1 - c),
                                                device_id_type=MESH)

        sends = [copy(k, c) for k in range(n)]
        for cp in sends:
            cp.start()
        for k in range(n):
            copy(k, 1 - c).wait_recv()
        for cp in sends:
            cp.wait_send()

    return pl.pallas_call(
        body, name=name, in_specs=[ANY] * n, out_specs=[ANY] * n,
        out_shape=[jax.ShapeDtypeStruct(b.shape, b.dtype) for b in bufs],
        input_output_aliases={k: k for k in range(n)},
        scratch_shapes=[pltpu.SemaphoreType.DMA((n,)), pltpu.SemaphoreType.DMA((n,))],
    )(*bufs)


def _rs_rows(H, width):
    for cand in (1024, 512, 256, 128, 64, 32, 16):
        if H % cand == 0 and cand * width * 4 <= 1536 * 1024:
            return cand
    return H


def _add_sibling(g, got, c, me, name):
    _, H, width = got.shape
    tb = _rs_rows(H, width)
    nb = H // tb

    def body(sp_ref, g_ref, got_ref, sb_ref, sf_ref):
        p = pl.program_id(1)
        s = g_ref[0] + got_ref[0]
        sb_ref[0] = s.astype(BF16)

        @pl.when(p == sp_ref[1])
        def _():
            sf_ref[...] = s

    grid_spec = pltpu.PrefetchScalarGridSpec(
        num_scalar_prefetch=1, grid=(nb, N_CHIPS),
        in_specs=[pl.BlockSpec((1, tb, width), lambda i, p, sp: (p, sp[0] * nb + i, 0)),
                  pl.BlockSpec((1, tb, width), lambda i, p, sp: (p, i, 0))],
        out_specs=[pl.BlockSpec((1, tb, width), lambda i, p, sp: (p, i, 0)),
                   pl.BlockSpec((tb, width), lambda i, p, sp: (i, 0))])
    return pl.pallas_call(
        body, name=name, grid_spec=grid_spec,
        out_shape=[jax.ShapeDtypeStruct((N_CHIPS, H, width), BF16), jax.ShapeDtypeStruct((H, width), F32)],
        compiler_params=_cp(("arbitrary", "arbitrary")),
    )(jnp.stack([c, me]).astype(jnp.int32), g, got)


def _add_chips(sf, got, others_and_c, name):
    H, width = sf.shape
    tb = _rs_rows(H, width)
    nb = H // tb

    def body(sp_ref, sf_ref, r1_ref, r2_ref, r3_ref, out_ref):
        out_ref[...] = ((sf_ref[...] + r1_ref[0].astype(F32)) + r2_ref[0].astype(F32)) + r3_ref[0].astype(F32)

    def slot(k):
        return pl.BlockSpec((1, tb, width), lambda i, sp: (sp[k], i, 0))

    blk = pl.BlockSpec((tb, width), lambda i, sp: (i, 0))
    grid_spec = pltpu.PrefetchScalarGridSpec(
        num_scalar_prefetch=1, grid=(nb,), in_specs=[blk, slot(0), slot(1), slot(2)],
        out_specs=pl.BlockSpec((tb, width), lambda i, sp: (sp[3] * nb + i, 0)))
    return pl.pallas_call(
        body, name=name, grid_spec=grid_spec,
        out_shape=jax.ShapeDtypeStruct((2 * H, width), F32),
        compiler_params=_cp(("arbitrary",)),
    )(others_and_c.astype(jnp.int32), sf, got, got, got)


REPLICATED = ("norm_mix", "norm_ffn", "gla_b_gate_f", "gla_b_gate_b", "gla_norm", "attn_q_norm", "attn_k_norm",
              "ffn_b_conv")


PIECE_ROWS = 16


def _piece_rows(shape):
    n = 1
    for s in shape:
        n *= s
    rows = n // LANES
    return rows, -(-rows // PIECE_ROWS) * PIECE_ROWS


def _pack(pieces, dtype, row_multiple):
    flat = []
    for p in pieces:
        rows, padded = _piece_rows(p.shape)
        flat.append(jnp.pad(p.astype(dtype).reshape(rows, LANES), ((0, padded - rows), (0, 0))))
    rows = sum(f.shape[0] for f in flat)
    padded = -(-rows // row_multiple) * row_multiple
    if padded > rows:
        flat.append(jnp.zeros((padded - rows, LANES), dtype))
    return jnp.concatenate(flat, axis=0)


def _unpack(buf, shapes):
    out, r = [], 0
    for shp in shapes:
        rows, padded = _piece_rows(shp)
        out.append(buf[r:r + rows].reshape(shp))
        r += padded
    return out


def _own_slot(shard2d, me):
    return lax.dynamic_update_index_in_dim(lax.empty((N_CHIPS,) + shard2d.shape, shard2d.dtype), shard2d, me, 0)


def _layer_small(w, l):
    j = l // 2
    if l % 2 == 0:
        return [w["gla_w_gate_up_f"][j], w["gla_w_gate_up_b"][j], w["ffn_w_conv"][l]]
    return [w["ffn_w_conv"][l]]


def _layer_weight_bufs(w, l, me):
    j = l // 2
    mixer = ("gla_w_in", "gla_w_out") if l % 2 == 0 else ("attn_w_qkv", "attn_w_out")
    bufs = [_own_slot(w[n][j].astype(BF16), me) for n in mixer]
    bufs.append(_own_slot(_pack(_layer_small(w, l), F32, 32), me))
    bufs += [_own_slot(w["ffn_w_up"][l].astype(BF16), me), _own_slot(w["ffn_w_down"][l].astype(BF16), me)]
    return bufs


N_MIXER_BUFS = 3


def _layer_weights(w, l, got):
    rows = lambda t: t.reshape(-1, t.shape[2])
    cols = lambda t: jnp.concatenate([t[p] for p in range(N_CHIPS)], axis=1)
    out = {}
    if len(got) != N_MIXER_BUFS:
        up, down = got[-2:]
        out.update(up=up, up_full=cols(up), down=rows(down))
    if len(got) != 2:
        mix_in, mix_out, small = got[:N_MIXER_BUFS]
        shapes = [t.shape for t in _layer_small(w, l)]
        parts = [_unpack(small[p], shapes) for p in range(N_CHIPS)]
        full_small = [jnp.concatenate([parts[p][k] for p in range(N_CHIPS)], axis=-1) for k in range(len(shapes))]
        out.update(conv=full_small[-1])
        if l % 2 == 0:
            out.update(gla_in=jnp.pad(cols(mix_in), ((0, 0), (0, GLA_IN_PAD - GLA_IN))), gla_out=rows(mix_out),
                       gate_f=full_small[0], gate_b=full_small[1])
        else:
            out.update(qkv=mix_in, attn_out=rows(mix_out))
    return out


HBM = pl.BlockSpec(memory_space=pltpu.HBM)
SEM = pl.BlockSpec(memory_space=pltpu.SEMAPHORE)
SIDE_EFFECT = pltpu.SideEffectType.DATAFLOW_SIDE_EFFECTING


def _gather_start(bufs, after, name):
    n = len(bufs)
    halves = [b.shape[1] // 2 for b in bufs]

    def body(*refs):
        refs = refs[:n] + refs[n + 1:]
        send_sems, recv_sems = refs[n:2 * n], refs[2 * n:3 * n]
        outs, token = refs[3 * n:4 * n], refs[4 * n]
        x, y, c = _place()
        me = 2 * x + y
        for k in range(n):
            blk = outs[k].at[me, _half_rows(c, halves[k])]
            for px, py in _other_chips(x, y):
                pltpu.make_async_remote_copy(src_ref=blk, dst_ref=blk, send_sem=send_sems[k], recv_sem=recv_sems[k],
                                             device_id=(px, py, c), device_id_type=MESH).start()
        token[...] = jnp.zeros_like(token)

    res = pl.pallas_call(
        body, name=name,
        in_specs=[HBM] * n + [ANY],
        out_specs=[SEM] * (2 * n) + [HBM] * n + [pl.BlockSpec(memory_space=pltpu.VMEM)],
        out_shape=[pltpu.SemaphoreType.DMA(())] * (2 * n) + [pltpu.HBM(b.shape, b.dtype) for b in bufs]
        + [jax.ShapeDtypeStruct((8, LANES), F32)],
        input_output_aliases={k: 2 * n + k for k in range(n)},
        compiler_params=pltpu.CompilerParams(has_side_effects=SIDE_EFFECT),
    )(*[pltpu.with_memory_space_constraint(b, pltpu.HBM) for b in bufs], after)
    return res[:n], res[n:2 * n], res[2 * n:3 * n], res[3 * n]


def _gather_wait(send_sems, recv_sems, thru, after, name):
    n = len(thru)
    halves = [b.shape[1] // 2 for b in thru]

    def body(*refs):
        ss, rs = refs[n:2 * n], refs[2 * n:3 * n]
        outs = refs[3 * n + 1:]
        x, y, c = _place()
        for k in range(n):
            three = outs[k].at[pl.ds(0, N_CHIPS - 1), _half_rows(c, halves[k])]
            cp = pltpu.make_async_remote_copy(src_ref=three, dst_ref=three, send_sem=ss[k], recv_sem=rs[k],
                                              device_id=(x, y, c), device_id_type=MESH)
            cp.wait_send()
            cp.wait_recv()

    return pl.pallas_call(
        body, name=name,
        in_specs=[HBM] * n + [SEM] * (2 * n) + [ANY],
        out_specs=[HBM] * n,
        out_shape=[pltpu.HBM(b.shape, b.dtype) for b in thru],
        input_output_aliases={k: k for k in range(n)},
        compiler_params=pltpu.CompilerParams(has_side_effects=SIDE_EFFECT),
    )(*thru, *send_sems, *recv_sems, after)


def _send_start(sbs, name):
    n = len(sbs)

    def body(*refs):
        send_sems, recv_sems = refs[2 * n:3 * n], refs[3 * n:4 * n]
        srcs, lands, token = refs[4 * n:5 * n], refs[5 * n:6 * n], refs[6 * n]
        x, y, c = _place()
        me = 2 * x + y
        for k in range(n):
            for px, py in _other_chips(x, y):
                pltpu.make_async_remote_copy(src_ref=srcs[k].at[2 * px + py], dst_ref=lands[k].at[me],
                                             send_sem=send_sems[k], recv_sem=recv_sems[k],
                                             device_id=(px, py, c), device_id_type=MESH).start()
        token[...] = jnp.zeros_like(token)

    hbm = lambda a: pltpu.with_memory_space_constraint(a, pltpu.HBM)
    res = pl.pallas_call(
        body, name=name,
        in_specs=[HBM] * (2 * n),
        out_specs=[SEM] * (2 * n) + [HBM] * (2 * n) + [pl.BlockSpec(memory_space=pltpu.VMEM)],
        out_shape=[pltpu.SemaphoreType.DMA(())] * (2 * n) + [pltpu.HBM(s.shape, s.dtype) for s in sbs] * 2
        + [jax.ShapeDtypeStruct((8, LANES), F32)],
        input_output_aliases={k: 2 * n + k for k in range(2 * n)},
        compiler_params=pltpu.CompilerParams(has_side_effects=SIDE_EFFECT),
    )(*[hbm(s) for s in sbs], *[hbm(lax.empty(s.shape, s.dtype)) for s in sbs])
    return res[:n], res[n:2 * n], res[2 * n:3 * n], res[3 * n:4 * n], res[4 * n]


def _send_wait(send_sems, recv_sems, srcs, lands, after, name):
    n = len(srcs)

    def body(*refs):
        ss, rs = refs[2 * n:3 * n], refs[3 * n:4 * n]
        s_out, l_out = refs[4 * n + 1:5 * n + 1], refs[5 * n + 1:]
        x, y, c = _place()
        for k in range(n):
            cp = pltpu.make_async_remote_copy(src_ref=s_out[k].at[pl.ds(0, N_CHIPS - 1)],
                                              dst_ref=l_out[k].at[pl.ds(0, N_CHIPS - 1)], send_sem=ss[k],
                                              recv_sem=rs[k], device_id=(x, y, c), device_id_type=MESH)
            cp.wait_send()
            cp.wait_recv()

    res = pl.pallas_call(
        body, name=name,
        in_specs=[HBM] * (2 * n) + [SEM] * (2 * n) + [ANY],
        out_specs=[HBM] * (2 * n),
        out_shape=[pltpu.HBM(s.shape, s.dtype) for s in srcs] * 2,
        input_output_aliases={k: k for k in range(2 * n)},
        compiler_params=pltpu.CompilerParams(has_side_effects=SIDE_EFFECT),
    )(*srcs, *lands, *send_sems, *recv_sems, after)
    return res[n:]


def _pass_to_sibling(bufs, name):
    n = len(bufs)
    halves = [b.shape[1] // 2 for b in bufs]

    def body(*refs):
        outs = refs[n:2 * n]
        send_sems, recv_sems = refs[2 * n:]
        x, y, c = _place()
        chips = _other_chips(x, y)

        def copy(k, j, core):
            px, py = chips[j]
            blk = outs[k].at[2 * px + py, _half_rows(core, halves[k])]
            return pltpu.make_async_remote_copy(src_ref=blk, dst_ref=blk, send_sem=send_sems.at[3 * k + j],
                                                recv_sem=recv_sems.at[3 * k + j], device_id=(x, y, 1 - c),
                                                device_id_type=MESH)

        sends = [copy(k, j, c) for k in range(n) for j in range(3)]
        for cp in sends:
            cp.start()
        for k in range(n):
            for j in range(3):
                copy(k, j, 1 - c).wait_recv()
        for cp in sends:
            cp.wait_send()

    return pl.pallas_call(
        body, name=name,
        in_specs=[ANY] * n, out_specs=[ANY] * n,
        out_shape=[jax.ShapeDtypeStruct(b.shape, b.dtype) for b in bufs],
        input_output_aliases={k: k for k in range(n)},
        scratch_shapes=[pltpu.SemaphoreType.DMA((3 * n,)), pltpu.SemaphoreType.DMA((3 * n,))],
    )(*bufs)


def _rope_tables(S):
    rows = S // GRID_W
    row_idx = jnp.repeat(jnp.arange(rows, dtype=F32), GRID_W)
    col_idx = jnp.tile(jnp.arange(GRID_W, dtype=F32), rows)
    pairs = ATT_HD // 4
    inv_freq = ROPE_THETA ** (-jnp.arange(pairs, dtype=F32) / pairs)
    ang = jnp.concatenate([row_idx[:, None] * inv_freq, col_idx[:, None] * inv_freq], axis=-1)
    cos, sin = jnp.cos(ang), jnp.sin(ang)
    return jnp.concatenate([cos, cos], axis=-1), jnp.concatenate([-sin, sin], axis=-1)


def _gate_rows(w, first_row):
    return jnp.zeros((LANES, GLA_KEY), F32).at[first_row:first_row + GLA_RANK].set(w.astype(F32))


def _local_step(x, target, weights_of, grads_out, P):
    S = x.shape[0]
    rc, rs = _rope_tables(S)
    row = lambda a: a.reshape(1, -1)
    saved = []
    for i in range(DEPTH):
        j = i // 2
        W = dict(weights_of(i, "mix", x))
        nm = row(P["norm_mix"][i])
        h1 = _rmsnorm_fwd(x, nm, f"norm_mix_fwd{i}")
        if i % 2 == 0:
            wgf = _gate_rows(W["gate_f"], 0)
            wgb = _gate_rows(W["gate_b"], GLA_RANK)
            bgf, bgb = row(P["gla_b_gate_f"][j]), row(P["gla_b_gate_b"][j])
            gn = row(P["gla_norm"][j])
            proj = _matmul_rows(h1, W["gla_in"], f"gla_in{i}")
            laf, lab = _gla_gate_fwd(proj, wgf, bgf, wgb, bgb, f"gla_gate_fwd{i}")
            of, stf = _gla_scan_fwd(proj, laf, False, f"gla_scan_f_fwd{i}")
            ob, stb = _gla_scan_fwd(proj, lab, True, f"gla_scan_b_fwd{i}")
            z = _gla_out_fwd(of, ob, proj, gn, f"gla_out_fwd{i}")
            xm = _matmul_rows(z, W["gla_out"], f"gla_outproj{i}", res=x)
            mix = dict(proj=proj, laf=laf, lab=lab, of=of, ob=ob, stf=stf, stb=stb, z=z, wgf=wgf, wgb=wgb)
        else:
            proj = _matmul_rows(h1, W["qkv"], f"attn_qkv{i}", w_layer=0)
            qn, kn = row(P["attn_q_norm"][j]), row(P["attn_k_norm"][j])
            qk, vb, kt, vt = _qk_prep_fwd(proj, qn, kn, rc, rs, f"qk_prep_fwd{i}")
            o, lse = _attn_fwd(qk, vt, f"attn_fwd{i}")
            xm = _matmul_rows(o, W["attn_out"], f"attn_outproj{i}", res=x)
            mix = dict(proj=proj, qk=qk, vb=vb, kt=kt, o=o, lse=lse)
        W.update(weights_of(i, "ffn", xm))
        h2 = _rmsnorm_fwd(xm, row(P["norm_ffn"][i]), f"norm_ffn_fwd{i}")
        a, uv, ug = _ffn_mid_fwd(h2, W["up_full"], W["conv"], row(P["ffn_b_conv"][i]), f"ffn_mid_fwd{i}")
        xo = _matmul_rows(a, W["down"], f"ffn_down{i}", res=xm)
        saved.append(dict(x=x, h1=h1, xm=xm, h2=h2, uv=uv, ug=ug, mix=mix, W=W))
        x = xo

    dx, dxb, loss = _loss_grad(x, target, "loss")

    G = {n: [None] * (DEPTH if n.startswith(("norm", "ffn")) else DEPTH // 2) for n in REPLICATED}
    token = None
    for i in reversed(range(DEPTH)):
        j = i // 2
        sv = saved[i]
        mix = sv["mix"]
        W = sv["W"]
        bconv = row(P["ffn_b_conv"][i])
        if token is not None:
            t = token[0:1, 0:1]
            bconv = jnp.where(t == 0.0, bconv, t)
        duv, dug, a, gwv, gwg = _ffn_mid_bwd(dxb, W["down"], sv["uv"], sv["ug"], W["conv"], bconv, f"ffn_mid_bwd{i}")
        L = dict(down=_wgrad(a, dxb, f"ffn_down_wgrad{i}", chips="rows"),
                 up=_wgrad(sv["h2"], (duv, dug), f"ffn_up_wgrad{i}", chips="cols"),
                 small=[jnp.concatenate([gwv[:3], gwg[:3]], axis=1)])
        G["ffn_b_conv"][i] = jnp.concatenate([gwv[3], gwg[3]], axis=0)
        dxm, dxmb, dn = _dgrad_norm((duv, dug), W["up"], sv["xm"], row(P["norm_ffn"][i]), dx, f"ffn_up_dgrad{i}",
                                    w_layer=0)
        G["norm_ffn"][i] = dn[0]
        if i % 2 == 0:
            proj = mix["proj"]
            bgf, bgb = row(P["gla_b_gate_f"][j]), row(P["gla_b_gate_b"][j])
            gn = row(P["gla_norm"][j])
            dz = _matmul_rows(dxmb, W["gla_out"], f"gla_outproj_dgrad{i}", transposed=True)
            L["out"] = _wgrad(mix["z"], dxmb, f"gla_outproj_wgrad{i}", chips="rows")
            do, dg, dgn = _gla_out_bwd(dz, mix["of"], mix["ob"], proj, gn, f"gla_out_bwd{i}")
            G["gla_norm"][j] = dgn[0]
            dqf, dkf, dvf, dlaf = _gla_scan_bwd(do, proj, mix["laf"], mix["stf"], False, f"gla_scan_f_bwd{i}")
            dqb, dkb, dvb, dlab = _gla_scan_bwd(do, proj, mix["lab"], mix["stb"], True, f"gla_scan_b_bwd{i}")
            dr, dwf, dbf, dwb, dbb = _gla_gate_bwd(dlaf, dlab, proj, mix["wgf"], bgf, mix["wgb"], bgb,
                                                   f"gla_gate_bwd{i}")
            L["small"] = [dwf[:GLA_RANK], dwb[GLA_RANK:2 * GLA_RANK]] + L["small"]
            G["gla_b_gate_f"][j] = dbf[0]
            G["gla_b_gate_b"][j] = dbb[0]
            dproj = jnp.concatenate([dqf + dqb, dkf + dkb, dvf + dvb, dg, dr], axis=1).astype(BF16)
            L["mix_in"] = _wgrad(sv["h1"], dproj, f"gla_in_wgrad{i}")
            dx, dxb, dn = _dgrad_norm(dproj, W["gla_in"], sv["x"], row(P["norm_mix"][i]), dxm, f"mix_in_dgrad{i}")
        else:
            proj = mix["proj"]
            qn, kn = row(P["attn_q_norm"][j]), row(P["attn_k_norm"][j])
            do = _matmul_rows(dxmb, W["attn_out"], f"attn_outproj_dgrad{i}", transposed=True)
            L["out"] = _wgrad(mix["o"], dxmb, f"attn_outproj_wgrad{i}", chips="rows")
            dq, dk, dv = _attn_bwd(do, mix["o"], mix["lse"], mix["qk"], mix["vb"], mix["kt"], f"attn_bwd{i}")
            dqk = jnp.concatenate([dq, dk], axis=1)
            dpqk, dqn, dkn = _qk_prep_bwd(dqk, proj, qn, kn, rc, rs, f"qk_prep_bwd{i}")
            G["attn_q_norm"][j] = dqn[0]
            G["attn_k_norm"][j] = dkn[0]
            dproj = jnp.concatenate([dpqk, dv], axis=1).astype(BF16)
            L["mix_in"] = _wgrad(sv["h1"], dproj, f"attn_qkv_wgrad{i}", chips="cols")
            dx, dxb, dn = _dgrad_norm(dproj, W["qkv"], sv["x"], row(P["norm_mix"][i]), dxm, f"mix_in_dgrad{i}",
                                      w_layer=0)
        G["norm_mix"][i] = dn[0]
        token = grads_out(i, L, G, loss)
    return loss, dx, G


def kernel(x, norm_mix, norm_ffn, gla_w_in, gla_w_gate_up_f, gla_b_gate_f, gla_w_gate_up_b, gla_b_gate_b, gla_norm, gla_w_out, attn_w_qkv, attn_q_norm, attn_k_norm, attn_w_out, ffn_w_up, ffn_w_conv, ffn_b_conv, ffn_w_down, loss_target, m_norm_mix, m_norm_ffn, m_gla_w_in, m_gla_w_gate_up_f, m_gla_b_gate_f, m_gla_w_gate_up_b, m_gla_b_gate_b, m_gla_norm, m_gla_w_out, m_attn_w_qkv, m_attn_q_norm, m_attn_k_norm, m_attn_w_out, m_ffn_w_up, m_ffn_w_conv, m_ffn_b_conv, m_ffn_w_down, v_norm_mix, v_norm_ffn, v_gla_w_in, v_gla_w_gate_up_f, v_gla_b_gate_f, v_gla_w_gate_up_b, v_gla_b_gate_b, v_gla_norm, v_gla_w_out, v_attn_w_qkv, v_attn_q_norm, v_attn_k_norm, v_attn_w_out, v_ffn_w_up, v_ffn_w_conv, v_ffn_b_conv, v_ffn_w_down):
    names = ("norm_mix", "norm_ffn", "gla_w_in", "gla_w_gate_up_f", "gla_b_gate_f", "gla_w_gate_up_b",
             "gla_b_gate_b", "gla_norm", "gla_w_out", "attn_w_qkv", "attn_q_norm", "attn_k_norm", "attn_w_out",
             "ffn_w_up", "ffn_w_conv", "ffn_b_conv", "ffn_w_down")
    w = dict(zip(names, (norm_mix, norm_ffn, gla_w_in, gla_w_gate_up_f, gla_b_gate_f, gla_w_gate_up_b,
                         gla_b_gate_b, gla_norm, gla_w_out, attn_w_qkv, attn_q_norm, attn_k_norm, attn_w_out,
                         ffn_w_up, ffn_w_conv, ffn_b_conv, ffn_w_down)))
    m = dict(zip(names, (m_norm_mix, m_norm_ffn, m_gla_w_in, m_gla_w_gate_up_f, m_gla_b_gate_f,
                         m_gla_w_gate_up_b, m_gla_b_gate_b, m_gla_norm, m_gla_w_out, m_attn_w_qkv, m_attn_q_norm,
                         m_attn_k_norm, m_attn_w_out, m_ffn_w_up, m_ffn_w_conv, m_ffn_b_conv, m_ffn_w_down)))
    v = dict(zip(names, (v_norm_mix, v_norm_ffn, v_gla_w_in, v_gla_w_gate_up_f, v_gla_b_gate_f,
                         v_gla_w_gate_up_b, v_gla_b_gate_b, v_gla_norm, v_gla_w_out, v_attn_w_qkv, v_attn_q_norm,
                         v_attn_k_norm, v_attn_w_out, v_ffn_w_up, v_ffn_w_conv, v_ffn_b_conv, v_ffn_w_down)))
    px, py, pc = _place()
    me = 2 * px + py

    started, token = [], w["norm_mix"]
    for l in range(DEPTH):
        started.append(_gather_start(_layer_weight_bufs(w, l, me), token, f"gather_start{l}"))
        token = started[-1][3]
    fetched = {}

    def weights_of(l, part, after):
        send_sems, recv_sems, thru, _ = started[l]
        if l == 0:
            pick = slice(0, N_MIXER_BUFS) if part == "mix" else slice(N_MIXER_BUFS, None)
            landed = _gather_wait(send_sems[pick], recv_sems[pick], thru[pick], token if part == "mix" else after,
                                  f"gather_wait{l}_{part}")
            return _layer_weights(w, l, _pass_to_sibling(landed, f"gather_pass{l}_{part}"))
        if part == "mix":
            landed = _gather_wait(send_sems, recv_sems, thru, after, f"gather_wait{l}")
            fetched[l] = _layer_weights(w, l, _pass_to_sibling(landed, f"gather_pass{l}"))
        return fetched[l]

    sent = {}

    def grads_out(l, L, G, loss_part):
        mix_in = L["mix_in"]
        if l % 2 == 0:
            width = w["gla_w_in"].shape[2]
            mix_in = jnp.stack([mix_in[:, p * width:(p + 1) * width] for p in range(N_CHIPS)])
        cut = lambda t, p: lax.slice_in_dim(t, p * (t.shape[-1] // N_CHIPS), (p + 1) * (t.shape[-1] // N_CHIPS),
                                            axis=t.ndim - 1)
        small = jnp.stack([_pack([cut(t, p) for t in L["small"]], F32, 32) for p in range(N_CHIPS)])
        if l == 0:
            packed = _pack([jnp.stack(G[n]) for n in REPLICATED] + [loss_part], F32, 16)
            sent["small_sum"], tok = _allreduce_small(packed, "small_allreduce")
            small = jnp.where(tok[0:1, 0:1] == 0.0, small, tok[0:1, 0:1])
        bufs = [mix_in, L["out"], small, L["up"], L["down"]]
        gots = _swap_other_half(bufs, f"grads{l}_to_sibling")
        sums = [_add_sibling(b, g, pc, me, f"grads{l}_add_sibling{k}") for k, (b, g) in enumerate(zip(bufs, gots))]
        send_sems, recv_sems, srcs, lands, tok = _send_start([s[0] for s in sums], f"grads{l}_start")
        sent[l] = (send_sems, recv_sems, srcs, lands, [s[1] for s in sums])
        sent["last_token"] = tok
        return tok

    P = {n: w[n] for n in REPLICATED}

    loss_part, dx, grads = _local_step(x[0], loss_target[0], weights_of, grads_out, P)

    small_sum = sent["small_sum"]
    others_and_c = jnp.stack([jnp.where(me <= k, k + 1, k) for k in range(N_CHIPS - 1)] + [pc])
    mine, after = {}, sent["last_token"]
    for l in reversed(range(DEPTH)):
        send_sems, recv_sems, srcs, lands, own = sent[l]
        landed = _send_wait(send_sems, recv_sems, srcs, lands, after, f"grads{l}_wait")
        halves = [_add_chips(own[k], landed[k], others_and_c, f"grads{l}_add_chips{k}") for k in range(len(own))]
        mine[l] = _join_halves(halves, f"grads{l}_join_halves")
        after = mine[l][0]
    gsh = {}
    for n, k, layers in (("ffn_w_up", 3, range(DEPTH)), ("ffn_w_down", 4, range(DEPTH)),
                         ("gla_w_in", 0, range(0, DEPTH, 2)), ("gla_w_out", 1, range(0, DEPTH, 2)),
                         ("attn_w_qkv", 0, range(1, DEPTH, 2)), ("attn_w_out", 1, range(1, DEPTH, 2))):
        gsh[n] = jnp.stack([mine[l][k] for l in layers])
    small_mine = [_unpack(mine[l][2], [t.shape for t in _layer_small(w, l)]) for l in range(DEPTH)]
    gsh["ffn_w_conv"] = jnp.stack([small_mine[l][-1] for l in range(DEPTH)])
    gsh["gla_w_gate_up_f"] = jnp.stack([small_mine[l][0] for l in range(0, DEPTH, 2)])
    gsh["gla_w_gate_up_b"] = jnp.stack([small_mine[l][1] for l in range(0, DEPTH, 2)])

    parts = _unpack(small_sum, [w[n].shape for n in REPLICATED] + [(1, LANES)])
    gsh.update(dict(zip(REPLICATED, parts[:-1])))
    loss = parts[-1][0, 0]

    delta, new_m, new_v = {}, {}, {}
    for n in names:
        shp = w[n].shape
        two_d = (-1, shp[-1])
        d, nm, nv = _adamw(w[n].reshape(two_d), gsh[n].reshape(two_d), m[n].reshape(two_d), v[n].reshape(two_d),
                           f"adamw_{n}")
        delta[n], new_m[n], new_v[n] = d.reshape(shp), nm.reshape(shp), nv.reshape(shp)

    return (loss, dx[None], *[gsh[n] for n in names], *[delta[n] for n in names],
            *[new_m[n] for n in names], *[new_v[n] for n in names])
```

```python
import jax
import jax.numpy as jnp
from jax import lax
from jax.experimental import pallas as pl
from jax.experimental.pallas import tpu as pltpu

F32 = jnp.float32
BF16 = jnp.bfloat16
MESH = pl.DeviceIdType.MESH
HIGHEST = lax.Precision.HIGHEST

D_MODEL = 1024
DEPTH = 4
GRID_W = 64
NORM_EPS = 1e-6
GLA_HEADS = 4
GLA_DK = 128
GLA_DV = 256
GLA_KEY = GLA_HEADS * GLA_DK
GLA_VAL = GLA_HEADS * GLA_DV
GLA_RANK = 16
GLA_CHUNK = 64
GLA_GATE_NORMALIZER = 16.0
GLA_IN = 2 * GLA_KEY + 2 * GLA_VAL + 2 * GLA_RANK
GLA_IN_PAD = 3200
GLA_R_BLOCK = (2 * GLA_KEY + 2 * GLA_VAL) // 128
ATT_HD = 128
ATT_QH = 8
ATT_KVH = 2
ATT_GROUP = ATT_QH // ATT_KVH
ATT_QKV = (ATT_QH + 2 * ATT_KVH) * ATT_HD
ROPE_THETA = 10000.0
D_FF = 2816
ADAM_LR = 0.001
ADAM_B1 = 0.9
ADAM_B2 = 0.999
ADAM_EPS = 1e-08
ADAM_WD = 0.01
ADAM_STEP = 10

N_CHIPS = 4
LANES = 128
VMEM_LIMIT = 56 * 1024 * 1024


def _cp(sem):
    return pltpu.CompilerParams(dimension_semantics=sem, vmem_limit_bytes=VMEM_LIMIT)


def _pick(n, cands):
    for c in cands:
        if n % c == 0:
            return c
    return n


def _dg(a, b, ca, cb):
    return lax.dot_general(a, b, (((ca,), (cb,)), ((), ())), preferred_element_type=F32)


def _sigmoid(x):
    return 0.5 * jnp.tanh(0.5 * x) + 0.5


def _rmsnorm_fwd(x, w, name):
    S, D = x.shape
    tm = _pick(S, (512, 256))

    def body(x_ref, w_ref, h_ref):
        xv = x_ref[...]
        r = lax.rsqrt(jnp.mean(xv * xv, axis=-1, keepdims=True) + NORM_EPS)
        h_ref[...] = (xv * r * w_ref[...]).astype(BF16)

    return pl.pallas_call(
        body, name=name, grid=(S // tm,),
        in_specs=[pl.BlockSpec((tm, D), lambda i: (i, 0)), pl.BlockSpec((1, D), lambda i: (0, 0))],
        out_specs=pl.BlockSpec((tm, D), lambda i: (i, 0)),
        out_shape=jax.ShapeDtypeStruct((S, D), BF16),
        compiler_params=_cp(("parallel",)),
    )(x, w)


def _loss_grad(y, t, name):
    S, D = y.shape
    tm = _pick(S, (512, 256))

    def body(y_ref, t_ref, dy_ref, dyb_ref, loss_ref):
        i = pl.program_id(0)
        d = y_ref[...] - t_ref[...]
        dy = d * (1.0 / D)
        dy_ref[...] = dy
        dyb_ref[...] = dy.astype(BF16)
        sq = jnp.sum(jnp.sum(d * d, axis=1, keepdims=True), axis=0, keepdims=True)
        part = jnp.broadcast_to(sq * (0.5 / D), (1, LANES))

        @pl.when(i == 0)
        def _():
            loss_ref[...] = part

        @pl.when(i > 0)
        def _():
            loss_ref[...] += part

    return pl.pallas_call(
        body, name=name, grid=(S // tm,),
        in_specs=[pl.BlockSpec((tm, D), lambda i: (i, 0)), pl.BlockSpec((tm, D), lambda i: (i, 0))],
        out_specs=[pl.BlockSpec((tm, D), lambda i: (i, 0)), pl.BlockSpec((tm, D), lambda i: (i, 0)),
                   pl.BlockSpec((1, LANES), lambda i: (0, 0))],
        out_shape=[jax.ShapeDtypeStruct((S, D), F32), jax.ShapeDtypeStruct((S, D), BF16),
                   jax.ShapeDtypeStruct((1, LANES), F32)],
        compiler_params=_cp(("arbitrary",)),
    )(y, t)


WGRAD_TK = 512


def _wgrad(a, b, name, chips=None):
    S, Kw = a.shape
    pair = isinstance(b, (tuple, list))
    tn = b[0].shape[1] if pair else b.shape[1]
    N = 2 * tn if pair else tn
    tk = _pick(S, (WGRAD_TK, 256, 128))
    nk = S // tk
    if pair:
        b_specs = [pl.BlockSpec((tk, tn), lambda j, k: (jnp.where(j == 0, k, nk - 1), 0)),
                   pl.BlockSpec((tk, tn), lambda j, k: (jnp.where(j == 1, k, 0), 0))]
    else:
        b_specs = [pl.BlockSpec((tk, tn), lambda j, k: (k, 0))]
    if chips == "cols":
        cw = N // N_CHIPS
        span = tn // cw
        o_spec = pl.BlockSpec((span, Kw, cw), lambda j, k: (j, 0, 0))
        out_shape = jax.ShapeDtypeStruct((N_CHIPS, Kw, cw), F32)
    elif chips == "rows":
        assert not pair
        o_spec = pl.BlockSpec((N_CHIPS, Kw // N_CHIPS, N), lambda j, k: (0, 0, 0))
        out_shape = jax.ShapeDtypeStruct((N_CHIPS, Kw // N_CHIPS, N), F32)
    else:
        assert not pair
        o_spec = pl.BlockSpec((Kw, N), lambda j, k: (0, 0))
        out_shape = jax.ShapeDtypeStruct((Kw, N), F32)
    nb = len(b_specs)

    def body(*refs):
        a_ref, b_refs, o_ref, acc = refs[0], refs[1:1 + nb], refs[-2], refs[-1]
        j = pl.program_id(0)
        k = pl.program_id(1)

        @pl.when(k == 0)
        def _():
            acc[...] = jnp.zeros_like(acc)

        av = a_ref[...].astype(BF16)
        for h in range(nb):
            @pl.when(j == h)
            def _():
                acc[...] += _dg(av, b_refs[h][...].astype(BF16), 0, 0)

        @pl.when(k == nk - 1)
        def _():
            v = acc[...]
            if chips == "cols":
                for s in range(span):
                    o_ref[s] = v[:, s * cw:(s + 1) * cw]
            elif chips == "rows":
                rows = Kw // N_CHIPS
                for p in range(N_CHIPS):
                    o_ref[p] = v[p * rows:(p + 1) * rows, :]
            else:
                o_ref[...] = v

    return pl.pallas_call(
        body, name=name, grid=(nb, nk),
        in_specs=[pl.BlockSpec((tk, Kw), lambda j, k: (k, 0))] + b_specs,
        out_specs=o_spec, out_shape=out_shape,
        scratch_shapes=[pltpu.VMEM((Kw, tn), F32)],
        compiler_params=_cp(("parallel", "arbitrary")),
    )(a, *(tuple(b) if pair else (b,)))


def _matmul_rows(a, w, name, res=None, w_layer=None, transposed=False):
    M, K = a.shape
    if w_layer is not None:
        cw = w.shape[2]
        N = N_CHIPS * cw
        w_spec = pl.BlockSpec((N_CHIPS, K, cw), lambda i: (0, w_layer, 0))
    else:
        N = w.shape[0] if transposed else w.shape[1]
        assert w.shape[1 if transposed else 0] == K
        w_spec = pl.BlockSpec(w.shape, lambda i: (0, 0))
    tm = _pick(M, (512, 256, 128))
    has_res = res is not None

    def body(*refs):
        a_ref, w_ref = refs[0], refs[1]
        r_ref = refs[2] if has_res else None
        o_ref = refs[-1]
        av = a_ref[...].astype(BF16)
        if w_layer is not None:
            for p in range(N_CHIPS):
                o_ref[:, pl.ds(p * cw, cw)] = jnp.dot(av, w_ref[p], preferred_element_type=F32)
        else:
            v = _dg(av, w_ref[...], 1, 1 if transposed else 0)
            o_ref[...] = v + r_ref[...] if has_res else v

    row = pl.BlockSpec((tm, N), lambda i: (i, 0))
    return pl.pallas_call(
        body, name=name, grid=(M // tm,),
        in_specs=[pl.BlockSpec((tm, K), lambda i: (i, 0)), w_spec] + ([row] if has_res else []),
        out_specs=row, out_shape=jax.ShapeDtypeStruct((M, N), F32),
        compiler_params=_cp(("parallel",)),
    )(*((a, w) + ((res,) if has_res else ())))


def _dgrad_norm(dy, w, x, wn, dres, name, w_layer=None):
    pair = isinstance(dy, (tuple, list))
    M = dy[0].shape[0] if pair else dy.shape[0]
    Kp = 2 * dy[0].shape[1] if pair else dy.shape[1]
    D = x.shape[1]
    if w_layer is not None:
        cw = w.shape[2]
        assert N_CHIPS * cw == Kp and w.shape[1] % D == 0
        w_spec = pl.BlockSpec((N_CHIPS, D, cw), lambda i: (0, w_layer, 0))
    else:
        assert w.shape == (D, Kp)
        w_spec = pl.BlockSpec((D, Kp), lambda i: (0, 0))
    tm = _pick(M, (256, 128))
    width = Kp // 2 if pair else Kp
    dy_specs = [pl.BlockSpec((tm, width), lambda i: (i, 0))] * (2 if pair else 1)
    nd = len(dy_specs)

    def body(*refs):
        dy_refs = refs[:nd]
        w_ref, x_ref, wn_ref, dres_ref, dx_ref, dxb_ref, dwn_ref = refs[nd:]
        i = pl.program_id(0)
        if w_layer is not None:
            dh = None
            for p in range(N_CHIPS):
                src, off = divmod(p * cw, width)
                part = _dg(dy_refs[src][:, pl.ds(off, cw)], w_ref[p], 1, 1)
                dh = part if dh is None else dh + part
        else:
            dh = _dg(dy_refs[0][...], w_ref[...], 1, 1)
        xv = x_ref[...]
        r = lax.rsqrt(jnp.mean(xv * xv, axis=-1, keepdims=True) + NORM_EPS)
        yv = xv * r
        dyv = dh * wn_ref[...]
        dxv = r * (dyv - yv * jnp.mean(dyv * yv, axis=-1, keepdims=True)) + dres_ref[...]
        dx_ref[...] = dxv
        dxb_ref[...] = dxv.astype(BF16)
        part = jnp.sum(dh * yv, axis=0, keepdims=True)

        @pl.when(i == 0)
        def _():
            dwn_ref[...] = part

        @pl.when(i > 0)
        def _():
            dwn_ref[...] += part

    row = pl.BlockSpec((tm, D), lambda i: (i, 0))
    one = pl.BlockSpec((1, D), lambda i: (0, 0))
    return pl.pallas_call(
        body, name=name, grid=(M // tm,),
        in_specs=dy_specs + [w_spec, row, one, row],
        out_specs=[row, row, one],
        out_shape=[jax.ShapeDtypeStruct((M, D), F32), jax.ShapeDtypeStruct((M, D), BF16),
                   jax.ShapeDtypeStruct((1, D), F32)],
        compiler_params=_cp(("arbitrary",)),
    )(*(tuple(dy) if pair else (dy,)), w, x, wn, dres)


FFN_TN_FWD = 256
FFN_TN_BWD = 128
FFN_ROWS = 256
PAD = 8


def _conv3(pad_ref, w, r0, tr):
    um = pad_ref[pl.ds(PAD - 1 + r0, tr), :]
    uc = pad_ref[pl.ds(PAD + r0, tr), :]
    up = pad_ref[pl.ds(PAD + 1 + r0, tr), :]
    return w[0:1, :] * um + w[1:2, :] * uc + w[2:3, :] * up, (um, uc, up)


def _zero_pads(pad_ref, S, tn):
    pad_ref[pl.ds(0, PAD), :] = jnp.zeros((PAD, tn), F32)
    pad_ref[pl.ds(PAD + S, PAD), :] = jnp.zeros((PAD, tn), F32)


def _ffn_mid_fwd(h, wup, wconv, bconv, name):
    S, D = h.shape
    F = wup.shape[1] // 2
    tn = FFN_TN_FWD
    nb = F // tn
    tr = min(FFN_ROWS, S)

    def body(h_ref, wv_ref, wg_ref, cv_ref, cg_ref, bv_ref, bg_ref, a_ref, uv_ref, ug_ref):
        _zero_pads(uv_ref, S, tn)
        _zero_pads(ug_ref, S, tn)
        hv = h_ref[...]
        uv_ref[pl.ds(PAD, S), :] = jnp.dot(hv, wv_ref[...], preferred_element_type=F32)
        ug_ref[pl.ds(PAD, S), :] = jnp.dot(hv, wg_ref[...], preferred_element_type=F32)
        cwv, cwg, bv, bg = cv_ref[...], cg_ref[...], bv_ref[...], bg_ref[...]
        for r0 in range(0, S, tr):
            cv = _conv3(uv_ref, cwv, r0, tr)[0] + bv
            cg = _conv3(ug_ref, cwg, r0, tr)[0] + bg
            a_ref[pl.ds(r0, tr), :] = (cg * _sigmoid(cg) * cv).astype(BF16)

    col = lambda off: (lambda j: (0, j + off))
    padded = pl.BlockSpec((S + 2 * PAD, tn), col(0))
    return pl.pallas_call(
        body, name=name, grid=(nb,),
        in_specs=[pl.BlockSpec((S, D), lambda j: (0, 0)),
                  pl.BlockSpec((D, tn), col(0)), pl.BlockSpec((D, tn), col(nb)),
                  pl.BlockSpec((3, tn), col(0)), pl.BlockSpec((3, tn), col(nb)),
                  pl.BlockSpec((1, tn), col(0)), pl.BlockSpec((1, tn), col(nb))],
        out_specs=[pl.BlockSpec((S, tn), col(0)), padded, padded],
        out_shape=[jax.ShapeDtypeStruct((S, F), BF16), jax.ShapeDtypeStruct((S + 2 * PAD, F), F32),
                   jax.ShapeDtypeStruct((S + 2 * PAD, F), F32)],
        compiler_params=_cp(("parallel",)),
    )(h, wup, wup, wconv, wconv, bconv, bconv)


def _rows8(rows):
    n = rows[0].shape[1]
    idx = lax.broadcasted_iota(jnp.int32, (8, n), 0)
    out = jnp.zeros((8, n), F32)
    for k, r in enumerate(rows):
        out = jnp.where(idx == k, r, out)
    return out


def _ffn_mid_bwd(dyb, wdown, uv, ug, wconv, bconv, name):
    S, D = dyb.shape
    F = wdown.shape[0]
    tn = FFN_TN_BWD
    nb = F // tn
    tr = min(FFN_ROWS, S)

    def body(dy_ref, wd_ref, uv_ref, ug_ref, cv_ref, cg_ref, bv_ref, bg_ref,
             duv_ref, dug_ref, a_ref, gwv_ref, gwg_ref, pdv, pdg):
        for p in (pdv, pdg):
            _zero_pads(p, S, tn)
        wd = wd_ref[...]
        cwv, cwg, bv, bg = cv_ref[...], cg_ref[...], bv_ref[...], bg_ref[...]
        zero = jnp.zeros((1, tn), F32)
        gv = [zero, zero, zero, zero]
        gg = [zero, zero, zero, zero]
        for r0 in range(0, S, tr):
            cv, shv = _conv3(uv_ref, cwv, r0, tr)
            cg, shg = _conv3(ug_ref, cwg, r0, tr)
            cv = cv + bv
            cg = cg + bg
            sg = _sigmoid(cg)
            sl = cg * sg
            a_ref[pl.ds(r0, tr), :] = (sl * cv).astype(BF16)
            da = _dg(dy_ref[pl.ds(r0, tr), :], wd, 1, 1)
            dcv = da * sl
            dcg = da * cv * (sg * (1.0 + cg * (1.0 - sg)))
            pdv[pl.ds(PAD + r0, tr), :] = dcv
            pdg[pl.ds(PAD + r0, tr), :] = dcg
            for k in range(3):
                gv[k] = gv[k] + jnp.sum(dcv * shv[k], axis=0, keepdims=True)
                gg[k] = gg[k] + jnp.sum(dcg * shg[k], axis=0, keepdims=True)
            gv[3] = gv[3] + jnp.sum(dcv, axis=0, keepdims=True)
            gg[3] = gg[3] + jnp.sum(dcg, axis=0, keepdims=True)
        gwv_ref[...] = _rows8(gv)
        gwg_ref[...] = _rows8(gg)
        for r0 in range(0, S, tr):
            for pd, cw, out in ((pdv, cwv, duv_ref), (pdg, cwg, dug_ref)):
                dm = pd[pl.ds(PAD - 1 + r0, tr), :]
                dc = pd[pl.ds(PAD + r0, tr), :]
                dp = pd[pl.ds(PAD + 1 + r0, tr), :]
                out[pl.ds(r0, tr), :] = (cw[0:1, :] * dp + cw[1:2, :] * dc + cw[2:3, :] * dm).astype(BF16)

    col = lambda off: (lambda j: (0, j + off))
    blk = pl.BlockSpec((S, tn), col(0))
    padded = pl.BlockSpec((S + 2 * PAD, tn), col(0))
    g8 = pl.BlockSpec((8, tn), col(0))
    return pl.pallas_call(
        body, name=name, grid=(nb,),
        in_specs=[pl.BlockSpec((S, D), lambda j: (0, 0)), pl.BlockSpec((tn, D), lambda j: (j, 0)), padded, padded,
                  pl.BlockSpec((3, tn), col(0)), pl.BlockSpec((3, tn), col(nb)),
                  pl.BlockSpec((1, tn), col(0)), pl.BlockSpec((1, tn), col(nb))],
        out_specs=[blk, blk, blk, g8, g8],
        out_shape=[jax.ShapeDtypeStruct((S, F), BF16), jax.ShapeDtypeStruct((S, F), BF16),
                   jax.ShapeDtypeStruct((S, F), BF16), jax.ShapeDtypeStruct((8, F), F32),
                   jax.ShapeDtypeStruct((8, F), F32)],
        scratch_shapes=[pltpu.VMEM((S + 2 * PAD, tn), F32)] * 2,
        compiler_params=_cp(("parallel",)),
    )(dyb, wdown, uv, ug, wconv, wconv, bconv, bconv)


def _log_sigmoid(x):
    return jnp.minimum(x, 0.0) - jnp.log(1.0 + jnp.exp(-jnp.abs(x)))


def _gla_gate_fwd(proj, wgf, bgf, wgb, bgb, name):
    S = proj.shape[0]
    tm = _pick(S, (512, 256))

    def body(r_ref, wf_ref, bf_ref, wb_ref, bb_ref, laf_ref, lab_ref):
        r = r_ref[...].astype(BF16)
        lf = jnp.dot(r, wf_ref[...].astype(BF16), preferred_element_type=F32) + bf_ref[...]
        lb = jnp.dot(r, wb_ref[...].astype(BF16), preferred_element_type=F32) + bb_ref[...]
        laf_ref[...] = _log_sigmoid(lf) * (1.0 / GLA_GATE_NORMALIZER)
        lab_ref[...] = _log_sigmoid(lb) * (1.0 / GLA_GATE_NORMALIZER)

    full = lambda shp: pl.BlockSpec(shp, lambda i: (0, 0))
    row = pl.BlockSpec((tm, GLA_KEY), lambda i: (i, 0))
    return pl.pallas_call(
        body, name=name, grid=(S // tm,),
        in_specs=[pl.BlockSpec((tm, LANES), lambda i: (i, GLA_R_BLOCK)),
                  full((LANES, GLA_KEY)), full((1, GLA_KEY)), full((LANES, GLA_KEY)), full((1, GLA_KEY))],
        out_specs=[row, row],
        out_shape=[jax.ShapeDtypeStruct((S, GLA_KEY), F32)] * 2,
        compiler_params=_cp(("parallel",)),
    )(proj, wgf, bgf, wgb, bgb)


def _gla_gate_bwd(dlaf, dlab, proj, wgf, bgf, wgb, bgb, name):
    S = proj.shape[0]
    tm = _pick(S, (512, 256))

    def body(dlf_ref, dlb_ref, r_ref, wf_ref, bf_ref, wb_ref, bb_ref, dr_ref, dwf_ref, dbf_ref, dwb_ref, dbb_ref):
        i = pl.program_id(0)
        r = r_ref[...].astype(BF16)
        wf = wf_ref[...].astype(BF16)
        wb = wb_ref[...].astype(BF16)
        lf = jnp.dot(r, wf, preferred_element_type=F32) + bf_ref[...]
        lb = jnp.dot(r, wb, preferred_element_type=F32) + bb_ref[...]
        glf = dlf_ref[...] * (1.0 / GLA_GATE_NORMALIZER) * (1.0 / (1.0 + jnp.exp(lf)))
        glb = dlb_ref[...] * (1.0 / GLA_GATE_NORMALIZER) * (1.0 / (1.0 + jnp.exp(lb)))
        gfb = glf.astype(BF16)
        gbb = glb.astype(BF16)
        dr_ref[...] = _dg(gfb, wf, 1, 1) + _dg(gbb, wb, 1, 1)
        parts = (_dg(r, gfb, 0, 0), jnp.sum(glf, axis=0, keepdims=True),
                 _dg(r, gbb, 0, 0), jnp.sum(glb, axis=0, keepdims=True))
        outs = (dwf_ref, dbf_ref, dwb_ref, dbb_ref)

        @pl.when(i == 0)
        def _():
            for o, p in zip(outs, parts):
                o[...] = p

        @pl.when(i > 0)
        def _():
            for o, p in zip(outs, parts):
                o[...] += p

    full = lambda shp: pl.BlockSpec(shp, lambda i: (0, 0))
    row = pl.BlockSpec((tm, GLA_KEY), lambda i: (i, 0))
    return pl.pallas_call(
        body, name=name, grid=(S // tm,),
        in_specs=[row, row, pl.BlockSpec((tm, LANES), lambda i: (i, GLA_R_BLOCK)),
                  full((LANES, GLA_KEY)), full((1, GLA_KEY)), full((LANES, GLA_KEY)), full((1, GLA_KEY))],
        out_specs=[pl.BlockSpec((tm, LANES), lambda i: (i, 0)),
                   full((LANES, GLA_KEY)), full((1, GLA_KEY)), full((LANES, GLA_KEY)), full((1, GLA_KEY))],
        out_shape=[jax.ShapeDtypeStruct((S, LANES), F32),
                   jax.ShapeDtypeStruct((LANES, GLA_KEY), F32), jax.ShapeDtypeStruct((1, GLA_KEY), F32),
                   jax.ShapeDtypeStruct((LANES, GLA_KEY), F32), jax.ShapeDtypeStruct((1, GLA_KEY), F32)],
        compiler_params=_cp(("arbitrary",)),
    )(dlaf, dlab, proj, wgf, bgf, wgb, bgb)


def _gla_masks(rev):
    C = GLA_CHUNK
    t = lax.broadcasted_iota(jnp.int32, (C, C), 0)
    s = lax.broadcasted_iota(jnp.int32, (C, C), 1)
    if rev:
        return (s >= t), (s > t), (t >= s), (t > s)
    return (s <= t), (s <= t), (t <= s), (t <= s)


def _cum_dot(cum, x):
    return jnp.dot(cum.astype(F32), x, precision=HIGHEST, preferred_element_type=F32)


def _gla_chunk_common(q, k, la, cum, end_row):
    b = _cum_dot(cum, la)
    bend = b[end_row:end_row + 1, :]
    e = jnp.exp(b)
    qd = q * (GLA_DK ** -0.5) * e
    ei = jnp.exp(-b)
    ee = jnp.exp(bend - b)
    d = jnp.exp(bend)
    return e, ei, ee, d, qd, k * ei, k * ee


GLA_CB = 16


def _gla_specs(S, rev_order):
    n = S // GLA_CHUNK
    cb = min(GLA_CB, n)
    nblk = n // cb
    rows = cb * GLA_CHUNK
    ci = (lambda i: nblk - 1 - i) if rev_order else (lambda i: i)
    q_spec = pl.BlockSpec((rows, GLA_DK), lambda h, i: (ci(i), h))
    k_spec = pl.BlockSpec((rows, GLA_DK), lambda h, i: (ci(i), GLA_HEADS + h))
    v_spec = pl.BlockSpec((rows, GLA_DV), lambda h, i: (ci(i), GLA_KEY * 2 // GLA_DV + h))
    la_spec = pl.BlockSpec((rows, GLA_DK), lambda h, i: (ci(i), h))
    o_spec = pl.BlockSpec((rows, GLA_DV), lambda h, i: (ci(i), h))
    st_spec = pl.BlockSpec((1, cb, GLA_DV, GLA_DK), lambda h, i: (h, ci(i), 0, 0))
    return n, cb, nblk, q_spec, k_spec, v_spec, la_spec, o_spec, st_spec


def _gla_scan_fwd(proj, la, rev, name):
    S = proj.shape[0]
    C = GLA_CHUNK
    n, cb, nblk, q_spec, k_spec, v_spec, la_spec, o_spec, st_spec = _gla_specs(S, rev)
    end_row = 0 if rev else C - 1
    order = list(range(cb))[::-1] if rev else list(range(cb))

    def body(q_ref, k_ref, v_ref, la_ref, o_ref, st_ref, state):
        i = pl.program_id(1)

        @pl.when(i == 0)
        def _():
            state[...] = jnp.zeros_like(state)

        cum, mask, _, _ = _gla_masks(rev)
        pre, intra, kv = {}, {}, {}
        for cc in order:
            rows = pl.ds(cc * C, C)
            q, k, v, lav = q_ref[rows, :], k_ref[rows, :], v_ref[rows, :], la_ref[rows, :]
            _, _, _, d, qd, ki, ke = _gla_chunk_common(q, k, lav, cum, end_row)
            qdb, kib, keb, vb = qd.astype(BF16), ki.astype(BF16), ke.astype(BF16), v.astype(BF16)
            pre[cc] = (d, qdb)
            att = jnp.where(mask, _dg(qdb, kib, 1, 1), 0.0)
            intra[cc] = jnp.dot(att.astype(BF16), vb, preferred_element_type=F32)
            kv[cc] = _dg(vb, keb, 0, 0)
        st = state[...]
        for cc in order:
            d, qdb = pre[cc]
            o_ref[pl.ds(cc * C, C), :] = intra[cc] + _dg(qdb, st.astype(BF16), 1, 1)
            st_ref[0, cc] = st
            st = st * d + kv[cc]
        state[...] = st

    return pl.pallas_call(
        body, name=name, grid=(GLA_HEADS, nblk),
        in_specs=[q_spec, k_spec, v_spec, la_spec],
        out_specs=[o_spec, st_spec],
        out_shape=[jax.ShapeDtypeStruct((S, GLA_VAL), F32),
                   jax.ShapeDtypeStruct((GLA_HEADS, n, GLA_DV, GLA_DK), F32)],
        scratch_shapes=[pltpu.VMEM((GLA_DV, GLA_DK), F32)],
        compiler_params=_cp(("parallel", "arbitrary")),
    )(proj, proj, proj, la)


def _gla_scan_bwd(do, proj, la, states, rev, name):
    S = proj.shape[0]
    C = GLA_CHUNK
    n, cb, nblk, q_spec, k_spec, v_spec, la_spec, o_spec, st_spec = _gla_specs(S, not rev)
    end_row = 0 if rev else C - 1
    order = list(range(cb)) if rev else list(range(cb))[::-1]

    def body(do_ref, q_ref, k_ref, v_ref, la_ref, st_ref, dq_ref, dk_ref, dv_ref, dla_ref, gstate):
        i = pl.program_id(1)

        @pl.when(i == 0)
        def _():
            gstate[...] = jnp.zeros_like(gstate)

        cum, mask, cum_t, mask_t = _gla_masks(rev)
        g = gstate[...]
        for cc in order:
            rows = pl.ds(cc * C, C)
            q, k, v, lav = q_ref[rows, :], k_ref[rows, :], v_ref[rows, :], la_ref[rows, :]
            dov = do_ref[rows, :]
            st = st_ref[0, cc]
            e, ei, ee, d, qd, ki, ke = _gla_chunk_common(q, k, lav, cum, end_row)
            qdb, kib, keb, vb = qd.astype(BF16), ki.astype(BF16), ke.astype(BF16), v.astype(BF16)
            dob, gb, stb = dov.astype(BF16), g.astype(BF16), st.astype(BF16)
            att_t = jnp.where(mask_t, _dg(kib, qdb, 1, 1), 0.0)
            da = jnp.where(mask, _dg(dob, vb, 1, 1), 0.0)
            da_t = jnp.where(mask_t, _dg(vb, dob, 1, 1), 0.0)
            dv_ref[rows, :] = jnp.dot(att_t.astype(BF16), dob, preferred_element_type=F32) + _dg(keb, gb, 1, 1)
            dqd = (jnp.dot(da.astype(BF16), kib, preferred_element_type=F32)
                   + jnp.dot(dob, stb, preferred_element_type=F32))
            dki = jnp.dot(da_t.astype(BF16), qdb, preferred_element_type=F32)
            dke = jnp.dot(vb, gb, preferred_element_type=F32)
            dd = jnp.sum(st * g, axis=0, keepdims=True)
            g = g * d + _dg(dob, qdb, 0, 0)
            dq_ref[rows, :] = dqd * e * (GLA_DK ** -0.5)
            dk_ref[rows, :] = dki * ei + dke * ee
            dkeke = dke * ke
            db = dqd * qd - dki * ki - dkeke
            dbend = jnp.sum(dkeke, axis=0, keepdims=True) + dd * d
            dla_ref[rows, :] = _cum_dot(cum_t, db) + dbend
        gstate[...] = g

    key_out = la_spec
    return pl.pallas_call(
        body, name=name, grid=(GLA_HEADS, nblk),
        in_specs=[o_spec, q_spec, k_spec, v_spec, la_spec, st_spec],
        out_specs=[key_out, key_out, o_spec, key_out],
        out_shape=[jax.ShapeDtypeStruct((S, GLA_KEY), F32), jax.ShapeDtypeStruct((S, GLA_KEY), F32),
                   jax.ShapeDtypeStruct((S, GLA_VAL), F32), jax.ShapeDtypeStruct((S, GLA_KEY), F32)],
        scratch_shapes=[pltpu.VMEM((GLA_DV, GLA_DK), F32)],
        compiler_params=_cp(("parallel", "arbitrary")),
    )(do, proj, proj, proj, la, states)


def _gla_out_fwd(of, ob, proj, gn, name):
    S = of.shape[0]
    tm = _pick(S, (256, 128))
    gblk = (2 * GLA_KEY + GLA_VAL) // GLA_VAL

    def body(of_ref, ob_ref, g_ref, gn_ref, z_ref):
        gnv = gn_ref[...]
        for h in range(GLA_HEADS):
            cols = pl.ds(h * GLA_DV, GLA_DV)
            o = of_ref[:, cols] + ob_ref[:, cols]
            r = lax.rsqrt(jnp.mean(o * o, axis=-1, keepdims=True) + NORM_EPS)
            gv = g_ref[:, cols]
            z_ref[:, cols] = (o * r * gnv * (gv * _sigmoid(gv))).astype(BF16)

    row = pl.BlockSpec((tm, GLA_VAL), lambda i: (i, 0))
    return pl.pallas_call(
        body, name=name, grid=(S // tm,),
        in_specs=[row, row, pl.BlockSpec((tm, GLA_VAL), lambda i: (i, gblk)),
                  pl.BlockSpec((1, GLA_DV), lambda i: (0, 0))],
        out_specs=row,
        out_shape=jax.ShapeDtypeStruct((S, GLA_VAL), BF16),
        compiler_params=_cp(("parallel",)),
    )(of, ob, proj, gn)


def _gla_out_bwd(dz, of, ob, proj, gn, name):
    S = of.shape[0]
    tm = _pick(S, (256, 128))
    gblk = (2 * GLA_KEY + GLA_VAL) // GLA_VAL

    def body(dz_ref, of_ref, ob_ref, g_ref, gn_ref, do_ref, dg_ref, dgn_ref):
        i = pl.program_id(0)
        gnv = gn_ref[...]
        part = jnp.zeros((1, GLA_DV), F32)
        for h in range(GLA_HEADS):
            cols = pl.ds(h * GLA_DV, GLA_DV)
            o = of_ref[:, cols] + ob_ref[:, cols]
            r = lax.rsqrt(jnp.mean(o * o, axis=-1, keepdims=True) + NORM_EPS)
            y = o * r
            gv = g_ref[:, cols]
            sg = _sigmoid(gv)
            dzv = dz_ref[:, cols]
            dg_ref[:, cols] = dzv * (y * gnv) * (sg * (1.0 + gv * (1.0 - sg)))
            don = dzv * (gv * sg)
            part = part + jnp.sum(don * y, axis=0, keepdims=True)
            dy = don * gnv
            do_ref[:, cols] = r * (dy - y * jnp.mean(dy * y, axis=-1, keepdims=True))

        @pl.when(i == 0)
        def _():
            dgn_ref[...] = part

        @pl.when(i > 0)
        def _():
            dgn_ref[...] += part

    row = pl.BlockSpec((tm, GLA_VAL), lambda i: (i, 0))
    one = pl.BlockSpec((1, GLA_DV), lambda i: (0, 0))
    return pl.pallas_call(
        body, name=name, grid=(S // tm,),
        in_specs=[row, row, row, pl.BlockSpec((tm, GLA_VAL), lambda i: (i, gblk)), one],
        out_specs=[row, row, one],
        out_shape=[jax.ShapeDtypeStruct((S, GLA_VAL), F32), jax.ShapeDtypeStruct((S, GLA_VAL), F32),
                   jax.ShapeDtypeStruct((1, GLA_DV), F32)],
        compiler_params=_cp(("arbitrary",)),
    )(dz, of, ob, proj, gn)


N_QK_HEADS = ATT_QH + ATT_KVH


def _qk_prep_fwd(proj, qn, kn, rc, rs, name):
    S = proj.shape[0]
    tm = _pick(S, (256, 128))
    W = N_QK_HEADS * ATT_HD
    scale = ATT_HD ** -0.5

    def body(p_ref, qn_ref, kn_ref, rc_ref, rs_ref, v_in_ref, qk_ref, v_ref, kt_ref, vt_ref):
        c, s = rc_ref[...], rs_ref[...]
        for h in range(N_QK_HEADS):
            cols = pl.ds(h * ATT_HD, ATT_HD)
            w = qn_ref[...] if h < ATT_QH else kn_ref[...]
            xv = p_ref[:, cols]
            r = lax.rsqrt(jnp.mean(xv * xv, axis=-1, keepdims=True) + NORM_EPS)
            y = xv * r * w
            out = y * c + pltpu.roll(y, ATT_HD // 2, 1) * s
            if h < ATT_QH:
                qk_ref[:, cols] = (out * scale).astype(BF16)
            else:
                qk_ref[:, cols] = out.astype(BF16)
                kt_ref[pl.ds((h - ATT_QH) * ATT_HD, ATT_HD), :] = out.T.astype(BF16)
        v_ref[...] = v_in_ref[...].astype(BF16)
        for h in range(ATT_KVH):
            vt_ref[pl.ds(h * ATT_HD, ATT_HD), :] = v_in_ref[:, pl.ds(h * ATT_HD, ATT_HD)].T.astype(BF16)

    one = pl.BlockSpec((1, ATT_HD), lambda i: (0, 0))
    tab = pl.BlockSpec((tm, ATT_HD), lambda i: (i, 0))
    vw = ATT_KVH * ATT_HD
    tr = pl.BlockSpec((vw, tm), lambda i: (0, i))
    return pl.pallas_call(
        body, name=name, grid=(S // tm,),
        in_specs=[pl.BlockSpec((tm, W), lambda i: (i, 0)), one, one, tab, tab,
                  pl.BlockSpec((tm, vw), lambda i: (i, W // vw))],
        out_specs=[pl.BlockSpec((tm, W), lambda i: (i, 0)), pl.BlockSpec((tm, vw), lambda i: (i, 0)), tr, tr],
        out_shape=[jax.ShapeDtypeStruct((S, W), BF16), jax.ShapeDtypeStruct((S, vw), BF16),
                   jax.ShapeDtypeStruct((vw, S), BF16), jax.ShapeDtypeStruct((vw, S), BF16)],
        compiler_params=_cp(("parallel",)),
    )(proj, qn, kn, rc, rs, proj)


def _qk_prep_bwd(dqk, proj, qn, kn, rc, rs, name):
    S = proj.shape[0]
    tm = _pick(S, (256, 128))
    W = N_QK_HEADS * ATT_HD

    def body(d_ref, p_ref, qn_ref, kn_ref, rc_ref, rs_ref, dp_ref, dqn_ref, dkn_ref):
        i = pl.program_id(0)
        c, s = rc_ref[...], rs_ref[...]
        parts = [jnp.zeros((1, ATT_HD), F32), jnp.zeros((1, ATT_HD), F32)]
        for h in range(N_QK_HEADS):
            cols = pl.ds(h * ATT_HD, ATT_HD)
            w = qn_ref[...] if h < ATT_QH else kn_ref[...]
            dout = d_ref[:, cols]
            dy = dout * c + pltpu.roll(dout * s, ATT_HD // 2, 1)
            xv = p_ref[:, cols]
            r = lax.rsqrt(jnp.mean(xv * xv, axis=-1, keepdims=True) + NORM_EPS)
            xr = xv * r
            which = 0 if h < ATT_QH else 1
            parts[which] = parts[which] + jnp.sum(dy * xr, axis=0, keepdims=True)
            dxr = dy * w
            dp_ref[:, cols] = r * (dxr - xr * jnp.mean(dxr * xr, axis=-1, keepdims=True))

        @pl.when(i == 0)
        def _():
            dqn_ref[...] = parts[0]
            dkn_ref[...] = parts[1]

        @pl.when(i > 0)
        def _():
            dqn_ref[...] += parts[0]
            dkn_ref[...] += parts[1]

    one = pl.BlockSpec((1, ATT_HD), lambda i: (0, 0))
    tab = pl.BlockSpec((tm, ATT_HD), lambda i: (i, 0))
    row = pl.BlockSpec((tm, W), lambda i: (i, 0))
    return pl.pallas_call(
        body, name=name, grid=(S // tm,),
        in_specs=[row, row, one, one, tab, tab],
        out_specs=[row, one, one],
        out_shape=[jax.ShapeDtypeStruct((S, W), F32), jax.ShapeDtypeStruct((1, ATT_HD), F32),
                   jax.ShapeDtypeStruct((1, ATT_HD), F32)],
        compiler_params=_cp(("arbitrary",)),
    )(dqk, proj, qn, kn, rc, rs)


ATT_TQ = 1024
LSE_ROWS = 8


def _attn_fwd(qk, vt, name):
    S = qk.shape[0]
    tq = min(ATT_TQ, S)

    def body(q_ref, k_ref, vt_ref, o_ref, lse_ref):
        st = _dg(k_ref[...], q_ref[...], 1, 1)
        m = jnp.max(st, axis=0, keepdims=True)
        pt = jnp.exp(st - m)
        l = jnp.sum(pt, axis=0, keepdims=True)
        ot = jnp.dot(vt_ref[...], pt.astype(BF16), preferred_element_type=F32)
        o_ref[...] = (ot * (1.0 / l)).T
        lse_ref[...] = jnp.broadcast_to(m + jnp.log(l), (LSE_ROWS, tq))

    qo = pl.BlockSpec((tq, ATT_HD), lambda h, i: (i, h))
    return pl.pallas_call(
        body, name=name, grid=(ATT_QH, S // tq),
        in_specs=[qo, pl.BlockSpec((S, ATT_HD), lambda h, i: (0, ATT_QH + h // ATT_GROUP)),
                  pl.BlockSpec((ATT_HD, S), lambda h, i: (h // ATT_GROUP, 0))],
        out_specs=[qo, pl.BlockSpec((LSE_ROWS, tq), lambda h, i: (h, i))],
        out_shape=[jax.ShapeDtypeStruct((S, ATT_QH * ATT_HD), F32),
                   jax.ShapeDtypeStruct((ATT_QH * LSE_ROWS, S), F32)],
        compiler_params=_cp(("parallel", "parallel")),
    )(qk, qk, vt)


def _attn_bwd(do, o, lse, qk, v, kt, name):
    S = qk.shape[0]
    tq = min(ATT_TQ, S)
    scale = ATT_HD ** -0.5

    def body(do_ref, o_ref, lse_ref, q_ref, k_ref, v_ref, kt_ref, dq_ref, dk_ref, dv_ref):
        g = pl.program_id(1)
        i = pl.program_id(2)

        @pl.when((g == 0) & (i == 0))
        def _():
            dk_ref[...] = jnp.zeros_like(dk_ref)
            dv_ref[...] = jnp.zeros_like(dv_ref)

        q = q_ref[...]
        dov = do_ref[...]
        dob = dov.astype(BF16)
        delta = jnp.sum((dov * o_ref[...]).T, axis=0, keepdims=True)
        st = _dg(k_ref[...], q, 1, 1)
        pt = jnp.exp(st - lse_ref[0:1, :])
        dpt = _dg(v_ref[...], dob, 1, 1)
        dst = (pt * (dpt - delta)).astype(BF16)
        dv_ref[...] += jnp.dot(pt.astype(BF16), dob, preferred_element_type=F32)
        dk_ref[...] += jnp.dot(dst, q, preferred_element_type=F32)
        dq_ref[...] = jnp.dot(kt_ref[...], dst, preferred_element_type=F32).T * scale

    qo = pl.BlockSpec((tq, ATT_HD), lambda kv, g, i: (i, kv * ATT_GROUP + g))
    kvo = pl.BlockSpec((S, ATT_HD), lambda kv, g, i: (0, kv))
    return pl.pallas_call(
        body, name=name, grid=(ATT_KVH, ATT_GROUP, S // tq),
        in_specs=[qo, qo, pl.BlockSpec((LSE_ROWS, tq), lambda kv, g, i: (kv * ATT_GROUP + g, i)), qo,
                  pl.BlockSpec((S, ATT_HD), lambda kv, g, i: (0, ATT_QH + kv)), kvo,
                  pl.BlockSpec((ATT_HD, S), lambda kv, g, i: (kv, 0))],
        out_specs=[qo, kvo, kvo],
        out_shape=[jax.ShapeDtypeStruct((S, ATT_QH * ATT_HD), F32),
                   jax.ShapeDtypeStruct((S, ATT_KVH * ATT_HD), F32),
                   jax.ShapeDtypeStruct((S, ATT_KVH * ATT_HD), F32)],
        compiler_params=_cp(("parallel", "arbitrary", "arbitrary")),
    )(do, o, lse, qk, qk, v, kt)


def _adamw(w, g, m, v, name):
    rows, cols = w.shape
    tr = rows
    for cand in (512, 256, 128, 64, 32, 16, 8):
        if rows % cand == 0 and cand * cols * 4 <= 2 * 1024 * 1024:
            tr = cand
            break

    def body(w_ref, g_ref, m_ref, v_ref, d_ref, nm_ref, nv_ref):
        gv = g_ref[...]
        nm = ADAM_B1 * m_ref[...] + (1.0 - ADAM_B1) * gv
        nv = ADAM_B2 * v_ref[...] + (1.0 - ADAM_B2) * (gv * gv)
        m_hat = nm / (1.0 - ADAM_B1 ** ADAM_STEP)
        v_hat = nv / (1.0 - ADAM_B2 ** ADAM_STEP)
        d_ref[...] = -ADAM_LR * (m_hat / (jnp.sqrt(v_hat) + ADAM_EPS) + ADAM_WD * w_ref[...])
        nm_ref[...] = nm
        nv_ref[...] = nv

    blk = pl.BlockSpec((tr, cols), lambda i: (i, 0))
    return pl.pallas_call(
        body, name=name, grid=(rows // tr,),
        in_specs=[blk] * 4, out_specs=[blk] * 3,
        out_shape=[jax.ShapeDtypeStruct((rows, cols), F32)] * 3,
        compiler_params=_cp(("parallel",)),
    )(w, g, m, v)


ANY = pl.BlockSpec(memory_space=pl.ANY)


def _place():
    return lax.axis_index("x"), lax.axis_index("y"), lax.axis_index("c")


def _other_chips(x, y):
    return [(1 - x, y), (x, 1 - y), (1 - x, 1 - y)]


def _half_rows(c, H):
    return pl.ds(pl.multiple_of(c * H, 8), H)


def _allreduce_small(v, name):
    R = v.shape[0]
    n_dev = 8

    def body(v_ref, sum_ref, all_ref, token_ref, send_sems, recv_sems, local_sem):
        token_ref[...] = jnp.zeros_like(token_ref)
        x, y, c = _place()
        me, sibling = (x, y, c), (x, y, 1 - c)
        chips = _other_chips(x, y)

        def rows(px, py, pc):
            return all_ref.at[pl.ds(pl.multiple_of((4 * px + 2 * py + pc) * R, 8), R), :]

        def copy(k, block, to, src=None):
            return pltpu.make_async_remote_copy(
                src_ref=rows(*block) if src is None else src, dst_ref=rows(*block),
                send_sem=send_sems.at[k], recv_sem=recv_sems.at[k], device_id=to, device_id_type=MESH)

        own = pltpu.make_async_copy(v_ref, rows(*me), local_sem)
        own.start()
        first = [copy(0, me, sibling, src=v_ref)]
        first += [copy(1 + j, me, (*chip, c), src=v_ref) for j, chip in enumerate(chips)]
        for cp in first:
            cp.start()
        passed = [copy(4 + j, (*chip, c), sibling) for j, chip in enumerate(chips)]
        for j, chip in enumerate(chips):
            copy(1 + j, (*chip, c), me).wait_recv()
            passed[j].start()
        copy(0, sibling, me).wait_recv()
        for j, chip in enumerate(chips):
            copy(4 + j, (*chip, 1 - c), me).wait_recv()
        for cp in first + passed:
            cp.wait_send()
        own.wait()
        acc = all_ref[pl.ds(0, R), :]
        for d in range(1, n_dev):
            acc = acc + all_ref[pl.ds(d * R, R), :]
        sum_ref[...] = acc

    vm = pl.BlockSpec(memory_space=pltpu.VMEM)
    total, _, token = pl.pallas_call(
        body, name=name,
        in_specs=[vm], out_specs=[vm, vm, vm],
        out_shape=[jax.ShapeDtypeStruct((R, LANES), F32), jax.ShapeDtypeStruct((n_dev * R, LANES), F32),
                   jax.ShapeDtypeStruct((8, LANES), F32)],
        scratch_shapes=[pltpu.SemaphoreType.DMA((7,)), pltpu.SemaphoreType.DMA((7,)), pltpu.SemaphoreType.DMA],
    )(v)
    return total, token


def _swap_other_half(bufs, name):
    n = len(bufs)
    halves = [b.shape[1] // 2 for b in bufs]

    def body(*refs):
        g_refs, got_refs = refs[:n], refs[n:2 * n]
        send_sems, recv_sems = refs[2 * n:]
        x, y, c = _place()
        copies = [pltpu.make_async_remote_copy(
            src_ref=g_refs[k].at[p, _half_rows(1 - c, halves[k])], dst_ref=got_refs[k].at[p],
            send_sem=send_sems.at[N_CHIPS * k + p], recv_sem=recv_sems.at[N_CHIPS * k + p],
            device_id=(x, y, 1 - c), device_id_type=MESH) for k in range(n) for p in range(N_CHIPS)]
        for cp in copies:
            cp.start()
        for cp in copies:
            cp.wait_recv()
        for cp in copies:
            cp.wait_send()

    return pl.pallas_call(
        body, name=name, in_specs=[ANY] * n, out_specs=[ANY] * n,
        out_shape=[jax.ShapeDtypeStruct((N_CHIPS, h, b.shape[2]), b.dtype) for b, h in zip(bufs, halves)],
        scratch_shapes=[pltpu.SemaphoreType.DMA((N_CHIPS * n,)), pltpu.SemaphoreType.DMA((N_CHIPS * n,))],
    )(*bufs)


def _join_halves(bufs, name):
    n = len(bufs)
    halves = [b.shape[0] // 2 for b in bufs]

    def body(*refs):
        outs = refs[n:2 * n]
        send_sems, recv_sems = refs[2 * n:]
        x, y, c = _place()

        def copy(k, core):
            blk = outs[k].at[_half_rows(core, halves[k])]
            return pltpu.make_async_remote_copy(src_ref=blk, dst_ref=blk, send_sem=send_sems.at[k],
                                                recv_sem=recv_sems.at[k], device_id=(x, y, 1 - c),
                                                device_id_type=MESH)

        sends = [copy(k, c) for k in range(n)]
        for cp in sends:
            cp.start()
        for k in range(n):
            copy(k, 1 - c).wait_recv()
        for cp in sends:
            cp.wait_send()

    return pl.pallas_call(
        body, name=name, in_specs=[ANY] * n, out_specs=[ANY] * n,
        out_shape=[jax.ShapeDtypeStruct(b.shape, b.dtype) for b in bufs],
        input_output_aliases={k: k for k in range(n)},
        scratch_shapes=[pltpu.SemaphoreType.DMA((n,)), pltpu.SemaphoreType.DMA((n,))],
    )(*bufs)


def _rs_rows(H, width):
    for cand in range(H, 0, -16):
        if H % cand == 0 and cand % 16 == 0 and cand * width * 4 <= 1536 * 1024:
            return cand
    return H


def _add_sibling(g, got, c, me, name):
    _, H, width = got.shape
    tb = _rs_rows(H, width)
    nb = H // tb

    def body(sp_ref, g_ref, got_ref, sb_ref, sf_ref):
        p = pl.program_id(1)
        s = g_ref[0] + got_ref[0]
        sb_ref[0] = s.astype(BF16)

        @pl.when(p == sp_ref[1])
        def _():
            sf_ref[...] = s

    grid_spec = pltpu.PrefetchScalarGridSpec(
        num_scalar_prefetch=1, grid=(nb, N_CHIPS),
        in_specs=[pl.BlockSpec((1, tb, width), lambda i, p, sp: (p, sp[0] * nb + i, 0)),
                  pl.BlockSpec((1, tb, width), lambda i, p, sp: (p, i, 0))],
        out_specs=[pl.BlockSpec((1, tb, width), lambda i, p, sp: (p, i, 0)),
                   pl.BlockSpec((tb, width), lambda i, p, sp: (i, 0))])
    return pl.pallas_call(
        body, name=name, grid_spec=grid_spec,
        out_shape=[jax.ShapeDtypeStruct((N_CHIPS, H, width), BF16), jax.ShapeDtypeStruct((H, width), F32)],
        compiler_params=_cp(("arbitrary", "arbitrary")),
    )(jnp.stack([c, me]).astype(jnp.int32), g, got)


def _add_chips(sf, got, others_and_c, name):
    H, width = sf.shape
    tb = _rs_rows(H, width)
    nb = H // tb

    def body(sp_ref, sf_ref, r1_ref, r2_ref, r3_ref, out_ref):
        out_ref[...] = ((sf_ref[...] + r1_ref[0].astype(F32)) + r2_ref[0].astype(F32)) + r3_ref[0].astype(F32)

    def slot(k):
        return pl.BlockSpec((1, tb, width), lambda i, sp: (sp[k], i, 0))

    blk = pl.BlockSpec((tb, width), lambda i, sp: (i, 0))
    grid_spec = pltpu.PrefetchScalarGridSpec(
        num_scalar_prefetch=1, grid=(nb,), in_specs=[blk, slot(0), slot(1), slot(2)],
        out_specs=pl.BlockSpec((tb, width), lambda i, sp: (sp[3] * nb + i, 0)))
    return pl.pallas_call(
        body, name=name, grid_spec=grid_spec,
        out_shape=jax.ShapeDtypeStruct((2 * H, width), F32),
        compiler_params=_cp(("arbitrary",)),
    )(others_and_c.astype(jnp.int32), sf, got, got, got)


REPLICATED = ("norm_mix", "norm_ffn", "gla_b_gate_f", "gla_b_gate_b", "gla_norm", "attn_q_norm", "attn_k_norm",
              "ffn_b_conv")


PIECE_ROWS = 16


def _piece_rows(shape):
    n = 1
    for s in shape:
        n *= s
    rows = n // LANES
    return rows, -(-rows // PIECE_ROWS) * PIECE_ROWS


def _pack(pieces, dtype, row_multiple):
    flat = []
    for p in pieces:
        rows, padded = _piece_rows(p.shape)
        flat.append(jnp.pad(p.astype(dtype).reshape(rows, LANES), ((0, padded - rows), (0, 0))))
    rows = sum(f.shape[0] for f in flat)
    padded = -(-rows // row_multiple) * row_multiple
    if padded > rows:
        flat.append(jnp.zeros((padded - rows, LANES), dtype))
    return jnp.concatenate(flat, axis=0)


def _unpack(buf, shapes):
    out, r = [], 0
    for shp in shapes:
        rows, padded = _piece_rows(shp)
        out.append(buf[r:r + rows].reshape(shp))
        r += padded
    return out


def _own_slot(shard2d, me):
    return lax.dynamic_update_index_in_dim(lax.empty((N_CHIPS,) + shard2d.shape, shard2d.dtype), shard2d, me, 0)


def _layer_small(w, l):
    j = l // 2
    if l % 2 == 0:
        return [w["gla_w_gate_up_f"][j], w["gla_w_gate_up_b"][j], w["ffn_w_conv"][l]]
    return [w["ffn_w_conv"][l]]


def _layer_weight_bufs(w, l, me):
    j = l // 2
    mixer = ("gla_w_in", "gla_w_out") if l % 2 == 0 else ("attn_w_qkv", "attn_w_out")
    bufs = [_own_slot(w[n][j].astype(BF16), me) for n in mixer]
    bufs.append(_own_slot(_pack(_layer_small(w, l), F32, 32), me))
    bufs += [_own_slot(w["ffn_w_up"][l].astype(BF16), me), _own_slot(w["ffn_w_down"][l].astype(BF16), me)]
    return bufs


N_MIXER_BUFS = 3


def _layer_weights(w, l, got):
    rows = lambda t: t.reshape(-1, t.shape[2])
    cols = lambda t: jnp.concatenate([t[p] for p in range(N_CHIPS)], axis=1)
    out = {}
    if len(got) != N_MIXER_BUFS:
        up, down = got[-2:]
        out.update(up=up, up_full=cols(up), down=rows(down))
    if len(got) != 2:
        mix_in, mix_out, small = got[:N_MIXER_BUFS]
        shapes = [t.shape for t in _layer_small(w, l)]
        parts = [_unpack(small[p], shapes) for p in range(N_CHIPS)]
        full_small = [jnp.concatenate([parts[p][k] for p in range(N_CHIPS)], axis=-1) for k in range(len(shapes))]
        out.update(conv=full_small[-1])
        if l % 2 == 0:
            out.update(gla_in=jnp.pad(cols(mix_in), ((0, 0), (0, GLA_IN_PAD - GLA_IN))), gla_out=rows(mix_out),
                       gate_f=full_small[0], gate_b=full_small[1])
        else:
            out.update(qkv=mix_in, attn_out=rows(mix_out))
    return out


HBM = pl.BlockSpec(memory_space=pltpu.HBM)
SEM = pl.BlockSpec(memory_space=pltpu.SEMAPHORE)
SIDE_EFFECT = pltpu.SideEffectType.DATAFLOW_SIDE_EFFECTING


def _gather_start(bufs, after, name):
    n = len(bufs)
    halves = [b.shape[1] // 2 for b in bufs]

    def body(*refs):
        refs = refs[:n] + refs[n + 1:]
        send_sems, recv_sems = refs[n:2 * n], refs[2 * n:3 * n]
        outs, token = refs[3 * n:4 * n], refs[4 * n]
        x, y, c = _place()
        me = 2 * x + y
        for k in range(n):
            blk = outs[k].at[me, _half_rows(c, halves[k])]
            for px, py in _other_chips(x, y):
                pltpu.make_async_remote_copy(src_ref=blk, dst_ref=blk, send_sem=send_sems[k], recv_sem=recv_sems[k],
                                             device_id=(px, py, c), device_id_type=MESH).start()
        token[...] = jnp.zeros_like(token)

    res = pl.pallas_call(
        body, name=name,
        in_specs=[HBM] * n + [ANY],
        out_specs=[SEM] * (2 * n) + [HBM] * n + [pl.BlockSpec(memory_space=pltpu.VMEM)],
        out_shape=[pltpu.SemaphoreType.DMA(())] * (2 * n) + [pltpu.HBM(b.shape, b.dtype) for b in bufs]
        + [jax.ShapeDtypeStruct((8, LANES), F32)],
        input_output_aliases={k: 2 * n + k for k in range(n)},
        compiler_params=pltpu.CompilerParams(has_side_effects=SIDE_EFFECT),
    )(*[pltpu.with_memory_space_constraint(b, pltpu.HBM) for b in bufs], after)
    return res[:n], res[n:2 * n], res[2 * n:3 * n], res[3 * n]


def _gather_wait(send_sems, recv_sems, thru, after, name):
    n = len(thru)
    halves = [b.shape[1] // 2 for b in thru]

    def body(*refs):
        ss, rs = refs[n:2 * n], refs[2 * n:3 * n]
        outs = refs[3 * n + 1:]
        x, y, c = _place()
        for k in range(n):
            three = outs[k].at[pl.ds(0, N_CHIPS - 1), _half_rows(c, halves[k])]
            cp = pltpu.make_async_remote_copy(src_ref=three, dst_ref=three, send_sem=ss[k], recv_sem=rs[k],
                                              device_id=(x, y, c), device_id_type=MESH)
            cp.wait_send()
            cp.wait_recv()

    return pl.pallas_call(
        body, name=name,
        in_specs=[HBM] * n + [SEM] * (2 * n) + [ANY],
        out_specs=[HBM] * n,
        out_shape=[pltpu.HBM(b.shape, b.dtype) for b in thru],
        input_output_aliases={k: k for k in range(n)},
        compiler_params=pltpu.CompilerParams(has_side_effects=SIDE_EFFECT),
    )(*thru, *send_sems, *recv_sems, after)


def _send_start(sbs, name):
    n = len(sbs)

    def body(*refs):
        send_sems, recv_sems = refs[2 * n:3 * n], refs[3 * n:4 * n]
        srcs, lands, token = refs[4 * n:5 * n], refs[5 * n:6 * n], refs[6 * n]
        x, y, c = _place()
        me = 2 * x + y
        for k in range(n):
            for px, py in _other_chips(x, y):
                pltpu.make_async_remote_copy(src_ref=srcs[k].at[2 * px + py], dst_ref=lands[k].at[me],
                                             send_sem=send_sems[k], recv_sem=recv_sems[k],
                                             device_id=(px, py, c), device_id_type=MESH).start()
        token[...] = jnp.zeros_like(token)

    hbm = lambda a: pltpu.with_memory_space_constraint(a, pltpu.HBM)
    res = pl.pallas_call(
        body, name=name,
        in_specs=[HBM] * (2 * n),
        out_specs=[SEM] * (2 * n) + [HBM] * (2 * n) + [pl.BlockSpec(memory_space=pltpu.VMEM)],
        out_shape=[pltpu.SemaphoreType.DMA(())] * (2 * n) + [pltpu.HBM(s.shape, s.dtype) for s in sbs] * 2
        + [jax.ShapeDtypeStruct((8, LANES), F32)],
        input_output_aliases={k: 2 * n + k for k in range(2 * n)},
        compiler_params=pltpu.CompilerParams(has_side_effects=SIDE_EFFECT),
    )(*[hbm(s) for s in sbs], *[hbm(lax.empty(s.shape, s.dtype)) for s in sbs])
    return res[:n], res[n:2 * n], res[2 * n:3 * n], res[3 * n:4 * n], res[4 * n]


def _send_wait(send_sems, recv_sems, srcs, lands, after, name):
    n = len(srcs)

    def body(*refs):
        ss, rs = refs[2 * n:3 * n], refs[3 * n:4 * n]
        s_out, l_out = refs[4 * n + 1:5 * n + 1], refs[5 * n + 1:]
        x, y, c = _place()
        for k in range(n):
            cp = pltpu.make_async_remote_copy(src_ref=s_out[k].at[pl.ds(0, N_CHIPS - 1)],
                                              dst_ref=l_out[k].at[pl.ds(0, N_CHIPS - 1)], send_sem=ss[k],
                                              recv_sem=rs[k], device_id=(x, y, c), device_id_type=MESH)
            cp.wait_send()
            cp.wait_recv()

    res = pl.pallas_call(
        body, name=name,
        in_specs=[HBM] * (2 * n) + [SEM] * (2 * n) + [ANY],
        out_specs=[HBM] * (2 * n),
        out_shape=[pltpu.HBM(s.shape, s.dtype) for s in srcs] * 2,
        input_output_aliases={k: k for k in range(2 * n)},
        compiler_params=pltpu.CompilerParams(has_side_effects=SIDE_EFFECT),
    )(*srcs, *lands, *send_sems, *recv_sems, after)
    return res[n:]


def _pass_to_sibling(bufs, name):
    n = len(bufs)
    halves = [b.shape[1] // 2 for b in bufs]

    def body(*refs):
        outs = refs[n:2 * n]
        send_sems, recv_sems = refs[2 * n:]
        x, y, c = _place()
        chips = _other_chips(x, y)

        def copy(k, j, core):
            px, py = chips[j]
            blk = outs[k].at[2 * px + py, _half_rows(core, halves[k])]
            return pltpu.make_async_remote_copy(src_ref=blk, dst_ref=blk, send_sem=send_sems.at[3 * k + j],
                                                recv_sem=recv_sems.at[3 * k + j], device_id=(x, y, 1 - c),
                                                device_id_type=MESH)

        sends = [copy(k, j, c) for k in range(n) for j in range(3)]
        for cp in sends:
            cp.start()
        for k in range(n):
            for j in range(3):
                copy(k, j, 1 - c).wait_recv()
        for cp in sends:
            cp.wait_send()

    return pl.pallas_call(
        body, name=name,
        in_specs=[ANY] * n, out_specs=[ANY] * n,
        out_shape=[jax.ShapeDtypeStruct(b.shape, b.dtype) for b in bufs],
        input_output_aliases={k: k for k in range(n)},
        scratch_shapes=[pltpu.SemaphoreType.DMA((3 * n,)), pltpu.SemaphoreType.DMA((3 * n,))],
    )(*bufs)


def _rope_tables(S):
    rows = S // GRID_W
    row_idx = jnp.repeat(jnp.arange(rows, dtype=F32), GRID_W)
    col_idx = jnp.tile(jnp.arange(GRID_W, dtype=F32), rows)
    pairs = ATT_HD // 4
    inv_freq = ROPE_THETA ** (-jnp.arange(pairs, dtype=F32) / pairs)
    ang = jnp.concatenate([row_idx[:, None] * inv_freq, col_idx[:, None] * inv_freq], axis=-1)
    cos, sin = jnp.cos(ang), jnp.sin(ang)
    return jnp.concatenate([cos, cos], axis=-1), jnp.concatenate([-sin, sin], axis=-1)


def _gate_rows(w, first_row):
    return jnp.zeros((LANES, GLA_KEY), F32).at[first_row:first_row + GLA_RANK].set(w.astype(F32))


def _local_step(x, target, weights_of, grads_out, P):
    S = x.shape[0]
    rc, rs = _rope_tables(S)
    row = lambda a: a.reshape(1, -1)
    saved = []
    for i in range(DEPTH):
        j = i // 2
        W = dict(weights_of(i, "mix", x))
        nm = row(P["norm_mix"][i])
        h1 = _rmsnorm_fwd(x, nm, f"norm_mix_fwd{i}")
        if i % 2 == 0:
            wgf = _gate_rows(W["gate_f"], 0)
            wgb = _gate_rows(W["gate_b"], GLA_RANK)
            bgf, bgb = row(P["gla_b_gate_f"][j]), row(P["gla_b_gate_b"][j])
            gn = row(P["gla_norm"][j])
            proj = _matmul_rows(h1, W["gla_in"], f"gla_in{i}")
            laf, lab = _gla_gate_fwd(proj, wgf, bgf, wgb, bgb, f"gla_gate_fwd{i}")
            of, stf = _gla_scan_fwd(proj, laf, False, f"gla_scan_f_fwd{i}")
            ob, stb = _gla_scan_fwd(proj, lab, True, f"gla_scan_b_fwd{i}")
            z = _gla_out_fwd(of, ob, proj, gn, f"gla_out_fwd{i}")
            xm = _matmul_rows(z, W["gla_out"], f"gla_outproj{i}", res=x)
            mix = dict(proj=proj, laf=laf, lab=lab, of=of, ob=ob, stf=stf, stb=stb, z=z, wgf=wgf, wgb=wgb)
        else:
            proj = _matmul_rows(h1, W["qkv"], f"attn_qkv{i}", w_layer=0)
            qn, kn = row(P["attn_q_norm"][j]), row(P["attn_k_norm"][j])
            qk, vb, kt, vt = _qk_prep_fwd(proj, qn, kn, rc, rs, f"qk_prep_fwd{i}")
            o, lse = _attn_fwd(qk, vt, f"attn_fwd{i}")
            xm = _matmul_rows(o, W["attn_out"], f"attn_outproj{i}", res=x)
            mix = dict(proj=proj, qk=qk, vb=vb, kt=kt, o=o, lse=lse)
        W.update(weights_of(i, "ffn", xm))
        h2 = _rmsnorm_fwd(xm, row(P["norm_ffn"][i]), f"norm_ffn_fwd{i}")
        a, uv, ug = _ffn_mid_fwd(h2, W["up_full"], W["conv"], row(P["ffn_b_conv"][i]), f"ffn_mid_fwd{i}")
        xo = _matmul_rows(a, W["down"], f"ffn_down{i}", res=xm)
        saved.append(dict(x=x, h1=h1, xm=xm, h2=h2, uv=uv, ug=ug, mix=mix, W=W))
        x = xo

    dx, dxb, loss = _loss_grad(x, target, "loss")

    G = {n: [None] * (DEPTH if n.startswith(("norm", "ffn")) else DEPTH // 2) for n in REPLICATED}
    token = None
    for i in reversed(range(DEPTH)):
        j = i // 2
        sv = saved[i]
        mix = sv["mix"]
        W = sv["W"]
        bconv = row(P["ffn_b_conv"][i])
        if token is not None:
            t = token[0:1, 0:1]
            bconv = jnp.where(t == 0.0, bconv, t)
        duv, dug, a, gwv, gwg = _ffn_mid_bwd(dxb, W["down"], sv["uv"], sv["ug"], W["conv"], bconv, f"ffn_mid_bwd{i}")
        L = dict(down=_wgrad(a, dxb, f"ffn_down_wgrad{i}", chips="rows"),
                 up=_wgrad(sv["h2"], (duv, dug), f"ffn_up_wgrad{i}", chips="cols"),
                 small=[jnp.concatenate([gwv[:3], gwg[:3]], axis=1)])
        G["ffn_b_conv"][i] = jnp.concatenate([gwv[3], gwg[3]], axis=0)
        dxm, dxmb, dn = _dgrad_norm((duv, dug), W["up"], sv["xm"], row(P["norm_ffn"][i]), dx, f"ffn_up_dgrad{i}",
                                    w_layer=0)
        G["norm_ffn"][i] = dn[0]
        if i % 2 == 0:
            proj = mix["proj"]
            bgf, bgb = row(P["gla_b_gate_f"][j]), row(P["gla_b_gate_b"][j])
            gn = row(P["gla_norm"][j])
            dz = _matmul_rows(dxmb, W["gla_out"], f"gla_outproj_dgrad{i}", transposed=True)
            L["out"] = _wgrad(mix["z"], dxmb, f"gla_outproj_wgrad{i}", chips="rows")
            do, dg, dgn = _gla_out_bwd(dz, mix["of"], mix["ob"], proj, gn, f"gla_out_bwd{i}")
            G["gla_norm"][j] = dgn[0]
            dqf, dkf, dvf, dlaf = _gla_scan_bwd(do, proj, mix["laf"], mix["stf"], False, f"gla_scan_f_bwd{i}")
            dqb, dkb, dvb, dlab = _gla_scan_bwd(do, proj, mix["lab"], mix["stb"], True, f"gla_scan_b_bwd{i}")
            dr, dwf, dbf, dwb, dbb = _gla_gate_bwd(dlaf, dlab, proj, mix["wgf"], bgf, mix["wgb"], bgb,
                                                   f"gla_gate_bwd{i}")
            L["small"] = [dwf[:GLA_RANK], dwb[GLA_RANK:2 * GLA_RANK]] + L["small"]
            G["gla_b_gate_f"][j] = dbf[0]
            G["gla_b_gate_b"][j] = dbb[0]
            dproj = jnp.concatenate([dqf + dqb, dkf + dkb, dvf + dvb, dg, dr], axis=1).astype(BF16)
            L["mix_in"] = _wgrad(sv["h1"], dproj, f"gla_in_wgrad{i}")
            dx, dxb, dn = _dgrad_norm(dproj, W["gla_in"], sv["x"], row(P["norm_mix"][i]), dxm, f"mix_in_dgrad{i}")
        else:
            proj = mix["proj"]
            qn, kn = row(P["attn_q_norm"][j]), row(P["attn_k_norm"][j])
            do = _matmul_rows(dxmb, W["attn_out"], f"attn_outproj_dgrad{i}", transposed=True)
            L["out"] = _wgrad(mix["o"], dxmb, f"attn_outproj_wgrad{i}", chips="rows")
            dq, dk, dv = _attn_bwd(do, mix["o"], mix["lse"], mix["qk"], mix["vb"], mix["kt"], f"attn_bwd{i}")
            dqk = jnp.concatenate([dq, dk], axis=1)
            dpqk, dqn, dkn = _qk_prep_bwd(dqk, proj, qn, kn, rc, rs, f"qk_prep_bwd{i}")
            G["attn_q_norm"][j] = dqn[0]
            G["attn_k_norm"][j] = dkn[0]
            dproj = jnp.concatenate([dpqk, dv], axis=1).astype(BF16)
            L["mix_in"] = _wgrad(sv["h1"], dproj, f"attn_qkv_wgrad{i}", chips="cols")
            dx, dxb, dn = _dgrad_norm(dproj, W["qkv"], sv["x"], row(P["norm_mix"][i]), dxm, f"mix_in_dgrad{i}",
                                      w_layer=0)
        G["norm_mix"][i] = dn[0]
        token = grads_out(i, L, G, loss)
    return loss, dx, G


def kernel(x, norm_mix, norm_ffn, gla_w_in, gla_w_gate_up_f, gla_b_gate_f, gla_w_gate_up_b, gla_b_gate_b, gla_norm, gla_w_out, attn_w_qkv, attn_q_norm, attn_k_norm, attn_w_out, ffn_w_up, ffn_w_conv, ffn_b_conv, ffn_w_down, loss_target, m_norm_mix, m_norm_ffn, m_gla_w_in, m_gla_w_gate_up_f, m_gla_b_gate_f, m_gla_w_gate_up_b, m_gla_b_gate_b, m_gla_norm, m_gla_w_out, m_attn_w_qkv, m_attn_q_norm, m_attn_k_norm, m_attn_w_out, m_ffn_w_up, m_ffn_w_conv, m_ffn_b_conv, m_ffn_w_down, v_norm_mix, v_norm_ffn, v_gla_w_in, v_gla_w_gate_up_f, v_gla_b_gate_f, v_gla_w_gate_up_b, v_gla_b_gate_b, v_gla_norm, v_gla_w_out, v_attn_w_qkv, v_attn_q_norm, v_attn_k_norm, v_attn_w_out, v_ffn_w_up, v_ffn_w_conv, v_ffn_b_conv, v_ffn_w_down):
    names = ("norm_mix", "norm_ffn", "gla_w_in", "gla_w_gate_up_f", "gla_b_gate_f", "gla_w_gate_up_b",
             "gla_b_gate_b", "gla_norm", "gla_w_out", "attn_w_qkv", "attn_q_norm", "attn_k_norm", "attn_w_out",
             "ffn_w_up", "ffn_w_conv", "ffn_b_conv", "ffn_w_down")
    w = dict(zip(names, (norm_mix, norm_ffn, gla_w_in, gla_w_gate_up_f, gla_b_gate_f, gla_w_gate_up_b,
                         gla_b_gate_b, gla_norm, gla_w_out, attn_w_qkv, attn_q_norm, attn_k_norm, attn_w_out,
                         ffn_w_up, ffn_w_conv, ffn_b_conv, ffn_w_down)))
    m = dict(zip(names, (m_norm_mix, m_norm_ffn, m_gla_w_in, m_gla_w_gate_up_f, m_gla_b_gate_f,
                         m_gla_w_gate_up_b, m_gla_b_gate_b, m_gla_norm, m_gla_w_out, m_attn_w_qkv, m_attn_q_norm,
                         m_attn_k_norm, m_attn_w_out, m_ffn_w_up, m_ffn_w_conv, m_ffn_b_conv, m_ffn_w_down)))
    v = dict(zip(names, (v_norm_mix, v_norm_ffn, v_gla_w_in, v_gla_w_gate_up_f, v_gla_b_gate_f,
                         v_gla_w_gate_up_b, v_gla_b_gate_b, v_gla_norm, v_gla_w_out, v_attn_w_qkv, v_attn_q_norm,
                         v_attn_k_norm, v_attn_w_out, v_ffn_w_up, v_ffn_w_conv, v_ffn_b_conv, v_ffn_w_down)))
    px, py, pc = _place()
    me = 2 * px + py

    started, token = [], w["norm_mix"]
    for l in range(DEPTH):
        started.append(_gather_start(_layer_weight_bufs(w, l, me), token, f"gather_start{l}"))
        token = started[-1][3]
    fetched = {}

    def weights_of(l, part, after):
        send_sems, recv_sems, thru, _ = started[l]
        if l == 0:
            pick = slice(0, N_MIXER_BUFS) if part == "mix" else slice(N_MIXER_BUFS, None)
            landed = _gather_wait(send_sems[pick], recv_sems[pick], thru[pick], token if part == "mix" else after,
                                  f"gather_wait{l}_{part}")
            return _layer_weights(w, l, _pass_to_sibling(landed, f"gather_pass{l}_{part}"))
        if part == "mix":
            landed = _gather_wait(send_sems, recv_sems, thru, after, f"gather_wait{l}")
            fetched[l] = _layer_weights(w, l, _pass_to_sibling(landed, f"gather_pass{l}"))
        return fetched[l]

    sent = {}

    def grads_out(l, L, G, loss_part):
        mix_in = L["mix_in"]
        if l % 2 == 0:
            width = w["gla_w_in"].shape[2]
            mix_in = jnp.stack([mix_in[:, p * width:(p + 1) * width] for p in range(N_CHIPS)])
        cut = lambda t, p: lax.slice_in_dim(t, p * (t.shape[-1] // N_CHIPS), (p + 1) * (t.shape[-1] // N_CHIPS),
                                            axis=t.ndim - 1)
        small = jnp.stack([_pack([cut(t, p) for t in L["small"]], F32, 32) for p in range(N_CHIPS)])
        if l == 0:
            packed = _pack([jnp.stack(G[n]) for n in REPLICATED] + [loss_part], F32, 16)
            sent["small_sum"], tok = _allreduce_small(packed, "small_allreduce")
            small = jnp.where(tok[0:1, 0:1] == 0.0, small, tok[0:1, 0:1])
        bufs = [mix_in, L["out"], small, L["up"], L["down"]]
        gots = _swap_other_half(bufs, f"grads{l}_to_sibling")
        sums = [_add_sibling(b, g, pc, me, f"grads{l}_add_sibling{k}") for k, (b, g) in enumerate(zip(bufs, gots))]
        send_sems, recv_sems, srcs, lands, tok = _send_start([s[0] for s in sums], f"grads{l}_start")
        sent[l] = (send_sems, recv_sems, srcs, lands, [s[1] for s in sums])
        sent["last_token"] = tok
        return tok

    P = {n: w[n] for n in REPLICATED}

    loss_part, dx, grads = _local_step(x[0], loss_target[0], weights_of, grads_out, P)

    small_sum = sent["small_sum"]
    others_and_c = jnp.stack([jnp.where(me <= k, k + 1, k) for k in range(N_CHIPS - 1)] + [pc])
    mine, after = {}, sent["last_token"]
    for l in reversed(range(DEPTH)):
        send_sems, recv_sems, srcs, lands, own = sent[l]
        landed = _send_wait(send_sems, recv_sems, srcs, lands, after, f"grads{l}_wait")
        halves = [_add_chips(own[k], landed[k], others_and_c, f"grads{l}_add_chips{k}") for k in range(len(own))]
        mine[l] = _join_halves(halves, f"grads{l}_join_halves")
        after = mine[l][0]
    gsh = {}
    for n, k, layers in (("ffn_w_up", 3, range(DEPTH)), ("ffn_w_down", 4, range(DEPTH)),
                         ("gla_w_in", 0, range(0, DEPTH, 2)), ("gla_w_out", 1, range(0, DEPTH, 2)),
                         ("attn_w_qkv", 0, range(1, DEPTH, 2)), ("attn_w_out", 1, range(1, DEPTH, 2))):
        gsh[n] = jnp.stack([mine[l][k] for l in layers])
    small_mine = [_unpack(mine[l][2], [t.shape for t in _layer_small(w, l)]) for l in range(DEPTH)]
    gsh["ffn_w_conv"] = jnp.stack([small_mine[l][-1] for l in range(DEPTH)])
    gsh["gla_w_gate_up_f"] = jnp.stack([small_mine[l][0] for l in range(0, DEPTH, 2)])
    gsh["gla_w_gate_up_b"] = jnp.stack([small_mine[l][1] for l in range(0, DEPTH, 2)])

    parts = _unpack(small_sum, [w[n].shape for n in REPLICATED] + [(1, LANES)])
    gsh.update(dict(zip(REPLICATED, parts[:-1])))
    loss = parts[-1][0, 0]

    delta, new_m, new_v = {}, {}, {}
    for n in names:
        shp = w[n].shape
        two_d = (-1, shp[-1])
        d, nm, nv = _adamw(w[n].reshape(two_d), gsh[n].reshape(two_d), m[n].reshape(two_d), v[n].reshape(two_d),
                           f"adamw_{n}")
        delta[n], new_m[n], new_v[n] = d.reshape(shp), nm.reshape(shp), nv.reshape(shp)

    return (loss, dx[None], *[gsh[n] for n in names], *[delta[n] for n in names],
            *[new_m[n] for n in names], *[new_v[n] for n in names])
```

```python
import jax
import jax.numpy as jnp
from jax import lax
from jax.experimental import pallas as pl
from jax.experimental.pallas import tpu as pltpu

F32 = jnp.float32
BF16 = jnp.bfloat16
MESH = pl.DeviceIdType.MESH
HIGHEST = lax.Precision.HIGHEST

D_MODEL = 1024
DEPTH = 4
GRID_W = 64
NORM_EPS = 1e-6
GLA_HEADS = 4
GLA_DK = 128
GLA_DV = 256
GLA_KEY = GLA_HEADS * GLA_DK
GLA_VAL = GLA_HEADS * GLA_DV
GLA_RANK = 16
GLA_CHUNK = 64
GLA_GATE_NORMALIZER = 16.0
GLA_IN = 2 * GLA_KEY + 2 * GLA_VAL + 2 * GLA_RANK
GLA_IN_PAD = 3200
GLA_R_BLOCK = (2 * GLA_KEY + 2 * GLA_VAL) // 128
ATT_HD = 128
ATT_QH = 8
ATT_KVH = 2
ATT_GROUP = ATT_QH // ATT_KVH
ATT_QKV = (ATT_QH + 2 * ATT_KVH) * ATT_HD
ROPE_THETA = 10000.0
D_FF = 2816
ADAM_LR = 0.001
ADAM_B1 = 0.9
ADAM_B2 = 0.999
ADAM_EPS = 1e-08
ADAM_WD = 0.01
ADAM_STEP = 10

N_CHIPS = 4
LANES = 128
VMEM_LIMIT = 56 * 1024 * 1024


def _cp(sem):
    return pltpu.CompilerParams(dimension_semantics=sem, vmem_limit_bytes=VMEM_LIMIT)


def _pick(n, cands):
    for c in cands:
        if n % c == 0:
            return c
    return n


def _dg(a, b, ca, cb):
    return lax.dot_general(a, b, (((ca,), (cb,)), ((), ())), preferred_element_type=F32)


def _sigmoid(x):
    return 0.5 * jnp.tanh(0.5 * x) + 0.5


def _rmsnorm_fwd(x, w, name):
    S, D = x.shape
    tm = _pick(S, (512, 256))

    def body(x_ref, w_ref, h_ref):
        xv = x_ref[...]
        r = lax.rsqrt(jnp.mean(xv * xv, axis=-1, keepdims=True) + NORM_EPS)
        h_ref[...] = (xv * r * w_ref[...]).astype(BF16)

    return pl.pallas_call(
        body, name=name, grid=(S // tm,),
        in_specs=[pl.BlockSpec((tm, D), lambda i: (i, 0)), pl.BlockSpec((1, D), lambda i: (0, 0))],
        out_specs=pl.BlockSpec((tm, D), lambda i: (i, 0)),
        out_shape=jax.ShapeDtypeStruct((S, D), BF16),
        compiler_params=_cp(("parallel",)),
    )(x, w)


def _loss_grad(y, t, name):
    S, D = y.shape
    tm = _pick(S, (512, 256))

    def body(y_ref, t_ref, dy_ref, dyb_ref, loss_ref):
        i = pl.program_id(0)
        d = y_ref[...] - t_ref[...]
        dy = d * (1.0 / D)
        dy_ref[...] = dy
        dyb_ref[...] = dy.astype(BF16)
        sq = jnp.sum(jnp.sum(d * d, axis=1, keepdims=True), axis=0, keepdims=True)
        part = jnp.broadcast_to(sq * (0.5 / D), (1, LANES))

        @pl.when(i == 0)
        def _():
            loss_ref[...] = part

        @pl.when(i > 0)
        def _():
            loss_ref[...] += part

    return pl.pallas_call(
        body, name=name, grid=(S // tm,),
        in_specs=[pl.BlockSpec((tm, D), lambda i: (i, 0)), pl.BlockSpec((tm, D), lambda i: (i, 0))],
        out_specs=[pl.BlockSpec((tm, D), lambda i: (i, 0)), pl.BlockSpec((tm, D), lambda i: (i, 0)),
                   pl.BlockSpec((1, LANES), lambda i: (0, 0))],
        out_shape=[jax.ShapeDtypeStruct((S, D), F32), jax.ShapeDtypeStruct((S, D), BF16),
                   jax.ShapeDtypeStruct((1, LANES), F32)],
        compiler_params=_cp(("arbitrary",)),
    )(y, t)


WGRAD_TK = 512


def _wgrad(a, b, name, chips=None):
    S, Kw = a.shape
    pair = isinstance(b, (tuple, list))
    tn = b[0].shape[1] if pair else b.shape[1]
    N = 2 * tn if pair else tn
    tk = _pick(S, (WGRAD_TK, 256, 128))
    nk = S // tk
    if pair:
        b_specs = [pl.BlockSpec((tk, tn), lambda j, k: (jnp.where(j == 0, k, nk - 1), 0)),
                   pl.BlockSpec((tk, tn), lambda j, k: (jnp.where(j == 1, k, 0), 0))]
    else:
        b_specs = [pl.BlockSpec((tk, tn), lambda j, k: (k, 0))]
    if chips == "cols":
        cw = N // N_CHIPS
        span = tn // cw
        o_spec = pl.BlockSpec((span, Kw, cw), lambda j, k: (j, 0, 0))
        out_shape = jax.ShapeDtypeStruct((N_CHIPS, Kw, cw), F32)
    elif chips == "rows":
        assert not pair
        o_spec = pl.BlockSpec((N_CHIPS, Kw // N_CHIPS, N), lambda j, k: (0, 0, 0))
        out_shape = jax.ShapeDtypeStruct((N_CHIPS, Kw // N_CHIPS, N), F32)
    else:
        assert not pair
        o_spec = pl.BlockSpec((Kw, N), lambda j, k: (0, 0))
        out_shape = jax.ShapeDtypeStruct((Kw, N), F32)
    nb = len(b_specs)

    def body(*refs):
        a_ref, b_refs, o_ref, acc = refs[0], refs[1:1 + nb], refs[-2], refs[-1]
        j = pl.program_id(0)
        k = pl.program_id(1)

        @pl.when(k == 0)
        def _():
            acc[...] = jnp.zeros_like(acc)

        av = a_ref[...].astype(BF16)
        for h in range(nb):
            @pl.when(j == h)
            def _():
                acc[...] += _dg(av, b_refs[h][...].astype(BF16), 0, 0)

        @pl.when(k == nk - 1)
        def _():
            v = acc[...]
            if chips == "cols":
                for s in range(span):
                    o_ref[s] = v[:, s * cw:(s + 1) * cw]
            elif chips == "rows":
                rows = Kw // N_CHIPS
                for p in range(N_CHIPS):
                    o_ref[p] = v[p * rows:(p + 1) * rows, :]
            else:
                o_ref[...] = v

    return pl.pallas_call(
        body, name=name, grid=(nb, nk),
        in_specs=[pl.BlockSpec((tk, Kw), lambda j, k: (k, 0))] + b_specs,
        out_specs=o_spec, out_shape=out_shape,
        scratch_shapes=[pltpu.VMEM((Kw, tn), F32)],
        compiler_params=_cp(("parallel", "arbitrary")),
    )(a, *(tuple(b) if pair else (b,)))


def _matmul_rows(a, w, name, res=None, w_layer=None, transposed=False):
    M, K = a.shape
    if w_layer is not None:
        cw = w.shape[2]
        N = N_CHIPS * cw
        w_spec = pl.BlockSpec((N_CHIPS, K, cw), lambda i: (0, w_layer, 0))
    else:
        N = w.shape[0] if transposed else w.shape[1]
        assert w.shape[1 if transposed else 0] == K
        w_spec = pl.BlockSpec(w.shape, lambda i: (0, 0))
    tm = _pick(M, (512, 256, 128))
    has_res = res is not None

    def body(*refs):
        a_ref, w_ref = refs[0], refs[1]
        r_ref = refs[2] if has_res else None
        o_ref = refs[-1]
        av = a_ref[...].astype(BF16)
        if w_layer is not None:
            for p in range(N_CHIPS):
                o_ref[:, pl.ds(p * cw, cw)] = jnp.dot(av, w_ref[p], preferred_element_type=F32)
        else:
            v = _dg(av, w_ref[...], 1, 1 if transposed else 0)
            o_ref[...] = v + r_ref[...] if has_res else v

    row = pl.BlockSpec((tm, N), lambda i: (i, 0))
    return pl.pallas_call(
        body, name=name, grid=(M // tm,),
        in_specs=[pl.BlockSpec((tm, K), lambda i: (i, 0)), w_spec] + ([row] if has_res else []),
        out_specs=row, out_shape=jax.ShapeDtypeStruct((M, N), F32),
        compiler_params=_cp(("parallel",)),
    )(*((a, w) + ((res,) if has_res else ())))


def _dgrad_norm(dy, w, x, wn, dres, name, w_layer=None):
    pair = isinstance(dy, (tuple, list))
    M = dy[0].shape[0] if pair else dy.shape[0]
    Kp = 2 * dy[0].shape[1] if pair else dy.shape[1]
    D = x.shape[1]
    if w_layer is not None:
        cw = w.shape[2]
        assert N_CHIPS * cw == Kp and w.shape[1] % D == 0
        w_spec = pl.BlockSpec((N_CHIPS, D, cw), lambda i: (0, w_layer, 0))
    else:
        assert w.shape == (D, Kp)
        w_spec = pl.BlockSpec((D, Kp), lambda i: (0, 0))
    tm = _pick(M, (256, 128))
    width = Kp // 2 if pair else Kp
    dy_specs = [pl.BlockSpec((tm, width), lambda i: (i, 0))] * (2 if pair else 1)
    nd = len(dy_specs)

    def body(*refs):
        dy_refs = refs[:nd]
        w_ref, x_ref, wn_ref, dres_ref, dx_ref, dxb_ref, dwn_ref = refs[nd:]
        i = pl.program_id(0)
        if w_layer is not None:
            dh = None
            for p in range(N_CHIPS):
                src, off = divmod(p * cw, width)
                part = _dg(dy_refs[src][:, pl.ds(off, cw)], w_ref[p], 1, 1)
                dh = part if dh is None else dh + part
        else:
            dh = _dg(dy_refs[0][...], w_ref[...], 1, 1)
        xv = x_ref[...]
        r = lax.rsqrt(jnp.mean(xv * xv, axis=-1, keepdims=True) + NORM_EPS)
        yv = xv * r
        dyv = dh * wn_ref[...]
        dxv = r * (dyv - yv * jnp.mean(dyv * yv, axis=-1, keepdims=True)) + dres_ref[...]
        dx_ref[...] = dxv
        dxb_ref[...] = dxv.astype(BF16)
        part = jnp.sum(dh * yv, axis=0, keepdims=True)

        @pl.when(i == 0)
        def _():
            dwn_ref[...] = part

        @pl.when(i > 0)
        def _():
            dwn_ref[...] += part

    row = pl.BlockSpec((tm, D), lambda i: (i, 0))
    one = pl.BlockSpec((1, D), lambda i: (0, 0))
    return pl.pallas_call(
        body, name=name, grid=(M // tm,),
        in_specs=dy_specs + [w_spec, row, one, row],
        out_specs=[row, row, one],
        out_shape=[jax.ShapeDtypeStruct((M, D), F32), jax.ShapeDtypeStruct((M, D), BF16),
                   jax.ShapeDtypeStruct((1, D), F32)],
        compiler_params=_cp(("arbitrary",)),
    )(*(tuple(dy) if pair else (dy,)), w, x, wn, dres)


FFN_TN_FWD = 256
FFN_TN_BWD = 128
FFN_ROWS = 256
PAD = 8


def _conv3(pad_ref, w, r0, tr):
    um = pad_ref[pl.ds(PAD - 1 + r0, tr), :]
    uc = pad_ref[pl.ds(PAD + r0, tr), :]
    up = pad_ref[pl.ds(PAD + 1 + r0, tr), :]
    return w[0:1, :] * um + w[1:2, :] * uc + w[2:3, :] * up, (um, uc, up)


def _zero_pads(pad_ref, S, tn):
    pad_ref[pl.ds(0, PAD), :] = jnp.zeros((PAD, tn), F32)
    pad_ref[pl.ds(PAD + S, PAD), :] = jnp.zeros((PAD, tn), F32)


def _ffn_mid_fwd(h, wup, wconv, bconv, name):
    S, D = h.shape
    F = wup.shape[1] // 2
    tn = FFN_TN_FWD
    nb = F // tn
    tr = min(FFN_ROWS, S)

    def body(h_ref, wv_ref, wg_ref, cv_ref, cg_ref, bv_ref, bg_ref, a_ref, uv_ref, ug_ref):
        _zero_pads(uv_ref, S, tn)
        _zero_pads(ug_ref, S, tn)
        hv = h_ref[...]
        uv_ref[pl.ds(PAD, S), :] = jnp.dot(hv, wv_ref[...], preferred_element_type=F32)
        ug_ref[pl.ds(PAD, S), :] = jnp.dot(hv, wg_ref[...], preferred_element_type=F32)
        cwv, cwg, bv, bg = cv_ref[...], cg_ref[...], bv_ref[...], bg_ref[...]
        for r0 in range(0, S, tr):
            cv = _conv3(uv_ref, cwv, r0, tr)[0] + bv
            cg = _conv3(ug_ref, cwg, r0, tr)[0] + bg
            a_ref[pl.ds(r0, tr), :] = (cg * _sigmoid(cg) * cv).astype(BF16)

    col = lambda off: (lambda j: (0, j + off))
    padded = pl.BlockSpec((S + 2 * PAD, tn), col(0))
    return pl.pallas_call(
        body, name=name, grid=(nb,),
        in_specs=[pl.BlockSpec((S, D), lambda j: (0, 0)),
                  pl.BlockSpec((D, tn), col(0)), pl.BlockSpec((D, tn), col(nb)),
                  pl.BlockSpec((3, tn), col(0)), pl.BlockSpec((3, tn), col(nb)),
                  pl.BlockSpec((1, tn), col(0)), pl.BlockSpec((1, tn), col(nb))],
        out_specs=[pl.BlockSpec((S, tn), col(0)), padded, padded],
        out_shape=[jax.ShapeDtypeStruct((S, F), BF16), jax.ShapeDtypeStruct((S + 2 * PAD, F), F32),
                   jax.ShapeDtypeStruct((S + 2 * PAD, F), F32)],
        compiler_params=_cp(("parallel",)),
    )(h, wup, wup, wconv, wconv, bconv, bconv)


def _rows8(rows):
    n = rows[0].shape[1]
    idx = lax.broadcasted_iota(jnp.int32, (8, n), 0)
    out = jnp.zeros((8, n), F32)
    for k, r in enumerate(rows):
        out = jnp.where(idx == k, r, out)
    return out


def _ffn_mid_bwd(dyb, wdown, uv, ug, wconv, bconv, name):
    S, D = dyb.shape
    F = wdown.shape[0]
    tn = FFN_TN_BWD
    nb = F // tn
    tr = min(FFN_ROWS, S)

    def body(dy_ref, wd_ref, uv_ref, ug_ref, cv_ref, cg_ref, bv_ref, bg_ref,
             duv_ref, dug_ref, a_ref, gwv_ref, gwg_ref, pdv, pdg):
        for p in (pdv, pdg):
            _zero_pads(p, S, tn)
        wd = wd_ref[...]
        cwv, cwg, bv, bg = cv_ref[...], cg_ref[...], bv_ref[...], bg_ref[...]
        zero = jnp.zeros((1, tn), F32)
        gv = [zero, zero, zero, zero]
        gg = [zero, zero, zero, zero]
        for r0 in range(0, S, tr):
            cv, shv = _conv3(uv_ref, cwv, r0, tr)
            cg, shg = _conv3(ug_ref, cwg, r0, tr)
            cv = cv + bv
            cg = cg + bg
            sg = _sigmoid(cg)
            sl = cg * sg
            a_ref[pl.ds(r0, tr), :] = (sl * cv).astype(BF16)
            da = _dg(dy_ref[pl.ds(r0, tr), :], wd, 1, 1)
            dcv = da * sl
            dcg = da * cv * (sg * (1.0 + cg * (1.0 - sg)))
            pdv[pl.ds(PAD + r0, tr), :] = dcv
            pdg[pl.ds(PAD + r0, tr), :] = dcg
            for k in range(3):
                gv[k] = gv[k] + jnp.sum(dcv * shv[k], axis=0, keepdims=True)
                gg[k] = gg[k] + jnp.sum(dcg * shg[k], axis=0, keepdims=True)
            gv[3] = gv[3] + jnp.sum(dcv, axis=0, keepdims=True)
            gg[3] = gg[3] + jnp.sum(dcg, axis=0, keepdims=True)
        gwv_ref[...] = _rows8(gv)
        gwg_ref[...] = _rows8(gg)
        for r0 in range(0, S, tr):
            for pd, cw, out in ((pdv, cwv, duv_ref), (pdg, cwg, dug_ref)):
                dm = pd[pl.ds(PAD - 1 + r0, tr), :]
                dc = pd[pl.ds(PAD + r0, tr), :]
                dp = pd[pl.ds(PAD + 1 + r0, tr), :]
                out[pl.ds(r0, tr), :] = (cw[0:1, :] * dp + cw[1:2, :] * dc + cw[2:3, :] * dm).astype(BF16)

    col = lambda off: (lambda j: (0, j + off))
    blk = pl.BlockSpec((S, tn), col(0))
    padded = pl.BlockSpec((S + 2 * PAD, tn), col(0))
    g8 = pl.BlockSpec((8, tn), col(0))
    return pl.pallas_call(
        body, name=name, grid=(nb,),
        in_specs=[pl.BlockSpec((S, D), lambda j: (0, 0)), pl.BlockSpec((tn, D), lambda j: (j, 0)), padded, padded,
                  pl.BlockSpec((3, tn), col(0)), pl.BlockSpec((3, tn), col(nb)),
                  pl.BlockSpec((1, tn), col(0)), pl.BlockSpec((1, tn), col(nb))],
        out_specs=[blk, blk, blk, g8, g8],
        out_shape=[jax.ShapeDtypeStruct((S, F), BF16), jax.ShapeDtypeStruct((S, F), BF16),
                   jax.ShapeDtypeStruct((S, F), BF16), jax.ShapeDtypeStruct((8, F), F32),
                   jax.ShapeDtypeStruct((8, F), F32)],
        scratch_shapes=[pltpu.VMEM((S + 2 * PAD, tn), F32)] * 2,
        compiler_params=_cp(("parallel",)),
    )(dyb, wdown, uv, ug, wconv, wconv, bconv, bconv)


def _log_sigmoid(x):
    return jnp.minimum(x, 0.0) - jnp.log(1.0 + jnp.exp(-jnp.abs(x)))


def _gla_gate_fwd(proj, wgf, bgf, wgb, bgb, name):
    S = proj.shape[0]
    tm = _pick(S, (512, 256))

    def body(r_ref, wf_ref, bf_ref, wb_ref, bb_ref, laf_ref, lab_ref):
        r = r_ref[...].astype(BF16)
        lf = jnp.dot(r, wf_ref[...].astype(BF16), preferred_element_type=F32) + bf_ref[...]
        lb = jnp.dot(r, wb_ref[...].astype(BF16), preferred_element_type=F32) + bb_ref[...]
        laf_ref[...] = _log_sigmoid(lf) * (1.0 / GLA_GATE_NORMALIZER)
        lab_ref[...] = _log_sigmoid(lb) * (1.0 / GLA_GATE_NORMALIZER)

    full = lambda shp: pl.BlockSpec(shp, lambda i: (0, 0))
    row = pl.BlockSpec((tm, GLA_KEY), lambda i: (i, 0))
    return pl.pallas_call(
        body, name=name, grid=(S // tm,),
        in_specs=[pl.BlockSpec((tm, LANES), lambda i: (i, GLA_R_BLOCK)),
                  full((LANES, GLA_KEY)), full((1, GLA_KEY)), full((LANES, GLA_KEY)), full((1, GLA_KEY))],
        out_specs=[row, row],
        out_shape=[jax.ShapeDtypeStruct((S, GLA_KEY), F32)] * 2,
        compiler_params=_cp(("parallel",)),
    )(proj, wgf, bgf, wgb, bgb)


def _gla_gate_bwd(dlaf, dlab, proj, wgf, bgf, wgb, bgb, name):
    S = proj.shape[0]
    tm = _pick(S, (512, 256))

    def body(dlf_ref, dlb_ref, r_ref, wf_ref, bf_ref, wb_ref, bb_ref, dr_ref, dwf_ref, dbf_ref, dwb_ref, dbb_ref):
        i = pl.program_id(0)
        r = r_ref[...].astype(BF16)
        wf = wf_ref[...].astype(BF16)
        wb = wb_ref[...].astype(BF16)
        lf = jnp.dot(r, wf, preferred_element_type=F32) + bf_ref[...]
        lb = jnp.dot(r, wb, preferred_element_type=F32) + bb_ref[...]
        glf = dlf_ref[...] * (1.0 / GLA_GATE_NORMALIZER) * (1.0 / (1.0 + jnp.exp(lf)))
        glb = dlb_ref[...] * (1.0 / GLA_GATE_NORMALIZER) * (1.0 / (1.0 + jnp.exp(lb)))
        gfb = glf.astype(BF16)
        gbb = glb.astype(BF16)
        dr_ref[...] = _dg(gfb, wf, 1, 1) + _dg(gbb, wb, 1, 1)
        parts = (_dg(r, gfb, 0, 0), jnp.sum(glf, axis=0, keepdims=True),
                 _dg(r, gbb, 0, 0), jnp.sum(glb, axis=0, keepdims=True))
        outs = (dwf_ref, dbf_ref, dwb_ref, dbb_ref)

        @pl.when(i == 0)
        def _():
            for o, p in zip(outs, parts):
                o[...] = p

        @pl.when(i > 0)
        def _():
            for o, p in zip(outs, parts):
                o[...] += p

    full = lambda shp: pl.BlockSpec(shp, lambda i: (0, 0))
    row = pl.BlockSpec((tm, GLA_KEY), lambda i: (i, 0))
    return pl.pallas_call(
        body, name=name, grid=(S // tm,),
        in_specs=[row, row, pl.BlockSpec((tm, LANES), lambda i: (i, GLA_R_BLOCK)),
                  full((LANES, GLA_KEY)), full((1, GLA_KEY)), full((LANES, GLA_KEY)), full((1, GLA_KEY))],
        out_specs=[pl.BlockSpec((tm, LANES), lambda i: (i, 0)),
                   full((LANES, GLA_KEY)), full((1, GLA_KEY)), full((LANES, GLA_KEY)), full((1, GLA_KEY))],
        out_shape=[jax.ShapeDtypeStruct((S, LANES), F32),
                   jax.ShapeDtypeStruct((LANES, GLA_KEY), F32), jax.ShapeDtypeStruct((1, GLA_KEY), F32),
                   jax.ShapeDtypeStruct((LANES, GLA_KEY), F32), jax.ShapeDtypeStruct((1, GLA_KEY), F32)],
        compiler_params=_cp(("arbitrary",)),
    )(dlaf, dlab, proj, wgf, bgf, wgb, bgb)


def _gla_masks(rev):
    C = GLA_CHUNK
    t = lax.broadcasted_iota(jnp.int32, (C, C), 0)
    s = lax.broadcasted_iota(jnp.int32, (C, C), 1)
    if rev:
        return (s >= t), (s > t), (t >= s), (t > s)
    return (s <= t), (s <= t), (t <= s), (t <= s)


def _cum_dot(cum, x):
    return jnp.dot(cum.astype(F32), x, precision=HIGHEST, preferred_element_type=F32)


def _gla_chunk_common(q, k, la, cum, end_row):
    b = _cum_dot(cum, la)
    bend = b[end_row:end_row + 1, :]
    e = jnp.exp(b)
    qd = q * (GLA_DK ** -0.5) * e
    ei = jnp.exp(-b)
    ee = jnp.exp(bend - b)
    d = jnp.exp(bend)
    return e, ei, ee, d, qd, k * ei, k * ee


GLA_CB = 16


def _gla_specs(S, rev_order):
    n = S // GLA_CHUNK
    cb = min(GLA_CB, n)
    nblk = n // cb
    rows = cb * GLA_CHUNK
    ci = (lambda i: nblk - 1 - i) if rev_order else (lambda i: i)
    q_spec = pl.BlockSpec((rows, GLA_DK), lambda h, i: (ci(i), h))
    k_spec = pl.BlockSpec((rows, GLA_DK), lambda h, i: (ci(i), GLA_HEADS + h))
    v_spec = pl.BlockSpec((rows, GLA_DV), lambda h, i: (ci(i), GLA_KEY * 2 // GLA_DV + h))
    la_spec = pl.BlockSpec((rows, GLA_DK), lambda h, i: (ci(i), h))
    o_spec = pl.BlockSpec((rows, GLA_DV), lambda h, i: (ci(i), h))
    st_spec = pl.BlockSpec((1, cb, GLA_DV, GLA_DK), lambda h, i: (h, ci(i), 0, 0))
    return n, cb, nblk, q_spec, k_spec, v_spec, la_spec, o_spec, st_spec


def _gla_scan_fwd(proj, la, rev, name):
    S = proj.shape[0]
    C = GLA_CHUNK
    n, cb, nblk, q_spec, k_spec, v_spec, la_spec, o_spec, st_spec = _gla_specs(S, rev)
    end_row = 0 if rev else C - 1
    order = list(range(cb))[::-1] if rev else list(range(cb))

    def body(q_ref, k_ref, v_ref, la_ref, o_ref, st_ref, state):
        i = pl.program_id(1)

        @pl.when(i == 0)
        def _():
            state[...] = jnp.zeros_like(state)

        cum, mask, _, _ = _gla_masks(rev)
        pre, intra, kv = {}, {}, {}
        for cc in order:
            rows = pl.ds(cc * C, C)
            q, k, v, lav = q_ref[rows, :], k_ref[rows, :], v_ref[rows, :], la_ref[rows, :]
            _, _, _, d, qd, ki, ke = _gla_chunk_common(q, k, lav, cum, end_row)
            qdb, kib, keb, vb = qd.astype(BF16), ki.astype(BF16), ke.astype(BF16), v.astype(BF16)
            pre[cc] = (d, qdb)
            att = jnp.where(mask, _dg(qdb, kib, 1, 1), 0.0)
            intra[cc] = jnp.dot(att.astype(BF16), vb, preferred_element_type=F32)
            kv[cc] = _dg(vb, keb, 0, 0)
        st = state[...]
        for cc in order:
            d, qdb = pre[cc]
            o_ref[pl.ds(cc * C, C), :] = intra[cc] + _dg(qdb, st.astype(BF16), 1, 1)
            st_ref[0, cc] = st
            st = st * d + kv[cc]
        state[...] = st

    return pl.pallas_call(
        body, name=name, grid=(GLA_HEADS, nblk),
        in_specs=[q_spec, k_spec, v_spec, la_spec],
        out_specs=[o_spec, st_spec],
        out_shape=[jax.ShapeDtypeStruct((S, GLA_VAL), F32),
                   jax.ShapeDtypeStruct((GLA_HEADS, n, GLA_DV, GLA_DK), F32)],
        scratch_shapes=[pltpu.VMEM((GLA_DV, GLA_DK), F32)],
        compiler_params=_cp(("parallel", "arbitrary")),
    )(proj, proj, proj, la)


def _gla_scan_bwd(do, proj, la, states, rev, name):
    S = proj.shape[0]
    C = GLA_CHUNK
    n, cb, nblk, q_spec, k_spec, v_spec, la_spec, o_spec, st_spec = _gla_specs(S, not rev)
    end_row = 0 if rev else C - 1
    order = list(range(cb)) if rev else list(range(cb))[::-1]

    def body(do_ref, q_ref, k_ref, v_ref, la_ref, st_ref, dq_ref, dk_ref, dv_ref, dla_ref, gstate):
        i = pl.program_id(1)

        @pl.when(i == 0)
        def _():
            gstate[...] = jnp.zeros_like(gstate)

        cum, mask, cum_t, mask_t = _gla_masks(rev)
        g = gstate[...]
        for cc in order:
            rows = pl.ds(cc * C, C)
            q, k, v, lav = q_ref[rows, :], k_ref[rows, :], v_ref[rows, :], la_ref[rows, :]
            dov = do_ref[rows, :]
            st = st_ref[0, cc]
            e, ei, ee, d, qd, ki, ke = _gla_chunk_common(q, k, lav, cum, end_row)
            qdb, kib, keb, vb = qd.astype(BF16), ki.astype(BF16), ke.astype(BF16), v.astype(BF16)
            dob, gb, stb = dov.astype(BF16), g.astype(BF16), st.astype(BF16)
            att_t = jnp.where(mask_t, _dg(kib, qdb, 1, 1), 0.0)
            da = jnp.where(mask, _dg(dob, vb, 1, 1), 0.0)
            da_t = jnp.where(mask_t, _dg(vb, dob, 1, 1), 0.0)
            dv_ref[rows, :] = jnp.dot(att_t.astype(BF16), dob, preferred_element_type=F32) + _dg(keb, gb, 1, 1)
            dqd = (jnp.dot(da.astype(BF16), kib, preferred_element_type=F32)
                   + jnp.dot(dob, stb, preferred_element_type=F32))
            dki = jnp.dot(da_t.astype(BF16), qdb, preferred_element_type=F32)
            dke = jnp.dot(vb, gb, preferred_element_type=F32)
            dd = jnp.sum(st * g, axis=0, keepdims=True)
            g = g * d + _dg(dob, qdb, 0, 0)
            dq_ref[rows, :] = dqd * e * (GLA_DK ** -0.5)
            dk_ref[rows, :] = dki * ei + dke * ee
            dkeke = dke * ke
            db = dqd * qd - dki * ki - dkeke
            dbend = jnp.sum(dkeke, axis=0, keepdims=True) + dd * d
            dla_ref[rows, :] = _cum_dot(cum_t, db) + dbend
        gstate[...] = g

    key_out = la_spec
    return pl.pallas_call(
        body, name=name, grid=(GLA_HEADS, nblk),
        in_specs=[o_spec, q_spec, k_spec, v_spec, la_spec, st_spec],
        out_specs=[key_out, key_out, o_spec, key_out],
        out_shape=[jax.ShapeDtypeStruct((S, GLA_KEY), F32), jax.ShapeDtypeStruct((S, GLA_KEY), F32),
                   jax.ShapeDtypeStruct((S, GLA_VAL), F32), jax.ShapeDtypeStruct((S, GLA_KEY), F32)],
        scratch_shapes=[pltpu.VMEM((GLA_DV, GLA_DK), F32)],
        compiler_params=_cp(("parallel", "arbitrary")),
    )(do, proj, proj, proj, la, states)


def _gla_out_fwd(of, ob, proj, gn, name):
    S = of.shape[0]
    tm = _pick(S, (256, 128))
    gblk = (2 * GLA_KEY + GLA_VAL) // GLA_VAL

    def body(of_ref, ob_ref, g_ref, gn_ref, z_ref):
        gnv = gn_ref[...]
        for h in range(GLA_HEADS):
            cols = pl.ds(h * GLA_DV, GLA_DV)
            o = of_ref[:, cols] + ob_ref[:, cols]
            r = lax.rsqrt(jnp.mean(o * o, axis=-1, keepdims=True) + NORM_EPS)
            gv = g_ref[:, cols]
            z_ref[:, cols] = (o * r * gnv * (gv * _sigmoid(gv))).astype(BF16)

    row = pl.BlockSpec((tm, GLA_VAL), lambda i: (i, 0))
    return pl.pallas_call(
        body, name=name, grid=(S // tm,),
        in_specs=[row, row, pl.BlockSpec((tm, GLA_VAL), lambda i: (i, gblk)),
                  pl.BlockSpec((1, GLA_DV), lambda i: (0, 0))],
        out_specs=row,
        out_shape=jax.ShapeDtypeStruct((S, GLA_VAL), BF16),
        compiler_params=_cp(("parallel",)),
    )(of, ob, proj, gn)


def _gla_out_bwd(dz, of, ob, proj, gn, name):
    S = of.shape[0]
    tm = _pick(S, (256, 128))
    gblk = (2 * GLA_KEY + GLA_VAL) // GLA_VAL

    def body(dz_ref, of_ref, ob_ref, g_ref, gn_ref, do_ref, dg_ref, dgn_ref):
        i = pl.program_id(0)
        gnv = gn_ref[...]
        part = jnp.zeros((1, GLA_DV), F32)
        for h in range(GLA_HEADS):
            cols = pl.ds(h * GLA_DV, GLA_DV)
            o = of_ref[:, cols] + ob_ref[:, cols]
            r = lax.rsqrt(jnp.mean(o * o, axis=-1, keepdims=True) + NORM_EPS)
            y = o * r
            gv = g_ref[:, cols]
            sg = _sigmoid(gv)
            dzv = dz_ref[:, cols]
            dg_ref[:, cols] = dzv * (y * gnv) * (sg * (1.0 + gv * (1.0 - sg)))
            don = dzv * (gv * sg)
            part = part + jnp.sum(don * y, axis=0, keepdims=True)
            dy = don * gnv
            do_ref[:, cols] = r * (dy - y * jnp.mean(dy * y, axis=-1, keepdims=True))

        @pl.when(i == 0)
        def _():
            dgn_ref[...] = part

        @pl.when(i > 0)
        def _():
            dgn_ref[...] += part

    row = pl.BlockSpec((tm, GLA_VAL), lambda i: (i, 0))
    one = pl.BlockSpec((1, GLA_DV), lambda i: (0, 0))
    return pl.pallas_call(
        body, name=name, grid=(S // tm,),
        in_specs=[row, row, row, pl.BlockSpec((tm, GLA_VAL), lambda i: (i, gblk)), one],
        out_specs=[row, row, one],
        out_shape=[jax.ShapeDtypeStruct((S, GLA_VAL), F32), jax.ShapeDtypeStruct((S, GLA_VAL), F32),
                   jax.ShapeDtypeStruct((1, GLA_DV), F32)],
        compiler_params=_cp(("arbitrary",)),
    )(dz, of, ob, proj, gn)


N_QK_HEADS = ATT_QH + ATT_KVH


def _qk_prep_fwd(proj, qn, kn, rc, rs, name):
    S = proj.shape[0]
    tm = _pick(S, (256, 128))
    W = N_QK_HEADS * ATT_HD
    scale = ATT_HD ** -0.5

    def body(p_ref, qn_ref, kn_ref, rc_ref, rs_ref, v_in_ref, qk_ref, v_ref, kt_ref, vt_ref):
        c, s = rc_ref[...], rs_ref[...]
        for h in range(N_QK_HEADS):
            cols = pl.ds(h * ATT_HD, ATT_HD)
            w = qn_ref[...] if h < ATT_QH else kn_ref[...]
            xv = p_ref[:, cols]
            r = lax.rsqrt(jnp.mean(xv * xv, axis=-1, keepdims=True) + NORM_EPS)
            y = xv * r * w
            out = y * c + pltpu.roll(y, ATT_HD // 2, 1) * s
            if h < ATT_QH:
                qk_ref[:, cols] = (out * scale).astype(BF16)
            else:
                qk_ref[:, cols] = out.astype(BF16)
                kt_ref[pl.ds((h - ATT_QH) * ATT_HD, ATT_HD), :] = out.T.astype(BF16)
        v_ref[...] = v_in_ref[...].astype(BF16)
        for h in range(ATT_KVH):
            vt_ref[pl.ds(h * ATT_HD, ATT_HD), :] = v_in_ref[:, pl.ds(h * ATT_HD, ATT_HD)].T.astype(BF16)

    one = pl.BlockSpec((1, ATT_HD), lambda i: (0, 0))
    tab = pl.BlockSpec((tm, ATT_HD), lambda i: (i, 0))
    vw = ATT_KVH * ATT_HD
    tr = pl.BlockSpec((vw, tm), lambda i: (0, i))
    return pl.pallas_call(
        body, name=name, grid=(S // tm,),
        in_specs=[pl.BlockSpec((tm, W), lambda i: (i, 0)), one, one, tab, tab,
                  pl.BlockSpec((tm, vw), lambda i: (i, W // vw))],
        out_specs=[pl.BlockSpec((tm, W), lambda i: (i, 0)), pl.BlockSpec((tm, vw), lambda i: (i, 0)), tr, tr],
        out_shape=[jax.ShapeDtypeStruct((S, W), BF16), jax.ShapeDtypeStruct((S, vw), BF16),
                   jax.ShapeDtypeStruct((vw, S), BF16), jax.ShapeDtypeStruct((vw, S), BF16)],
        compiler_params=_cp(("parallel",)),
    )(proj, qn, kn, rc, rs, proj)


def _qk_prep_bwd(dqk, proj, qn, kn, rc, rs, name):
    S = proj.shape[0]
    tm = _pick(S, (256, 128))
    W = N_QK_HEADS * ATT_HD

    def body(d_ref, p_ref, qn_ref, kn_ref, rc_ref, rs_ref, dp_ref, dqn_ref, dkn_ref):
        i = pl.program_id(0)
        c, s = rc_ref[...], rs_ref[...]
        parts = [jnp.zeros((1, ATT_HD), F32), jnp.zeros((1, ATT_HD), F32)]
        for h in range(N_QK_HEADS):
            cols = pl.ds(h * ATT_HD, ATT_HD)
            w = qn_ref[...] if h < ATT_QH else kn_ref[...]
            dout = d_ref[:, cols]
            dy = dout * c + pltpu.roll(dout * s, ATT_HD // 2, 1)
            xv = p_ref[:, cols]
            r = lax.rsqrt(jnp.mean(xv * xv, axis=-1, keepdims=True) + NORM_EPS)
            xr = xv * r
            which = 0 if h < ATT_QH else 1
            parts[which] = parts[which] + jnp.sum(dy * xr, axis=0, keepdims=True)
            dxr = dy * w
            dp_ref[:, cols] = r * (dxr - xr * jnp.mean(dxr * xr, axis=-1, keepdims=True))

        @pl.when(i == 0)
        def _():
            dqn_ref[...] = parts[0]
            dkn_ref[...] = parts[1]

        @pl.when(i > 0)
        def _():
            dqn_ref[...] += parts[0]
            dkn_ref[...] += parts[1]

    one = pl.BlockSpec((1, ATT_HD), lambda i: (0, 0))
    tab = pl.BlockSpec((tm, ATT_HD), lambda i: (i, 0))
    row = pl.BlockSpec((tm, W), lambda i: (i, 0))
    return pl.pallas_call(
        body, name=name, grid=(S // tm,),
        in_specs=[row, row, one, one, tab, tab],
        out_specs=[row, one, one],
        out_shape=[jax.ShapeDtypeStruct((S, W), F32), jax.ShapeDtypeStruct((1, ATT_HD), F32),
                   jax.ShapeDtypeStruct((1, ATT_HD), F32)],
        compiler_params=_cp(("arbitrary",)),
    )(dqk, proj, qn, kn, rc, rs)


ATT_TQ = 1024
LSE_ROWS = 8


def _attn_fwd(qk, vt, name):
    S = qk.shape[0]
    tq = min(ATT_TQ, S)

    def body(q_ref, k_ref, vt_ref, o_ref, lse_ref):
        st = _dg(k_ref[...], q_ref[...], 1, 1)
        m = jnp.max(st, axis=0, keepdims=True)
        pt = jnp.exp(st - m)
        l = jnp.sum(pt, axis=0, keepdims=True)
        ot = jnp.dot(vt_ref[...], pt.astype(BF16), preferred_element_type=F32)
        o_ref[...] = (ot * (1.0 / l)).T
        lse_ref[...] = jnp.broadcast_to(m + jnp.log(l), (LSE_ROWS, tq))

    qo = pl.BlockSpec((tq, ATT_HD), lambda h, i: (i, h))
    return pl.pallas_call(
        body, name=name, grid=(ATT_QH, S // tq),
        in_specs=[qo, pl.BlockSpec((S, ATT_HD), lambda h, i: (0, ATT_QH + h // ATT_GROUP)),
                  pl.BlockSpec((ATT_HD, S), lambda h, i: (h // ATT_GROUP, 0))],
        out_specs=[qo, pl.BlockSpec((LSE_ROWS, tq), lambda h, i: (h, i))],
        out_shape=[jax.ShapeDtypeStruct((S, ATT_QH * ATT_HD), F32),
                   jax.ShapeDtypeStruct((ATT_QH * LSE_ROWS, S), F32)],
        compiler_params=_cp(("parallel", "parallel")),
    )(qk, qk, vt)


def _attn_bwd(do, o, lse, qk, v, kt, name):
    S = qk.shape[0]
    tq = min(ATT_TQ, S)
    scale = ATT_HD ** -0.5

    def body(do_ref, o_ref, lse_ref, q_ref, k_ref, v_ref, kt_ref, dq_ref, dk_ref, dv_ref):
        g = pl.program_id(1)
        i = pl.program_id(2)

        @pl.when((g == 0) & (i == 0))
        def _():
            dk_ref[...] = jnp.zeros_like(dk_ref)
            dv_ref[...] = jnp.zeros_like(dv_ref)

        q = q_ref[...]
        dov = do_ref[...]
        dob = dov.astype(BF16)
        delta = jnp.sum((dov * o_ref[...]).T, axis=0, keepdims=True)
        st = _dg(k_ref[...], q, 1, 1)
        pt = jnp.exp(st - lse_ref[0:1, :])
        dpt = _dg(v_ref[...], dob, 1, 1)
        dst = (pt * (dpt - delta)).astype(BF16)
        dv_ref[...] += jnp.dot(pt.astype(BF16), dob, preferred_element_type=F32)
        dk_ref[...] += jnp.dot(dst, q, preferred_element_type=F32)
        dq_ref[...] = jnp.dot(kt_ref[...], dst, preferred_element_type=F32).T * scale

    qo = pl.BlockSpec((tq, ATT_HD), lambda kv, g, i: (i, kv * ATT_GROUP + g))
    kvo = pl.BlockSpec((S, ATT_HD), lambda kv, g, i: (0, kv))
    return pl.pallas_call(
        body, name=name, grid=(ATT_KVH, ATT_GROUP, S // tq),
        in_specs=[qo, qo, pl.BlockSpec((LSE_ROWS, tq), lambda kv, g, i: (kv * ATT_GROUP + g, i)), qo,
                  pl.BlockSpec((S, ATT_HD), lambda kv, g, i: (0, ATT_QH + kv)), kvo,
                  pl.BlockSpec((ATT_HD, S), lambda kv, g, i: (kv, 0))],
        out_specs=[qo, kvo, kvo],
        out_shape=[jax.ShapeDtypeStruct((S, ATT_QH * ATT_HD), F32),
                   jax.ShapeDtypeStruct((S, ATT_KVH * ATT_HD), F32),
                   jax.ShapeDtypeStruct((S, ATT_KVH * ATT_HD), F32)],
        compiler_params=_cp(("parallel", "arbitrary", "arbitrary")),
    )(do, o, lse, qk, qk, v, kt)


def _adamw(w, g, m, v, name):
    rows, cols = w.shape
    tr = rows
    for cand in (512, 256, 128, 64, 32, 16, 8):
        if rows % cand == 0 and cand * cols * 4 <= 2 * 1024 * 1024:
            tr = cand
            break

    def body(w_ref, g_ref, m_ref, v_ref, d_ref, nm_ref, nv_ref):
        gv = g_ref[...]
        nm = ADAM_B1 * m_ref[...] + (1.0 - ADAM_B1) * gv
        nv = ADAM_B2 * v_ref[...] + (1.0 - ADAM_B2) * (gv * gv)
        m_hat = nm / (1.0 - ADAM_B1 ** ADAM_STEP)
        v_hat = nv / (1.0 - ADAM_B2 ** ADAM_STEP)
        d_ref[...] = -ADAM_LR * (m_hat / (jnp.sqrt(v_hat) + ADAM_EPS) + ADAM_WD * w_ref[...])
        nm_ref[...] = nm
        nv_ref[...] = nv

    blk = pl.BlockSpec((tr, cols), lambda i: (i, 0))
    return pl.pallas_call(
        body, name=name, grid=(rows // tr,),
        in_specs=[blk] * 4, out_specs=[blk] * 3,
        out_shape=[jax.ShapeDtypeStruct((rows, cols), F32)] * 3,
        compiler_params=_cp(("parallel",)),
    )(w, g, m, v)


ANY = pl.BlockSpec(memory_space=pl.ANY)


def _place():
    return lax.axis_index("x"), lax.axis_index("y"), lax.axis_index("c")


def _other_chips(x, y):
    return [(1 - x, y), (x, 1 - y), (1 - x, 1 - y)]


def _half_rows(c, H):
    return pl.ds(pl.multiple_of(c * H, 8), H)


def _allreduce_small(v, name):
    R = v.shape[0]
    n_dev = 8

    def body(v_ref, sum_ref, all_ref, token_ref, send_sems, recv_sems, local_sem):
        token_ref[...] = jnp.zeros_like(token_ref)
        x, y, c = _place()
        me, sibling = (x, y, c), (x, y, 1 - c)
        chips = _other_chips(x, y)

        def rows(px, py, pc):
            return all_ref.at[pl.ds(pl.multiple_of((4 * px + 2 * py + pc) * R, 8), R), :]

        def copy(k, block, to, src=None):
            return pltpu.make_async_remote_copy(
                src_ref=rows(*block) if src is None else src, dst_ref=rows(*block),
                send_sem=send_sems.at[k], recv_sem=recv_sems.at[k], device_id=to, device_id_type=MESH)

        own = pltpu.make_async_copy(v_ref, rows(*me), local_sem)
        own.start()
        first = [copy(0, me, sibling, src=v_ref)]
        first += [copy(1 + j, me, (*chip, c), src=v_ref) for j, chip in enumerate(chips)]
        for cp in first:
            cp.start()
        passed = [copy(4 + j, (*chip, c), sibling) for j, chip in enumerate(chips)]
        for j, chip in enumerate(chips):
            copy(1 + j, (*chip, c), me).wait_recv()
            passed[j].start()
        copy(0, sibling, me).wait_recv()
        for j, chip in enumerate(chips):
            copy(4 + j, (*chip, 1 - c), me).wait_recv()
        for cp in first + passed:
            cp.wait_send()
        own.wait()
        acc = all_ref[pl.ds(0, R), :]
        for d in range(1, n_dev):
            acc = acc + all_ref[pl.ds(d * R, R), :]
        sum_ref[...] = acc

    vm = pl.BlockSpec(memory_space=pltpu.VMEM)
    total, _, token = pl.pallas_call(
        body, name=name,
        in_specs=[vm], out_specs=[vm, vm, vm],
        out_shape=[jax.ShapeDtypeStruct((R, LANES), F32), jax.ShapeDtypeStruct((n_dev * R, LANES), F32),
                   jax.ShapeDtypeStruct((8, LANES), F32)],
        scratch_shapes=[pltpu.SemaphoreType.DMA((7,)), pltpu.SemaphoreType.DMA((7,)), pltpu.SemaphoreType.DMA],
    )(v)
    return total, token


def _join_halves(bufs, name):
    n = len(bufs)
    halves = [b.shape[0] // 2 for b in bufs]

    def body(*refs):
        outs = refs[n:2 * n]
        send_sems, recv_sems = refs[2 * n:]
        x, y, c = _place()

        def copy(k, core):
            blk = outs[k].at[_half_rows(core, halves[k])]
            return pltpu.make_async_remote_copy(src_ref=blk, dst_ref=blk, send_sem=send_sems.at[k],
                                                recv_sem=recv_sems.at[k], device_id=(x, y, 1 - c),
                                                device_id_type=MESH)

        sends = [copy(k, c) for k in range(n)]
        for cp in sends:
            cp.start()
        for k in range(n):
            copy(k, 1 - c).wait_recv()
        for cp in sends:
            cp.wait_send()

    return pl.pallas_call(
        body, name=name, in_specs=[ANY] * n, out_specs=[ANY] * n,
        out_shape=[jax.ShapeDtypeStruct(b.shape, b.dtype) for b in bufs],
        input_output_aliases={k: k for k in range(n)},
        scratch_shapes=[pltpu.SemaphoreType.DMA((n,)), pltpu.SemaphoreType.DMA((n,))],
    )(*bufs)


def _rs_rows(H, width):
    for cand in range(H, 0, -16):
        if H % cand == 0 and cand % 16 == 0 and cand * width * 4 <= 1536 * 1024:
            return cand
    return H


def _add_sibling(g, got, c, me, name):
    _, H, width = got.shape
    tb = _rs_rows(H, width)
    nb = H // tb

    def body(sp_ref, g_ref, got_ref, sb_ref, sf_ref):
        p = pl.program_id(1)
        s = g_ref[0] + got_ref[0]
        sb_ref[0] = s.astype(BF16)

        @pl.when(p == sp_ref[1])
        def _():
            sf_ref[...] = s

    grid_spec = pltpu.PrefetchScalarGridSpec(
        num_scalar_prefetch=1, grid=(nb, N_CHIPS),
        in_specs=[pl.BlockSpec((1, tb, width), lambda i, p, sp: (p, sp[0] * nb + i, 0)),
                  pl.BlockSpec((1, tb, width), lambda i, p, sp: (p, i, 0))],
        out_specs=[pl.BlockSpec((1, tb, width), lambda i, p, sp: (p, i, 0)),
                   pl.BlockSpec((tb, width), lambda i, p, sp: (i, 0))])
    return pl.pallas_call(
        body, name=name, grid_spec=grid_spec,
        out_shape=[jax.ShapeDtypeStruct((N_CHIPS, H, width), BF16), jax.ShapeDtypeStruct((H, width), F32)],
        compiler_params=_cp(("arbitrary", "arbitrary")),
    )(jnp.stack([c, me]).astype(jnp.int32), g, got)


def _add_chips(sf, got, others_and_c, name):
    H, width = sf.shape
    tb = _rs_rows(H, width)
    nb = H // tb

    def body(sp_ref, sf_ref, r1_ref, r2_ref, r3_ref, out_ref):
        out_ref[...] = ((sf_ref[...] + r1_ref[0].astype(F32)) + r2_ref[0].astype(F32)) + r3_ref[0].astype(F32)

    def slot(k):
        return pl.BlockSpec((1, tb, width), lambda i, sp: (sp[k], i, 0))

    blk = pl.BlockSpec((tb, width), lambda i, sp: (i, 0))
    grid_spec = pltpu.PrefetchScalarGridSpec(
        num_scalar_prefetch=1, grid=(nb,), in_specs=[blk, slot(0), slot(1), slot(2)],
        out_specs=pl.BlockSpec((tb, width), lambda i, sp: (sp[3] * nb + i, 0)))
    return pl.pallas_call(
        body, name=name, grid_spec=grid_spec,
        out_shape=jax.ShapeDtypeStruct((2 * H, width), F32),
        compiler_params=_cp(("arbitrary",)),
    )(others_and_c.astype(jnp.int32), sf, got, got, got)


REPLICATED = ("norm_mix", "norm_ffn", "gla_b_gate_f", "gla_b_gate_b", "gla_norm", "attn_q_norm", "attn_k_norm",
              "ffn_b_conv")


PIECE_ROWS = 16


def _piece_rows(shape):
    n = 1
    for s in shape:
        n *= s
    rows = n // LANES
    return rows, -(-rows // PIECE_ROWS) * PIECE_ROWS


def _pack(pieces, dtype, row_multiple):
    flat = []
    for p in pieces:
        rows, padded = _piece_rows(p.shape)
        flat.append(jnp.pad(p.astype(dtype).reshape(rows, LANES), ((0, padded - rows), (0, 0))))
    rows = sum(f.shape[0] for f in flat)
    padded = -(-rows // row_multiple) * row_multiple
    if padded > rows:
        flat.append(jnp.zeros((padded - rows, LANES), dtype))
    return jnp.concatenate(flat, axis=0)


def _unpack(buf, shapes):
    out, r = [], 0
    for shp in shapes:
        rows, padded = _piece_rows(shp)
        out.append(buf[r:r + rows].reshape(shp))
        r += padded
    return out


def _own_slot(shard2d, me):
    return lax.dynamic_update_index_in_dim(lax.empty((N_CHIPS,) + shard2d.shape, shard2d.dtype), shard2d, me, 0)


def _layer_small(w, l):
    j = l // 2
    if l % 2 == 0:
        return [w["gla_w_gate_up_f"][j], w["gla_w_gate_up_b"][j], w["ffn_w_conv"][l]]
    return [w["ffn_w_conv"][l]]


def _layer_weight_bufs(w, l, me):
    j = l // 2
    mixer = ("gla_w_in", "gla_w_out") if l % 2 == 0 else ("attn_w_qkv", "attn_w_out")
    bufs = [_own_slot(w[n][j].astype(BF16), me) for n in mixer]
    bufs.append(_own_slot(_pack(_layer_small(w, l), F32, 32), me))
    bufs += [_own_slot(w["ffn_w_up"][l].astype(BF16), me), _own_slot(w["ffn_w_down"][l].astype(BF16), me)]
    return bufs


N_MIXER_BUFS = 3


def _layer_weights(w, l, got):
    rows = lambda t: t.reshape(-1, t.shape[2])
    cols = lambda t: jnp.concatenate([t[p] for p in range(N_CHIPS)], axis=1)
    out = {}
    if len(got) != N_MIXER_BUFS:
        up, down = got[-2:]
        out.update(up=up, up_full=cols(up), down=rows(down))
    if len(got) != 2:
        mix_in, mix_out, small = got[:N_MIXER_BUFS]
        shapes = [t.shape for t in _layer_small(w, l)]
        parts = [_unpack(small[p], shapes) for p in range(N_CHIPS)]
        full_small = [jnp.concatenate([parts[p][k] for p in range(N_CHIPS)], axis=-1) for k in range(len(shapes))]
        out.update(conv=full_small[-1])
        if l % 2 == 0:
            out.update(gla_in=jnp.pad(cols(mix_in), ((0, 0), (0, GLA_IN_PAD - GLA_IN))), gla_out=rows(mix_out),
                       gate_f=full_small[0], gate_b=full_small[1])
        else:
            out.update(qkv=mix_in, attn_out=rows(mix_out))
    return out


HBM = pl.BlockSpec(memory_space=pltpu.HBM)
SEM = pl.BlockSpec(memory_space=pltpu.SEMAPHORE)
SIDE_EFFECT = pltpu.SideEffectType.DATAFLOW_SIDE_EFFECTING


def _gather_start(bufs, after, name):
    n = len(bufs)
    halves = [b.shape[1] // 2 for b in bufs]

    def body(*refs):
        refs = refs[:n] + refs[n + 1:]
        send_sems, recv_sems = refs[n:2 * n], refs[2 * n:3 * n]
        outs, token = refs[3 * n:4 * n], refs[4 * n]
        x, y, c = _place()
        me = 2 * x + y
        for k in range(n):
            blk = outs[k].at[me, _half_rows(c, halves[k])]
            for px, py in _other_chips(x, y):
                pltpu.make_async_remote_copy(src_ref=blk, dst_ref=blk, send_sem=send_sems[k], recv_sem=recv_sems[k],
                                             device_id=(px, py, c), device_id_type=MESH).start()
        token[...] = jnp.zeros_like(token)

    res = pl.pallas_call(
        body, name=name,
        in_specs=[HBM] * n + [ANY],
        out_specs=[SEM] * (2 * n) + [HBM] * n + [pl.BlockSpec(memory_space=pltpu.VMEM)],
        out_shape=[pltpu.SemaphoreType.DMA(())] * (2 * n) + [pltpu.HBM(b.shape, b.dtype) for b in bufs]
        + [jax.ShapeDtypeStruct((8, LANES), F32)],
        input_output_aliases={k: 2 * n + k for k in range(n)},
        compiler_params=pltpu.CompilerParams(has_side_effects=SIDE_EFFECT),
    )(*[pltpu.with_memory_space_constraint(b, pltpu.HBM) for b in bufs], after)
    return res[:n], res[n:2 * n], res[2 * n:3 * n], res[3 * n]


def _gather_wait(send_sems, recv_sems, thru, after, name):
    n = len(thru)
    halves = [b.shape[1] // 2 for b in thru]

    def body(*refs):
        ss, rs = refs[n:2 * n], refs[2 * n:3 * n]
        outs = refs[3 * n + 1:]
        x, y, c = _place()
        for k in range(n):
            three = outs[k].at[pl.ds(0, N_CHIPS - 1), _half_rows(c, halves[k])]
            cp = pltpu.make_async_remote_copy(src_ref=three, dst_ref=three, send_sem=ss[k], recv_sem=rs[k],
                                              device_id=(x, y, c), device_id_type=MESH)
            cp.wait_send()
            cp.wait_recv()

    return pl.pallas_call(
        body, name=name,
        in_specs=[HBM] * n + [SEM] * (2 * n) + [ANY],
        out_specs=[HBM] * n,
        out_shape=[pltpu.HBM(b.shape, b.dtype) for b in thru],
        input_output_aliases={k: k for k in range(n)},
        compiler_params=pltpu.CompilerParams(has_side_effects=SIDE_EFFECT),
    )(*thru, *send_sems, *recv_sems, after)


def _swap_start(bufs, name):
    n = len(bufs)
    halves = [b.shape[1] // 2 for b in bufs]
    land_shapes = [(N_CHIPS, h, b.shape[2]) for b, h in zip(bufs, halves)]

    def body(*refs):
        send_sems, recv_sems = refs[2 * n:3 * n], refs[3 * n:4 * n]
        srcs, lands, token = refs[4 * n:5 * n], refs[5 * n:6 * n], refs[6 * n]
        x, y, c = _place()
        for k in range(n):
            for p in range(N_CHIPS):
                pltpu.make_async_remote_copy(src_ref=srcs[k].at[p, _half_rows(1 - c, halves[k])],
                                             dst_ref=lands[k].at[p], send_sem=send_sems[k], recv_sem=recv_sems[k],
                                             device_id=(x, y, 1 - c), device_id_type=MESH).start()
        token[...] = jnp.zeros_like(token)

    hbm = lambda a: pltpu.with_memory_space_constraint(a, pltpu.HBM)
    res = pl.pallas_call(
        body, name=name,
        in_specs=[HBM] * (2 * n),
        out_specs=[SEM] * (2 * n) + [HBM] * (2 * n) + [pl.BlockSpec(memory_space=pltpu.VMEM)],
        out_shape=[pltpu.SemaphoreType.DMA(())] * (2 * n) + [pltpu.HBM(b.shape, b.dtype) for b in bufs]
        + [pltpu.HBM(s, b.dtype) for s, b in zip(land_shapes, bufs)] + [jax.ShapeDtypeStruct((8, LANES), F32)],
        input_output_aliases={k: 2 * n + k for k in range(2 * n)},
        compiler_params=pltpu.CompilerParams(has_side_effects=SIDE_EFFECT),
    )(*[hbm(b) for b in bufs], *[hbm(lax.empty(s, b.dtype)) for s, b in zip(land_shapes, bufs)])
    return res[:n], res[n:2 * n], res[2 * n:3 * n], res[3 * n:4 * n], res[4 * n]


def _swap_wait(send_sems, recv_sems, bufs, lands, after, name):
    n = len(bufs)

    def body(*refs):
        ss, rs = refs[2 * n:3 * n], refs[3 * n:4 * n]
        l_out = refs[5 * n + 1:]
        x, y, c = _place()
        for k in range(n):
            cp = pltpu.make_async_remote_copy(src_ref=l_out[k], dst_ref=l_out[k], send_sem=ss[k], recv_sem=rs[k],
                                              device_id=(x, y, 1 - c), device_id_type=MESH)
            cp.wait_send()
            cp.wait_recv()

    res = pl.pallas_call(
        body, name=name,
        in_specs=[HBM] * (2 * n) + [SEM] * (2 * n) + [ANY],
        out_specs=[HBM] * (2 * n),
        out_shape=[pltpu.HBM(b.shape, b.dtype) for b in bufs] + [pltpu.HBM(l.shape, l.dtype) for l in lands],
        input_output_aliases={k: k for k in range(2 * n)},
        compiler_params=pltpu.CompilerParams(has_side_effects=SIDE_EFFECT),
    )(*bufs, *lands, *send_sems, *recv_sems, after)
    return res[:n], res[n:]


def _send_start(sbs, name):
    n = len(sbs)

    def body(*refs):
        send_sems, recv_sems = refs[2 * n:3 * n], refs[3 * n:4 * n]
        srcs, lands, token = refs[4 * n:5 * n], refs[5 * n:6 * n], refs[6 * n]
        x, y, c = _place()
        me = 2 * x + y
        for k in range(n):
            for px, py in _other_chips(x, y):
                pltpu.make_async_remote_copy(src_ref=srcs[k].at[2 * px + py], dst_ref=lands[k].at[me],
                                             send_sem=send_sems[k], recv_sem=recv_sems[k],
                                             device_id=(px, py, c), device_id_type=MESH).start()
        token[...] = jnp.zeros_like(token)

    hbm = lambda a: pltpu.with_memory_space_constraint(a, pltpu.HBM)
    res = pl.pallas_call(
        body, name=name,
        in_specs=[HBM] * (2 * n),
        out_specs=[SEM] * (2 * n) + [HBM] * (2 * n) + [pl.BlockSpec(memory_space=pltpu.VMEM)],
        out_shape=[pltpu.SemaphoreType.DMA(())] * (2 * n) + [pltpu.HBM(s.shape, s.dtype) for s in sbs] * 2
        + [jax.ShapeDtypeStruct((8, LANES), F32)],
        input_output_aliases={k: 2 * n + k for k in range(2 * n)},
        compiler_params=pltpu.CompilerParams(has_side_effects=SIDE_EFFECT),
    )(*[hbm(s) for s in sbs], *[hbm(lax.empty(s.shape, s.dtype)) for s in sbs])
    return res[:n], res[n:2 * n], res[2 * n:3 * n], res[3 * n:4 * n], res[4 * n]


def _send_wait(send_sems, recv_sems, srcs, lands, after, name):
    n = len(srcs)

    def body(*refs):
        ss, rs = refs[2 * n:3 * n], refs[3 * n:4 * n]
        s_out, l_out = refs[4 * n + 1:5 * n + 1], refs[5 * n + 1:]
        x, y, c = _place()
        for k in range(n):
            cp = pltpu.make_async_remote_copy(src_ref=s_out[k].at[pl.ds(0, N_CHIPS - 1)],
                                              dst_ref=l_out[k].at[pl.ds(0, N_CHIPS - 1)], send_sem=ss[k],
                                              recv_sem=rs[k], device_id=(x, y, c), device_id_type=MESH)
            cp.wait_send()
            cp.wait_recv()

    res = pl.pallas_call(
        body, name=name,
        in_specs=[HBM] * (2 * n) + [SEM] * (2 * n) + [ANY],
        out_specs=[HBM] * (2 * n),
        out_shape=[pltpu.HBM(s.shape, s.dtype) for s in srcs] * 2,
        input_output_aliases={k: k for k in range(2 * n)},
        compiler_params=pltpu.CompilerParams(has_side_effects=SIDE_EFFECT),
    )(*srcs, *lands, *send_sems, *recv_sems, after)
    return res[n:]


def _pass_to_sibling(bufs, name):
    n = len(bufs)
    halves = [b.shape[1] // 2 for b in bufs]

    def body(*refs):
        outs = refs[n:2 * n]
        send_sems, recv_sems = refs[2 * n:]
        x, y, c = _place()
        chips = _other_chips(x, y)

        def copy(k, j, core):
            px, py = chips[j]
            blk = outs[k].at[2 * px + py, _half_rows(core, halves[k])]
            return pltpu.make_async_remote_copy(src_ref=blk, dst_ref=blk, send_sem=send_sems.at[3 * k + j],
                                                recv_sem=recv_sems.at[3 * k + j], device_id=(x, y, 1 - c),
                                                device_id_type=MESH)

        sends = [copy(k, j, c) for k in range(n) for j in range(3)]
        for cp in sends:
            cp.start()
        for k in range(n):
            for j in range(3):
                copy(k, j, 1 - c).wait_recv()
        for cp in sends:
            cp.wait_send()

    return pl.pallas_call(
        body, name=name,
        in_specs=[ANY] * n, out_specs=[ANY] * n,
        out_shape=[jax.ShapeDtypeStruct(b.shape, b.dtype) for b in bufs],
        input_output_aliases={k: k for k in range(n)},
        scratch_shapes=[pltpu.SemaphoreType.DMA((3 * n,)), pltpu.SemaphoreType.DMA((3 * n,))],
    )(*bufs)


def _rope_tables(S):
    rows = S // GRID_W
    row_idx = jnp.repeat(jnp.arange(rows, dtype=F32), GRID_W)
    col_idx = jnp.tile(jnp.arange(GRID_W, dtype=F32), rows)
    pairs = ATT_HD // 4
    inv_freq = ROPE_THETA ** (-jnp.arange(pairs, dtype=F32) / pairs)
    ang = jnp.concatenate([row_idx[:, None] * inv_freq, col_idx[:, None] * inv_freq], axis=-1)
    cos, sin = jnp.cos(ang), jnp.sin(ang)
    return jnp.concatenate([cos, cos], axis=-1), jnp.concatenate([-sin, sin], axis=-1)


def _gate_rows(w, first_row):
    return jnp.zeros((LANES, GLA_KEY), F32).at[first_row:first_row + GLA_RANK].set(w.astype(F32))


def _local_step(x, target, weights_of, grads_out, grads_mid, P):
    S = x.shape[0]
    rc, rs = _rope_tables(S)
    row = lambda a: a.reshape(1, -1)
    saved = []
    for i in range(DEPTH):
        j = i // 2
        W = dict(weights_of(i, "mix", x))
        nm = row(P["norm_mix"][i])
        h1 = _rmsnorm_fwd(x, nm, f"norm_mix_fwd{i}")
        if i % 2 == 0:
            wgf = _gate_rows(W["gate_f"], 0)
            wgb = _gate_rows(W["gate_b"], GLA_RANK)
            bgf, bgb = row(P["gla_b_gate_f"][j]), row(P["gla_b_gate_b"][j])
            gn = row(P["gla_norm"][j])
            proj = _matmul_rows(h1, W["gla_in"], f"gla_in{i}")
            laf, lab = _gla_gate_fwd(proj, wgf, bgf, wgb, bgb, f"gla_gate_fwd{i}")
            of, stf = _gla_scan_fwd(proj, laf, False, f"gla_scan_f_fwd{i}")
            ob, stb = _gla_scan_fwd(proj, lab, True, f"gla_scan_b_fwd{i}")
            z = _gla_out_fwd(of, ob, proj, gn, f"gla_out_fwd{i}")
            xm = _matmul_rows(z, W["gla_out"], f"gla_outproj{i}", res=x)
            mix = dict(proj=proj, laf=laf, lab=lab, of=of, ob=ob, stf=stf, stb=stb, z=z, wgf=wgf, wgb=wgb)
        else:
            proj = _matmul_rows(h1, W["qkv"], f"attn_qkv{i}", w_layer=0)
            qn, kn = row(P["attn_q_norm"][j]), row(P["attn_k_norm"][j])
            qk, vb, kt, vt = _qk_prep_fwd(proj, qn, kn, rc, rs, f"qk_prep_fwd{i}")
            o, lse = _attn_fwd(qk, vt, f"attn_fwd{i}")
            xm = _matmul_rows(o, W["attn_out"], f"attn_outproj{i}", res=x)
            mix = dict(proj=proj, qk=qk, vb=vb, kt=kt, o=o, lse=lse)
        W.update(weights_of(i, "ffn", xm))
        h2 = _rmsnorm_fwd(xm, row(P["norm_ffn"][i]), f"norm_ffn_fwd{i}")
        a, uv, ug = _ffn_mid_fwd(h2, W["up_full"], W["conv"], row(P["ffn_b_conv"][i]), f"ffn_mid_fwd{i}")
        xo = _matmul_rows(a, W["down"], f"ffn_down{i}", res=xm)
        saved.append(dict(x=x, h1=h1, xm=xm, h2=h2, uv=uv, ug=ug, mix=mix, W=W))
        x = xo

    dx, dxb, loss = _loss_grad(x, target, "loss")

    G = {n: [None] * (DEPTH if n.startswith(("norm", "ffn")) else DEPTH // 2) for n in REPLICATED}
    token = None
    for i in reversed(range(DEPTH)):
        j = i // 2
        sv = saved[i]
        mix = sv["mix"]
        W = sv["W"]
        bconv = row(P["ffn_b_conv"][i])
        if token is not None:
            t = token[0:1, 0:1]
            bconv = jnp.where(t == 0.0, bconv, t)
        duv, dug, a, gwv, gwg = _ffn_mid_bwd(dxb, W["down"], sv["uv"], sv["ug"], W["conv"], bconv, f"ffn_mid_bwd{i}")
        L = dict(down=_wgrad(a, dxb, f"ffn_down_wgrad{i}", chips="rows"),
                 up=_wgrad(sv["h2"], (duv, dug), f"ffn_up_wgrad{i}", chips="cols"),
                 small=[jnp.concatenate([gwv[:3], gwg[:3]], axis=1)])
        G["ffn_b_conv"][i] = jnp.concatenate([gwv[3], gwg[3]], axis=0)
        nffn = row(P["norm_ffn"][i])
        if token is not None:
            t = grads_mid(i + 1, duv)[0:1, 0:1]
            nffn = jnp.where(t == 0.0, nffn, t)
        dxm, dxmb, dn = _dgrad_norm((duv, dug), W["up"], sv["xm"], nffn, dx, f"ffn_up_dgrad{i}", w_layer=0)
        G["norm_ffn"][i] = dn[0]
        if i % 2 == 0:
            proj = mix["proj"]
            bgf, bgb = row(P["gla_b_gate_f"][j]), row(P["gla_b_gate_b"][j])
            gn = row(P["gla_norm"][j])
            dz = _matmul_rows(dxmb, W["gla_out"], f"gla_outproj_dgrad{i}", transposed=True)
            L["out"] = _wgrad(mix["z"], dxmb, f"gla_outproj_wgrad{i}", chips="rows")
            do, dg, dgn = _gla_out_bwd(dz, mix["of"], mix["ob"], proj, gn, f"gla_out_bwd{i}")
            G["gla_norm"][j] = dgn[0]
            dqf, dkf, dvf, dlaf = _gla_scan_bwd(do, proj, mix["laf"], mix["stf"], False, f"gla_scan_f_bwd{i}")
            dqb, dkb, dvb, dlab = _gla_scan_bwd(do, proj, mix["lab"], mix["stb"], True, f"gla_scan_b_bwd{i}")
            dr, dwf, dbf, dwb, dbb = _gla_gate_bwd(dlaf, dlab, proj, mix["wgf"], bgf, mix["wgb"], bgb,
                                                   f"gla_gate_bwd{i}")
            L["small"] = [dwf[:GLA_RANK], dwb[GLA_RANK:2 * GLA_RANK]] + L["small"]
            G["gla_b_gate_f"][j] = dbf[0]
            G["gla_b_gate_b"][j] = dbb[0]
            dproj = jnp.concatenate([dqf + dqb, dkf + dkb, dvf + dvb, dg, dr], axis=1).astype(BF16)
            L["mix_in"] = _wgrad(sv["h1"], dproj, f"gla_in_wgrad{i}")
            dx, dxb, dn = _dgrad_norm(dproj, W["gla_in"], sv["x"], row(P["norm_mix"][i]), dxm, f"mix_in_dgrad{i}")
        else:
            proj = mix["proj"]
            qn, kn = row(P["attn_q_norm"][j]), row(P["attn_k_norm"][j])
            do = _matmul_rows(dxmb, W["attn_out"], f"attn_outproj_dgrad{i}", transposed=True)
            L["out"] = _wgrad(mix["o"], dxmb, f"attn_outproj_wgrad{i}", chips="rows")
            dq, dk, dv = _attn_bwd(do, mix["o"], mix["lse"], mix["qk"], mix["vb"], mix["kt"], f"attn_bwd{i}")
            dqk = jnp.concatenate([dq, dk], axis=1)
            dpqk, dqn, dkn = _qk_prep_bwd(dqk, proj, qn, kn, rc, rs, f"qk_prep_bwd{i}")
            G["attn_q_norm"][j] = dqn[0]
            G["attn_k_norm"][j] = dkn[0]
            dproj = jnp.concatenate([dpqk, dv], axis=1).astype(BF16)
            L["mix_in"] = _wgrad(sv["h1"], dproj, f"attn_qkv_wgrad{i}", chips="cols")
            dx, dxb, dn = _dgrad_norm(dproj, W["qkv"], sv["x"], row(P["norm_mix"][i]), dxm, f"mix_in_dgrad{i}",
                                      w_layer=0)
        G["norm_mix"][i] = dn[0]
        token = grads_out(i, L, G, loss)
    grads_mid(0, dx)
    return loss, dx, G


def kernel(x, norm_mix, norm_ffn, gla_w_in, gla_w_gate_up_f, gla_b_gate_f, gla_w_gate_up_b, gla_b_gate_b, gla_norm, gla_w_out, attn_w_qkv, attn_q_norm, attn_k_norm, attn_w_out, ffn_w_up, ffn_w_conv, ffn_b_conv, ffn_w_down, loss_target, m_norm_mix, m_norm_ffn, m_gla_w_in, m_gla_w_gate_up_f, m_gla_b_gate_f, m_gla_w_gate_up_b, m_gla_b_gate_b, m_gla_norm, m_gla_w_out, m_attn_w_qkv, m_attn_q_norm, m_attn_k_norm, m_attn_w_out, m_ffn_w_up, m_ffn_w_conv, m_ffn_b_conv, m_ffn_w_down, v_norm_mix, v_norm_ffn, v_gla_w_in, v_gla_w_gate_up_f, v_gla_b_gate_f, v_gla_w_gate_up_b, v_gla_b_gate_b, v_gla_norm, v_gla_w_out, v_attn_w_qkv, v_attn_q_norm, v_attn_k_norm, v_attn_w_out, v_ffn_w_up, v_ffn_w_conv, v_ffn_b_conv, v_ffn_w_down):
    names = ("norm_mix", "norm_ffn", "gla_w_in", "gla_w_gate_up_f", "gla_b_gate_f", "gla_w_gate_up_b",
             "gla_b_gate_b", "gla_norm", "gla_w_out", "attn_w_qkv", "attn_q_norm", "attn_k_norm", "attn_w_out",
             "ffn_w_up", "ffn_w_conv", "ffn_b_conv", "ffn_w_down")
    w = dict(zip(names, (norm_mix, norm_ffn, gla_w_in, gla_w_gate_up_f, gla_b_gate_f, gla_w_gate_up_b,
                         gla_b_gate_b, gla_norm, gla_w_out, attn_w_qkv, attn_q_norm, attn_k_norm, attn_w_out,
                         ffn_w_up, ffn_w_conv, ffn_b_conv, ffn_w_down)))
    m = dict(zip(names, (m_norm_mix, m_norm_ffn, m_gla_w_in, m_gla_w_gate_up_f, m_gla_b_gate_f,
                         m_gla_w_gate_up_b, m_gla_b_gate_b, m_gla_norm, m_gla_w_out, m_attn_w_qkv, m_attn_q_norm,
                         m_attn_k_norm, m_attn_w_out, m_ffn_w_up, m_ffn_w_conv, m_ffn_b_conv, m_ffn_w_down)))
    v = dict(zip(names, (v_norm_mix, v_norm_ffn, v_gla_w_in, v_gla_w_gate_up_f, v_gla_b_gate_f,
                         v_gla_w_gate_up_b, v_gla_b_gate_b, v_gla_norm, v_gla_w_out, v_attn_w_qkv, v_attn_q_norm,
                         v_attn_k_norm, v_attn_w_out, v_ffn_w_up, v_ffn_w_conv, v_ffn_b_conv, v_ffn_w_down)))
    px, py, pc = _place()
    me = 2 * px + py

    started, token = [], w["norm_mix"]
    for l in range(DEPTH):
        started.append(_gather_start(_layer_weight_bufs(w, l, me), token, f"gather_start{l}"))
        token = started[-1][3]
    fetched = {}

    def weights_of(l, part, after):
        send_sems, recv_sems, thru, _ = started[l]
        if l == 0:
            pick = slice(0, N_MIXER_BUFS) if part == "mix" else slice(N_MIXER_BUFS, None)
            landed = _gather_wait(send_sems[pick], recv_sems[pick], thru[pick], token if part == "mix" else after,
                                  f"gather_wait{l}_{part}")
            return _layer_weights(w, l, _pass_to_sibling(landed, f"gather_pass{l}_{part}"))
        if part == "mix":
            landed = _gather_wait(send_sems, recv_sems, thru, after, f"gather_wait{l}")
            fetched[l] = _layer_weights(w, l, _pass_to_sibling(landed, f"gather_pass{l}"))
        return fetched[l]

    sent = {}

    def grads_out(l, L, G, loss_part):
        mix_in = L["mix_in"]
        if l % 2 == 0:
            width = w["gla_w_in"].shape[2]
            mix_in = jnp.stack([mix_in[:, p * width:(p + 1) * width] for p in range(N_CHIPS)])
        cut = lambda t, p: lax.slice_in_dim(t, p * (t.shape[-1] // N_CHIPS), (p + 1) * (t.shape[-1] // N_CHIPS),
                                            axis=t.ndim - 1)
        small = jnp.stack([_pack([cut(t, p) for t in L["small"]], F32, 32) for p in range(N_CHIPS)])
        if l == 0:
            packed = _pack([jnp.stack(G[n]) for n in REPLICATED] + [loss_part], F32, 16)
            sent["small_sum"], tok = _allreduce_small(packed, "small_allreduce")
            small = jnp.where(tok[0:1, 0:1] == 0.0, small, tok[0:1, 0:1])
        bufs = [mix_in, L["out"], small, L["up"], L["down"]]
        *swap, tok = _swap_start(bufs, f"grads{l}_to_sibling")
        sent[l] = swap
        return tok

    def grads_mid(l, after):
        bufs, gots = _swap_wait(*sent[l], after, f"grads{l}_from_sibling")
        sums = [_add_sibling(b, g, pc, me, f"grads{l}_add_sibling{k}") for k, (b, g) in enumerate(zip(bufs, gots))]
        send_sems, recv_sems, srcs, lands, tok = _send_start([s[0] for s in sums], f"grads{l}_start")
        sent[l] = (send_sems, recv_sems, srcs, lands, [s[1] for s in sums])
        sent["last_token"] = tok
        return tok

    P = {n: w[n] for n in REPLICATED}

    loss_part, dx, grads = _local_step(x[0], loss_target[0], weights_of, grads_out, grads_mid, P)

    small_sum = sent["small_sum"]
    others_and_c = jnp.stack([jnp.where(me <= k, k + 1, k) for k in range(N_CHIPS - 1)] + [pc])
    mine, after = {}, sent["last_token"]
    for l in reversed(range(DEPTH)):
        send_sems, recv_sems, srcs, lands, own = sent[l]
        landed = _send_wait(send_sems, recv_sems, srcs, lands, after, f"grads{l}_wait")
        halves = [_add_chips(own[k], landed[k], others_and_c, f"grads{l}_add_chips{k}") for k in range(len(own))]
        mine[l] = _join_halves(halves, f"grads{l}_join_halves")
        after = mine[l][0]
    gsh = {}
    for n, k, layers in (("ffn_w_up", 3, range(DEPTH)), ("ffn_w_down", 4, range(DEPTH)),
                         ("gla_w_in", 0, range(0, DEPTH, 2)), ("gla_w_out", 1, range(0, DEPTH, 2)),
                         ("attn_w_qkv", 0, range(1, DEPTH, 2)), ("attn_w_out", 1, range(1, DEPTH, 2))):
        gsh[n] = jnp.stack([mine[l][k] for l in layers])
    small_mine = [_unpack(mine[l][2], [t.shape for t in _layer_small(w, l)]) for l in range(DEPTH)]
    gsh["ffn_w_conv"] = jnp.stack([small_mine[l][-1] for l in range(DEPTH)])
    gsh["gla_w_gate_up_f"] = jnp.stack([small_mine[l][0] for l in range(0, DEPTH, 2)])
    gsh["gla_w_gate_up_b"] = jnp.stack([small_mine[l][1] for l in range(0, DEPTH, 2)])

    parts = _unpack(small_sum, [w[n].shape for n in REPLICATED] + [(1, LANES)])
    gsh.update(dict(zip(REPLICATED, parts[:-1])))
    loss = parts[-1][0, 0]

    delta, new_m, new_v = {}, {}, {}
    for n in names:
        shp = w[n].shape
        two_d = (-1, shp[-1])
        d, nm, nv = _adamw(w[n].reshape(two_d), gsh[n].reshape(two_d), m[n].reshape(two_d), v[n].reshape(two_d),
                           f"adamw_{n}")
        delta[n], new_m[n], new_v[n] = d.reshape(shp), nm.reshape(shp), nv.reshape(shp)

    return (loss, dx[None], *[gsh[n] for n in names], *[delta[n] for n in names],
            *[new_m[n] for n in names], *[new_v[n] for n in names])
```

```python
import jax
import jax.numpy as jnp
from jax import lax
from jax.experimental import pallas as pl
from jax.experimental.pallas import tpu as pltpu

F32 = jnp.float32
BF16 = jnp.bfloat16
MESH = pl.DeviceIdType.MESH
HIGHEST = lax.Precision.HIGHEST

D_MODEL = 1024
DEPTH = 4
GRID_W = 64
NORM_EPS = 1e-6
GLA_HEADS = 4
GLA_DK = 128
GLA_DV = 256
GLA_KEY = GLA_HEADS * GLA_DK
GLA_VAL = GLA_HEADS * GLA_DV
GLA_RANK = 16
GLA_CHUNK = 64
GLA_GATE_NORMALIZER = 16.0
GLA_IN = 2 * GLA_KEY + 2 * GLA_VAL + 2 * GLA_RANK
GLA_IN_PAD = 3200
GLA_R_BLOCK = (2 * GLA_KEY + 2 * GLA_VAL) // 128
ATT_HD = 128
ATT_QH = 8
ATT_KVH = 2
ATT_GROUP = ATT_QH // ATT_KVH
ATT_QKV = (ATT_QH + 2 * ATT_KVH) * ATT_HD
ROPE_THETA = 10000.0
D_FF = 2816
ADAM_LR = 0.001
ADAM_B1 = 0.9
ADAM_B2 = 0.999
ADAM_EPS = 1e-08
ADAM_WD = 0.01
ADAM_STEP = 10

N_CHIPS = 4
LANES = 128
VMEM_LIMIT = 56 * 1024 * 1024


def _cp(sem):
    return pltpu.CompilerParams(dimension_semantics=sem, vmem_limit_bytes=VMEM_LIMIT)


def _pick(n, cands):
    for c in cands:
        if n % c == 0:
            return c
    return n


def _dg(a, b, ca, cb):
    return lax.dot_general(a, b, (((ca,), (cb,)), ((), ())), preferred_element_type=F32)


def _sigmoid(x):
    return 0.5 * jnp.tanh(0.5 * x) + 0.5


def _rmsnorm_fwd(x, w, name):
    S, D = x.shape
    tm = _pick(S, (512, 256))

    def body(x_ref, w_ref, h_ref):
        xv = x_ref[...]
        r = lax.rsqrt(jnp.mean(xv * xv, axis=-1, keepdims=True) + NORM_EPS)
        h_ref[...] = (xv * r * w_ref[...]).astype(BF16)

    return pl.pallas_call(
        body, name=name, grid=(S // tm,),
        in_specs=[pl.BlockSpec((tm, D), lambda i: (i, 0)), pl.BlockSpec((1, D), lambda i: (0, 0))],
        out_specs=pl.BlockSpec((tm, D), lambda i: (i, 0)),
        out_shape=jax.ShapeDtypeStruct((S, D), BF16),
        compiler_params=_cp(("parallel",)),
    )(x, w)


def _loss_grad(y, t, name):
    S, D = y.shape
    tm = _pick(S, (512, 256))

    def body(y_ref, t_ref, dy_ref, dyb_ref, loss_ref):
        i = pl.program_id(0)
        d = y_ref[...] - t_ref[...]
        dy = d * (1.0 / D)
        dy_ref[...] = dy
        dyb_ref[...] = dy.astype(BF16)
        sq = jnp.sum(jnp.sum(d * d, axis=1, keepdims=True), axis=0, keepdims=True)
        part = jnp.broadcast_to(sq * (0.5 / D), (1, LANES))

        @pl.when(i == 0)
        def _():
            loss_ref[...] = part

        @pl.when(i > 0)
        def _():
            loss_ref[...] += part

    return pl.pallas_call(
        body, name=name, grid=(S // tm,),
        in_specs=[pl.BlockSpec((tm, D), lambda i: (i, 0)), pl.BlockSpec((tm, D), lambda i: (i, 0))],
        out_specs=[pl.BlockSpec((tm, D), lambda i: (i, 0)), pl.BlockSpec((tm, D), lambda i: (i, 0)),
                   pl.BlockSpec((1, LANES), lambda i: (0, 0))],
        out_shape=[jax.ShapeDtypeStruct((S, D), F32), jax.ShapeDtypeStruct((S, D), BF16),
                   jax.ShapeDtypeStruct((1, LANES), F32)],
        compiler_params=_cp(("arbitrary",)),
    )(y, t)


WGRAD_TK = 512


def _wgrad(a, b, name, chips=None):
    S, Kw = a.shape
    pair = isinstance(b, (tuple, list))
    tn = b[0].shape[1] if pair else b.shape[1]
    N = 2 * tn if pair else tn
    tk = _pick(S, (WGRAD_TK, 256, 128))
    nk = S // tk
    if pair:
        b_specs = [pl.BlockSpec((tk, tn), lambda j, k: (jnp.where(j == 0, k, nk - 1), 0)),
                   pl.BlockSpec((tk, tn), lambda j, k: (jnp.where(j == 1, k, 0), 0))]
    else:
        b_specs = [pl.BlockSpec((tk, tn), lambda j, k: (k, 0))]
    if chips == "cols":
        cw = N // N_CHIPS
        span = tn // cw
        o_spec = pl.BlockSpec((span, Kw, cw), lambda j, k: (j, 0, 0))
        out_shape = jax.ShapeDtypeStruct((N_CHIPS, Kw, cw), F32)
    elif chips == "rows":
        assert not pair
        o_spec = pl.BlockSpec((N_CHIPS, Kw // N_CHIPS, N), lambda j, k: (0, 0, 0))
        out_shape = jax.ShapeDtypeStruct((N_CHIPS, Kw // N_CHIPS, N), F32)
    else:
        assert not pair
        o_spec = pl.BlockSpec((Kw, N), lambda j, k: (0, 0))
        out_shape = jax.ShapeDtypeStruct((Kw, N), F32)
    nb = len(b_specs)

    def body(*refs):
        a_ref, b_refs, o_ref, acc = refs[0], refs[1:1 + nb], refs[-2], refs[-1]
        j = pl.program_id(0)
        k = pl.program_id(1)

        @pl.when(k == 0)
        def _():
            acc[...] = jnp.zeros_like(acc)

        av = a_ref[...].astype(BF16)
        for h in range(nb):
            @pl.when(j == h)
            def _():
                acc[...] += _dg(av, b_refs[h][...].astype(BF16), 0, 0)

        @pl.when(k == nk - 1)
        def _():
            v = acc[...]
            if chips == "cols":
                for s in range(span):
                    o_ref[s] = v[:, s * cw:(s + 1) * cw]
            elif chips == "rows":
                rows = Kw // N_CHIPS
                for p in range(N_CHIPS):
                    o_ref[p] = v[p * rows:(p + 1) * rows, :]
            else:
                o_ref[...] = v

    return pl.pallas_call(
        body, name=name, grid=(nb, nk),
        in_specs=[pl.BlockSpec((tk, Kw), lambda j, k: (k, 0))] + b_specs,
        out_specs=o_spec, out_shape=out_shape,
        scratch_shapes=[pltpu.VMEM((Kw, tn), F32)],
        compiler_params=_cp(("parallel", "arbitrary")),
    )(a, *(tuple(b) if pair else (b,)))


def _matmul_rows(a, w, name, res=None, w_layer=None, transposed=False):
    M, K = a.shape
    if w_layer is not None:
        cw = w.shape[2]
        N = N_CHIPS * cw
        w_spec = pl.BlockSpec((N_CHIPS, K, cw), lambda i: (0, w_layer, 0))
    else:
        N = w.shape[0] if transposed else w.shape[1]
        assert w.shape[1 if transposed else 0] == K
        w_spec = pl.BlockSpec(w.shape, lambda i: (0, 0))
    tm = _pick(M, (512, 256, 128))
    has_res = res is not None

    def body(*refs):
        a_ref, w_ref = refs[0], refs[1]
        r_ref = refs[2] if has_res else None
        o_ref = refs[-1]
        av = a_ref[...].astype(BF16)
        if w_layer is not None:
            for p in range(N_CHIPS):
                o_ref[:, pl.ds(p * cw, cw)] = jnp.dot(av, w_ref[p], preferred_element_type=F32)
        else:
            v = _dg(av, w_ref[...], 1, 1 if transposed else 0)
            o_ref[...] = v + r_ref[...] if has_res else v

    row = pl.BlockSpec((tm, N), lambda i: (i, 0))
    return pl.pallas_call(
        body, name=name, grid=(M // tm,),
        in_specs=[pl.BlockSpec((tm, K), lambda i: (i, 0)), w_spec] + ([row] if has_res else []),
        out_specs=row, out_shape=jax.ShapeDtypeStruct((M, N), F32),
        compiler_params=_cp(("parallel",)),
    )(*((a, w) + ((res,) if has_res else ())))


def _dgrad_norm(dy, w, x, wn, dres, name, w_layer=None):
    pair = isinstance(dy, (tuple, list))
    M = dy[0].shape[0] if pair else dy.shape[0]
    Kp = 2 * dy[0].shape[1] if pair else dy.shape[1]
    D = x.shape[1]
    if w_layer is not None:
        cw = w.shape[2]
        assert N_CHIPS * cw == Kp and w.shape[1] % D == 0
        w_spec = pl.BlockSpec((N_CHIPS, D, cw), lambda i: (0, w_layer, 0))
    else:
        assert w.shape == (D, Kp)
        w_spec = pl.BlockSpec((D, Kp), lambda i: (0, 0))
    tm = _pick(M, (256, 128))
    width = Kp // 2 if pair else Kp
    dy_specs = [pl.BlockSpec((tm, width), lambda i: (i, 0))] * (2 if pair else 1)
    nd = len(dy_specs)

    def body(*refs):
        dy_refs = refs[:nd]
        w_ref, x_ref, wn_ref, dres_ref, dx_ref, dxb_ref, dwn_ref = refs[nd:]
        i = pl.program_id(0)
        if w_layer is not None:
            dh = None
            for p in range(N_CHIPS):
                src, off = divmod(p * cw, width)
                part = _dg(dy_refs[src][:, pl.ds(off, cw)], w_ref[p], 1, 1)
                dh = part if dh is None else dh + part
        else:
            dh = _dg(dy_refs[0][...], w_ref[...], 1, 1)
        xv = x_ref[...]
        r = lax.rsqrt(jnp.mean(xv * xv, axis=-1, keepdims=True) + NORM_EPS)
        yv = xv * r
        dyv = dh * wn_ref[...]
        dxv = r * (dyv - yv * jnp.mean(dyv * yv, axis=-1, keepdims=True)) + dres_ref[...]
        dx_ref[...] = dxv
        dxb_ref[...] = dxv.astype(BF16)
        part = jnp.sum(dh * yv, axis=0, keepdims=True)

        @pl.when(i == 0)
        def _():
            dwn_ref[...] = part

        @pl.when(i > 0)
        def _():
            dwn_ref[...] += part

    row = pl.BlockSpec((tm, D), lambda i: (i, 0))
    one = pl.BlockSpec((1, D), lambda i: (0, 0))
    return pl.pallas_call(
        body, name=name, grid=(M // tm,),
        in_specs=dy_specs + [w_spec, row, one, row],
        out_specs=[row, row, one],
        out_shape=[jax.ShapeDtypeStruct((M, D), F32), jax.ShapeDtypeStruct((M, D), BF16),
                   jax.ShapeDtypeStruct((1, D), F32)],
        compiler_params=_cp(("arbitrary",)),
    )(*(tuple(dy) if pair else (dy,)), w, x, wn, dres)


FFN_TN_FWD = 256
FFN_TN_BWD = 128
FFN_ROWS = 256
PAD = 8


def _conv3(pad_ref, w, r0, tr):
    um = pad_ref[pl.ds(PAD - 1 + r0, tr), :]
    uc = pad_ref[pl.ds(PAD + r0, tr), :]
    up = pad_ref[pl.ds(PAD + 1 + r0, tr), :]
    return w[0:1, :] * um + w[1:2, :] * uc + w[2:3, :] * up, (um, uc, up)


def _zero_pads(pad_ref, S, tn):
    pad_ref[pl.ds(0, PAD), :] = jnp.zeros((PAD, tn), F32)
    pad_ref[pl.ds(PAD + S, PAD), :] = jnp.zeros((PAD, tn), F32)


def _ffn_mid_fwd(h, wup, wconv, bconv, name):
    S, D = h.shape
    F = wup.shape[1] // 2
    tn = FFN_TN_FWD
    nb = F // tn
    tr = min(FFN_ROWS, S)

    def body(h_ref, wv_ref, wg_ref, cv_ref, cg_ref, bv_ref, bg_ref, a_ref, uv_ref, ug_ref):
        _zero_pads(uv_ref, S, tn)
        _zero_pads(ug_ref, S, tn)
        hv = h_ref[...]
        uv_ref[pl.ds(PAD, S), :] = jnp.dot(hv, wv_ref[...], preferred_element_type=F32)
        ug_ref[pl.ds(PAD, S), :] = jnp.dot(hv, wg_ref[...], preferred_element_type=F32)
        cwv, cwg, bv, bg = cv_ref[...], cg_ref[...], bv_ref[...], bg_ref[...]
        for r0 in range(0, S, tr):
            cv = _conv3(uv_ref, cwv, r0, tr)[0] + bv
            cg = _conv3(ug_ref, cwg, r0, tr)[0] + bg
            a_ref[pl.ds(r0, tr), :] = (cg * _sigmoid(cg) * cv).astype(BF16)

    col = lambda off: (lambda j: (0, j + off))
    padded = pl.BlockSpec((S + 2 * PAD, tn), col(0))
    return pl.pallas_call(
        body, name=name, grid=(nb,),
        in_specs=[pl.BlockSpec((S, D), lambda j: (0, 0)),
                  pl.BlockSpec((D, tn), col(0)), pl.BlockSpec((D, tn), col(nb)),
                  pl.BlockSpec((3, tn), col(0)), pl.BlockSpec((3, tn), col(nb)),
                  pl.BlockSpec((1, tn), col(0)), pl.BlockSpec((1, tn), col(nb))],
        out_specs=[pl.BlockSpec((S, tn), col(0)), padded, padded],
        out_shape=[jax.ShapeDtypeStruct((S, F), BF16), jax.ShapeDtypeStruct((S + 2 * PAD, F), F32),
                   jax.ShapeDtypeStruct((S + 2 * PAD, F), F32)],
        compiler_params=_cp(("parallel",)),
    )(h, wup, wup, wconv, wconv, bconv, bconv)


def _rows8(rows):
    n = rows[0].shape[1]
    idx = lax.broadcasted_iota(jnp.int32, (8, n), 0)
    out = jnp.zeros((8, n), F32)
    for k, r in enumerate(rows):
        out = jnp.where(idx == k, r, out)
    return out


def _ffn_mid_bwd(dyb, wdown, uv, ug, wconv, bconv, name):
    S, D = dyb.shape
    F = wdown.shape[0]
    tn = FFN_TN_BWD
    nb = F // tn
    tr = min(FFN_ROWS, S)

    def body(dy_ref, wd_ref, uv_ref, ug_ref, cv_ref, cg_ref, bv_ref, bg_ref,
             duv_ref, dug_ref, a_ref, gwv_ref, gwg_ref, pdv, pdg):
        for p in (pdv, pdg):
            _zero_pads(p, S, tn)
        wd = wd_ref[...]
        cwv, cwg, bv, bg = cv_ref[...], cg_ref[...], bv_ref[...], bg_ref[...]
        zero = jnp.zeros((1, tn), F32)
        gv = [zero, zero, zero, zero]
        gg = [zero, zero, zero, zero]
        for r0 in range(0, S, tr):
            cv, shv = _conv3(uv_ref, cwv, r0, tr)
            cg, shg = _conv3(ug_ref, cwg, r0, tr)
            cv = cv + bv
            cg = cg + bg
            sg = _sigmoid(cg)
            sl = cg * sg
            a_ref[pl.ds(r0, tr), :] = (sl * cv).astype(BF16)
            da = _dg(dy_ref[pl.ds(r0, tr), :], wd, 1, 1)
            dcv = da * sl
            dcg = da * cv * (sg * (1.0 + cg * (1.0 - sg)))
            pdv[pl.ds(PAD + r0, tr), :] = dcv
            pdg[pl.ds(PAD + r0, tr), :] = dcg
            for k in range(3):
                gv[k] = gv[k] + jnp.sum(dcv * shv[k], axis=0, keepdims=True)
                gg[k] = gg[k] + jnp.sum(dcg * shg[k], axis=0, keepdims=True)
            gv[3] = gv[3] + jnp.sum(dcv, axis=0, keepdims=True)
            gg[3] = gg[3] + jnp.sum(dcg, axis=0, keepdims=True)
        gwv_ref[...] = _rows8(gv)
        gwg_ref[...] = _rows8(gg)
        for r0 in range(0, S, tr):
            for pd, cw, out in ((pdv, cwv, duv_ref), (pdg, cwg, dug_ref)):
                dm = pd[pl.ds(PAD - 1 + r0, tr), :]
                dc = pd[pl.ds(PAD + r0, tr), :]
                dp = pd[pl.ds(PAD + 1 + r0, tr), :]
                out[pl.ds(r0, tr), :] = (cw[0:1, :] * dp + cw[1:2, :] * dc + cw[2:3, :] * dm).astype(BF16)

    col = lambda off: (lambda j: (0, j + off))
    blk = pl.BlockSpec((S, tn), col(0))
    padded = pl.BlockSpec((S + 2 * PAD, tn), col(0))
    g8 = pl.BlockSpec((8, tn), col(0))
    return pl.pallas_call(
        body, name=name, grid=(nb,),
        in_specs=[pl.BlockSpec((S, D), lambda j: (0, 0)), pl.BlockSpec((tn, D), lambda j: (j, 0)), padded, padded,
                  pl.BlockSpec((3, tn), col(0)), pl.BlockSpec((3, tn), col(nb)),
                  pl.BlockSpec((1, tn), col(0)), pl.BlockSpec((1, tn), col(nb))],
        out_specs=[blk, blk, blk, g8, g8],
        out_shape=[jax.ShapeDtypeStruct((S, F), BF16), jax.ShapeDtypeStruct((S, F), BF16),
                   jax.ShapeDtypeStruct((S, F), BF16), jax.ShapeDtypeStruct((8, F), F32),
                   jax.ShapeDtypeStruct((8, F), F32)],
        scratch_shapes=[pltpu.VMEM((S + 2 * PAD, tn), F32)] * 2,
        compiler_params=_cp(("parallel",)),
    )(dyb, wdown, uv, ug, wconv, wconv, bconv, bconv)


def _log_sigmoid(x):
    return jnp.minimum(x, 0.0) - jnp.log(1.0 + jnp.exp(-jnp.abs(x)))


def _gla_gate_fwd(proj, wgf, bgf, wgb, bgb, name):
    S = proj.shape[0]
    tm = _pick(S, (512, 256))

    def body(r_ref, wf_ref, bf_ref, wb_ref, bb_ref, laf_ref, lab_ref):
        r = r_ref[...].astype(BF16)
        lf = jnp.dot(r, wf_ref[...].astype(BF16), preferred_element_type=F32) + bf_ref[...]
        lb = jnp.dot(r, wb_ref[...].astype(BF16), preferred_element_type=F32) + bb_ref[...]
        laf_ref[...] = _log_sigmoid(lf) * (1.0 / GLA_GATE_NORMALIZER)
        lab_ref[...] = _log_sigmoid(lb) * (1.0 / GLA_GATE_NORMALIZER)

    full = lambda shp: pl.BlockSpec(shp, lambda i: (0, 0))
    row = pl.BlockSpec((tm, GLA_KEY), lambda i: (i, 0))
    return pl.pallas_call(
        body, name=name, grid=(S // tm,),
        in_specs=[pl.BlockSpec((tm, LANES), lambda i: (i, GLA_R_BLOCK)),
                  full((LANES, GLA_KEY)), full((1, GLA_KEY)), full((LANES, GLA_KEY)), full((1, GLA_KEY))],
        out_specs=[row, row],
        out_shape=[jax.ShapeDtypeStruct((S, GLA_KEY), F32)] * 2,
        compiler_params=_cp(("parallel",)),
    )(proj, wgf, bgf, wgb, bgb)


def _gla_gate_bwd(dlaf, dlab, proj, wgf, bgf, wgb, bgb, name):
    S = proj.shape[0]
    tm = _pick(S, (512, 256))

    def body(dlf_ref, dlb_ref, r_ref, wf_ref, bf_ref, wb_ref, bb_ref, dr_ref, dwf_ref, dbf_ref, dwb_ref, dbb_ref):
        i = pl.program_id(0)
        r = r_ref[...].astype(BF16)
        wf = wf_ref[...].astype(BF16)
        wb = wb_ref[...].astype(BF16)
        lf = jnp.dot(r, wf, preferred_element_type=F32) + bf_ref[...]
        lb = jnp.dot(r, wb, preferred_element_type=F32) + bb_ref[...]
        glf = dlf_ref[...] * (1.0 / GLA_GATE_NORMALIZER) * (1.0 / (1.0 + jnp.exp(lf)))
        glb = dlb_ref[...] * (1.0 / GLA_GATE_NORMALIZER) * (1.0 / (1.0 + jnp.exp(lb)))
        gfb = glf.astype(BF16)
        gbb = glb.astype(BF16)
        dr_ref[...] = _dg(gfb, wf, 1, 1) + _dg(gbb, wb, 1, 1)
        parts = (_dg(r, gfb, 0, 0), jnp.sum(glf, axis=0, keepdims=True),
                 _dg(r, gbb, 0, 0), jnp.sum(glb, axis=0, keepdims=True))
        outs = (dwf_ref, dbf_ref, dwb_ref, dbb_ref)

        @pl.when(i == 0)
        def _():
            for o, p in zip(outs, parts):
                o[...] = p

        @pl.when(i > 0)
        def _():
            for o, p in zip(outs, parts):
                o[...] += p

    full = lambda shp: pl.BlockSpec(shp, lambda i: (0, 0))
    row = pl.BlockSpec((tm, GLA_KEY), lambda i: (i, 0))
    return pl.pallas_call(
        body, name=name, grid=(S // tm,),
        in_specs=[row, row, pl.BlockSpec((tm, LANES), lambda i: (i, GLA_R_BLOCK)),
                  full((LANES, GLA_KEY)), full((1, GLA_KEY)), full((LANES, GLA_KEY)), full((1, GLA_KEY))],
        out_specs=[pl.BlockSpec((tm, LANES), lambda i: (i, 0)),
                   full((LANES, GLA_KEY)), full((1, GLA_KEY)), full((LANES, GLA_KEY)), full((1, GLA_KEY))],
        out_shape=[jax.ShapeDtypeStruct((S, LANES), F32),
                   jax.ShapeDtypeStruct((LANES, GLA_KEY), F32), jax.ShapeDtypeStruct((1, GLA_KEY), F32),
                   jax.ShapeDtypeStruct((LANES, GLA_KEY), F32), jax.ShapeDtypeStruct((1, GLA_KEY), F32)],
        compiler_params=_cp(("arbitrary",)),
    )(dlaf, dlab, proj, wgf, bgf, wgb, bgb)


def _gla_masks(rev):
    C = GLA_CHUNK
    t = lax.broadcasted_iota(jnp.int32, (C, C), 0)
    s = lax.broadcasted_iota(jnp.int32, (C, C), 1)
    if rev:
        return (s >= t), (s > t), (t >= s), (t > s)
    return (s <= t), (s <= t), (t <= s), (t <= s)


def _cum_dot(cum, x):
    return jnp.dot(cum.astype(F32), x, precision=HIGHEST, preferred_element_type=F32)


def _gla_chunk_common(q, k, la, cum, end_row):
    b = _cum_dot(cum, la)
    bend = b[end_row:end_row + 1, :]
    e = jnp.exp(b)
    qd = q * (GLA_DK ** -0.5) * e
    ei = jnp.exp(-b)
    ee = jnp.exp(bend - b)
    d = jnp.exp(bend)
    return e, ei, ee, d, qd, k * ei, k * ee


GLA_CB = 32


def _gla_specs(S, rev_order):
    n = S // GLA_CHUNK
    cb = min(GLA_CB, n)
    nblk = n // cb
    rows = cb * GLA_CHUNK
    ci = (lambda i: nblk - 1 - i) if rev_order else (lambda i: i)
    q_spec = pl.BlockSpec((rows, GLA_DK), lambda h, i: (ci(i), h))
    k_spec = pl.BlockSpec((rows, GLA_DK), lambda h, i: (ci(i), GLA_HEADS + h))
    v_spec = pl.BlockSpec((rows, GLA_DV), lambda h, i: (ci(i), GLA_KEY * 2 // GLA_DV + h))
    la_spec = pl.BlockSpec((rows, GLA_DK), lambda h, i: (ci(i), h))
    o_spec = pl.BlockSpec((rows, GLA_DV), lambda h, i: (ci(i), h))
    st_spec = pl.BlockSpec((1, cb, GLA_DV, GLA_DK), lambda h, i: (h, ci(i), 0, 0))
    return n, cb, nblk, q_spec, k_spec, v_spec, la_spec, o_spec, st_spec


def _gla_scan_fwd(proj, la, rev, name):
    S = proj.shape[0]
    C = GLA_CHUNK
    n, cb, nblk, q_spec, k_spec, v_spec, la_spec, o_spec, st_spec = _gla_specs(S, rev)
    end_row = 0 if rev else C - 1
    order = list(range(cb))[::-1] if rev else list(range(cb))

    def body(q_ref, k_ref, v_ref, la_ref, o_ref, st_ref, state):
        i = pl.program_id(1)

        @pl.when(i == 0)
        def _():
            state[...] = jnp.zeros_like(state)

        cum, mask, _, _ = _gla_masks(rev)
        pre, intra, kv = {}, {}, {}
        for cc in order:
            rows = pl.ds(cc * C, C)
            q, k, v, lav = q_ref[rows, :], k_ref[rows, :], v_ref[rows, :], la_ref[rows, :]
            _, _, _, d, qd, ki, ke = _gla_chunk_common(q, k, lav, cum, end_row)
            qdb, kib, keb, vb = qd.astype(BF16), ki.astype(BF16), ke.astype(BF16), v.astype(BF16)
            pre[cc] = (d, qdb)
            att = jnp.where(mask, _dg(qdb, kib, 1, 1), 0.0)
            intra[cc] = jnp.dot(att.astype(BF16), vb, preferred_element_type=F32)
            kv[cc] = _dg(vb, keb, 0, 0)
        st = state[...]
        for cc in order:
            d, qdb = pre[cc]
            o_ref[pl.ds(cc * C, C), :] = intra[cc] + _dg(qdb, st.astype(BF16), 1, 1)
            st_ref[0, cc] = st
            st = st * d + kv[cc]
        state[...] = st

    return pl.pallas_call(
        body, name=name, grid=(GLA_HEADS, nblk),
        in_specs=[q_spec, k_spec, v_spec, la_spec],
        out_specs=[o_spec, st_spec],
        out_shape=[jax.ShapeDtypeStruct((S, GLA_VAL), F32),
                   jax.ShapeDtypeStruct((GLA_HEADS, n, GLA_DV, GLA_DK), F32)],
        scratch_shapes=[pltpu.VMEM((GLA_DV, GLA_DK), F32)],
        compiler_params=_cp(("parallel", "arbitrary")),
    )(proj, proj, proj, la)


def _gla_scan_bwd(do, proj, la, states, rev, name):
    S = proj.shape[0]
    C = GLA_CHUNK
    n, cb, nblk, q_spec, k_spec, v_spec, la_spec, o_spec, st_spec = _gla_specs(S, not rev)
    end_row = 0 if rev else C - 1
    order = list(range(cb)) if rev else list(range(cb))[::-1]

    def body(do_ref, q_ref, k_ref, v_ref, la_ref, st_ref, dq_ref, dk_ref, dv_ref, dla_ref, gstate):
        i = pl.program_id(1)

        @pl.when(i == 0)
        def _():
            gstate[...] = jnp.zeros_like(gstate)

        cum, mask, cum_t, mask_t = _gla_masks(rev)
        g = gstate[...]
        for cc in order:
            rows = pl.ds(cc * C, C)
            q, k, v, lav = q_ref[rows, :], k_ref[rows, :], v_ref[rows, :], la_ref[rows, :]
            dov = do_ref[rows, :]
            st = st_ref[0, cc]
            e, ei, ee, d, qd, ki, ke = _gla_chunk_common(q, k, lav, cum, end_row)
            qdb, kib, keb, vb = qd.astype(BF16), ki.astype(BF16), ke.astype(BF16), v.astype(BF16)
            dob, gb, stb = dov.astype(BF16), g.astype(BF16), st.astype(BF16)
            att_t = jnp.where(mask_t, _dg(kib, qdb, 1, 1), 0.0)
            da = jnp.where(mask, _dg(dob, vb, 1, 1), 0.0)
            da_t = jnp.where(mask_t, _dg(vb, dob, 1, 1), 0.0)
            dv_ref[rows, :] = jnp.dot(att_t.astype(BF16), dob, preferred_element_type=F32) + _dg(keb, gb, 1, 1)
            dqd = (jnp.dot(da.astype(BF16), kib, preferred_element_type=F32)
                   + jnp.dot(dob, stb, preferred_element_type=F32))
            dki = jnp.dot(da_t.astype(BF16), qdb, preferred_element_type=F32)
            dke = jnp.dot(vb, gb, preferred_element_type=F32)
            dd = jnp.sum(st * g, axis=0, keepdims=True)
            g = g * d + _dg(dob, qdb, 0, 0)
            dq_ref[rows, :] = dqd * e * (GLA_DK ** -0.5)
            dk_ref[rows, :] = dki * ei + dke * ee
            dkeke = dke * ke
            db = dqd * qd - dki * ki - dkeke
            dbend = jnp.sum(dkeke, axis=0, keepdims=True) + dd * d
            dla_ref[rows, :] = _cum_dot(cum_t, db) + dbend
        gstate[...] = g

    key_out = la_spec
    return pl.pallas_call(
        body, name=name, grid=(GLA_HEADS, nblk),
        in_specs=[o_spec, q_spec, k_spec, v_spec, la_spec, st_spec],
        out_specs=[key_out, key_out, o_spec, key_out],
        out_shape=[jax.ShapeDtypeStruct((S, GLA_KEY), F32), jax.ShapeDtypeStruct((S, GLA_KEY), F32),
                   jax.ShapeDtypeStruct((S, GLA_VAL), F32), jax.ShapeDtypeStruct((S, GLA_KEY), F32)],
        scratch_shapes=[pltpu.VMEM((GLA_DV, GLA_DK), F32)],
        compiler_params=_cp(("parallel", "arbitrary")),
    )(do, proj, proj, proj, la, states)


def _gla_out_fwd(of, ob, proj, gn, name):
    S = of.shape[0]
    tm = _pick(S, (256, 128))
    gblk = (2 * GLA_KEY + GLA_VAL) // GLA_VAL

    def body(of_ref, ob_ref, g_ref, gn_ref, z_ref):
        gnv = gn_ref[...]
        for h in range(GLA_HEADS):
            cols = pl.ds(h * GLA_DV, GLA_DV)
            o = of_ref[:, cols] + ob_ref[:, cols]
            r = lax.rsqrt(jnp.mean(o * o, axis=-1, keepdims=True) + NORM_EPS)
            gv = g_ref[:, cols]
            z_ref[:, cols] = (o * r * gnv * (gv * _sigmoid(gv))).astype(BF16)

    row = pl.BlockSpec((tm, GLA_VAL), lambda i: (i, 0))
    return pl.pallas_call(
        body, name=name, grid=(S // tm,),
        in_specs=[row, row, pl.BlockSpec((tm, GLA_VAL), lambda i: (i, gblk)),
                  pl.BlockSpec((1, GLA_DV), lambda i: (0, 0))],
        out_specs=row,
        out_shape=jax.ShapeDtypeStruct((S, GLA_VAL), BF16),
        compiler_params=_cp(("parallel",)),
    )(of, ob, proj, gn)


def _gla_out_bwd(dz, of, ob, proj, gn, name):
    S = of.shape[0]
    tm = _pick(S, (256, 128))
    gblk = (2 * GLA_KEY + GLA_VAL) // GLA_VAL

    def body(dz_ref, of_ref, ob_ref, g_ref, gn_ref, do_ref, dg_ref, dgn_ref):
        i = pl.program_id(0)
        gnv = gn_ref[...]
        part = jnp.zeros((1, GLA_DV), F32)
        for h in range(GLA_HEADS):
            cols = pl.ds(h * GLA_DV, GLA_DV)
            o = of_ref[:, cols] + ob_ref[:, cols]
            r = lax.rsqrt(jnp.mean(o * o, axis=-1, keepdims=True) + NORM_EPS)
            y = o * r
            gv = g_ref[:, cols]
            sg = _sigmoid(gv)
            dzv = dz_ref[:, cols]
            dg_ref[:, cols] = dzv * (y * gnv) * (sg * (1.0 + gv * (1.0 - sg)))
            don = dzv * (gv * sg)
            part = part + jnp.sum(don * y, axis=0, keepdims=True)
            dy = don * gnv
            do_ref[:, cols] = r * (dy - y * jnp.mean(dy * y, axis=-1, keepdims=True))

        @pl.when(i == 0)
        def _():
            dgn_ref[...] = part

        @pl.when(i > 0)
        def _():
            dgn_ref[...] += part

    row = pl.BlockSpec((tm, GLA_VAL), lambda i: (i, 0))
    one = pl.BlockSpec((1, GLA_DV), lambda i: (0, 0))
    return pl.pallas_call(
        body, name=name, grid=(S // tm,),
        in_specs=[row, row, row, pl.BlockSpec((tm, GLA_VAL), lambda i: (i, gblk)), one],
        out_specs=[row, row, one],
        out_shape=[jax.ShapeDtypeStruct((S, GLA_VAL), F32), jax.ShapeDtypeStruct((S, GLA_VAL), F32),
                   jax.ShapeDtypeStruct((1, GLA_DV), F32)],
        compiler_params=_cp(("arbitrary",)),
    )(dz, of, ob, proj, gn)


N_QK_HEADS = ATT_QH + ATT_KVH


def _qk_prep_fwd(proj, qn, kn, rc, rs, name):
    S = proj.shape[0]
    tm = _pick(S, (256, 128))
    W = N_QK_HEADS * ATT_HD
    scale = ATT_HD ** -0.5

    def body(p_ref, qn_ref, kn_ref, rc_ref, rs_ref, v_in_ref, qk_ref, v_ref, kt_ref, vt_ref):
        c, s = rc_ref[...], rs_ref[...]
        for h in range(N_QK_HEADS):
            cols = pl.ds(h * ATT_HD, ATT_HD)
            w = qn_ref[...] if h < ATT_QH else kn_ref[...]
            xv = p_ref[:, cols]
            r = lax.rsqrt(jnp.mean(xv * xv, axis=-1, keepdims=True) + NORM_EPS)
            y = xv * r * w
            out = y * c + pltpu.roll(y, ATT_HD // 2, 1) * s
            if h < ATT_QH:
                qk_ref[:, cols] = (out * scale).astype(BF16)
            else:
                qk_ref[:, cols] = out.astype(BF16)
                kt_ref[pl.ds((h - ATT_QH) * ATT_HD, ATT_HD), :] = out.T.astype(BF16)
        v_ref[...] = v_in_ref[...].astype(BF16)
        for h in range(ATT_KVH):
            vt_ref[pl.ds(h * ATT_HD, ATT_HD), :] = v_in_ref[:, pl.ds(h * ATT_HD, ATT_HD)].T.astype(BF16)

    one = pl.BlockSpec((1, ATT_HD), lambda i: (0, 0))
    tab = pl.BlockSpec((tm, ATT_HD), lambda i: (i, 0))
    vw = ATT_KVH * ATT_HD
    tr = pl.BlockSpec((vw, tm), lambda i: (0, i))
    return pl.pallas_call(
        body, name=name, grid=(S // tm,),
        in_specs=[pl.BlockSpec((tm, W), lambda i: (i, 0)), one, one, tab, tab,
                  pl.BlockSpec((tm, vw), lambda i: (i, W // vw))],
        out_specs=[pl.BlockSpec((tm, W), lambda i: (i, 0)), pl.BlockSpec((tm, vw), lambda i: (i, 0)), tr, tr],
        out_shape=[jax.ShapeDtypeStruct((S, W), BF16), jax.ShapeDtypeStruct((S, vw), BF16),
                   jax.ShapeDtypeStruct((vw, S), BF16), jax.ShapeDtypeStruct((vw, S), BF16)],
        compiler_params=_cp(("parallel",)),
    )(proj, qn, kn, rc, rs, proj)


def _qk_prep_bwd(dqk, proj, qn, kn, rc, rs, name):
    S = proj.shape[0]
    tm = _pick(S, (256, 128))
    W = N_QK_HEADS * ATT_HD

    def body(d_ref, p_ref, qn_ref, kn_ref, rc_ref, rs_ref, dp_ref, dqn_ref, dkn_ref):
        i = pl.program_id(0)
        c, s = rc_ref[...], rs_ref[...]
        parts = [jnp.zeros((1, ATT_HD), F32), jnp.zeros((1, ATT_HD), F32)]
        for h in range(N_QK_HEADS):
            cols = pl.ds(h * ATT_HD, ATT_HD)
            w = qn_ref[...] if h < ATT_QH else kn_ref[...]
            dout = d_ref[:, cols]
            dy = dout * c + pltpu.roll(dout * s, ATT_HD // 2, 1)
            xv = p_ref[:, cols]
            r = lax.rsqrt(jnp.mean(xv * xv, axis=-1, keepdims=True) + NORM_EPS)
            xr = xv * r
            which = 0 if h < ATT_QH else 1
            parts[which] = parts[which] + jnp.sum(dy * xr, axis=0, keepdims=True)
            dxr = dy * w
            dp_ref[:, cols] = r * (dxr - xr * jnp.mean(dxr * xr, axis=-1, keepdims=True))

        @pl.when(i == 0)
        def _():
            dqn_ref[...] = parts[0]
            dkn_ref[...] = parts[1]

        @pl.when(i > 0)
        def _():
            dqn_ref[...] += parts[0]
            dkn_ref[...] += parts[1]

    one = pl.BlockSpec((1, ATT_HD), lambda i: (0, 0))
    tab = pl.BlockSpec((tm, ATT_HD), lambda i: (i, 0))
    row = pl.BlockSpec((tm, W), lambda i: (i, 0))
    return pl.pallas_call(
        body, name=name, grid=(S // tm,),
        in_specs=[row, row, one, one, tab, tab],
        out_specs=[row, one, one],
        out_shape=[jax.ShapeDtypeStruct((S, W), F32), jax.ShapeDtypeStruct((1, ATT_HD), F32),
                   jax.ShapeDtypeStruct((1, ATT_HD), F32)],
        compiler_params=_cp(("arbitrary",)),
    )(dqk, proj, qn, kn, rc, rs)


ATT_TQ = 1024
LSE_ROWS = 8


def _attn_fwd(qk, vt, name):
    S = qk.shape[0]
    tq = min(ATT_TQ, S)

    def body(q_ref, k_ref, vt_ref, o_ref, lse_ref):
        st = _dg(k_ref[...], q_ref[...], 1, 1)
        m = jnp.max(st, axis=0, keepdims=True)
        pt = jnp.exp(st - m)
        l = jnp.sum(pt, axis=0, keepdims=True)
        ot = jnp.dot(vt_ref[...], pt.astype(BF16), preferred_element_type=F32)
        o_ref[...] = (ot * (1.0 / l)).T
        lse_ref[...] = jnp.broadcast_to(m + jnp.log(l), (LSE_ROWS, tq))

    qo = pl.BlockSpec((tq, ATT_HD), lambda h, i: (i, h))
    return pl.pallas_call(
        body, name=name, grid=(ATT_QH, S // tq),
        in_specs=[qo, pl.BlockSpec((S, ATT_HD), lambda h, i: (0, ATT_QH + h // ATT_GROUP)),
                  pl.BlockSpec((ATT_HD, S), lambda h, i: (h // ATT_GROUP, 0))],
        out_specs=[qo, pl.BlockSpec((LSE_ROWS, tq), lambda h, i: (h, i))],
        out_shape=[jax.ShapeDtypeStruct((S, ATT_QH * ATT_HD), F32),
                   jax.ShapeDtypeStruct((ATT_QH * LSE_ROWS, S), F32)],
        compiler_params=_cp(("parallel", "parallel")),
    )(qk, qk, vt)


def _attn_bwd(do, o, lse, qk, v, kt, name):
    S = qk.shape[0]
    tq = min(ATT_TQ, S)
    scale = ATT_HD ** -0.5

    def body(do_ref, o_ref, lse_ref, q_ref, k_ref, v_ref, kt_ref, dq_ref, dk_ref, dv_ref):
        g = pl.program_id(1)
        i = pl.program_id(2)

        @pl.when((g == 0) & (i == 0))
        def _():
            dk_ref[...] = jnp.zeros_like(dk_ref)
            dv_ref[...] = jnp.zeros_like(dv_ref)

        q = q_ref[...]
        dov = do_ref[...]
        dob = dov.astype(BF16)
        delta = jnp.sum((dov * o_ref[...]).T, axis=0, keepdims=True)
        st = _dg(k_ref[...], q, 1, 1)
        pt = jnp.exp(st - lse_ref[0:1, :])
        dpt = _dg(v_ref[...], dob, 1, 1)
        dst = (pt * (dpt - delta)).astype(BF16)
        dv_ref[...] += jnp.dot(pt.astype(BF16), dob, preferred_element_type=F32)
        dk_ref[...] += jnp.dot(dst, q, preferred_element_type=F32)
        dq_ref[...] = jnp.dot(kt_ref[...], dst, preferred_element_type=F32).T * scale

    qo = pl.BlockSpec((tq, ATT_HD), lambda kv, g, i: (i, kv * ATT_GROUP + g))
    kvo = pl.BlockSpec((S, ATT_HD), lambda kv, g, i: (0, kv))
    return pl.pallas_call(
        body, name=name, grid=(ATT_KVH, ATT_GROUP, S // tq),
        in_specs=[qo, qo, pl.BlockSpec((LSE_ROWS, tq), lambda kv, g, i: (kv * ATT_GROUP + g, i)), qo,
                  pl.BlockSpec((S, ATT_HD), lambda kv, g, i: (0, ATT_QH + kv)), kvo,
                  pl.BlockSpec((ATT_HD, S), lambda kv, g, i: (kv, 0))],
        out_specs=[qo, kvo, kvo],
        out_shape=[jax.ShapeDtypeStruct((S, ATT_QH * ATT_HD), F32),
                   jax.ShapeDtypeStruct((S, ATT_KVH * ATT_HD), F32),
                   jax.ShapeDtypeStruct((S, ATT_KVH * ATT_HD), F32)],
        compiler_params=_cp(("parallel", "arbitrary", "arbitrary")),
    )(do, o, lse, qk, qk, v, kt)


def _adamw(w, g, m, v, name):
    rows, cols = w.shape
    tr = rows
    for cand in (512, 256, 128, 64, 32, 16, 8):
        if rows % cand == 0 and cand * cols * 4 <= 2 * 1024 * 1024:
            tr = cand
            break

    def body(w_ref, g_ref, m_ref, v_ref, d_ref, nm_ref, nv_ref):
        gv = g_ref[...]
        nm = ADAM_B1 * m_ref[...] + (1.0 - ADAM_B1) * gv
        nv = ADAM_B2 * v_ref[...] + (1.0 - ADAM_B2) * (gv * gv)
        m_hat = nm / (1.0 - ADAM_B1 ** ADAM_STEP)
        v_hat = nv / (1.0 - ADAM_B2 ** ADAM_STEP)
        d_ref[...] = -ADAM_LR * (m_hat / (jnp.sqrt(v_hat) + ADAM_EPS) + ADAM_WD * w_ref[...])
        nm_ref[...] = nm
        nv_ref[...] = nv

    blk = pl.BlockSpec((tr, cols), lambda i: (i, 0))
    return pl.pallas_call(
        body, name=name, grid=(rows // tr,),
        in_specs=[blk] * 4, out_specs=[blk] * 3,
        out_shape=[jax.ShapeDtypeStruct((rows, cols), F32)] * 3,
        compiler_params=_cp(("parallel",)),
    )(w, g, m, v)


ANY = pl.BlockSpec(memory_space=pl.ANY)


def _place():
    return lax.axis_index("x"), lax.axis_index("y"), lax.axis_index("c")


def _other_chips(x, y):
    return [(1 - x, y), (x, 1 - y), (1 - x, 1 - y)]


def _half_rows(c, H):
    return pl.ds(pl.multiple_of(c * H, 8), H)


def _allreduce_small(v, name):
    R = v.shape[0]
    n_dev = 8

    def body(v_ref, sum_ref, all_ref, token_ref, send_sems, recv_sems, local_sem):
        token_ref[...] = jnp.zeros_like(token_ref)
        x, y, c = _place()
        me, sibling = (x, y, c), (x, y, 1 - c)
        chips = _other_chips(x, y)

        def rows(px, py, pc):
            return all_ref.at[pl.ds(pl.multiple_of((4 * px + 2 * py + pc) * R, 8), R), :]

        def copy(k, block, to, src=None):
            return pltpu.make_async_remote_copy(
                src_ref=rows(*block) if src is None else src, dst_ref=rows(*block),
                send_sem=send_sems.at[k], recv_sem=recv_sems.at[k], device_id=to, device_id_type=MESH)

        own = pltpu.make_async_copy(v_ref, rows(*me), local_sem)
        own.start()
        first = [copy(0, me, sibling, src=v_ref)]
        first += [copy(1 + j, me, (*chip, c), src=v_ref) for j, chip in enumerate(chips)]
        for cp in first:
            cp.start()
        passed = [copy(4 + j, (*chip, c), sibling) for j, chip in enumerate(chips)]
        for j, chip in enumerate(chips):
            copy(1 + j, (*chip, c), me).wait_recv()
            passed[j].start()
        copy(0, sibling, me).wait_recv()
        for j, chip in enumerate(chips):
            copy(4 + j, (*chip, 1 - c), me).wait_recv()
        for cp in first + passed:
            cp.wait_send()
        own.wait()
        acc = all_ref[pl.ds(0, R), :]
        for d in range(1, n_dev):
            acc = acc + all_ref[pl.ds(d * R, R), :]
        sum_ref[...] = acc

    vm = pl.BlockSpec(memory_space=pltpu.VMEM)
    total, _, token = pl.pallas_call(
        body, name=name,
        in_specs=[vm], out_specs=[vm, vm, vm],
        out_shape=[jax.ShapeDtypeStruct((R, LANES), F32), jax.ShapeDtypeStruct((n_dev * R, LANES), F32),
                   jax.ShapeDtypeStruct((8, LANES), F32)],
        scratch_shapes=[pltpu.SemaphoreType.DMA((7,)), pltpu.SemaphoreType.DMA((7,)), pltpu.SemaphoreType.DMA],
    )(v)
    return total, token


def _join_halves(bufs, name):
    n = len(bufs)
    halves = [b.shape[0] // 2 for b in bufs]

    def body(*refs):
        outs = refs[n:2 * n]
        send_sems, recv_sems = refs[2 * n:]
        x, y, c = _place()

        def copy(k, core):
            blk = outs[k].at[_half_rows(core, halves[k])]
            return pltpu.make_async_remote_copy(src_ref=blk, dst_ref=blk, send_sem=send_sems.at[k],
                                                recv_sem=recv_sems.at[k], device_id=(x, y, 1 - c),
                                                device_id_type=MESH)

        sends = [copy(k, c) for k in range(n)]
        for cp in sends:
            cp.start()
        for k in range(n):
            copy(k, 1 - c).wait_recv()
        for cp in sends:
            cp.wait_send()

    return pl.pallas_call(
        body, name=name, in_specs=[ANY] * n, out_specs=[ANY] * n,
        out_shape=[jax.ShapeDtypeStruct(b.shape, b.dtype) for b in bufs],
        input_output_aliases={k: k for k in range(n)},
        scratch_shapes=[pltpu.SemaphoreType.DMA((n,)), pltpu.SemaphoreType.DMA((n,))],
    )(*bufs)


def _rs_rows(H, width):
    for cand in range(H, 0, -16):
        if H % cand == 0 and cand % 16 == 0 and cand * width * 4 <= 1536 * 1024:
            return cand
    return H


def _add_sibling(g, got, c, me, name):
    _, H, width = got.shape
    tb = _rs_rows(H, width)
    nb = H // tb

    def body(sp_ref, g_ref, got_ref, sb_ref, sf_ref):
        p = pl.program_id(1)
        s = g_ref[0] + got_ref[0]
        sb_ref[0] = s.astype(BF16)

        @pl.when(p == sp_ref[1])
        def _():
            sf_ref[...] = s

    grid_spec = pltpu.PrefetchScalarGridSpec(
        num_scalar_prefetch=1, grid=(nb, N_CHIPS),
        in_specs=[pl.BlockSpec((1, tb, width), lambda i, p, sp: (p, sp[0] * nb + i, 0)),
                  pl.BlockSpec((1, tb, width), lambda i, p, sp: (p, i, 0))],
        out_specs=[pl.BlockSpec((1, tb, width), lambda i, p, sp: (p, i, 0)),
                   pl.BlockSpec((tb, width), lambda i, p, sp: (i, 0))])
    return pl.pallas_call(
        body, name=name, grid_spec=grid_spec,
        out_shape=[jax.ShapeDtypeStruct((N_CHIPS, H, width), BF16), jax.ShapeDtypeStruct((H, width), F32)],
        compiler_params=_cp(("arbitrary", "arbitrary")),
    )(jnp.stack([c, me]).astype(jnp.int32), g, got)


def _add_chips(sf, got, others_and_c, name):
    H, width = sf.shape
    tb = _rs_rows(H, width)
    nb = H // tb

    def body(sp_ref, sf_ref, r1_ref, r2_ref, r3_ref, out_ref):
        out_ref[...] = ((sf_ref[...] + r1_ref[0].astype(F32)) + r2_ref[0].astype(F32)) + r3_ref[0].astype(F32)

    def slot(k):
        return pl.BlockSpec((1, tb, width), lambda i, sp: (sp[k], i, 0))

    blk = pl.BlockSpec((tb, width), lambda i, sp: (i, 0))
    grid_spec = pltpu.PrefetchScalarGridSpec(
        num_scalar_prefetch=1, grid=(nb,), in_specs=[blk, slot(0), slot(1), slot(2)],
        out_specs=pl.BlockSpec((tb, width), lambda i, sp: (sp[3] * nb + i, 0)))
    return pl.pallas_call(
        body, name=name, grid_spec=grid_spec,
        out_shape=jax.ShapeDtypeStruct((2 * H, width), F32),
        compiler_params=_cp(("arbitrary",)),
    )(others_and_c.astype(jnp.int32), sf, got, got, got)


REPLICATED = ("norm_mix", "norm_ffn", "gla_b_gate_f", "gla_b_gate_b", "gla_norm", "attn_q_norm", "attn_k_norm",
              "ffn_b_conv")


PIECE_ROWS = 16


def _piece_rows(shape):
    n = 1
    for s in shape:
        n *= s
    rows = n // LANES
    return rows, -(-rows // PIECE_ROWS) * PIECE_ROWS


def _pack(pieces, dtype, row_multiple):
    flat = []
    for p in pieces:
        rows, padded = _piece_rows(p.shape)
        flat.append(jnp.pad(p.astype(dtype).reshape(rows, LANES), ((0, padded - rows), (0, 0))))
    rows = sum(f.shape[0] for f in flat)
    padded = -(-rows // row_multiple) * row_multiple
    if padded > rows:
        flat.append(jnp.zeros((padded - rows, LANES), dtype))
    return jnp.concatenate(flat, axis=0)


def _unpack(buf, shapes):
    out, r = [], 0
    for shp in shapes:
        rows, padded = _piece_rows(shp)
        out.append(buf[r:r + rows].reshape(shp))
        r += padded
    return out


def _own_slot(shard2d, me):
    return lax.dynamic_update_index_in_dim(lax.empty((N_CHIPS,) + shard2d.shape, shard2d.dtype), shard2d, me, 0)


def _layer_small(w, l):
    j = l // 2
    if l % 2 == 0:
        return [w["gla_w_gate_up_f"][j], w["gla_w_gate_up_b"][j], w["ffn_w_conv"][l]]
    return [w["ffn_w_conv"][l]]


def _layer_weight_bufs(w, l, me):
    j = l // 2
    mixer = ("gla_w_in", "gla_w_out") if l % 2 == 0 else ("attn_w_qkv", "attn_w_out")
    bufs = [_own_slot(w[n][j].astype(BF16), me) for n in mixer]
    bufs.append(_own_slot(_pack(_layer_small(w, l), F32, 32), me))
    bufs += [_own_slot(w["ffn_w_up"][l].astype(BF16), me), _own_slot(w["ffn_w_down"][l].astype(BF16), me)]
    return bufs


N_MIXER_BUFS = 3


def _layer_weights(w, l, got):
    rows = lambda t: t.reshape(-1, t.shape[2])
    cols = lambda t: jnp.concatenate([t[p] for p in range(N_CHIPS)], axis=1)
    out = {}
    if len(got) != N_MIXER_BUFS:
        up, down = got[-2:]
        out.update(up=up, up_full=cols(up), down=rows(down))
    if len(got) != 2:
        mix_in, mix_out, small = got[:N_MIXER_BUFS]
        shapes = [t.shape for t in _layer_small(w, l)]
        parts = [_unpack(small[p], shapes) for p in range(N_CHIPS)]
        full_small = [jnp.concatenate([parts[p][k] for p in range(N_CHIPS)], axis=-1) for k in range(len(shapes))]
        out.update(conv=full_small[-1])
        if l % 2 == 0:
            out.update(gla_in=jnp.pad(cols(mix_in), ((0, 0), (0, GLA_IN_PAD - GLA_IN))), gla_out=rows(mix_out),
                       gate_f=full_small[0], gate_b=full_small[1])
        else:
            out.update(qkv=mix_in, attn_out=rows(mix_out))
    return out


HBM = pl.BlockSpec(memory_space=pltpu.HBM)
SEM = pl.BlockSpec(memory_space=pltpu.SEMAPHORE)
SIDE_EFFECT = pltpu.SideEffectType.DATAFLOW_SIDE_EFFECTING


def _gather_start(bufs, after, name):
    n = len(bufs)
    halves = [b.shape[1] // 2 for b in bufs]

    def body(*refs):
        refs = refs[:n] + refs[n + 1:]
        send_sems, recv_sems = refs[n:2 * n], refs[2 * n:3 * n]
        outs, token = refs[3 * n:4 * n], refs[4 * n]
        x, y, c = _place()
        me = 2 * x + y
        for k in range(n):
            blk = outs[k].at[me, _half_rows(c, halves[k])]
            for px, py in _other_chips(x, y):
                pltpu.make_async_remote_copy(src_ref=blk, dst_ref=blk, send_sem=send_sems[k], recv_sem=recv_sems[k],
                                             device_id=(px, py, c), device_id_type=MESH).start()
        token[...] = jnp.zeros_like(token)

    res = pl.pallas_call(
        body, name=name,
        in_specs=[HBM] * n + [ANY],
        out_specs=[SEM] * (2 * n) + [HBM] * n + [pl.BlockSpec(memory_space=pltpu.VMEM)],
        out_shape=[pltpu.SemaphoreType.DMA(())] * (2 * n) + [pltpu.HBM(b.shape, b.dtype) for b in bufs]
        + [jax.ShapeDtypeStruct((8, LANES), F32)],
        input_output_aliases={k: 2 * n + k for k in range(n)},
        compiler_params=pltpu.CompilerParams(has_side_effects=SIDE_EFFECT),
    )(*[pltpu.with_memory_space_constraint(b, pltpu.HBM) for b in bufs], after)
    return res[:n], res[n:2 * n], res[2 * n:3 * n], res[3 * n]


def _gather_wait(send_sems, recv_sems, thru, after, name):
    n = len(thru)
    halves = [b.shape[1] // 2 for b in thru]

    def body(*refs):
        ss, rs = refs[n:2 * n], refs[2 * n:3 * n]
        outs = refs[3 * n + 1:]
        x, y, c = _place()
        for k in range(n):
            three = outs[k].at[pl.ds(0, N_CHIPS - 1), _half_rows(c, halves[k])]
            cp = pltpu.make_async_remote_copy(src_ref=three, dst_ref=three, send_sem=ss[k], recv_sem=rs[k],
                                              device_id=(x, y, c), device_id_type=MESH)
            cp.wait_send()
            cp.wait_recv()

    return pl.pallas_call(
        body, name=name,
        in_specs=[HBM] * n + [SEM] * (2 * n) + [ANY],
        out_specs=[HBM] * n,
        out_shape=[pltpu.HBM(b.shape, b.dtype) for b in thru],
        input_output_aliases={k: k for k in range(n)},
        compiler_params=pltpu.CompilerParams(has_side_effects=SIDE_EFFECT),
    )(*thru, *send_sems, *recv_sems, after)


def _swap_start(bufs, name):
    n = len(bufs)
    halves = [b.shape[1] // 2 for b in bufs]
    land_shapes = [(N_CHIPS, h, b.shape[2]) for b, h in zip(bufs, halves)]

    def body(*refs):
        send_sems, recv_sems = refs[2 * n:3 * n], refs[3 * n:4 * n]
        srcs, lands, token = refs[4 * n:5 * n], refs[5 * n:6 * n], refs[6 * n]
        x, y, c = _place()
        for k in range(n):
            for p in range(N_CHIPS):
                pltpu.make_async_remote_copy(src_ref=srcs[k].at[p, _half_rows(1 - c, halves[k])],
                                             dst_ref=lands[k].at[p], send_sem=send_sems[k], recv_sem=recv_sems[k],
                                             device_id=(x, y, 1 - c), device_id_type=MESH).start()
        token[...] = jnp.zeros_like(token)

    hbm = lambda a: pltpu.with_memory_space_constraint(a, pltpu.HBM)
    res = pl.pallas_call(
        body, name=name,
        in_specs=[HBM] * (2 * n),
        out_specs=[SEM] * (2 * n) + [HBM] * (2 * n) + [pl.BlockSpec(memory_space=pltpu.VMEM)],
        out_shape=[pltpu.SemaphoreType.DMA(())] * (2 * n) + [pltpu.HBM(b.shape, b.dtype) for b in bufs]
        + [pltpu.HBM(s, b.dtype) for s, b in zip(land_shapes, bufs)] + [jax.ShapeDtypeStruct((8, LANES), F32)],
        input_output_aliases={k: 2 * n + k for k in range(2 * n)},
        compiler_params=pltpu.CompilerParams(has_side_effects=SIDE_EFFECT),
    )(*[hbm(b) for b in bufs], *[hbm(lax.empty(s, b.dtype)) for s, b in zip(land_shapes, bufs)])
    return res[:n], res[n:2 * n], res[2 * n:3 * n], res[3 * n:4 * n], res[4 * n]


def _swap_wait(send_sems, recv_sems, bufs, lands, after, name):
    n = len(bufs)

    def body(*refs):
        ss, rs = refs[2 * n:3 * n], refs[3 * n:4 * n]
        l_out = refs[5 * n + 1:]
        x, y, c = _place()
        for k in range(n):
            cp = pltpu.make_async_remote_copy(src_ref=l_out[k], dst_ref=l_out[k], send_sem=ss[k], recv_sem=rs[k],
                                              device_id=(x, y, 1 - c), device_id_type=MESH)
            cp.wait_send()
            cp.wait_recv()

    res = pl.pallas_call(
        body, name=name,
        in_specs=[HBM] * (2 * n) + [SEM] * (2 * n) + [ANY],
        out_specs=[HBM] * (2 * n),
        out_shape=[pltpu.HBM(b.shape, b.dtype) for b in bufs] + [pltpu.HBM(l.shape, l.dtype) for l in lands],
        input_output_aliases={k: k for k in range(2 * n)},
        compiler_params=pltpu.CompilerParams(has_side_effects=SIDE_EFFECT),
    )(*bufs, *lands, *send_sems, *recv_sems, after)
    return res[:n], res[n:]


def _send_start(sbs, name):
    n = len(sbs)

    def body(*refs):
        send_sems, recv_sems = refs[2 * n:3 * n], refs[3 * n:4 * n]
        srcs, lands, token = refs[4 * n:5 * n], refs[5 * n:6 * n], refs[6 * n]
        x, y, c = _place()
        me = 2 * x + y
        for k in range(n):
            for px, py in _other_chips(x, y):
                pltpu.make_async_remote_copy(src_ref=srcs[k].at[2 * px + py], dst_ref=lands[k].at[me],
                                             send_sem=send_sems[k], recv_sem=recv_sems[k],
                                             device_id=(px, py, c), device_id_type=MESH).start()
        token[...] = jnp.zeros_like(token)

    hbm = lambda a: pltpu.with_memory_space_constraint(a, pltpu.HBM)
    res = pl.pallas_call(
        body, name=name,
        in_specs=[HBM] * (2 * n),
        out_specs=[SEM] * (2 * n) + [HBM] * (2 * n) + [pl.BlockSpec(memory_space=pltpu.VMEM)],
        out_shape=[pltpu.SemaphoreType.DMA(())] * (2 * n) + [pltpu.HBM(s.shape, s.dtype) for s in sbs] * 2
        + [jax.ShapeDtypeStruct((8, LANES), F32)],
        input_output_aliases={k: 2 * n + k for k in range(2 * n)},
        compiler_params=pltpu.CompilerParams(has_side_effects=SIDE_EFFECT),
    )(*[hbm(s) for s in sbs], *[hbm(lax.empty(s.shape, s.dtype)) for s in sbs])
    return res[:n], res[n:2 * n], res[2 * n:3 * n], res[3 * n:4 * n], res[4 * n]


def _send_wait(send_sems, recv_sems, srcs, lands, after, name):
    n = len(srcs)

    def body(*refs):
        ss, rs = refs[2 * n:3 * n], refs[3 * n:4 * n]
        s_out, l_out = refs[4 * n + 1:5 * n + 1], refs[5 * n + 1:]
        x, y, c = _place()
        for k in range(n):
            cp = pltpu.make_async_remote_copy(src_ref=s_out[k].at[pl.ds(0, N_CHIPS - 1)],
                                              dst_ref=l_out[k].at[pl.ds(0, N_CHIPS - 1)], send_sem=ss[k],
                                              recv_sem=rs[k], device_id=(x, y, c), device_id_type=MESH)
            cp.wait_send()
            cp.wait_recv()

    res = pl.pallas_call(
        body, name=name,
        in_specs=[HBM] * (2 * n) + [SEM] * (2 * n) + [ANY],
        out_specs=[HBM] * (2 * n),
        out_shape=[pltpu.HBM(s.shape, s.dtype) for s in srcs] * 2,
        input_output_aliases={k: k for k in range(2 * n)},
        compiler_params=pltpu.CompilerParams(has_side_effects=SIDE_EFFECT),
    )(*srcs, *lands, *send_sems, *recv_sems, after)
    return res[n:]


def _pass_to_sibling(bufs, name):
    n = len(bufs)
    halves = [b.shape[1] // 2 for b in bufs]

    def body(*refs):
        outs = refs[n:2 * n]
        send_sems, recv_sems = refs[2 * n:]
        x, y, c = _place()
        chips = _other_chips(x, y)

        def copy(k, j, core):
            px, py = chips[j]
            blk = outs[k].at[2 * px + py, _half_rows(core, halves[k])]
            return pltpu.make_async_remote_copy(src_ref=blk, dst_ref=blk, send_sem=send_sems.at[3 * k + j],
                                                recv_sem=recv_sems.at[3 * k + j], device_id=(x, y, 1 - c),
                                                device_id_type=MESH)

        sends = [copy(k, j, c) for k in range(n) for j in range(3)]
        for cp in sends:
            cp.start()
        for k in range(n):
            for j in range(3):
                copy(k, j, 1 - c).wait_recv()
        for cp in sends:
            cp.wait_send()

    return pl.pallas_call(
        body, name=name,
        in_specs=[ANY] * n, out_specs=[ANY] * n,
        out_shape=[jax.ShapeDtypeStruct(b.shape, b.dtype) for b in bufs],
        input_output_aliases={k: k for k in range(n)},
        scratch_shapes=[pltpu.SemaphoreType.DMA((3 * n,)), pltpu.SemaphoreType.DMA((3 * n,))],
    )(*bufs)


def _rope_tables(S):
    rows = S // GRID_W
    row_idx = jnp.repeat(jnp.arange(rows, dtype=F32), GRID_W)
    col_idx = jnp.tile(jnp.arange(GRID_W, dtype=F32), rows)
    pairs = ATT_HD // 4
    inv_freq = ROPE_THETA ** (-jnp.arange(pairs, dtype=F32) / pairs)
    ang = jnp.concatenate([row_idx[:, None] * inv_freq, col_idx[:, None] * inv_freq], axis=-1)
    cos, sin = jnp.cos(ang), jnp.sin(ang)
    return jnp.concatenate([cos, cos], axis=-1), jnp.concatenate([-sin, sin], axis=-1)


def _gate_rows(w, first_row):
    return jnp.zeros((LANES, GLA_KEY), F32).at[first_row:first_row + GLA_RANK].set(w.astype(F32))


def _local_step(x, target, weights_of, grads_out, grads_mid, P):
    S = x.shape[0]
    rc, rs = _rope_tables(S)
    row = lambda a: a.reshape(1, -1)
    saved = []
    for i in range(DEPTH):
        j = i // 2
        W = dict(weights_of(i, "mix", x))
        nm = row(P["norm_mix"][i])
        h1 = _rmsnorm_fwd(x, nm, f"norm_mix_fwd{i}")
        if i % 2 == 0:
            wgf = _gate_rows(W["gate_f"], 0)
            wgb = _gate_rows(W["gate_b"], GLA_RANK)
            bgf, bgb = row(P["gla_b_gate_f"][j]), row(P["gla_b_gate_b"][j])
            gn = row(P["gla_norm"][j])
            proj = _matmul_rows(h1, W["gla_in"], f"gla_in{i}")
            laf, lab = _gla_gate_fwd(proj, wgf, bgf, wgb, bgb, f"gla_gate_fwd{i}")
            of, stf = _gla_scan_fwd(proj, laf, False, f"gla_scan_f_fwd{i}")
            ob, stb = _gla_scan_fwd(proj, lab, True, f"gla_scan_b_fwd{i}")
            z = _gla_out_fwd(of, ob, proj, gn, f"gla_out_fwd{i}")
            xm = _matmul_rows(z, W["gla_out"], f"gla_outproj{i}", res=x)
            mix = dict(proj=proj, laf=laf, lab=lab, of=of, ob=ob, stf=stf, stb=stb, z=z, wgf=wgf, wgb=wgb)
        else:
            proj = _matmul_rows(h1, W["qkv"], f"attn_qkv{i}", w_layer=0)
            qn, kn = row(P["attn_q_norm"][j]), row(P["attn_k_norm"][j])
            qk, vb, kt, vt = _qk_prep_fwd(proj, qn, kn, rc, rs, f"qk_prep_fwd{i}")
            o, lse = _attn_fwd(qk, vt, f"attn_fwd{i}")
            xm = _matmul_rows(o, W["attn_out"], f"attn_outproj{i}", res=x)
            mix = dict(proj=proj, qk=qk, vb=vb, kt=kt, o=o, lse=lse)
        W.update(weights_of(i, "ffn", xm))
        h2 = _rmsnorm_fwd(xm, row(P["norm_ffn"][i]), f"norm_ffn_fwd{i}")
        a, uv, ug = _ffn_mid_fwd(h2, W["up_full"], W["conv"], row(P["ffn_b_conv"][i]), f"ffn_mid_fwd{i}")
        xo = _matmul_rows(a, W["down"], f"ffn_down{i}", res=xm)
        saved.append(dict(x=x, h1=h1, xm=xm, h2=h2, uv=uv, ug=ug, mix=mix, W=W))
        x = xo

    dx, dxb, loss = _loss_grad(x, target, "loss")

    G = {n: [None] * (DEPTH if n.startswith(("norm", "ffn")) else DEPTH // 2) for n in REPLICATED}
    token = None
    for i in reversed(range(DEPTH)):
        j = i // 2
        sv = saved[i]
        mix = sv["mix"]
        W = sv["W"]
        bconv = row(P["ffn_b_conv"][i])
        if token is not None:
            t = token[0:1, 0:1]
            bconv = jnp.where(t == 0.0, bconv, t)
        duv, dug, a, gwv, gwg = _ffn_mid_bwd(dxb, W["down"], sv["uv"], sv["ug"], W["conv"], bconv, f"ffn_mid_bwd{i}")
        L = dict(down=_wgrad(a, dxb, f"ffn_down_wgrad{i}", chips="rows"),
                 up=_wgrad(sv["h2"], (duv, dug), f"ffn_up_wgrad{i}", chips="cols"),
                 small=[jnp.concatenate([gwv[:3], gwg[:3]], axis=1)])
        G["ffn_b_conv"][i] = jnp.concatenate([gwv[3], gwg[3]], axis=0)
        nffn = row(P["norm_ffn"][i])
        if token is not None:
            t = grads_mid(i + 1, duv)[0:1, 0:1]
            nffn = jnp.where(t == 0.0, nffn, t)
        dxm, dxmb, dn = _dgrad_norm((duv, dug), W["up"], sv["xm"], nffn, dx, f"ffn_up_dgrad{i}", w_layer=0)
        G["norm_ffn"][i] = dn[0]
        if i % 2 == 0:
            proj = mix["proj"]
            bgf, bgb = row(P["gla_b_gate_f"][j]), row(P["gla_b_gate_b"][j])
            gn = row(P["gla_norm"][j])
            dz = _matmul_rows(dxmb, W["gla_out"], f"gla_outproj_dgrad{i}", transposed=True)
            L["out"] = _wgrad(mix["z"], dxmb, f"gla_outproj_wgrad{i}", chips="rows")
            do, dg, dgn = _gla_out_bwd(dz, mix["of"], mix["ob"], proj, gn, f"gla_out_bwd{i}")
            G["gla_norm"][j] = dgn[0]
            dqf, dkf, dvf, dlaf = _gla_scan_bwd(do, proj, mix["laf"], mix["stf"], False, f"gla_scan_f_bwd{i}")
            dqb, dkb, dvb, dlab = _gla_scan_bwd(do, proj, mix["lab"], mix["stb"], True, f"gla_scan_b_bwd{i}")
            dr, dwf, dbf, dwb, dbb = _gla_gate_bwd(dlaf, dlab, proj, mix["wgf"], bgf, mix["wgb"], bgb,
                                                   f"gla_gate_bwd{i}")
            L["small"] = [dwf[:GLA_RANK], dwb[GLA_RANK:2 * GLA_RANK]] + L["small"]
            G["gla_b_gate_f"][j] = dbf[0]
            G["gla_b_gate_b"][j] = dbb[0]
            dproj = jnp.concatenate([dqf + dqb, dkf + dkb, dvf + dvb, dg, dr], axis=1).astype(BF16)
            L["mix_in"] = _wgrad(sv["h1"], dproj, f"gla_in_wgrad{i}")
            dx, dxb, dn = _dgrad_norm(dproj, W["gla_in"], sv["x"], row(P["norm_mix"][i]), dxm, f"mix_in_dgrad{i}")
        else:
            proj = mix["proj"]
            qn, kn = row(P["attn_q_norm"][j]), row(P["attn_k_norm"][j])
            do = _matmul_rows(dxmb, W["attn_out"], f"attn_outproj_dgrad{i}", transposed=True)
            L["out"] = _wgrad(mix["o"], dxmb, f"attn_outproj_wgrad{i}", chips="rows")
            dq, dk, dv = _attn_bwd(do, mix["o"], mix["lse"], mix["qk"], mix["vb"], mix["kt"], f"attn_bwd{i}")
            dqk = jnp.concatenate([dq, dk], axis=1)
            dpqk, dqn, dkn = _qk_prep_bwd(dqk, proj, qn, kn, rc, rs, f"qk_prep_bwd{i}")
            G["attn_q_norm"][j] = dqn[0]
            G["attn_k_norm"][j] = dkn[0]
            dproj = jnp.concatenate([dpqk, dv], axis=1).astype(BF16)
            L["mix_in"] = _wgrad(sv["h1"], dproj, f"attn_qkv_wgrad{i}", chips="cols")
            dx, dxb, dn = _dgrad_norm(dproj, W["qkv"], sv["x"], row(P["norm_mix"][i]), dxm, f"mix_in_dgrad{i}",
                                      w_layer=0)
        G["norm_mix"][i] = dn[0]
        token = grads_out(i, L, G, loss)
    grads_mid(0, dx)
    return loss, dx, G


def kernel(x, norm_mix, norm_ffn, gla_w_in, gla_w_gate_up_f, gla_b_gate_f, gla_w_gate_up_b, gla_b_gate_b, gla_norm, gla_w_out, attn_w_qkv, attn_q_norm, attn_k_norm, attn_w_out, ffn_w_up, ffn_w_conv, ffn_b_conv, ffn_w_down, loss_target, m_norm_mix, m_norm_ffn, m_gla_w_in, m_gla_w_gate_up_f, m_gla_b_gate_f, m_gla_w_gate_up_b, m_gla_b_gate_b, m_gla_norm, m_gla_w_out, m_attn_w_qkv, m_attn_q_norm, m_attn_k_norm, m_attn_w_out, m_ffn_w_up, m_ffn_w_conv, m_ffn_b_conv, m_ffn_w_down, v_norm_mix, v_norm_ffn, v_gla_w_in, v_gla_w_gate_up_f, v_gla_b_gate_f, v_gla_w_gate_up_b, v_gla_b_gate_b, v_gla_norm, v_gla_w_out, v_attn_w_qkv, v_attn_q_norm, v_attn_k_norm, v_attn_w_out, v_ffn_w_up, v_ffn_w_conv, v_ffn_b_conv, v_ffn_w_down):
    names = ("norm_mix", "norm_ffn", "gla_w_in", "gla_w_gate_up_f", "gla_b_gate_f", "gla_w_gate_up_b",
             "gla_b_gate_b", "gla_norm", "gla_w_out", "attn_w_qkv", "attn_q_norm", "attn_k_norm", "attn_w_out",
             "ffn_w_up", "ffn_w_conv", "ffn_b_conv", "ffn_w_down")
    w = dict(zip(names, (norm_mix, norm_ffn, gla_w_in, gla_w_gate_up_f, gla_b_gate_f, gla_w_gate_up_b,
                         gla_b_gate_b, gla_norm, gla_w_out, attn_w_qkv, attn_q_norm, attn_k_norm, attn_w_out,
                         ffn_w_up, ffn_w_conv, ffn_b_conv, ffn_w_down)))
    m = dict(zip(names, (m_norm_mix, m_norm_ffn, m_gla_w_in, m_gla_w_gate_up_f, m_gla_b_gate_f,
                         m_gla_w_gate_up_b, m_gla_b_gate_b, m_gla_norm, m_gla_w_out, m_attn_w_qkv, m_attn_q_norm,
                         m_attn_k_norm, m_attn_w_out, m_ffn_w_up, m_ffn_w_conv, m_ffn_b_conv, m_ffn_w_down)))
    v = dict(zip(names, (v_norm_mix, v_norm_ffn, v_gla_w_in, v_gla_w_gate_up_f, v_gla_b_gate_f,
                         v_gla_w_gate_up_b, v_gla_b_gate_b, v_gla_norm, v_gla_w_out, v_attn_w_qkv, v_attn_q_norm,
                         v_attn_k_norm, v_attn_w_out, v_ffn_w_up, v_ffn_w_conv, v_ffn_b_conv, v_ffn_w_down)))
    px, py, pc = _place()
    me = 2 * px + py

    started, token = [], w["norm_mix"]
    for l in range(DEPTH):
        started.append(_gather_start(_layer_weight_bufs(w, l, me), token, f"gather_start{l}"))
        token = started[-1][3]
    fetched = {}

    def weights_of(l, part, after):
        send_sems, recv_sems, thru, _ = started[l]
        if l == 0:
            pick = slice(0, N_MIXER_BUFS) if part == "mix" else slice(N_MIXER_BUFS, None)
            landed = _gather_wait(send_sems[pick], recv_sems[pick], thru[pick], token if part == "mix" else after,
                                  f"gather_wait{l}_{part}")
            return _layer_weights(w, l, _pass_to_sibling(landed, f"gather_pass{l}_{part}"))
        if part == "mix":
            landed = _gather_wait(send_sems, recv_sems, thru, after, f"gather_wait{l}")
            fetched[l] = _layer_weights(w, l, _pass_to_sibling(landed, f"gather_pass{l}"))
        return fetched[l]

    sent = {}

    def grads_out(l, L, G, loss_part):
        mix_in = L["mix_in"]
        if l % 2 == 0:
            width = w["gla_w_in"].shape[2]
            mix_in = jnp.stack([mix_in[:, p * width:(p + 1) * width] for p in range(N_CHIPS)])
        cut = lambda t, p: lax.slice_in_dim(t, p * (t.shape[-1] // N_CHIPS), (p + 1) * (t.shape[-1] // N_CHIPS),
                                            axis=t.ndim - 1)
        small = jnp.stack([_pack([cut(t, p) for t in L["small"]], F32, 32) for p in range(N_CHIPS)])
        if l == 0:
            packed = _pack([jnp.stack(G[n]) for n in REPLICATED] + [loss_part], F32, 16)
            sent["small_sum"], tok = _allreduce_small(packed, "small_allreduce")
            small = jnp.where(tok[0:1, 0:1] == 0.0, small, tok[0:1, 0:1])
        bufs = [mix_in, L["out"], small, L["up"], L["down"]]
        *swap, tok = _swap_start(bufs, f"grads{l}_to_sibling")
        sent[l] = swap
        return tok

    def grads_mid(l, after):
        bufs, gots = _swap_wait(*sent[l], after, f"grads{l}_from_sibling")
        sums = [_add_sibling(b, g, pc, me, f"grads{l}_add_sibling{k}") for k, (b, g) in enumerate(zip(bufs, gots))]
        send_sems, recv_sems, srcs, lands, tok = _send_start([s[0] for s in sums], f"grads{l}_start")
        sent[l] = (send_sems, recv_sems, srcs, lands, [s[1] for s in sums])
        sent["last_token"] = tok
        return tok

    P = {n: w[n] for n in REPLICATED}

    loss_part, dx, grads = _local_step(x[0], loss_target[0], weights_of, grads_out, grads_mid, P)

    small_sum = sent["small_sum"]
    others_and_c = jnp.stack([jnp.where(me <= k, k + 1, k) for k in range(N_CHIPS - 1)] + [pc])
    mine, after = {}, sent["last_token"]
    for l in reversed(range(DEPTH)):
        send_sems, recv_sems, srcs, lands, own = sent[l]
        landed = _send_wait(send_sems, recv_sems, srcs, lands, after, f"grads{l}_wait")
        halves = [_add_chips(own[k], landed[k], others_and_c, f"grads{l}_add_chips{k}") for k in range(len(own))]
        mine[l] = _join_halves(halves, f"grads{l}_join_halves")
        after = mine[l][0]
    gsh = {}
    for n, k, layers in (("ffn_w_up", 3, range(DEPTH)), ("ffn_w_down", 4, range(DEPTH)),
                         ("gla_w_in", 0, range(0, DEPTH, 2)), ("gla_w_out", 1, range(0, DEPTH, 2)),
                         ("attn_w_qkv", 0, range(1, DEPTH, 2)), ("attn_w_out", 1, range(1, DEPTH, 2))):
        gsh[n] = jnp.stack([mine[l][k] for l in layers])
    small_mine = [_unpack(mine[l][2], [t.shape for t in _layer_small(w, l)]) for l in range(DEPTH)]
    gsh["ffn_w_conv"] = jnp.stack([small_mine[l][-1] for l in range(DEPTH)])
    gsh["gla_w_gate_up_f"] = jnp.stack([small_mine[l][0] for l in range(0, DEPTH, 2)])
    gsh["gla_w_gate_up_b"] = jnp.stack([small_mine[l][1] for l in range(0, DEPTH, 2)])

    parts = _unpack(small_sum, [w[n].shape for n in REPLICATED] + [(1, LANES)])
    gsh.update(dict(zip(REPLICATED, parts[:-1])))
    loss = parts[-1][0, 0]

    delta, new_m, new_v = {}, {}, {}
    for n in names:
        shp = w[n].shape
        two_d = (-1, shp[-1])
        d, nm, nv = _adamw(w[n].reshape(two_d), gsh[n].reshape(two_d), m[n].reshape(two_d), v[n].reshape(two_d),
                           f"adamw_{n}")
        delta[n], new_m[n], new_v[n] = d.reshape(shp), nm.reshape(shp), nv.reshape(shp)

    return (loss, dx[None], *[gsh[n] for n in names], *[delta[n] for n in names],
            *[new_m[n] for n in names], *[new_v[n] for n in names])
```

```python
import jax
import jax.numpy as jnp
from jax import lax
from jax.experimental import pallas as pl
from jax.experimental.pallas import tpu as pltpu

F32 = jnp.float32
BF16 = jnp.bfloat16
MESH = pl.DeviceIdType.MESH
HIGHEST = lax.Precision.HIGHEST

D_MODEL = 1024
DEPTH = 4
GRID_W = 64
NORM_EPS = 1e-6
GLA_HEADS = 4
GLA_DK = 128
GLA_DV = 256
GLA_KEY = GLA_HEADS * GLA_DK
GLA_VAL = GLA_HEADS * GLA_DV
GLA_RANK = 16
GLA_CHUNK = 64
GLA_GATE_NORMALIZER = 16.0
GLA_IN = 2 * GLA_KEY + 2 * GLA_VAL + 2 * GLA_RANK
GLA_IN_PAD = 3200
GLA_R_BLOCK = (2 * GLA_KEY + 2 * GLA_VAL) // 128
ATT_HD = 128
ATT_QH = 8
ATT_KVH = 2
ATT_GROUP = ATT_QH // ATT_KVH
ATT_QKV = (ATT_QH + 2 * ATT_KVH) * ATT_HD
ROPE_THETA = 10000.0
D_FF = 2816
ADAM_LR = 0.001
ADAM_B1 = 0.9
ADAM_B2 = 0.999
ADAM_EPS = 1e-08
ADAM_WD = 0.01
ADAM_STEP = 10

N_CHIPS = 4
LANES = 128
VMEM_LIMIT = 56 * 1024 * 1024


def _cp(sem):
    return pltpu.CompilerParams(dimension_semantics=sem, vmem_limit_bytes=VMEM_LIMIT)


def _pick(n, cands):
    for c in cands:
        if n % c == 0:
            return c
    return n


def _dg(a, b, ca, cb):
    return lax.dot_general(a, b, (((ca,), (cb,)), ((), ())), preferred_element_type=F32)


def _sigmoid(x):
    return 0.5 * jnp.tanh(0.5 * x) + 0.5


def _rmsnorm_fwd(x, w, name):
    S, D = x.shape
    tm = _pick(S, (512, 256))

    def body(x_ref, w_ref, h_ref):
        xv = x_ref[...]
        r = lax.rsqrt(jnp.mean(xv * xv, axis=-1, keepdims=True) + NORM_EPS)
        h_ref[...] = (xv * r * w_ref[...]).astype(BF16)

    return pl.pallas_call(
        body, name=name, grid=(S // tm,),
        in_specs=[pl.BlockSpec((tm, D), lambda i: (i, 0)), pl.BlockSpec((1, D), lambda i: (0, 0))],
        out_specs=pl.BlockSpec((tm, D), lambda i: (i, 0)),
        out_shape=jax.ShapeDtypeStruct((S, D), BF16),
        compiler_params=_cp(("parallel",)),
    )(x, w)


def _loss_grad(y, t, name):
    S, D = y.shape
    tm = _pick(S, (512, 256))

    def body(y_ref, t_ref, dy_ref, dyb_ref, loss_ref):
        i = pl.program_id(0)
        d = y_ref[...] - t_ref[...]
        dy = d * (1.0 / D)
        dy_ref[...] = dy
        dyb_ref[...] = dy.astype(BF16)
        sq = jnp.sum(jnp.sum(d * d, axis=1, keepdims=True), axis=0, keepdims=True)
        part = jnp.broadcast_to(sq * (0.5 / D), (1, LANES))

        @pl.when(i == 0)
        def _():
            loss_ref[...] = part

        @pl.when(i > 0)
        def _():
            loss_ref[...] += part

    return pl.pallas_call(
        body, name=name, grid=(S // tm,),
        in_specs=[pl.BlockSpec((tm, D), lambda i: (i, 0)), pl.BlockSpec((tm, D), lambda i: (i, 0))],
        out_specs=[pl.BlockSpec((tm, D), lambda i: (i, 0)), pl.BlockSpec((tm, D), lambda i: (i, 0)),
                   pl.BlockSpec((1, LANES), lambda i: (0, 0))],
        out_shape=[jax.ShapeDtypeStruct((S, D), F32), jax.ShapeDtypeStruct((S, D), BF16),
                   jax.ShapeDtypeStruct((1, LANES), F32)],
        compiler_params=_cp(("arbitrary",)),
    )(y, t)


WGRAD_TK = 512


def _wgrad(a, b, name, chips=None):
    S, Kw = a.shape
    pair = isinstance(b, (tuple, list))
    tn = b[0].shape[1] if pair else b.shape[1]
    N = 2 * tn if pair else tn
    tk = _pick(S, (WGRAD_TK, 256, 128))
    nk = S // tk
    if pair:
        b_specs = [pl.BlockSpec((tk, tn), lambda j, k: (jnp.where(j == 0, k, nk - 1), 0)),
                   pl.BlockSpec((tk, tn), lambda j, k: (jnp.where(j == 1, k, 0), 0))]
    else:
        b_specs = [pl.BlockSpec((tk, tn), lambda j, k: (k, 0))]
    if chips == "cols":
        cw = N // N_CHIPS
        span = tn // cw
        o_spec = pl.BlockSpec((span, Kw, cw), lambda j, k: (j, 0, 0))
        out_shape = jax.ShapeDtypeStruct((N_CHIPS, Kw, cw), F32)
    elif chips == "rows":
        assert not pair
        o_spec = pl.BlockSpec((N_CHIPS, Kw // N_CHIPS, N), lambda j, k: (0, 0, 0))
        out_shape = jax.ShapeDtypeStruct((N_CHIPS, Kw // N_CHIPS, N), F32)
    else:
        assert not pair
        o_spec = pl.BlockSpec((Kw, N), lambda j, k: (0, 0))
        out_shape = jax.ShapeDtypeStruct((Kw, N), F32)
    nb = len(b_specs)

    def body(*refs):
        a_ref, b_refs, o_ref, acc = refs[0], refs[1:1 + nb], refs[-2], refs[-1]
        j = pl.program_id(0)
        k = pl.program_id(1)

        @pl.when(k == 0)
        def _():
            acc[...] = jnp.zeros_like(acc)

        av = a_ref[...].astype(BF16)
        for h in range(nb):
            @pl.when(j == h)
            def _():
                acc[...] += _dg(av, b_refs[h][...].astype(BF16), 0, 0)

        @pl.when(k == nk - 1)
        def _():
            v = acc[...]
            if chips == "cols":
                for s in range(span):
                    o_ref[s] = v[:, s * cw:(s + 1) * cw]
            elif chips == "rows":
                rows = Kw // N_CHIPS
                for p in range(N_CHIPS):
                    o_ref[p] = v[p * rows:(p + 1) * rows, :]
            else:
                o_ref[...] = v

    return pl.pallas_call(
        body, name=name, grid=(nb, nk),
        in_specs=[pl.BlockSpec((tk, Kw), lambda j, k: (k, 0))] + b_specs,
        out_specs=o_spec, out_shape=out_shape,
        scratch_shapes=[pltpu.VMEM((Kw, tn), F32)],
        compiler_params=_cp(("parallel", "arbitrary")),
    )(a, *(tuple(b) if pair else (b,)))


def _matmul_rows(a, w, name, res=None, w_layer=None, transposed=False):
    M, K = a.shape
    if w_layer is not None:
        cw = w.shape[2]
        N = N_CHIPS * cw
        w_spec = pl.BlockSpec((N_CHIPS, K, cw), lambda i: (0, w_layer, 0))
    else:
        N = w.shape[0] if transposed else w.shape[1]
        assert w.shape[1 if transposed else 0] == K
        w_spec = pl.BlockSpec(w.shape, lambda i: (0, 0))
    tm = _pick(M, (512, 256, 128))
    has_res = res is not None

    def body(*refs):
        a_ref, w_ref = refs[0], refs[1]
        r_ref = refs[2] if has_res else None
        o_ref = refs[-1]
        av = a_ref[...].astype(BF16)
        if w_layer is not None:
            for p in range(N_CHIPS):
                o_ref[:, pl.ds(p * cw, cw)] = jnp.dot(av, w_ref[p], preferred_element_type=F32)
        else:
            v = _dg(av, w_ref[...], 1, 1 if transposed else 0)
            o_ref[...] = v + r_ref[...] if has_res else v

    row = pl.BlockSpec((tm, N), lambda i: (i, 0))
    return pl.pallas_call(
        body, name=name, grid=(M // tm,),
        in_specs=[pl.BlockSpec((tm, K), lambda i: (i, 0)), w_spec] + ([row] if has_res else []),
        out_specs=row, out_shape=jax.ShapeDtypeStruct((M, N), F32),
        compiler_params=_cp(("parallel",)),
    )(*((a, w) + ((res,) if has_res else ())))


def _dgrad_norm(dy, w, x, wn, dres, name, w_layer=None):
    pair = isinstance(dy, (tuple, list))
    M = dy[0].shape[0] if pair else dy.shape[0]
    Kp = 2 * dy[0].shape[1] if pair else dy.shape[1]
    D = x.shape[1]
    if w_layer is not None:
        cw = w.shape[2]
        assert N_CHIPS * cw == Kp and w.shape[1] % D == 0
        w_spec = pl.BlockSpec((N_CHIPS, D, cw), lambda i: (0, w_layer, 0))
    else:
        assert w.shape == (D, Kp)
        w_spec = pl.BlockSpec((D, Kp), lambda i: (0, 0))
    tm = _pick(M, (256, 128))
    width = Kp // 2 if pair else Kp
    dy_specs = [pl.BlockSpec((tm, width), lambda i: (i, 0))] * (2 if pair else 1)
    nd = len(dy_specs)

    def body(*refs):
        dy_refs = refs[:nd]
        w_ref, x_ref, wn_ref, dres_ref, dx_ref, dxb_ref, dwn_ref = refs[nd:]
        i = pl.program_id(0)
        if w_layer is not None:
            dh = None
            for p in range(N_CHIPS):
                src, off = divmod(p * cw, width)
                part = _dg(dy_refs[src][:, pl.ds(off, cw)], w_ref[p], 1, 1)
                dh = part if dh is None else dh + part
        else:
            dh = _dg(dy_refs[0][...], w_ref[...], 1, 1)
        xv = x_ref[...]
        r = lax.rsqrt(jnp.mean(xv * xv, axis=-1, keepdims=True) + NORM_EPS)
        yv = xv * r
        dyv = dh * wn_ref[...]
        dxv = r * (dyv - yv * jnp.mean(dyv * yv, axis=-1, keepdims=True)) + dres_ref[...]
        dx_ref[...] = dxv
        dxb_ref[...] = dxv.astype(BF16)
        part = jnp.sum(dh * yv, axis=0, keepdims=True)

        @pl.when(i == 0)
        def _():
            dwn_ref[...] = part

        @pl.when(i > 0)
        def _():
            dwn_ref[...] += part

    row = pl.BlockSpec((tm, D), lambda i: (i, 0))
    one = pl.BlockSpec((1, D), lambda i: (0, 0))
    return pl.pallas_call(
        body, name=name, grid=(M // tm,),
        in_specs=dy_specs + [w_spec, row, one, row],
        out_specs=[row, row, one],
        out_shape=[jax.ShapeDtypeStruct((M, D), F32), jax.ShapeDtypeStruct((M, D), BF16),
                   jax.ShapeDtypeStruct((1, D), F32)],
        compiler_params=_cp(("arbitrary",)),
    )(*(tuple(dy) if pair else (dy,)), w, x, wn, dres)


FFN_TN_FWD = 256
FFN_TN_BWD = 128
FFN_ROWS = 256
PAD = 8


def _conv3(pad_ref, w, r0, tr):
    um = pad_ref[pl.ds(PAD - 1 + r0, tr), :]
    uc = pad_ref[pl.ds(PAD + r0, tr), :]
    up = pad_ref[pl.ds(PAD + 1 + r0, tr), :]
    return w[0:1, :] * um + w[1:2, :] * uc + w[2:3, :] * up, (um, uc, up)


def _zero_pads(pad_ref, S, tn):
    pad_ref[pl.ds(0, PAD), :] = jnp.zeros((PAD, tn), F32)
    pad_ref[pl.ds(PAD + S, PAD), :] = jnp.zeros((PAD, tn), F32)


def _ffn_mid_fwd(h, wup, wconv, bconv, name):
    S, D = h.shape
    F = wup.shape[1] // 2
    tn = FFN_TN_FWD
    nb = F // tn
    tr = min(FFN_ROWS, S)

    def body(h_ref, wv_ref, wg_ref, cv_ref, cg_ref, bv_ref, bg_ref, a_ref, uv_ref, ug_ref):
        _zero_pads(uv_ref, S, tn)
        _zero_pads(ug_ref, S, tn)
        hv = h_ref[...]
        uv_ref[pl.ds(PAD, S), :] = jnp.dot(hv, wv_ref[...], preferred_element_type=F32)
        ug_ref[pl.ds(PAD, S), :] = jnp.dot(hv, wg_ref[...], preferred_element_type=F32)
        cwv, cwg, bv, bg = cv_ref[...], cg_ref[...], bv_ref[...], bg_ref[...]
        for r0 in range(0, S, tr):
            cv = _conv3(uv_ref, cwv, r0, tr)[0] + bv
            cg = _conv3(ug_ref, cwg, r0, tr)[0] + bg
            a_ref[pl.ds(r0, tr), :] = (cg * _sigmoid(cg) * cv).astype(BF16)

    col = lambda off: (lambda j: (0, j + off))
    padded = pl.BlockSpec((S + 2 * PAD, tn), col(0))
    return pl.pallas_call(
        body, name=name, grid=(nb,),
        in_specs=[pl.BlockSpec((S, D), lambda j: (0, 0)),
                  pl.BlockSpec((D, tn), col(0)), pl.BlockSpec((D, tn), col(nb)),
                  pl.BlockSpec((3, tn), col(0)), pl.BlockSpec((3, tn), col(nb)),
                  pl.BlockSpec((1, tn), col(0)), pl.BlockSpec((1, tn), col(nb))],
        out_specs=[pl.BlockSpec((S, tn), col(0)), padded, padded],
        out_shape=[jax.ShapeDtypeStruct((S, F), BF16), jax.ShapeDtypeStruct((S + 2 * PAD, F), F32),
                   jax.ShapeDtypeStruct((S + 2 * PAD, F), F32)],
        compiler_params=_cp(("parallel",)),
    )(h, wup, wup, wconv, wconv, bconv, bconv)


def _rows8(rows):
    n = rows[0].shape[1]
    idx = lax.broadcasted_iota(jnp.int32, (8, n), 0)
    out = jnp.zeros((8, n), F32)
    for k, r in enumerate(rows):
        out = jnp.where(idx == k, r, out)
    return out


def _ffn_mid_bwd(dyb, wdown, uv, ug, wconv, bconv, name):
    S, D = dyb.shape
    F = wdown.shape[0]
    tn = FFN_TN_BWD
    nb = F // tn
    tr = min(FFN_ROWS, S)

    def body(dy_ref, wd_ref, uv_ref, ug_ref, cv_ref, cg_ref, bv_ref, bg_ref,
             duv_ref, dug_ref, a_ref, gwv_ref, gwg_ref, pdv, pdg):
        for p in (pdv, pdg):
            _zero_pads(p, S, tn)
        wd = wd_ref[...]
        cwv, cwg, bv, bg = cv_ref[...], cg_ref[...], bv_ref[...], bg_ref[...]
        zero = jnp.zeros((1, tn), F32)
        gv = [zero, zero, zero, zero]
        gg = [zero, zero, zero, zero]
        for r0 in range(0, S, tr):
            cv, shv = _conv3(uv_ref, cwv, r0, tr)
            cg, shg = _conv3(ug_ref, cwg, r0, tr)
            cv = cv + bv
            cg = cg + bg
            sg = _sigmoid(cg)
            sl = cg * sg
            a_ref[pl.ds(r0, tr), :] = (sl * cv).astype(BF16)
            da = _dg(dy_ref[pl.ds(r0, tr), :], wd, 1, 1)
            dcv = da * sl
            dcg = da * cv * (sg * (1.0 + cg * (1.0 - sg)))
            pdv[pl.ds(PAD + r0, tr), :] = dcv
            pdg[pl.ds(PAD + r0, tr), :] = dcg
            for k in range(3):
                gv[k] = gv[k] + jnp.sum(dcv * shv[k], axis=0, keepdims=True)
                gg[k] = gg[k] + jnp.sum(dcg * shg[k], axis=0, keepdims=True)
            gv[3] = gv[3] + jnp.sum(dcv, axis=0, keepdims=True)
            gg[3] = gg[3] + jnp.sum(dcg, axis=0, keepdims=True)
        gwv_ref[...] = _rows8(gv)
        gwg_ref[...] = _rows8(gg)
        for r0 in range(0, S, tr):
            for pd, cw, out in ((pdv, cwv, duv_ref), (pdg, cwg, dug_ref)):
                dm = pd[pl.ds(PAD - 1 + r0, tr), :]
                dc = pd[pl.ds(PAD + r0, tr), :]
                dp = pd[pl.ds(PAD + 1 + r0, tr), :]
                out[pl.ds(r0, tr), :] = (cw[0:1, :] * dp + cw[1:2, :] * dc + cw[2:3, :] * dm).astype(BF16)

    col = lambda off: (lambda j: (0, j + off))
    blk = pl.BlockSpec((S, tn), col(0))
    padded = pl.BlockSpec((S + 2 * PAD, tn), col(0))
    g8 = pl.BlockSpec((8, tn), col(0))
    return pl.pallas_call(
        body, name=name, grid=(nb,),
        in_specs=[pl.BlockSpec((S, D), lambda j: (0, 0)), pl.BlockSpec((tn, D), lambda j: (j, 0)), padded, padded,
                  pl.BlockSpec((3, tn), col(0)), pl.BlockSpec((3, tn), col(nb)),
                  pl.BlockSpec((1, tn), col(0)), pl.BlockSpec((1, tn), col(nb))],
        out_specs=[blk, blk, blk, g8, g8],
        out_shape=[jax.ShapeDtypeStruct((S, F), BF16), jax.ShapeDtypeStruct((S, F), BF16),
                   jax.ShapeDtypeStruct((S, F), BF16), jax.ShapeDtypeStruct((8, F), F32),
                   jax.ShapeDtypeStruct((8, F), F32)],
        scratch_shapes=[pltpu.VMEM((S + 2 * PAD, tn), F32)] * 2,
        compiler_params=_cp(("parallel",)),
    )(dyb, wdown, uv, ug, wconv, wconv, bconv, bconv)


def _log_sigmoid(x):
    return jnp.minimum(x, 0.0) - jnp.log(1.0 + jnp.exp(-jnp.abs(x)))


def _gla_gate_fwd(proj, wgf, bgf, wgb, bgb, name):
    S = proj.shape[0]
    tm = _pick(S, (512, 256))

    def body(r_ref, wf_ref, bf_ref, wb_ref, bb_ref, laf_ref, lab_ref):
        r = r_ref[...].astype(BF16)
        lf = jnp.dot(r, wf_ref[...].astype(BF16), preferred_element_type=F32) + bf_ref[...]
        lb = jnp.dot(r, wb_ref[...].astype(BF16), preferred_element_type=F32) + bb_ref[...]
        laf_ref[...] = _log_sigmoid(lf) * (1.0 / GLA_GATE_NORMALIZER)
        lab_ref[...] = _log_sigmoid(lb) * (1.0 / GLA_GATE_NORMALIZER)

    full = lambda shp: pl.BlockSpec(shp, lambda i: (0, 0))
    row = pl.BlockSpec((tm, GLA_KEY), lambda i: (i, 0))
    return pl.pallas_call(
        body, name=name, grid=(S // tm,),
        in_specs=[pl.BlockSpec((tm, LANES), lambda i: (i, GLA_R_BLOCK)),
                  full((LANES, GLA_KEY)), full((1, GLA_KEY)), full((LANES, GLA_KEY)), full((1, GLA_KEY))],
        out_specs=[row, row],
        out_shape=[jax.ShapeDtypeStruct((S, GLA_KEY), F32)] * 2,
        compiler_params=_cp(("parallel",)),
    )(proj, wgf, bgf, wgb, bgb)


def _gla_gate_bwd(dlaf, dlab, proj, wgf, bgf, wgb, bgb, name):
    S = proj.shape[0]
    tm = _pick(S, (512, 256))

    def body(dlf_ref, dlb_ref, r_ref, wf_ref, bf_ref, wb_ref, bb_ref, dr_ref, dwf_ref, dbf_ref, dwb_ref, dbb_ref):
        i = pl.program_id(0)
        r = r_ref[...].astype(BF16)
        wf = wf_ref[...].astype(BF16)
        wb = wb_ref[...].astype(BF16)
        lf = jnp.dot(r, wf, preferred_element_type=F32) + bf_ref[...]
        lb = jnp.dot(r, wb, preferred_element_type=F32) + bb_ref[...]
        glf = dlf_ref[...] * (1.0 / GLA_GATE_NORMALIZER) * (1.0 / (1.0 + jnp.exp(lf)))
        glb = dlb_ref[...] * (1.0 / GLA_GATE_NORMALIZER) * (1.0 / (1.0 + jnp.exp(lb)))
        gfb = glf.astype(BF16)
        gbb = glb.astype(BF16)
        dr_ref[...] = _dg(gfb, wf, 1, 1) + _dg(gbb, wb, 1, 1)
        parts = (_dg(r, gfb, 0, 0), jnp.sum(glf, axis=0, keepdims=True),
                 _dg(r, gbb, 0, 0), jnp.sum(glb, axis=0, keepdims=True))
        outs = (dwf_ref, dbf_ref, dwb_ref, dbb_ref)

        @pl.when(i == 0)
        def _():
            for o, p in zip(outs, parts):
                o[...] = p

        @pl.when(i > 0)
        def _():
            for o, p in zip(outs, parts):
                o[...] += p

    full = lambda shp: pl.BlockSpec(shp, lambda i: (0, 0))
    row = pl.BlockSpec((tm, GLA_KEY), lambda i: (i, 0))
    return pl.pallas_call(
        body, name=name, grid=(S // tm,),
        in_specs=[row, row, pl.BlockSpec((tm, LANES), lambda i: (i, GLA_R_BLOCK)),
                  full((LANES, GLA_KEY)), full((1, GLA_KEY)), full((LANES, GLA_KEY)), full((1, GLA_KEY))],
        out_specs=[pl.BlockSpec((tm, LANES), lambda i: (i, 0)),
                   full((LANES, GLA_KEY)), full((1, GLA_KEY)), full((LANES, GLA_KEY)), full((1, GLA_KEY))],
        out_shape=[jax.ShapeDtypeStruct((S, LANES), F32),
                   jax.ShapeDtypeStruct((LANES, GLA_KEY), F32), jax.ShapeDtypeStruct((1, GLA_KEY), F32),
                   jax.ShapeDtypeStruct((LANES, GLA_KEY), F32), jax.ShapeDtypeStruct((1, GLA_KEY), F32)],
        compiler_params=_cp(("arbitrary",)),
    )(dlaf, dlab, proj, wgf, bgf, wgb, bgb)


def _gla_masks(rev):
    C = GLA_CHUNK
    t = lax.broadcasted_iota(jnp.int32, (C, C), 0)
    s = lax.broadcasted_iota(jnp.int32, (C, C), 1)
    if rev:
        return (s >= t), (s > t), (t >= s), (t > s)
    return (s <= t), (s <= t), (t <= s), (t <= s)


def _cum_dot(cum, x):
    return jnp.dot(cum.astype(F32), x, precision=HIGHEST, preferred_element_type=F32)


def _gla_chunk_common(q, k, la, cum, end_row):
    b = _cum_dot(cum, la)
    bend = b[end_row:end_row + 1, :]
    e = jnp.exp(b)
    qd = q * (GLA_DK ** -0.5) * e
    ei = jnp.exp(-b)
    ee = jnp.exp(bend - b)
    d = jnp.exp(bend)
    return e, ei, ee, d, qd, k * ei, k * ee


GLA_CB = 16


def _gla_specs(S, rev_order):
    n = S // GLA_CHUNK
    cb = min(GLA_CB, n)
    nblk = n // cb
    rows = cb * GLA_CHUNK
    ci = (lambda i: nblk - 1 - i) if rev_order else (lambda i: i)
    q_spec = pl.BlockSpec((rows, GLA_DK), lambda h, i: (ci(i), h))
    k_spec = pl.BlockSpec((rows, GLA_DK), lambda h, i: (ci(i), GLA_HEADS + h))
    v_spec = pl.BlockSpec((rows, GLA_DV), lambda h, i: (ci(i), GLA_KEY * 2 // GLA_DV + h))
    la_spec = pl.BlockSpec((rows, GLA_DK), lambda h, i: (ci(i), h))
    o_spec = pl.BlockSpec((rows, GLA_DV), lambda h, i: (ci(i), h))
    st_spec = pl.BlockSpec((1, cb, GLA_DV, GLA_DK), lambda h, i: (h, ci(i), 0, 0))
    return n, cb, nblk, q_spec, k_spec, v_spec, la_spec, o_spec, st_spec


def _gla_scan_fwd(proj, la, rev, name):
    S = proj.shape[0]
    C = GLA_CHUNK
    n, cb, nblk, q_spec, k_spec, v_spec, la_spec, o_spec, st_spec = _gla_specs(S, rev)
    end_row = 0 if rev else C - 1
    order = list(range(cb))[::-1] if rev else list(range(cb))

    def body(q_ref, k_ref, v_ref, la_ref, o_ref, st_ref, state):
        i = pl.program_id(1)

        @pl.when(i == 0)
        def _():
            state[...] = jnp.zeros_like(state)

        cum, mask, _, _ = _gla_masks(rev)
        pre, intra, kv = {}, {}, {}
        for cc in order:
            rows = pl.ds(cc * C, C)
            q, k, v, lav = q_ref[rows, :], k_ref[rows, :], v_ref[rows, :], la_ref[rows, :]
            _, _, _, d, qd, ki, ke = _gla_chunk_common(q, k, lav, cum, end_row)
            qdb, kib, keb, vb = qd.astype(BF16), ki.astype(BF16), ke.astype(BF16), v.astype(BF16)
            pre[cc] = (d, qdb)
            att = jnp.where(mask, _dg(qdb, kib, 1, 1), 0.0)
            intra[cc] = jnp.dot(att.astype(BF16), vb, preferred_element_type=F32)
            kv[cc] = _dg(vb, keb, 0, 0)
        st = state[...]
        for cc in order:
            d, qdb = pre[cc]
            o_ref[pl.ds(cc * C, C), :] = intra[cc] + _dg(qdb, st.astype(BF16), 1, 1)
            st_ref[0, cc] = st
            st = st * d + kv[cc]
        state[...] = st

    return pl.pallas_call(
        body, name=name, grid=(GLA_HEADS, nblk),
        in_specs=[q_spec, k_spec, v_spec, la_spec],
        out_specs=[o_spec, st_spec],
        out_shape=[jax.ShapeDtypeStruct((S, GLA_VAL), F32),
                   jax.ShapeDtypeStruct((GLA_HEADS, n, GLA_DV, GLA_DK), F32)],
        scratch_shapes=[pltpu.VMEM((GLA_DV, GLA_DK), F32)],
        compiler_params=_cp(("parallel", "arbitrary")),
    )(proj, proj, proj, la)


def _gla_scan_bwd(do, proj, la, states, rev, name):
    S = proj.shape[0]
    C = GLA_CHUNK
    n, cb, nblk, q_spec, k_spec, v_spec, la_spec, o_spec, st_spec = _gla_specs(S, not rev)
    end_row = 0 if rev else C - 1
    order = list(range(cb)) if rev else list(range(cb))[::-1]

    def body(do_ref, q_ref, k_ref, v_ref, la_ref, st_ref, dq_ref, dk_ref, dv_ref, dla_ref, gstate):
        i = pl.program_id(1)

        @pl.when(i == 0)
        def _():
            gstate[...] = jnp.zeros_like(gstate)

        cum, mask, cum_t, mask_t = _gla_masks(rev)
        g = gstate[...]
        for cc in order:
            rows = pl.ds(cc * C, C)
            q, k, v, lav = q_ref[rows, :], k_ref[rows, :], v_ref[rows, :], la_ref[rows, :]
            dov = do_ref[rows, :]
            st = st_ref[0, cc]
            e, ei, ee, d, qd, ki, ke = _gla_chunk_common(q, k, lav, cum, end_row)
            qdb, kib, keb, vb = qd.astype(BF16), ki.astype(BF16), ke.astype(BF16), v.astype(BF16)
            dob, gb, stb = dov.astype(BF16), g.astype(BF16), st.astype(BF16)
            att_t = jnp.where(mask_t, _dg(kib, qdb, 1, 1), 0.0)
            da = jnp.where(mask, _dg(dob, vb, 1, 1), 0.0)
            da_t = jnp.where(mask_t, _dg(vb, dob, 1, 1), 0.0)
            dv_ref[rows, :] = jnp.dot(att_t.astype(BF16), dob, preferred_element_type=F32) + _dg(keb, gb, 1, 1)
            dqd = (jnp.dot(da.astype(BF16), kib, preferred_element_type=F32)
                   + jnp.dot(dob, stb, preferred_element_type=F32))
            dki = jnp.dot(da_t.astype(BF16), qdb, preferred_element_type=F32)
            dke = jnp.dot(vb, gb, preferred_element_type=F32)
            dd = jnp.sum(st * g, axis=0, keepdims=True)
            g = g * d + _dg(dob, qdb, 0, 0)
            dq_ref[rows, :] = dqd * e * (GLA_DK ** -0.5)
            dk_ref[rows, :] = dki * ei + dke * ee
            dkeke = dke * ke
            db = dqd * qd - dki * ki - dkeke
            dbend = jnp.sum(dkeke, axis=0, keepdims=True) + dd * d
            dla_ref[rows, :] = _cum_dot(cum_t, db) + dbend
        gstate[...] = g

    key_out = la_spec
    return pl.pallas_call(
        body, name=name, grid=(GLA_HEADS, nblk),
        in_specs=[o_spec, q_spec, k_spec, v_spec, la_spec, st_spec],
        out_specs=[key_out, key_out, o_spec, key_out],
        out_shape=[jax.ShapeDtypeStruct((S, GLA_KEY), F32), jax.ShapeDtypeStruct((S, GLA_KEY), F32),
                   jax.ShapeDtypeStruct((S, GLA_VAL), F32), jax.ShapeDtypeStruct((S, GLA_KEY), F32)],
        scratch_shapes=[pltpu.VMEM((GLA_DV, GLA_DK), F32)],
        compiler_params=_cp(("parallel", "arbitrary")),
    )(do, proj, proj, proj, la, states)


def _gla_out_fwd(of, ob, proj, gn, name):
    S = of.shape[0]
    tm = _pick(S, (512, 256, 128))
    gblk = (2 * GLA_KEY + GLA_VAL) // GLA_VAL

    def body(of_ref, ob_ref, g_ref, gn_ref, z_ref):
        gnv = gn_ref[...]
        for h in range(GLA_HEADS):
            cols = pl.ds(h * GLA_DV, GLA_DV)
            o = of_ref[:, cols] + ob_ref[:, cols]
            r = lax.rsqrt(jnp.mean(o * o, axis=-1, keepdims=True) + NORM_EPS)
            gv = g_ref[:, cols]
            z_ref[:, cols] = (o * r * gnv * (gv * _sigmoid(gv))).astype(BF16)

    row = pl.BlockSpec((tm, GLA_VAL), lambda i: (i, 0))
    return pl.pallas_call(
        body, name=name, grid=(S // tm,),
        in_specs=[row, row, pl.BlockSpec((tm, GLA_VAL), lambda i: (i, gblk)),
                  pl.BlockSpec((1, GLA_DV), lambda i: (0, 0))],
        out_specs=row,
        out_shape=jax.ShapeDtypeStruct((S, GLA_VAL), BF16),
        compiler_params=_cp(("parallel",)),
    )(of, ob, proj, gn)


def _gla_out_bwd(dz, of, ob, proj, gn, name):
    S = of.shape[0]
    tm = _pick(S, (512, 256, 128))
    gblk = (2 * GLA_KEY + GLA_VAL) // GLA_VAL

    def body(dz_ref, of_ref, ob_ref, g_ref, gn_ref, do_ref, dg_ref, dgn_ref):
        i = pl.program_id(0)
        gnv = gn_ref[...]
        part = jnp.zeros((1, GLA_DV), F32)
        for h in range(GLA_HEADS):
            cols = pl.ds(h * GLA_DV, GLA_DV)
            o = of_ref[:, cols] + ob_ref[:, cols]
            r = lax.rsqrt(jnp.mean(o * o, axis=-1, keepdims=True) + NORM_EPS)
            y = o * r
            gv = g_ref[:, cols]
            sg = _sigmoid(gv)
            dzv = dz_ref[:, cols]
            dg_ref[:, cols] = dzv * (y * gnv) * (sg * (1.0 + gv * (1.0 - sg)))
            don = dzv * (gv * sg)
            part = part + jnp.sum(don * y, axis=0, keepdims=True)
            dy = don * gnv
            do_ref[:, cols] = r * (dy - y * jnp.mean(dy * y, axis=-1, keepdims=True))

        @pl.when(i == 0)
        def _():
            dgn_ref[...] = part

        @pl.when(i > 0)
        def _():
            dgn_ref[...] += part

    row = pl.BlockSpec((tm, GLA_VAL), lambda i: (i, 0))
    one = pl.BlockSpec((1, GLA_DV), lambda i: (0, 0))
    return pl.pallas_call(
        body, name=name, grid=(S // tm,),
        in_specs=[row, row, row, pl.BlockSpec((tm, GLA_VAL), lambda i: (i, gblk)), one],
        out_specs=[row, row, one],
        out_shape=[jax.ShapeDtypeStruct((S, GLA_VAL), F32), jax.ShapeDtypeStruct((S, GLA_VAL), F32),
                   jax.ShapeDtypeStruct((1, GLA_DV), F32)],
        compiler_params=_cp(("arbitrary",)),
    )(dz, of, ob, proj, gn)


N_QK_HEADS = ATT_QH + ATT_KVH


def _qk_prep_fwd(proj, qn, kn, rc, rs, name):
    S = proj.shape[0]
    tm = _pick(S, (512, 256, 128))
    W = N_QK_HEADS * ATT_HD
    scale = ATT_HD ** -0.5

    def body(p_ref, qn_ref, kn_ref, rc_ref, rs_ref, v_in_ref, qk_ref, v_ref, kt_ref, vt_ref):
        c, s = rc_ref[...], rs_ref[...]
        for h in range(N_QK_HEADS):
            cols = pl.ds(h * ATT_HD, ATT_HD)
            w = qn_ref[...] if h < ATT_QH else kn_ref[...]
            xv = p_ref[:, cols]
            r = lax.rsqrt(jnp.mean(xv * xv, axis=-1, keepdims=True) + NORM_EPS)
            y = xv * r * w
            out = y * c + pltpu.roll(y, ATT_HD // 2, 1) * s
            if h < ATT_QH:
                qk_ref[:, cols] = (out * scale).astype(BF16)
            else:
                qk_ref[:, cols] = out.astype(BF16)
                kt_ref[pl.ds((h - ATT_QH) * ATT_HD, ATT_HD), :] = out.T.astype(BF16)
        v_ref[...] = v_in_ref[...].astype(BF16)
        for h in range(ATT_KVH):
            vt_ref[pl.ds(h * ATT_HD, ATT_HD), :] = v_in_ref[:, pl.ds(h * ATT_HD, ATT_HD)].T.astype(BF16)

    one = pl.BlockSpec((1, ATT_HD), lambda i: (0, 0))
    tab = pl.BlockSpec((tm, ATT_HD), lambda i: (i, 0))
    vw = ATT_KVH * ATT_HD
    tr = pl.BlockSpec((vw, tm), lambda i: (0, i))
    return pl.pallas_call(
        body, name=name, grid=(S // tm,),
        in_specs=[pl.BlockSpec((tm, W), lambda i: (i, 0)), one, one, tab, tab,
                  pl.BlockSpec((tm, vw), lambda i: (i, W // vw))],
        out_specs=[pl.BlockSpec((tm, W), lambda i: (i, 0)), pl.BlockSpec((tm, vw), lambda i: (i, 0)), tr, tr],
        out_shape=[jax.ShapeDtypeStruct((S, W), BF16), jax.ShapeDtypeStruct((S, vw), BF16),
                   jax.ShapeDtypeStruct((vw, S), BF16), jax.ShapeDtypeStruct((vw, S), BF16)],
        compiler_params=_cp(("parallel",)),
    )(proj, qn, kn, rc, rs, proj)


def _qk_prep_bwd(dqk, proj, qn, kn, rc, rs, name):
    S = proj.shape[0]
    tm = _pick(S, (512, 256, 128))
    W = N_QK_HEADS * ATT_HD

    def body(d_ref, p_ref, qn_ref, kn_ref, rc_ref, rs_ref, dp_ref, dqn_ref, dkn_ref):
        i = pl.program_id(0)
        c, s = rc_ref[...], rs_ref[...]
        parts = [jnp.zeros((1, ATT_HD), F32), jnp.zeros((1, ATT_HD), F32)]
        for h in range(N_QK_HEADS):
            cols = pl.ds(h * ATT_HD, ATT_HD)
            w = qn_ref[...] if h < ATT_QH else kn_ref[...]
            dout = d_ref[:, cols]
            dy = dout * c + pltpu.roll(dout * s, ATT_HD // 2, 1)
            xv = p_ref[:, cols]
            r = lax.rsqrt(jnp.mean(xv * xv, axis=-1, keepdims=True) + NORM_EPS)
            xr = xv * r
            which = 0 if h < ATT_QH else 1
            parts[which] = parts[which] + jnp.sum(dy * xr, axis=0, keepdims=True)
            dxr = dy * w
            dp_ref[:, cols] = r * (dxr - xr * jnp.mean(dxr * xr, axis=-1, keepdims=True))

        @pl.when(i == 0)
        def _():
            dqn_ref[...] = parts[0]
            dkn_ref[...] = parts[1]

        @pl.when(i > 0)
        def _():
            dqn_ref[...] += parts[0]
            dkn_ref[...] += parts[1]

    one = pl.BlockSpec((1, ATT_HD), lambda i: (0, 0))
    tab = pl.BlockSpec((tm, ATT_HD), lambda i: (i, 0))
    row = pl.BlockSpec((tm, W), lambda i: (i, 0))
    return pl.pallas_call(
        body, name=name, grid=(S // tm,),
        in_specs=[row, row, one, one, tab, tab],
        out_specs=[row, one, one],
        out_shape=[jax.ShapeDtypeStruct((S, W), F32), jax.ShapeDtypeStruct((1, ATT_HD), F32),
                   jax.ShapeDtypeStruct((1, ATT_HD), F32)],
        compiler_params=_cp(("arbitrary",)),
    )(dqk, proj, qn, kn, rc, rs)


ATT_TQ = 1024
LSE_ROWS = 8


def _attn_fwd(qk, vt, name):
    S = qk.shape[0]
    tq = min(ATT_TQ, S)

    def body(q_ref, k_ref, vt_ref, o_ref, lse_ref):
        st = _dg(k_ref[...], q_ref[...], 1, 1)
        m = jnp.max(st, axis=0, keepdims=True)
        pt = jnp.exp(st - m)
        l = jnp.sum(pt, axis=0, keepdims=True)
        ot = jnp.dot(vt_ref[...], pt.astype(BF16), preferred_element_type=F32)
        o_ref[...] = (ot * (1.0 / l)).T
        lse_ref[...] = jnp.broadcast_to(m + jnp.log(l), (LSE_ROWS, tq))

    qo = pl.BlockSpec((tq, ATT_HD), lambda h, i: (i, h))
    return pl.pallas_call(
        body, name=name, grid=(ATT_QH, S // tq),
        in_specs=[qo, pl.BlockSpec((S, ATT_HD), lambda h, i: (0, ATT_QH + h // ATT_GROUP)),
                  pl.BlockSpec((ATT_HD, S), lambda h, i: (h // ATT_GROUP, 0))],
        out_specs=[qo, pl.BlockSpec((LSE_ROWS, tq), lambda h, i: (h, i))],
        out_shape=[jax.ShapeDtypeStruct((S, ATT_QH * ATT_HD), F32),
                   jax.ShapeDtypeStruct((ATT_QH * LSE_ROWS, S), F32)],
        compiler_params=_cp(("parallel", "parallel")),
    )(qk, qk, vt)


def _attn_bwd(do, o, lse, qk, v, kt, name):
    S = qk.shape[0]
    tq = min(ATT_TQ, S)
    scale = ATT_HD ** -0.5

    def body(do_ref, o_ref, lse_ref, q_ref, k_ref, v_ref, kt_ref, dq_ref, dk_ref, dv_ref):
        g = pl.program_id(1)
        i = pl.program_id(2)

        @pl.when((g == 0) & (i == 0))
        def _():
            dk_ref[...] = jnp.zeros_like(dk_ref)
            dv_ref[...] = jnp.zeros_like(dv_ref)

        q = q_ref[...]
        dov = do_ref[...]
        dob = dov.astype(BF16)
        delta = jnp.sum((dov * o_ref[...]).T, axis=0, keepdims=True)
        st = _dg(k_ref[...], q, 1, 1)
        pt = jnp.exp(st - lse_ref[0:1, :])
        dpt = _dg(v_ref[...], dob, 1, 1)
        dst = (pt * (dpt - delta)).astype(BF16)
        dv_ref[...] += jnp.dot(pt.astype(BF16), dob, preferred_element_type=F32)
        dk_ref[...] += jnp.dot(dst, q, preferred_element_type=F32)
        dq_ref[...] = jnp.dot(kt_ref[...], dst, preferred_element_type=F32).T * scale

    qo = pl.BlockSpec((tq, ATT_HD), lambda kv, g, i: (i, kv * ATT_GROUP + g))
    kvo = pl.BlockSpec((S, ATT_HD), lambda kv, g, i: (0, kv))
    return pl.pallas_call(
        body, name=name, grid=(ATT_KVH, ATT_GROUP, S // tq),
        in_specs=[qo, qo, pl.BlockSpec((LSE_ROWS, tq), lambda kv, g, i: (kv * ATT_GROUP + g, i)), qo,
                  pl.BlockSpec((S, ATT_HD), lambda kv, g, i: (0, ATT_QH + kv)), kvo,
                  pl.BlockSpec((ATT_HD, S), lambda kv, g, i: (kv, 0))],
        out_specs=[qo, kvo, kvo],
        out_shape=[jax.ShapeDtypeStruct((S, ATT_QH * ATT_HD), F32),
                   jax.ShapeDtypeStruct((S, ATT_KVH * ATT_HD), F32),
                   jax.ShapeDtypeStruct((S, ATT_KVH * ATT_HD), F32)],
        compiler_params=_cp(("parallel", "arbitrary", "arbitrary")),
    )(do, o, lse, qk, qk, v, kt)


def _adamw(w, g, m, v, name):
    rows, cols = w.shape
    tr = rows
    for cand in (512, 256, 128, 64, 32, 16, 8):
        if rows % cand == 0 and cand * cols * 4 <= 2 * 1024 * 1024:
            tr = cand
            break

    def body(w_ref, g_ref, m_ref, v_ref, d_ref, nm_ref, nv_ref):
        gv = g_ref[...]
        nm = ADAM_B1 * m_ref[...] + (1.0 - ADAM_B1) * gv
        nv = ADAM_B2 * v_ref[...] + (1.0 - ADAM_B2) * (gv * gv)
        m_hat = nm / (1.0 - ADAM_B1 ** ADAM_STEP)
        v_hat = nv / (1.0 - ADAM_B2 ** ADAM_STEP)
        d_ref[...] = -ADAM_LR * (m_hat / (jnp.sqrt(v_hat) + ADAM_EPS) + ADAM_WD * w_ref[...])
        nm_ref[...] = nm
        nv_ref[...] = nv

    blk = pl.BlockSpec((tr, cols), lambda i: (i, 0))
    return pl.pallas_call(
        body, name=name, grid=(rows // tr,),
        in_specs=[blk] * 4, out_specs=[blk] * 3,
        out_shape=[jax.ShapeDtypeStruct((rows, cols), F32)] * 3,
        compiler_params=_cp(("parallel",)),
    )(w, g, m, v)


ANY = pl.BlockSpec(memory_space=pl.ANY)


def _place():
    return lax.axis_index("x"), lax.axis_index("y"), lax.axis_index("c")


def _other_chips(x, y):
    return [(1 - x, y), (x, 1 - y), (1 - x, 1 - y)]


def _half_rows(c, H):
    return pl.ds(pl.multiple_of(c * H, 8), H)


def _allreduce_small(v, name):
    R = v.shape[0]
    n_dev = 8

    def body(v_ref, sum_ref, all_ref, token_ref, send_sems, recv_sems, local_sem):
        token_ref[...] = jnp.zeros_like(token_ref)
        x, y, c = _place()
        me, sibling = (x, y, c), (x, y, 1 - c)
        chips = _other_chips(x, y)

        def rows(px, py, pc):
            return all_ref.at[pl.ds(pl.multiple_of((4 * px + 2 * py + pc) * R, 8), R), :]

        def copy(k, block, to, src=None):
            return pltpu.make_async_remote_copy(
                src_ref=rows(*block) if src is None else src, dst_ref=rows(*block),
                send_sem=send_sems.at[k], recv_sem=recv_sems.at[k], device_id=to, device_id_type=MESH)

        own = pltpu.make_async_copy(v_ref, rows(*me), local_sem)
        own.start()
        first = [copy(0, me, sibling, src=v_ref)]
        first += [copy(1 + j, me, (*chip, c), src=v_ref) for j, chip in enumerate(chips)]
        for cp in first:
            cp.start()
        passed = [copy(4 + j, (*chip, c), sibling) for j, chip in enumerate(chips)]
        for j, chip in enumerate(chips):
            copy(1 + j, (*chip, c), me).wait_recv()
            passed[j].start()
        copy(0, sibling, me).wait_recv()
        for j, chip in enumerate(chips):
            copy(4 + j, (*chip, 1 - c), me).wait_recv()
        for cp in first + passed:
            cp.wait_send()
        own.wait()
        acc = all_ref[pl.ds(0, R), :]
        for d in range(1, n_dev):
            acc = acc + all_ref[pl.ds(d * R, R), :]
        sum_ref[...] = acc

    vm = pl.BlockSpec(memory_space=pltpu.VMEM)
    total, _, token = pl.pallas_call(
        body, name=name,
        in_specs=[vm], out_specs=[vm, vm, vm],
        out_shape=[jax.ShapeDtypeStruct((R, LANES), F32), jax.ShapeDtypeStruct((n_dev * R, LANES), F32),
                   jax.ShapeDtypeStruct((8, LANES), F32)],
        scratch_shapes=[pltpu.SemaphoreType.DMA((7,)), pltpu.SemaphoreType.DMA((7,)), pltpu.SemaphoreType.DMA],
    )(v)
    return total, token


def _join_halves(bufs, name):
    n = len(bufs)
    halves = [b.shape[0] // 2 for b in bufs]

    def body(*refs):
        outs = refs[n:2 * n]
        send_sems, recv_sems = refs[2 * n:]
        x, y, c = _place()

        def copy(k, core):
            blk = outs[k].at[_half_rows(core, halves[k])]
            return pltpu.make_async_remote_copy(src_ref=blk, dst_ref=blk, send_sem=send_sems.at[k],
                                                recv_sem=recv_sems.at[k], device_id=(x, y, 1 - c),
                                                device_id_type=MESH)

        sends = [copy(k, c) for k in range(n)]
        for cp in sends:
            cp.start()
        for k in range(n):
            copy(k, 1 - c).wait_recv()
        for cp in sends:
            cp.wait_send()

    return pl.pallas_call(
        body, name=name, in_specs=[ANY] * n, out_specs=[ANY] * n,
        out_shape=[jax.ShapeDtypeStruct(b.shape, b.dtype) for b in bufs],
        input_output_aliases={k: k for k in range(n)},
        scratch_shapes=[pltpu.SemaphoreType.DMA((n,)), pltpu.SemaphoreType.DMA((n,))],
    )(*bufs)


def _rs_rows(H, width):
    for cand in range(H, 0, -16):
        if H % cand == 0 and cand % 16 == 0 and cand * width * 4 <= 1536 * 1024:
            return cand
    return H


def _add_sibling(g, got, c, me, name):
    _, H, width = got.shape
    tb = _rs_rows(H, width)
    nb = H // tb

    def body(sp_ref, g_ref, got_ref, sb_ref, sf_ref):
        p = pl.program_id(1)
        s = g_ref[0] + got_ref[0]
        sb_ref[0] = s.astype(BF16)

        @pl.when(p == sp_ref[1])
        def _():
            sf_ref[...] = s

    grid_spec = pltpu.PrefetchScalarGridSpec(
        num_scalar_prefetch=1, grid=(nb, N_CHIPS),
        in_specs=[pl.BlockSpec((1, tb, width), lambda i, p, sp: (p, sp[0] * nb + i, 0)),
                  pl.BlockSpec((1, tb, width), lambda i, p, sp: (p, i, 0))],
        out_specs=[pl.BlockSpec((1, tb, width), lambda i, p, sp: (p, i, 0)),
                   pl.BlockSpec((tb, width), lambda i, p, sp: (i, 0))])
    return pl.pallas_call(
        body, name=name, grid_spec=grid_spec,
        out_shape=[jax.ShapeDtypeStruct((N_CHIPS, H, width), BF16), jax.ShapeDtypeStruct((H, width), F32)],
        compiler_params=_cp(("arbitrary", "arbitrary")),
    )(jnp.stack([c, me]).astype(jnp.int32), g, got)


def _add_chips(sf, got, others_and_c, name):
    H, width = sf.shape
    tb = _rs_rows(H, width)
    nb = H // tb

    def body(sp_ref, sf_ref, r1_ref, r2_ref, r3_ref, out_ref):
        out_ref[...] = ((sf_ref[...] + r1_ref[0].astype(F32)) + r2_ref[0].astype(F32)) + r3_ref[0].astype(F32)

    def slot(k):
        return pl.BlockSpec((1, tb, width), lambda i, sp: (sp[k], i, 0))

    blk = pl.BlockSpec((tb, width), lambda i, sp: (i, 0))
    grid_spec = pltpu.PrefetchScalarGridSpec(
        num_scalar_prefetch=1, grid=(nb,), in_specs=[blk, slot(0), slot(1), slot(2)],
        out_specs=pl.BlockSpec((tb, width), lambda i, sp: (sp[3] * nb + i, 0)))
    return pl.pallas_call(
        body, name=name, grid_spec=grid_spec,
        out_shape=jax.ShapeDtypeStruct((2 * H, width), F32),
        compiler_params=_cp(("arbitrary",)),
    )(others_and_c.astype(jnp.int32), sf, got, got, got)


REPLICATED = ("norm_mix", "norm_ffn", "gla_b_gate_f", "gla_b_gate_b", "gla_norm", "attn_q_norm", "attn_k_norm",
              "ffn_b_conv")


PIECE_ROWS = 16


def _piece_rows(shape):
    n = 1
    for s in shape:
        n *= s
    rows = n // LANES
    return rows, -(-rows // PIECE_ROWS) * PIECE_ROWS


def _pack(pieces, dtype, row_multiple):
    flat = []
    for p in pieces:
        rows, padded = _piece_rows(p.shape)
        flat.append(jnp.pad(p.astype(dtype).reshape(rows, LANES), ((0, padded - rows), (0, 0))))
    rows = sum(f.shape[0] for f in flat)
    padded = -(-rows // row_multiple) * row_multiple
    if padded > rows:
        flat.append(jnp.zeros((padded - rows, LANES), dtype))
    return jnp.concatenate(flat, axis=0)


def _unpack(buf, shapes):
    out, r = [], 0
    for shp in shapes:
        rows, padded = _piece_rows(shp)
        out.append(buf[r:r + rows].reshape(shp))
        r += padded
    return out


def _own_slot(shard2d, me):
    return lax.dynamic_update_index_in_dim(lax.empty((N_CHIPS,) + shard2d.shape, shard2d.dtype), shard2d, me, 0)


def _layer_small(w, l):
    j = l // 2
    if l % 2 == 0:
        return [w["gla_w_gate_up_f"][j], w["gla_w_gate_up_b"][j], w["ffn_w_conv"][l]]
    return [w["ffn_w_conv"][l]]


def _layer_weight_bufs(w, l, me):
    j = l // 2
    mixer = ("gla_w_in", "gla_w_out") if l % 2 == 0 else ("attn_w_qkv", "attn_w_out")
    bufs = [_own_slot(w[n][j].astype(BF16), me) for n in mixer]
    bufs.append(_own_slot(_pack(_layer_small(w, l), F32, 32), me))
    bufs += [_own_slot(w["ffn_w_up"][l].astype(BF16), me), _own_slot(w["ffn_w_down"][l].astype(BF16), me)]
    return bufs


N_MIXER_BUFS = 3


def _layer_weights(w, l, got):
    rows = lambda t: t.reshape(-1, t.shape[2])
    cols = lambda t: jnp.concatenate([t[p] for p in range(N_CHIPS)], axis=1)
    out = {}
    if len(got) != N_MIXER_BUFS:
        up, down = got[-2:]
        out.update(up=up, up_full=cols(up), down=rows(down))
    if len(got) != 2:
        mix_in, mix_out, small = got[:N_MIXER_BUFS]
        shapes = [t.shape for t in _layer_small(w, l)]
        parts = [_unpack(small[p], shapes) for p in range(N_CHIPS)]
        full_small = [jnp.concatenate([parts[p][k] for p in range(N_CHIPS)], axis=-1) for k in range(len(shapes))]
        out.update(conv=full_small[-1])
        if l % 2 == 0:
            out.update(gla_in=jnp.pad(cols(mix_in), ((0, 0), (0, GLA_IN_PAD - GLA_IN))), gla_out=rows(mix_out),
                       gate_f=full_small[0], gate_b=full_small[1])
        else:
            out.update(qkv=mix_in, attn_out=rows(mix_out))
    return out


HBM = pl.BlockSpec(memory_space=pltpu.HBM)
SEM = pl.BlockSpec(memory_space=pltpu.SEMAPHORE)
SIDE_EFFECT = pltpu.SideEffectType.DATAFLOW_SIDE_EFFECTING


def _gather_start(bufs, after, name):
    n = len(bufs)
    halves = [b.shape[1] // 2 for b in bufs]

    def body(*refs):
        refs = refs[:n] + refs[n + 1:]
        send_sems, recv_sems = refs[n:2 * n], refs[2 * n:3 * n]
        outs, token = refs[3 * n:4 * n], refs[4 * n]
        x, y, c = _place()
        me = 2 * x + y
        for k in range(n):
            blk = outs[k].at[me, _half_rows(c, halves[k])]
            for px, py in _other_chips(x, y):
                pltpu.make_async_remote_copy(src_ref=blk, dst_ref=blk, send_sem=send_sems[k], recv_sem=recv_sems[k],
                                             device_id=(px, py, c), device_id_type=MESH).start()
        token[...] = jnp.zeros_like(token)

    res = pl.pallas_call(
        body, name=name,
        in_specs=[HBM] * n + [ANY],
        out_specs=[SEM] * (2 * n) + [HBM] * n + [pl.BlockSpec(memory_space=pltpu.VMEM)],
        out_shape=[pltpu.SemaphoreType.DMA(())] * (2 * n) + [pltpu.HBM(b.shape, b.dtype) for b in bufs]
        + [jax.ShapeDtypeStruct((8, LANES), F32)],
        input_output_aliases={k: 2 * n + k for k in range(n)},
        compiler_params=pltpu.CompilerParams(has_side_effects=SIDE_EFFECT),
    )(*[pltpu.with_memory_space_constraint(b, pltpu.HBM) for b in bufs], after)
    return res[:n], res[n:2 * n], res[2 * n:3 * n], res[3 * n]


def _gather_wait(send_sems, recv_sems, thru, after, name):
    n = len(thru)
    halves = [b.shape[1] // 2 for b in thru]

    def body(*refs):
        ss, rs = refs[n:2 * n], refs[2 * n:3 * n]
        outs = refs[3 * n + 1:]
        x, y, c = _place()
        for k in range(n):
            three = outs[k].at[pl.ds(0, N_CHIPS - 1), _half_rows(c, halves[k])]
            cp = pltpu.make_async_remote_copy(src_ref=three, dst_ref=three, send_sem=ss[k], recv_sem=rs[k],
                                              device_id=(x, y, c), device_id_type=MESH)
            cp.wait_send()
            cp.wait_recv()

    return pl.pallas_call(
        body, name=name,
        in_specs=[HBM] * n + [SEM] * (2 * n) + [ANY],
        out_specs=[HBM] * n,
        out_shape=[pltpu.HBM(b.shape, b.dtype) for b in thru],
        input_output_aliases={k: k for k in range(n)},
        compiler_params=pltpu.CompilerParams(has_side_effects=SIDE_EFFECT),
    )(*thru, *send_sems, *recv_sems, after)


def _swap_start(bufs, name):
    n = len(bufs)
    halves = [b.shape[1] // 2 for b in bufs]
    land_shapes = [(N_CHIPS, h, b.shape[2]) for b, h in zip(bufs, halves)]

    def body(*refs):
        send_sems, recv_sems = refs[2 * n:3 * n], refs[3 * n:4 * n]
        srcs, lands, token = refs[4 * n:5 * n], refs[5 * n:6 * n], refs[6 * n]
        x, y, c = _place()
        for k in range(n):
            for p in range(N_CHIPS):
                pltpu.make_async_remote_copy(src_ref=srcs[k].at[p, _half_rows(1 - c, halves[k])],
                                             dst_ref=lands[k].at[p], send_sem=send_sems[k], recv_sem=recv_sems[k],
                                             device_id=(x, y, 1 - c), device_id_type=MESH).start()
        token[...] = jnp.zeros_like(token)

    hbm = lambda a: pltpu.with_memory_space_constraint(a, pltpu.HBM)
    res = pl.pallas_call(
        body, name=name,
        in_specs=[HBM] * (2 * n),
        out_specs=[SEM] * (2 * n) + [HBM] * (2 * n) + [pl.BlockSpec(memory_space=pltpu.VMEM)],
        out_shape=[pltpu.SemaphoreType.DMA(())] * (2 * n) + [pltpu.HBM(b.shape, b.dtype) for b in bufs]
        + [pltpu.HBM(s, b.dtype) for s, b in zip(land_shapes, bufs)] + [jax.ShapeDtypeStruct((8, LANES), F32)],
        input_output_aliases={k: 2 * n + k for k in range(2 * n)},
        compiler_params=pltpu.CompilerParams(has_side_effects=SIDE_EFFECT),
    )(*[hbm(b) for b in bufs], *[hbm(lax.empty(s, b.dtype)) for s, b in zip(land_shapes, bufs)])
    return res[:n], res[n:2 * n], res[2 * n:3 * n], res[3 * n:4 * n], res[4 * n]


def _swap_wait(send_sems, recv_sems, bufs, lands, after, name):
    n = len(bufs)

    def body(*refs):
        ss, rs = refs[2 * n:3 * n], refs[3 * n:4 * n]
        l_out = refs[5 * n + 1:]
        x, y, c = _place()
        for k in range(n):
            cp = pltpu.make_async_remote_copy(src_ref=l_out[k], dst_ref=l_out[k], send_sem=ss[k], recv_sem=rs[k],
                                              device_id=(x, y, 1 - c), device_id_type=MESH)
            cp.wait_send()
            cp.wait_recv()

    res = pl.pallas_call(
        body, name=name,
        in_specs=[HBM] * (2 * n) + [SEM] * (2 * n) + [ANY],
        out_specs=[HBM] * (2 * n),
        out_shape=[pltpu.HBM(b.shape, b.dtype) for b in bufs] + [pltpu.HBM(l.shape, l.dtype) for l in lands],
        input_output_aliases={k: k for k in range(2 * n)},
        compiler_params=pltpu.CompilerParams(has_side_effects=SIDE_EFFECT),
    )(*bufs, *lands, *send_sems, *recv_sems, after)
    return res[:n], res[n:]


def _send_start(sbs, name):
    n = len(sbs)

    def body(*refs):
        send_sems, recv_sems = refs[2 * n:3 * n], refs[3 * n:4 * n]
        srcs, lands, token = refs[4 * n:5 * n], refs[5 * n:6 * n], refs[6 * n]
        x, y, c = _place()
        me = 2 * x + y
        for k in range(n):
            for px, py in _other_chips(x, y):
                pltpu.make_async_remote_copy(src_ref=srcs[k].at[2 * px + py], dst_ref=lands[k].at[me],
                                             send_sem=send_sems[k], recv_sem=recv_sems[k],
                                             device_id=(px, py, c), device_id_type=MESH).start()
        token[...] = jnp.zeros_like(token)

    hbm = lambda a: pltpu.with_memory_space_constraint(a, pltpu.HBM)
    res = pl.pallas_call(
        body, name=name,
        in_specs=[HBM] * (2 * n),
        out_specs=[SEM] * (2 * n) + [HBM] * (2 * n) + [pl.BlockSpec(memory_space=pltpu.VMEM)],
        out_shape=[pltpu.SemaphoreType.DMA(())] * (2 * n) + [pltpu.HBM(s.shape, s.dtype) for s in sbs] * 2
        + [jax.ShapeDtypeStruct((8, LANES), F32)],
        input_output_aliases={k: 2 * n + k for k in range(2 * n)},
        compiler_params=pltpu.CompilerParams(has_side_effects=SIDE_EFFECT),
    )(*[hbm(s) for s in sbs], *[hbm(lax.empty(s.shape, s.dtype)) for s in sbs])
    return res[:n], res[n:2 * n], res[2 * n:3 * n], res[3 * n:4 * n], res[4 * n]


def _send_wait(send_sems, recv_sems, srcs, lands, after, name):
    n = len(srcs)

    def body(*refs):
        ss, rs = refs[2 * n:3 * n], refs[3 * n:4 * n]
        s_out, l_out = refs[4 * n + 1:5 * n + 1], refs[5 * n + 1:]
        x, y, c = _place()
        for k in range(n):
            cp = pltpu.make_async_remote_copy(src_ref=s_out[k].at[pl.ds(0, N_CHIPS - 1)],
                                              dst_ref=l_out[k].at[pl.ds(0, N_CHIPS - 1)], send_sem=ss[k],
                                              recv_sem=rs[k], device_id=(x, y, c), device_id_type=MESH)
            cp.wait_send()
            cp.wait_recv()

    res = pl.pallas_call(
        body, name=name,
        in_specs=[HBM] * (2 * n) + [SEM] * (2 * n) + [ANY],
        out_specs=[HBM] * (2 * n),
        out_shape=[pltpu.HBM(s.shape, s.dtype) for s in srcs] * 2,
        input_output_aliases={k: k for k in range(2 * n)},
        compiler_params=pltpu.CompilerParams(has_side_effects=SIDE_EFFECT),
    )(*srcs, *lands, *send_sems, *recv_sems, after)
    return res[n:]


def _pass_to_sibling(bufs, name):
    n = len(bufs)
    halves = [b.shape[1] // 2 for b in bufs]

    def body(*refs):
        outs = refs[n:2 * n]
        send_sems, recv_sems = refs[2 * n:]
        x, y, c = _place()
        chips = _other_chips(x, y)

        def copy(k, j, core):
            px, py = chips[j]
            blk = outs[k].at[2 * px + py, _half_rows(core, halves[k])]
            return pltpu.make_async_remote_copy(src_ref=blk, dst_ref=blk, send_sem=send_sems.at[3 * k + j],
                                                recv_sem=recv_sems.at[3 * k + j], device_id=(x, y, 1 - c),
                                                device_id_type=MESH)

        sends = [copy(k, j, c) for k in range(n) for j in range(3)]
        for cp in sends:
            cp.start()
        for k in range(n):
            for j in range(3):
                copy(k, j, 1 - c).wait_recv()
        for cp in sends:
            cp.wait_send()

    return pl.pallas_call(
        body, name=name,
        in_specs=[ANY] * n, out_specs=[ANY] * n,
        out_shape=[jax.ShapeDtypeStruct(b.shape, b.dtype) for b in bufs],
        input_output_aliases={k: k for k in range(n)},
        scratch_shapes=[pltpu.SemaphoreType.DMA((3 * n,)), pltpu.SemaphoreType.DMA((3 * n,))],
    )(*bufs)


def _rope_tables(S):
    rows = S // GRID_W
    row_idx = jnp.repeat(jnp.arange(rows, dtype=F32), GRID_W)
    col_idx = jnp.tile(jnp.arange(GRID_W, dtype=F32), rows)
    pairs = ATT_HD // 4
    inv_freq = ROPE_THETA ** (-jnp.arange(pairs, dtype=F32) / pairs)
    ang = jnp.concatenate([row_idx[:, None] * inv_freq, col_idx[:, None] * inv_freq], axis=-1)
    cos, sin = jnp.cos(ang), jnp.sin(ang)
    return jnp.concatenate([cos, cos], axis=-1), jnp.concatenate([-sin, sin], axis=-1)


def _gate_rows(w, first_row):
    return jnp.zeros((LANES, GLA_KEY), F32).at[first_row:first_row + GLA_RANK].set(w.astype(F32))


def _local_step(x, target, weights_of, grads_out, grads_mid, P):
    S = x.shape[0]
    rc, rs = _rope_tables(S)
    row = lambda a: a.reshape(1, -1)
    saved = []
    for i in range(DEPTH):
        j = i // 2
        W = dict(weights_of(i, "mix", x))
        nm = row(P["norm_mix"][i])
        h1 = _rmsnorm_fwd(x, nm, f"norm_mix_fwd{i}")
        if i % 2 == 0:
            wgf = _gate_rows(W["gate_f"], 0)
            wgb = _gate_rows(W["gate_b"], GLA_RANK)
            bgf, bgb = row(P["gla_b_gate_f"][j]), row(P["gla_b_gate_b"][j])
            gn = row(P["gla_norm"][j])
            proj = _matmul_rows(h1, W["gla_in"], f"gla_in{i}")
            laf, lab = _gla_gate_fwd(proj, wgf, bgf, wgb, bgb, f"gla_gate_fwd{i}")
            of, stf = _gla_scan_fwd(proj, laf, False, f"gla_scan_f_fwd{i}")
            ob, stb = _gla_scan_fwd(proj, lab, True, f"gla_scan_b_fwd{i}")
            z = _gla_out_fwd(of, ob, proj, gn, f"gla_out_fwd{i}")
            xm = _matmul_rows(z, W["gla_out"], f"gla_outproj{i}", res=x)
            mix = dict(proj=proj, laf=laf, lab=lab, of=of, ob=ob, stf=stf, stb=stb, z=z, wgf=wgf, wgb=wgb)
        else:
            proj = _matmul_rows(h1, W["qkv"], f"attn_qkv{i}", w_layer=0)
            qn, kn = row(P["attn_q_norm"][j]), row(P["attn_k_norm"][j])
            qk, vb, kt, vt = _qk_prep_fwd(proj, qn, kn, rc, rs, f"qk_prep_fwd{i}")
            o, lse = _attn_fwd(qk, vt, f"attn_fwd{i}")
            xm = _matmul_rows(o, W["attn_out"], f"attn_outproj{i}", res=x)
            mix = dict(proj=proj, qk=qk, vb=vb, kt=kt, o=o, lse=lse)
        W.update(weights_of(i, "ffn", xm))
        h2 = _rmsnorm_fwd(xm, row(P["norm_ffn"][i]), f"norm_ffn_fwd{i}")
        a, uv, ug = _ffn_mid_fwd(h2, W["up_full"], W["conv"], row(P["ffn_b_conv"][i]), f"ffn_mid_fwd{i}")
        xo = _matmul_rows(a, W["down"], f"ffn_down{i}", res=xm)
        saved.append(dict(x=x, h1=h1, xm=xm, h2=h2, uv=uv, ug=ug, mix=mix, W=W))
        x = xo

    dx, dxb, loss = _loss_grad(x, target, "loss")

    G = {n: [None] * (DEPTH if n.startswith(("norm", "ffn")) else DEPTH // 2) for n in REPLICATED}
    token = None
    for i in reversed(range(DEPTH)):
        j = i // 2
        sv = saved[i]
        mix = sv["mix"]
        W = sv["W"]
        bconv = row(P["ffn_b_conv"][i])
        if token is not None:
            t = token[0:1, 0:1]
            bconv = jnp.where(t == 0.0, bconv, t)
        duv, dug, a, gwv, gwg = _ffn_mid_bwd(dxb, W["down"], sv["uv"], sv["ug"], W["conv"], bconv, f"ffn_mid_bwd{i}")
        L = dict(down=_wgrad(a, dxb, f"ffn_down_wgrad{i}", chips="rows"),
                 up=_wgrad(sv["h2"], (duv, dug), f"ffn_up_wgrad{i}", chips="cols"),
                 small=[jnp.concatenate([gwv[:3], gwg[:3]], axis=1)])
        G["ffn_b_conv"][i] = jnp.concatenate([gwv[3], gwg[3]], axis=0)
        nffn = row(P["norm_ffn"][i])
        if token is not None:
            t = grads_mid(i + 1, duv)[0:1, 0:1]
            nffn = jnp.where(t == 0.0, nffn, t)
        dxm, dxmb, dn = _dgrad_norm((duv, dug), W["up"], sv["xm"], nffn, dx, f"ffn_up_dgrad{i}", w_layer=0)
        G["norm_ffn"][i] = dn[0]
        if i % 2 == 0:
            proj = mix["proj"]
            bgf, bgb = row(P["gla_b_gate_f"][j]), row(P["gla_b_gate_b"][j])
            gn = row(P["gla_norm"][j])
            dz = _matmul_rows(dxmb, W["gla_out"], f"gla_outproj_dgrad{i}", transposed=True)
            L["out"] = _wgrad(mix["z"], dxmb, f"gla_outproj_wgrad{i}", chips="rows")
            do, dg, dgn = _gla_out_bwd(dz, mix["of"], mix["ob"], proj, gn, f"gla_out_bwd{i}")
            G["gla_norm"][j] = dgn[0]
            dqf, dkf, dvf, dlaf = _gla_scan_bwd(do, proj, mix["laf"], mix["stf"], False, f"gla_scan_f_bwd{i}")
            dqb, dkb, dvb, dlab = _gla_scan_bwd(do, proj, mix["lab"], mix["stb"], True, f"gla_scan_b_bwd{i}")
            dr, dwf, dbf, dwb, dbb = _gla_gate_bwd(dlaf, dlab, proj, mix["wgf"], bgf, mix["wgb"], bgb,
                                                   f"gla_gate_bwd{i}")
            L["small"] = [dwf[:GLA_RANK], dwb[GLA_RANK:2 * GLA_RANK]] + L["small"]
            G["gla_b_gate_f"][j] = dbf[0]
            G["gla_b_gate_b"][j] = dbb[0]
            dproj = jnp.concatenate([dqf + dqb, dkf + dkb, dvf + dvb, dg, dr], axis=1).astype(BF16)
            L["mix_in"] = _wgrad(sv["h1"], dproj, f"gla_in_wgrad{i}")
            dx, dxb, dn = _dgrad_norm(dproj, W["gla_in"], sv["x"], row(P["norm_mix"][i]), dxm, f"mix_in_dgrad{i}")
        else:
            proj = mix["proj"]
            qn, kn = row(P["attn_q_norm"][j]), row(P["attn_k_norm"][j])
            do = _matmul_rows(dxmb, W["attn_out"], f"attn_outproj_dgrad{i}", transposed=True)
            L["out"] = _wgrad(mix["o"], dxmb, f"attn_outproj_wgrad{i}", chips="rows")
            dq, dk, dv = _attn_bwd(do, mix["o"], mix["lse"], mix["qk"], mix["vb"], mix["kt"], f"attn_bwd{i}")
            dqk = jnp.concatenate([dq, dk], axis=1)
            dpqk, dqn, dkn = _qk_prep_bwd(dqk, proj, qn, kn, rc, rs, f"qk_prep_bwd{i}")
            G["attn_q_norm"][j] = dqn[0]
            G["attn_k_norm"][j] = dkn[0]
            dproj = jnp.concatenate([dpqk, dv], axis=1).astype(BF16)
            L["mix_in"] = _wgrad(sv["h1"], dproj, f"attn_qkv_wgrad{i}", chips="cols")
            dx, dxb, dn = _dgrad_norm(dproj, W["qkv"], sv["x"], row(P["norm_mix"][i]), dxm, f"mix_in_dgrad{i}",
                                      w_layer=0)
        G["norm_mix"][i] = dn[0]
        token = grads_out(i, L, G, loss)
    grads_mid(0, dx)
    return loss, dx, G


def kernel(x, norm_mix, norm_ffn, gla_w_in, gla_w_gate_up_f, gla_b_gate_f, gla_w_gate_up_b, gla_b_gate_b, gla_norm, gla_w_out, attn_w_qkv, attn_q_norm, attn_k_norm, attn_w_out, ffn_w_up, ffn_w_conv, ffn_b_conv, ffn_w_down, loss_target, m_norm_mix, m_norm_ffn, m_gla_w_in, m_gla_w_gate_up_f, m_gla_b_gate_f, m_gla_w_gate_up_b, m_gla_b_gate_b, m_gla_norm, m_gla_w_out, m_attn_w_qkv, m_attn_q_norm, m_attn_k_norm, m_attn_w_out, m_ffn_w_up, m_ffn_w_conv, m_ffn_b_conv, m_ffn_w_down, v_norm_mix, v_norm_ffn, v_gla_w_in, v_gla_w_gate_up_f, v_gla_b_gate_f, v_gla_w_gate_up_b, v_gla_b_gate_b, v_gla_norm, v_gla_w_out, v_attn_w_qkv, v_attn_q_norm, v_attn_k_norm, v_attn_w_out, v_ffn_w_up, v_ffn_w_conv, v_ffn_b_conv, v_ffn_w_down):
    names = ("norm_mix", "norm_ffn", "gla_w_in", "gla_w_gate_up_f", "gla_b_gate_f", "gla_w_gate_up_b",
             "gla_b_gate_b", "gla_norm", "gla_w_out", "attn_w_qkv", "attn_q_norm", "attn_k_norm", "attn_w_out",
             "ffn_w_up", "ffn_w_conv", "ffn_b_conv", "ffn_w_down")
    w = dict(zip(names, (norm_mix, norm_ffn, gla_w_in, gla_w_gate_up_f, gla_b_gate_f, gla_w_gate_up_b,
                         gla_b_gate_b, gla_norm, gla_w_out, attn_w_qkv, attn_q_norm, attn_k_norm, attn_w_out,
                         ffn_w_up, ffn_w_conv, ffn_b_conv, ffn_w_down)))
    m = dict(zip(names, (m_norm_mix, m_norm_ffn, m_gla_w_in, m_gla_w_gate_up_f, m_gla_b_gate_f,
                         m_gla_w_gate_up_b, m_gla_b_gate_b, m_gla_norm, m_gla_w_out, m_attn_w_qkv, m_attn_q_norm,
                         m_attn_k_norm, m_attn_w_out, m_ffn_w_up, m_ffn_w_conv, m_ffn_b_conv, m_ffn_w_down)))
    v = dict(zip(names, (v_norm_mix, v_norm_ffn, v_gla_w_in, v_gla_w_gate_up_f, v_gla_b_gate_f,
                         v_gla_w_gate_up_b, v_gla_b_gate_b, v_gla_norm, v_gla_w_out, v_attn_w_qkv, v_attn_q_norm,
                         v_attn_k_norm, v_attn_w_out, v_ffn_w_up, v_ffn_w_conv, v_ffn_b_conv, v_ffn_w_down)))
    px, py, pc = _place()
    me = 2 * px + py

    started, token = [], w["norm_mix"]
    for l in range(DEPTH):
        started.append(_gather_start(_layer_weight_bufs(w, l, me), token, f"gather_start{l}"))
        token = started[-1][3]
    fetched = {}

    def weights_of(l, part, after):
        send_sems, recv_sems, thru, _ = started[l]
        if l == 0:
            pick = slice(0, N_MIXER_BUFS) if part == "mix" else slice(N_MIXER_BUFS, None)
            landed = _gather_wait(send_sems[pick], recv_sems[pick], thru[pick], token if part == "mix" else after,
                                  f"gather_wait{l}_{part}")
            return _layer_weights(w, l, _pass_to_sibling(landed, f"gather_pass{l}_{part}"))
        if part == "mix":
            landed = _gather_wait(send_sems, recv_sems, thru, after, f"gather_wait{l}")
            fetched[l] = _layer_weights(w, l, _pass_to_sibling(landed, f"gather_pass{l}"))
        return fetched[l]

    sent = {}

    def grads_out(l, L, G, loss_part):
        mix_in = L["mix_in"]
        if l % 2 == 0:
            width = w["gla_w_in"].shape[2]
            mix_in = jnp.stack([mix_in[:, p * width:(p + 1) * width] for p in range(N_CHIPS)])
        cut = lambda t, p: lax.slice_in_dim(t, p * (t.shape[-1] // N_CHIPS), (p + 1) * (t.shape[-1] // N_CHIPS),
                                            axis=t.ndim - 1)
        small = jnp.stack([_pack([cut(t, p) for t in L["small"]], F32, 32) for p in range(N_CHIPS)])
        if l == 0:
            packed = _pack([jnp.stack(G[n]) for n in REPLICATED] + [loss_part], F32, 16)
            sent["small_sum"], tok = _allreduce_small(packed, "small_allreduce")
            small = jnp.where(tok[0:1, 0:1] == 0.0, small, tok[0:1, 0:1])
        bufs = [mix_in, L["out"], small, L["up"], L["down"]]
        *swap, tok = _swap_start(bufs, f"grads{l}_to_sibling")
        sent[l] = swap
        return tok

    def grads_mid(l, after):
        bufs, gots = _swap_wait(*sent[l], after, f"grads{l}_from_sibling")
        sums = [_add_sibling(b, g, pc, me, f"grads{l}_add_sibling{k}") for k, (b, g) in enumerate(zip(bufs, gots))]
        send_sems, recv_sems, srcs, lands, tok = _send_start([s[0] for s in sums], f"grads{l}_start")
        sent[l] = (send_sems, recv_sems, srcs, lands, [s[1] for s in sums])
        sent["last_token"] = tok
        return tok

    P = {n: w[n] for n in REPLICATED}

    loss_part, dx, grads = _local_step(x[0], loss_target[0], weights_of, grads_out, grads_mid, P)

    small_sum = sent["small_sum"]
    others_and_c = jnp.stack([jnp.where(me <= k, k + 1, k) for k in range(N_CHIPS - 1)] + [pc])
    mine, after = {}, sent["last_token"]
    for l in reversed(range(DEPTH)):
        send_sems, recv_sems, srcs, lands, own = sent[l]
        landed = _send_wait(send_sems, recv_sems, srcs, lands, after, f"grads{l}_wait")
        halves = [_add_chips(own[k], landed[k], others_and_c, f"grads{l}_add_chips{k}") for k in range(len(own))]
        mine[l] = _join_halves(halves, f"grads{l}_join_halves")
        after = mine[l][0]
    gsh = {}
    for n, k, layers in (("ffn_w_up", 3, range(DEPTH)), ("ffn_w_down", 4, range(DEPTH)),
                         ("gla_w_in", 0, range(0, DEPTH, 2)), ("gla_w_out", 1, range(0, DEPTH, 2)),
                         ("attn_w_qkv", 0, range(1, DEPTH, 2)), ("attn_w_out", 1, range(1, DEPTH, 2))):
        gsh[n] = jnp.stack([mine[l][k] for l in layers])
    small_mine = [_unpack(mine[l][2], [t.shape for t in _layer_small(w, l)]) for l in range(DEPTH)]
    gsh["ffn_w_conv"] = jnp.stack([small_mine[l][-1] for l in range(DEPTH)])
    gsh["gla_w_gate_up_f"] = jnp.stack([small_mine[l][0] for l in range(0, DEPTH, 2)])
    gsh["gla_w_gate_up_b"] = jnp.stack([small_mine[l][1] for l in range(0, DEPTH, 2)])

    parts = _unpack(small_sum, [w[n].shape for n in REPLICATED] + [(1, LANES)])
    gsh.update(dict(zip(REPLICATED, parts[:-1])))
    loss = parts[-1][0, 0]

    delta, new_m, new_v = {}, {}, {}
    for n in names:
        shp = w[n].shape
        two_d = (-1, shp[-1])
        d, nm, nv = _adamw(w[n].reshape(two_d), gsh[n].reshape(two_d), m[n].reshape(two_d), v[n].reshape(two_d),
                           f"adamw_{n}")
        delta[n], new_m[n], new_v[n] = d.reshape(shp), nm.reshape(shp), nv.reshape(shp)

    return (loss, dx[None], *[gsh[n] for n in names], *[delta[n] for n in names],
            *[new_m[n] for n in names], *[new_v[n] for n in names])
```
